```python
import jax, jax.numpy as jnp
from jax import lax
import numpy as np

D_MODEL = 1024
BATCH = 32
SEQ = 2048
DEPTH = 2

F32 = jnp.float32
N_MEM = 256
HEAD_DIM = 64
ROPE_THETA = 10000.0
EPS = 1e-6
NEG_INF = -1e30
POS_INF = 1e30
Q_BLOCK = 128

A_HEADS = 8
A_KV_HEADS = 2
A_WINDOW = 128
A_WIDTH = A_HEADS * HEAD_DIM
A_KV_WIDTH = A_KV_HEADS * HEAD_DIM

B_WIDTH = 512
B_BLOCKS = 8
B_BLOCK_DIM = B_WIDTH // B_BLOCKS
B_CONV = 4
B_C = 8.0

M_HEADS = 4
M_WIDTH = M_HEADS * HEAD_DIM

C_HEADS = 4
C_HEAD_DIM = 128
C_WIDTH = C_HEADS * C_HEAD_DIM
C_CHUNK = 64

D_HEADS = 8
D_KV_HEADS = 2
D_WIDTH = D_HEADS * HEAD_DIM
D_KV_WIDTH = D_KV_HEADS * HEAD_DIM
CMP_LEN = 32
CMP_STRIDE = 16
CMP_HIDDEN = 128
SEL_LEN = 64
SEL_TOPK = 4
SEL_Q_BLOCK = 64
D_WINDOW = 512
D_BRANCHES = 3

EVEN_SPLITS = [A_WIDTH, A_KV_WIDTH, A_KV_WIDTH, A_WIDTH, B_WIDTH, B_WIDTH, M_WIDTH, M_WIDTH]
ODD_SPLITS = [C_WIDTH, C_WIDTH, C_WIDTH, C_WIDTH, D_WIDTH] + [D_KV_WIDTH] * 6 + [D_BRANCHES * D_HEADS, D_WIDTH, M_WIDTH, M_WIDTH]
EVEN_IN = sum(EVEN_SPLITS)
ODD_IN = sum(ODD_SPLITS)
EVEN_MIX = A_WIDTH + B_WIDTH + M_WIDTH
ODD_MIX = C_WIDTH + D_WIDTH + M_WIDTH
N_EVEN = (DEPTH + 1) // 2
N_ODD = DEPTH // 2

kernel_name = "hybrid_swa_rglru_hgrn2_nsa_trunk"


def rms_norm(x, g):
    xf = x.astype(F32)
    y = xf * lax.rsqrt(jnp.mean(xf * xf, axis=-1, keepdims=True) + EPS)
    return (y * g.astype(F32)).astype(x.dtype)


def rope(x, pos):
    half = x.shape[-1] // 2
    inv = ROPE_THETA ** (-jnp.arange(half, dtype=F32) / half)
    ang = pos.astype(F32)[:, None] * inv[None, :]
    cos = jnp.cos(ang)[:, None, :]
    sin = jnp.sin(ang)[:, None, :]
    x1 = x[..., :half].astype(F32)
    x2 = x[..., half:].astype(F32)
    return jnp.concatenate([x1 * cos - x2 * sin, x2 * cos + x1 * sin], axis=-1).astype(x.dtype)


def split_cols(z, widths):
    return jnp.split(z, [int(c) for c in np.cumsum(widths)[:-1]], axis=-1)


def banded_attention(q, k, v, window, sinks=None):
    B_, S_, Hkv, G, hd = q.shape
    pad = -(-(window - 1) // Q_BLOCK) * Q_BLOCK
    span = pad + Q_BLOCK
    kp = jnp.pad(k, ((0, 0), (pad, 0), (0, 0), (0, 0)))
    vp = jnp.pad(v, ((0, 0), (pad, 0), (0, 0), (0, 0)))
    scale = hd ** -0.5

    def one_block(j):
        start = j * Q_BLOCK
        qb = lax.dynamic_slice_in_dim(q, start, Q_BLOCK, axis=1)
        kb = lax.dynamic_slice_in_dim(kp, start, span, axis=1)
        vb = lax.dynamic_slice_in_dim(vp, start, span, axis=1)
        s = jnp.einsum('bqkgd,bskd->bkgqs', qb, kb).astype(F32) * scale
        tq = start + jnp.arange(Q_BLOCK)
        ts = start - pad + jnp.arange(span)
        rel = tq[:, None] - ts[None, :]
        mask = (rel >= 0) & (rel < window) & (ts[None, :] >= 0)
        s = jnp.where(mask, s, NEG_INF)
        if sinks is None:
            p = jax.nn.softmax(s, axis=-1)
        else:
            sink = sinks.astype(F32).reshape(Hkv, G)[None, :, :, None, None]
            m = jnp.maximum(jnp.max(s, axis=-1, keepdims=True), sink)
            e = jnp.exp(s - m)
            p = e / (jnp.sum(e, axis=-1, keepdims=True) + jnp.exp(sink - m))
        return jnp.einsum('bkgqs,bskd->bqkgd', p.astype(vb.dtype), vb)

    out = lax.map(one_block, jnp.arange(S_ // Q_BLOCK))
    return jnp.moveaxis(out, 0, 1).reshape(B_, S_, Hkv, G, hd)


def swa_sink_attention(q, k, v, qn, kn, sinks, pos):
    B_, S_, _ = q.shape
    G = A_HEADS // A_KV_HEADS
    q = rope(rms_norm(q.reshape(B_, S_, A_HEADS, HEAD_DIM), qn), pos).reshape(B_, S_, A_KV_HEADS, G, HEAD_DIM)
    k = rope(rms_norm(k.reshape(B_, S_, A_KV_HEADS, HEAD_DIM), kn), pos)
    v = v.reshape(B_, S_, A_KV_HEADS, HEAD_DIM)
    o = banded_attention(q, k, v, A_WINDOW, sinks)
    return o.reshape(B_, S_, A_WIDTH)


def rglru(xb, conv_w, conv_b, w_r, b_r, w_i, b_i, lam):
    B_, S_, W = xb.shape
    xc = lax.conv_general_dilated(xb, conv_w[:, None, :], window_strides=(1,), padding=[(B_CONV - 1, 0)],
                                  dimension_numbers=('NWC', 'WIO', 'NWC'), feature_group_count=W) + conv_b
    xh = xc.reshape(B_, S_, B_BLOCKS, B_BLOCK_DIM)
    r = jax.nn.sigmoid(jnp.einsum('bshi,hij->bshj', xh, w_r).reshape(B_, S_, W) + b_r).astype(F32)
    i = jax.nn.sigmoid(jnp.einsum('bshi,hij->bshj', xh, w_i).reshape(B_, S_, W) + b_i)
    log_a = -B_C * r * jax.nn.softplus(-lam.astype(F32))
    a = jnp.exp(log_a)
    u = jnp.sqrt(-jnp.expm1(2.0 * log_a)) * (i * xc).astype(F32)

    def combine(c1, c2):
        a1, b1 = c1
        a2, b2 = c2
        return a1 * a2, a2 * b1 + b2

    _, h = lax.associative_scan(combine, (a, u), axis=1)
    return h.astype(xb.dtype)


def memory_cross_attention(qm, mem_n, w_mem_kv, qn, kn):
    B_, S_, _ = qm.shape
    N_ = mem_n.shape[1]
    q = rms_norm(qm.reshape(B_, S_, M_HEADS, HEAD_DIM), qn)
    km, vm = jnp.split(jnp.einsum('bnd,de->bne', mem_n, w_mem_kv), 2, axis=-1)
    km = rms_norm(km.reshape(B_, N_, M_HEADS, HEAD_DIM), kn)
    vm = vm.reshape(B_, N_, M_HEADS, HEAD_DIM)
    s = jnp.einsum('bshd,bnhd->bhsn', q, km).astype(F32) * HEAD_DIM ** -0.5
    p = jax.nn.softmax(s, axis=-1)
    o = jnp.einsum('bhsn,bnhd->bshd', p.astype(vm.dtype), vm)
    return o.reshape(B_, S_, M_WIDTH)


def hgrn_lower_bounds(p):
    c = jnp.cumsum(jax.nn.softmax(p.astype(F32), axis=0), axis=0)
    return c - c[0:1]


def hgrn2(q, f_logit, i_in, lb, o_gain):
    B_, S_, _ = q.shape
    nc = S_ // C_CHUNK
    f = lb + (1.0 - lb) * jax.nn.sigmoid(f_logit.astype(F32))
    log_f = jnp.log(f)
    k = 1.0 - f
    qf = jax.nn.silu(q.astype(F32))
    v = i_in.astype(F32)

    def to_chunks(t):
        return t.reshape(B_, nc, C_CHUNK, C_HEADS, C_HEAD_DIM).transpose(1, 0, 3, 2, 4)

    causal = jnp.tril(jnp.ones((C_CHUNK, C_CHUNK), dtype=bool))

    def step(state, inp):
        qc, kc, vc, gc = inp
        b = jnp.cumsum(gc, axis=2)
        o_inter = jnp.einsum('bhtk,bhkv->bhtv', qc * jnp.exp(b), state)
        diff = b[:, :, :, None, :] - b[:, :, None, :, :]
        decay = jnp.exp(jnp.where(causal[:, :, None], diff, NEG_INF))
        attn = jnp.einsum('bhtk,bhsk,bhtsk->bhts', qc, kc, decay)
        o_intra = jnp.einsum('bhts,bhsv->bhtv', attn, vc)
        b_last = b[:, :, -1:, :]
        state = jnp.exp(b_last[:, :, 0, :])[..., None] * state + jnp.einsum('bhsk,bhsv->bhkv', kc * jnp.exp(b_last - b), vc)
        return state, o_inter + o_intra

    init = jnp.zeros((B_, C_HEADS, C_HEAD_DIM, C_HEAD_DIM), F32)
    _, o = lax.scan(step, init, (to_chunks(qf), to_chunks(k), to_chunks(v), to_chunks(log_f)))
    o = o.transpose(1, 0, 3, 2, 4).reshape(B_, S_, C_HEADS, C_HEAD_DIM)
    o = rms_norm(o, o_gain)
    return o.reshape(B_, S_, C_WIDTH).astype(q.dtype)


def nsa(q, k_cmp, v_cmp, k_slc, v_slc, k_win, v_win, gate_logits, qn, kn_cmp, kn_slc, kn_win,
        pe_k, pe_v, w1k, w2k, w1v, w2v, pos):
    B_, S_, _ = q.shape
    Hkv = D_KV_HEADS
    G = D_HEADS // D_KV_HEADS
    scale = HEAD_DIM ** -0.5
    q = rope(rms_norm(q.reshape(B_, S_, D_HEADS, HEAD_DIM), qn), pos).reshape(B_, S_, Hkv, G, HEAD_DIM)
    t_pos = jnp.arange(S_)

    n_cmp = (S_ - CMP_LEN) // CMP_STRIDE + 1
    cmp_start = jnp.arange(n_cmp) * CMP_STRIDE
    blk_idx = cmp_start[:, None] + jnp.arange(CMP_LEN)[None, :]

    def compress(t, pe, w1, w2):
        tb = t.reshape(B_, S_, Hkv, HEAD_DIM)[:, blk_idx] + pe[None, None, :, None, :]
        flat = tb.transpose(0, 1, 3, 2, 4).reshape(B_, n_cmp, Hkv, CMP_LEN * HEAD_DIM)
        return jax.nn.silu(flat @ w1) @ w2

    cmp_end = cmp_start + CMP_LEN - 1
    kc = rope(rms_norm(compress(k_cmp, pe_k, w1k, w2k), kn_cmp), cmp_end)
    vc = compress(v_cmp, pe_v, w1v, w2v)
    s_c = jnp.einsum('bskgd,bnkd->bkgsn', q, kc).astype(F32) * scale
    mask_c = cmp_end[None, :] <= t_pos[:, None]
    p_c = jax.nn.softmax(jnp.where(mask_c, s_c, NEG_INF), axis=-1)
    p_c = jnp.where(jnp.any(mask_c, axis=-1)[:, None], p_c, 0.0)
    o_cmp = jnp.einsum('bkgsn,bnkd->bskgd', p_c.astype(vc.dtype), vc)

    n_sel = S_ // SEL_LEN
    sel_start = jnp.arange(n_sel) * SEL_LEN
    overlap = ((cmp_start[:, None] < sel_start[None, :] + SEL_LEN) &
               (cmp_start[:, None] + CMP_LEN > sel_start[None, :])).astype(F32)
    imp = jnp.einsum('bkgsn,nj->bksj', p_c, overlap)
    cur = t_pos // SEL_LEN
    jj = jnp.arange(n_sel)
    forced = (jj[None, :] == 0) | (jj[None, :] == cur[:, None])
    valid = jj[None, :] <= cur[:, None]
    score = jnp.where(forced, POS_INF, jnp.where(valid, imp, NEG_INF))
    k_top = min(SEL_TOPK, n_sel)
    _, sel_idx = lax.top_k(score, k_top)
    ks = rope(rms_norm(k_slc.reshape(B_, S_, Hkv, HEAD_DIM), kn_slc), pos)
    ksb = ks.reshape(B_, n_sel, SEL_LEN, Hkv, HEAD_DIM).transpose(0, 3, 1, 2, 4)
    vsb = v_slc.reshape(B_, n_sel, SEL_LEN, Hkv, HEAD_DIM).transpose(0, 3, 1, 2, 4)
    bi = jnp.arange(B_)[:, None, None, None]
    hi = jnp.arange(Hkv)[None, :, None, None]

    def sel_block(j):
        start = j * SEL_Q_BLOCK
        qb = lax.dynamic_slice_in_dim(q, start, SEL_Q_BLOCK, axis=1)
        idx = lax.dynamic_slice_in_dim(sel_idx, start, SEL_Q_BLOCK, axis=2)
        kg = ksb[bi, hi, idx]
        vg = vsb[bi, hi, idx]
        s = jnp.einsum('bqkgd,bkqnld->bkgqnl', qb, kg).astype(F32) * scale
        tq = start + jnp.arange(SEL_Q_BLOCK)
        key_pos = idx[..., None] * SEL_LEN + jnp.arange(SEL_LEN)
        mask = (key_pos <= tq[None, None, :, None, None])[:, :, None]
        s = jnp.where(mask, s, NEG_INF)
        p = jax.nn.softmax(s.reshape(B_, Hkv, G, SEL_Q_BLOCK, k_top * SEL_LEN), axis=-1).reshape(s.shape)
        return jnp.einsum('bkgqnl,bkqnld->bqkgd', p.astype(vg.dtype), vg)

    o_sel = lax.map(sel_block, jnp.arange(S_ // SEL_Q_BLOCK))
    o_sel = jnp.moveaxis(o_sel, 0, 1).reshape(B_, S_, Hkv, G, HEAD_DIM)

    kw = rope(rms_norm(k_win.reshape(B_, S_, Hkv, HEAD_DIM), kn_win), pos)
    o_win = banded_attention(q, kw, v_win.reshape(B_, S_, Hkv, HEAD_DIM), D_WINDOW)

    g = jax.nn.sigmoid(gate_logits).reshape(B_, S_, Hkv, G, D_BRANCHES, 1)
    o = g[..., 0, :] * o_cmp + g[..., 1, :] * o_sel + g[..., 2, :] * o_win
    return o.reshape(B_, S_, D_WIDTH)


def even_layer(h, mem, g, mem_g, w_mem_kv, m_qn, m_kn, w_in, w_out, a_qn, a_kn, a_sinks,
               conv_w, conv_b, w_r, b_r, w_i, b_i, lam, pos):
    xn = rms_norm(h, g)
    z = jnp.einsum('bsd,de->bse', xn, w_in)
    qa, ka, va, ga, xb, gb, qm, gm = split_cols(z, EVEN_SPLITS)
    oa = swa_sink_attention(qa, ka, va, a_qn, a_kn, a_sinks, pos) * jax.nn.silu(ga)
    ob = rglru(xb, conv_w, conv_b, w_r, b_r, w_i, b_i, lam) * jax.nn.silu(gb)
    om = memory_cross_attention(qm, rms_norm(mem, mem_g), w_mem_kv, m_qn, m_kn) * jax.nn.silu(gm)
    return h + jnp.einsum('bse,ed->bsd', jnp.concatenate([oa, ob, om], axis=-1), w_out)


def odd_layer(h, mem, g, mem_g, w_mem_kv, m_qn, m_kn, w_in, w_out, lb, c_og,
              d_qn, d_kn_cmp, d_kn_slc, d_kn_win, pe_k, pe_v, w1k, w2k, w1v, w2v, pos):
    xn = rms_norm(h, g)
    z = jnp.einsum('bsd,de->bse', xn, w_in)
    (qc, fc, ic, gc, qd, kcd, vcd, ksd, vsd, kwd, vwd, gate_d, gd, qm, gm) = split_cols(z, ODD_SPLITS)
    oc = hgrn2(qc, fc, ic, lb, c_og) * jax.nn.silu(gc)
    od = nsa(qd, kcd, vcd, ksd, vsd, kwd, vwd, gate_d, d_qn, d_kn_cmp, d_kn_slc, d_kn_win,
             pe_k, pe_v, w1k, w2k, w1v, w2v, pos) * jax.nn.silu(gd)
    om = memory_cross_attention(qm, rms_norm(mem, mem_g), w_mem_kv, m_qn, m_kn) * jax.nn.silu(gm)
    return h + jnp.einsum('bse,ed->bsd', jnp.concatenate([oc, od, om], axis=-1), w_out)


def setup_inputs(seed: int = 0) -> dict:
    key = jax.random.key(seed)
    keys = iter(jax.random.split(key, 48))

    def nrm(shape, scale):
        return jax.random.normal(next(keys), shape, F32) * scale

    def gain(shape):
        return 1.0 + nrm(shape, 0.02)

    u = jax.random.uniform(next(keys), (N_EVEN, B_WIDTH), F32, minval=0.9, maxval=0.999)
    s = u ** (1.0 / B_C)
    lam = jnp.log(s) - jnp.log1p(-s)
    return {
        "x": nrm((BATCH, SEQ, D_MODEL), 1.0),
        "mem": nrm((BATCH, N_MEM, D_MODEL), 1.0),
        "norm_g": gain((DEPTH, D_MODEL)),
        "mem_norm_g": gain((DEPTH, D_MODEL)),
        "mem_w_kv": nrm((DEPTH, D_MODEL, 2 * M_WIDTH), D_MODEL ** -0.5),
        "mem_qn": gain((DEPTH, HEAD_DIM)),
        "mem_kn": gain((DEPTH, HEAD_DIM)),
        "ev_w_in": nrm((N_EVEN, D_MODEL, EVEN_IN), D_MODEL ** -0.5),
        "ev_w_out": nrm((N_EVEN, EVEN_MIX, D_MODEL), EVEN_MIX ** -0.5),
        "a_qn": gain((N_EVEN, HEAD_DIM)),
        "a_kn": gain((N_EVEN, HEAD_DIM)),
        "a_sinks": nrm((N_EVEN, A_HEADS), 0.5),
        "b_conv_w": nrm((N_EVEN, B_CONV, B_WIDTH), B_CONV ** -0.5),
        "b_conv_b": nrm((N_EVEN, B_WIDTH), 0.01),
        "b_w_r": nrm((N_EVEN, B_BLOCKS, B_BLOCK_DIM, B_BLOCK_DIM), B_BLOCK_DIM ** -0.5),
        "b_b_r": nrm((N_EVEN, B_WIDTH), 0.01),
        "b_w_i": nrm((N_EVEN, B_BLOCKS, B_BLOCK_DIM, B_BLOCK_DIM), B_BLOCK_DIM ** -0.5),
        "b_b_i": nrm((N_EVEN, B_WIDTH), 0.01),
        "b_lambda": lam,
        "od_w_in": nrm((N_ODD, D_MODEL, ODD_IN), D_MODEL ** -0.5),
        "od_w_out": nrm((N_ODD, ODD_MIX, D_MODEL), ODD_MIX ** -0.5),
        "c_lb": nrm((DEPTH, C_WIDTH), 0.5),
        "c_onorm": gain((N_ODD, C_HEAD_DIM)),
        "d_qn": gain((N_ODD, HEAD_DIM)),
        "d_kn_cmp": gain((N_ODD, HEAD_DIM)),
        "d_kn_slc": gain((N_ODD, HEAD_DIM)),
        "d_kn_win": gain((N_ODD, HEAD_DIM)),
        "d_pe_k": nrm((N_ODD, CMP_LEN, HEAD_DIM), 0.02),
        "d_pe_v": nrm((N_ODD, CMP_LEN, HEAD_DIM), 0.02),
        "d_w1k": nrm((N_ODD, CMP_LEN * HEAD_DIM, CMP_HIDDEN), (CMP_LEN * HEAD_DIM) ** -0.5),
        "d_w2k": nrm((N_ODD, CMP_HIDDEN, HEAD_DIM), CMP_HIDDEN ** -0.5),
        "d_w1v": nrm((N_ODD, CMP_LEN * HEAD_DIM, CMP_HIDDEN), (CMP_LEN * HEAD_DIM) ** -0.5),
        "d_w2v": nrm((N_ODD, CMP_HIDDEN, HEAD_DIM), CMP_HIDDEN ** -0.5),
    }


def reference(x, mem, norm_g, mem_norm_g, mem_w_kv, mem_qn, mem_kn, ev_w_in, ev_w_out, a_qn, a_kn, a_sinks,
              b_conv_w, b_conv_b, b_w_r, b_b_r, b_w_i, b_b_i, b_lambda, od_w_in, od_w_out, c_lb, c_onorm,
              d_qn, d_kn_cmp, d_kn_slc, d_kn_win, d_pe_k, d_pe_v, d_w1k, d_w2k, d_w1v, d_w2v):
    pos = jnp.arange(x.shape[1])
    lbs = hgrn_lower_bounds(c_lb)
    h = x
    for l in range(DEPTH):
        e = l // 2
        if l % 2 == 0:
            h = even_layer(h, mem, norm_g[l], mem_norm_g[l], mem_w_kv[l], mem_qn[l], mem_kn[l],
                           ev_w_in[e], ev_w_out[e], a_qn[e], a_kn[e], a_sinks[e],
                           b_conv_w[e], b_conv_b[e], b_w_r[e], b_b_r[e], b_w_i[e], b_b_i[e], b_lambda[e], pos)
        else:
            h = odd_layer(h, mem, norm_g[l], mem_norm_g[l], mem_w_kv[l], mem_qn[l], mem_kn[l],
                          od_w_in[e], od_w_out[e], lbs[l], c_onorm[e],
                          d_qn[e], d_kn_cmp[e], d_kn_slc[e], d_kn_win[e], d_pe_k[e], d_pe_v[e],
                          d_w1k[e], d_w2k[e], d_w1v[e], d_w2v[e], pos)
    return h
```

```python
import functools

import numpy as np
import jax
import jax.numpy as jnp
from jax import lax
from jax.experimental import pallas as pl
from jax.experimental.pallas import tpu as pltpu

F32 = jnp.float32
BF16 = jnp.bfloat16

D_MODEL = 1024
N_MEM = 256
HEAD_DIM = 64
ROPE_THETA = 10000.0
EPS = 1e-6
NEG_INF = -1e30
POS_INF = 1e30
LANES = 128

A_HEADS, A_KV_HEADS, A_WINDOW = 8, 2, 128
B_WIDTH, B_BLOCKS, B_CONV, B_C = 512, 8, 4, 8.0
M_HEADS = 4
C_HEADS, C_HEAD_DIM, C_CHUNK = 4, 128, 64
D_HEADS, D_KV_HEADS = 8, 2
CMP_LEN, CMP_STRIDE, CMP_HIDDEN = 32, 16, 128
SEL_LEN, SEL_TOPK = 64, 4
D_WINDOW = 512
D_BRANCHES = 3
SCALE = HEAD_DIM ** -0.5

EVEN_ORDER = [("qa", 512), ("ga", 512), ("xb", 512), ("gb", 512), ("qm", 256), ("gm", 256), ("ka", 128), ("va", 128)]
EVEN_SRC = {"qa": (0, 512), "ka": (512, 128), "va": (640, 128), "ga": (768, 512), "xb": (1280, 512),
            "gb": (1792, 512), "qm": (2304, 256), "gm": (2560, 256)}
ODD_ORDER = [("qc", 512), ("fc", 512), ("ic", 512), ("gc", 512), ("qd", 512), ("gd", 512), ("qm", 256), ("gm", 256),
             ("kcd", 128), ("vcd", 128), ("ksd", 128), ("vsd", 128), ("kwd", 128), ("vwd", 128), ("gate", 128)]
ODD_SRC = {"qc": (0, 512), "fc": (512, 512), "ic": (1024, 512), "gc": (1536, 512), "qd": (2048, 512),
           "kcd": (2560, 128), "vcd": (2688, 128), "ksd": (2816, 128), "vsd": (2944, 128), "kwd": (3072, 128),
           "vwd": (3200, 128), "gate": (3328, 24), "gd": (3352, 512), "qm": (3864, 256), "gm": (4120, 256)}

VMEM_LIMIT = 48 * 1024 * 1024


def _offsets(order):
    off, out = 0, {}
    for name, w in order:
        out[name] = (off, w)
        off += w
    return out, off


EVEN_OFF, EVEN_COLS = _offsets(EVEN_ORDER)
ODD_OFF, ODD_COLS = _offsets(ODD_ORDER)


def _cparams(sem):
    return pltpu.CompilerParams(dimension_semantics=sem, vmem_limit_bytes=VMEM_LIMIT)


def _dot(a, b):
    return jnp.dot(a, b, preferred_element_type=F32)


def _dot_nt(a, b):
    return lax.dot_general(a, b, (((1,), (1,)), ((), ())), preferred_element_type=F32)


def _split(x):
    hi = x.astype(BF16)
    lo = (x - hi.astype(F32)).astype(BF16)
    return hi, lo


def _dot_split_lhs(x, m):
    hi, lo = _split(x)
    return _dot(hi, m) + _dot(lo, m)


def _dot_split_rhs(m, x):
    hi, lo = _split(x)
    return _dot(m, hi) + _dot(m, lo)


def _lane(shape):
    return lax.broadcasted_iota(jnp.int32, shape, len(shape) - 1)


def _row(shape):
    return lax.broadcasted_iota(jnp.int32, shape, len(shape) - 2)


def _silu(x):
    return x * jax.nn.sigmoid(x)


def _seg_ones():
    r = lax.broadcasted_iota(jnp.int32, (LANES, LANES), 0) >> 6
    c = lax.broadcasted_iota(jnp.int32, (LANES, LANES), 1) >> 6
    return jnp.where(r == c, 1.0, 0.0).astype(BF16)


def _head_rms(x, gain):
    seg = _seg_ones()
    cols = []
    for c in range(x.shape[1] // LANES):
        xc = x[:, c * LANES:(c + 1) * LANES]
        ms = _dot_split_lhs(xc * xc, seg) * (1.0 / HEAD_DIM)
        cols.append(xc * lax.rsqrt(ms + EPS))
    y = cols[0] if len(cols) == 1 else jnp.concatenate(cols, axis=1)
    return y * gain


def _rope(x, cos, sin_s):
    first = (_lane((x.shape[0], LANES)) & 63) < 32
    cols = []
    for c in range(x.shape[1] // LANES):
        xc = x[:, c * LANES:(c + 1) * LANES]
        sw = jnp.where(first, pltpu.roll(xc, 96, 1), pltpu.roll(xc, 32, 1))
        cols.append(xc * cos + sw * sin_s)
    return cols[0] if len(cols) == 1 else jnp.concatenate(cols, axis=1)


def _dup_heads(k):
    sw = pltpu.roll(k, 64, 1)
    lo = _lane(k.shape) < 64
    return jnp.where(lo, k, sw), jnp.where(lo, sw, k)


def _q_heads(q):
    lo = _lane((q.shape[0], LANES)) < 64
    out = []
    for c in range(q.shape[1] // LANES):
        qc = q[:, c * LANES:(c + 1) * LANES]
        out.append(jnp.where(lo, qc, 0.0).astype(BF16))
        out.append(jnp.where(lo, 0.0, qc).astype(BF16))
    return out


def _merge_heads(o):
    lo = _lane(o[0].shape) < 64
    cols = [jnp.where(lo, o[2 * c], o[2 * c + 1]) for c in range(len(o) // 2)]
    return cols[0] if len(cols) == 1 else jnp.concatenate(cols, axis=1)


def _inproj_kernel(x_ref, g_ref, w_ref, o_ref):
    x = x_ref[...]
    ms = jnp.mean(x * x, axis=-1, keepdims=True)
    xn = (x * lax.rsqrt(ms + EPS) * g_ref[...]).astype(BF16)
    n = o_ref.shape[-1]
    for c in range(0, n, 512):
        w = min(512, n - c)
        o_ref[:, c:c + w] = _dot(xn, w_ref[:, c:c + w])


def _inproj(x2d, gain, w_bf16, tm=256):
    n, d = x2d.shape
    nc = w_bf16.shape[1]
    return pl.pallas_call(
        _inproj_kernel,
        grid=(n // tm,),
        in_specs=[pl.BlockSpec((tm, d), lambda i: (i, 0)),
                  pl.BlockSpec((1, d), lambda i: (0, 0)),
                  pl.BlockSpec((d, nc), lambda i: (0, 0))],
        out_specs=pl.BlockSpec((tm, nc), lambda i: (i, 0)),
        out_shape=jax.ShapeDtypeStruct((n, nc), F32),
        compiler_params=_cparams(("arbitrary",)),
        name="inproj",
    )(x2d, gain.reshape(1, d), w_bf16)


def _outproj_kernel(h_ref, a_ref, b_ref, m_ref, w_ref, o_ref):
    a, b, m = a_ref[...], b_ref[...], m_ref[...]
    for c in range(0, D_MODEL, 256):
        acc = h_ref[:, c:c + 256]
        acc += _dot(a, w_ref[0:512, c:c + 256])
        acc += _dot(b, w_ref[512:1024, c:c + 256])
        acc += _dot(m, w_ref[1024:1280, c:c + 256])
        o_ref[:, c:c + 256] = acc


def _outproj(h2d, oa, ob, om, w_bf16, tm=512):
    n, d = h2d.shape
    return pl.pallas_call(
        _outproj_kernel,
        grid=(n // tm,),
        in_specs=[pl.BlockSpec((tm, d), lambda i: (i, 0)),
                  pl.BlockSpec((tm, 512), lambda i: (i, 0)),
                  pl.BlockSpec((tm, 512), lambda i: (i, 0)),
                  pl.BlockSpec((tm, 256), lambda i: (i, 0)),
                  pl.BlockSpec((1280, d), lambda i: (0, 0))],
        out_specs=pl.BlockSpec((tm, d), lambda i: (i, 0)),
        out_shape=jax.ShapeDtypeStruct((n, d), F32),
        compiler_params=_cparams(("arbitrary",)),
        name="outproj",
    )(h2d, oa, ob, om, w_bf16)


def _mem_kernel(q_ref, gm_ref, mem_ref, mg_ref, wkv_ref, qn_ref, kn_ref, o_ref, kd_scr, vd_scr):
    @pl.when(pl.program_id(1) == 0)
    def _():
        m = mem_ref[0]
        ms = jnp.mean(m * m, axis=-1, keepdims=True)
        mn = (m * lax.rsqrt(ms + EPS) * mg_ref[...]).astype(BF16)
        kv = _dot(mn, wkv_ref[...])
        km = _head_rms(kv[:, 0:256], kn_ref[...])
        vm = kv[:, 256:512]
        for c in range(2):
            k0, k1 = _dup_heads(km[:, c * LANES:(c + 1) * LANES])
            v0, v1 = _dup_heads(vm[:, c * LANES:(c + 1) * LANES])
            kd_scr[2 * c] = k0.astype(BF16)
            kd_scr[2 * c + 1] = k1.astype(BF16)
            vd_scr[2 * c] = v0.astype(BF16)
            vd_scr[2 * c + 1] = v1.astype(BF16)

    q = _head_rms(q_ref[0], qn_ref[...])
    outs = []
    for h, qh in enumerate(_q_heads(q)):
        s = _dot_nt(qh, kd_scr[h]) * SCALE
        m = jnp.max(s, axis=-1, keepdims=True)
        e = jnp.exp(s - m)
        l = jnp.sum(e, axis=-1, keepdims=True)
        outs.append(_dot(e.astype(BF16), vd_scr[h]) / l)
    o_ref[0] = (_merge_heads(outs) * _silu(gm_ref[0])).astype(BF16)


def _mem_attention(z, q_off, g_off, mem, mem_g, wkv_bf16, qn, kn, tq=256):
    b, s, _ = z.shape
    return pl.pallas_call(
        _mem_kernel,
        grid=(b, s // tq),
        in_specs=[pl.BlockSpec((1, tq, 256), lambda bi, i: (bi, i, q_off // 256)),
                  pl.BlockSpec((1, tq, 256), lambda bi, i: (bi, i, g_off // 256)),
                  pl.BlockSpec((1, N_MEM, D_MODEL), lambda bi, i: (bi, 0, 0)),
                  pl.BlockSpec((1, D_MODEL), lambda bi, i: (0, 0)),
                  pl.BlockSpec((D_MODEL, 512), lambda bi, i: (0, 0)),
                  pl.BlockSpec((1, 256), lambda bi, i: (0, 0)),
                  pl.BlockSpec((1, 256), lambda bi, i: (0, 0))],
        out_specs=pl.BlockSpec((1, tq, 256), lambda bi, i: (bi, i, 0)),
        out_shape=jax.ShapeDtypeStruct((b, s, 256), BF16),
        scratch_shapes=[pltpu.VMEM((M_HEADS, N_MEM, LANES), BF16), pltpu.VMEM((M_HEADS, N_MEM, LANES), BF16)],
        compiler_params=_cparams(("arbitrary", "arbitrary")),
        name="mem_attention",
    )(z, z, mem, mem_g.reshape(1, D_MODEL), wkv_bf16, jnp.tile(qn, 4).reshape(1, 256), jnp.tile(kn, 4).reshape(1, 256))


def _swa_kernel(q_ref, g_ref, kc_ref, kp_ref, vc_ref, vp_ref, cosc_ref, sinc_ref, cosp_ref, sinp_ref,
                qn_ref, kn_ref, sink_ref, o_ref):
    i = pl.program_id(1)
    t = 128
    q = _rope(_head_rms(q_ref[0], qn_ref[...]), cosc_ref[...], sinc_ref[...])
    kc = _rope(_head_rms(kc_ref[0], kn_ref[...]), cosc_ref[...], sinc_ref[...])
    kp = _rope(_head_rms(kp_ref[0], kn_ref[...]), cosp_ref[...], sinp_ref[...])
    k_all = jnp.concatenate([kp, kc], axis=0)
    v_all = jnp.concatenate([vp_ref[0], vc_ref[0]], axis=0)
    kd = [x.astype(BF16) for x in _dup_heads(k_all)]
    vd = [x.astype(BF16) for x in _dup_heads(v_all)]
    tq = i * t + _row((t, 2 * t))
    ts = (i - 1) * t + _lane((t, 2 * t))
    rel = tq - ts
    mask = (rel >= 0) & (rel < A_WINDOW) & (ts >= 0)
    outs = []
    for h, qh in enumerate(_q_heads(q)):
        kv = h // (A_HEADS // A_KV_HEADS)
        s = jnp.where(mask, _dot_nt(qh, kd[kv]) * SCALE, NEG_INF)
        sink = sink_ref[h:h + 1, 0:1]
        m = jnp.maximum(jnp.max(s, axis=-1, keepdims=True), sink)
        e = jnp.exp(s - m)
        den = jnp.sum(e, axis=-1, keepdims=True) + jnp.exp(sink - m)
        outs.append(_dot(e.astype(BF16), vd[kv]) / den)
    o_ref[0] = (_merge_heads(outs) * _silu(g_ref[0])).astype(BF16)


def _swa(z, cos, sin_s, qn, kn, sinks):
    b, s, _ = z.shape
    t = 128
    qo, go = EVEN_OFF["qa"][0] // 512, EVEN_OFF["ga"][0] // 512
    ko, vo = EVEN_OFF["ka"][0] // 128, EVEN_OFF["va"][0] // 128
    prev = lambda i: jnp.maximum(i - 1, 0)
    return pl.pallas_call(
        _swa_kernel,
        grid=(b, s // t),
        in_specs=[pl.BlockSpec((1, t, 512), lambda bi, i: (bi, i, qo)),
                  pl.BlockSpec((1, t, 512), lambda bi, i: (bi, i, go)),
                  pl.BlockSpec((1, t, 128), lambda bi, i: (bi, i, ko)),
                  pl.BlockSpec((1, t, 128), lambda bi, i: (bi, prev(i), ko)),
                  pl.BlockSpec((1, t, 128), lambda bi, i: (bi, i, vo)),
                  pl.BlockSpec((1, t, 128), lambda bi, i: (bi, prev(i), vo)),
                  pl.BlockSpec((t, 128), lambda bi, i: (i, 0)),
                  pl.BlockSpec((t, 128), lambda bi, i: (i, 0)),
                  pl.BlockSpec((t, 128), lambda bi, i: (prev(i), 0)),
                  pl.BlockSpec((t, 128), lambda bi, i: (prev(i), 0)),
                  pl.BlockSpec((1, 512), lambda bi, i: (0, 0)),
                  pl.BlockSpec((1, 128), lambda bi, i: (0, 0)),
                  pl.BlockSpec((A_HEADS, 128), lambda bi, i: (0, 0))],
        out_specs=pl.BlockSpec((1, t, 512), lambda bi, i: (bi, i, 0)),
        out_shape=jax.ShapeDtypeStruct((b, s, 512), BF16),
        compiler_params=_cparams(("arbitrary", "arbitrary")),
        name="swa",
    )(z, z, z, z, z, z, cos, sin_s, cos, sin_s, jnp.tile(qn, 8).reshape(1, 512), jnp.tile(kn, 2).reshape(1, 128),
      jnp.broadcast_to(sinks.reshape(A_HEADS, 1), (A_HEADS, 128)))


def _rglru_kernel(x_ref, g_ref, cw_ref, cb_ref, wr_ref, br_ref, wi_ref, bi_ref, lam_ref, o_ref,
                  xbuf, hcar, a_scr, u_scr, h_scr):
    t = x_ref.shape[1]

    @pl.when(pl.program_id(1) == 0)
    def _():
        xbuf[0:8, :] = jnp.zeros((8, B_WIDTH), F32)
        hcar[...] = jnp.zeros((8, B_WIDTH), F32)

    xbuf[8:t + 8, :] = x_ref[0]
    xc = cb_ref[...] + cw_ref[0:1, :] * xbuf[5:t + 5, :]
    for j in range(1, B_CONV):
        xc = xc + cw_ref[j:j + 1, :] * xbuf[5 + j:t + 5 + j, :]
    xbuf[0:8, :] = xbuf[t:t + 8, :]

    r_cols, i_cols = [], []
    for c in range(B_WIDTH // LANES):
        xcc = xc[:, c * LANES:(c + 1) * LANES].astype(BF16)
        r_cols.append(_dot(xcc, wr_ref[c]))
        i_cols.append(_dot(xcc, wi_ref[c]))
    r = jax.nn.sigmoid(jnp.concatenate(r_cols, axis=1) + br_ref[...])
    ig = jax.nn.sigmoid(jnp.concatenate(i_cols, axis=1) + bi_ref[...])
    nl = -lam_ref[...]
    softplus = jnp.maximum(nl, 0.0) + jnp.log(1.0 + jnp.exp(-jnp.abs(nl)))
    log_a = -B_C * r * softplus
    a = jnp.exp(log_a)
    u = jnp.sqrt(1.0 - jnp.exp(2.0 * log_a)) * (ig * xc)

    r8 = _row((t, B_WIDTH)) & 7
    for d in (1, 2, 4):
        a_sh = pltpu.roll(a, d, 0)
        u_sh = pltpu.roll(u, d, 0)
        m = r8 >= d
        u = jnp.where(m, a * u_sh + u, u)
        a = jnp.where(m, a * a_sh, a)
    a_scr[...] = a
    u_scr[...] = u

    def body(j, h):
        off = pl.multiple_of(j * 8, 8)
        hh = a_scr[pl.ds(off, 8), :] * h + u_scr[pl.ds(off, 8), :]
        h_scr[pl.ds(off, 8), :] = hh
        return hh[7:8, :]

    h_last = lax.fori_loop(0, t // 8, body, hcar[0:1, :])
    hcar[0:1, :] = h_last
    o_ref[0] = (h_scr[...] * _silu(g_ref[0])).astype(BF16)


def _rglru(z, conv_w, conv_b, wr_bd, b_r, wi_bd, b_i, lam, t=256):
    b, s, _ = z.shape
    xo, go = EVEN_OFF["xb"][0] // 512, EVEN_OFF["gb"][0] // 512
    row = lambda v: v.reshape(1, B_WIDTH)
    full = lambda shp: pl.BlockSpec(shp, lambda bi, i: (0,) * len(shp))
    return pl.pallas_call(
        _rglru_kernel,
        grid=(b, s // t),
        in_specs=[pl.BlockSpec((1, t, 512), lambda bi, i: (bi, i, xo)),
                  pl.BlockSpec((1, t, 512), lambda bi, i: (bi, i, go)),
                  full((B_CONV, B_WIDTH)), full((1, B_WIDTH)),
                  full((4, LANES, LANES)), full((1, B_WIDTH)),
                  full((4, LANES, LANES)), full((1, B_WIDTH)), full((1, B_WIDTH))],
        out_specs=pl.BlockSpec((1, t, 512), lambda bi, i: (bi, i, 0)),
        out_shape=jax.ShapeDtypeStruct((b, s, 512), BF16),
        scratch_shapes=[pltpu.VMEM((t + 8, B_WIDTH), F32), pltpu.VMEM((8, B_WIDTH), F32),
                        pltpu.VMEM((t, B_WIDTH), F32), pltpu.VMEM((t, B_WIDTH), F32), pltpu.VMEM((t, B_WIDTH), F32)],
        compiler_params=_cparams(("arbitrary", "arbitrary")),
        name="rglru",
    )(z, z, conv_w, row(conv_b), wr_bd, row(b_r), wi_bd, row(b_i), row(lam))


def _hgrn_consts():
    c = C_CHUNK
    t = np.arange(c)[:, None]
    s = np.arange(c)[None, :]
    mats = [(s <= t)]
    masks = []
    hs = c // 2
    while hs >= 1:
        mid = (t // (2 * hs)) * 2 * hs + hs - 1
        mats.append(s <= mid)
        same = (t // (2 * hs)) == (s // (2 * hs))
        masks.append(same & ((t // hs) % 2 == 1) & ((s // hs) % 2 == 0))
        hs //= 2
    return np.concatenate(mats, axis=0).astype(np.float32), np.stack(masks).astype(np.float32)


def _hgrn_kernel(q_ref, f_ref, i_ref, g_ref, lb_ref, og_ref, mst_ref, msk_ref, o_ref, st_scr):
    @pl.when(pl.program_id(1) == 0)
    def _():
        st_scr[...] = jnp.zeros(st_scr.shape, F32)

    c = C_CHUNK
    nlev = msk_ref.shape[0]
    p = lb_ref[...]
    pm = jnp.maximum(p[0:1, :], p[1:2, :])
    e0, e1 = jnp.exp(p[0:1, :] - pm), jnp.exp(p[1:2, :] - pm)
    lb = e1 / (e0 + e1)
    for ch in range(q_ref.shape[1] // c):
        rows = slice(ch * c, (ch + 1) * c)
        f = lb + (1.0 - lb) * jax.nn.sigmoid(f_ref[0, rows, :])
        g = jnp.log(f)
        bst = _dot_split_rhs(mst_ref[...], g)
        kk_all = 1.0 - f
        qf_all = _silu(q_ref[0, rows, :])
        v_all = i_ref[0, rows, :]
        outs = []
        for h in range(C_HEADS):
            col = slice(h * LANES, (h + 1) * LANES)
            qf, kk, v = qf_all[:, col], kk_all[:, col], v_all[:, col]
            b = bst[0:c, col]
            st = st_scr[h]
            o = _dot_nt((qf * jnp.exp(b)).astype(BF16), st.astype(BF16))
            att = jnp.zeros((c, c), F32)
            for l in range(nlev):
                bm = bst[(l + 1) * c:(l + 2) * c, col]
                qt = (qf * jnp.exp(jnp.minimum(b - bm, 0.0))).astype(BF16)
                kt = (kk * jnp.exp(jnp.minimum(bm - b, 0.0))).astype(BF16)
                att = att + msk_ref[l] * _dot_nt(qt, kt)
            diag = jnp.sum(qf * kk, axis=-1, keepdims=True)
            vb = v.astype(BF16)
            o = o + _dot(att.astype(BF16), vb) + diag * v
            b_last = b[c - 1:c, :]
            kd = (kk * jnp.exp(b_last - b)).astype(BF16)
            st_scr[h] = st * jnp.exp(b_last) + _dot(v.T.astype(BF16), kd)
            ms = jnp.mean(o * o, axis=-1, keepdims=True)
            outs.append(o * lax.rsqrt(ms + EPS) * og_ref[...])
        o_ref[0, rows, :] = (jnp.concatenate(outs, axis=1) * _silu(g_ref[0, rows, :])).astype(BF16)


def _hgrn(z, c_lb, c_og, t=256):
    b, s, _ = z.shape
    mst, msk = _hgrn_consts()
    blk = lambda name: pl.BlockSpec((1, t, 512), lambda bi, i, o=ODD_OFF[name][0] // 512: (bi, i, o))
    full = lambda shp: pl.BlockSpec(shp, lambda bi, i: (0,) * len(shp))
    return pl.pallas_call(
        _hgrn_kernel,
        grid=(b, s // t),
        in_specs=[blk("qc"), blk("fc"), blk("ic"), blk("gc"), full(c_lb.shape), full((1, C_HEAD_DIM)),
                  full(mst.shape), full(msk.shape)],
        out_specs=pl.BlockSpec((1, t, 512), lambda bi, i: (bi, i, 0)),
        out_shape=jax.ShapeDtypeStruct((b, s, 512), BF16),
        scratch_shapes=[pltpu.VMEM((C_HEADS, C_HEAD_DIM, C_HEAD_DIM), F32)],
        compiler_params=_cparams(("arbitrary", "arbitrary")),
        name="hgrn2",
    )(z, z, z, z, c_lb, c_og.reshape(1, C_HEAD_DIM), jnp.asarray(mst, BF16), jnp.asarray(msk, F32))


def _nsa_prep_kernel(kcr_ref, vcr_ref, ks_ref, vs_ref, kw_ref, vw_ref, cos_ref, sin_ref, cosc_ref, sinc_ref,
                     kncmp_ref, knslc_ref, knwin_ref, pek_ref, pev_ref, w1k_ref, w2k_ref, w1v_ref, w2v_ref,
                     kc_o, vc_o, ks_o, vs_o, kw_o, vw_o):
    def compress(xr_ref, pe_ref, w1_ref, w2_ref):
        xr = xr_ref[0]
        top = _dot((xr + pe_ref[0:1, :]).astype(BF16), w1_ref[0])
        bot = _dot((xr + pe_ref[1:2, :]).astype(BF16), w1_ref[1])
        pre = top + pltpu.roll(bot, bot.shape[0] - 1, 0)
        return _dot(_silu(pre).astype(BF16), w2_ref[...])

    kc = _rope(_head_rms(compress(kcr_ref, pek_ref, w1k_ref, w2k_ref), kncmp_ref[...]), cosc_ref[...], sinc_ref[...])
    vc = compress(vcr_ref, pev_ref, w1v_ref, w2v_ref)
    for src, dst in ((kc, kc_o), (vc, vc_o)):
        d0, d1 = _dup_heads(src)
        dst[0, 0] = d0.astype(BF16)
        dst[0, 1] = d1.astype(BF16)

    s = ks_ref.shape[1]
    step = 256
    for r0 in range(0, s, step):
        rows = slice(r0, r0 + step)
        cos, sin = cos_ref[rows, :], sin_ref[rows, :]
        ks = _rope(_head_rms(ks_ref[0, rows, :], knslc_ref[...]), cos, sin)
        kw = _rope(_head_rms(kw_ref[0, rows, :], knwin_ref[...]), cos, sin)
        for src, dst in ((ks, ks_o), (vs_ref[0, rows, :], vs_o), (kw, kw_o), (vw_ref[0, rows, :], vw_o)):
            d0, d1 = _dup_heads(src)
            dst[0, 0, rows, :] = d0.astype(BF16)
            dst[0, 1, rows, :] = d1.astype(BF16)


def _nsa_prep(z, kcr, vcr, cos, sin_s, cosc, sinc, kn_cmp, kn_slc, kn_win, pek, pev, w1k, w2k, w1v, w2v):
    b, s, _ = z.shape
    nseg = s // CMP_STRIDE
    zb = lambda name: pl.BlockSpec((1, s, 128), lambda bi, o=ODD_OFF[name][0] // 128: (bi, 0, o))
    full = lambda shp: pl.BlockSpec(shp, lambda bi: (0,) * len(shp))
    seg = pl.BlockSpec((1, nseg, 2048), lambda bi: (bi, 0, 0))
    small = jax.ShapeDtypeStruct((b, 2, nseg, 128), BF16)
    big = jax.ShapeDtypeStruct((b, 2, s, 128), BF16)
    small_spec = pl.BlockSpec((1, 2, nseg, 128), lambda bi: (bi, 0, 0, 0))
    big_spec = pl.BlockSpec((1, 2, s, 128), lambda bi: (bi, 0, 0, 0))
    g2 = lambda g: jnp.tile(g, 2).reshape(1, 128)
    return pl.pallas_call(
        _nsa_prep_kernel,
        grid=(b,),
        in_specs=[seg, seg, zb("ksd"), zb("vsd"), zb("kwd"), zb("vwd"),
                  full((s, 128)), full((s, 128)), full((nseg, 128)), full((nseg, 128)),
                  full((1, 128)), full((1, 128)), full((1, 128)), full((2, 2048)), full((2, 2048)),
                  full((2, 2048, 256)), full((256, 128)), full((2, 2048, 256)), full((256, 128))],
        out_specs=[small_spec, small_spec, big_spec, big_spec, big_spec, big_spec],
        out_shape=[small, small, big, big, big, big],
        compiler_params=_cparams(("arbitrary",)),
        name="nsa_prep",
    )(kcr, vcr, z, z, z, z, cos, sin_s, cosc, sinc, g2(kn_cmp), g2(kn_slc), g2(kn_win), pek, pev, w1k, w2k, w1v, w2v)


def _nsa_kernel(q_ref, gd_ref, gate_ref, cos_ref, sin_ref, qn_ref, kc_ref, vc_ref, ks_ref, vs_ref, kw_ref, vw_ref,
                ovl_ref, gexp_ref, o_ref):
    i = pl.program_id(1)
    t = 128
    grp = D_HEADS // D_KV_HEADS
    q = _rope(_head_rms(q_ref[0], qn_ref[...]), cos_ref[...], sin_ref[...])
    qh = _q_heads(q)
    tpos = i * t + _row((t, LANES))
    lane = _lane((t, LANES))
    lane_f = lane.astype(F32)

    ncmp = kc_ref.shape[2] - 1
    mask_c = (lane * CMP_STRIDE + (CMP_LEN - 1) <= tpos) & (lane < ncmp)
    row_ok = tpos >= CMP_LEN - 1
    cur = tpos // SEL_LEN
    nsel = ks_ref.shape[2] // SEL_LEN
    o_cmp, sel = [], []
    for kv in range(D_KV_HEADS):
        psum = jnp.zeros((t, LANES), F32)
        for g in range(grp):
            s = jnp.where(mask_c, _dot_nt(qh[kv * grp + g], kc_ref[0, kv]) * SCALE, NEG_INF)
            m = jnp.max(s, axis=-1, keepdims=True)
            e = jnp.exp(s - m)
            p = jnp.where(row_ok, e / jnp.sum(e, axis=-1, keepdims=True), 0.0)
            o_cmp.append(_dot(p.astype(BF16), vc_ref[0, kv]))
            psum = psum + p
        imp = _dot_split_lhs(psum, ovl_ref[...])
        forced = (lane == 0) | (lane == cur)
        score = jnp.where(forced, POS_INF, jnp.where(lane <= cur, imp, NEG_INF))
        score = jnp.where(lane < nsel, score, -2e38)
        chosen = jnp.zeros((t, LANES), F32)
        for _ in range(min(SEL_TOPK, nsel)):
            mx = jnp.max(score, axis=-1, keepdims=True)
            first = jnp.min(jnp.where(score == mx, lane_f, 1e9), axis=-1, keepdims=True)
            hit = lane_f == first
            chosen = jnp.where(hit, 1.0, chosen)
            score = jnp.where(hit, -3e38, score)
        sel.append(chosen.astype(BF16))

    blk_row = _row((LANES, LANES))
    blk_lane = _lane((LANES, LANES)) >> 6

    def sel_body(j, carry):
        ms, ls, accs = carry
        off = pl.multiple_of(j * t, t)
        expand = jnp.where(blk_row == 2 * j + blk_lane, 1.0, 0.0).astype(BF16)
        causal = (j * t + lane) <= tpos
        ms2, ls2, accs2 = [], [], []
        for kv in range(D_KV_HEADS):
            keep = (_dot(sel[kv], expand) > 0.5) & causal
            k = ks_ref[0, kv, pl.ds(off, t), :]
            v = vs_ref[0, kv, pl.ds(off, t), :]
            for g in range(grp):
                h = kv * grp + g
                s = jnp.where(keep, _dot_nt(qh[h], k) * SCALE, NEG_INF)
                m_new = jnp.maximum(ms[h], jnp.max(s, axis=-1, keepdims=True))
                alpha = jnp.exp(ms[h] - m_new)
                e = jnp.where(keep, jnp.exp(s - m_new), 0.0)
                ls2.append(alpha * ls[h] + jnp.sum(e, axis=-1, keepdims=True))
                accs2.append(alpha * accs[h] + _dot(e.astype(BF16), v))
                ms2.append(m_new)
        return tuple(ms2), tuple(ls2), tuple(accs2)

    init = (tuple(jnp.full((t, 1), NEG_INF, F32) for _ in range(D_HEADS)),
            tuple(jnp.zeros((t, 1), F32) for _ in range(D_HEADS)),
            tuple(jnp.zeros((t, LANES), F32) for _ in range(D_HEADS)))
    _, ls, accs = lax.fori_loop(0, i + 1, sel_body, init)
    o_sel = [accs[h] / ls[h] for h in range(D_HEADS)]

    ntile = D_WINDOW // t + 1
    o_win = []
    masks, offs = [], []
    for jj in range(ntile):
        j = i - (ntile - 1) + jj
        ts = j * t + lane
        rel = tpos - ts
        masks.append((ts >= 0) & (rel >= 0) & (rel < D_WINDOW))
        offs.append(pl.multiple_of(jnp.maximum(j, 0) * t, t))
    for kv in range(D_KV_HEADS):
        ks = [kw_ref[0, kv, pl.ds(offs[jj], t), :] for jj in range(ntile)]
        vs = [vw_ref[0, kv, pl.ds(offs[jj], t), :] for jj in range(ntile)]
        for g in range(grp):
            h = kv * grp + g
            ss = [jnp.where(masks[jj], _dot_nt(qh[h], ks[jj]) * SCALE, NEG_INF) for jj in range(ntile)]
            m = functools.reduce(jnp.maximum, [jnp.max(s, axis=-1, keepdims=True) for s in ss])
            es = [jnp.exp(s - m) for s in ss]
            l = functools.reduce(jnp.add, [jnp.sum(e, axis=-1, keepdims=True) for e in es])
            acc = functools.reduce(jnp.add, [_dot(es[jj].astype(BF16), vs[jj]) for jj in range(ntile)])
            o_win.append(acc / l)

    gs = jax.nn.sigmoid(gate_ref[0])
    gates = [_dot_split_lhs(gs, gexp_ref[br]) for br in range(D_BRANCHES)]
    o = gates[0] * _merge_heads(o_cmp) + gates[1] * _merge_heads(o_sel) + gates[2] * _merge_heads(o_win)
    o_ref[0] = (o * _silu(gd_ref[0])).astype(BF16)


def _nsa_consts(s):
    ncmp = (s - CMP_LEN) // CMP_STRIDE + 1
    nsel = s // SEL_LEN
    cs = np.arange(ncmp)[:, None] * CMP_STRIDE
    ss = np.arange(nsel)[None, :] * SEL_LEN
    ovl = np.zeros((LANES, LANES), np.float32)
    ovl[:ncmp, :nsel] = (cs < ss + SEL_LEN) & (cs + CMP_LEN > ss)
    gexp = np.zeros((D_BRANCHES, LANES, D_HEADS * HEAD_DIM), np.float32)
    for h in range(D_HEADS):
        for br in range(D_BRANCHES):
            gexp[br, h * D_BRANCHES + br, h * HEAD_DIM:(h + 1) * HEAD_DIM] = 1.0
    return ovl, gexp


def _nsa(z, cos, sin_s, qn, kc, vc, ks, vs, kw, vw):
    b, s, _ = z.shape
    t = 128
    nseg = s // CMP_STRIDE
    ovl, gexp = _nsa_consts(s)
    full = lambda shp: pl.BlockSpec(shp, lambda bi, i: (0,) * len(shp))
    small_spec = pl.BlockSpec((1, 2, nseg, 128), lambda bi, i: (bi, 0, 0, 0))
    big_spec = pl.BlockSpec((1, 2, s, 128), lambda bi, i: (bi, 0, 0, 0))
    return pl.pallas_call(
        _nsa_kernel,
        grid=(b, s // t),
        in_specs=[pl.BlockSpec((1, t, 512), lambda bi, i: (bi, i, ODD_OFF["qd"][0] // 512)),
                  pl.BlockSpec((1, t, 512), lambda bi, i: (bi, i, ODD_OFF["gd"][0] // 512)),
                  pl.BlockSpec((1, t, 128), lambda bi, i: (bi, i, ODD_OFF["gate"][0] // 128)),
                  pl.BlockSpec((t, 128), lambda bi, i: (i, 0)),
                  pl.BlockSpec((t, 128), lambda bi, i: (i, 0)),
                  full((1, 512)), small_spec, small_spec, big_spec, big_spec, big_spec, big_spec,
                  full((LANES, LANES)), full((D_BRANCHES, LANES, 512))],
        out_specs=pl.BlockSpec((1, t, 512), lambda bi, i: (bi, i, 0)),
        out_shape=jax.ShapeDtypeStruct((b, s, 512), BF16),
        compiler_params=_cparams(("arbitrary", "arbitrary")),
        name="nsa",
    )(z, z, z, cos, sin_s, jnp.tile(qn, 8).reshape(1, 512), kc, vc, ks, vs, kw, vw,
      jnp.asarray(ovl, BF16), jnp.asarray(gexp, BF16))


def _permute_cols(w, order, src):
    cols = []
    for name, width in order:
        o, sw = src[name]
        blk = w[:, o:o + sw]
        if sw < width:
            blk = jnp.pad(blk, ((0, 0), (0, width - sw)))
        cols.append(blk)
    return jnp.concatenate(cols, axis=1).astype(BF16)


def _block_diag_pairs(w):
    z = jnp.zeros((4, LANES, LANES), w.dtype)
    z = z.at[:, 0:64, 0:64].set(w[0::2])
    z = z.at[:, 64:128, 64:128].set(w[1::2])
    return z.astype(BF16)


def _rope_tables(pos):
    half = HEAD_DIM // 2
    inv = ROPE_THETA ** (-jnp.arange(half, dtype=F32) / half)
    ang = pos.astype(F32)[:, None] * inv[None, :]
    cos, sin = jnp.cos(ang), jnp.sin(ang)
    cos_t = jnp.tile(cos, (1, 4))
    sin_t = jnp.tile(jnp.concatenate([-sin, sin], axis=1), (1, 2))
    return cos_t, sin_t


def _expand_compress_w1(w1):
    hdim = w1.shape[1]
    w = w1.reshape(2, CMP_STRIDE, HEAD_DIM, hdim)
    z = jnp.zeros((2, CMP_STRIDE, 2, HEAD_DIM, 2, hdim), w1.dtype)
    z = z.at[:, :, 0, :, 0, :].set(w)
    z = z.at[:, :, 1, :, 1, :].set(w)
    return z.reshape(2, CMP_STRIDE * 2 * HEAD_DIM, 2 * hdim).astype(BF16)


def _expand_compress_w2(w2):
    hdim, hd = w2.shape
    z = jnp.zeros((2 * hdim, 2 * hd), w2.dtype)
    z = z.at[0:hdim, 0:hd].set(w2)
    z = z.at[hdim:, hd:].set(w2)
    return z.astype(BF16)


def _expand_pe(pe):
    p = pe.reshape(2, CMP_STRIDE, 1, HEAD_DIM)
    return jnp.broadcast_to(p, (2, CMP_STRIDE, 2, HEAD_DIM)).reshape(2, CMP_STRIDE * 2 * HEAD_DIM)


def _even_layer(h, mem, g, mem_g, w_mem_kv, m_qn, m_kn, w_in, w_out, a_qn, a_kn, a_sinks,
                conv_w, conv_b, w_r, b_r, w_i, b_i, lam, cos, sin_s):
    b, s, d = h.shape
    h2 = h.reshape(b * s, d)
    z = _inproj(h2, g, _permute_cols(w_in, EVEN_ORDER, EVEN_SRC)).reshape(b, s, EVEN_COLS)
    oa = _swa(z, cos, sin_s, a_qn, a_kn, a_sinks)
    ob = _rglru(z, conv_w, conv_b, _block_diag_pairs(w_r), b_r, _block_diag_pairs(w_i), b_i, lam)
    om = _mem_attention(z, EVEN_OFF["qm"][0], EVEN_OFF["gm"][0], mem, mem_g, w_mem_kv.astype(BF16), m_qn, m_kn)
    out = _outproj(h2, oa.reshape(b * s, 512), ob.reshape(b * s, 512), om.reshape(b * s, 256), w_out.astype(BF16))
    return out.reshape(b, s, d)


def _odd_layer(h, mem, g, mem_g, w_mem_kv, m_qn, m_kn, w_in, w_out, c_lb, c_og,
               d_qn, d_kn_cmp, d_kn_slc, d_kn_win, pe_k, pe_v, w1k, w2k, w1v, w2v, cos, sin_s, cosc, sinc):
    b, s, d = h.shape
    h2 = h.reshape(b * s, d)
    z = _inproj(h2, g, _permute_cols(w_in, ODD_ORDER, ODD_SRC)).reshape(b, s, ODD_COLS)
    oc = _hgrn(z, c_lb, c_og)
    nseg = s // CMP_STRIDE
    seg = lambda name: z[:, :, ODD_OFF[name][0]:ODD_OFF[name][0] + 128].reshape(b, nseg, CMP_STRIDE * 128)
    kc, vc, ks, vs, kw, vw = _nsa_prep(
        z, seg("kcd"), seg("vcd"), cos, sin_s, cosc, sinc, d_kn_cmp, d_kn_slc, d_kn_win,
        _expand_pe(pe_k), _expand_pe(pe_v), _expand_compress_w1(w1k), _expand_compress_w2(w2k),
        _expand_compress_w1(w1v), _expand_compress_w2(w2v))
    od = _nsa(z, cos, sin_s, d_qn, kc, vc, ks, vs, kw, vw)
    om = _mem_attention(z, ODD_OFF["qm"][0], ODD_OFF["gm"][0], mem, mem_g, w_mem_kv.astype(BF16), m_qn, m_kn)
    out = _outproj(h2, oc.reshape(b * s, 512), od.reshape(b * s, 512), om.reshape(b * s, 256), w_out.astype(BF16))
    return out.reshape(b, s, d)


def kernel(x, mem, norm_g, mem_norm_g, mem_w_kv, mem_qn, mem_kn, ev_w_in, ev_w_out, a_qn, a_kn, a_sinks,
           b_conv_w, b_conv_b, b_w_r, b_b_r, b_w_i, b_b_i, b_lambda, od_w_in, od_w_out, c_lb, c_onorm,
           d_qn, d_kn_cmp, d_kn_slc, d_kn_win, d_pe_k, d_pe_v, d_w1k, d_w2k, d_w1v, d_w2v):
    depth = norm_g.shape[0]
    assert depth == 2 and c_lb.shape[0] == 2, "the HGRN2 lower-bound formula in the kernel is written for depth 2"
    s = x.shape[1]
    assert s % 256 == 0 and s >= D_WINDOW
    pos = jnp.arange(s)
    cos, sin_s = _rope_tables(pos)
    nseg = s // CMP_STRIDE
    cmp_end = jnp.minimum(jnp.arange(nseg) * CMP_STRIDE + CMP_LEN - 1, s - 1)
    cosc, sinc = _rope_tables(cmp_end)
    h = _even_layer(x, mem, norm_g[0], mem_norm_g[0], mem_w_kv[0], mem_qn[0], mem_kn[0], ev_w_in[0], ev_w_out[0],
                    a_qn[0], a_kn[0], a_sinks[0], b_conv_w[0], b_conv_b[0], b_w_r[0], b_b_r[0], b_w_i[0], b_b_i[0],
                    b_lambda[0], cos, sin_s)
    h = _odd_layer(h, mem, norm_g[1], mem_norm_g[1], mem_w_kv[1], mem_qn[1], mem_kn[1], od_w_in[0], od_w_out[0],
                   c_lb, c_onorm[0], d_qn[0], d_kn_cmp[0], d_kn_slc[0], d_kn_win[0], d_pe_k[0], d_pe_v[0],
                   d_w1k[0], d_w2k[0], d_w1v[0], d_w2v[0], cos, sin_s, cosc, sinc)
    return h
```

```python
import numpy as np
import jax
import jax.numpy as jnp
from jax import lax
from jax.experimental import pallas as pl
from jax.experimental.pallas import tpu as pltpu

F32 = jnp.float32
BF16 = jnp.bfloat16

D_MODEL = 1024
N_MEM = 256
HEAD_DIM = 64
ROPE_THETA = 10000.0
EPS = 1e-6
NEG_INF = -1e30
POS_INF = 1e30
LANES = 128

A_HEADS, A_KV_HEADS, A_WINDOW = 8, 2, 128
B_WIDTH, B_BLOCKS, B_CONV, B_C = 512, 8, 4, 8.0
M_HEADS = 4
C_HEADS, C_HEAD_DIM, C_CHUNK = 4, 128, 64
D_HEADS, D_KV_HEADS = 8, 2
CMP_LEN, CMP_STRIDE, CMP_HIDDEN = 32, 16, 128
SEL_LEN, SEL_TOPK = 64, 4
D_WINDOW = 512
D_BRANCHES = 3
SCALE = HEAD_DIM ** -0.5

EVEN_ORDER = [("qa", 512), ("ga", 512), ("xb", 512), ("gb", 512), ("qm", 256), ("gm", 256), ("ka", 128), ("va", 128)]
EVEN_SRC = {"qa": (0, 512), "ka": (512, 128), "va": (640, 128), "ga": (768, 512), "xb": (1280, 512),
            "gb": (1792, 512), "qm": (2304, 256), "gm": (2560, 256)}
ODD_ORDER = [("qc", 512), ("fc", 512), ("ic", 512), ("gc", 512), ("qd", 512), ("gd", 512), ("qm", 256), ("gm", 256),
             ("kcd", 128), ("vcd", 128), ("ksd", 128), ("vsd", 128), ("kwd", 128), ("vwd", 128), ("gate", 128)]
ODD_SRC = {"qc": (0, 512), "fc": (512, 512), "ic": (1024, 512), "gc": (1536, 512), "qd": (2048, 512),
           "kcd": (2560, 128), "vcd": (2688, 128), "ksd": (2816, 128), "vsd": (2944, 128), "kwd": (3072, 128),
           "vwd": (3200, 128), "gate": (3328, 24), "gd": (3352, 512), "qm": (3864, 256), "gm": (4120, 256)}

VMEM_LIMIT = 48 * 1024 * 1024


def _offsets(order):
    off, out = 0, {}
    for name, w in order:
        out[name] = (off, w)
        off += w
    return out, off


EVEN_OFF, EVEN_COLS = _offsets(EVEN_ORDER)
ODD_OFF, ODD_COLS = _offsets(ODD_ORDER)


def _cparams(sem):
    return pltpu.CompilerParams(dimension_semantics=sem, vmem_limit_bytes=VMEM_LIMIT)


def _dot(a, b):
    return jnp.dot(a, b, preferred_element_type=F32)


def _dot_nt(a, b):
    return lax.dot_general(a, b, (((1,), (1,)), ((), ())), preferred_element_type=F32)


def _split(x):
    hi = x.astype(BF16)
    lo = (x - hi.astype(F32)).astype(BF16)
    return hi, lo


def _dot_split_lhs(x, m):
    hi, lo = _split(x)
    return _dot(hi, m) + _dot(lo, m)


def _dot_split_rhs(m, x):
    hi, lo = _split(x)
    return _dot(m, hi) + _dot(m, lo)


def _lane(shape):
    return lax.broadcasted_iota(jnp.int32, shape, len(shape) - 1)


def _row(shape):
    return lax.broadcasted_iota(jnp.int32, shape, len(shape) - 2)


def _silu(x):
    return x * jax.nn.sigmoid(x)


def _seg_ones():
    r = lax.broadcasted_iota(jnp.int32, (LANES, LANES), 0) >> 6
    c = lax.broadcasted_iota(jnp.int32, (LANES, LANES), 1) >> 6
    return jnp.where(r == c, 1.0, 0.0).astype(BF16)


def _head_rms(x, gain):
    seg = _seg_ones()
    cols = []
    for c in range(x.shape[1] // LANES):
        xc = x[:, c * LANES:(c + 1) * LANES]
        ms = _dot_split_lhs(xc * xc, seg) * (1.0 / HEAD_DIM)
        cols.append(xc * lax.rsqrt(ms + EPS))
    y = cols[0] if len(cols) == 1 else jnp.concatenate(cols, axis=1)
    return y * gain


def _rope(x, cos, sin_s):
    first = (_lane((x.shape[0], LANES)) & 63) < 32
    cols = []
    for c in range(x.shape[1] // LANES):
        xc = x[:, c * LANES:(c + 1) * LANES]
        sw = jnp.where(first, pltpu.roll(xc, 96, 1), pltpu.roll(xc, 32, 1))
        cols.append(xc * cos + sw * sin_s)
    return cols[0] if len(cols) == 1 else jnp.concatenate(cols, axis=1)


def _dup_heads(k):
    sw = pltpu.roll(k, 64, 1)
    lo = _lane(k.shape) < 64
    return jnp.where(lo, k, sw), jnp.where(lo, sw, k)


def _q_heads(q):
    lo = _lane((q.shape[0], LANES)) < 64
    out = []
    for c in range(q.shape[1] // LANES):
        qc = q[:, c * LANES:(c + 1) * LANES]
        out.append(jnp.where(lo, qc, 0.0).astype(BF16))
        out.append(jnp.where(lo, 0.0, qc).astype(BF16))
    return out


def _merge_heads(o):
    lo = _lane(o[0].shape) < 64
    cols = [jnp.where(lo, o[2 * c], o[2 * c + 1]) for c in range(len(o) // 2)]
    return cols[0] if len(cols) == 1 else jnp.concatenate(cols, axis=1)


def _inproj_kernel(x_ref, g_ref, w_ref, o_ref):
    x = x_ref[...]
    ms = jnp.mean(x * x, axis=-1, keepdims=True)
    xn = (x * lax.rsqrt(ms + EPS) * g_ref[...]).astype(BF16)
    n = o_ref.shape[-1]
    for c in range(0, n, 512):
        w = min(512, n - c)
        o_ref[:, c:c + w] = _dot(xn, w_ref[:, c:c + w])


def _inproj(x2d, gain, w_bf16, tm=256):
    n, d = x2d.shape
    nc = w_bf16.shape[1]
    return pl.pallas_call(
        _inproj_kernel,
        grid=(n // tm,),
        in_specs=[pl.BlockSpec((tm, d), lambda i: (i, 0)),
                  pl.BlockSpec((1, d), lambda i: (0, 0)),
                  pl.BlockSpec((d, nc), lambda i: (0, 0))],
        out_specs=pl.BlockSpec((tm, nc), lambda i: (i, 0)),
        out_shape=jax.ShapeDtypeStruct((n, nc), F32),
        compiler_params=_cparams(("arbitrary",)),
        name="inproj",
    )(x2d, gain.reshape(1, d), w_bf16)


def _outproj_kernel(h_ref, a_ref, b_ref, m_ref, w_ref, o_ref):
    a, b, m = a_ref[...], b_ref[...], m_ref[...]
    for c in range(0, D_MODEL, 256):
        acc = h_ref[:, c:c + 256]
        acc += _dot(a, w_ref[0:512, c:c + 256])
        acc += _dot(b, w_ref[512:1024, c:c + 256])
        acc += _dot(m, w_ref[1024:1280, c:c + 256])
        o_ref[:, c:c + 256] = acc


def _outproj(h2d, oa, ob, om, w_bf16, tm=512):
    n, d = h2d.shape
    return pl.pallas_call(
        _outproj_kernel,
        grid=(n // tm,),
        in_specs=[pl.BlockSpec((tm, d), lambda i: (i, 0)),
                  pl.BlockSpec((tm, 512), lambda i: (i, 0)),
                  pl.BlockSpec((tm, 512), lambda i: (i, 0)),
                  pl.BlockSpec((tm, 256), lambda i: (i, 0)),
                  pl.BlockSpec((1280, d), lambda i: (0, 0))],
        out_specs=pl.BlockSpec((tm, d), lambda i: (i, 0)),
        out_shape=jax.ShapeDtypeStruct((n, d), F32),
        compiler_params=_cparams(("arbitrary",)),
        name="outproj",
    )(h2d, oa, ob, om, w_bf16)


def _mem_kernel(q_ref, gm_ref, mem_ref, mg_ref, wkv_ref, qn_ref, kn_ref, o_ref, kd_scr, vd_scr):
    @pl.when(pl.program_id(1) == 0)
    def _():
        m = mem_ref[0]
        ms = jnp.mean(m * m, axis=-1, keepdims=True)
        mn = (m * lax.rsqrt(ms + EPS) * mg_ref[...]).astype(BF16)
        kv = _dot(mn, wkv_ref[...])
        km = _head_rms(kv[:, 0:256], kn_ref[...])
        vm = kv[:, 256:512]
        for c in range(2):
            k0, k1 = _dup_heads(km[:, c * LANES:(c + 1) * LANES])
            v0, v1 = _dup_heads(vm[:, c * LANES:(c + 1) * LANES])
            kd_scr[2 * c] = k0.astype(BF16)
            kd_scr[2 * c + 1] = k1.astype(BF16)
            vd_scr[2 * c] = v0.astype(BF16)
            vd_scr[2 * c + 1] = v1.astype(BF16)

    q = _head_rms(q_ref[0], qn_ref[...])
    outs = []
    for h, qh in enumerate(_q_heads(q)):
        s = _dot_nt(qh, kd_scr[h]) * SCALE
        m = jnp.max(s, axis=-1, keepdims=True)
        e = jnp.exp(s - m)
        l = jnp.sum(e, axis=-1, keepdims=True)
        outs.append(_dot(e.astype(BF16), vd_scr[h]) / l)
    o_ref[0] = (_merge_heads(outs) * _silu(gm_ref[0])).astype(BF16)


def _mem_attention(z, q_off, g_off, mem, mem_g, wkv_bf16, qn, kn, tq=256):
    b, s, _ = z.shape
    return pl.pallas_call(
        _mem_kernel,
        grid=(b, s // tq),
        in_specs=[pl.BlockSpec((1, tq, 256), lambda bi, i: (bi, i, q_off // 256)),
                  pl.BlockSpec((1, tq, 256), lambda bi, i: (bi, i, g_off // 256)),
                  pl.BlockSpec((1, N_MEM, D_MODEL), lambda bi, i: (bi, 0, 0)),
                  pl.BlockSpec((1, D_MODEL), lambda bi, i: (0, 0)),
                  pl.BlockSpec((D_MODEL, 512), lambda bi, i: (0, 0)),
                  pl.BlockSpec((1, 256), lambda bi, i: (0, 0)),
                  pl.BlockSpec((1, 256), lambda bi, i: (0, 0))],
        out_specs=pl.BlockSpec((1, tq, 256), lambda bi, i: (bi, i, 0)),
        out_shape=jax.ShapeDtypeStruct((b, s, 256), BF16),
        scratch_shapes=[pltpu.VMEM((M_HEADS, N_MEM, LANES), BF16), pltpu.VMEM((M_HEADS, N_MEM, LANES), BF16)],
        compiler_params=_cparams(("arbitrary", "arbitrary")),
        name="mem_attention",
    )(z, z, mem, mem_g.reshape(1, D_MODEL), wkv_bf16, jnp.tile(qn, 4).reshape(1, 256), jnp.tile(kn, 4).reshape(1, 256))


def _swa_kernel(q_ref, g_ref, kc_ref, kp_ref, vc_ref, vp_ref, cosc_ref, sinc_ref, cosp_ref, sinp_ref,
                qn_ref, kn_ref, sink_ref, o_ref):
    i = pl.program_id(1)
    t = 128
    q = _rope(_head_rms(q_ref[0], qn_ref[...]), cosc_ref[...], sinc_ref[...])
    kc = _rope(_head_rms(kc_ref[0], kn_ref[...]), cosc_ref[...], sinc_ref[...])
    kp = _rope(_head_rms(kp_ref[0], kn_ref[...]), cosp_ref[...], sinp_ref[...])
    k_all = jnp.concatenate([kp, kc], axis=0)
    v_all = jnp.concatenate([vp_ref[0], vc_ref[0]], axis=0)
    kd = [x.astype(BF16) for x in _dup_heads(k_all)]
    vd = [x.astype(BF16) for x in _dup_heads(v_all)]
    tq = i * t + _row((t, 2 * t))
    ts = (i - 1) * t + _lane((t, 2 * t))
    rel = tq - ts
    mask = (rel >= 0) & (rel < A_WINDOW) & (ts >= 0)
    outs = []
    for h, qh in enumerate(_q_heads(q)):
        kv = h // (A_HEADS // A_KV_HEADS)
        s = jnp.where(mask, _dot_nt(qh, kd[kv]) * SCALE, NEG_INF)
        sink = sink_ref[h:h + 1, 0:1]
        m = jnp.maximum(jnp.max(s, axis=-1, keepdims=True), sink)
        e = jnp.exp(s - m)
        den = jnp.sum(e, axis=-1, keepdims=True) + jnp.exp(sink - m)
        outs.append(_dot(e.astype(BF16), vd[kv]) / den)
    o_ref[0] = (_merge_heads(outs) * _silu(g_ref[0])).astype(BF16)


def _swa(z, cos, sin_s, qn, kn, sinks):
    b, s, _ = z.shape
    t = 128
    qo, go = EVEN_OFF["qa"][0] // 512, EVEN_OFF["ga"][0] // 512
    ko, vo = EVEN_OFF["ka"][0] // 128, EVEN_OFF["va"][0] // 128
    prev = lambda i: jnp.maximum(i - 1, 0)
    return pl.pallas_call(
        _swa_kernel,
        grid=(b, s // t),
        in_specs=[pl.BlockSpec((1, t, 512), lambda bi, i: (bi, i, qo)),
                  pl.BlockSpec((1, t, 512), lambda bi, i: (bi, i, go)),
                  pl.BlockSpec((1, t, 128), lambda bi, i: (bi, i, ko)),
                  pl.BlockSpec((1, t, 128), lambda bi, i: (bi, prev(i), ko)),
                  pl.BlockSpec((1, t, 128), lambda bi, i: (bi, i, vo)),
                  pl.BlockSpec((1, t, 128), lambda bi, i: (bi, prev(i), vo)),
                  pl.BlockSpec((t, 128), lambda bi, i: (i, 0)),
                  pl.BlockSpec((t, 128), lambda bi, i: (i, 0)),
                  pl.BlockSpec((t, 128), lambda bi, i: (prev(i), 0)),
                  pl.BlockSpec((t, 128), lambda bi, i: (prev(i), 0)),
                  pl.BlockSpec((1, 512), lambda bi, i: (0, 0)),
                  pl.BlockSpec((1, 128), lambda bi, i: (0, 0)),
                  pl.BlockSpec((A_HEADS, 128), lambda bi, i: (0, 0))],
        out_specs=pl.BlockSpec((1, t, 512), lambda bi, i: (bi, i, 0)),
        out_shape=jax.ShapeDtypeStruct((b, s, 512), BF16),
        compiler_params=_cparams(("arbitrary", "arbitrary")),
        name="swa",
    )(z, z, z, z, z, z, cos, sin_s, cos, sin_s, jnp.tile(qn, 8).reshape(1, 512), jnp.tile(kn, 2).reshape(1, 128),
      jnp.broadcast_to(sinks.reshape(A_HEADS, 1), (A_HEADS, 128)))


def _rglru_kernel(x_ref, g_ref, cw_ref, cb_ref, wr_ref, br_ref, wi_ref, bi_ref, lam_ref, o_ref,
                  xbuf, hcar, a_scr, u_scr, h_scr):
    t = x_ref.shape[1]

    @pl.when(pl.program_id(1) == 0)
    def _():
        xbuf[0:8, :] = jnp.zeros((8, B_WIDTH), F32)
        hcar[...] = jnp.zeros((8, B_WIDTH), F32)

    xbuf[8:t + 8, :] = x_ref[0]
    xc = cb_ref[...] + cw_ref[0:1, :] * xbuf[5:t + 5, :]
    for j in range(1, B_CONV):
        xc = xc + cw_ref[j:j + 1, :] * xbuf[5 + j:t + 5 + j, :]
    xbuf[0:8, :] = xbuf[t:t + 8, :]

    r_cols, i_cols = [], []
    for c in range(B_WIDTH // LANES):
        xcc = xc[:, c * LANES:(c + 1) * LANES].astype(BF16)
        r_cols.append(_dot(xcc, wr_ref[c]))
        i_cols.append(_dot(xcc, wi_ref[c]))
    r = jax.nn.sigmoid(jnp.concatenate(r_cols, axis=1) + br_ref[...])
    ig = jax.nn.sigmoid(jnp.concatenate(i_cols, axis=1) + bi_ref[...])
    nl = -lam_ref[...]
    softplus = jnp.maximum(nl, 0.0) + jnp.log(1.0 + jnp.exp(-jnp.abs(nl)))
    log_a = -B_C * r * softplus
    a = jnp.exp(log_a)
    u = jnp.sqrt(1.0 - jnp.exp(2.0 * log_a)) * (ig * xc)

    r8 = _row((t, B_WIDTH)) & 7
    for d in (1, 2, 4):
        a_sh = pltpu.roll(a, d, 0)
        u_sh = pltpu.roll(u, d, 0)
        m = r8 >= d
        u = jnp.where(m, a * u_sh + u, u)
        a = jnp.where(m, a * a_sh, a)
    a_scr[...] = a
    u_scr[...] = u

    def body(j, h):
        off = pl.multiple_of(j * 8, 8)
        hh = a_scr[pl.ds(off, 8), :] * h + u_scr[pl.ds(off, 8), :]
        h_scr[pl.ds(off, 8), :] = hh
        return hh[7:8, :]

    h_last = lax.fori_loop(0, t // 8, body, hcar[0:1, :])
    hcar[0:1, :] = h_last
    o_ref[0] = (h_scr[...] * _silu(g_ref[0])).astype(BF16)


def _rglru(z, conv_w, conv_b, wr_bd, b_r, wi_bd, b_i, lam, t=256):
    b, s, _ = z.shape
    xo, go = EVEN_OFF["xb"][0] // 512, EVEN_OFF["gb"][0] // 512
    row = lambda v: v.reshape(1, B_WIDTH)
    full = lambda shp: pl.BlockSpec(shp, lambda bi, i: (0,) * len(shp))
    return pl.pallas_call(
        _rglru_kernel,
        grid=(b, s // t),
        in_specs=[pl.BlockSpec((1, t, 512), lambda bi, i: (bi, i, xo)),
                  pl.BlockSpec((1, t, 512), lambda bi, i: (bi, i, go)),
                  full((B_CONV, B_WIDTH)), full((1, B_WIDTH)),
                  full((4, LANES, LANES)), full((1, B_WIDTH)),
                  full((4, LANES, LANES)), full((1, B_WIDTH)), full((1, B_WIDTH))],
        out_specs=pl.BlockSpec((1, t, 512), lambda bi, i: (bi, i, 0)),
        out_shape=jax.ShapeDtypeStruct((b, s, 512), BF16),
        scratch_shapes=[pltpu.VMEM((t + 8, B_WIDTH), F32), pltpu.VMEM((8, B_WIDTH), F32),
                        pltpu.VMEM((t, B_WIDTH), F32), pltpu.VMEM((t, B_WIDTH), F32), pltpu.VMEM((t, B_WIDTH), F32)],
        compiler_params=_cparams(("arbitrary", "arbitrary")),
        name="rglru",
    )(z, z, conv_w, row(conv_b), wr_bd, row(b_r), wi_bd, row(b_i), row(lam))


def _hgrn_consts():
    c = C_CHUNK
    t = np.arange(c)[:, None]
    s = np.arange(c)[None, :]
    mats = [(s <= t)]
    masks = []
    hs = c // 2
    while hs >= 1:
        mid = (t // (2 * hs)) * 2 * hs + hs - 1
        mats.append(s <= mid)
        same = (t // (2 * hs)) == (s // (2 * hs))
        masks.append(same & ((t // hs) % 2 == 1) & ((s // hs) % 2 == 0))
        hs //= 2
    return np.concatenate(mats, axis=0).astype(np.float32), np.stack(masks).astype(np.float32)


def _hgrn_kernel(q_ref, f_ref, i_ref, g_ref, lb_ref, og_ref, mst_ref, msk_ref, o_ref, st_scr):
    @pl.when(pl.program_id(1) == 0)
    def _():
        st_scr[...] = jnp.zeros(st_scr.shape, F32)

    c = C_CHUNK
    nlev = msk_ref.shape[0]
    p = lb_ref[...]
    pm = jnp.maximum(p[0:1, :], p[1:2, :])
    e0, e1 = jnp.exp(p[0:1, :] - pm), jnp.exp(p[1:2, :] - pm)
    lb = e1 / (e0 + e1)
    for ch in range(q_ref.shape[1] // c):
        rows = slice(ch * c, (ch + 1) * c)
        f = lb + (1.0 - lb) * jax.nn.sigmoid(f_ref[0, rows, :])
        g = jnp.log(f)
        bst = _dot_split_rhs(mst_ref[...], g)
        kk_all = 1.0 - f
        qf_all = _silu(q_ref[0, rows, :])
        v_all = i_ref[0, rows, :]
        outs = []
        for h in range(C_HEADS):
            col = slice(h * LANES, (h + 1) * LANES)
            qf, kk, v = qf_all[:, col], kk_all[:, col], v_all[:, col]
            b = bst[0:c, col]
            st = st_scr[h]
            o = _dot_nt((qf * jnp.exp(b)).astype(BF16), st.astype(BF16))
            att = jnp.zeros((c, c), F32)
            for l in range(nlev):
                bm = bst[(l + 1) * c:(l + 2) * c, col]
                qt = (qf * jnp.exp(jnp.minimum(b - bm, 0.0))).astype(BF16)
                kt = (kk * jnp.exp(jnp.minimum(bm - b, 0.0))).astype(BF16)
                att = att + msk_ref[l] * _dot_nt(qt, kt)
            diag = jnp.sum(qf * kk, axis=-1, keepdims=True)
            vb = v.astype(BF16)
            o = o + _dot(att.astype(BF16), vb) + diag * v
            b_last = b[c - 1:c, :]
            kd = (kk * jnp.exp(b_last - b)).astype(BF16)
            st_scr[h] = st * jnp.exp(b_last) + _dot(v.T.astype(BF16), kd)
            ms = jnp.mean(o * o, axis=-1, keepdims=True)
            outs.append(o * lax.rsqrt(ms + EPS) * og_ref[...])
        o_ref[0, rows, :] = (jnp.concatenate(outs, axis=1) * _silu(g_ref[0, rows, :])).astype(BF16)


def _hgrn(z, c_lb, c_og, t=256):
    b, s, _ = z.shape
    mst, msk = _hgrn_consts()
    blk = lambda name: pl.BlockSpec((1, t, 512), lambda bi, i, o=ODD_OFF[name][0] // 512: (bi, i, o))
    full = lambda shp: pl.BlockSpec(shp, lambda bi, i: (0,) * len(shp))
    return pl.pallas_call(
        _hgrn_kernel,
        grid=(b, s // t),
        in_specs=[blk("qc"), blk("fc"), blk("ic"), blk("gc"), full(c_lb.shape), full((1, C_HEAD_DIM)),
                  full(mst.shape), full(msk.shape)],
        out_specs=pl.BlockSpec((1, t, 512), lambda bi, i: (bi, i, 0)),
        out_shape=jax.ShapeDtypeStruct((b, s, 512), BF16),
        scratch_shapes=[pltpu.VMEM((C_HEADS, C_HEAD_DIM, C_HEAD_DIM), F32)],
        compiler_params=_cparams(("arbitrary", "arbitrary")),
        name="hgrn2",
    )(z, z, z, z, c_lb, c_og.reshape(1, C_HEAD_DIM), jnp.asarray(mst, BF16), jnp.asarray(msk, F32))


def _nsa_prep_kernel(kcr_ref, vcr_ref, ks_ref, vs_ref, kw_ref, vw_ref, cos_ref, sin_ref, cosc_ref, sinc_ref,
                     kncmp_ref, knslc_ref, knwin_ref, pek_ref, pev_ref, w1k_ref, w2k_ref, w1v_ref, w2vt_ref,
                     kc_o, vct_o, ks_o, vst_o, kw_o, vwt_o):
    def hidden(xr_ref, pe_ref, w1_ref):
        xr = xr_ref[0]
        top = _dot((xr + pe_ref[0:1, :]).astype(BF16), w1_ref[0])
        bot = _dot((xr + pe_ref[1:2, :]).astype(BF16), w1_ref[1])
        pre = top + pltpu.roll(bot, bot.shape[0] - 1, 0)
        return _silu(pre).astype(BF16)

    kc = _dot(hidden(kcr_ref, pek_ref, w1k_ref), w2k_ref[...])
    kc = _rope(_head_rms(kc, kncmp_ref[...]), cosc_ref[...], sinc_ref[...])
    d0, d1 = _dup_heads(kc)
    kc_o[0, 0] = d0.astype(BF16)
    kc_o[0, 1] = d1.astype(BF16)
    vct_o[0] = _dot_nt(w2vt_ref[...], hidden(vcr_ref, pev_ref, w1v_ref)).astype(BF16)

    s = ks_ref.shape[1]
    step = 256
    for r0 in range(0, s, step):
        rows = slice(r0, r0 + step)
        cos, sin = cos_ref[rows, :], sin_ref[rows, :]
        ks = _rope(_head_rms(ks_ref[0, rows, :], knslc_ref[...]), cos, sin)
        kw = _rope(_head_rms(kw_ref[0, rows, :], knwin_ref[...]), cos, sin)
        for src, dst in ((ks, ks_o), (kw, kw_o)):
            d0, d1 = _dup_heads(src)
            dst[0, 0, rows, :] = d0.astype(BF16)
            dst[0, 1, rows, :] = d1.astype(BF16)
        vst_o[0, :, rows] = vs_ref[0, rows, :].T.astype(BF16)
        vwt_o[0, :, rows] = vw_ref[0, rows, :].T.astype(BF16)


def _nsa_prep(z, kcr, vcr, cos, sin_s, cosc, sinc, kn_cmp, kn_slc, kn_win, pek, pev, w1k, w2k, w1v, w2v):
    b, s, _ = z.shape
    nseg = s // CMP_STRIDE
    zb = lambda name: pl.BlockSpec((1, s, 128), lambda bi, o=ODD_OFF[name][0] // 128: (bi, 0, o))
    full = lambda shp: pl.BlockSpec(shp, lambda bi: (0,) * len(shp))
    seg = pl.BlockSpec((1, nseg, 2048), lambda bi: (bi, 0, 0))
    g2 = lambda g: jnp.tile(g, 2).reshape(1, 128)
    dup = lambda n: (jax.ShapeDtypeStruct((b, 2, n, 128), BF16), pl.BlockSpec((1, 2, n, 128), lambda bi: (bi, 0, 0, 0)))
    tr = lambda n: (jax.ShapeDtypeStruct((b, 128, n), BF16), pl.BlockSpec((1, 128, n), lambda bi: (bi, 0, 0)))
    outs = [dup(nseg), tr(nseg), dup(s), tr(s), dup(s), tr(s)]
    return pl.pallas_call(
        _nsa_prep_kernel,
        grid=(b,),
        in_specs=[seg, seg, zb("ksd"), zb("vsd"), zb("kwd"), zb("vwd"),
                  full((s, 128)), full((s, 128)), full((nseg, 128)), full((nseg, 128)),
                  full((1, 128)), full((1, 128)), full((1, 128)), full((2, 2048)), full((2, 2048)),
                  full((2, 2048, 256)), full((256, 128)), full((2, 2048, 256)), full((128, 256))],
        out_specs=[o[1] for o in outs],
        out_shape=[o[0] for o in outs],
        compiler_params=_cparams(("arbitrary",)),
        name="nsa_prep",
    )(kcr, vcr, z, z, z, z, cos, sin_s, cosc, sinc, g2(kn_cmp), g2(kn_slc), g2(kn_win), pek, pev, w1k, w2k, w1v, w2v)


def _tile4(x):
    return jnp.concatenate([x, x, x, x], axis=1)


def _attend_tile(state, k_tile, vt_tile, qs, bias):
    m, l, acc = state
    s = _dot_nt(k_tile, qs) + bias
    m_new = jnp.maximum(m, jnp.max(s, axis=0, keepdims=True))
    alpha = jnp.exp(m - m_new)
    p = jnp.exp(s - m_new)
    l = alpha * l + jnp.sum(p, axis=0, keepdims=True)
    acc = alpha * acc + _dot(vt_tile, p.astype(BF16))
    return m_new, l, acc


def _nsa_kernel(q_ref, gd_ref, gate_ref, cos_ref, sin_ref, qn_ref, kc_ref, vct_ref, ks_ref, vst_ref, kw_ref, vwt_ref,
                ovlt_ref, o_ref):
    i = pl.program_id(1)
    t = 128
    grp = D_HEADS // D_KV_HEADS
    w = grp * t
    q = _rope(_head_rms(q_ref[0], qn_ref[...]), cos_ref[...], sin_ref[...]) * SCALE
    qh = _q_heads(q)
    qs = [jnp.concatenate(qh[kv * grp:(kv + 1) * grp], axis=0) for kv in range(D_KV_HEADS)]
    krow = _row((t, t))
    tq = i * t + _lane((t, t))
    vt_rows = lambda kv: slice(kv * HEAD_DIM, (kv + 1) * HEAD_DIM)
    init = (jnp.full((1, w), NEG_INF, F32), jnp.zeros((1, w), F32), jnp.zeros((HEAD_DIM, w), F32))

    ncmp = kc_ref.shape[2] - 1
    nsel = ovlt_ref.shape[0]
    bias_c = _tile4(jnp.where((krow * CMP_STRIDE + (CMP_LEN - 1) <= tq) & (krow < ncmp), 0.0, NEG_INF))
    row_ok = _tile4(tq[0:1, :] >= CMP_LEN - 1)
    jrow = _row((nsel, t))
    jrow_f = jrow.astype(F32)
    cur = (i * t + _lane((nsel, t))) // SEL_LEN
    forced = (jrow == 0) | (jrow == cur)
    oc_t, sel_t = [], []
    for kv in range(D_KV_HEADS):
        s = _dot_nt(kc_ref[0, kv], qs[kv]) + bias_c
        e = jnp.exp(s - jnp.max(s, axis=0, keepdims=True))
        inv = jnp.where(row_ok, 1.0 / jnp.sum(e, axis=0, keepdims=True), 0.0)
        p = e * inv
        oc_t.append(_dot(vct_ref[0, vt_rows(kv), :], p.astype(BF16)))
        psum = p[:, 0:t] + p[:, t:2 * t] + p[:, 2 * t:3 * t] + p[:, 3 * t:4 * t]
        imp = _dot_split_rhs(ovlt_ref[...], psum)
        score = jnp.where(forced, POS_INF, jnp.where(jrow <= cur, imp, NEG_INF))
        chosen = jnp.zeros((nsel, t), F32)
        for _ in range(min(SEL_TOPK, nsel)):
            mx = jnp.max(score, axis=0, keepdims=True)
            first = jnp.min(jnp.where(score == mx, jrow_f, 1e9), axis=0, keepdims=True)
            hit = jrow_f == first
            chosen = jnp.where(hit, 1.0, chosen)
            score = jnp.where(hit, -3e38, score)
        sel_t.append(chosen.astype(BF16))

    blk_of_key = _row((t, nsel)) >> 6
    blk_lane = _lane((t, nsel))

    def sel_step(j, states):
        off = pl.multiple_of(j * t, t)
        expand = jnp.where(blk_lane == 2 * j + blk_of_key, 1.0, 0.0).astype(BF16)
        causal = (j * t + krow) <= tq
        out = []
        for kv in range(D_KV_HEADS):
            keep = (_dot(expand, sel_t[kv]) > 0.5) & causal
            bias = _tile4(jnp.where(keep, 0.0, NEG_INF))
            out.append(_attend_tile(states[kv], ks_ref[0, kv, pl.ds(off, t), :],
                                    vst_ref[0, vt_rows(kv), pl.ds(off, t)], qs[kv], bias))
        return tuple(out)

    st = sel_step(i, (init, init))
    st = lax.fori_loop(0, i, sel_step, st)
    os_t = [acc * (1.0 / l) for (_, l, acc) in st]

    st = (init, init)
    for back in range(D_WINDOW // t + 1):
        j = i - back
        off = pl.multiple_of(jnp.maximum(j, 0) * t, t)
        ts = j * t + krow
        rel = tq - ts
        bias = _tile4(jnp.where((ts >= 0) & (rel >= 0) & (rel < D_WINDOW), 0.0, NEG_INF))
        st = tuple(_attend_tile(st[kv], kw_ref[0, kv, pl.ds(off, t), :], vwt_ref[0, vt_rows(kv), pl.ds(off, t)],
                                qs[kv], bias) for kv in range(D_KV_HEADS))
    ow_t = [acc * (1.0 / l) for (_, l, acc) in st]

    g_t = jax.nn.sigmoid(gate_ref[0]).T
    rows = []
    for h in range(D_HEADS):
        kv, g = divmod(h, grp)
        cols = slice(g * t, (g + 1) * t)
        r = D_BRANCHES * h
        rows.append(g_t[r:r + 1, :] * oc_t[kv][:, cols] + g_t[r + 1:r + 2, :] * os_t[kv][:, cols]
                    + g_t[r + 2:r + 3, :] * ow_t[kv][:, cols])
    o = jnp.concatenate(rows, axis=0).T
    o_ref[0] = (o * _silu(gd_ref[0])).astype(BF16)


def _overlap_t(s):
    ncmp = (s - CMP_LEN) // CMP_STRIDE + 1
    nsel = s // SEL_LEN
    cs = np.arange(ncmp)[None, :] * CMP_STRIDE
    ss = np.arange(nsel)[:, None] * SEL_LEN
    ovl = np.zeros((nsel, s // CMP_STRIDE), np.float32)
    ovl[:, :ncmp] = (cs < ss + SEL_LEN) & (cs + CMP_LEN > ss)
    return ovl


def _nsa(z, cos, sin_s, qn, kc, vct, ks, vst, kw, vwt):
    b, s, _ = z.shape
    t = 128
    nseg = s // CMP_STRIDE
    assert nseg == LANES, "compressed-block scores are laid out on one 128-row tile"
    ovlt = _overlap_t(s)
    full = lambda shp: pl.BlockSpec(shp, lambda bi, i: (0,) * len(shp))
    dup = lambda n: pl.BlockSpec((1, 2, n, 128), lambda bi, i: (bi, 0, 0, 0))
    tr = lambda n: pl.BlockSpec((1, 128, n), lambda bi, i: (bi, 0, 0))
    return pl.pallas_call(
        _nsa_kernel,
        grid=(b, s // t),
        in_specs=[pl.BlockSpec((1, t, 512), lambda bi, i: (bi, i, ODD_OFF["qd"][0] // 512)),
                  pl.BlockSpec((1, t, 512), lambda bi, i: (bi, i, ODD_OFF["gd"][0] // 512)),
                  pl.BlockSpec((1, t, 128), lambda bi, i: (bi, i, ODD_OFF["gate"][0] // 128)),
                  pl.BlockSpec((t, 128), lambda bi, i: (i, 0)),
                  pl.BlockSpec((t, 128), lambda bi, i: (i, 0)),
                  full((1, 512)), dup(nseg), tr(nseg), dup(s), tr(s), dup(s), tr(s), full(ovlt.shape)],
        out_specs=pl.BlockSpec((1, t, 512), lambda bi, i: (bi, i, 0)),
        out_shape=jax.ShapeDtypeStruct((b, s, 512), BF16),
        compiler_params=_cparams(("arbitrary", "arbitrary")),
        name="nsa",
    )(z, z, z, cos, sin_s, jnp.tile(qn, 8).reshape(1, 512), kc, vct, ks, vst, kw, vwt, jnp.asarray(ovlt, BF16))


def _permute_cols(w, order, src):
    cols = []
    for name, width in order:
        o, sw = src[name]
        blk = w[:, o:o + sw]
        if sw < width:
            blk = jnp.pad(blk, ((0, 0), (0, width - sw)))
        cols.append(blk)
    return jnp.concatenate(cols, axis=1).astype(BF16)


def _block_diag_pairs(w):
    z = jnp.zeros((4, LANES, LANES), w.dtype)
    z = z.at[:, 0:64, 0:64].set(w[0::2])
    z = z.at[:, 64:128, 64:128].set(w[1::2])
    return z.astype(BF16)


def _rope_tables(pos):
    half = HEAD_DIM // 2
    inv = ROPE_THETA ** (-jnp.arange(half, dtype=F32) / half)
    ang = pos.astype(F32)[:, None] * inv[None, :]
    cos, sin = jnp.cos(ang), jnp.sin(ang)
    cos_t = jnp.tile(cos, (1, 4))
    sin_t = jnp.tile(jnp.concatenate([-sin, sin], axis=1), (1, 2))
    return cos_t, sin_t


def _expand_compress_w1(w1):
    hdim = w1.shape[1]
    w = w1.reshape(2, CMP_STRIDE, HEAD_DIM, hdim)
    z = jnp.zeros((2, CMP_STRIDE, 2, HEAD_DIM, 2, hdim), w1.dtype)
    z = z.at[:, :, 0, :, 0, :].set(w)
    z = z.at[:, :, 1, :, 1, :].set(w)
    return z.reshape(2, CMP_STRIDE * 2 * HEAD_DIM, 2 * hdim).astype(BF16)


def _expand_compress_w2(w2):
    hdim, hd = w2.shape
    z = jnp.zeros((2 * hdim, 2 * hd), w2.dtype)
    z = z.at[0:hdim, 0:hd].set(w2)
    z = z.at[hdim:, hd:].set(w2)
    return z.astype(BF16)


def _expand_pe(pe):
    p = pe.reshape(2, CMP_STRIDE, 1, HEAD_DIM)
    return jnp.broadcast_to(p, (2, CMP_STRIDE, 2, HEAD_DIM)).reshape(2, CMP_STRIDE * 2 * HEAD_DIM)


def _even_layer(h, mem, g, mem_g, w_mem_kv, m_qn, m_kn, w_in, w_out, a_qn, a_kn, a_sinks,
                conv_w, conv_b, w_r, b_r, w_i, b_i, lam, cos, sin_s):
    b, s, d = h.shape
    h2 = h.reshape(b * s, d)
    z = _inproj(h2, g, _permute_cols(w_in, EVEN_ORDER, EVEN_SRC)).reshape(b, s, EVEN_COLS)
    oa = _swa(z, cos, sin_s, a_qn, a_kn, a_sinks)
    ob = _rglru(z, conv_w, conv_b, _block_diag_pairs(w_r), b_r, _block_diag_pairs(w_i), b_i, lam)
    om = _mem_attention(z, EVEN_OFF["qm"][0], EVEN_OFF["gm"][0], mem, mem_g, w_mem_kv.astype(BF16), m_qn, m_kn)
    out = _outproj(h2, oa.reshape(b * s, 512), ob.reshape(b * s, 512), om.reshape(b * s, 256), w_out.astype(BF16))
    return out.reshape(b, s, d)


def _odd_layer(h, mem, g, mem_g, w_mem_kv, m_qn, m_kn, w_in, w_out, c_lb, c_og,
               d_qn, d_kn_cmp, d_kn_slc, d_kn_win, pe_k, pe_v, w1k, w2k, w1v, w2v, cos, sin_s, cosc, sinc):
    b, s, d = h.shape
    h2 = h.reshape(b * s, d)
    z = _inproj(h2, g, _permute_cols(w_in, ODD_ORDER, ODD_SRC)).reshape(b, s, ODD_COLS)
    oc = _hgrn(z, c_lb, c_og)
    nseg = s // CMP_STRIDE
    seg = lambda name: z[:, :, ODD_OFF[name][0]:ODD_OFF[name][0] + 128].reshape(b, nseg, CMP_STRIDE * 128)
    kc, vc, ks, vs, kw, vw = _nsa_prep(
        z, seg("kcd"), seg("vcd"), cos, sin_s, cosc, sinc, d_kn_cmp, d_kn_slc, d_kn_win,
        _expand_pe(pe_k), _expand_pe(pe_v), _expand_compress_w1(w1k), _expand_compress_w2(w2k),
        _expand_compress_w1(w1v), _expand_compress_w2(w2v).T)
    od = _nsa(z, cos, sin_s, d_qn, kc, vc, ks, vs, kw, vw)
    om = _mem_attention(z, ODD_OFF["qm"][0], ODD_OFF["gm"][0], mem, mem_g, w_mem_kv.astype(BF16), m_qn, m_kn)
    out = _outproj(h2, oc.reshape(b * s, 512), od.reshape(b * s, 512), om.reshape(b * s, 256), w_out.astype(BF16))
    return out.reshape(b, s, d)


def kernel(x, mem, norm_g, mem_norm_g, mem_w_kv, mem_qn, mem_kn, ev_w_in, ev_w_out, a_qn, a_kn, a_sinks,
           b_conv_w, b_conv_b, b_w_r, b_b_r, b_w_i, b_b_i, b_lambda, od_w_in, od_w_out, c_lb, c_onorm,
           d_qn, d_kn_cmp, d_kn_slc, d_kn_win, d_pe_k, d_pe_v, d_w1k, d_w2k, d_w1v, d_w2v):
    depth = norm_g.shape[0]
    assert depth == 2 and c_lb.shape[0] == 2, "the HGRN2 lower-bound formula in the kernel is written for depth 2"
    s = x.shape[1]
    assert s % 256 == 0 and s >= D_WINDOW
    pos = jnp.arange(s)
    cos, sin_s = _rope_tables(pos)
    nseg = s // CMP_STRIDE
    cmp_end = jnp.minimum(jnp.arange(nseg) * CMP_STRIDE + CMP_LEN - 1, s - 1)
    cosc, sinc = _rope_tables(cmp_end)
    h = _even_layer(x, mem, norm_g[0], mem_norm_g[0], mem_w_kv[0], mem_qn[0], mem_kn[0], ev_w_in[0], ev_w_out[0],
                    a_qn[0], a_kn[0], a_sinks[0], b_conv_w[0], b_conv_b[0], b_w_r[0], b_b_r[0], b_w_i[0], b_b_i[0],
                    b_lambda[0], cos, sin_s)
    h = _odd_layer(h, mem, norm_g[1], mem_norm_g[1], mem_w_kv[1], mem_qn[1], mem_kn[1], od_w_in[0], od_w_out[0],
                   c_lb, c_onorm[0], d_qn[0], d_kn_cmp[0], d_kn_slc[0], d_kn_win[0], d_pe_k[0], d_pe_v[0],
                   d_w1k[0], d_w2k[0], d_w1v[0], d_w2v[0], cos, sin_s, cosc, sinc)
    return h
```

```python
import numpy as np
import jax
import jax.numpy as jnp
from jax import lax
from jax.experimental import pallas as pl
from jax.experimental.pallas import tpu as pltpu

F32 = jnp.float32
BF16 = jnp.bfloat16

D_MODEL = 1024
N_MEM = 256
HEAD_DIM = 64
ROPE_THETA = 10000.0
EPS = 1e-6
NEG_INF = -1e30
POS_INF = 1e30
MASK_BIG = 1e30
LANES = 128

A_HEADS, A_KV_HEADS, A_WINDOW = 8, 2, 128
B_WIDTH, B_BLOCKS, B_CONV, B_C = 512, 8, 4, 8.0
M_HEADS = 4
C_HEADS, C_HEAD_DIM, C_CHUNK = 4, 128, 64
D_HEADS, D_KV_HEADS = 8, 2
CMP_LEN, CMP_STRIDE, CMP_HIDDEN = 32, 16, 128
SEL_LEN, SEL_TOPK = 64, 4
D_WINDOW = 512
D_BRANCHES = 3
SCALE = HEAD_DIM ** -0.5

EVEN_ORDER = [("qa", 512), ("ga", 512), ("xb", 512), ("gb", 512), ("qm", 256), ("gm", 256), ("ka", 128), ("va", 128)]
EVEN_SRC = {"qa": (0, 512), "ka": (512, 128), "va": (640, 128), "ga": (768, 512), "xb": (1280, 512),
            "gb": (1792, 512), "qm": (2304, 256), "gm": (2560, 256)}
ODD_ORDER = [("qc", 512), ("fc", 512), ("ic", 512), ("gc", 512), ("qd", 512), ("gd", 512), ("qm", 256), ("gm", 256),
             ("kcd", 128), ("vcd", 128), ("ksd", 128), ("vsd", 128), ("kwd", 128), ("vwd", 128), ("gate", 128)]
ODD_SRC = {"qc": (0, 512), "fc": (512, 512), "ic": (1024, 512), "gc": (1536, 512), "qd": (2048, 512),
           "kcd": (2560, 128), "vcd": (2688, 128), "ksd": (2816, 128), "vsd": (2944, 128), "kwd": (3072, 128),
           "vwd": (3200, 128), "gate": (3328, 24), "gd": (3352, 512), "qm": (3864, 256), "gm": (4120, 256)}

VMEM_LIMIT = 48 * 1024 * 1024


def _offsets(order):
    off, out = 0, {}
    for name, w in order:
        out[name] = (off, w)
        off += w
    return out, off


EVEN_OFF, EVEN_COLS = _offsets(EVEN_ORDER)
ODD_OFF, ODD_COLS = _offsets(ODD_ORDER)


def _cparams(sem):
    return pltpu.CompilerParams(dimension_semantics=sem, vmem_limit_bytes=VMEM_LIMIT)


def _dot(a, b):
    return jnp.dot(a, b, preferred_element_type=F32)


def _dot_nt(a, b):
    return lax.dot_general(a, b, (((1,), (1,)), ((), ())), preferred_element_type=F32)


def _split(x):
    hi = x.astype(BF16)
    lo = (x - hi.astype(F32)).astype(BF16)
    return hi, lo


def _dot_split_lhs(x, m):
    hi, lo = _split(x)
    return _dot(hi, m) + _dot(lo, m)


def _dot_split_rhs(m, x):
    hi, lo = _split(x)
    return _dot(m, hi) + _dot(m, lo)


def _lane(shape):
    return lax.broadcasted_iota(jnp.int32, shape, len(shape) - 1)


def _row(shape):
    return lax.broadcasted_iota(jnp.int32, shape, len(shape) - 2)


def _silu(x):
    return x * jax.nn.sigmoid(x)


def _seg_ones():
    r = lax.broadcasted_iota(jnp.int32, (LANES, LANES), 0) >> 6
    c = lax.broadcasted_iota(jnp.int32, (LANES, LANES), 1) >> 6
    return jnp.where(r == c, 1.0, 0.0).astype(BF16)


def _head_rms(x, gain):
    seg = _seg_ones()
    cols = []
    for c in range(x.shape[1] // LANES):
        xc = x[:, c * LANES:(c + 1) * LANES]
        ms = _dot_split_lhs(xc * xc, seg) * (1.0 / HEAD_DIM)
        cols.append(xc * lax.rsqrt(ms + EPS))
    y = cols[0] if len(cols) == 1 else jnp.concatenate(cols, axis=1)
    return y * gain


def _rope(x, cos, sin_s):
    first = (_lane((x.shape[0], LANES)) & 63) < 32
    cols = []
    for c in range(x.shape[1] // LANES):
        xc = x[:, c * LANES:(c + 1) * LANES]
        sw = jnp.where(first, pltpu.roll(xc, 96, 1), pltpu.roll(xc, 32, 1))
        cols.append(xc * cos + sw * sin_s)
    return cols[0] if len(cols) == 1 else jnp.concatenate(cols, axis=1)


def _split_kv(k, fill=0.0):
    lo = _lane(k.shape) < 64
    return jnp.where(lo, k, fill), jnp.where(lo, pltpu.roll(k, 64, 1), fill)


def _q_heads(q, fills=None):
    lo = _lane((q.shape[0], LANES)) < 64
    out = []
    for c in range(q.shape[1] // LANES):
        qc = q[:, c * LANES:(c + 1) * LANES]
        fill = 0.0 if fills is None else fills[c]
        out.append(jnp.where(lo, qc, fill).astype(BF16))
        out.append(jnp.where(lo, pltpu.roll(qc, 64, 1), fill).astype(BF16))
    return out


def _softmax_cols(s, extra=None):
    m = jnp.max(s, axis=0, keepdims=True)
    if extra is not None:
        m = jnp.maximum(m, extra)
    p = jnp.exp(s - m)
    den = jnp.sum(p, axis=0, keepdims=True)
    if extra is not None:
        den = den + jnp.exp(extra - m)
    return p, 1.0 / den


def _online_update(state, s, vt_tile):
    m, l, acc = state
    m_new = jnp.maximum(m, jnp.max(s, axis=0, keepdims=True))
    alpha = jnp.exp(m - m_new)
    p = jnp.exp(s - m_new)
    l = alpha * l + jnp.sum(p, axis=0, keepdims=True)
    acc = alpha * acc + _dot(vt_tile, p.astype(BF16))
    return m_new, l, acc


def _tile4(x):
    return jnp.concatenate([x, x, x, x], axis=1)


def _inproj_kernel(x_ref, g_ref, w_ref, o_ref):
    x = x_ref[...]
    ms = jnp.mean(x * x, axis=-1, keepdims=True)
    xn = (x * lax.rsqrt(ms + EPS) * g_ref[...]).astype(BF16)
    n = o_ref.shape[-1]
    for c in range(0, n, 512):
        w = min(512, n - c)
        o_ref[:, c:c + w] = _dot(xn, w_ref[:, c:c + w])


def _inproj(x2d, gain, w_bf16, tm=256):
    n, d = x2d.shape
    nc = w_bf16.shape[1]
    return pl.pallas_call(
        _inproj_kernel,
        grid=(n // tm,),
        in_specs=[pl.BlockSpec((tm, d), lambda i: (i, 0)),
                  pl.BlockSpec((1, d), lambda i: (0, 0)),
                  pl.BlockSpec((d, nc), lambda i: (0, 0))],
        out_specs=pl.BlockSpec((tm, nc), lambda i: (i, 0)),
        out_shape=jax.ShapeDtypeStruct((n, nc), F32),
        compiler_params=_cparams(("arbitrary",)),
        name="inproj",
    )(x2d, gain.reshape(1, d), w_bf16)


def _outproj_kernel(h_ref, a_ref, b_ref, m_ref, w_ref, o_ref):
    a, b, m = a_ref[...], b_ref[...], m_ref[...]
    for c in range(0, D_MODEL, 256):
        acc = h_ref[:, c:c + 256]
        acc += _dot(a, w_ref[0:512, c:c + 256])
        acc += _dot(b, w_ref[512:1024, c:c + 256])
        acc += _dot(m, w_ref[1024:1280, c:c + 256])
        o_ref[:, c:c + 256] = acc


def _outproj(h2d, oa, ob, om, w_bf16, tm=512):
    n, d = h2d.shape
    return pl.pallas_call(
        _outproj_kernel,
        grid=(n // tm,),
        in_specs=[pl.BlockSpec((tm, d), lambda i: (i, 0)),
                  pl.BlockSpec((tm, 512), lambda i: (i, 0)),
                  pl.BlockSpec((tm, 512), lambda i: (i, 0)),
                  pl.BlockSpec((tm, 256), lambda i: (i, 0)),
                  pl.BlockSpec((1280, d), lambda i: (0, 0))],
        out_specs=pl.BlockSpec((tm, d), lambda i: (i, 0)),
        out_shape=jax.ShapeDtypeStruct((n, d), F32),
        compiler_params=_cparams(("arbitrary",)),
        name="outproj",
    )(h2d, oa, ob, om, w_bf16)


def _mem_kernel(q_ref, gm_ref, mem_ref, mg_ref, wkv_ref, qn_ref, kn_ref, o_ref, k_scr, vt_scr):
    @pl.when(pl.program_id(1) == 0)
    def _():
        m = mem_ref[0]
        ms = jnp.mean(m * m, axis=-1, keepdims=True)
        mn = (m * lax.rsqrt(ms + EPS) * mg_ref[...]).astype(BF16)
        kv = _dot(mn, wkv_ref[...])
        km = _head_rms(kv[:, 0:256], kn_ref[...])
        for c in range(2):
            k0, k1 = _split_kv(km[:, c * LANES:(c + 1) * LANES])
            k_scr[2 * c] = k0.astype(BF16)
            k_scr[2 * c + 1] = k1.astype(BF16)
        vt_scr[...] = kv[:, 256:512].T.astype(BF16)

    q = _head_rms(q_ref[0], qn_ref[...]) * SCALE
    scores = [_dot_nt(k_scr[h], qh) for h, qh in enumerate(_q_heads(q))]
    rows = []
    for h, sc in enumerate(scores):
        p, inv = _softmax_cols(sc)
        rows.append(_dot(vt_scr[h * HEAD_DIM:(h + 1) * HEAD_DIM, :], p.astype(BF16)) * inv)
    o = jnp.concatenate(rows, axis=0).T
    o_ref[0] = (o * _silu(gm_ref[0])).astype(BF16)


def _mem_attention(z, q_off, g_off, mem, mem_g, wkv_bf16, qn, kn, tq=256):
    b, s, _ = z.shape
    return pl.pallas_call(
        _mem_kernel,
        grid=(b, s // tq),
        in_specs=[pl.BlockSpec((1, tq, 256), lambda bi, i: (bi, i, q_off // 256)),
                  pl.BlockSpec((1, tq, 256), lambda bi, i: (bi, i, g_off // 256)),
                  pl.BlockSpec((1, N_MEM, D_MODEL), lambda bi, i: (bi, 0, 0)),
                  pl.BlockSpec((1, D_MODEL), lambda bi, i: (0, 0)),
                  pl.BlockSpec((D_MODEL, 512), lambda bi, i: (0, 0)),
                  pl.BlockSpec((1, 256), lambda bi, i: (0, 0)),
                  pl.BlockSpec((1, 256), lambda bi, i: (0, 0))],
        out_specs=pl.BlockSpec((1, tq, 256), lambda bi, i: (bi, i, 0)),
        out_shape=jax.ShapeDtypeStruct((b, s, 256), BF16),
        scratch_shapes=[pltpu.VMEM((M_HEADS, N_MEM, LANES), BF16), pltpu.VMEM((M_HEADS * HEAD_DIM, N_MEM), BF16)],
        compiler_params=_cparams(("arbitrary", "arbitrary")),
        name="mem_attention",
    )(z, z, mem, mem_g.reshape(1, D_MODEL), wkv_bf16, jnp.tile(qn, 4).reshape(1, 256), jnp.tile(kn, 4).reshape(1, 256))


def _swa_kernel(q_ref, g_ref, kc_ref, kp_ref, vc_ref, vp_ref, cosc_ref, sinc_ref, cosp_ref, sinp_ref,
                qn_ref, kn_ref, sink_ref, o_ref):
    i = pl.program_id(1)
    t = 128
    grp = A_HEADS // A_KV_HEADS
    q = _rope(_head_rms(q_ref[0], qn_ref[...]), cosc_ref[...], sinc_ref[...]) * SCALE
    qh = _q_heads(q)
    kc = _rope(_head_rms(kc_ref[0], kn_ref[...]), cosc_ref[...], sinc_ref[...])
    kp = _rope(_head_rms(kp_ref[0], kn_ref[...]), cosp_ref[...], sinp_ref[...])
    ks = [x.astype(BF16) for x in _split_kv(jnp.concatenate([kp, kc], axis=0))]
    vt = jnp.concatenate([vp_ref[0].T, vc_ref[0].T], axis=1).astype(BF16)
    ts = (i - 1) * t + _row((2 * t, t))
    rel = i * t + _lane((2 * t, t)) - ts
    bias = _tile4(jnp.where((rel >= 0) & (rel < A_WINDOW) & (ts >= 0), 0.0, NEG_INF))
    scores = [_dot_nt(ks[kv], jnp.concatenate(qh[kv * grp:(kv + 1) * grp], axis=0)) for kv in range(A_KV_HEADS)]
    rows = []
    for kv in range(A_KV_HEADS):
        sink = jnp.concatenate([sink_ref[kv * grp + g:kv * grp + g + 1, :] for g in range(grp)], axis=1)
        p, inv = _softmax_cols(scores[kv] + bias, sink)
        o_t = _dot(vt[kv * HEAD_DIM:(kv + 1) * HEAD_DIM, :], p.astype(BF16)) * inv
        rows.extend(o_t[:, g * t:(g + 1) * t] for g in range(grp))
    o = jnp.concatenate(rows, axis=0).T
    o_ref[0] = (o * _silu(g_ref[0])).astype(BF16)


def _swa(z, cos, sin_s, qn, kn, sinks):
    b, s, _ = z.shape
    t = 128
    qo, go = EVEN_OFF["qa"][0] // 512, EVEN_OFF["ga"][0] // 512
    ko, vo = EVEN_OFF["ka"][0] // 128, EVEN_OFF["va"][0] // 128
    prev = lambda i: jnp.maximum(i - 1, 0)
    return pl.pallas_call(
        _swa_kernel,
        grid=(b, s // t),
        in_specs=[pl.BlockSpec((1, t, 512), lambda bi, i: (bi, i, qo)),
                  pl.BlockSpec((1, t, 512), lambda bi, i: (bi, i, go)),
                  pl.BlockSpec((1, t, 128), lambda bi, i: (bi, i, ko)),
                  pl.BlockSpec((1, t, 128), lambda bi, i: (bi, prev(i), ko)),
                  pl.BlockSpec((1, t, 128), lambda bi, i: (bi, i, vo)),
                  pl.BlockSpec((1, t, 128), lambda bi, i: (bi, prev(i), vo)),
                  pl.BlockSpec((t, 128), lambda bi, i: (i, 0)),
                  pl.BlockSpec((t, 128), lambda bi, i: (i, 0)),
                  pl.BlockSpec((t, 128), lambda bi, i: (prev(i), 0)),
                  pl.BlockSpec((t, 128), lambda bi, i: (prev(i), 0)),
                  pl.BlockSpec((1, 512), lambda bi, i: (0, 0)),
                  pl.BlockSpec((1, 128), lambda bi, i: (0, 0)),
                  pl.BlockSpec((A_HEADS, 128), lambda bi, i: (0, 0))],
        out_specs=pl.BlockSpec((1, t, 512), lambda bi, i: (bi, i, 0)),
        out_shape=jax.ShapeDtypeStruct((b, s, 512), BF16),
        compiler_params=_cparams(("arbitrary", "arbitrary")),
        name="swa",
    )(z, z, z, z, z, z, cos, sin_s, cos, sin_s, jnp.tile(qn, 8).reshape(1, 512), jnp.tile(kn, 2).reshape(1, 128),
      jnp.broadcast_to(sinks.reshape(A_HEADS, 1), (A_HEADS, 128)))


def _rglru_kernel(x_ref, g_ref, cw_ref, cb_ref, wr_ref, br_ref, wi_ref, bi_ref, lam_ref, o_ref,
                  xbuf, hcar, a_scr, u_scr, h_scr):
    t = x_ref.shape[1]

    @pl.when(pl.program_id(1) == 0)
    def _():
        xbuf[0:8, :] = jnp.zeros((8, B_WIDTH), F32)
        hcar[...] = jnp.zeros((8, B_WIDTH), F32)

    xbuf[8:t + 8, :] = x_ref[0]
    xc = cb_ref[...] + cw_ref[0:1, :] * xbuf[5:t + 5, :]
    for j in range(1, B_CONV):
        xc = xc + cw_ref[j:j + 1, :] * xbuf[5 + j:t + 5 + j, :]
    xbuf[0:8, :] = xbuf[t:t + 8, :]

    r_cols, i_cols = [], []
    for c in range(B_WIDTH // LANES):
        xcc = xc[:, c * LANES:(c + 1) * LANES].astype(BF16)
        r_cols.append(_dot(xcc, wr_ref[c]))
        i_cols.append(_dot(xcc, wi_ref[c]))
    r = jax.nn.sigmoid(jnp.concatenate(r_cols, axis=1) + br_ref[...])
    ig = jax.nn.sigmoid(jnp.concatenate(i_cols, axis=1) + bi_ref[...])
    nl = -lam_ref[...]
    softplus = jnp.maximum(nl, 0.0) + jnp.log(1.0 + jnp.exp(-jnp.abs(nl)))
    log_a = -B_C * r * softplus
    a = jnp.exp(log_a)
    u = jnp.sqrt(1.0 - jnp.exp(2.0 * log_a)) * (ig * xc)

    r8 = _row((t, B_WIDTH)) & 7
    for d in (1, 2, 4):
        a_sh = pltpu.roll(a, d, 0)
        u_sh = pltpu.roll(u, d, 0)
        m = r8 >= d
        u = jnp.where(m, a * u_sh + u, u)
        a = jnp.where(m, a * a_sh, a)
    a_scr[...] = a
    u_scr[...] = u

    def body(j, h):
        off = pl.multiple_of(j * 8, 8)
        hh = a_scr[pl.ds(off, 8), :] * h + u_scr[pl.ds(off, 8), :]
        h_scr[pl.ds(off, 8), :] = hh
        return hh[7:8, :]

    h_last = lax.fori_loop(0, t // 8, body, hcar[0:1, :])
    hcar[0:1, :] = h_last
    o_ref[0] = (h_scr[...] * _silu(g_ref[0])).astype(BF16)


def _rglru(z, conv_w, conv_b, wr_bd, b_r, wi_bd, b_i, lam, t=256):
    b, s, _ = z.shape
    xo, go = EVEN_OFF["xb"][0] // 512, EVEN_OFF["gb"][0] // 512
    row = lambda v: v.reshape(1, B_WIDTH)
    full = lambda shp: pl.BlockSpec(shp, lambda bi, i: (0,) * len(shp))
    return pl.pallas_call(
        _rglru_kernel,
        grid=(b, s // t),
        in_specs=[pl.BlockSpec((1, t, 512), lambda bi, i: (bi, i, xo)),
                  pl.BlockSpec((1, t, 512), lambda bi, i: (bi, i, go)),
                  full((B_CONV, B_WIDTH)), full((1, B_WIDTH)),
                  full((4, LANES, LANES)), full((1, B_WIDTH)),
                  full((4, LANES, LANES)), full((1, B_WIDTH)), full((1, B_WIDTH))],
        out_specs=pl.BlockSpec((1, t, 512), lambda bi, i: (bi, i, 0)),
        out_shape=jax.ShapeDtypeStruct((b, s, 512), BF16),
        scratch_shapes=[pltpu.VMEM((t + 8, B_WIDTH), F32), pltpu.VMEM((8, B_WIDTH), F32),
                        pltpu.VMEM((t, B_WIDTH), F32), pltpu.VMEM((t, B_WIDTH), F32), pltpu.VMEM((t, B_WIDTH), F32)],
        compiler_params=_cparams(("arbitrary", "arbitrary")),
        name="rglru",
    )(z, z, conv_w, row(conv_b), wr_bd, row(b_r), wi_bd, row(b_i), row(lam))


def _hgrn_consts():
    c = C_CHUNK
    t = np.arange(c)[:, None]
    s = np.arange(c)[None, :]
    mats = [(s <= t)]
    masks = []
    hs = c // 2
    while hs >= 1:
        mid = (t // (2 * hs)) * 2 * hs + hs - 1
        mats.append(s <= mid)
        same = (t // (2 * hs)) == (s // (2 * hs))
        masks.append(same & ((t // hs) % 2 == 1) & ((s // hs) % 2 == 0))
        hs //= 2
    return np.concatenate(mats, axis=0).astype(np.float32), np.stack(masks).astype(np.float32)


def _hgrn_kernel(q_ref, f_ref, i_ref, g_ref, lb_ref, og_ref, mst_ref, msk_ref, o_ref, st_scr):
    @pl.when(pl.program_id(1) == 0)
    def _():
        st_scr[...] = jnp.zeros(st_scr.shape, F32)

    c = C_CHUNK
    nlev = msk_ref.shape[0]
    p = lb_ref[...]
    pm = jnp.maximum(p[0:1, :], p[1:2, :])
    e0, e1 = jnp.exp(p[0:1, :] - pm), jnp.exp(p[1:2, :] - pm)
    lb = e1 / (e0 + e1)
    for ch in range(q_ref.shape[1] // c):
        rows = slice(ch * c, (ch + 1) * c)
        f = lb + (1.0 - lb) * jax.nn.sigmoid(f_ref[0, rows, :])
        g = jnp.log(f)
        bst = _dot_split_rhs(mst_ref[...], g)
        kk_all = 1.0 - f
        qf_all = _silu(q_ref[0, rows, :])
        v_all = i_ref[0, rows, :]
        outs = []
        for h in range(C_HEADS):
            col = slice(h * LANES, (h + 1) * LANES)
            qf, kk, v = qf_all[:, col], kk_all[:, col], v_all[:, col]
            b = bst[0:c, col]
            st = st_scr[h]
            o = _dot_nt((qf * jnp.exp(b)).astype(BF16), st.astype(BF16))
            att = jnp.zeros((c, c), F32)
            for l in range(nlev):
                bm = bst[(l + 1) * c:(l + 2) * c, col]
                qt = (qf * jnp.exp(jnp.minimum(b - bm, 0.0))).astype(BF16)
                kt = (kk * jnp.exp(jnp.minimum(bm - b, 0.0))).astype(BF16)
                att = att + msk_ref[l] * _dot_nt(qt, kt)
            diag = jnp.sum(qf * kk, axis=-1, keepdims=True)
            vb = v.astype(BF16)
            o = o + _dot(att.astype(BF16), vb) + diag * v
            b_last = b[c - 1:c, :]
            kd = (kk * jnp.exp(b_last - b)).astype(BF16)
            st_scr[h] = st * jnp.exp(b_last) + _dot(v.T.astype(BF16), kd)
            ms = jnp.mean(o * o, axis=-1, keepdims=True)
            outs.append(o * lax.rsqrt(ms + EPS) * og_ref[...])
        o_ref[0, rows, :] = (jnp.concatenate(outs, axis=1) * _silu(g_ref[0, rows, :])).astype(BF16)


def _hgrn(z, c_lb, c_og, t=256):
    b, s, _ = z.shape
    mst, msk = _hgrn_consts()
    blk = lambda name: pl.BlockSpec((1, t, 512), lambda bi, i, o=ODD_OFF[name][0] // 512: (bi, i, o))
    full = lambda shp: pl.BlockSpec(shp, lambda bi, i: (0,) * len(shp))
    return pl.pallas_call(
        _hgrn_kernel,
        grid=(b, s // t),
        in_specs=[blk("qc"), blk("fc"), blk("ic"), blk("gc"), full(c_lb.shape), full((1, C_HEAD_DIM)),
                  full(mst.shape), full(msk.shape)],
        out_specs=pl.BlockSpec((1, t, 512), lambda bi, i: (bi, i, 0)),
        out_shape=jax.ShapeDtypeStruct((b, s, 512), BF16),
        scratch_shapes=[pltpu.VMEM((C_HEADS, C_HEAD_DIM, C_HEAD_DIM), F32)],
        compiler_params=_cparams(("arbitrary", "arbitrary")),
        name="hgrn2",
    )(z, z, z, z, c_lb, c_og.reshape(1, C_HEAD_DIM), jnp.asarray(mst, BF16), jnp.asarray(msk, F32))


def _nsa_prep_kernel(kcr_ref, vcr_ref, ks_ref, vs_ref, kw_ref, vw_ref, cos_ref, sin_ref, cosc_ref, sinc_ref,
                     kncmp_ref, knslc_ref, knwin_ref, pek_ref, pev_ref, w1k_ref, w2k_ref, w1v_ref, w2vt_ref,
                     kc_o, vct_o, ks_o, vst_o, kw_o, vwt_o):
    def hidden(xr_ref, pe_ref, w1_ref):
        xr = xr_ref[0]
        top = _dot((xr + pe_ref[0:1, :]).astype(BF16), w1_ref[0])
        bot = _dot((xr + pe_ref[1:2, :]).astype(BF16), w1_ref[1])
        pre = top + pltpu.roll(bot, bot.shape[0] - 1, 0)
        return _silu(pre).astype(BF16)

    kc = _dot(hidden(kcr_ref, pek_ref, w1k_ref), w2k_ref[...])
    kc = _rope(_head_rms(kc, kncmp_ref[...]), cosc_ref[...], sinc_ref[...])
    d0, d1 = _split_kv(kc)
    kc_o[0, 0] = d0.astype(BF16)
    kc_o[0, 1] = d1.astype(BF16)
    vct_o[0] = _dot_nt(w2vt_ref[...], hidden(vcr_ref, pev_ref, w1v_ref)).astype(BF16)

    s = ks_ref.shape[1]
    step = 256
    for r0 in range(0, s, step):
        rows = slice(r0, r0 + step)
        cos, sin = cos_ref[rows, :], sin_ref[rows, :]
        ks = _rope(_head_rms(ks_ref[0, rows, :], knslc_ref[...]), cos, sin)
        kw = _rope(_head_rms(kw_ref[0, rows, :], knwin_ref[...]), cos, sin)
        blk = (r0 + _row((step, LANES))) // SEL_LEN
        onehot = jnp.where(_lane((step, LANES)) - HEAD_DIM == blk, MASK_BIG, 0.0)
        for src, dst, fill in ((ks, ks_o, onehot), (kw, kw_o, 0.0)):
            d0, d1 = _split_kv(src, fill)
            dst[0, 0, rows, :] = d0.astype(BF16)
            dst[0, 1, rows, :] = d1.astype(BF16)
        vst_o[0, :, rows] = vs_ref[0, rows, :].T.astype(BF16)
        vwt_o[0, :, rows] = vw_ref[0, rows, :].T.astype(BF16)


def _nsa_prep(z, kcr, vcr, cos, sin_s, cosc, sinc, kn_cmp, kn_slc, kn_win, pek, pev, w1k, w2k, w1v, w2v):
    b, s, _ = z.shape
    nseg = s // CMP_STRIDE
    zb = lambda name: pl.BlockSpec((1, s, 128), lambda bi, o=ODD_OFF[name][0] // 128: (bi, 0, o))
    full = lambda shp: pl.BlockSpec(shp, lambda bi: (0,) * len(shp))
    seg = pl.BlockSpec((1, nseg, 2048), lambda bi: (bi, 0, 0))
    g2 = lambda g: jnp.tile(g, 2).reshape(1, 128)
    dup = lambda n: (jax.ShapeDtypeStruct((b, 2, n, 128), BF16), pl.BlockSpec((1, 2, n, 128), lambda bi: (bi, 0, 0, 0)))
    tr = lambda n: (jax.ShapeDtypeStruct((b, 128, n), BF16), pl.BlockSpec((1, 128, n), lambda bi: (bi, 0, 0)))
    outs = [dup(nseg), tr(nseg), dup(s), tr(s), dup(s), tr(s)]
    return pl.pallas_call(
        _nsa_prep_kernel,
        grid=(b,),
        in_specs=[seg, seg, zb("ksd"), zb("vsd"), zb("kwd"), zb("vwd"),
                  full((s, 128)), full((s, 128)), full((nseg, 128)), full((nseg, 128)),
                  full((1, 128)), full((1, 128)), full((1, 128)), full((2, 2048)), full((2, 2048)),
                  full((2, 2048, 256)), full((256, 128)), full((2, 2048, 256)), full((128, 256))],
        out_specs=[o[1] for o in outs],
        out_shape=[o[0] for o in outs],
        compiler_params=_cparams(("arbitrary",)),
        name="nsa_prep",
    )(kcr, vcr, z, z, z, z, cos, sin_s, cosc, sinc, g2(kn_cmp), g2(kn_slc), g2(kn_win), pek, pev, w1k, w2k, w1v, w2v)


SEL_CHUNK = 512


def _nsa_kernel(q_ref, gd_ref, gate_ref, cos_ref, sin_ref, qn_ref, kc_ref, vct_ref, ks_ref, vst_ref, kw_ref, vwt_ref,
                ovlt_ref, o_ref):
    i = pl.program_id(1)
    t = 128
    grp = D_HEADS // D_KV_HEADS
    w = grp * t
    kvs = range(D_KV_HEADS)
    q = _rope(_head_rms(q_ref[0], qn_ref[...]), cos_ref[...], sin_ref[...]) * SCALE
    stack = lambda heads: [jnp.concatenate(heads[kv * grp:(kv + 1) * grp], axis=0) for kv in kvs]
    qs = stack(_q_heads(q))
    vt_rows = lambda kv: slice(kv * HEAD_DIM, (kv + 1) * HEAD_DIM)
    tq1 = i * t + _lane((1, t))

    nwin = D_WINDOW + t
    win0 = pl.multiple_of(jnp.maximum(i * t + t - nwin, 0), t)
    s_cmp = [_dot_nt(kc_ref[0, kv], qs[kv]) for kv in kvs]
    s_win = [_dot_nt(kw_ref[0, kv, pl.ds(win0, nwin), :], qs[kv]) for kv in kvs]

    ncmp = kc_ref.shape[2] - 1
    nsel = ovlt_ref.shape[0]
    crow = _row((t, t))
    bias_c = _tile4(jnp.where((crow * CMP_STRIDE + (CMP_LEN - 1) <= tq1) & (crow < ncmp), 0.0, NEG_INF))
    row_ok = _tile4(tq1 >= CMP_LEN - 1)
    jrow = _row((nsel, t))
    jrow_f = jrow.astype(F32)
    cur = (i * t + _lane((nsel, t))) // SEL_LEN
    forced = (jrow == 0) | (jrow == cur)
    oc_t, fills = [], []
    for kv in kvs:
        p, inv = _softmax_cols(s_cmp[kv] + bias_c)
        p = p * jnp.where(row_ok, inv, 0.0)
        oc_t.append(_dot(vct_ref[0, vt_rows(kv), :], p.astype(BF16)))
        psum = p[:, 0:t] + p[:, t:2 * t] + p[:, 2 * t:3 * t] + p[:, 3 * t:4 * t]
        imp = _dot_split_rhs(ovlt_ref[...], psum)
        score = jnp.where(forced, POS_INF, jnp.where(jrow <= cur, imp, NEG_INF))
        chosen = jnp.zeros((nsel, t), F32)
        for _ in range(min(SEL_TOPK, nsel)):
            mx = jnp.max(score, axis=0, keepdims=True)
            first = jnp.min(jnp.where(score == mx, jrow_f, 1e9), axis=0, keepdims=True)
            hit = jrow_f == first
            chosen = jnp.where(hit, 1.0, chosen)
            score = jnp.where(hit, -3e38, score)
        frame = jnp.concatenate([jnp.zeros((HEAD_DIM, t), F32), chosen - 1.0,
                                 jnp.zeros((LANES - HEAD_DIM - nsel, t), F32)], axis=0)
        fills.extend([frame.T] * (grp // 2))

    kpos = win0 + _row((nwin, t))
    rel = tq1 - kpos
    bias_w = _tile4(jnp.where((rel >= 0) & (rel < D_WINDOW), 0.0, NEG_INF))
    ow_t = []
    for kv in kvs:
        p, inv = _softmax_cols(s_win[kv] + bias_w)
        ow_t.append(_dot(vwt_ref[0, vt_rows(kv), pl.ds(win0, nwin)], p.astype(BF16)) * inv)

    qsel = stack(_q_heads(q, fills))
    ch = SEL_CHUNK
    own = (i * t) // ch

    def sel_chunk(c, states, bias=None):
        off = pl.multiple_of(c * ch, ch)
        sc = [_dot_nt(ks_ref[0, kv, pl.ds(off, ch), :], qsel[kv]) for kv in kvs]
        if bias is not None:
            sc = [x + bias for x in sc]
        return tuple(_online_update(states[kv], sc[kv], vst_ref[0, vt_rows(kv), pl.ds(off, ch)]) for kv in kvs)

    init = (jnp.full((1, w), NEG_INF, F32), jnp.zeros((1, w), F32), jnp.zeros((HEAD_DIM, w), F32))
    causal = _tile4(jnp.where(own * ch + _row((ch, t)) <= tq1, 0.0, NEG_INF))
    st = sel_chunk(own, (init, init), causal)
    st = lax.fori_loop(0, own, sel_chunk, st)
    os_t = [acc * (1.0 / l) for (_, l, acc) in st]

    g_t = jax.nn.sigmoid(gate_ref[0]).T
    rows = []
    for h in range(D_HEADS):
        kv, g = divmod(h, grp)
        cols = slice(g * t, (g + 1) * t)
        r = D_BRANCHES * h
        rows.append(g_t[r:r + 1, :] * oc_t[kv][:, cols] + g_t[r + 1:r + 2, :] * os_t[kv][:, cols]
                    + g_t[r + 2:r + 3, :] * ow_t[kv][:, cols])
    o = jnp.concatenate(rows, axis=0).T
    o_ref[0] = (o * _silu(gd_ref[0])).astype(BF16)


def _overlap_t(s):
    ncmp = (s - CMP_LEN) // CMP_STRIDE + 1
    nsel = s // SEL_LEN
    cs = np.arange(ncmp)[None, :] * CMP_STRIDE
    ss = np.arange(nsel)[:, None] * SEL_LEN
    ovl = np.zeros((nsel, s // CMP_STRIDE), np.float32)
    ovl[:, :ncmp] = (cs < ss + SEL_LEN) & (cs + CMP_LEN > ss)
    return ovl


def _nsa(z, cos, sin_s, qn, kc, vct, ks, vst, kw, vwt):
    b, s, _ = z.shape
    t = 128
    nseg = s // CMP_STRIDE
    assert nseg == LANES, "compressed-block scores are laid out on one 128-row tile"
    ovlt = _overlap_t(s)
    full = lambda shp: pl.BlockSpec(shp, lambda bi, i: (0,) * len(shp))
    dup = lambda n: pl.BlockSpec((1, 2, n, 128), lambda bi, i: (bi, 0, 0, 0))
    tr = lambda n: pl.BlockSpec((1, 128, n), lambda bi, i: (bi, 0, 0))
    return pl.pallas_call(
        _nsa_kernel,
        grid=(b, s // t),
        in_specs=[pl.BlockSpec((1, t, 512), lambda bi, i: (bi, i, ODD_OFF["qd"][0] // 512)),
                  pl.BlockSpec((1, t, 512), lambda bi, i: (bi, i, ODD_OFF["gd"][0] // 512)),
                  pl.BlockSpec((1, t, 128), lambda bi, i: (bi, i, ODD_OFF["gate"][0] // 128)),
                  pl.BlockSpec((t, 128), lambda bi, i: (i, 0)),
                  pl.BlockSpec((t, 128), lambda bi, i: (i, 0)),
                  full((1, 512)), dup(nseg), tr(nseg), dup(s), tr(s), dup(s), tr(s), full(ovlt.shape)],
        out_specs=pl.BlockSpec((1, t, 512), lambda bi, i: (bi, i, 0)),
        out_shape=jax.ShapeDtypeStruct((b, s, 512), BF16),
        compiler_params=_cparams(("arbitrary", "arbitrary")),
        name="nsa",
    )(z, z, z, cos, sin_s, jnp.tile(qn, 8).reshape(1, 512), kc, vct, ks, vst, kw, vwt, jnp.asarray(ovlt, BF16))


def _permute_cols(w, order, src):
    cols = []
    for name, width in order:
        o, sw = src[name]
        blk = w[:, o:o + sw]
        if sw < width:
            blk = jnp.pad(blk, ((0, 0), (0, width - sw)))
        cols.append(blk)
    return jnp.concatenate(cols, axis=1).astype(BF16)


def _block_diag_pairs(w):
    z = jnp.zeros((4, LANES, LANES), w.dtype)
    z = z.at[:, 0:64, 0:64].set(w[0::2])
    z = z.at[:, 64:128, 64:128].set(w[1::2])
    return z.astype(BF16)


def _rope_tables(pos):
    half = HEAD_DIM // 2
    inv = ROPE_THETA ** (-jnp.arange(half, dtype=F32) / half)
    ang = pos.astype(F32)[:, None] * inv[None, :]
    cos, sin = jnp.cos(ang), jnp.sin(ang)
    cos_t = jnp.tile(cos, (1, 4))
    sin_t = jnp.tile(jnp.concatenate([-sin, sin], axis=1), (1, 2))
    return cos_t, sin_t


def _expand_compress_w1(w1):
    hdim = w1.shape[1]
    w = w1.reshape(2, CMP_STRIDE, HEAD_DIM, hdim)
    z = jnp.zeros((2, CMP_STRIDE, 2, HEAD_DIM, 2, hdim), w1.dtype)
    z = z.at[:, :, 0, :, 0, :].set(w)
    z = z.at[:, :, 1, :, 1, :].set(w)
    return z.reshape(2, CMP_STRIDE * 2 * HEAD_DIM, 2 * hdim).astype(BF16)


def _expand_compress_w2(w2):
    hdim, hd = w2.shape
    z = jnp.zeros((2 * hdim, 2 * hd), w2.dtype)
    z = z.at[0:hdim, 0:hd].set(w2)
    z = z.at[hdim:, hd:].set(w2)
    return z.astype(BF16)


def _expand_pe(pe):
    p = pe.reshape(2, CMP_STRIDE, 1, HEAD_DIM)
    return jnp.broadcast_to(p, (2, CMP_STRIDE, 2, HEAD_DIM)).reshape(2, CMP_STRIDE * 2 * HEAD_DIM)


def _even_layer(h, mem, g, mem_g, w_mem_kv, m_qn, m_kn, w_in, w_out, a_qn, a_kn, a_sinks,
                conv_w, conv_b, w_r, b_r, w_i, b_i, lam, cos, sin_s):
    b, s, d = h.shape
    h2 = h.reshape(b * s, d)
    z = _inproj(h2, g, _permute_cols(w_in, EVEN_ORDER, EVEN_SRC)).reshape(b, s, EVEN_COLS)
    oa = _swa(z, cos, sin_s, a_qn, a_kn, a_sinks)
    ob = _rglru(z, conv_w, conv_b, _block_diag_pairs(w_r), b_r, _block_diag_pairs(w_i), b_i, lam)
    om = _mem_attention(z, EVEN_OFF["qm"][0], EVEN_OFF["gm"][0], mem, mem_g, w_mem_kv.astype(BF16), m_qn, m_kn)
    out = _outproj(h2, oa.reshape(b * s, 512), ob.reshape(b * s, 512), om.reshape(b * s, 256), w_out.astype(BF16))
    return out.reshape(b, s, d)


def _odd_layer(h, mem, g, mem_g, w_mem_kv, m_qn, m_kn, w_in, w_out, c_lb, c_og,
               d_qn, d_kn_cmp, d_kn_slc, d_kn_win, pe_k, pe_v, w1k, w2k, w1v, w2v, cos, sin_s, cosc, sinc):
    b, s, d = h.shape
    h2 = h.reshape(b * s, d)
    z = _inproj(h2, g, _permute_cols(w_in, ODD_ORDER, ODD_SRC)).reshape(b, s, ODD_COLS)
    oc = _hgrn(z, c_lb, c_og)
    nseg = s // CMP_STRIDE
    seg = lambda name: z[:, :, ODD_OFF[name][0]:ODD_OFF[name][0] + 128].reshape(b, nseg, CMP_STRIDE * 128)
    kc, vc, ks, vs, kw, vw = _nsa_prep(
        z, seg("kcd"), seg("vcd"), cos, sin_s, cosc, sinc, d_kn_cmp, d_kn_slc, d_kn_win,
        _expand_pe(pe_k), _expand_pe(pe_v), _expand_compress_w1(w1k), _expand_compress_w2(w2k),
        _expand_compress_w1(w1v), _expand_compress_w2(w2v).T)
    od = _nsa(z, cos, sin_s, d_qn, kc, vc, ks, vs, kw, vw)
    om = _mem_attention(z, ODD_OFF["qm"][0], ODD_OFF["gm"][0], mem, mem_g, w_mem_kv.astype(BF16), m_qn, m_kn)
    out = _outproj(h2, oc.reshape(b * s, 512), od.reshape(b * s, 512), om.reshape(b * s, 256), w_out.astype(BF16))
    return out.reshape(b, s, d)


def kernel(x, mem, norm_g, mem_norm_g, mem_w_kv, mem_qn, mem_kn, ev_w_in, ev_w_out, a_qn, a_kn, a_sinks,
           b_conv_w, b_conv_b, b_w_r, b_b_r, b_w_i, b_b_i, b_lambda, od_w_in, od_w_out, c_lb, c_onorm,
           d_qn, d_kn_cmp, d_kn_slc, d_kn_win, d_pe_k, d_pe_v, d_w1k, d_w2k, d_w1v, d_w2v):
    depth = norm_g.shape[0]
    assert depth == 2 and c_lb.shape[0] == 2, "the HGRN2 lower-bound formula in the kernel is written for depth 2"
    s = x.shape[1]
    assert s % 256 == 0 and s >= D_WINDOW
    pos = jnp.arange(s)
    cos, sin_s = _rope_tables(pos)
    nseg = s // CMP_STRIDE
    cmp_end = jnp.minimum(jnp.arange(nseg) * CMP_STRIDE + CMP_LEN - 1, s - 1)
    cosc, sinc = _rope_tables(cmp_end)
    h = _even_layer(x, mem, norm_g[0], mem_norm_g[0], mem_w_kv[0], mem_qn[0], mem_kn[0], ev_w_in[0], ev_w_out[0],
                    a_qn[0], a_kn[0], a_sinks[0], b_conv_w[0], b_conv_b[0], b_w_r[0], b_b_r[0], b_w_i[0], b_b_i[0],
                    b_lambda[0], cos, sin_s)
    h = _odd_layer(h, mem, norm_g[1], mem_norm_g[1], mem_w_kv[1], mem_qn[1], mem_kn[1], od_w_in[0], od_w_out[0],
                   c_lb, c_onorm[0], d_qn[0], d_kn_cmp[0], d_kn_slc[0], d_kn_win[0], d_pe_k[0], d_pe_v[0],
                   d_w1k[0], d_w2k[0], d_w1v[0], d_w2v[0], cos, sin_s, cosc, sinc)
    return h
```

```python
import numpy as np
import jax
import jax.numpy as jnp
from jax import lax
from jax.experimental import pallas as pl
from jax.experimental.pallas import tpu as pltpu

F32 = jnp.float32
BF16 = jnp.bfloat16

D_MODEL = 1024
N_MEM = 256
HEAD_DIM = 64
ROPE_THETA = 10000.0
EPS = 1e-6
NEG_INF = -1e30
POS_INF = 1e30
MASK_BIG = 1e30
LANES = 128

A_HEADS, A_KV_HEADS, A_WINDOW = 8, 2, 128
B_WIDTH, B_BLOCKS, B_CONV, B_C = 512, 8, 4, 8.0
M_HEADS = 4
C_HEADS, C_HEAD_DIM, C_CHUNK = 4, 128, 64
D_HEADS, D_KV_HEADS = 8, 2
CMP_LEN, CMP_STRIDE, CMP_HIDDEN = 32, 16, 128
SEL_LEN, SEL_TOPK = 64, 4
D_WINDOW = 512
D_BRANCHES = 3
SCALE = HEAD_DIM ** -0.5
LOG2E = 1.4426950408889634
Q_SCALE = SCALE * LOG2E

EVEN_ORDER = [("qa", 512), ("ga", 512), ("xb", 512), ("gb", 512), ("qm", 256), ("gm", 256), ("ka", 128), ("va", 128)]
EVEN_SRC = {"qa": (0, 512), "ka": (512, 128), "va": (640, 128), "ga": (768, 512), "xb": (1280, 512),
            "gb": (1792, 512), "qm": (2304, 256), "gm": (2560, 256)}
ODD_ORDER = [("qc", 512), ("fc", 512), ("ic", 512), ("gc", 512), ("qd", 512), ("gd", 512), ("qm", 256), ("gm", 256),
             ("kcd", 128), ("vcd", 128), ("ksd", 128), ("vsd", 128), ("kwd", 128), ("vwd", 128), ("gate", 128)]
ODD_SRC = {"qc": (0, 512), "fc": (512, 512), "ic": (1024, 512), "gc": (1536, 512), "qd": (2048, 512),
           "kcd": (2560, 128), "vcd": (2688, 128), "ksd": (2816, 128), "vsd": (2944, 128), "kwd": (3072, 128),
           "vwd": (3200, 128), "gate": (3328, 24), "gd": (3352, 512), "qm": (3864, 256), "gm": (4120, 256)}

VMEM_LIMIT = 48 * 1024 * 1024


def _offsets(order):
    off, out = 0, {}
    for name, w in order:
        out[name] = (off, w)
        off += w
    return out, off


EVEN_OFF, EVEN_COLS = _offsets(EVEN_ORDER)
ODD_OFF, ODD_COLS = _offsets(ODD_ORDER)


def _cparams(sem):
    return pltpu.CompilerParams(dimension_semantics=sem, vmem_limit_bytes=VMEM_LIMIT)


def _dot(a, b):
    return jnp.dot(a, b, preferred_element_type=F32)


def _dot_nt(a, b):
    return lax.dot_general(a, b, (((1,), (1,)), ((), ())), preferred_element_type=F32)


def _split(x):
    hi = x.astype(BF16)
    lo = (x - hi.astype(F32)).astype(BF16)
    return hi, lo


def _dot_split_lhs(x, m):
    hi, lo = _split(x)
    return _dot(hi, m) + _dot(lo, m)


def _dot_split_rhs(m, x):
    hi, lo = _split(x)
    return _dot(m, hi) + _dot(m, lo)


def _lane(shape):
    return lax.broadcasted_iota(jnp.int32, shape, len(shape) - 1)


def _row(shape):
    return lax.broadcasted_iota(jnp.int32, shape, len(shape) - 2)


def _silu(x):
    return x * jax.nn.sigmoid(x)


def _seg_ones():
    r = lax.broadcasted_iota(jnp.int32, (LANES, LANES), 0) >> 6
    c = lax.broadcasted_iota(jnp.int32, (LANES, LANES), 1) >> 6
    return jnp.where(r == c, 1.0, 0.0).astype(BF16)


def _head_rms(x, gain):
    seg = _seg_ones()
    cols = []
    for c in range(x.shape[1] // LANES):
        xc = x[:, c * LANES:(c + 1) * LANES]
        ms = _dot_split_lhs(xc * xc, seg) * (1.0 / HEAD_DIM)
        cols.append(xc * lax.rsqrt(ms + EPS))
    y = cols[0] if len(cols) == 1 else jnp.concatenate(cols, axis=1)
    return y * gain


def _rope(x, cos, sin_s):
    first = (_lane((x.shape[0], LANES)) & 63) < 32
    cols = []
    for c in range(x.shape[1] // LANES):
        xc = x[:, c * LANES:(c + 1) * LANES]
        sw = jnp.where(first, pltpu.roll(xc, 96, 1), pltpu.roll(xc, 32, 1))
        cols.append(xc * cos + sw * sin_s)
    return cols[0] if len(cols) == 1 else jnp.concatenate(cols, axis=1)


def _split_kv(k, fill=0.0):
    lo = _lane(k.shape) < 64
    return jnp.where(lo, k, fill), jnp.where(lo, pltpu.roll(k, 64, 1), fill)


def _q_heads(q, fills=None):
    lo = _lane((q.shape[0], LANES)) < 64
    out = []
    for c in range(q.shape[1] // LANES):
        qc = q[:, c * LANES:(c + 1) * LANES]
        fill = 0.0 if fills is None else fills[c]
        out.append(jnp.where(lo, qc, fill).astype(BF16))
        out.append(jnp.where(lo, pltpu.roll(qc, 64, 1), fill).astype(BF16))
    return out


def _softmax_cols(s, extra=None):
    m = jnp.max(s, axis=0, keepdims=True)
    if extra is not None:
        m = jnp.maximum(m, extra)
    p = jnp.exp2(s - m)
    den = jnp.sum(p, axis=0, keepdims=True)
    if extra is not None:
        den = den + jnp.exp2(extra - m)
    return p, 1.0 / den


def _online_update(state, s, vt_tile):
    m, l, acc = state
    m_new = jnp.maximum(m, jnp.max(s, axis=0, keepdims=True))
    alpha = jnp.exp2(m - m_new)
    p = jnp.exp2(s - m_new)
    l = alpha * l + jnp.sum(p, axis=0, keepdims=True)
    acc = alpha * acc + _dot(vt_tile, p.astype(BF16))
    return m_new, l, acc


def _tile4(x):
    return jnp.concatenate([x, x, x, x], axis=1)


def _inproj_kernel(x_ref, g_ref, w_ref, o_ref):
    x = x_ref[...]
    ms = jnp.mean(x * x, axis=-1, keepdims=True)
    xn = (x * lax.rsqrt(ms + EPS) * g_ref[...]).astype(BF16)
    n = o_ref.shape[-1]
    for c in range(0, n, 512):
        w = min(512, n - c)
        o_ref[:, c:c + w] = _dot(xn, w_ref[:, c:c + w]).astype(BF16)


def _inproj(x2d, gain, w_bf16, tm=256):
    n, d = x2d.shape
    nc = w_bf16.shape[1]
    return pl.pallas_call(
        _inproj_kernel,
        grid=(n // tm,),
        in_specs=[pl.BlockSpec((tm, d), lambda i: (i, 0)),
                  pl.BlockSpec((1, d), lambda i: (0, 0)),
                  pl.BlockSpec((d, nc), lambda i: (0, 0))],
        out_specs=pl.BlockSpec((tm, nc), lambda i: (i, 0)),
        out_shape=jax.ShapeDtypeStruct((n, nc), BF16),
        compiler_params=_cparams(("arbitrary",)),
        name="inproj",
    )(x2d, gain.reshape(1, d), w_bf16)


def _outproj_kernel(h_ref, a_ref, b_ref, m_ref, w_ref, o_ref):
    a, b, m = a_ref[...], b_ref[...], m_ref[...]
    for c in range(0, D_MODEL, 256):
        acc = h_ref[:, c:c + 256]
        acc += _dot(a, w_ref[0:512, c:c + 256])
        acc += _dot(b, w_ref[512:1024, c:c + 256])
        acc += _dot(m, w_ref[1024:1280, c:c + 256])
        o_ref[:, c:c + 256] = acc


def _outproj(h2d, oa, ob, om, w_bf16, tm=512):
    n, d = h2d.shape
    return pl.pallas_call(
        _outproj_kernel,
        grid=(n // tm,),
        in_specs=[pl.BlockSpec((tm, d), lambda i: (i, 0)),
                  pl.BlockSpec((tm, 512), lambda i: (i, 0)),
                  pl.BlockSpec((tm, 512), lambda i: (i, 0)),
                  pl.BlockSpec((tm, 256), lambda i: (i, 0)),
                  pl.BlockSpec((1280, d), lambda i: (0, 0))],
        out_specs=pl.BlockSpec((tm, d), lambda i: (i, 0)),
        out_shape=jax.ShapeDtypeStruct((n, d), F32),
        compiler_params=_cparams(("arbitrary",)),
        name="outproj",
    )(h2d, oa, ob, om, w_bf16)


def _mem_kernel(q_ref, gm_ref, mem_ref, mg_ref, wkv_ref, qn_ref, kn_ref, o_ref, k_scr, vt_scr):
    @pl.when(pl.program_id(1) == 0)
    def _():
        m = mem_ref[0]
        ms = jnp.mean(m * m, axis=-1, keepdims=True)
        mn = (m * lax.rsqrt(ms + EPS) * mg_ref[...]).astype(BF16)
        kv = _dot(mn, wkv_ref[...])
        km = _head_rms(kv[:, 0:256], kn_ref[...])
        for c in range(2):
            k0, k1 = _split_kv(km[:, c * LANES:(c + 1) * LANES])
            k_scr[2 * c] = k0.astype(BF16)
            k_scr[2 * c + 1] = k1.astype(BF16)
        vt_scr[...] = kv[:, 256:512].T.astype(BF16)

    q = _head_rms(q_ref[0].astype(F32), qn_ref[...]) * Q_SCALE
    scores = [_dot_nt(k_scr[h], qh) for h, qh in enumerate(_q_heads(q))]
    rows = []
    for h, sc in enumerate(scores):
        p, inv = _softmax_cols(sc)
        rows.append(_dot(vt_scr[h * HEAD_DIM:(h + 1) * HEAD_DIM, :], p.astype(BF16)) * inv)
    o = jnp.concatenate(rows, axis=0).T
    o_ref[0] = (o * _silu(gm_ref[0].astype(F32))).astype(BF16)


def _mem_attention(z, q_off, g_off, mem, mem_g, wkv_bf16, qn, kn, tq=256):
    b, s, _ = z.shape
    return pl.pallas_call(
        _mem_kernel,
        grid=(b, s // tq),
        in_specs=[pl.BlockSpec((1, tq, 256), lambda bi, i: (bi, i, q_off // 256)),
                  pl.BlockSpec((1, tq, 256), lambda bi, i: (bi, i, g_off // 256)),
                  pl.BlockSpec((1, N_MEM, D_MODEL), lambda bi, i: (bi, 0, 0)),
                  pl.BlockSpec((1, D_MODEL), lambda bi, i: (0, 0)),
                  pl.BlockSpec((D_MODEL, 512), lambda bi, i: (0, 0)),
                  pl.BlockSpec((1, 256), lambda bi, i: (0, 0)),
                  pl.BlockSpec((1, 256), lambda bi, i: (0, 0))],
        out_specs=pl.BlockSpec((1, tq, 256), lambda bi, i: (bi, i, 0)),
        out_shape=jax.ShapeDtypeStruct((b, s, 256), BF16),
        scratch_shapes=[pltpu.VMEM((M_HEADS, N_MEM, LANES), BF16), pltpu.VMEM((M_HEADS * HEAD_DIM, N_MEM), BF16)],
        compiler_params=_cparams(("arbitrary", "arbitrary")),
        name="mem_attention",
    )(z, z, mem, mem_g.reshape(1, D_MODEL), wkv_bf16, jnp.tile(qn, 4).reshape(1, 256), jnp.tile(kn, 4).reshape(1, 256))


def _swa_kernel(q_ref, g_ref, kc_ref, kp_ref, vc_ref, vp_ref, cosc_ref, sinc_ref, cosp_ref, sinp_ref,
                qn_ref, kn_ref, sink_ref, o_ref):
    i = pl.program_id(1)
    t = 128
    grp = A_HEADS // A_KV_HEADS
    q = _rope(_head_rms(q_ref[0].astype(F32), qn_ref[...]), cosc_ref[...], sinc_ref[...]) * Q_SCALE
    qh = _q_heads(q)
    kc = _rope(_head_rms(kc_ref[0].astype(F32), kn_ref[...]), cosc_ref[...], sinc_ref[...])
    kp = _rope(_head_rms(kp_ref[0].astype(F32), kn_ref[...]), cosp_ref[...], sinp_ref[...])
    ks = [x.astype(BF16) for x in _split_kv(jnp.concatenate([kp, kc], axis=0))]
    vt = jnp.concatenate([vp_ref[0].astype(F32).T, vc_ref[0].astype(F32).T], axis=1).astype(BF16)
    ts = (i - 1) * t + _row((2 * t, t))
    rel = i * t + _lane((2 * t, t)) - ts
    bias = _tile4(jnp.where((rel >= 0) & (rel < A_WINDOW) & (ts >= 0), 0.0, NEG_INF))
    scores = [_dot_nt(ks[kv], jnp.concatenate(qh[kv * grp:(kv + 1) * grp], axis=0)) for kv in range(A_KV_HEADS)]
    rows = []
    for kv in range(A_KV_HEADS):
        sink = jnp.concatenate([sink_ref[kv * grp + g:kv * grp + g + 1, :] for g in range(grp)], axis=1) * LOG2E
        p, inv = _softmax_cols(scores[kv] + bias, sink)
        o_t = _dot(vt[kv * HEAD_DIM:(kv + 1) * HEAD_DIM, :], p.astype(BF16)) * inv
        rows.extend(o_t[:, g * t:(g + 1) * t] for g in range(grp))
    o = jnp.concatenate(rows, axis=0).T
    o_ref[0] = (o * _silu(g_ref[0].astype(F32))).astype(BF16)


def _swa(z, cos, sin_s, qn, kn, sinks):
    b, s, _ = z.shape
    t = 128
    qo, go = EVEN_OFF["qa"][0] // 512, EVEN_OFF["ga"][0] // 512
    ko, vo = EVEN_OFF["ka"][0] // 128, EVEN_OFF["va"][0] // 128
    prev = lambda i: jnp.maximum(i - 1, 0)
    return pl.pallas_call(
        _swa_kernel,
        grid=(b, s // t),
        in_specs=[pl.BlockSpec((1, t, 512), lambda bi, i: (bi, i, qo)),
                  pl.BlockSpec((1, t, 512), lambda bi, i: (bi, i, go)),
                  pl.BlockSpec((1, t, 128), lambda bi, i: (bi, i, ko)),
                  pl.BlockSpec((1, t, 128), lambda bi, i: (bi, prev(i), ko)),
                  pl.BlockSpec((1, t, 128), lambda bi, i: (bi, i, vo)),
                  pl.BlockSpec((1, t, 128), lambda bi, i: (bi, prev(i), vo)),
                  pl.BlockSpec((t, 128), lambda bi, i: (i, 0)),
                  pl.BlockSpec((t, 128), lambda bi, i: (i, 0)),
                  pl.BlockSpec((t, 128), lambda bi, i: (prev(i), 0)),
                  pl.BlockSpec((t, 128), lambda bi, i: (prev(i), 0)),
                  pl.BlockSpec((1, 512), lambda bi, i: (0, 0)),
                  pl.BlockSpec((1, 128), lambda bi, i: (0, 0)),
                  pl.BlockSpec((A_HEADS, 128), lambda bi, i: (0, 0))],
        out_specs=pl.BlockSpec((1, t, 512), lambda bi, i: (bi, i, 0)),
        out_shape=jax.ShapeDtypeStruct((b, s, 512), BF16),
        compiler_params=_cparams(("arbitrary", "arbitrary")),
        name="swa",
    )(z, z, z, z, z, z, cos, sin_s, cos, sin_s, jnp.tile(qn, 8).reshape(1, 512), jnp.tile(kn, 2).reshape(1, 128),
      jnp.broadcast_to(sinks.reshape(A_HEADS, 1), (A_HEADS, 128)))


def _rglru_kernel(x_ref, g_ref, cw_ref, cb_ref, wr_ref, br_ref, wi_ref, bi_ref, lam_ref, o_ref,
                  xbuf, hcar, a_scr, u_scr, h_scr):
    t = x_ref.shape[1]

    @pl.when(pl.program_id(1) == 0)
    def _():
        xbuf[0:8, :] = jnp.zeros((8, B_WIDTH), F32)
        hcar[...] = jnp.zeros((8, B_WIDTH), F32)

    xbuf[8:t + 8, :] = x_ref[0].astype(F32)
    xc = cb_ref[...] + cw_ref[0:1, :] * xbuf[5:t + 5, :]
    for j in range(1, B_CONV):
        xc = xc + cw_ref[j:j + 1, :] * xbuf[5 + j:t + 5 + j, :]
    xbuf[0:8, :] = xbuf[t:t + 8, :]

    r_cols, i_cols = [], []
    for c in range(B_WIDTH // LANES):
        xcc = xc[:, c * LANES:(c + 1) * LANES].astype(BF16)
        r_cols.append(_dot(xcc, wr_ref[c]))
        i_cols.append(_dot(xcc, wi_ref[c]))
    r = jax.nn.sigmoid(jnp.concatenate(r_cols, axis=1) + br_ref[...])
    ig = jax.nn.sigmoid(jnp.concatenate(i_cols, axis=1) + bi_ref[...])
    nl = -lam_ref[...]
    softplus = jnp.maximum(nl, 0.0) + jnp.log(1.0 + jnp.exp(-jnp.abs(nl)))
    log_a = -B_C * r * softplus
    a = jnp.exp(log_a)
    u = jnp.sqrt(1.0 - jnp.exp(2.0 * log_a)) * (ig * xc)

    r8 = _row((t, B_WIDTH)) & 7
    for d in (1, 2, 4):
        a_sh = pltpu.roll(a, d, 0)
        u_sh = pltpu.roll(u, d, 0)
        m = r8 >= d
        u = jnp.where(m, a * u_sh + u, u)
        a = jnp.where(m, a * a_sh, a)
    a_scr[...] = a
    u_scr[...] = u

    def body(j, h):
        off = pl.multiple_of(j * 8, 8)
        hh = a_scr[pl.ds(off, 8), :] * h + u_scr[pl.ds(off, 8), :]
        h_scr[pl.ds(off, 8), :] = hh
        return hh[7:8, :]

    h_last = lax.fori_loop(0, t // 8, body, hcar[0:1, :])
    hcar[0:1, :] = h_last
    o_ref[0] = (h_scr[...] * _silu(g_ref[0].astype(F32))).astype(BF16)


def _rglru(z, conv_w, conv_b, wr_bd, b_r, wi_bd, b_i, lam, t=256):
    b, s, _ = z.shape
    xo, go = EVEN_OFF["xb"][0] // 512, EVEN_OFF["gb"][0] // 512
    row = lambda v: v.reshape(1, B_WIDTH)
    full = lambda shp: pl.BlockSpec(shp, lambda bi, i: (0,) * len(shp))
    return pl.pallas_call(
        _rglru_kernel,
        grid=(b, s // t),
        in_specs=[pl.BlockSpec((1, t, 512), lambda bi, i: (bi, i, xo)),
                  pl.BlockSpec((1, t, 512), lambda bi, i: (bi, i, go)),
                  full((B_CONV, B_WIDTH)), full((1, B_WIDTH)),
                  full((4, LANES, LANES)), full((1, B_WIDTH)),
                  full((4, LANES, LANES)), full((1, B_WIDTH)), full((1, B_WIDTH))],
        out_specs=pl.BlockSpec((1, t, 512), lambda bi, i: (bi, i, 0)),
        out_shape=jax.ShapeDtypeStruct((b, s, 512), BF16),
        scratch_shapes=[pltpu.VMEM((t + 8, B_WIDTH), F32), pltpu.VMEM((8, B_WIDTH), F32),
                        pltpu.VMEM((t, B_WIDTH), F32), pltpu.VMEM((t, B_WIDTH), F32), pltpu.VMEM((t, B_WIDTH), F32)],
        compiler_params=_cparams(("arbitrary", "arbitrary")),
        name="rglru",
    )(z, z, conv_w, row(conv_b), wr_bd, row(b_r), wi_bd, row(b_i), row(lam))


def _hgrn_consts():
    c = C_CHUNK
    t = np.arange(c)[:, None]
    s = np.arange(c)[None, :]
    mats = [(s <= t)]
    masks = []
    hs = c // 2
    while hs >= 1:
        mid = (t // (2 * hs)) * 2 * hs + hs - 1
        mats.append(s <= mid)
        same = (t // (2 * hs)) == (s // (2 * hs))
        masks.append(same & ((t // hs) % 2 == 1) & ((s // hs) % 2 == 0))
        hs //= 2
    pairs = [np.concatenate(masks[n:n + 2], axis=1) for n in range(0, len(masks), 2)]
    return np.concatenate(mats, axis=0).astype(np.float32), np.stack(pairs).astype(np.float32)


def _hgrn_kernel(q_ref, f_ref, i_ref, g_ref, lb_ref, og_ref, mst_ref, msk_ref, o_ref, st_scr):
    @pl.when(pl.program_id(1) == 0)
    def _():
        st_scr[...] = jnp.zeros(st_scr.shape, F32)

    c = C_CHUNK
    npair = msk_ref.shape[0]
    p = lb_ref[...]
    pm = jnp.maximum(p[0:1, :], p[1:2, :])
    e0, e1 = jnp.exp(p[0:1, :] - pm), jnp.exp(p[1:2, :] - pm)
    lb = e1 / (e0 + e1)
    zeros = jnp.zeros((c, LANES), BF16)
    for ch in range(q_ref.shape[1] // c):
        rows = slice(ch * c, (ch + 1) * c)
        f = lb + (1.0 - lb) * jax.nn.sigmoid(f_ref[0, rows, :].astype(F32))
        g = jnp.log(f)
        g_hi, g_lo = _split(g)
        b_all = _dot(mst_ref[0:c, :], g_hi) + _dot(mst_ref[0:c, :], g_lo)
        bm_all = _dot(mst_ref[c:, :], g_hi)
        kk_all = 1.0 - f
        qf_all = _silu(q_ref[0, rows, :].astype(F32))
        v_all = i_ref[0, rows, :].astype(F32)
        outs = []
        for h in range(C_HEADS):
            col = slice(h * LANES, (h + 1) * LANES)
            qf, kk, v = qf_all[:, col], kk_all[:, col], v_all[:, col]
            b = b_all[:, col]
            st = st_scr[h]
            o = _dot_nt((qf * jnp.exp(b)).astype(BF16), st.astype(BF16))
            att = []
            for n in range(npair):
                qt, kt = [], []
                for l in (2 * n, 2 * n + 1):
                    bm = bm_all[l * c:(l + 1) * c, col]
                    qt.append((qf * jnp.exp(jnp.minimum(b - bm, 1.0))).astype(BF16))
                    kt.append((kk * jnp.exp(jnp.minimum(bm - b, 1.0))).astype(BF16))
                lhs = jnp.concatenate(qt, axis=1)
                rhs = jnp.concatenate([jnp.concatenate([kt[0], zeros], axis=1),
                                       jnp.concatenate([zeros, kt[1]], axis=1)], axis=0)
                att.append(jnp.where(msk_ref[n] > 0.5, _dot_nt(lhs, rhs), 0.0).astype(BF16))
            diag = jnp.sum(qf * kk, axis=-1, keepdims=True)
            vb = v.astype(BF16)
            o = o + _dot(jnp.concatenate(att, axis=1), jnp.concatenate([vb] * (2 * npair), axis=0)) + diag * v
            b_last = b[c - 1:c, :]
            kd = (kk * jnp.exp(b_last - b)).astype(BF16)
            st_scr[h] = st * jnp.exp(b_last) + _dot(v.T.astype(BF16), kd)
            ms = jnp.mean(o * o, axis=-1, keepdims=True)
            outs.append(o * lax.rsqrt(ms + EPS) * og_ref[...])
        o_ref[0, rows, :] = (jnp.concatenate(outs, axis=1) * _silu(g_ref[0, rows, :].astype(F32))).astype(BF16)


def _hgrn(z, c_lb, c_og, t=256):
    b, s, _ = z.shape
    mst, msk = _hgrn_consts()
    blk = lambda name: pl.BlockSpec((1, t, 512), lambda bi, i, o=ODD_OFF[name][0] // 512: (bi, i, o))
    full = lambda shp: pl.BlockSpec(shp, lambda bi, i: (0,) * len(shp))
    return pl.pallas_call(
        _hgrn_kernel,
        grid=(b, s // t),
        in_specs=[blk("qc"), blk("fc"), blk("ic"), blk("gc"), full(c_lb.shape), full((1, C_HEAD_DIM)),
                  full(mst.shape), full(msk.shape)],
        out_specs=pl.BlockSpec((1, t, 512), lambda bi, i: (bi, i, 0)),
        out_shape=jax.ShapeDtypeStruct((b, s, 512), BF16),
        scratch_shapes=[pltpu.VMEM((C_HEADS, C_HEAD_DIM, C_HEAD_DIM), F32)],
        compiler_params=_cparams(("arbitrary", "arbitrary")),
        name="hgrn2",
    )(z, z, z, z, c_lb, c_og.reshape(1, C_HEAD_DIM), jnp.asarray(mst, BF16), jnp.asarray(msk, F32))


def _nsa_prep_kernel(kcr_ref, vcr_ref, ks_ref, vs_ref, kw_ref, vw_ref, cos_ref, sin_ref, cosc_ref, sinc_ref,
                     kncmp_ref, knslc_ref, knwin_ref, pek_ref, pev_ref, w1k_ref, w2k_ref, w1v_ref, w2vt_ref,
                     kc_o, vct_o, ks_o, vst_o, kw_o, vwt_o):
    def hidden(xr_ref, pe_ref, w1_ref):
        xr = xr_ref[0].astype(F32)
        top = _dot((xr + pe_ref[0:1, :]).astype(BF16), w1_ref[0])
        bot = _dot((xr + pe_ref[1:2, :]).astype(BF16), w1_ref[1])
        pre = top + pltpu.roll(bot, bot.shape[0] - 1, 0)
        return _silu(pre).astype(BF16)

    kc = _dot(hidden(kcr_ref, pek_ref, w1k_ref), w2k_ref[...])
    kc = _rope(_head_rms(kc, kncmp_ref[...]), cosc_ref[...], sinc_ref[...])
    d0, d1 = _split_kv(kc)
    kc_o[0, 0] = d0.astype(BF16)
    kc_o[0, 1] = d1.astype(BF16)
    vct_o[0] = _dot_nt(w2vt_ref[...], hidden(vcr_ref, pev_ref, w1v_ref)).astype(BF16)

    s = ks_ref.shape[1]
    step = 256
    for r0 in range(0, s, step):
        rows = slice(r0, r0 + step)
        cos, sin = cos_ref[rows, :], sin_ref[rows, :]
        ks = _rope(_head_rms(ks_ref[0, rows, :].astype(F32), knslc_ref[...]), cos, sin)
        kw = _rope(_head_rms(kw_ref[0, rows, :].astype(F32), knwin_ref[...]), cos, sin)
        blk = (r0 + _row((step, LANES))) // SEL_LEN
        onehot = jnp.where(_lane((step, LANES)) - HEAD_DIM == blk, MASK_BIG, 0.0)
        for src, dst, fill in ((ks, ks_o, onehot), (kw, kw_o, 0.0)):
            d0, d1 = _split_kv(src, fill)
            dst[0, 0, rows, :] = d0.astype(BF16)
            dst[0, 1, rows, :] = d1.astype(BF16)
        vst_o[0, :, rows] = vs_ref[0, rows, :].astype(F32).T.astype(BF16)
        vwt_o[0, :, rows] = vw_ref[0, rows, :].astype(F32).T.astype(BF16)


def _nsa_prep(z, kcr, vcr, cos, sin_s, cosc, sinc, kn_cmp, kn_slc, kn_win, pek, pev, w1k, w2k, w1v, w2v):
    b, s, _ = z.shape
    nseg = s // CMP_STRIDE
    zb = lambda name: pl.BlockSpec((1, s, 128), lambda bi, o=ODD_OFF[name][0] // 128: (bi, 0, o))
    full = lambda shp: pl.BlockSpec(shp, lambda bi: (0,) * len(shp))
    seg = pl.BlockSpec((1, nseg, 2048), lambda bi: (bi, 0, 0))
    g2 = lambda g: jnp.tile(g, 2).reshape(1, 128)
    dup = lambda n: (jax.ShapeDtypeStruct((b, 2, n, 128), BF16), pl.BlockSpec((1, 2, n, 128), lambda bi: (bi, 0, 0, 0)))
    tr = lambda n: (jax.ShapeDtypeStruct((b, 128, n), BF16), pl.BlockSpec((1, 128, n), lambda bi: (bi, 0, 0)))
    outs = [dup(nseg), tr(nseg), dup(s), tr(s), dup(s), tr(s)]
    return pl.pallas_call(
        _nsa_prep_kernel,
        grid=(b,),
        in_specs=[seg, seg, zb("ksd"), zb("vsd"), zb("kwd"), zb("vwd"),
                  full((s, 128)), full((s, 128)), full((nseg, 128)), full((nseg, 128)),
                  full((1, 128)), full((1, 128)), full((1, 128)), full((2, 2048)), full((2, 2048)),
                  full((2, 2048, 256)), full((256, 128)), full((2, 2048, 256)), full((128, 256))],
        out_specs=[o[1] for o in outs],
        out_shape=[o[0] for o in outs],
        compiler_params=_cparams(("arbitrary",)),
        name="nsa_prep",
    )(kcr, vcr, z, z, z, z, cos, sin_s, cosc, sinc, g2(kn_cmp), g2(kn_slc), g2(kn_win), pek, pev, w1k, w2k, w1v, w2v)


SEL_CHUNK = 512
SEL_PIECE = 128


def _nsa_kernel(q_ref, gd_ref, gate_ref, cos_ref, sin_ref, qn_ref, kc_ref, vct_ref, ks_ref, vst_ref, kw_ref, vwt_ref,
                ovlt_ref, o_ref):
    i = pl.program_id(1)
    t = 128
    grp = D_HEADS // D_KV_HEADS
    w = grp * t
    kvs = range(D_KV_HEADS)
    q = _rope(_head_rms(q_ref[0].astype(F32), qn_ref[...]), cos_ref[...], sin_ref[...]) * Q_SCALE
    stack = lambda heads: [jnp.concatenate(heads[kv * grp:(kv + 1) * grp], axis=0) for kv in kvs]
    qs = stack(_q_heads(q))
    vt_rows = lambda kv: slice(kv * HEAD_DIM, (kv + 1) * HEAD_DIM)
    tq1 = i * t + _lane((1, t))

    nwin = D_WINDOW + t
    s_cmp = [_dot_nt(kc_ref[0, kv], qs[kv]) for kv in kvs]

    ncmp = kc_ref.shape[2] - 1
    nsel = ovlt_ref.shape[0]
    crow = _row((t, t))
    bias_c = _tile4(jnp.where((crow * CMP_STRIDE + (CMP_LEN - 1) <= tq1) & (crow < ncmp), 0.0, NEG_INF))
    row_ok = _tile4(tq1 >= CMP_LEN - 1)
    jrow = _row((nsel, t))
    jrow_f = jrow.astype(F32)
    cur = (i * t + _lane((nsel, t))) // SEL_LEN
    forced = (jrow == 0) | (jrow == cur)
    oc_t, fills = [], []
    for kv in kvs:
        p, inv = _softmax_cols(s_cmp[kv] + bias_c)
        p = p * jnp.where(row_ok, inv, 0.0)
        oc_t.append(_dot(vct_ref[0, vt_rows(kv), :], p.astype(BF16)))
        psum = p[:, 0:t] + p[:, t:2 * t] + p[:, 2 * t:3 * t] + p[:, 3 * t:4 * t]
        imp = _dot_split_rhs(ovlt_ref[...], psum)
        score = jnp.where(forced, POS_INF, jnp.where(jrow <= cur, imp, NEG_INF))
        chosen = jnp.zeros((nsel, t), F32)
        for _ in range(min(SEL_TOPK, nsel)):
            mx = jnp.max(score, axis=0, keepdims=True)
            first = jnp.min(jnp.where(score == mx, jrow_f, 1e9), axis=0, keepdims=True)
            hit = jrow_f == first
            chosen = jnp.where(hit, 1.0, chosen)
            score = jnp.where(hit, -3e38, score)
        frame = jnp.concatenate([jnp.zeros((HEAD_DIM, t), F32), chosen - 1.0,
                                 jnp.zeros((LANES - HEAD_DIM - nsel, t), F32)], axis=0)
        fills.extend([frame.T] * (grp // 2))

    init = (jnp.full((1, w), NEG_INF, F32), jnp.zeros((1, w), F32), jnp.zeros((HEAD_DIM, w), F32))
    pc = SEL_PIECE

    def sweep(states, k_ref, vt_ref, queries, offs, biases, n):
        scores = [[_dot_nt(k_ref[0, kv, pl.ds(off, n), :], queries[kv]) for kv in kvs] for off in offs]
        for off, sc, bias in zip(offs, scores, biases):
            states = tuple(_online_update(states[kv], sc[kv] if bias is None else sc[kv] + bias,
                                          vt_ref[0, vt_rows(kv), pl.ds(off, n)]) for kv in kvs)
        return states

    win0 = pl.multiple_of(jnp.maximum(i * t + t - nwin, 0), t)
    rel = tq1 - (win0 + _row((nwin, t)))
    bias_w = _tile4(jnp.where((rel >= 0) & (rel < D_WINDOW), 0.0, NEG_INF))
    s_win = [_dot_nt(kw_ref[0, kv, pl.ds(win0, nwin), :], qs[kv]) for kv in kvs]
    ow_t = []
    for kv in kvs:
        p, inv = _softmax_cols(s_win[kv] + bias_w)
        ow_t.append(_dot(vwt_ref[0, vt_rows(kv), pl.ds(win0, nwin)], p.astype(BF16)) * inv)

    qsel = stack(_q_heads(q, fills))
    ch = SEL_CHUNK
    own = pl.multiple_of(((i * t) // ch) * ch, ch)
    st = sweep((init, init), ks_ref, vst_ref, qsel, [own],
               [_tile4(jnp.where(own + _row((ch, t)) <= tq1, 0.0, NEG_INF))], ch)

    def sel_chunk(c, states):
        base = pl.multiple_of(c * ch, ch)
        return sweep(states, ks_ref, vst_ref, qsel, [base + n * pc for n in range(ch // pc)], [None] * (ch // pc), pc)

    st = lax.fori_loop(0, (i * t) // ch, sel_chunk, st)
    os_t = [acc * (1.0 / l) for (_, l, acc) in st]

    g_t = jax.nn.sigmoid(gate_ref[0].astype(F32)).T
    rows = []
    for h in range(D_HEADS):
        kv, g = divmod(h, grp)
        cols = slice(g * t, (g + 1) * t)
        r = D_BRANCHES * h
        rows.append(g_t[r:r + 1, :] * oc_t[kv][:, cols] + g_t[r + 1:r + 2, :] * os_t[kv][:, cols]
                    + g_t[r + 2:r + 3, :] * ow_t[kv][:, cols])
    o = jnp.concatenate(rows, axis=0).T
    o_ref[0] = (o * _silu(gd_ref[0].astype(F32))).astype(BF16)


def _overlap_t(s):
    ncmp = (s - CMP_LEN) // CMP_STRIDE + 1
    nsel = s // SEL_LEN
    cs = np.arange(ncmp)[None, :] * CMP_STRIDE
    ss = np.arange(nsel)[:, None] * SEL_LEN
    ovl = np.zeros((nsel, s // CMP_STRIDE), np.float32)
    ovl[:, :ncmp] = (cs < ss + SEL_LEN) & (cs + CMP_LEN > ss)
    return ovl


def _nsa(z, cos, sin_s, qn, kc, vct, ks, vst, kw, vwt):
    b, s, _ = z.shape
    t = 128
    nseg = s // CMP_STRIDE
    assert nseg == LANES, "compressed-block scores are laid out on one 128-row tile"
    ovlt = _overlap_t(s)
    full = lambda shp: pl.BlockSpec(shp, lambda bi, i: (0,) * len(shp))
    dup = lambda n: pl.BlockSpec((1, 2, n, 128), lambda bi, i: (bi, 0, 0, 0))
    tr = lambda n: pl.BlockSpec((1, 128, n), lambda bi, i: (bi, 0, 0))
    return pl.pallas_call(
        _nsa_kernel,
        grid=(b, s // t),
        in_specs=[pl.BlockSpec((1, t, 512), lambda bi, i: (bi, i, ODD_OFF["qd"][0] // 512)),
                  pl.BlockSpec((1, t, 512), lambda bi, i: (bi, i, ODD_OFF["gd"][0] // 512)),
                  pl.BlockSpec((1, t, 128), lambda bi, i: (bi, i, ODD_OFF["gate"][0] // 128)),
                  pl.BlockSpec((t, 128), lambda bi, i: (i, 0)),
                  pl.BlockSpec((t, 128), lambda bi, i: (i, 0)),
                  full((1, 512)), dup(nseg), tr(nseg), dup(s), tr(s), dup(s), tr(s), full(ovlt.shape)],
        out_specs=pl.BlockSpec((1, t, 512), lambda bi, i: (bi, i, 0)),
        out_shape=jax.ShapeDtypeStruct((b, s, 512), BF16),
        compiler_params=_cparams(("arbitrary", "arbitrary")),
        name="nsa",
    )(z, z, z, cos, sin_s, jnp.tile(qn, 8).reshape(1, 512), kc, vct, ks, vst, kw, vwt, jnp.asarray(ovlt, BF16))


def _permute_cols(w, order, src):
    cols = []
    for name, width in order:
        o, sw = src[name]
        blk = w[:, o:o + sw]
        if sw < width:
            blk = jnp.pad(blk, ((0, 0), (0, width - sw)))
        cols.append(blk)
    return jnp.concatenate(cols, axis=1).astype(BF16)


def _block_diag_pairs(w):
    z = jnp.zeros((4, LANES, LANES), w.dtype)
    z = z.at[:, 0:64, 0:64].set(w[0::2])
    z = z.at[:, 64:128, 64:128].set(w[1::2])
    return z.astype(BF16)


def _rope_tables(pos):
    half = HEAD_DIM // 2
    inv = ROPE_THETA ** (-jnp.arange(half, dtype=F32) / half)
    ang = pos.astype(F32)[:, None] * inv[None, :]
    cos, sin = jnp.cos(ang), jnp.sin(ang)
    cos_t = jnp.tile(cos, (1, 4))
    sin_t = jnp.tile(jnp.concatenate([-sin, sin], axis=1), (1, 2))
    return cos_t, sin_t


def _expand_compress_w1(w1):
    hdim = w1.shape[1]
    w = w1.reshape(2, CMP_STRIDE, HEAD_DIM, hdim)
    z = jnp.zeros((2, CMP_STRIDE, 2, HEAD_DIM, 2, hdim), w1.dtype)
    z = z.at[:, :, 0, :, 0, :].set(w)
    z = z.at[:, :, 1, :, 1, :].set(w)
    return z.reshape(2, CMP_STRIDE * 2 * HEAD_DIM, 2 * hdim).astype(BF16)


def _expand_compress_w2(w2):
    hdim, hd = w2.shape
    z = jnp.zeros((2 * hdim, 2 * hd), w2.dtype)
    z = z.at[0:hdim, 0:hd].set(w2)
    z = z.at[hdim:, hd:].set(w2)
    return z.astype(BF16)


def _expand_pe(pe):
    p = pe.reshape(2, CMP_STRIDE, 1, HEAD_DIM)
    return jnp.broadcast_to(p, (2, CMP_STRIDE, 2, HEAD_DIM)).reshape(2, CMP_STRIDE * 2 * HEAD_DIM)


def _even_layer(h, mem, g, mem_g, w_mem_kv, m_qn, m_kn, w_in, w_out, a_qn, a_kn, a_sinks,
                conv_w, conv_b, w_r, b_r, w_i, b_i, lam, cos, sin_s):
    b, s, d = h.shape
    h2 = h.reshape(b * s, d)
    z = _inproj(h2, g, _permute_cols(w_in, EVEN_ORDER, EVEN_SRC)).reshape(b, s, EVEN_COLS)
    oa = _swa(z, cos, sin_s, a_qn, a_kn, a_sinks)
    ob = _rglru(z, conv_w, conv_b, _block_diag_pairs(w_r), b_r, _block_diag_pairs(w_i), b_i, lam)
    om = _mem_attention(z, EVEN_OFF["qm"][0], EVEN_OFF["gm"][0], mem, mem_g, w_mem_kv.astype(BF16), m_qn, m_kn)
    out = _outproj(h2, oa.reshape(b * s, 512), ob.reshape(b * s, 512), om.reshape(b * s, 256), w_out.astype(BF16))
    return out.reshape(b, s, d)


def _odd_layer(h, mem, g, mem_g, w_mem_kv, m_qn, m_kn, w_in, w_out, c_lb, c_og,
               d_qn, d_kn_cmp, d_kn_slc, d_kn_win, pe_k, pe_v, w1k, w2k, w1v, w2v, cos, sin_s, cosc, sinc):
    b, s, d = h.shape
    h2 = h.reshape(b * s, d)
    z = _inproj(h2, g, _permute_cols(w_in, ODD_ORDER, ODD_SRC)).reshape(b, s, ODD_COLS)
    oc = _hgrn(z, c_lb, c_og)
    nseg = s // CMP_STRIDE
    seg = lambda name: z[:, :, ODD_OFF[name][0]:ODD_OFF[name][0] + 128].reshape(b, nseg, CMP_STRIDE * 128)
    kc, vc, ks, vs, kw, vw = _nsa_prep(
        z, seg("kcd"), seg("vcd"), cos, sin_s, cosc, sinc, d_kn_cmp, d_kn_slc, d_kn_win,
        _expand_pe(pe_k), _expand_pe(pe_v), _expand_compress_w1(w1k), _expand_compress_w2(w2k),
        _expand_compress_w1(w1v), _expand_compress_w2(w2v).T)
    od = _nsa(z, cos, sin_s, d_qn, kc, vc, ks, vs, kw, vw)
    om = _mem_attention(z, ODD_OFF["qm"][0], ODD_OFF["gm"][0], mem, mem_g, w_mem_kv.astype(BF16), m_qn, m_kn)
    out = _outproj(h2, oc.reshape(b * s, 512), od.reshape(b * s, 512), om.reshape(b * s, 256), w_out.astype(BF16))
    return out.reshape(b, s, d)


def kernel(x, mem, norm_g, mem_norm_g, mem_w_kv, mem_qn, mem_kn, ev_w_in, ev_w_out, a_qn, a_kn, a_sinks,
           b_conv_w, b_conv_b, b_w_r, b_b_r, b_w_i, b_b_i, b_lambda, od_w_in, od_w_out, c_lb, c_onorm,
           d_qn, d_kn_cmp, d_kn_slc, d_kn_win, d_pe_k, d_pe_v, d_w1k, d_w2k, d_w1v, d_w2v):
    depth = norm_g.shape[0]
    assert depth == 2 and c_lb.shape[0] == 2, "the HGRN2 lower-bound formula in the kernel is written for depth 2"
    s = x.shape[1]
    assert s % 256 == 0 and s >= D_WINDOW
    pos = jnp.arange(s)
    cos, sin_s = _rope_tables(pos)
    nseg = s // CMP_STRIDE
    cmp_end = jnp.minimum(jnp.arange(nseg) * CMP_STRIDE + CMP_LEN - 1, s - 1)
    cosc, sinc = _rope_tables(cmp_end)
    h = _even_layer(x, mem, norm_g[0], mem_norm_g[0], mem_w_kv[0], mem_qn[0], mem_kn[0], ev_w_in[0], ev_w_out[0],
                    a_qn[0], a_kn[0], a_sinks[0], b_conv_w[0], b_conv_b[0], b_w_r[0], b_b_r[0], b_w_i[0], b_b_i[0],
                    b_lambda[0], cos, sin_s)
    h = _odd_layer(h, mem, norm_g[1], mem_norm_g[1], mem_w_kv[1], mem_qn[1], mem_kn[1], od_w_in[0], od_w_out[0],
                   c_lb, c_onorm[0], d_qn[0], d_kn_cmp[0], d_kn_slc[0], d_kn_win[0], d_pe_k[0], d_pe_v[0],
                   d_w1k[0], d_w2k[0], d_w1v[0], d_w2v[0], cos, sin_s, cosc, sinc)
    return h
```

```python
import numpy as np
import jax
import jax.numpy as jnp
from jax import lax
from jax.experimental import pallas as pl
from jax.experimental.pallas import tpu as pltpu

F32 = jnp.float32
BF16 = jnp.bfloat16

D_MODEL = 1024
N_MEM = 256
HEAD_DIM = 64
ROPE_THETA = 10000.0
EPS = 1e-6
NEG_INF = -1e30
POS_INF = 1e30
MASK_BIG = 1e30
LANES = 128

A_HEADS, A_KV_HEADS, A_WINDOW = 8, 2, 128
B_WIDTH, B_BLOCKS, B_CONV, B_C = 512, 8, 4, 8.0
M_HEADS = 4
C_HEADS, C_HEAD_DIM, C_CHUNK = 4, 128, 64
D_HEADS, D_KV_HEADS = 8, 2
CMP_LEN, CMP_STRIDE, CMP_HIDDEN = 32, 16, 128
SEL_LEN, SEL_TOPK = 64, 4
D_WINDOW = 512
D_BRANCHES = 3
SCALE = HEAD_DIM ** -0.5
LOG2E = 1.4426950408889634
Q_SCALE = SCALE * LOG2E

EVEN_ORDER = [("qa", 512), ("ga", 512), ("xb", 512), ("gb", 512), ("qm", 256), ("gm", 256), ("ka", 128), ("va", 128)]
EVEN_SRC = {"qa": (0, 512), "ka": (512, 128), "va": (640, 128), "ga": (768, 512), "xb": (1280, 512),
            "gb": (1792, 512), "qm": (2304, 256), "gm": (2560, 256)}
ODD_ORDER = [("qc", 512), ("fc", 512), ("ic", 512), ("gc", 512), ("qd", 512), ("gd", 512), ("qm", 256), ("gm", 256),
             ("kcd", 128), ("vcd", 128), ("ksd", 128), ("vsd", 128), ("kwd", 128), ("vwd", 128), ("gate", 128)]
ODD_SRC = {"qc": (0, 512), "fc": (512, 512), "ic": (1024, 512), "gc": (1536, 512), "qd": (2048, 512),
           "kcd": (2560, 128), "vcd": (2688, 128), "ksd": (2816, 128), "vsd": (2944, 128), "kwd": (3072, 128),
           "vwd": (3200, 128), "gate": (3328, 24), "gd": (3352, 512), "qm": (3864, 256), "gm": (4120, 256)}

VMEM_LIMIT = 48 * 1024 * 1024


def _offsets(order):
    off, out = 0, {}
    for name, w in order:
        out[name] = (off, w)
        off += w
    return out, off


EVEN_OFF, EVEN_COLS = _offsets(EVEN_ORDER)
ODD_OFF, ODD_COLS = _offsets(ODD_ORDER)


def _cparams(sem):
    return pltpu.CompilerParams(dimension_semantics=sem, vmem_limit_bytes=VMEM_LIMIT)


def _dot(a, b):
    return jnp.dot(a, b, preferred_element_type=F32)


def _dot_nt(a, b):
    return lax.dot_general(a, b, (((1,), (1,)), ((), ())), preferred_element_type=F32)


def _split(x):
    hi = x.astype(BF16)
    lo = (x - hi.astype(F32)).astype(BF16)
    return hi, lo


def _dot_split_lhs(x, m):
    hi, lo = _split(x)
    return _dot(hi, m) + _dot(lo, m)


def _dot_split_rhs(m, x):
    hi, lo = _split(x)
    return _dot(m, hi) + _dot(m, lo)


def _lane(shape):
    return lax.broadcasted_iota(jnp.int32, shape, len(shape) - 1)


def _row(shape):
    return lax.broadcasted_iota(jnp.int32, shape, len(shape) - 2)


def _silu(x):
    return x * jax.nn.sigmoid(x)


def _seg_ones():
    r = lax.broadcasted_iota(jnp.int32, (LANES, LANES), 0) >> 6
    c = lax.broadcasted_iota(jnp.int32, (LANES, LANES), 1) >> 6
    return jnp.where(r == c, 1.0, 0.0).astype(BF16)


def _head_rms(x, gain):
    seg = _seg_ones()
    cols = []
    for c in range(x.shape[1] // LANES):
        xc = x[:, c * LANES:(c + 1) * LANES]
        ms = _dot_split_lhs(xc * xc, seg) * (1.0 / HEAD_DIM)
        cols.append(xc * lax.rsqrt(ms + EPS))
    y = cols[0] if len(cols) == 1 else jnp.concatenate(cols, axis=1)
    return y * gain


def _rope(x, cos, sin_s):
    first = (_lane((x.shape[0], LANES)) & 63) < 32
    cols = []
    for c in range(x.shape[1] // LANES):
        xc = x[:, c * LANES:(c + 1) * LANES]
        sw = jnp.where(first, pltpu.roll(xc, 96, 1), pltpu.roll(xc, 32, 1))
        cols.append(xc * cos + sw * sin_s)
    return cols[0] if len(cols) == 1 else jnp.concatenate(cols, axis=1)


def _split_kv(k, fill=0.0):
    lo = _lane(k.shape) < 64
    return jnp.where(lo, k, fill), jnp.where(lo, pltpu.roll(k, 64, 1), fill)


def _q_heads(q, fills=None):
    lo = _lane((q.shape[0], LANES)) < 64
    out = []
    for c in range(q.shape[1] // LANES):
        qc = q[:, c * LANES:(c + 1) * LANES]
        fill = 0.0 if fills is None else fills[c]
        out.append(jnp.where(lo, qc, fill).astype(BF16))
        out.append(jnp.where(lo, pltpu.roll(qc, 64, 1), fill).astype(BF16))
    return out


def _softmax_cols(s, extra=None):
    m = jnp.max(s, axis=0, keepdims=True)
    if extra is not None:
        m = jnp.maximum(m, extra)
    p = jnp.exp2(s - m)
    den = jnp.sum(p, axis=0, keepdims=True)
    if extra is not None:
        den = den + jnp.exp2(extra - m)
    return p, 1.0 / den


def _online_update(state, s, vt_tile):
    m, l, acc = state
    m_new = jnp.maximum(m, jnp.max(s, axis=0, keepdims=True))
    alpha = jnp.exp2(m - m_new)
    p = jnp.exp2(s - m_new)
    l = alpha * l + jnp.sum(p, axis=0, keepdims=True)
    acc = alpha * acc + _dot(vt_tile, p.astype(BF16))
    return m_new, l, acc


def _tile4(x):
    return jnp.concatenate([x, x, x, x], axis=1)


def _inproj_kernel(x_ref, g_ref, w_ref, o_ref):
    x = x_ref[...]
    ms = jnp.mean(x * x, axis=-1, keepdims=True)
    xn = (x * lax.rsqrt(ms + EPS) * g_ref[...]).astype(BF16)
    n = o_ref.shape[-1]
    for c in range(0, n, 512):
        w = min(512, n - c)
        o_ref[:, c:c + w] = _dot(xn, w_ref[:, c:c + w]).astype(BF16)


def _inproj(x2d, gain, w_bf16, tm=512):
    n, d = x2d.shape
    nc = w_bf16.shape[1]
    return pl.pallas_call(
        _inproj_kernel,
        grid=(n // tm,),
        in_specs=[pl.BlockSpec((tm, d), lambda i: (i, 0)),
                  pl.BlockSpec((1, d), lambda i: (0, 0)),
                  pl.BlockSpec((d, nc), lambda i: (0, 0))],
        out_specs=pl.BlockSpec((tm, nc), lambda i: (i, 0)),
        out_shape=jax.ShapeDtypeStruct((n, nc), BF16),
        compiler_params=_cparams(("arbitrary",)),
        name="inproj",
    )(x2d, gain.reshape(1, d), w_bf16)


def _outproj_kernel(h_ref, a_ref, b_ref, m_ref, w_ref, o_ref):
    a, b, m = a_ref[...], b_ref[...], m_ref[...]
    for c in range(0, D_MODEL, 256):
        acc = h_ref[:, c:c + 256]
        acc += _dot(a, w_ref[0:512, c:c + 256])
        acc += _dot(b, w_ref[512:1024, c:c + 256])
        acc += _dot(m, w_ref[1024:1280, c:c + 256])
        o_ref[:, c:c + 256] = acc


def _outproj(h2d, oa, ob, om, w_bf16, tm=1024):
    n, d = h2d.shape
    return pl.pallas_call(
        _outproj_kernel,
        grid=(n // tm,),
        in_specs=[pl.BlockSpec((tm, d), lambda i: (i, 0)),
                  pl.BlockSpec((tm, 512), lambda i: (i, 0)),
                  pl.BlockSpec((tm, 512), lambda i: (i, 0)),
                  pl.BlockSpec((tm, 256), lambda i: (i, 0)),
                  pl.BlockSpec((1280, d), lambda i: (0, 0))],
        out_specs=pl.BlockSpec((tm, d), lambda i: (i, 0)),
        out_shape=jax.ShapeDtypeStruct((n, d), F32),
        compiler_params=_cparams(("arbitrary",)),
        name="outproj",
    )(h2d, oa, ob, om, w_bf16)


def _mem_kernel(q_ref, gm_ref, mem_ref, mg_ref, wkv_ref, qn_ref, kn_ref, o_ref, k_scr, vt_scr):
    @pl.when(pl.program_id(1) == 0)
    def _():
        m = mem_ref[0]
        ms = jnp.mean(m * m, axis=-1, keepdims=True)
        mn = (m * lax.rsqrt(ms + EPS) * mg_ref[...]).astype(BF16)
        kv = _dot(mn, wkv_ref[...])
        km = _head_rms(kv[:, 0:256], kn_ref[...])
        for c in range(2):
            k0, k1 = _split_kv(km[:, c * LANES:(c + 1) * LANES])
            k_scr[2 * c] = k0.astype(BF16)
            k_scr[2 * c + 1] = k1.astype(BF16)
        vt_scr[...] = kv[:, 256:512].T.astype(BF16)

    q = _head_rms(q_ref[0].astype(F32), qn_ref[...]) * Q_SCALE
    scores = [_dot_nt(k_scr[h], qh) for h, qh in enumerate(_q_heads(q))]
    rows = []
    for h, sc in enumerate(scores):
        p, inv = _softmax_cols(sc)
        rows.append(_dot(vt_scr[h * HEAD_DIM:(h + 1) * HEAD_DIM, :], p.astype(BF16)) * inv)
    o = jnp.concatenate(rows, axis=0).T
    o_ref[0] = (o * _silu(gm_ref[0].astype(F32))).astype(BF16)


def _mem_attention(z, q_off, g_off, mem, mem_g, wkv_bf16, qn, kn, tq=512):
    b, s, _ = z.shape
    return pl.pallas_call(
        _mem_kernel,
        grid=(b, s // tq),
        in_specs=[pl.BlockSpec((1, tq, 256), lambda bi, i: (bi, i, q_off // 256)),
                  pl.BlockSpec((1, tq, 256), lambda bi, i: (bi, i, g_off // 256)),
                  pl.BlockSpec((1, N_MEM, D_MODEL), lambda bi, i: (bi, 0, 0)),
                  pl.BlockSpec((1, D_MODEL), lambda bi, i: (0, 0)),
                  pl.BlockSpec((D_MODEL, 512), lambda bi, i: (0, 0)),
                  pl.BlockSpec((1, 256), lambda bi, i: (0, 0)),
                  pl.BlockSpec((1, 256), lambda bi, i: (0, 0))],
        out_specs=pl.BlockSpec((1, tq, 256), lambda bi, i: (bi, i, 0)),
        out_shape=jax.ShapeDtypeStruct((b, s, 256), BF16),
        scratch_shapes=[pltpu.VMEM((M_HEADS, N_MEM, LANES), BF16), pltpu.VMEM((M_HEADS * HEAD_DIM, N_MEM), BF16)],
        compiler_params=_cparams(("arbitrary", "arbitrary")),
        name="mem_attention",
    )(z, z, mem, mem_g.reshape(1, D_MODEL), wkv_bf16, jnp.tile(qn, 4).reshape(1, 256), jnp.tile(kn, 4).reshape(1, 256))


def _swa_kernel(q_ref, g_ref, kc_ref, kp_ref, vc_ref, vp_ref, cosc_ref, sinc_ref, cosp_ref, sinp_ref,
                qn_ref, kn_ref, sink_ref, o_ref):
    i = pl.program_id(1)
    t = q_ref.shape[1]
    tp = kp_ref.shape[1]
    grp = A_HEADS // A_KV_HEADS
    q = _rope(_head_rms(q_ref[0].astype(F32), qn_ref[...]), cosc_ref[...], sinc_ref[...]) * Q_SCALE
    qh = _q_heads(q)
    kc = _rope(_head_rms(kc_ref[0].astype(F32), kn_ref[...]), cosc_ref[...], sinc_ref[...])
    kp = _rope(_head_rms(kp_ref[0].astype(F32), kn_ref[...]), cosp_ref[...], sinp_ref[...])
    ks = [x.astype(BF16) for x in _split_kv(jnp.concatenate([kp, kc], axis=0))]
    vt = jnp.concatenate([vp_ref[0].astype(F32).T, vc_ref[0].astype(F32).T], axis=1).astype(BF16)
    ts = i * t - tp + _row((tp + t, t))
    rel = i * t + _lane((tp + t, t)) - ts
    bias = _tile4(jnp.where((rel >= 0) & (rel < A_WINDOW) & (ts >= 0), 0.0, NEG_INF))
    scores = [_dot_nt(ks[kv], jnp.concatenate(qh[kv * grp:(kv + 1) * grp], axis=0)) for kv in range(A_KV_HEADS)]
    rows = []
    for kv in range(A_KV_HEADS):
        sink = jnp.concatenate([jnp.broadcast_to(sink_ref[kv * grp + g:kv * grp + g + 1, 0:1], (1, t))
                                for g in range(grp)], axis=1) * LOG2E
        p, inv = _softmax_cols(scores[kv] + bias, sink)
        o_t = _dot(vt[kv * HEAD_DIM:(kv + 1) * HEAD_DIM, :], p.astype(BF16)) * inv
        rows.extend(o_t[:, g * t:(g + 1) * t] for g in range(grp))
    o = jnp.concatenate(rows, axis=0).T
    o_ref[0] = (o * _silu(g_ref[0].astype(F32))).astype(BF16)


def _swa(z, cos, sin_s, qn, kn, sinks, t=256):
    b, s, _ = z.shape
    tp = A_WINDOW
    qo, go = EVEN_OFF["qa"][0] // 512, EVEN_OFF["ga"][0] // 512
    ko, vo = EVEN_OFF["ka"][0] // 128, EVEN_OFF["va"][0] // 128
    prev = lambda i: jnp.maximum(i * (t // tp) - 1, 0)
    return pl.pallas_call(
        _swa_kernel,
        grid=(b, s // t),
        in_specs=[pl.BlockSpec((1, t, 512), lambda bi, i: (bi, i, qo)),
                  pl.BlockSpec((1, t, 512), lambda bi, i: (bi, i, go)),
                  pl.BlockSpec((1, t, 128), lambda bi, i: (bi, i, ko)),
                  pl.BlockSpec((1, tp, 128), lambda bi, i: (bi, prev(i), ko)),
                  pl.BlockSpec((1, t, 128), lambda bi, i: (bi, i, vo)),
                  pl.BlockSpec((1, tp, 128), lambda bi, i: (bi, prev(i), vo)),
                  pl.BlockSpec((t, 128), lambda bi, i: (i, 0)),
                  pl.BlockSpec((t, 128), lambda bi, i: (i, 0)),
                  pl.BlockSpec((tp, 128), lambda bi, i: (prev(i), 0)),
                  pl.BlockSpec((tp, 128), lambda bi, i: (prev(i), 0)),
                  pl.BlockSpec((1, 512), lambda bi, i: (0, 0)),
                  pl.BlockSpec((1, 128), lambda bi, i: (0, 0)),
                  pl.BlockSpec((A_HEADS, 128), lambda bi, i: (0, 0))],
        out_specs=pl.BlockSpec((1, t, 512), lambda bi, i: (bi, i, 0)),
        out_shape=jax.ShapeDtypeStruct((b, s, 512), BF16),
        compiler_params=_cparams(("arbitrary", "arbitrary")),
        name="swa",
    )(z, z, z, z, z, z, cos, sin_s, cos, sin_s, jnp.tile(qn, 8).reshape(1, 512), jnp.tile(kn, 2).reshape(1, 128),
      jnp.broadcast_to(sinks.reshape(A_HEADS, 1), (A_HEADS, 128)))


def _rglru_kernel(x_ref, g_ref, cw_ref, cb_ref, wr_ref, br_ref, wi_ref, bi_ref, lam_ref, o_ref,
                  xbuf, hcar, a_scr, u_scr, h_scr):
    t = x_ref.shape[1]

    @pl.when(pl.program_id(1) == 0)
    def _():
        xbuf[0:8, :] = jnp.zeros((8, B_WIDTH), F32)
        hcar[...] = jnp.zeros((8, B_WIDTH), F32)

    xbuf[8:t + 8, :] = x_ref[0].astype(F32)
    xc = cb_ref[...] + cw_ref[0:1, :] * xbuf[5:t + 5, :]
    for j in range(1, B_CONV):
        xc = xc + cw_ref[j:j + 1, :] * xbuf[5 + j:t + 5 + j, :]
    xbuf[0:8, :] = xbuf[t:t + 8, :]

    r_cols, i_cols = [], []
    for c in range(B_WIDTH // LANES):
        xcc = xc[:, c * LANES:(c + 1) * LANES].astype(BF16)
        r_cols.append(_dot(xcc, wr_ref[c]))
        i_cols.append(_dot(xcc, wi_ref[c]))
    r = jax.nn.sigmoid(jnp.concatenate(r_cols, axis=1) + br_ref[...])
    ig = jax.nn.sigmoid(jnp.concatenate(i_cols, axis=1) + bi_ref[...])
    nl = -lam_ref[...]
    softplus = jnp.maximum(nl, 0.0) + jnp.log(1.0 + jnp.exp(-jnp.abs(nl)))
    log_a = -B_C * r * softplus
    a = jnp.exp(log_a)
    u = jnp.sqrt(1.0 - jnp.exp(2.0 * log_a)) * (ig * xc)

    r8 = _row((t, B_WIDTH)) & 7
    for d in (1, 2, 4):
        a_sh = pltpu.roll(a, d, 0)
        u_sh = pltpu.roll(u, d, 0)
        m = r8 >= d
        u = jnp.where(m, a * u_sh + u, u)
        a = jnp.where(m, a * a_sh, a)
    a_scr[...] = a
    u_scr[...] = u

    def body(j, h):
        off = pl.multiple_of(j * 8, 8)
        hh = a_scr[pl.ds(off, 8), :] * h + u_scr[pl.ds(off, 8), :]
        h_scr[pl.ds(off, 8), :] = hh
        return hh[7:8, :]

    h_last = lax.fori_loop(0, t // 8, body, hcar[0:1, :])
    hcar[0:1, :] = h_last
    o_ref[0] = (h_scr[...] * _silu(g_ref[0].astype(F32))).astype(BF16)


def _rglru(z, conv_w, conv_b, wr_bd, b_r, wi_bd, b_i, lam, t=512):
    b, s, _ = z.shape
    xo, go = EVEN_OFF["xb"][0] // 512, EVEN_OFF["gb"][0] // 512
    row = lambda v: v.reshape(1, B_WIDTH)
    full = lambda shp: pl.BlockSpec(shp, lambda bi, i: (0,) * len(shp))
    return pl.pallas_call(
        _rglru_kernel,
        grid=(b, s // t),
        in_specs=[pl.BlockSpec((1, t, 512), lambda bi, i: (bi, i, xo)),
                  pl.BlockSpec((1, t, 512), lambda bi, i: (bi, i, go)),
                  full((B_CONV, B_WIDTH)), full((1, B_WIDTH)),
                  full((4, LANES, LANES)), full((1, B_WIDTH)),
                  full((4, LANES, LANES)), full((1, B_WIDTH)), full((1, B_WIDTH))],
        out_specs=pl.BlockSpec((1, t, 512), lambda bi, i: (bi, i, 0)),
        out_shape=jax.ShapeDtypeStruct((b, s, 512), BF16),
        scratch_shapes=[pltpu.VMEM((t + 8, B_WIDTH), F32), pltpu.VMEM((8, B_WIDTH), F32),
                        pltpu.VMEM((t, B_WIDTH), F32), pltpu.VMEM((t, B_WIDTH), F32), pltpu.VMEM((t, B_WIDTH), F32)],
        compiler_params=_cparams(("arbitrary", "arbitrary")),
        name="rglru",
    )(z, z, conv_w, row(conv_b), wr_bd, row(b_r), wi_bd, row(b_i), row(lam))


def _hgrn_consts():
    c = C_CHUNK
    t = np.arange(c)[:, None]
    s = np.arange(c)[None, :]
    mats = [(s <= t)]
    masks = []
    hs = c // 2
    while hs >= 1:
        mid = (t // (2 * hs)) * 2 * hs + hs - 1
        mats.append(s <= mid)
        same = (t // (2 * hs)) == (s // (2 * hs))
        masks.append(same & ((t // hs) % 2 == 1) & ((s // hs) % 2 == 0))
        hs //= 2
    pairs = [np.concatenate(masks[n:n + 2], axis=1) for n in range(0, len(masks), 2)]
    return np.concatenate(mats, axis=0).astype(np.float32), np.stack(pairs).astype(np.float32)


def _hgrn_kernel(q_ref, f_ref, i_ref, g_ref, lb_ref, og_ref, mst_ref, msk_ref, o_ref, st_scr):
    @pl.when(pl.program_id(1) == 0)
    def _():
        st_scr[...] = jnp.zeros(st_scr.shape, F32)

    c = C_CHUNK
    npair = msk_ref.shape[0]
    p = lb_ref[...]
    pm = jnp.maximum(p[0:1, :], p[1:2, :])
    e0, e1 = jnp.exp(p[0:1, :] - pm), jnp.exp(p[1:2, :] - pm)
    lb = e1 / (e0 + e1)
    zeros = jnp.zeros((c, LANES), BF16)
    for ch in range(q_ref.shape[1] // c):
        rows = slice(ch * c, (ch + 1) * c)
        f = lb + (1.0 - lb) * jax.nn.sigmoid(f_ref[0, rows, :].astype(F32))
        g = jnp.log(f)
        g_hi, g_lo = _split(g)
        b_all = _dot(mst_ref[0:c, :], g_hi) + _dot(mst_ref[0:c, :], g_lo)
        bm_all = _dot(mst_ref[c:, :], g_hi)
        kk_all = 1.0 - f
        qf_all = _silu(q_ref[0, rows, :].astype(F32))
        v_all = i_ref[0, rows, :].astype(F32)
        outs = []
        for h in range(C_HEADS):
            col = slice(h * LANES, (h + 1) * LANES)
            qf, kk, v = qf_all[:, col], kk_all[:, col], v_all[:, col]
            b = b_all[:, col]
            st = st_scr[h]
            o = _dot_nt((qf * jnp.exp(b)).astype(BF16), st.astype(BF16))
            att = []
            for n in range(npair):
                qt, kt = [], []
                for l in (2 * n, 2 * n + 1):
                    bm = bm_all[l * c:(l + 1) * c, col]
                    qt.append((qf * jnp.exp(jnp.minimum(b - bm, 1.0))).astype(BF16))
                    kt.append((kk * jnp.exp(jnp.minimum(bm - b, 1.0))).astype(BF16))
                lhs = jnp.concatenate(qt, axis=1)
                rhs = jnp.concatenate([jnp.concatenate([kt[0], zeros], axis=1),
                                       jnp.concatenate([zeros, kt[1]], axis=1)], axis=0)
                att.append(jnp.where(msk_ref[n] > 0.5, _dot_nt(lhs, rhs), 0.0).astype(BF16))
            diag = jnp.sum(qf * kk, axis=-1, keepdims=True)
            vb = v.astype(BF16)
            o = o + _dot(jnp.concatenate(att, axis=1), jnp.concatenate([vb] * (2 * npair), axis=0)) + diag * v
            b_last = b[c - 1:c, :]
            kd = (kk * jnp.exp(b_last - b)).astype(BF16)
            st_scr[h] = st * jnp.exp(b_last) + _dot(v.T.astype(BF16), kd)
            ms = jnp.mean(o * o, axis=-1, keepdims=True)
            outs.append(o * lax.rsqrt(ms + EPS) * og_ref[...])
        o_ref[0, rows, :] = (jnp.concatenate(outs, axis=1) * _silu(g_ref[0, rows, :].astype(F32))).astype(BF16)


def _hgrn(z, c_lb, c_og, t=512):
    b, s, _ = z.shape
    mst, msk = _hgrn_consts()
    blk = lambda name: pl.BlockSpec((1, t, 512), lambda bi, i, o=ODD_OFF[name][0] // 512: (bi, i, o))
    full = lambda shp: pl.BlockSpec(shp, lambda bi, i: (0,) * len(shp))
    return pl.pallas_call(
        _hgrn_kernel,
        grid=(b, s // t),
        in_specs=[blk("qc"), blk("fc"), blk("ic"), blk("gc"), full(c_lb.shape), full((1, C_HEAD_DIM)),
                  full(mst.shape), full(msk.shape)],
        out_specs=pl.BlockSpec((1, t, 512), lambda bi, i: (bi, i, 0)),
        out_shape=jax.ShapeDtypeStruct((b, s, 512), BF16),
        scratch_shapes=[pltpu.VMEM((C_HEADS, C_HEAD_DIM, C_HEAD_DIM), F32)],
        compiler_params=_cparams(("arbitrary", "arbitrary")),
        name="hgrn2",
    )(z, z, z, z, c_lb, c_og.reshape(1, C_HEAD_DIM), jnp.asarray(mst, BF16), jnp.asarray(msk, F32))


def _nsa_prep_kernel(kcr_ref, vcr_ref, ks_ref, vs_ref, kw_ref, vw_ref, cos_ref, sin_ref, cosc_ref, sinc_ref,
                     kncmp_ref, knslc_ref, knwin_ref, pek_ref, pev_ref, w1k_ref, w2k_ref, w1v_ref, w2vt_ref,
                     kc_o, vct_o, ks_o, vst_o, kw_o, vwt_o):
    def hidden(xr_ref, pe_ref, w1_ref):
        xr = xr_ref[0].astype(F32)
        top = _dot((xr + pe_ref[0:1, :]).astype(BF16), w1_ref[0])
        bot = _dot((xr + pe_ref[1:2, :]).astype(BF16), w1_ref[1])
        pre = top + pltpu.roll(bot, bot.shape[0] - 1, 0)
        return _silu(pre).astype(BF16)

    kc = _dot(hidden(kcr_ref, pek_ref, w1k_ref), w2k_ref[...])
    kc = _rope(_head_rms(kc, kncmp_ref[...]), cosc_ref[...], sinc_ref[...])
    d0, d1 = _split_kv(kc)
    kc_o[0, 0] = d0.astype(BF16)
    kc_o[0, 1] = d1.astype(BF16)
    vct_o[0] = _dot_nt(w2vt_ref[...], hidden(vcr_ref, pev_ref, w1v_ref)).astype(BF16)

    s = ks_ref.shape[1]
    step = 256
    for r0 in range(0, s, step):
        rows = slice(r0, r0 + step)
        cos, sin = cos_ref[rows, :], sin_ref[rows, :]
        ks = _rope(_head_rms(ks_ref[0, rows, :].astype(F32), knslc_ref[...]), cos, sin)
        kw = _rope(_head_rms(kw_ref[0, rows, :].astype(F32), knwin_ref[...]), cos, sin)
        blk = (r0 + _row((step, LANES))) // SEL_LEN
        onehot = jnp.where(_lane((step, LANES)) - HEAD_DIM == blk, MASK_BIG, 0.0)
        for src, dst, fill in ((ks, ks_o, onehot), (kw, kw_o, 0.0)):
            d0, d1 = _split_kv(src, fill)
            dst[0, 0, rows, :] = d0.astype(BF16)
            dst[0, 1, rows, :] = d1.astype(BF16)
        vst_o[0, :, rows] = vs_ref[0, rows, :].astype(F32).T.astype(BF16)
        vwt_o[0, :, rows] = vw_ref[0, rows, :].astype(F32).T.astype(BF16)


def _nsa_prep(z, kcr, vcr, cos, sin_s, cosc, sinc, kn_cmp, kn_slc, kn_win, pek, pev, w1k, w2k, w1v, w2v):
    b, s, _ = z.shape
    nseg = s // CMP_STRIDE
    zb = lambda name: pl.BlockSpec((1, s, 128), lambda bi, o=ODD_OFF[name][0] // 128: (bi, 0, o))
    full = lambda shp: pl.BlockSpec(shp, lambda bi: (0,) * len(shp))
    seg = pl.BlockSpec((1, nseg, 2048), lambda bi: (bi, 0, 0))
    g2 = lambda g: jnp.tile(g, 2).reshape(1, 128)
    dup = lambda n: (jax.ShapeDtypeStruct((b, 2, n, 128), BF16), pl.BlockSpec((1, 2, n, 128), lambda bi: (bi, 0, 0, 0)))
    tr = lambda n: (jax.ShapeDtypeStruct((b, 128, n), BF16), pl.BlockSpec((1, 128, n), lambda bi: (bi, 0, 0)))
    outs = [dup(nseg), tr(nseg), dup(s), tr(s), dup(s), tr(s)]
    return pl.pallas_call(
        _nsa_prep_kernel,
        grid=(b,),
        in_specs=[seg, seg, zb("ksd"), zb("vsd"), zb("kwd"), zb("vwd"),
                  full((s, 128)), full((s, 128)), full((nseg, 128)), full((nseg, 128)),
                  full((1, 128)), full((1, 128)), full((1, 128)), full((2, 2048)), full((2, 2048)),
                  full((2, 2048, 256)), full((256, 128)), full((2, 2048, 256)), full((128, 256))],
        out_specs=[o[1] for o in outs],
        out_shape=[o[0] for o in outs],
        compiler_params=_cparams(("arbitrary",)),
        name="nsa_prep",
    )(kcr, vcr, z, z, z, z, cos, sin_s, cosc, sinc, g2(kn_cmp), g2(kn_slc), g2(kn_win), pek, pev, w1k, w2k, w1v, w2v)


SEL_CHUNK = 512
SEL_PIECE = 128


def _nsa_kernel(q_ref, gd_ref, gate_ref, cos_ref, sin_ref, qn_ref, kc_ref, vct_ref, ks_ref, vst_ref, kw_ref, vwt_ref,
                ovlt_ref, o_ref):
    i = pl.program_id(1)
    t = q_ref.shape[1]
    grp = D_HEADS // D_KV_HEADS
    w = grp * t
    kvs = range(D_KV_HEADS)
    q = _rope(_head_rms(q_ref[0].astype(F32), qn_ref[...]), cos_ref[...], sin_ref[...]) * Q_SCALE
    stack = lambda heads: [jnp.concatenate(heads[kv * grp:(kv + 1) * grp], axis=0) for kv in kvs]
    qs = stack(_q_heads(q))
    vt_rows = lambda kv: slice(kv * HEAD_DIM, (kv + 1) * HEAD_DIM)
    tq1 = i * t + _lane((1, t))

    nwin = D_WINDOW + t
    s_cmp = [_dot_nt(kc_ref[0, kv], qs[kv]) for kv in kvs]

    ncmp = kc_ref.shape[2] - 1
    nsel = ovlt_ref.shape[0]
    crow = _row((kc_ref.shape[2], t))
    bias_c = _tile4(jnp.where((crow * CMP_STRIDE + (CMP_LEN - 1) <= tq1) & (crow < ncmp), 0.0, NEG_INF))
    row_ok = _tile4(tq1 >= CMP_LEN - 1)
    jrow = _row((nsel, t))
    jrow_f = jrow.astype(F32)
    cur = (i * t + _lane((nsel, t))) // SEL_LEN
    forced = (jrow == 0) | (jrow == cur)
    oc_t, fills = [], []
    for kv in kvs:
        p, inv = _softmax_cols(s_cmp[kv] + bias_c)
        p = p * jnp.where(row_ok, inv, 0.0)
        oc_t.append(_dot(vct_ref[0, vt_rows(kv), :], p.astype(BF16)))
        psum = p[:, 0:t] + p[:, t:2 * t] + p[:, 2 * t:3 * t] + p[:, 3 * t:4 * t]
        imp = _dot_split_rhs(ovlt_ref[...], psum)
        score = jnp.where(forced, POS_INF, jnp.where(jrow <= cur, imp, NEG_INF))
        chosen = jnp.zeros((nsel, t), F32)
        for _ in range(min(SEL_TOPK, nsel)):
            mx = jnp.max(score, axis=0, keepdims=True)
            first = jnp.min(jnp.where(score == mx, jrow_f, 1e9), axis=0, keepdims=True)
            hit = jrow_f == first
            chosen = jnp.where(hit, 1.0, chosen)
            score = jnp.where(hit, -3e38, score)
        frame = jnp.concatenate([jnp.zeros((HEAD_DIM, t), F32), chosen - 1.0,
                                 jnp.zeros((LANES - HEAD_DIM - nsel, t), F32)], axis=0)
        fills.extend([frame.T] * (grp // 2))

    init = (jnp.full((1, w), NEG_INF, F32), jnp.zeros((1, w), F32), jnp.zeros((HEAD_DIM, w), F32))
    pc = SEL_PIECE

    def sweep(states, k_ref, vt_ref, queries, offs, biases, n):
        scores = [[_dot_nt(k_ref[0, kv, pl.ds(off, n), :], queries[kv]) for kv in kvs] for off in offs]
        for off, sc, bias in zip(offs, scores, biases):
            states = tuple(_online_update(states[kv], sc[kv] if bias is None else sc[kv] + bias,
                                          vt_ref[0, vt_rows(kv), pl.ds(off, n)]) for kv in kvs)
        return states

    win0 = pl.multiple_of(jnp.maximum(i * t + t - nwin, 0), t)
    rel = tq1 - (win0 + _row((nwin, t)))
    bias_w = _tile4(jnp.where((rel >= 0) & (rel < D_WINDOW), 0.0, NEG_INF))
    s_win = [_dot_nt(kw_ref[0, kv, pl.ds(win0, nwin), :], qs[kv]) for kv in kvs]
    ow_t = []
    for kv in kvs:
        p, inv = _softmax_cols(s_win[kv] + bias_w)
        ow_t.append(_dot(vwt_ref[0, vt_rows(kv), pl.ds(win0, nwin)], p.astype(BF16)) * inv)

    qsel = stack(_q_heads(q, fills))
    ch = SEL_CHUNK
    own = pl.multiple_of(((i * t) // ch) * ch, ch)
    st = sweep((init, init), ks_ref, vst_ref, qsel, [own],
               [_tile4(jnp.where(own + _row((ch, t)) <= tq1, 0.0, NEG_INF))], ch)

    def sel_chunk(c, states):
        base = pl.multiple_of(c * ch, ch)
        return sweep(states, ks_ref, vst_ref, qsel, [base + n * pc for n in range(ch // pc)], [None] * (ch // pc), pc)

    st = lax.fori_loop(0, (i * t) // ch, sel_chunk, st)
    os_t = [acc * (1.0 / l) for (_, l, acc) in st]

    g_t = jax.nn.sigmoid(gate_ref[0].astype(F32)).T
    rows = []
    for h in range(D_HEADS):
        kv, g = divmod(h, grp)
        cols = slice(g * t, (g + 1) * t)
        r = D_BRANCHES * h
        rows.append(g_t[r:r + 1, :] * oc_t[kv][:, cols] + g_t[r + 1:r + 2, :] * os_t[kv][:, cols]
                    + g_t[r + 2:r + 3, :] * ow_t[kv][:, cols])
    o = jnp.concatenate(rows, axis=0).T
    o_ref[0] = (o * _silu(gd_ref[0].astype(F32))).astype(BF16)


def _overlap_t(s):
    ncmp = (s - CMP_LEN) // CMP_STRIDE + 1
    nsel = s // SEL_LEN
    cs = np.arange(ncmp)[None, :] * CMP_STRIDE
    ss = np.arange(nsel)[:, None] * SEL_LEN
    ovl = np.zeros((nsel, s // CMP_STRIDE), np.float32)
    ovl[:, :ncmp] = (cs < ss + SEL_LEN) & (cs + CMP_LEN > ss)
    return ovl


def _nsa(z, cos, sin_s, qn, kc, vct, ks, vst, kw, vwt, t=256):
    b, s, _ = z.shape
    nseg = s // CMP_STRIDE
    assert nseg == LANES, "compressed-block scores are laid out on one 128-row tile"
    ovlt = _overlap_t(s)
    full = lambda shp: pl.BlockSpec(shp, lambda bi, i: (0,) * len(shp))
    dup = lambda n: pl.BlockSpec((1, 2, n, 128), lambda bi, i: (bi, 0, 0, 0))
    tr = lambda n: pl.BlockSpec((1, 128, n), lambda bi, i: (bi, 0, 0))
    return pl.pallas_call(
        _nsa_kernel,
        grid=(b, s // t),
        in_specs=[pl.BlockSpec((1, t, 512), lambda bi, i: (bi, i, ODD_OFF["qd"][0] // 512)),
                  pl.BlockSpec((1, t, 512), lambda bi, i: (bi, i, ODD_OFF["gd"][0] // 512)),
                  pl.BlockSpec((1, t, 128), lambda bi, i: (bi, i, ODD_OFF["gate"][0] // 128)),
                  pl.BlockSpec((t, 128), lambda bi, i: (i, 0)),
                  pl.BlockSpec((t, 128), lambda bi, i: (i, 0)),
                  full((1, 512)), dup(nseg), tr(nseg), dup(s), tr(s), dup(s), tr(s), full(ovlt.shape)],
        out_specs=pl.BlockSpec((1, t, 512), lambda bi, i: (bi, i, 0)),
        out_shape=jax.ShapeDtypeStruct((b, s, 512), BF16),
        compiler_params=_cparams(("arbitrary", "arbitrary")),
        name="nsa",
    )(z, z, z, cos, sin_s, jnp.tile(qn, 8).reshape(1, 512), kc, vct, ks, vst, kw, vwt, jnp.asarray(ovlt, BF16))


def _permute_cols(w, order, src):
    cols = []
    for name, width in order:
        o, sw = src[name]
        blk = w[:, o:o + sw]
        if sw < width:
            blk = jnp.pad(blk, ((0, 0), (0, width - sw)))
        cols.append(blk)
    return jnp.concatenate(cols, axis=1).astype(BF16)


def _block_diag_pairs(w):
    z = jnp.zeros((4, LANES, LANES), w.dtype)
    z = z.at[:, 0:64, 0:64].set(w[0::2])
    z = z.at[:, 64:128, 64:128].set(w[1::2])
    return z.astype(BF16)


def _rope_tables(pos):
    half = HEAD_DIM // 2
    inv = ROPE_THETA ** (-jnp.arange(half, dtype=F32) / half)
    ang = pos.astype(F32)[:, None] * inv[None, :]
    cos, sin = jnp.cos(ang), jnp.sin(ang)
    cos_t = jnp.tile(cos, (1, 4))
    sin_t = jnp.tile(jnp.concatenate([-sin, sin], axis=1), (1, 2))
    return cos_t, sin_t


def _expand_compress_w1(w1):
    hdim = w1.shape[1]
    w = w1.reshape(2, CMP_STRIDE, HEAD_DIM, hdim)
    z = jnp.zeros((2, CMP_STRIDE, 2, HEAD_DIM, 2, hdim), w1.dtype)
    z = z.at[:, :, 0, :, 0, :].set(w)
    z = z.at[:, :, 1, :, 1, :].set(w)
    return z.reshape(2, CMP_STRIDE * 2 * HEAD_DIM, 2 * hdim).astype(BF16)


def _expand_compress_w2(w2):
    hdim, hd = w2.shape
    z = jnp.zeros((2 * hdim, 2 * hd), w2.dtype)
    z = z.at[0:hdim, 0:hd].set(w2)
    z = z.at[hdim:, hd:].set(w2)
    return z.astype(BF16)


def _expand_pe(pe):
    p = pe.reshape(2, CMP_STRIDE, 1, HEAD_DIM)
    return jnp.broadcast_to(p, (2, CMP_STRIDE, 2, HEAD_DIM)).reshape(2, CMP_STRIDE * 2 * HEAD_DIM)


def _even_layer(h, mem, g, mem_g, w_mem_kv, m_qn, m_kn, w_in, w_out, a_qn, a_kn, a_sinks,
                conv_w, conv_b, w_r, b_r, w_i, b_i, lam, cos, sin_s):
    b, s, d = h.shape
    h2 = h.reshape(b * s, d)
    z = _inproj(h2, g, _permute_cols(w_in, EVEN_ORDER, EVEN_SRC)).reshape(b, s, EVEN_COLS)
    oa = _swa(z, cos, sin_s, a_qn, a_kn, a_sinks)
    ob = _rglru(z, conv_w, conv_b, _block_diag_pairs(w_r), b_r, _block_diag_pairs(w_i), b_i, lam)
    om = _mem_attention(z, EVEN_OFF["qm"][0], EVEN_OFF["gm"][0], mem, mem_g, w_mem_kv.astype(BF16), m_qn, m_kn)
    out = _outproj(h2, oa.reshape(b * s, 512), ob.reshape(b * s, 512), om.reshape(b * s, 256), w_out.astype(BF16))
    return out.reshape(b, s, d)


def _odd_layer(h, mem, g, mem_g, w_mem_kv, m_qn, m_kn, w_in, w_out, c_lb, c_og,
               d_qn, d_kn_cmp, d_kn_slc, d_kn_win, pe_k, pe_v, w1k, w2k, w1v, w2v, cos, sin_s, cosc, sinc):
    b, s, d = h.shape
    h2 = h.reshape(b * s, d)
    z = _inproj(h2, g, _permute_cols(w_in, ODD_ORDER, ODD_SRC)).reshape(b, s, ODD_COLS)
    oc = _hgrn(z, c_lb, c_og)
    nseg = s // CMP_STRIDE
    seg = lambda name: z[:, :, ODD_OFF[name][0]:ODD_OFF[name][0] + 128].reshape(b, nseg, CMP_STRIDE * 128)
    kc, vc, ks, vs, kw, vw = _nsa_prep(
        z, seg("kcd"), seg("vcd"), cos, sin_s, cosc, sinc, d_kn_cmp, d_kn_slc, d_kn_win,
        _expand_pe(pe_k), _expand_pe(pe_v), _expand_compress_w1(w1k), _expand_compress_w2(w2k),
        _expand_compress_w1(w1v), _expand_compress_w2(w2v).T)
    od = _nsa(z, cos, sin_s, d_qn, kc, vc, ks, vs, kw, vw)
    om = _mem_attention(z, ODD_OFF["qm"][0], ODD_OFF["gm"][0], mem, mem_g, w_mem_kv.astype(BF16), m_qn, m_kn)
    out = _outproj(h2, oc.reshape(b * s, 512), od.reshape(b * s, 512), om.reshape(b * s, 256), w_out.astype(BF16))
    return out.reshape(b, s, d)


def kernel(x, mem, norm_g, mem_norm_g, mem_w_kv, mem_qn, mem_kn, ev_w_in, ev_w_out, a_qn, a_kn, a_sinks,
           b_conv_w, b_conv_b, b_w_r, b_b_r, b_w_i, b_b_i, b_lambda, od_w_in, od_w_out, c_lb, c_onorm,
           d_qn, d_kn_cmp, d_kn_slc, d_kn_win, d_pe_k, d_pe_v, d_w1k, d_w2k, d_w1v, d_w2v):
    depth = norm_g.shape[0]
    assert depth == 2 and c_lb.shape[0] == 2, "the HGRN2 lower-bound formula in the kernel is written for depth 2"
    s = x.shape[1]
    assert s % 256 == 0 and s >= D_WINDOW
    pos = jnp.arange(s)
    cos, sin_s = _rope_tables(pos)
    nseg = s // CMP_STRIDE
    cmp_end = jnp.minimum(jnp.arange(nseg) * CMP_STRIDE + CMP_LEN - 1, s - 1)
    cosc, sinc = _rope_tables(cmp_end)
    h = _even_layer(x, mem, norm_g[0], mem_norm_g[0], mem_w_kv[0], mem_qn[0], mem_kn[0], ev_w_in[0], ev_w_out[0],
                    a_qn[0], a_kn[0], a_sinks[0], b_conv_w[0], b_conv_b[0], b_w_r[0], b_b_r[0], b_w_i[0], b_b_i[0],
                    b_lambda[0], cos, sin_s)
    h = _odd_layer(h, mem, norm_g[1], mem_norm_g[1], mem_w_kv[1], mem_qn[1], mem_kn[1], od_w_in[0], od_w_out[0],
                   c_lb, c_onorm[0], d_qn[0], d_kn_cmp[0], d_kn_slc[0], d_kn_win[0], d_pe_k[0], d_pe_v[0],
                   d_w1k[0], d_w2k[0], d_w1v[0], d_w2v[0], cos, sin_s, cosc, sinc)
    return h
```

```python
import numpy as np
import jax
import jax.numpy as jnp
from jax import lax
from jax.experimental import pallas as pl
from jax.experimental.pallas import tpu as pltpu

F32 = jnp.float32
BF16 = jnp.bfloat16

D_MODEL = 1024
N_MEM = 256
HEAD_DIM = 64
ROPE_THETA = 10000.0
EPS = 1e-6
NEG_INF = -1e30
POS_INF = 1e30
MASK_BIG = 1e30
LANES = 128

A_HEADS, A_KV_HEADS, A_WINDOW = 8, 2, 128
B_WIDTH, B_BLOCKS, B_CONV, B_C = 512, 8, 4, 8.0
M_HEADS = 4
C_HEADS, C_HEAD_DIM, C_CHUNK = 4, 128, 64
D_HEADS, D_KV_HEADS = 8, 2
CMP_LEN, CMP_STRIDE, CMP_HIDDEN = 32, 16, 128
SEL_LEN, SEL_TOPK = 64, 4
D_WINDOW = 512
D_BRANCHES = 3
SCALE = HEAD_DIM ** -0.5
LOG2E = 1.4426950408889634
Q_SCALE = SCALE * LOG2E

EVEN_ORDER = [("qa", 512), ("ga", 512), ("xb", 512), ("gb", 512), ("qm", 256), ("gm", 256), ("ka", 128), ("va", 128)]
EVEN_SRC = {"qa": (0, 512), "ka": (512, 128), "va": (640, 128), "ga": (768, 512), "xb": (1280, 512),
            "gb": (1792, 512), "qm": (2304, 256), "gm": (2560, 256)}
ODD_ORDER = [("qc", 512), ("fc", 512), ("ic", 512), ("gc", 512), ("qd", 512), ("gd", 512), ("qm", 256), ("gm", 256),
             ("kcd", 128), ("vcd", 128), ("ksd", 128), ("vsd", 128), ("kwd", 128), ("vwd", 128), ("gate", 128)]
ODD_SRC = {"qc": (0, 512), "fc": (512, 512), "ic": (1024, 512), "gc": (1536, 512), "qd": (2048, 512),
           "kcd": (2560, 128), "vcd": (2688, 128), "ksd": (2816, 128), "vsd": (2944, 128), "kwd": (3072, 128),
           "vwd": (3200, 128), "gate": (3328, 24), "gd": (3352, 512), "qm": (3864, 256), "gm": (4120, 256)}

VMEM_LIMIT = 48 * 1024 * 1024


def _offsets(order):
    off, out = 0, {}
    for name, w in order:
        out[name] = (off, w)
        off += w
    return out, off


EVEN_OFF, EVEN_COLS = _offsets(EVEN_ORDER)
ODD_OFF, ODD_COLS = _offsets(ODD_ORDER)


def _cparams(sem):
    return pltpu.CompilerParams(dimension_semantics=sem, vmem_limit_bytes=VMEM_LIMIT)


def _dot(a, b):
    return jnp.dot(a, b, preferred_element_type=F32)


def _dot_nt(a, b):
    return lax.dot_general(a, b, (((1,), (1,)), ((), ())), preferred_element_type=F32)


def _split(x):
    hi = x.astype(BF16)
    lo = (x - hi.astype(F32)).astype(BF16)
    return hi, lo


def _dot_split_lhs(x, m):
    hi, lo = _split(x)
    return _dot(hi, m) + _dot(lo, m)


def _dot_split_rhs(m, x):
    hi, lo = _split(x)
    return _dot(m, hi) + _dot(m, lo)


def _lane(shape):
    return lax.broadcasted_iota(jnp.int32, shape, len(shape) - 1)


def _row(shape):
    return lax.broadcasted_iota(jnp.int32, shape, len(shape) - 2)


def _silu(x):
    return x * jax.nn.sigmoid(x)


def _seg_ones():
    r = lax.broadcasted_iota(jnp.int32, (LANES, LANES), 0) >> 6
    c = lax.broadcasted_iota(jnp.int32, (LANES, LANES), 1) >> 6
    return jnp.where(r == c, 1.0, 0.0).astype(BF16)


def _head_rms(x, gain):
    seg = _seg_ones()
    cols = []
    for c in range(x.shape[1] // LANES):
        xc = x[:, c * LANES:(c + 1) * LANES]
        ms = _dot_split_lhs(xc * xc, seg) * (1.0 / HEAD_DIM)
        cols.append(xc * lax.rsqrt(ms + EPS))
    y = cols[0] if len(cols) == 1 else jnp.concatenate(cols, axis=1)
    return y * gain


def _rope(x, cos, sin_s):
    first = (_lane((x.shape[0], LANES)) & 63) < 32
    cols = []
    for c in range(x.shape[1] // LANES):
        xc = x[:, c * LANES:(c + 1) * LANES]
        sw = jnp.where(first, pltpu.roll(xc, 96, 1), pltpu.roll(xc, 32, 1))
        cols.append(xc * cos + sw * sin_s)
    return cols[0] if len(cols) == 1 else jnp.concatenate(cols, axis=1)


def _split_kv(k, fill=0.0):
    lo = _lane(k.shape) < 64
    return jnp.where(lo, k, fill), jnp.where(lo, pltpu.roll(k, 64, 1), fill)


def _q_heads(q, fills=None):
    lo = _lane((q.shape[0], LANES)) < 64
    out = []
    for c in range(q.shape[1] // LANES):
        qc = q[:, c * LANES:(c + 1) * LANES]
        fill = 0.0 if fills is None else fills[c]
        out.append(jnp.where(lo, qc, fill).astype(BF16))
        out.append(jnp.where(lo, pltpu.roll(qc, 64, 1), fill).astype(BF16))
    return out


VT_ROWS = HEAD_DIM + 16


def _with_ones_row(vt):
    n = vt.shape[1]
    pad = jnp.where(_row((VT_ROWS - HEAD_DIM, n)) == 0, 1.0, 0.0)
    return jnp.concatenate([vt, pad], axis=0)


def _exp_cols(s, extra=None):
    m = jnp.max(s, axis=0, keepdims=True)
    if extra is not None:
        m = jnp.maximum(m, extra)
    return jnp.exp2(s - m).astype(BF16), m


def _finish(acc, extra_den=None):
    den = acc[HEAD_DIM:HEAD_DIM + 1, :]
    if extra_den is not None:
        den = den + extra_den
    return acc[0:HEAD_DIM, :] * (1.0 / den)


def _online_update(state, s, vt_tile):
    m, acc = state
    p, m_new = _exp_cols(s, m)
    acc = jnp.exp2(m - m_new) * acc + _dot(vt_tile, p)
    return m_new, acc


def _tile4(x):
    return jnp.concatenate([x, x, x, x], axis=1)


def _inproj_kernel(x_ref, g_ref, w_ref, o_ref):
    x = x_ref[...]
    ms = jnp.mean(x * x, axis=-1, keepdims=True)
    xn = (x * lax.rsqrt(ms + EPS) * g_ref[...]).astype(BF16)
    n = o_ref.shape[-1]
    for c in range(0, n, 512):
        w = min(512, n - c)
        o_ref[:, c:c + w] = _dot(xn, w_ref[:, c:c + w]).astype(BF16)


def _inproj(x2d, gain, w_bf16, tm=512):
    n, d = x2d.shape
    nc = w_bf16.shape[1]
    return pl.pallas_call(
        _inproj_kernel,
        grid=(n // tm,),
        in_specs=[pl.BlockSpec((tm, d), lambda i: (i, 0)),
                  pl.BlockSpec((1, d), lambda i: (0, 0)),
                  pl.BlockSpec((d, nc), lambda i: (0, 0))],
        out_specs=pl.BlockSpec((tm, nc), lambda i: (i, 0)),
        out_shape=jax.ShapeDtypeStruct((n, nc), BF16),
        compiler_params=_cparams(("arbitrary",)),
        name="inproj",
    )(x2d, gain.reshape(1, d), w_bf16)


def _outproj_kernel(h_ref, a_ref, b_ref, m_ref, w_ref, o_ref):
    a, b, m = a_ref[...], b_ref[...], m_ref[...]
    for c in range(0, D_MODEL, 256):
        acc = h_ref[:, c:c + 256]
        acc += _dot(a, w_ref[0:512, c:c + 256])
        acc += _dot(b, w_ref[512:1024, c:c + 256])
        acc += _dot(m, w_ref[1024:1280, c:c + 256])
        o_ref[:, c:c + 256] = acc


def _outproj(h2d, oa, ob, om, w_bf16, tm=1024):
    n, d = h2d.shape
    return pl.pallas_call(
        _outproj_kernel,
        grid=(n // tm,),
        in_specs=[pl.BlockSpec((tm, d), lambda i: (i, 0)),
                  pl.BlockSpec((tm, 512), lambda i: (i, 0)),
                  pl.BlockSpec((tm, 512), lambda i: (i, 0)),
                  pl.BlockSpec((tm, 256), lambda i: (i, 0)),
                  pl.BlockSpec((1280, d), lambda i: (0, 0))],
        out_specs=pl.BlockSpec((tm, d), lambda i: (i, 0)),
        out_shape=jax.ShapeDtypeStruct((n, d), F32),
        compiler_params=_cparams(("arbitrary",)),
        name="outproj",
    )(h2d, oa, ob, om, w_bf16)


def _mem_kernel(q_ref, gm_ref, mem_ref, mg_ref, wkv_ref, qn_ref, kn_ref, o_ref, k_scr, vt_scr):
    @pl.when(pl.program_id(1) == 0)
    def _():
        m = mem_ref[0]
        ms = jnp.mean(m * m, axis=-1, keepdims=True)
        mn = (m * lax.rsqrt(ms + EPS) * mg_ref[...]).astype(BF16)
        kv = _dot(mn, wkv_ref[...])
        km = _head_rms(kv[:, 0:256], kn_ref[...])
        for c in range(2):
            k0, k1 = _split_kv(km[:, c * LANES:(c + 1) * LANES])
            k_scr[2 * c] = k0.astype(BF16)
            k_scr[2 * c + 1] = k1.astype(BF16)
        vt = kv[:, 256:512].T
        for h in range(M_HEADS):
            vt_scr[h] = _with_ones_row(vt[h * HEAD_DIM:(h + 1) * HEAD_DIM, :]).astype(BF16)

    q = _head_rms(q_ref[0].astype(F32), qn_ref[...]) * Q_SCALE
    scores = [_dot_nt(k_scr[h], qh) for h, qh in enumerate(_q_heads(q))]
    rows = []
    for h, sc in enumerate(scores):
        p, _ = _exp_cols(sc)
        rows.append(_finish(_dot(vt_scr[h], p)))
    o = jnp.concatenate(rows, axis=0).T
    o_ref[0] = (o * _silu(gm_ref[0].astype(F32))).astype(BF16)


def _mem_attention(z, q_off, g_off, mem, mem_g, wkv_bf16, qn, kn, tq=512):
    b, s, _ = z.shape
    return pl.pallas_call(
        _mem_kernel,
        grid=(b, s // tq),
        in_specs=[pl.BlockSpec((1, tq, 256), lambda bi, i: (bi, i, q_off // 256)),
                  pl.BlockSpec((1, tq, 256), lambda bi, i: (bi, i, g_off // 256)),
                  pl.BlockSpec((1, N_MEM, D_MODEL), lambda bi, i: (bi, 0, 0)),
                  pl.BlockSpec((1, D_MODEL), lambda bi, i: (0, 0)),
                  pl.BlockSpec((D_MODEL, 512), lambda bi, i: (0, 0)),
                  pl.BlockSpec((1, 256), lambda bi, i: (0, 0)),
                  pl.BlockSpec((1, 256), lambda bi, i: (0, 0))],
        out_specs=pl.BlockSpec((1, tq, 256), lambda bi, i: (bi, i, 0)),
        out_shape=jax.ShapeDtypeStruct((b, s, 256), BF16),
        scratch_shapes=[pltpu.VMEM((M_HEADS, N_MEM, LANES), BF16), pltpu.VMEM((M_HEADS, VT_ROWS, N_MEM), BF16)],
        compiler_params=_cparams(("arbitrary", "arbitrary")),
        name="mem_attention",
    )(z, z, mem, mem_g.reshape(1, D_MODEL), wkv_bf16, jnp.tile(qn, 4).reshape(1, 256), jnp.tile(kn, 4).reshape(1, 256))


def _swa_kernel(q_ref, g_ref, kc_ref, kp_ref, vc_ref, vp_ref, cosc_ref, sinc_ref, cosp_ref, sinp_ref,
                qn_ref, kn_ref, sink_ref, o_ref):
    i = pl.program_id(1)
    t = q_ref.shape[1]
    tp = kp_ref.shape[1]
    grp = A_HEADS // A_KV_HEADS
    q = _rope(_head_rms(q_ref[0].astype(F32), qn_ref[...]), cosc_ref[...], sinc_ref[...]) * Q_SCALE
    qh = _q_heads(q)
    kc = _rope(_head_rms(kc_ref[0].astype(F32), kn_ref[...]), cosc_ref[...], sinc_ref[...])
    kp = _rope(_head_rms(kp_ref[0].astype(F32), kn_ref[...]), cosp_ref[...], sinp_ref[...])
    ks = [x.astype(BF16) for x in _split_kv(jnp.concatenate([kp, kc], axis=0))]
    vt = jnp.concatenate([vp_ref[0].astype(F32).T, vc_ref[0].astype(F32).T], axis=1)
    ts = i * t - tp + _row((tp + t, t))
    rel = i * t + _lane((tp + t, t)) - ts
    bias = _tile4(jnp.where((rel >= 0) & (rel < A_WINDOW) & (ts >= 0), 0.0, NEG_INF))
    scores = [_dot_nt(ks[kv], jnp.concatenate(qh[kv * grp:(kv + 1) * grp], axis=0)) for kv in range(A_KV_HEADS)]
    rows = []
    for kv in range(A_KV_HEADS):
        sink = jnp.concatenate([jnp.broadcast_to(sink_ref[kv * grp + g:kv * grp + g + 1, 0:1], (1, t))
                                for g in range(grp)], axis=1) * LOG2E
        p, m = _exp_cols(scores[kv] + bias, sink)
        vt_kv = _with_ones_row(vt[kv * HEAD_DIM:(kv + 1) * HEAD_DIM, :]).astype(BF16)
        o_t = _finish(_dot(vt_kv, p), jnp.exp2(sink - m))
        rows.extend(o_t[:, g * t:(g + 1) * t] for g in range(grp))
    o = jnp.concatenate(rows, axis=0).T
    o_ref[0] = (o * _silu(g_ref[0].astype(F32))).astype(BF16)


def _swa(z, cos, sin_s, qn, kn, sinks, t=256):
    b, s, _ = z.shape
    tp = A_WINDOW
    qo, go = EVEN_OFF["qa"][0] // 512, EVEN_OFF["ga"][0] // 512
    ko, vo = EVEN_OFF["ka"][0] // 128, EVEN_OFF["va"][0] // 128
    prev = lambda i: jnp.maximum(i * (t // tp) - 1, 0)
    return pl.pallas_call(
        _swa_kernel,
        grid=(b, s // t),
        in_specs=[pl.BlockSpec((1, t, 512), lambda bi, i: (bi, i, qo)),
                  pl.BlockSpec((1, t, 512), lambda bi, i: (bi, i, go)),
                  pl.BlockSpec((1, t, 128), lambda bi, i: (bi, i, ko)),
                  pl.BlockSpec((1, tp, 128), lambda bi, i: (bi, prev(i), ko)),
                  pl.BlockSpec((1, t, 128), lambda bi, i: (bi, i, vo)),
                  pl.BlockSpec((1, tp, 128), lambda bi, i: (bi, prev(i), vo)),
                  pl.BlockSpec((t, 128), lambda bi, i: (i, 0)),
                  pl.BlockSpec((t, 128), lambda bi, i: (i, 0)),
                  pl.BlockSpec((tp, 128), lambda bi, i: (prev(i), 0)),
                  pl.BlockSpec((tp, 128), lambda bi, i: (prev(i), 0)),
                  pl.BlockSpec((1, 512), lambda bi, i: (0, 0)),
                  pl.BlockSpec((1, 128), lambda bi, i: (0, 0)),
                  pl.BlockSpec((A_HEADS, 128), lambda bi, i: (0, 0))],
        out_specs=pl.BlockSpec((1, t, 512), lambda bi, i: (bi, i, 0)),
        out_shape=jax.ShapeDtypeStruct((b, s, 512), BF16),
        compiler_params=_cparams(("arbitrary", "arbitrary")),
        name="swa",
    )(z, z, z, z, z, z, cos, sin_s, cos, sin_s, jnp.tile(qn, 8).reshape(1, 512), jnp.tile(kn, 2).reshape(1, 128),
      jnp.broadcast_to(sinks.reshape(A_HEADS, 1), (A_HEADS, 128)))


def _rglru_kernel(x_ref, g_ref, cw_ref, cb_ref, wr_ref, br_ref, wi_ref, bi_ref, lam_ref, o_ref,
                  xbuf, hcar, a_scr, u_scr, h_scr):
    t = x_ref.shape[1]

    @pl.when(pl.program_id(1) == 0)
    def _():
        xbuf[0:8, :] = jnp.zeros((8, B_WIDTH), F32)
        hcar[...] = jnp.zeros((8, B_WIDTH), F32)

    xbuf[8:t + 8, :] = x_ref[0].astype(F32)
    xc = cb_ref[...] + cw_ref[0:1, :] * xbuf[5:t + 5, :]
    for j in range(1, B_CONV):
        xc = xc + cw_ref[j:j + 1, :] * xbuf[5 + j:t + 5 + j, :]
    xbuf[0:8, :] = xbuf[t:t + 8, :]

    r_cols, i_cols = [], []
    for c in range(B_WIDTH // LANES):
        xcc = xc[:, c * LANES:(c + 1) * LANES].astype(BF16)
        r_cols.append(_dot(xcc, wr_ref[c]))
        i_cols.append(_dot(xcc, wi_ref[c]))
    r = jax.nn.sigmoid(jnp.concatenate(r_cols, axis=1) + br_ref[...])
    ig = jax.nn.sigmoid(jnp.concatenate(i_cols, axis=1) + bi_ref[...])
    nl = -lam_ref[...]
    softplus = jnp.maximum(nl, 0.0) + jnp.log(1.0 + jnp.exp(-jnp.abs(nl)))
    log_a = -B_C * r * softplus
    a = jnp.exp(log_a)
    u = jnp.sqrt(1.0 - jnp.exp(2.0 * log_a)) * (ig * xc)

    r8 = _row((t, B_WIDTH)) & 7
    for d in (1, 2, 4):
        a_sh = pltpu.roll(a, d, 0)
        u_sh = pltpu.roll(u, d, 0)
        m = r8 >= d
        u = jnp.where(m, a * u_sh + u, u)
        a = jnp.where(m, a * a_sh, a)
    a_scr[...] = a
    u_scr[...] = u

    def body(j, h):
        off = pl.multiple_of(j * 8, 8)
        hh = a_scr[pl.ds(off, 8), :] * h + u_scr[pl.ds(off, 8), :]
        h_scr[pl.ds(off, 8), :] = hh
        return hh[7:8, :]

    h_last = lax.fori_loop(0, t // 8, body, hcar[0:1, :])
    hcar[0:1, :] = h_last
    o_ref[0] = (h_scr[...] * _silu(g_ref[0].astype(F32))).astype(BF16)


def _rglru(z, conv_w, conv_b, wr_bd, b_r, wi_bd, b_i, lam, t=512):
    b, s, _ = z.shape
    xo, go = EVEN_OFF["xb"][0] // 512, EVEN_OFF["gb"][0] // 512
    row = lambda v: v.reshape(1, B_WIDTH)
    full = lambda shp: pl.BlockSpec(shp, lambda bi, i: (0,) * len(shp))
    return pl.pallas_call(
        _rglru_kernel,
        grid=(b, s // t),
        in_specs=[pl.BlockSpec((1, t, 512), lambda bi, i: (bi, i, xo)),
                  pl.BlockSpec((1, t, 512), lambda bi, i: (bi, i, go)),
                  full((B_CONV, B_WIDTH)), full((1, B_WIDTH)),
                  full((4, LANES, LANES)), full((1, B_WIDTH)),
                  full((4, LANES, LANES)), full((1, B_WIDTH)), full((1, B_WIDTH))],
        out_specs=pl.BlockSpec((1, t, 512), lambda bi, i: (bi, i, 0)),
        out_shape=jax.ShapeDtypeStruct((b, s, 512), BF16),
        scratch_shapes=[pltpu.VMEM((t + 8, B_WIDTH), F32), pltpu.VMEM((8, B_WIDTH), F32),
                        pltpu.VMEM((t, B_WIDTH), F32), pltpu.VMEM((t, B_WIDTH), F32), pltpu.VMEM((t, B_WIDTH), F32)],
        compiler_params=_cparams(("arbitrary", "arbitrary")),
        name="rglru",
    )(z, z, conv_w, row(conv_b), wr_bd, row(b_r), wi_bd, row(b_i), row(lam))


def _hgrn_consts():
    c = C_CHUNK
    t = np.arange(c)[:, None]
    s = np.arange(c)[None, :]
    mats = [(s <= t)]
    masks = []
    hs = c // 2
    while hs >= 1:
        mid = (t // (2 * hs)) * 2 * hs + hs - 1
        mats.append(s <= mid)
        same = (t // (2 * hs)) == (s // (2 * hs))
        masks.append(same & ((t // hs) % 2 == 1) & ((s // hs) % 2 == 0))
        hs //= 2
    pairs = [np.concatenate(masks[n:n + 2], axis=1) for n in range(0, len(masks), 2)]
    return np.concatenate(mats, axis=0).astype(np.float32), np.stack(pairs).astype(np.float32)


def _hgrn_kernel(q_ref, f_ref, i_ref, g_ref, lb_ref, og_ref, mst_ref, msk_ref, o_ref, st_scr):
    @pl.when(pl.program_id(1) == 0)
    def _():
        st_scr[...] = jnp.zeros(st_scr.shape, F32)

    c = C_CHUNK
    npair = msk_ref.shape[0]
    p = lb_ref[...]
    pm = jnp.maximum(p[0:1, :], p[1:2, :])
    e0, e1 = jnp.exp(p[0:1, :] - pm), jnp.exp(p[1:2, :] - pm)
    lb = e1 / (e0 + e1)
    zeros = jnp.zeros((c, LANES), BF16)
    for ch in range(q_ref.shape[1] // c):
        rows = slice(ch * c, (ch + 1) * c)
        f = lb + (1.0 - lb) * jax.nn.sigmoid(f_ref[0, rows, :].astype(F32))
        g = jnp.log(f)
        g_hi, g_lo = _split(g)
        b_all = _dot(mst_ref[0:c, :], g_hi) + _dot(mst_ref[0:c, :], g_lo)
        bm_all = _dot(mst_ref[c:, :], g_hi)
        kk_all = 1.0 - f
        qf_all = _silu(q_ref[0, rows, :].astype(F32))
        v_all = i_ref[0, rows, :].astype(F32)
        outs = []
        for h in range(C_HEADS):
            col = slice(h * LANES, (h + 1) * LANES)
            qf, kk, v = qf_all[:, col], kk_all[:, col], v_all[:, col]
            b = b_all[:, col]
            st = st_scr[h]
            o = _dot_nt((qf * jnp.exp(b)).astype(BF16), st.astype(BF16))
            att = []
            for n in range(npair):
                qt, kt = [], []
                for l in (2 * n, 2 * n + 1):
                    bm = bm_all[l * c:(l + 1) * c, col]
                    qt.append((qf * jnp.exp(jnp.minimum(b - bm, 1.0))).astype(BF16))
                    kt.append((kk * jnp.exp(jnp.minimum(bm - b, 1.0))).astype(BF16))
                lhs = jnp.concatenate(qt, axis=1)
                rhs = jnp.concatenate([jnp.concatenate([kt[0], zeros], axis=1),
                                       jnp.concatenate([zeros, kt[1]], axis=1)], axis=0)
                att.append(jnp.where(msk_ref[n] > 0.5, _dot_nt(lhs, rhs), 0.0).astype(BF16))
            diag = jnp.sum(qf * kk, axis=-1, keepdims=True)
            vb = v.astype(BF16)
            o = o + _dot(jnp.concatenate(att, axis=1), jnp.concatenate([vb] * (2 * npair), axis=0)) + diag * v
            b_last = b[c - 1:c, :]
            kd = (kk * jnp.exp(b_last - b)).astype(BF16)
            st_scr[h] = st * jnp.exp(b_last) + _dot(v.T.astype(BF16), kd)
            ms = jnp.mean(o * o, axis=-1, keepdims=True)
            outs.append(o * lax.rsqrt(ms + EPS) * og_ref[...])
        o_ref[0, rows, :] = (jnp.concatenate(outs, axis=1) * _silu(g_ref[0, rows, :].astype(F32))).astype(BF16)


def _hgrn(z, c_lb, c_og, t=512):
    b, s, _ = z.shape
    mst, msk = _hgrn_consts()
    blk = lambda name: pl.BlockSpec((1, t, 512), lambda bi, i, o=ODD_OFF[name][0] // 512: (bi, i, o))
    full = lambda shp: pl.BlockSpec(shp, lambda bi, i: (0,) * len(shp))
    return pl.pallas_call(
        _hgrn_kernel,
        grid=(b, s // t),
        in_specs=[blk("qc"), blk("fc"), blk("ic"), blk("gc"), full(c_lb.shape), full((1, C_HEAD_DIM)),
                  full(mst.shape), full(msk.shape)],
        out_specs=pl.BlockSpec((1, t, 512), lambda bi, i: (bi, i, 0)),
        out_shape=jax.ShapeDtypeStruct((b, s, 512), BF16),
        scratch_shapes=[pltpu.VMEM((C_HEADS, C_HEAD_DIM, C_HEAD_DIM), F32)],
        compiler_params=_cparams(("arbitrary", "arbitrary")),
        name="hgrn2",
    )(z, z, z, z, c_lb, c_og.reshape(1, C_HEAD_DIM), jnp.asarray(mst, BF16), jnp.asarray(msk, F32))


def _nsa_prep_kernel(kcr_ref, vcr_ref, ks_ref, vs_ref, kw_ref, vw_ref, cos_ref, sin_ref, cosc_ref, sinc_ref,
                     kncmp_ref, knslc_ref, knwin_ref, pek_ref, pev_ref, w1k_ref, w2k_ref, w1v_ref, w2vt_ref,
                     kc_o, vct_o, ks_o, vst_o, kw_o, vwt_o):
    def hidden(xr_ref, pe_ref, w1_ref):
        xr = xr_ref[0].astype(F32)
        top = _dot((xr + pe_ref[0:1, :]).astype(BF16), w1_ref[0])
        bot = _dot((xr + pe_ref[1:2, :]).astype(BF16), w1_ref[1])
        pre = top + pltpu.roll(bot, bot.shape[0] - 1, 0)
        return _silu(pre).astype(BF16)

    kc = _dot(hidden(kcr_ref, pek_ref, w1k_ref), w2k_ref[...])
    kc = _rope(_head_rms(kc, kncmp_ref[...]), cosc_ref[...], sinc_ref[...])
    d0, d1 = _split_kv(kc)
    kc_o[0, 0] = d0.astype(BF16)
    kc_o[0, 1] = d1.astype(BF16)
    vct = _dot_nt(w2vt_ref[...], hidden(vcr_ref, pev_ref, w1v_ref))
    for kv in range(D_KV_HEADS):
        vct_o[0, kv] = _with_ones_row(vct[kv * HEAD_DIM:(kv + 1) * HEAD_DIM, :]).astype(BF16)

    s = ks_ref.shape[1]
    step = 256
    for r0 in range(0, s, step):
        rows = slice(r0, r0 + step)
        cos, sin = cos_ref[rows, :], sin_ref[rows, :]
        ks = _rope(_head_rms(ks_ref[0, rows, :].astype(F32), knslc_ref[...]), cos, sin)
        kw = _rope(_head_rms(kw_ref[0, rows, :].astype(F32), knwin_ref[...]), cos, sin)
        blk = (r0 + _row((step, LANES))) // SEL_LEN
        onehot = jnp.where(_lane((step, LANES)) - HEAD_DIM == blk, MASK_BIG, 0.0)
        for src, dst, fill in ((ks, ks_o, onehot), (kw, kw_o, 0.0)):
            d0, d1 = _split_kv(src, fill)
            dst[0, 0, rows, :] = d0.astype(BF16)
            dst[0, 1, rows, :] = d1.astype(BF16)
        for src, dst in ((vs_ref, vst_o), (vw_ref, vwt_o)):
            vt = src[0, rows, :].astype(F32).T
            for kv in range(D_KV_HEADS):
                dst[0, kv, :, rows] = _with_ones_row(vt[kv * HEAD_DIM:(kv + 1) * HEAD_DIM, :]).astype(BF16)


def _nsa_prep(z, kcr, vcr, cos, sin_s, cosc, sinc, kn_cmp, kn_slc, kn_win, pek, pev, w1k, w2k, w1v, w2v):
    b, s, _ = z.shape
    nseg = s // CMP_STRIDE
    zb = lambda name: pl.BlockSpec((1, s, 128), lambda bi, o=ODD_OFF[name][0] // 128: (bi, 0, o))
    full = lambda shp: pl.BlockSpec(shp, lambda bi: (0,) * len(shp))
    seg = pl.BlockSpec((1, nseg, 2048), lambda bi: (bi, 0, 0))
    g2 = lambda g: jnp.tile(g, 2).reshape(1, 128)
    dup = lambda n: (jax.ShapeDtypeStruct((b, 2, n, 128), BF16), pl.BlockSpec((1, 2, n, 128), lambda bi: (bi, 0, 0, 0)))
    tr = lambda n: (jax.ShapeDtypeStruct((b, 2, VT_ROWS, n), BF16),
                    pl.BlockSpec((1, 2, VT_ROWS, n), lambda bi: (bi, 0, 0, 0)))
    outs = [dup(nseg), tr(nseg), dup(s), tr(s), dup(s), tr(s)]
    return pl.pallas_call(
        _nsa_prep_kernel,
        grid=(b,),
        in_specs=[seg, seg, zb("ksd"), zb("vsd"), zb("kwd"), zb("vwd"),
                  full((s, 128)), full((s, 128)), full((nseg, 128)), full((nseg, 128)),
                  full((1, 128)), full((1, 128)), full((1, 128)), full((2, 2048)), full((2, 2048)),
                  full((2, 2048, 256)), full((256, 128)), full((2, 2048, 256)), full((128, 256))],
        out_specs=[o[1] for o in outs],
        out_shape=[o[0] for o in outs],
        compiler_params=_cparams(("arbitrary",)),
        name="nsa_prep",
    )(kcr, vcr, z, z, z, z, cos, sin_s, cosc, sinc, g2(kn_cmp), g2(kn_slc), g2(kn_win), pek, pev, w1k, w2k, w1v, w2v)


SEL_CHUNK = 512
SEL_PIECE = 128


def _nsa_kernel(q_ref, gd_ref, gate_ref, cos_ref, sin_ref, qn_ref, kc_ref, vct_ref, ks_ref, vst_ref, kw_ref, vwt_ref,
                ovlt_ref, o_ref):
    i = pl.program_id(1)
    t = q_ref.shape[1]
    grp = D_HEADS // D_KV_HEADS
    w = grp * t
    kvs = range(D_KV_HEADS)
    q = _rope(_head_rms(q_ref[0].astype(F32), qn_ref[...]), cos_ref[...], sin_ref[...]) * Q_SCALE
    stack = lambda heads: [jnp.concatenate(heads[kv * grp:(kv + 1) * grp], axis=0) for kv in kvs]
    qs = stack(_q_heads(q))
    tq1 = i * t + _lane((1, t))

    nwin = D_WINDOW + t
    s_cmp = [_dot_nt(kc_ref[0, kv], qs[kv]) for kv in kvs]

    ncmp = kc_ref.shape[2] - 1
    nsel = ovlt_ref.shape[0]
    crow = _row((kc_ref.shape[2], t))
    bias_c = _tile4(jnp.where((crow * CMP_STRIDE + (CMP_LEN - 1) <= tq1) & (crow < ncmp), 0.0, NEG_INF))
    row_ok = _tile4(tq1 >= CMP_LEN - 1)
    jrow = _row((nsel, t))
    jrow_f = jrow.astype(F32)
    cur = (i * t + _lane((nsel, t))) // SEL_LEN
    forced = (jrow == 0) | (jrow == cur)
    oc_t, fills = [], []
    for kv in kvs:
        e, _ = _exp_cols(s_cmp[kv] + bias_c)
        acc = _dot(vct_ref[0, kv], e)
        inv = jnp.where(row_ok, 1.0 / acc[HEAD_DIM:HEAD_DIM + 1, :], 0.0)
        oc_t.append(acc[0:HEAD_DIM, :] * inv)
        p = e.astype(F32) * inv
        psum = p[:, 0:t] + p[:, t:2 * t] + p[:, 2 * t:3 * t] + p[:, 3 * t:4 * t]
        imp = _dot_split_rhs(ovlt_ref[...], psum)
        score = jnp.where(forced, POS_INF, jnp.where(jrow <= cur, imp, NEG_INF))
        chosen = jnp.zeros((nsel, t), F32)
        for _ in range(min(SEL_TOPK, nsel)):
            mx = jnp.max(score, axis=0, keepdims=True)
            first = jnp.min(jnp.where(score == mx, jrow_f, 1e9), axis=0, keepdims=True)
            hit = jrow_f == first
            chosen = jnp.where(hit, 1.0, chosen)
            score = jnp.where(hit, -3e38, score)
        frame = jnp.concatenate([jnp.zeros((HEAD_DIM, t), F32), chosen - 1.0,
                                 jnp.zeros((LANES - HEAD_DIM - nsel, t), F32)], axis=0)
        fills.extend([frame.T] * (grp // 2))

    init = (jnp.full((1, w), NEG_INF, F32), jnp.zeros((VT_ROWS, w), F32))
    pc = SEL_PIECE

    def sweep(states, k_ref, vt_ref, queries, offs, biases, n):
        scores = [[_dot_nt(k_ref[0, kv, pl.ds(off, n), :], queries[kv]) for kv in kvs] for off in offs]
        for off, sc, bias in zip(offs, scores, biases):
            states = tuple(_online_update(states[kv], sc[kv] if bias is None else sc[kv] + bias,
                                          vt_ref[0, kv, :, pl.ds(off, n)]) for kv in kvs)
        return states

    win0 = pl.multiple_of(jnp.maximum(i * t + t - nwin, 0), t)
    rel = tq1 - (win0 + _row((nwin, t)))
    bias_w = _tile4(jnp.where((rel >= 0) & (rel < D_WINDOW), 0.0, NEG_INF))
    s_win = [_dot_nt(kw_ref[0, kv, pl.ds(win0, nwin), :], qs[kv]) for kv in kvs]
    ow_t = [_finish(_dot(vwt_ref[0, kv, :, pl.ds(win0, nwin)], _exp_cols(s_win[kv] + bias_w)[0])) for kv in kvs]

    qsel = stack(_q_heads(q, fills))
    ch = SEL_CHUNK
    own = pl.multiple_of(((i * t) // ch) * ch, ch)
    st = sweep((init, init), ks_ref, vst_ref, qsel, [own],
               [_tile4(jnp.where(own + _row((ch, t)) <= tq1, 0.0, NEG_INF))], ch)

    def sel_chunk(c, states):
        base = pl.multiple_of(c * ch, ch)
        return sweep(states, ks_ref, vst_ref, qsel, [base + n * pc for n in range(ch // pc)], [None] * (ch // pc), pc)

    st = lax.fori_loop(0, (i * t) // ch, sel_chunk, st)
    os_t = [_finish(acc) for (_, acc) in st]

    g_t = jax.nn.sigmoid(gate_ref[0].astype(F32)).T
    rows = []
    for h in range(D_HEADS):
        kv, g = divmod(h, grp)
        cols = slice(g * t, (g + 1) * t)
        r = D_BRANCHES * h
        rows.append(g_t[r:r + 1, :] * oc_t[kv][:, cols] + g_t[r + 1:r + 2, :] * os_t[kv][:, cols]
                    + g_t[r + 2:r + 3, :] * ow_t[kv][:, cols])
    o = jnp.concatenate(rows, axis=0).T
    o_ref[0] = (o * _silu(gd_ref[0].astype(F32))).astype(BF16)


def _overlap_t(s):
    ncmp = (s - CMP_LEN) // CMP_STRIDE + 1
    nsel = s // SEL_LEN
    cs = np.arange(ncmp)[None, :] * CMP_STRIDE
    ss = np.arange(nsel)[:, None] * SEL_LEN
    ovl = np.zeros((nsel, s // CMP_STRIDE), np.float32)
    ovl[:, :ncmp] = (cs < ss + SEL_LEN) & (cs + CMP_LEN > ss)
    return ovl


def _nsa(z, cos, sin_s, qn, kc, vct, ks, vst, kw, vwt, t=256):
    b, s, _ = z.shape
    nseg = s // CMP_STRIDE
    assert nseg == LANES, "compressed-block scores are laid out on one 128-row tile"
    ovlt = _overlap_t(s)
    full = lambda shp: pl.BlockSpec(shp, lambda bi, i: (0,) * len(shp))
    dup = lambda n: pl.BlockSpec((1, 2, n, 128), lambda bi, i: (bi, 0, 0, 0))
    tr = lambda n: pl.BlockSpec((1, 2, VT_ROWS, n), lambda bi, i: (bi, 0, 0, 0))
    return pl.pallas_call(
        _nsa_kernel,
        grid=(b, s // t),
        in_specs=[pl.BlockSpec((1, t, 512), lambda bi, i: (bi, i, ODD_OFF["qd"][0] // 512)),
                  pl.BlockSpec((1, t, 512), lambda bi, i: (bi, i, ODD_OFF["gd"][0] // 512)),
                  pl.BlockSpec((1, t, 128), lambda bi, i: (bi, i, ODD_OFF["gate"][0] // 128)),
                  pl.BlockSpec((t, 128), lambda bi, i: (i, 0)),
                  pl.BlockSpec((t, 128), lambda bi, i: (i, 0)),
                  full((1, 512)), dup(nseg), tr(nseg), dup(s), tr(s), dup(s), tr(s), full(ovlt.shape)],
        out_specs=pl.BlockSpec((1, t, 512), lambda bi, i: (bi, i, 0)),
        out_shape=jax.ShapeDtypeStruct((b, s, 512), BF16),
        compiler_params=_cparams(("arbitrary", "arbitrary")),
        name="nsa",
    )(z, z, z, cos, sin_s, jnp.tile(qn, 8).reshape(1, 512), kc, vct, ks, vst, kw, vwt, jnp.asarray(ovlt, BF16))


def _permute_cols(w, order, src):
    cols = []
    for name, width in order:
        o, sw = src[name]
        blk = w[:, o:o + sw]
        if sw < width:
            blk = jnp.pad(blk, ((0, 0), (0, width - sw)))
        cols.append(blk)
    return jnp.concatenate(cols, axis=1).astype(BF16)


def _block_diag_pairs(w):
    z = jnp.zeros((4, LANES, LANES), w.dtype)
    z = z.at[:, 0:64, 0:64].set(w[0::2])
    z = z.at[:, 64:128, 64:128].set(w[1::2])
    return z.astype(BF16)


def _rope_tables(pos):
    half = HEAD_DIM // 2
    inv = ROPE_THETA ** (-jnp.arange(half, dtype=F32) / half)
    ang = pos.astype(F32)[:, None] * inv[None, :]
    cos, sin = jnp.cos(ang), jnp.sin(ang)
    cos_t = jnp.tile(cos, (1, 4))
    sin_t = jnp.tile(jnp.concatenate([-sin, sin], axis=1), (1, 2))
    return cos_t, sin_t


def _expand_compress_w1(w1):
    hdim = w1.shape[1]
    w = w1.reshape(2, CMP_STRIDE, HEAD_DIM, hdim)
    z = jnp.zeros((2, CMP_STRIDE, 2, HEAD_DIM, 2, hdim), w1.dtype)
    z = z.at[:, :, 0, :, 0, :].set(w)
    z = z.at[:, :, 1, :, 1, :].set(w)
    return z.reshape(2, CMP_STRIDE * 2 * HEAD_DIM, 2 * hdim).astype(BF16)


def _expand_compress_w2(w2):
    hdim, hd = w2.shape
    z = jnp.zeros((2 * hdim, 2 * hd), w2.dtype)
    z = z.at[0:hdim, 0:hd].set(w2)
    z = z.at[hdim:, hd:].set(w2)
    return z.astype(BF16)


def _expand_pe(pe):
    p = pe.reshape(2, CMP_STRIDE, 1, HEAD_DIM)
    return jnp.broadcast_to(p, (2, CMP_STRIDE, 2, HEAD_DIM)).reshape(2, CMP_STRIDE * 2 * HEAD_DIM)


def _even_layer(h, mem, g, mem_g, w_mem_kv, m_qn, m_kn, w_in, w_out, a_qn, a_kn, a_sinks,
                conv_w, conv_b, w_r, b_r, w_i, b_i, lam, cos, sin_s):
    b, s, d = h.shape
    h2 = h.reshape(b * s, d)
    z = _inproj(h2, g, _permute_cols(w_in, EVEN_ORDER, EVEN_SRC)).reshape(b, s, EVEN_COLS)
    oa = _swa(z, cos, sin_s, a_qn, a_kn, a_sinks)
    ob = _rglru(z, conv_w, conv_b, _block_diag_pairs(w_r), b_r, _block_diag_pairs(w_i), b_i, lam)
    om = _mem_attention(z, EVEN_OFF["qm"][0], EVEN_OFF["gm"][0], mem, mem_g, w_mem_kv.astype(BF16), m_qn, m_kn)
    out = _outproj(h2, oa.reshape(b * s, 512), ob.reshape(b * s, 512), om.reshape(b * s, 256), w_out.astype(BF16))
    return out.reshape(b, s, d)


def _odd_layer(h, mem, g, mem_g, w_mem_kv, m_qn, m_kn, w_in, w_out, c_lb, c_og,
               d_qn, d_kn_cmp, d_kn_slc, d_kn_win, pe_k, pe_v, w1k, w2k, w1v, w2v, cos, sin_s, cosc, sinc):
    b, s, d = h.shape
    h2 = h.reshape(b * s, d)
    z = _inproj(h2, g, _permute_cols(w_in, ODD_ORDER, ODD_SRC)).reshape(b, s, ODD_COLS)
    oc = _hgrn(z, c_lb, c_og)
    nseg = s // CMP_STRIDE
    seg = lambda name: z[:, :, ODD_OFF[name][0]:ODD_OFF[name][0] + 128].reshape(b, nseg, CMP_STRIDE * 128)
    kc, vc, ks, vs, kw, vw = _nsa_prep(
        z, seg("kcd"), seg("vcd"), cos, sin_s, cosc, sinc, d_kn_cmp, d_kn_slc, d_kn_win,
        _expand_pe(pe_k), _expand_pe(pe_v), _expand_compress_w1(w1k), _expand_compress_w2(w2k),
        _expand_compress_w1(w1v), _expand_compress_w2(w2v).T)
    od = _nsa(z, cos, sin_s, d_qn, kc, vc, ks, vs, kw, vw)
    om = _mem_attention(z, ODD_OFF["qm"][0], ODD_OFF["gm"][0], mem, mem_g, w_mem_kv.astype(BF16), m_qn, m_kn)
    out = _outproj(h2, oc.reshape(b * s, 512), od.reshape(b * s, 512), om.reshape(b * s, 256), w_out.astype(BF16))
    return out.reshape(b, s, d)


def kernel(x, mem, norm_g, mem_norm_g, mem_w_kv, mem_qn, mem_kn, ev_w_in, ev_w_out, a_qn, a_kn, a_sinks,
           b_conv_w, b_conv_b, b_w_r, b_b_r, b_w_i, b_b_i, b_lambda, od_w_in, od_w_out, c_lb, c_onorm,
           d_qn, d_kn_cmp, d_kn_slc, d_kn_win, d_pe_k, d_pe_v, d_w1k, d_w2k, d_w1v, d_w2v):
    depth = norm_g.shape[0]
    assert depth == 2 and c_lb.shape[0] == 2, "the HGRN2 lower-bound formula in the kernel is written for depth 2"
    s = x.shape[1]
    assert s % 256 == 0 and s >= D_WINDOW
    pos = jnp.arange(s)
    cos, sin_s = _rope_tables(pos)
    nseg = s // CMP_STRIDE
    cmp_end = jnp.minimum(jnp.arange(nseg) * CMP_STRIDE + CMP_LEN - 1, s - 1)
    cosc, sinc = _rope_tables(cmp_end)
    h = _even_layer(x, mem, norm_g[0], mem_norm_g[0], mem_w_kv[0], mem_qn[0], mem_kn[0], ev_w_in[0], ev_w_out[0],
                    a_qn[0], a_kn[0], a_sinks[0], b_conv_w[0], b_conv_b[0], b_w_r[0], b_b_r[0], b_w_i[0], b_b_i[0],
                    b_lambda[0], cos, sin_s)
    h = _odd_layer(h, mem, norm_g[1], mem_norm_g[1], mem_w_kv[1], mem_qn[1], mem_kn[1], od_w_in[0], od_w_out[0],
                   c_lb, c_onorm[0], d_qn[0], d_kn_cmp[0], d_kn_slc[0], d_kn_win[0], d_pe_k[0], d_pe_v[0],
                   d_w1k[0], d_w2k[0], d_w1v[0], d_w2v[0], cos, sin_s, cosc, sinc)
    return h
```

```python
import numpy as np
import jax
import jax.numpy as jnp
from jax import lax
from jax.experimental import pallas as pl
from jax.experimental.pallas import tpu as pltpu

F32 = jnp.float32
BF16 = jnp.bfloat16

D_MODEL = 1024
N_MEM = 256
HEAD_DIM = 64
ROPE_THETA = 10000.0
EPS = 1e-6
NEG_INF = -1e30
POS_INF = 1e30
MASK_BIG = 1e30
LANES = 128

A_HEADS, A_KV_HEADS, A_WINDOW = 8, 2, 128
B_WIDTH, B_BLOCKS, B_CONV, B_C = 512, 8, 4, 8.0
M_HEADS = 4
C_HEADS, C_HEAD_DIM, C_CHUNK = 4, 128, 64
D_HEADS, D_KV_HEADS = 8, 2
CMP_LEN, CMP_STRIDE, CMP_HIDDEN = 32, 16, 128
SEL_LEN, SEL_TOPK = 64, 4
D_WINDOW = 512
D_BRANCHES = 3
SCALE = HEAD_DIM ** -0.5
LOG2E = 1.4426950408889634
Q_SCALE = SCALE * LOG2E

EVEN_ORDER = [("qa", 512), ("ga", 512), ("xb", 512), ("gb", 512), ("qm", 256), ("gm", 256), ("ka", 128), ("va", 128)]
EVEN_SRC = {"qa": (0, 512), "ka": (512, 128), "va": (640, 128), "ga": (768, 512), "xb": (1280, 512),
            "gb": (1792, 512), "qm": (2304, 256), "gm": (2560, 256)}
ODD_ORDER = [("qc", 512), ("fc", 512), ("ic", 512), ("gc", 512), ("qd", 512), ("gd", 512), ("qm", 256), ("gm", 256),
             ("kcd", 128), ("vcd", 128), ("ksd", 128), ("vsd", 128), ("kwd", 128), ("vwd", 128), ("gate", 128)]
ODD_SRC = {"qc": (0, 512), "fc": (512, 512), "ic": (1024, 512), "gc": (1536, 512), "qd": (2048, 512),
           "kcd": (2560, 128), "vcd": (2688, 128), "ksd": (2816, 128), "vsd": (2944, 128), "kwd": (3072, 128),
           "vwd": (3200, 128), "gate": (3328, 24), "gd": (3352, 512), "qm": (3864, 256), "gm": (4120, 256)}

VMEM_LIMIT = 48 * 1024 * 1024


def _offsets(order):
    off, out = 0, {}
    for name, w in order:
        out[name] = (off, w)
        off += w
    return out, off


EVEN_OFF, EVEN_COLS = _offsets(EVEN_ORDER)
ODD_OFF, ODD_COLS = _offsets(ODD_ORDER)


def _cparams(sem):
    return pltpu.CompilerParams(dimension_semantics=sem, vmem_limit_bytes=VMEM_LIMIT)


def _dot(a, b):
    return jnp.dot(a, b, preferred_element_type=F32)


def _dot_nt(a, b):
    return lax.dot_general(a, b, (((1,), (1,)), ((), ())), preferred_element_type=F32)


def _split(x):
    hi = x.astype(BF16)
    lo = (x - hi.astype(F32)).astype(BF16)
    return hi, lo


def _dot_split_lhs(x, m):
    hi, lo = _split(x)
    return _dot(hi, m) + _dot(lo, m)


def _dot_split_rhs(m, x):
    hi, lo = _split(x)
    return _dot(m, hi) + _dot(m, lo)


def _lane(shape):
    return lax.broadcasted_iota(jnp.int32, shape, len(shape) - 1)


def _row(shape):
    return lax.broadcasted_iota(jnp.int32, shape, len(shape) - 2)


def _silu(x):
    return x * jax.nn.sigmoid(x)


def _seg_ones():
    r = lax.broadcasted_iota(jnp.int32, (LANES, LANES), 0) >> 6
    c = lax.broadcasted_iota(jnp.int32, (LANES, LANES), 1) >> 6
    return jnp.where(r == c, 1.0, 0.0).astype(BF16)


def _head_rms(x, gain):
    seg = _seg_ones()
    cols = []
    for c in range(x.shape[1] // LANES):
        xc = x[:, c * LANES:(c + 1) * LANES]
        ms = _dot_split_lhs(xc * xc, seg) * (1.0 / HEAD_DIM)
        cols.append(xc * lax.rsqrt(ms + EPS))
    y = cols[0] if len(cols) == 1 else jnp.concatenate(cols, axis=1)
    return y * gain


def _rope(x, cos, sin_s):
    first = (_lane((x.shape[0], LANES)) & 63) < 32
    cols = []
    for c in range(x.shape[1] // LANES):
        xc = x[:, c * LANES:(c + 1) * LANES]
        sw = jnp.where(first, pltpu.roll(xc, 96, 1), pltpu.roll(xc, 32, 1))
        cols.append(xc * cos + sw * sin_s)
    return cols[0] if len(cols) == 1 else jnp.concatenate(cols, axis=1)


def _split_kv(k, fill=0.0):
    lo = _lane(k.shape) < 64
    return jnp.where(lo, k, fill), jnp.where(lo, pltpu.roll(k, 64, 1), fill)


def _q_heads(q, fills=None):
    lo = _lane((q.shape[0], LANES)) < 64
    out = []
    for c in range(q.shape[1] // LANES):
        qc = q[:, c * LANES:(c + 1) * LANES]
        fill = 0.0 if fills is None else fills[c]
        out.append(jnp.where(lo, qc, fill).astype(BF16))
        out.append(jnp.where(lo, pltpu.roll(qc, 64, 1), fill).astype(BF16))
    return out


VT_ROWS = HEAD_DIM + 16


def _with_ones_row(vt):
    n = vt.shape[1]
    pad = jnp.where(_row((VT_ROWS - HEAD_DIM, n)) == 0, 1.0, 0.0)
    return jnp.concatenate([vt, pad], axis=0)


def _exp_cols(s, extra=None):
    m = jnp.max(s, axis=0, keepdims=True)
    if extra is not None:
        m = jnp.maximum(m, extra)
    return jnp.exp2(s - m).astype(BF16), m


def _finish(acc, extra_den=None):
    den = acc[HEAD_DIM:HEAD_DIM + 1, :]
    if extra_den is not None:
        den = den + extra_den
    return acc[0:HEAD_DIM, :] * (1.0 / den)


def _online_update(state, s, vt_tile):
    m, acc = state
    p, m_new = _exp_cols(s, m)
    acc = jnp.exp2(m - m_new) * acc + _dot(vt_tile, p)
    return m_new, acc


def _tile4(x):
    return jnp.concatenate([x, x, x, x], axis=1)


def _inproj_kernel(x_ref, g_ref, w_ref, o_ref):
    x = x_ref[...]
    ms = jnp.mean(x * x, axis=-1, keepdims=True)
    xn = (x * lax.rsqrt(ms + EPS) * g_ref[...]).astype(BF16)
    n = o_ref.shape[-1]
    for c in range(0, n, 512):
        w = min(512, n - c)
        o_ref[:, c:c + w] = _dot(xn, w_ref[:, c:c + w]).astype(BF16)


def _inproj(x2d, gain, w_bf16, tm=512):
    n, d = x2d.shape
    nc = w_bf16.shape[1]
    return pl.pallas_call(
        _inproj_kernel,
        grid=(n // tm,),
        in_specs=[pl.BlockSpec((tm, d), lambda i: (i, 0)),
                  pl.BlockSpec((1, d), lambda i: (0, 0)),
                  pl.BlockSpec((d, nc), lambda i: (0, 0))],
        out_specs=pl.BlockSpec((tm, nc), lambda i: (i, 0)),
        out_shape=jax.ShapeDtypeStruct((n, nc), BF16),
        compiler_params=_cparams(("arbitrary",)),
        name="inproj",
    )(x2d, gain.reshape(1, d), w_bf16)


def _outproj_kernel(h_ref, a_ref, b_ref, m_ref, w_ref, o_ref):
    a, b, m = a_ref[...], b_ref[...], m_ref[...]
    for c in range(0, D_MODEL, 256):
        acc = h_ref[:, c:c + 256]
        acc += _dot(a, w_ref[0:512, c:c + 256])
        acc += _dot(b, w_ref[512:1024, c:c + 256])
        acc += _dot(m, w_ref[1024:1280, c:c + 256])
        o_ref[:, c:c + 256] = acc


def _outproj(h2d, oa, ob, om, w_bf16, tm=1024):
    n, d = h2d.shape
    return pl.pallas_call(
        _outproj_kernel,
        grid=(n // tm,),
        in_specs=[pl.BlockSpec((tm, d), lambda i: (i, 0)),
                  pl.BlockSpec((tm, 512), lambda i: (i, 0)),
                  pl.BlockSpec((tm, 512), lambda i: (i, 0)),
                  pl.BlockSpec((tm, 256), lambda i: (i, 0)),
                  pl.BlockSpec((1280, d), lambda i: (0, 0))],
        out_specs=pl.BlockSpec((tm, d), lambda i: (i, 0)),
        out_shape=jax.ShapeDtypeStruct((n, d), F32),
        compiler_params=_cparams(("arbitrary",)),
        name="outproj",
    )(h2d, oa, ob, om, w_bf16)


def _mem_kernel(q_ref, gm_ref, mem_ref, mg_ref, wkv_ref, qn_ref, kn_ref, o_ref, k_scr, vt_scr):
    @pl.when(pl.program_id(1) == 0)
    def _():
        m = mem_ref[0]
        ms = jnp.mean(m * m, axis=-1, keepdims=True)
        mn = (m * lax.rsqrt(ms + EPS) * mg_ref[...]).astype(BF16)
        kv = _dot(mn, wkv_ref[...])
        km = _head_rms(kv[:, 0:256], kn_ref[...])
        for c in range(2):
            k0, k1 = _split_kv(km[:, c * LANES:(c + 1) * LANES])
            k_scr[2 * c] = k0.astype(BF16)
            k_scr[2 * c + 1] = k1.astype(BF16)
        vt = kv[:, 256:512].T
        for h in range(M_HEADS):
            vt_scr[h] = _with_ones_row(vt[h * HEAD_DIM:(h + 1) * HEAD_DIM, :]).astype(BF16)

    q = _head_rms(q_ref[0].astype(F32), qn_ref[...]) * Q_SCALE
    scores = [_dot_nt(k_scr[h], qh) for h, qh in enumerate(_q_heads(q))]
    rows = []
    for h, sc in enumerate(scores):
        p, _ = _exp_cols(sc)
        rows.append(_finish(_dot(vt_scr[h], p)))
    o = jnp.concatenate(rows, axis=0).T
    o_ref[0] = (o * _silu(gm_ref[0].astype(F32))).astype(BF16)


def _mem_attention(z, q_off, g_off, mem, mem_g, wkv_bf16, qn, kn, tq=512):
    b, s, _ = z.shape
    return pl.pallas_call(
        _mem_kernel,
        grid=(b, s // tq),
        in_specs=[pl.BlockSpec((1, tq, 256), lambda bi, i: (bi, i, q_off // 256)),
                  pl.BlockSpec((1, tq, 256), lambda bi, i: (bi, i, g_off // 256)),
                  pl.BlockSpec((1, N_MEM, D_MODEL), lambda bi, i: (bi, 0, 0)),
                  pl.BlockSpec((1, D_MODEL), lambda bi, i: (0, 0)),
                  pl.BlockSpec((D_MODEL, 512), lambda bi, i: (0, 0)),
                  pl.BlockSpec((1, 256), lambda bi, i: (0, 0)),
                  pl.BlockSpec((1, 256), lambda bi, i: (0, 0))],
        out_specs=pl.BlockSpec((1, tq, 256), lambda bi, i: (bi, i, 0)),
        out_shape=jax.ShapeDtypeStruct((b, s, 256), BF16),
        scratch_shapes=[pltpu.VMEM((M_HEADS, N_MEM, LANES), BF16), pltpu.VMEM((M_HEADS, VT_ROWS, N_MEM), BF16)],
        compiler_params=_cparams(("arbitrary", "arbitrary")),
        name="mem_attention",
    )(z, z, mem, mem_g.reshape(1, D_MODEL), wkv_bf16, jnp.tile(qn, 4).reshape(1, 256), jnp.tile(kn, 4).reshape(1, 256))


def _swa_kernel(q_ref, g_ref, kc_ref, kp_ref, vc_ref, vp_ref, cosc_ref, sinc_ref, cosp_ref, sinp_ref,
                qn_ref, kn_ref, sink_ref, o_ref):
    i = pl.program_id(1)
    t = q_ref.shape[1]
    tp = kp_ref.shape[1]
    grp = A_HEADS // A_KV_HEADS
    q = _rope(_head_rms(q_ref[0].astype(F32), qn_ref[...]), cosc_ref[...], sinc_ref[...]) * Q_SCALE
    qh = _q_heads(q)
    kc = _rope(_head_rms(kc_ref[0].astype(F32), kn_ref[...]), cosc_ref[...], sinc_ref[...])
    kp = _rope(_head_rms(kp_ref[0].astype(F32), kn_ref[...]), cosp_ref[...], sinp_ref[...])
    ks = [x.astype(BF16) for x in _split_kv(jnp.concatenate([kp, kc], axis=0))]
    vt = jnp.concatenate([vp_ref[0].astype(F32).T, vc_ref[0].astype(F32).T], axis=1)
    ts = i * t - tp + _row((tp + t, t))
    rel = i * t + _lane((tp + t, t)) - ts
    bias = _tile4(jnp.where((rel >= 0) & (rel < A_WINDOW) & (ts >= 0), 0.0, NEG_INF))
    scores = [_dot_nt(ks[kv], jnp.concatenate(qh[kv * grp:(kv + 1) * grp], axis=0)) for kv in range(A_KV_HEADS)]
    rows = []
    for kv in range(A_KV_HEADS):
        sink = jnp.concatenate([jnp.broadcast_to(sink_ref[kv * grp + g:kv * grp + g + 1, 0:1], (1, t))
                                for g in range(grp)], axis=1) * LOG2E
        p, m = _exp_cols(scores[kv] + bias, sink)
        vt_kv = _with_ones_row(vt[kv * HEAD_DIM:(kv + 1) * HEAD_DIM, :]).astype(BF16)
        o_t = _finish(_dot(vt_kv, p), jnp.exp2(sink - m))
        rows.extend(o_t[:, g * t:(g + 1) * t] for g in range(grp))
    o = jnp.concatenate(rows, axis=0).T
    o_ref[0] = (o * _silu(g_ref[0].astype(F32))).astype(BF16)


def _swa(z, cos, sin_s, qn, kn, sinks, t=256):
    b, s, _ = z.shape
    tp = A_WINDOW
    qo, go = EVEN_OFF["qa"][0] // 512, EVEN_OFF["ga"][0] // 512
    ko, vo = EVEN_OFF["ka"][0] // 128, EVEN_OFF["va"][0] // 128
    prev = lambda i: jnp.maximum(i * (t // tp) - 1, 0)
    return pl.pallas_call(
        _swa_kernel,
        grid=(b, s // t),
        in_specs=[pl.BlockSpec((1, t, 512), lambda bi, i: (bi, i, qo)),
                  pl.BlockSpec((1, t, 512), lambda bi, i: (bi, i, go)),
                  pl.BlockSpec((1, t, 128), lambda bi, i: (bi, i, ko)),
                  pl.BlockSpec((1, tp, 128), lambda bi, i: (bi, prev(i), ko)),
                  pl.BlockSpec((1, t, 128), lambda bi, i: (bi, i, vo)),
                  pl.BlockSpec((1, tp, 128), lambda bi, i: (bi, prev(i), vo)),
                  pl.BlockSpec((t, 128), lambda bi, i: (i, 0)),
                  pl.BlockSpec((t, 128), lambda bi, i: (i, 0)),
                  pl.BlockSpec((tp, 128), lambda bi, i: (prev(i), 0)),
                  pl.BlockSpec((tp, 128), lambda bi, i: (prev(i), 0)),
                  pl.BlockSpec((1, 512), lambda bi, i: (0, 0)),
                  pl.BlockSpec((1, 128), lambda bi, i: (0, 0)),
                  pl.BlockSpec((A_HEADS, 128), lambda bi, i: (0, 0))],
        out_specs=pl.BlockSpec((1, t, 512), lambda bi, i: (bi, i, 0)),
        out_shape=jax.ShapeDtypeStruct((b, s, 512), BF16),
        compiler_params=_cparams(("arbitrary", "arbitrary")),
        name="swa",
    )(z, z, z, z, z, z, cos, sin_s, cos, sin_s, jnp.tile(qn, 8).reshape(1, 512), jnp.tile(kn, 2).reshape(1, 128),
      jnp.broadcast_to(sinks.reshape(A_HEADS, 1), (A_HEADS, 128)))


def _rglru_kernel(x_ref, g_ref, cw_ref, cb_ref, wr_ref, br_ref, wi_ref, bi_ref, lam_ref, o_ref,
                  xbuf, hcar, a_scr, u_scr, h_scr):
    t = x_ref.shape[1]

    @pl.when(pl.program_id(1) == 0)
    def _():
        xbuf[0:8, :] = jnp.zeros((8, B_WIDTH), F32)
        hcar[...] = jnp.zeros((8, B_WIDTH), F32)

    xbuf[8:t + 8, :] = x_ref[0].astype(F32)
    xc = cb_ref[...] + cw_ref[0:1, :] * xbuf[5:t + 5, :]
    for j in range(1, B_CONV):
        xc = xc + cw_ref[j:j + 1, :] * xbuf[5 + j:t + 5 + j, :]
    xbuf[0:8, :] = xbuf[t:t + 8, :]

    r_cols, i_cols = [], []
    for c in range(B_WIDTH // LANES):
        xcc = xc[:, c * LANES:(c + 1) * LANES].astype(BF16)
        r_cols.append(_dot(xcc, wr_ref[c]))
        i_cols.append(_dot(xcc, wi_ref[c]))
    r = jax.nn.sigmoid(jnp.concatenate(r_cols, axis=1) + br_ref[...])
    ig = jax.nn.sigmoid(jnp.concatenate(i_cols, axis=1) + bi_ref[...])
    nl = -lam_ref[...]
    softplus = jnp.maximum(nl, 0.0) + jnp.log(1.0 + jnp.exp(-jnp.abs(nl)))
    log_a = -B_C * r * softplus
    a = jnp.exp(log_a)
    u = jnp.sqrt(1.0 - a * a) * (ig * xc)

    r8 = _row((t, B_WIDTH)) & 7
    for d in (1, 2, 4):
        a_sh = pltpu.roll(a, d, 0)
        u_sh = pltpu.roll(u, d, 0)
        m = r8 >= d
        u = jnp.where(m, a * u_sh + u, u)
        a = jnp.where(m, a * a_sh, a)
    a_scr[...] = a
    u_scr[...] = u

    def body(j, h):
        off = pl.multiple_of(j * 8, 8)
        hh = a_scr[pl.ds(off, 8), :] * h + u_scr[pl.ds(off, 8), :]
        h_scr[pl.ds(off, 8), :] = hh
        return hh[7:8, :]

    h_last = lax.fori_loop(0, t // 8, body, hcar[0:1, :])
    hcar[0:1, :] = h_last
    o_ref[0] = (h_scr[...] * _silu(g_ref[0].astype(F32))).astype(BF16)


def _rglru(z, conv_w, conv_b, wr_bd, b_r, wi_bd, b_i, lam, t=512):
    b, s, _ = z.shape
    xo, go = EVEN_OFF["xb"][0] // 512, EVEN_OFF["gb"][0] // 512
    row = lambda v: v.reshape(1, B_WIDTH)
    full = lambda shp: pl.BlockSpec(shp, lambda bi, i: (0,) * len(shp))
    return pl.pallas_call(
        _rglru_kernel,
        grid=(b, s // t),
        in_specs=[pl.BlockSpec((1, t, 512), lambda bi, i: (bi, i, xo)),
                  pl.BlockSpec((1, t, 512), lambda bi, i: (bi, i, go)),
                  full((B_CONV, B_WIDTH)), full((1, B_WIDTH)),
                  full((4, LANES, LANES)), full((1, B_WIDTH)),
                  full((4, LANES, LANES)), full((1, B_WIDTH)), full((1, B_WIDTH))],
        out_specs=pl.BlockSpec((1, t, 512), lambda bi, i: (bi, i, 0)),
        out_shape=jax.ShapeDtypeStruct((b, s, 512), BF16),
        scratch_shapes=[pltpu.VMEM((t + 8, B_WIDTH), F32), pltpu.VMEM((8, B_WIDTH), F32),
                        pltpu.VMEM((t, B_WIDTH), F32), pltpu.VMEM((t, B_WIDTH), F32), pltpu.VMEM((t, B_WIDTH), F32)],
        compiler_params=_cparams(("arbitrary", "arbitrary")),
        name="rglru",
    )(z, z, conv_w, row(conv_b), wr_bd, row(b_r), wi_bd, row(b_i), row(lam))


def _hgrn_consts():
    c = C_CHUNK
    t = np.arange(c)[:, None]
    s = np.arange(c)[None, :]
    mats = [(s <= t)]
    masks = []
    hs = c // 2
    while hs >= 1:
        mid = (t // (2 * hs)) * 2 * hs + hs - 1
        mats.append(s <= mid)
        same = (t // (2 * hs)) == (s // (2 * hs))
        masks.append(same & ((t // hs) % 2 == 1) & ((s // hs) % 2 == 0))
        hs //= 2
    pairs = [np.concatenate(masks[n:n + 2], axis=1) for n in range(0, len(masks), 2)]
    return np.concatenate(mats, axis=0).astype(np.float32), np.stack(pairs).astype(np.float32)


def _hgrn_kernel(q_ref, f_ref, i_ref, g_ref, lb_ref, og_ref, mst_ref, msk_ref, o_ref, st_scr):
    @pl.when(pl.program_id(1) == 0)
    def _():
        st_scr[...] = jnp.zeros(st_scr.shape, F32)

    c = C_CHUNK
    npair = msk_ref.shape[0]
    p = lb_ref[...]
    pm = jnp.maximum(p[0:1, :], p[1:2, :])
    e0, e1 = jnp.exp(p[0:1, :] - pm), jnp.exp(p[1:2, :] - pm)
    lb = e1 / (e0 + e1)
    zeros = jnp.zeros((c, LANES), BF16)
    for ch in range(q_ref.shape[1] // c):
        rows = slice(ch * c, (ch + 1) * c)
        f = lb + (1.0 - lb) * jax.nn.sigmoid(f_ref[0, rows, :].astype(F32))
        g = jnp.log2(f)
        g_hi, g_lo = _split(g)
        b_all = _dot(mst_ref[0:c, :], g_hi) + _dot(mst_ref[0:c, :], g_lo)
        bm_all = _dot(mst_ref[c:, :], g_hi)
        kk_all = 1.0 - f
        qf_all = _silu(q_ref[0, rows, :].astype(F32))
        v_all = i_ref[0, rows, :].astype(F32)
        kk_bf, qf_bf = kk_all.astype(BF16), qf_all.astype(BF16)
        heads = range(C_HEADS)
        cols = [slice(h * LANES, (h + 1) * LANES) for h in heads]
        sts = [st_scr[h] for h in heads]
        os = [_dot_nt((qf_all[:, cols[h]] * jnp.exp2(b_all[:, cols[h]])).astype(BF16), sts[h].astype(BF16)) for h in heads]
        att = [[] for _ in heads]
        for n in range(npair):
            lhs, rhs = [], []
            for h in heads:
                b = b_all[:, cols[h]]
                qt, kt = [], []
                for l in (2 * n, 2 * n + 1):
                    bm = bm_all[l * c:(l + 1) * c, cols[h]]
                    qt.append(qf_bf[:, cols[h]] * jnp.exp2(jnp.minimum(b - bm, 1.0)).astype(BF16))
                    kt.append(kk_bf[:, cols[h]] * jnp.exp2(jnp.minimum(bm - b, 1.0)).astype(BF16))
                lhs.append(jnp.concatenate(qt, axis=1))
                rhs.append(jnp.concatenate([jnp.concatenate([kt[0], zeros], axis=1),
                                            jnp.concatenate([zeros, kt[1]], axis=1)], axis=0))
            prods = [_dot_nt(lhs[h], rhs[h]) for h in heads]
            for h in heads:
                att[h].append(jnp.where(msk_ref[n] > 0.5, prods[h], 0.0).astype(BF16))
        vbs = [v_all[:, cols[h]].astype(BF16) for h in heads]
        intra = [_dot(jnp.concatenate(att[h], axis=1), jnp.concatenate([vbs[h]] * (2 * npair), axis=0)) for h in heads]
        b_last = b_all[c - 1:c, :]
        kd = (kk_all * jnp.exp2(b_last - b_all)).astype(BF16)
        upd = [_dot(v_all[:, cols[h]].T.astype(BF16), kd[:, cols[h]]) for h in heads]
        decay = jnp.exp2(b_last)
        diag = qf_all * kk_all
        outs = []
        for h in heads:
            st_scr[h] = sts[h] * decay[:, cols[h]] + upd[h]
            o = os[h] + intra[h] + jnp.sum(diag[:, cols[h]], axis=-1, keepdims=True) * v_all[:, cols[h]]
            ms = jnp.mean(o * o, axis=-1, keepdims=True)
            outs.append(o * lax.rsqrt(ms + EPS) * og_ref[...])
        o_ref[0, rows, :] = (jnp.concatenate(outs, axis=1) * _silu(g_ref[0, rows, :].astype(F32))).astype(BF16)


def _hgrn(z, c_lb, c_og, t=512):
    b, s, _ = z.shape
    mst, msk = _hgrn_consts()
    blk = lambda name: pl.BlockSpec((1, t, 512), lambda bi, i, o=ODD_OFF[name][0] // 512: (bi, i, o))
    full = lambda shp: pl.BlockSpec(shp, lambda bi, i: (0,) * len(shp))
    return pl.pallas_call(
        _hgrn_kernel,
        grid=(b, s // t),
        in_specs=[blk("qc"), blk("fc"), blk("ic"), blk("gc"), full(c_lb.shape), full((1, C_HEAD_DIM)),
                  full(mst.shape), full(msk.shape)],
        out_specs=pl.BlockSpec((1, t, 512), lambda bi, i: (bi, i, 0)),
        out_shape=jax.ShapeDtypeStruct((b, s, 512), BF16),
        scratch_shapes=[pltpu.VMEM((C_HEADS, C_HEAD_DIM, C_HEAD_DIM), F32)],
        compiler_params=_cparams(("arbitrary", "arbitrary")),
        name="hgrn2",
    )(z, z, z, z, c_lb, c_og.reshape(1, C_HEAD_DIM), jnp.asarray(mst, BF16), jnp.asarray(msk, F32))


def _nsa_prep_kernel(kcr_ref, vcr_ref, ks_ref, vs_ref, kw_ref, vw_ref, cos_ref, sin_ref, cosc_ref, sinc_ref,
                     kncmp_ref, knslc_ref, knwin_ref, pek_ref, pev_ref, w1k_ref, w2k_ref, w1v_ref, w2vt_ref,
                     kc_o, vct_o, ks_o, vst_o, kw_o, vwt_o):
    def hidden(xr_ref, pe_ref, w1_ref):
        xr = xr_ref[0].astype(F32)
        top = _dot((xr + pe_ref[0:1, :]).astype(BF16), w1_ref[0])
        bot = _dot((xr + pe_ref[1:2, :]).astype(BF16), w1_ref[1])
        pre = top + pltpu.roll(bot, bot.shape[0] - 1, 0)
        return _silu(pre).astype(BF16)

    kc = _dot(hidden(kcr_ref, pek_ref, w1k_ref), w2k_ref[...])
    kc = _rope(_head_rms(kc, kncmp_ref[...]), cosc_ref[...], sinc_ref[...])
    d0, d1 = _split_kv(kc)
    kc_o[0, 0] = d0.astype(BF16)
    kc_o[0, 1] = d1.astype(BF16)
    vct = _dot_nt(w2vt_ref[...], hidden(vcr_ref, pev_ref, w1v_ref))
    for kv in range(D_KV_HEADS):
        vct_o[0, kv] = _with_ones_row(vct[kv * HEAD_DIM:(kv + 1) * HEAD_DIM, :]).astype(BF16)

    s = ks_ref.shape[1]
    step = 256
    for r0 in range(0, s, step):
        rows = slice(r0, r0 + step)
        cos, sin = cos_ref[rows, :], sin_ref[rows, :]
        ks = _rope(_head_rms(ks_ref[0, rows, :].astype(F32), knslc_ref[...]), cos, sin)
        kw = _rope(_head_rms(kw_ref[0, rows, :].astype(F32), knwin_ref[...]), cos, sin)
        blk = (r0 + _row((step, LANES))) // SEL_LEN
        onehot = jnp.where(_lane((step, LANES)) - HEAD_DIM == blk, MASK_BIG, 0.0)
        for src, dst, fill in ((ks, ks_o, onehot), (kw, kw_o, 0.0)):
            d0, d1 = _split_kv(src, fill)
            dst[0, 0, rows, :] = d0.astype(BF16)
            dst[0, 1, rows, :] = d1.astype(BF16)
        for src, dst in ((vs_ref, vst_o), (vw_ref, vwt_o)):
            vt = src[0, rows, :].astype(F32).T
            for kv in range(D_KV_HEADS):
                dst[0, kv, :, rows] = _with_ones_row(vt[kv * HEAD_DIM:(kv + 1) * HEAD_DIM, :]).astype(BF16)


def _nsa_prep(z, kcr, vcr, cos, sin_s, cosc, sinc, kn_cmp, kn_slc, kn_win, pek, pev, w1k, w2k, w1v, w2v):
    b, s, _ = z.shape
    nseg = s // CMP_STRIDE
    zb = lambda name: pl.BlockSpec((1, s, 128), lambda bi, o=ODD_OFF[name][0] // 128: (bi, 0, o))
    full = lambda shp: pl.BlockSpec(shp, lambda bi: (0,) * len(shp))
    seg = pl.BlockSpec((1, nseg, 2048), lambda bi: (bi, 0, 0))
    g2 = lambda g: jnp.tile(g, 2).reshape(1, 128)
    dup = lambda n: (jax.ShapeDtypeStruct((b, 2, n, 128), BF16), pl.BlockSpec((1, 2, n, 128), lambda bi: (bi, 0, 0, 0)))
    tr = lambda n: (jax.ShapeDtypeStruct((b, 2, VT_ROWS, n), BF16),
                    pl.BlockSpec((1, 2, VT_ROWS, n), lambda bi: (bi, 0, 0, 0)))
    outs = [dup(nseg), tr(nseg), dup(s), tr(s), dup(s), tr(s)]
    return pl.pallas_call(
        _nsa_prep_kernel,
        grid=(b,),
        in_specs=[seg, seg, zb("ksd"), zb("vsd"), zb("kwd"), zb("vwd"),
                  full((s, 128)), full((s, 128)), full((nseg, 128)), full((nseg, 128)),
                  full((1, 128)), full((1, 128)), full((1, 128)), full((2, 2048)), full((2, 2048)),
                  full((2, 2048, 256)), full((256, 128)), full((2, 2048, 256)), full((128, 256))],
        out_specs=[o[1] for o in outs],
        out_shape=[o[0] for o in outs],
        compiler_params=_cparams(("arbitrary",)),
        name="nsa_prep",
    )(kcr, vcr, z, z, z, z, cos, sin_s, cosc, sinc, g2(kn_cmp), g2(kn_slc), g2(kn_win), pek, pev, w1k, w2k, w1v, w2v)


SEL_CHUNK = 512
SEL_PIECE = 128


def _nsa_kernel(q_ref, gd_ref, gate_ref, cos_ref, sin_ref, qn_ref, kc_ref, vct_ref, ks_ref, vst_ref, kw_ref, vwt_ref,
                ovlt_ref, o_ref):
    i = pl.program_id(1)
    t = q_ref.shape[1]
    grp = D_HEADS // D_KV_HEADS
    w = grp * t
    kvs = range(D_KV_HEADS)
    q = _rope(_head_rms(q_ref[0].astype(F32), qn_ref[...]), cos_ref[...], sin_ref[...]) * Q_SCALE
    stack = lambda heads: [jnp.concatenate(heads[kv * grp:(kv + 1) * grp], axis=0) for kv in kvs]
    qs = stack(_q_heads(q))
    tq1 = i * t + _lane((1, t))

    nwin = D_WINDOW + t
    win0 = pl.multiple_of(jnp.maximum(i * t + t - nwin, 0), t)
    s_cmp = [_dot_nt(kc_ref[0, kv], qs[kv]) for kv in kvs]
    s_win = [_dot_nt(kw_ref[0, kv, pl.ds(win0, nwin), :], qs[kv]) for kv in kvs]

    ncmp = kc_ref.shape[2] - 1
    nsel = ovlt_ref.shape[0]
    crow = _row((kc_ref.shape[2], t))
    bias_c = _tile4(jnp.where((crow * CMP_STRIDE + (CMP_LEN - 1) <= tq1) & (crow < ncmp), 0.0, NEG_INF))
    row_ok = _tile4(tq1 >= CMP_LEN - 1)
    es = [_exp_cols(s_cmp[kv] + bias_c)[0] for kv in kvs]
    accs = [_dot(vct_ref[0, kv], es[kv]) for kv in kvs]
    invs = [jnp.where(row_ok, 1.0 / accs[kv][HEAD_DIM:HEAD_DIM + 1, :], 0.0) for kv in kvs]
    oc_t = [accs[kv][0:HEAD_DIM, :] * invs[kv] for kv in kvs]
    psums = []
    for kv in kvs:
        p = es[kv].astype(F32) * invs[kv]
        psums.append(p[:, 0:t] + p[:, t:2 * t] + p[:, 2 * t:3 * t] + p[:, 3 * t:4 * t])
    imps = [_dot_split_rhs(ovlt_ref[...], psums[kv]) for kv in kvs]

    rel = tq1 - (win0 + _row((nwin, t)))
    bias_w = _tile4(jnp.where((rel >= 0) & (rel < D_WINDOW), 0.0, NEG_INF))
    ow_t = [_finish(_dot(vwt_ref[0, kv, :, pl.ds(win0, nwin)], _exp_cols(s_win[kv] + bias_w)[0])) for kv in kvs]

    jrow = _row((nsel, t))
    jrow_f = jrow.astype(F32)
    cur = (i * t + _lane((nsel, t))) // SEL_LEN
    forced = (jrow == 0) | (jrow == cur)
    fills = []
    for kv in kvs:
        score = jnp.where(forced, POS_INF, jnp.where(jrow <= cur, imps[kv], NEG_INF))
        chosen = jnp.zeros((nsel, t), F32)
        for _ in range(min(SEL_TOPK, nsel)):
            mx = jnp.max(score, axis=0, keepdims=True)
            first = jnp.min(jnp.where(score == mx, jrow_f, 1e9), axis=0, keepdims=True)
            hit = jrow_f == first
            chosen = jnp.where(hit, 1.0, chosen)
            score = jnp.where(hit, -3e38, score)
        frame = jnp.concatenate([jnp.zeros((HEAD_DIM, t), F32), chosen - 1.0,
                                 jnp.zeros((LANES - HEAD_DIM - nsel, t), F32)], axis=0)
        fills.extend([frame.T] * (grp // 2))

    init = (jnp.full((1, w), NEG_INF, F32), jnp.zeros((VT_ROWS, w), F32))
    pc = SEL_PIECE

    def sweep(states, k_ref, vt_ref, queries, offs, biases, n):
        scores = [[_dot_nt(k_ref[0, kv, pl.ds(off, n), :], queries[kv]) for kv in kvs] for off in offs]
        for off, sc, bias in zip(offs, scores, biases):
            states = tuple(_online_update(states[kv], sc[kv] if bias is None else sc[kv] + bias,
                                          vt_ref[0, kv, :, pl.ds(off, n)]) for kv in kvs)
        return states

    qsel = stack(_q_heads(q, fills))
    ch = SEL_CHUNK
    own = pl.multiple_of(((i * t) // ch) * ch, ch)
    st = sweep((init, init), ks_ref, vst_ref, qsel, [own],
               [_tile4(jnp.where(own + _row((ch, t)) <= tq1, 0.0, NEG_INF))], ch)

    def sel_chunk(c, states):
        base = pl.multiple_of(c * ch, ch)
        return sweep(states, ks_ref, vst_ref, qsel, [base + n * pc for n in range(ch // pc)], [None] * (ch // pc), pc)

    st = lax.fori_loop(0, (i * t) // ch, sel_chunk, st)
    os_t = [_finish(acc) for (_, acc) in st]

    g_t = jax.nn.sigmoid(gate_ref[0].astype(F32)).T
    rows = []
    for h in range(D_HEADS):
        kv, g = divmod(h, grp)
        cols = slice(g * t, (g + 1) * t)
        r = D_BRANCHES * h
        rows.append(g_t[r:r + 1, :] * oc_t[kv][:, cols] + g_t[r + 1:r + 2, :] * os_t[kv][:, cols]
                    + g_t[r + 2:r + 3, :] * ow_t[kv][:, cols])
    o = jnp.concatenate(rows, axis=0).T
    o_ref[0] = (o * _silu(gd_ref[0].astype(F32))).astype(BF16)


def _overlap_t(s):
    ncmp = (s - CMP_LEN) // CMP_STRIDE + 1
    nsel = s // SEL_LEN
    cs = np.arange(ncmp)[None, :] * CMP_STRIDE
    ss = np.arange(nsel)[:, None] * SEL_LEN
    ovl = np.zeros((nsel, s // CMP_STRIDE), np.float32)
    ovl[:, :ncmp] = (cs < ss + SEL_LEN) & (cs + CMP_LEN > ss)
    return ovl


def _nsa(z, cos, sin_s, qn, kc, vct, ks, vst, kw, vwt, t=256):
    b, s, _ = z.shape
    nseg = s // CMP_STRIDE
    assert nseg == LANES, "compressed-block scores are laid out on one 128-row tile"
    ovlt = _overlap_t(s)
    full = lambda shp: pl.BlockSpec(shp, lambda bi, i: (0,) * len(shp))
    dup = lambda n: pl.BlockSpec((1, 2, n, 128), lambda bi, i: (bi, 0, 0, 0))
    tr = lambda n: pl.BlockSpec((1, 2, VT_ROWS, n), lambda bi, i: (bi, 0, 0, 0))
    return pl.pallas_call(
        _nsa_kernel,
        grid=(b, s // t),
        in_specs=[pl.BlockSpec((1, t, 512), lambda bi, i: (bi, i, ODD_OFF["qd"][0] // 512)),
                  pl.BlockSpec((1, t, 512), lambda bi, i: (bi, i, ODD_OFF["gd"][0] // 512)),
                  pl.BlockSpec((1, t, 128), lambda bi, i: (bi, i, ODD_OFF["gate"][0] // 128)),
                  pl.BlockSpec((t, 128), lambda bi, i: (i, 0)),
                  pl.BlockSpec((t, 128), lambda bi, i: (i, 0)),
                  full((1, 512)), dup(nseg), tr(nseg), dup(s), tr(s), dup(s), tr(s), full(ovlt.shape)],
        out_specs=pl.BlockSpec((1, t, 512), lambda bi, i: (bi, i, 0)),
        out_shape=jax.ShapeDtypeStruct((b, s, 512), BF16),
        compiler_params=_cparams(("arbitrary", "arbitrary")),
        name="nsa",
    )(z, z, z, cos, sin_s, jnp.tile(qn, 8).reshape(1, 512), kc, vct, ks, vst, kw, vwt, jnp.asarray(ovlt, BF16))


def _permute_cols(w, order, src):
    cols = []
    for name, width in order:
        o, sw = src[name]
        blk = w[:, o:o + sw]
        if sw < width:
            blk = jnp.pad(blk, ((0, 0), (0, width - sw)))
        cols.append(blk)
    return jnp.concatenate(cols, axis=1).astype(BF16)


def _block_diag_pairs(w):
    z = jnp.zeros((4, LANES, LANES), w.dtype)
    z = z.at[:, 0:64, 0:64].set(w[0::2])
    z = z.at[:, 64:128, 64:128].set(w[1::2])
    return z.astype(BF16)


def _rope_tables(pos):
    half = HEAD_DIM // 2
    inv = ROPE_THETA ** (-jnp.arange(half, dtype=F32) / half)
    ang = pos.astype(F32)[:, None] * inv[None, :]
    cos, sin = jnp.cos(ang), jnp.sin(ang)
    cos_t = jnp.tile(cos, (1, 4))
    sin_t = jnp.tile(jnp.concatenate([-sin, sin], axis=1), (1, 2))
    return cos_t, sin_t


def _expand_compress_w1(w1):
    hdim = w1.shape[1]
    w = w1.reshape(2, CMP_STRIDE, HEAD_DIM, hdim)
    z = jnp.zeros((2, CMP_STRIDE, 2, HEAD_DIM, 2, hdim), w1.dtype)
    z = z.at[:, :, 0, :, 0, :].set(w)
    z = z.at[:, :, 1, :, 1, :].set(w)
    return z.reshape(2, CMP_STRIDE * 2 * HEAD_DIM, 2 * hdim).astype(BF16)


def _expand_compress_w2(w2):
    hdim, hd = w2.shape
    z = jnp.zeros((2 * hdim, 2 * hd), w2.dtype)
    z = z.at[0:hdim, 0:hd].set(w2)
    z = z.at[hdim:, hd:].set(w2)
    return z.astype(BF16)


def _expand_pe(pe):
    p = pe.reshape(2, CMP_STRIDE, 1, HEAD_DIM)
    return jnp.broadcast_to(p, (2, CMP_STRIDE, 2, HEAD_DIM)).reshape(2, CMP_STRIDE * 2 * HEAD_DIM)


def _even_layer(h, mem, g, mem_g, w_mem_kv, m_qn, m_kn, w_in, w_out, a_qn, a_kn, a_sinks,
                conv_w, conv_b, w_r, b_r, w_i, b_i, lam, cos, sin_s):
    b, s, d = h.shape
    h2 = h.reshape(b * s, d)
    z = _inproj(h2, g, _permute_cols(w_in, EVEN_ORDER, EVEN_SRC)).reshape(b, s, EVEN_COLS)
    oa = _swa(z, cos, sin_s, a_qn, a_kn, a_sinks)
    ob = _rglru(z, conv_w, conv_b, _block_diag_pairs(w_r), b_r, _block_diag_pairs(w_i), b_i, lam)
    om = _mem_attention(z, EVEN_OFF["qm"][0], EVEN_OFF["gm"][0], mem, mem_g, w_mem_kv.astype(BF16), m_qn, m_kn)
    out = _outproj(h2, oa.reshape(b * s, 512), ob.reshape(b * s, 512), om.reshape(b * s, 256), w_out.astype(BF16))
    return out.reshape(b, s, d)


def _odd_layer(h, mem, g, mem_g, w_mem_kv, m_qn, m_kn, w_in, w_out, c_lb, c_og,
               d_qn, d_kn_cmp, d_kn_slc, d_kn_win, pe_k, pe_v, w1k, w2k, w1v, w2v, cos, sin_s, cosc, sinc):
    b, s, d = h.shape
    h2 = h.reshape(b * s, d)
    z = _inproj(h2, g, _permute_cols(w_in, ODD_ORDER, ODD_SRC)).reshape(b, s, ODD_COLS)
    oc = _hgrn(z, c_lb, c_og)
    nseg = s // CMP_STRIDE
    seg = lambda name: z[:, :, ODD_OFF[name][0]:ODD_OFF[name][0] + 128].reshape(b, nseg, CMP_STRIDE * 128)
    kc, vc, ks, vs, kw, vw = _nsa_prep(
        z, seg("kcd"), seg("vcd"), cos, sin_s, cosc, sinc, d_kn_cmp, d_kn_slc, d_kn_win,
        _expand_pe(pe_k), _expand_pe(pe_v), _expand_compress_w1(w1k), _expand_compress_w2(w2k),
        _expand_compress_w1(w1v), _expand_compress_w2(w2v).T)
    od = _nsa(z, cos, sin_s, d_qn, kc, vc, ks, vs, kw, vw)
    om = _mem_attention(z, ODD_OFF["qm"][0], ODD_OFF["gm"][0], mem, mem_g, w_mem_kv.astype(BF16), m_qn, m_kn)
    out = _outproj(h2, oc.reshape(b * s, 512), od.reshape(b * s, 512), om.reshape(b * s, 256), w_out.astype(BF16))
    return out.reshape(b, s, d)


def kernel(x, mem, norm_g, mem_norm_g, mem_w_kv, mem_qn, mem_kn, ev_w_in, ev_w_out, a_qn, a_kn, a_sinks,
           b_conv_w, b_conv_b, b_w_r, b_b_r, b_w_i, b_b_i, b_lambda, od_w_in, od_w_out, c_lb, c_onorm,
           d_qn, d_kn_cmp, d_kn_slc, d_kn_win, d_pe_k, d_pe_v, d_w1k, d_w2k, d_w1v, d_w2v):
    depth = norm_g.shape[0]
    assert depth == 2 and c_lb.shape[0] == 2, "the HGRN2 lower-bound formula in the kernel is written for depth 2"
    s = x.shape[1]
    assert s % 256 == 0 and s >= D_WINDOW
    pos = jnp.arange(s)
    cos, sin_s = _rope_tables(pos)
    nseg = s // CMP_STRIDE
    cmp_end = jnp.minimum(jnp.arange(nseg) * CMP_STRIDE + CMP_LEN - 1, s - 1)
    cosc, sinc = _rope_tables(cmp_end)
    h = _even_layer(x, mem, norm_g[0], mem_norm_g[0], mem_w_kv[0], mem_qn[0], mem_kn[0], ev_w_in[0], ev_w_out[0],
                    a_qn[0], a_kn[0], a_sinks[0], b_conv_w[0], b_conv_b[0], b_w_r[0], b_b_r[0], b_w_i[0], b_b_i[0],
                    b_lambda[0], cos, sin_s)
    h = _odd_layer(h, mem, norm_g[1], mem_norm_g[1], mem_w_kv[1], mem_qn[1], mem_kn[1], od_w_in[0], od_w_out[0],
                   c_lb, c_onorm[0], d_qn[0], d_kn_cmp[0], d_kn_slc[0], d_kn_win[0], d_pe_k[0], d_pe_v[0],
                   d_w1k[0], d_w2k[0], d_w1v[0], d_w2v[0], cos, sin_s, cosc, sinc)
    return h
```

```python
import functools

import numpy as np
import jax
import jax.numpy as jnp
from jax import lax
from jax.experimental import pallas as pl
from jax.experimental.pallas import tpu as pltpu

F32 = jnp.float32
BF16 = jnp.bfloat16

D_MODEL = 1024
N_MEM = 256
HEAD_DIM = 64
ROPE_THETA = 10000.0
EPS = 1e-6
NEG_INF = -1e30
POS_INF = 1e30
MASK_BIG = 1e30
LANES = 128

A_HEADS, A_KV_HEADS, A_WINDOW = 8, 2, 128
B_WIDTH, B_BLOCKS, B_CONV, B_C = 512, 8, 4, 8.0
M_HEADS = 4
C_HEADS, C_HEAD_DIM, C_CHUNK = 4, 128, 64
D_HEADS, D_KV_HEADS = 8, 2
CMP_LEN, CMP_STRIDE, CMP_HIDDEN = 32, 16, 128
SEL_LEN, SEL_TOPK = 64, 4
D_WINDOW = 512
D_BRANCHES = 3
SCALE = HEAD_DIM ** -0.5
LOG2E = 1.4426950408889634
Q_SCALE = SCALE * LOG2E

EVEN_ORDER = [("xb", 512), ("gb", 512), ("qa", 512), ("ga", 512), ("qm", 256), ("gm", 256), ("ka", 128), ("va", 128)]
EVEN_FUSED = 1024
EVEN_SRC = {"qa": (0, 512), "ka": (512, 128), "va": (640, 128), "ga": (768, 512), "xb": (1280, 512),
            "gb": (1792, 512), "qm": (2304, 256), "gm": (2560, 256)}
ODD_ORDER = [("qc", 512), ("fc", 512), ("ic", 512), ("gc", 512), ("qd", 512), ("gd", 512), ("qm", 256), ("gm", 256),
             ("kcd", 128), ("vcd", 128), ("ksd", 128), ("vsd", 128), ("kwd", 128), ("vwd", 128), ("gate", 128)]
ODD_SRC = {"qc": (0, 512), "fc": (512, 512), "ic": (1024, 512), "gc": (1536, 512), "qd": (2048, 512),
           "kcd": (2560, 128), "vcd": (2688, 128), "ksd": (2816, 128), "vsd": (2944, 128), "kwd": (3072, 128),
           "vwd": (3200, 128), "gate": (3328, 24), "gd": (3352, 512), "qm": (3864, 256), "gm": (4120, 256)}

VMEM_LIMIT = 48 * 1024 * 1024


def _offsets(order):
    off, out = 0, {}
    for name, w in order:
        out[name] = (off, w)
        off += w
    return out, off


EVEN_OFF, EVEN_COLS = _offsets(EVEN_ORDER[2:])
ODD_OFF, ODD_COLS = _offsets(ODD_ORDER)


def _cparams(sem):
    return pltpu.CompilerParams(dimension_semantics=sem, vmem_limit_bytes=VMEM_LIMIT)


def _dot(a, b):
    return jnp.dot(a, b, preferred_element_type=F32)


def _dot_nt(a, b):
    return lax.dot_general(a, b, (((1,), (1,)), ((), ())), preferred_element_type=F32)


def _split(x):
    hi = x.astype(BF16)
    lo = (x - hi.astype(F32)).astype(BF16)
    return hi, lo


def _dot_split_lhs(x, m):
    hi, lo = _split(x)
    return _dot(hi, m) + _dot(lo, m)


def _dot_split_rhs(m, x):
    hi, lo = _split(x)
    return _dot(m, hi) + _dot(m, lo)


def _lane(shape):
    return lax.broadcasted_iota(jnp.int32, shape, len(shape) - 1)


def _row(shape):
    return lax.broadcasted_iota(jnp.int32, shape, len(shape) - 2)


def _silu(x):
    return x * jax.nn.sigmoid(x)


def _seg_ones():
    r = lax.broadcasted_iota(jnp.int32, (LANES, LANES), 0) >> 6
    c = lax.broadcasted_iota(jnp.int32, (LANES, LANES), 1) >> 6
    return jnp.where(r == c, 1.0, 0.0).astype(BF16)


def _head_rms(x, gain):
    seg = _seg_ones()
    cols = []
    for c in range(x.shape[1] // LANES):
        xc = x[:, c * LANES:(c + 1) * LANES]
        ms = _dot_split_lhs(xc * xc, seg) * (1.0 / HEAD_DIM)
        cols.append(xc * lax.rsqrt(ms + EPS))
    y = cols[0] if len(cols) == 1 else jnp.concatenate(cols, axis=1)
    return y * gain


def _rope(x, cos, sin_s):
    first = (_lane((x.shape[0], LANES)) & 63) < 32
    cols = []
    for c in range(x.shape[1] // LANES):
        xc = x[:, c * LANES:(c + 1) * LANES]
        sw = jnp.where(first, pltpu.roll(xc, 96, 1), pltpu.roll(xc, 32, 1))
        cols.append(xc * cos + sw * sin_s)
    return cols[0] if len(cols) == 1 else jnp.concatenate(cols, axis=1)


def _split_kv(k, fill=0.0):
    lo = _lane(k.shape) < 64
    return jnp.where(lo, k, fill), jnp.where(lo, pltpu.roll(k, 64, 1), fill)


def _q_heads(q, fills=None):
    lo = _lane((q.shape[0], LANES)) < 64
    out = []
    for c in range(q.shape[1] // LANES):
        qc = q[:, c * LANES:(c + 1) * LANES]
        fill = 0.0 if fills is None else fills[c]
        out.append(jnp.where(lo, qc, fill).astype(BF16))
        out.append(jnp.where(lo, pltpu.roll(qc, 64, 1), fill).astype(BF16))
    return out


VT_ROWS = HEAD_DIM + 16


def _with_ones_row(vt):
    n = vt.shape[1]
    pad = jnp.where(_row((VT_ROWS - HEAD_DIM, n)) == 0, 1.0, 0.0)
    return jnp.concatenate([vt, pad], axis=0)


def _exp_cols(s, extra=None):
    m = jnp.max(s, axis=0, keepdims=True)
    if extra is not None:
        m = jnp.maximum(m, extra)
    return jnp.exp2(s - m).astype(BF16), m


def _finish(acc, extra_den=None):
    den = acc[HEAD_DIM:HEAD_DIM + 1, :]
    if extra_den is not None:
        den = den + extra_den
    return acc[0:HEAD_DIM, :] * (1.0 / den)


def _online_update(state, s, vt_tile):
    m, acc = state
    p, m_new = _exp_cols(s, m)
    acc = jnp.exp2(m - m_new) * acc + _dot(vt_tile, p)
    return m_new, acc


def _tile4(x):
    return jnp.concatenate([x, x, x, x], axis=1)


def _inproj_kernel(x_ref, g_ref, w_ref, o_ref):
    x = x_ref[...]
    ms = jnp.mean(x * x, axis=-1, keepdims=True)
    xn = (x * lax.rsqrt(ms + EPS) * g_ref[...]).astype(BF16)
    n = o_ref.shape[-1]
    for c in range(0, n, 512):
        w = min(512, n - c)
        o_ref[:, c:c + w] = _dot(xn, w_ref[:, c:c + w]).astype(BF16)


def _inproj(x2d, gain, w_bf16, tm=512):
    n, d = x2d.shape
    nc = w_bf16.shape[1]
    return pl.pallas_call(
        _inproj_kernel,
        grid=(n // tm,),
        in_specs=[pl.BlockSpec((tm, d), lambda i: (i, 0)),
                  pl.BlockSpec((1, d), lambda i: (0, 0)),
                  pl.BlockSpec((d, nc), lambda i: (0, 0))],
        out_specs=pl.BlockSpec((tm, nc), lambda i: (i, 0)),
        out_shape=jax.ShapeDtypeStruct((n, nc), BF16),
        compiler_params=_cparams(("arbitrary",)),
        name="inproj",
    )(x2d, gain.reshape(1, d), w_bf16)


def _outproj_kernel(h_ref, a_ref, b_ref, m_ref, w_ref, o_ref):
    a, b, m = a_ref[...], b_ref[...], m_ref[...]
    for c in range(0, D_MODEL, 256):
        acc = h_ref[:, c:c + 256]
        acc += _dot(a, w_ref[0:512, c:c + 256])
        acc += _dot(b, w_ref[512:1024, c:c + 256])
        acc += _dot(m, w_ref[1024:1280, c:c + 256])
        o_ref[:, c:c + 256] = acc


def _outproj(h2d, oa, ob, om, w_bf16, tm=1024):
    n, d = h2d.shape
    return pl.pallas_call(
        _outproj_kernel,
        grid=(n // tm,),
        in_specs=[pl.BlockSpec((tm, d), lambda i: (i, 0)),
                  pl.BlockSpec((tm, 512), lambda i: (i, 0)),
                  pl.BlockSpec((tm, 512), lambda i: (i, 0)),
                  pl.BlockSpec((tm, 256), lambda i: (i, 0)),
                  pl.BlockSpec((1280, d), lambda i: (0, 0))],
        out_specs=pl.BlockSpec((tm, d), lambda i: (i, 0)),
        out_shape=jax.ShapeDtypeStruct((n, d), F32),
        compiler_params=_cparams(("arbitrary",)),
        name="outproj",
    )(h2d, oa, ob, om, w_bf16)


def _mem_kernel(q_ref, gm_ref, mem_ref, mg_ref, wkv_ref, qn_ref, kn_ref, o_ref, k_scr, vt_scr):
    @pl.when(pl.program_id(1) == 0)
    def _():
        m = mem_ref[0]
        ms = jnp.mean(m * m, axis=-1, keepdims=True)
        mn = (m * lax.rsqrt(ms + EPS) * mg_ref[...]).astype(BF16)
        kv = _dot(mn, wkv_ref[...])
        km = _head_rms(kv[:, 0:256], kn_ref[...])
        for c in range(2):
            k0, k1 = _split_kv(km[:, c * LANES:(c + 1) * LANES])
            k_scr[2 * c] = k0.astype(BF16)
            k_scr[2 * c + 1] = k1.astype(BF16)
        vt = kv[:, 256:512].T
        for h in range(M_HEADS):
            vt_scr[h] = _with_ones_row(vt[h * HEAD_DIM:(h + 1) * HEAD_DIM, :]).astype(BF16)

    q = _head_rms(q_ref[0].astype(F32), qn_ref[...]) * Q_SCALE
    scores = [_dot_nt(k_scr[h], qh) for h, qh in enumerate(_q_heads(q))]
    rows = []
    for h, sc in enumerate(scores):
        p, _ = _exp_cols(sc)
        rows.append(_finish(_dot(vt_scr[h], p)))
    o = jnp.concatenate(rows, axis=0).T
    o_ref[0] = (o * _silu(gm_ref[0].astype(F32))).astype(BF16)


def _mem_attention(z, q_off, g_off, mem, mem_g, wkv_bf16, qn, kn, tq=512):
    b, s, _ = z.shape
    return pl.pallas_call(
        _mem_kernel,
        grid=(b, s // tq),
        in_specs=[pl.BlockSpec((1, tq, 256), lambda bi, i: (bi, i, q_off // 256)),
                  pl.BlockSpec((1, tq, 256), lambda bi, i: (bi, i, g_off // 256)),
                  pl.BlockSpec((1, N_MEM, D_MODEL), lambda bi, i: (bi, 0, 0)),
                  pl.BlockSpec((1, D_MODEL), lambda bi, i: (0, 0)),
                  pl.BlockSpec((D_MODEL, 512), lambda bi, i: (0, 0)),
                  pl.BlockSpec((1, 256), lambda bi, i: (0, 0)),
                  pl.BlockSpec((1, 256), lambda bi, i: (0, 0))],
        out_specs=pl.BlockSpec((1, tq, 256), lambda bi, i: (bi, i, 0)),
        out_shape=jax.ShapeDtypeStruct((b, s, 256), BF16),
        scratch_shapes=[pltpu.VMEM((M_HEADS, N_MEM, LANES), BF16), pltpu.VMEM((M_HEADS, VT_ROWS, N_MEM), BF16)],
        compiler_params=_cparams(("arbitrary", "arbitrary")),
        name="mem_attention",
    )(z, z, mem, mem_g.reshape(1, D_MODEL), wkv_bf16, jnp.tile(qn, 4).reshape(1, 256), jnp.tile(kn, 4).reshape(1, 256))


def _swa_kernel(q_ref, g_ref, kc_ref, kp_ref, vc_ref, vp_ref, cosc_ref, sinc_ref, cosp_ref, sinp_ref,
                qn_ref, kn_ref, sink_ref, o_ref):
    i = pl.program_id(1)
    t = q_ref.shape[1]
    tp = kp_ref.shape[1]
    grp = A_HEADS // A_KV_HEADS
    q = _rope(_head_rms(q_ref[0].astype(F32), qn_ref[...]), cosc_ref[...], sinc_ref[...]) * Q_SCALE
    qh = _q_heads(q)
    kc = _rope(_head_rms(kc_ref[0].astype(F32), kn_ref[...]), cosc_ref[...], sinc_ref[...])
    kp = _rope(_head_rms(kp_ref[0].astype(F32), kn_ref[...]), cosp_ref[...], sinp_ref[...])
    ks = [x.astype(BF16) for x in _split_kv(jnp.concatenate([kp, kc], axis=0))]
    vt = jnp.concatenate([vp_ref[0].astype(F32).T, vc_ref[0].astype(F32).T], axis=1)
    ts = i * t - tp + _row((tp + t, t))
    rel = i * t + _lane((tp + t, t)) - ts
    bias = _tile4(jnp.where((rel >= 0) & (rel < A_WINDOW) & (ts >= 0), 0.0, NEG_INF))
    scores = [_dot_nt(ks[kv], jnp.concatenate(qh[kv * grp:(kv + 1) * grp], axis=0)) for kv in range(A_KV_HEADS)]
    rows = []
    for kv in range(A_KV_HEADS):
        sink = jnp.concatenate([jnp.broadcast_to(sink_ref[kv * grp + g:kv * grp + g + 1, 0:1], (1, t))
                                for g in range(grp)], axis=1) * LOG2E
        p, m = _exp_cols(scores[kv] + bias, sink)
        vt_kv = _with_ones_row(vt[kv * HEAD_DIM:(kv + 1) * HEAD_DIM, :]).astype(BF16)
        o_t = _finish(_dot(vt_kv, p), jnp.exp2(sink - m))
        rows.extend(o_t[:, g * t:(g + 1) * t] for g in range(grp))
    o = jnp.concatenate(rows, axis=0).T
    o_ref[0] = (o * _silu(g_ref[0].astype(F32))).astype(BF16)


def _swa(z, cos, sin_s, qn, kn, sinks, t=256):
    b, s, _ = z.shape
    tp = A_WINDOW
    qo, go = EVEN_OFF["qa"][0] // 512, EVEN_OFF["ga"][0] // 512
    ko, vo = EVEN_OFF["ka"][0] // 128, EVEN_OFF["va"][0] // 128
    prev = lambda i: jnp.maximum(i * (t // tp) - 1, 0)
    return pl.pallas_call(
        _swa_kernel,
        grid=(b, s // t),
        in_specs=[pl.BlockSpec((1, t, 512), lambda bi, i: (bi, i, qo)),
                  pl.BlockSpec((1, t, 512), lambda bi, i: (bi, i, go)),
                  pl.BlockSpec((1, t, 128), lambda bi, i: (bi, i, ko)),
                  pl.BlockSpec((1, tp, 128), lambda bi, i: (bi, prev(i), ko)),
                  pl.BlockSpec((1, t, 128), lambda bi, i: (bi, i, vo)),
                  pl.BlockSpec((1, tp, 128), lambda bi, i: (bi, prev(i), vo)),
                  pl.BlockSpec((t, 128), lambda bi, i: (i, 0)),
                  pl.BlockSpec((t, 128), lambda bi, i: (i, 0)),
                  pl.BlockSpec((tp, 128), lambda bi, i: (prev(i), 0)),
                  pl.BlockSpec((tp, 128), lambda bi, i: (prev(i), 0)),
                  pl.BlockSpec((1, 512), lambda bi, i: (0, 0)),
                  pl.BlockSpec((1, 128), lambda bi, i: (0, 0)),
                  pl.BlockSpec((A_HEADS, 128), lambda bi, i: (0, 0))],
        out_specs=pl.BlockSpec((1, t, 512), lambda bi, i: (bi, i, 0)),
        out_shape=jax.ShapeDtypeStruct((b, s, 512), BF16),
        compiler_params=_cparams(("arbitrary", "arbitrary")),
        name="swa",
    )(z, z, z, z, z, z, cos, sin_s, cos, sin_s, jnp.tile(qn, 8).reshape(1, 512), jnp.tile(kn, 2).reshape(1, 128),
      jnp.broadcast_to(sinks.reshape(A_HEADS, 1), (A_HEADS, 128)))


def _rglru_tile(first, xb, gb, cw_ref, cb_ref, wr_ref, br_ref, wi_ref, bi_ref, lam_ref, xbuf, hcar, a_scr, u_scr, h_scr,
                between):
    t = xb.shape[0]

    @pl.when(first)
    def _():
        xbuf[0:8, :] = jnp.zeros((8, B_WIDTH), F32)
        hcar[...] = jnp.zeros((8, B_WIDTH), F32)

    xbuf[8:t + 8, :] = xb
    xc = cb_ref[...] + cw_ref[0:1, :] * xbuf[5:t + 5, :]
    for j in range(1, B_CONV):
        xc = xc + cw_ref[j:j + 1, :] * xbuf[5 + j:t + 5 + j, :]
    xbuf[0:8, :] = xbuf[t:t + 8, :]

    r_cols, i_cols = [], []
    for c in range(B_WIDTH // LANES):
        xcc = xc[:, c * LANES:(c + 1) * LANES].astype(BF16)
        r_cols.append(_dot(xcc, wr_ref[c]))
        i_cols.append(_dot(xcc, wi_ref[c]))
    between()
    r = jax.nn.sigmoid(jnp.concatenate(r_cols, axis=1) + br_ref[...])
    ig = jax.nn.sigmoid(jnp.concatenate(i_cols, axis=1) + bi_ref[...])
    nl = -lam_ref[...]
    softplus = jnp.maximum(nl, 0.0) + jnp.log(1.0 + jnp.exp(-jnp.abs(nl)))
    log_a = -B_C * r * softplus
    a = jnp.exp(log_a)
    om = 1.0 - a * a
    u = om * lax.rsqrt(jnp.maximum(om, 1e-30)) * (ig * xc)

    a = a.reshape(t // 8, 8, B_WIDTH)
    u = u.reshape(t // 8, 8, B_WIDTH)
    r8 = lax.broadcasted_iota(jnp.int32, a.shape, 1)
    for d in (1, 2, 4):
        a_sh = pltpu.roll(a, d, 1)
        u_sh = pltpu.roll(u, d, 1)
        m = r8 >= d
        u = jnp.where(m, a * u_sh + u, u)
        a = jnp.where(m, a * a_sh, a)
    a_scr[...] = a.reshape(t, B_WIDTH)
    u_scr[...] = u.reshape(t, B_WIDTH)

    def body(j, h):
        off = pl.multiple_of(j * 8, 8)
        hh = a_scr[pl.ds(off, 8), :] * h + u_scr[pl.ds(off, 8), :]
        h_scr[pl.ds(off, 8), :] = hh
        return hh[7:8, :]

    h_last = lax.fori_loop(0, t // 8, body, hcar[0:1, :], unroll=True)
    hcar[0:1, :] = h_last
    return h_scr[...] * _silu(gb)


def _inproj_rglru_kernel(tiles_per_seq, x_ref, g_ref, w_ref, cw_ref, cb_ref, wr_ref, br_ref, wi_ref, bi_ref, lam_ref,
                         z_ref, ob_ref, xbuf, hcar, a_scr, u_scr, h_scr):
    x = x_ref[...]
    ms = jnp.mean(x * x, axis=-1, keepdims=True)
    xn = (x * lax.rsqrt(ms + EPS) * g_ref[...]).astype(BF16)
    xb = _dot(xn, w_ref[:, 0:B_WIDTH])
    gb = _dot(xn, w_ref[:, B_WIDTH:2 * B_WIDTH])
    first = pl.program_id(0) % tiles_per_seq == 0

    def rest_of_projection():
        n = z_ref.shape[-1]
        for c in range(0, n, 512):
            w = min(512, n - c)
            z_ref[:, c:c + w] = _dot(xn, w_ref[:, EVEN_FUSED + c:EVEN_FUSED + c + w]).astype(BF16)

    ob_ref[...] = _rglru_tile(first, xb, gb, cw_ref, cb_ref, wr_ref, br_ref, wi_ref, bi_ref, lam_ref,
                              xbuf, hcar, a_scr, u_scr, h_scr, rest_of_projection).astype(BF16)


def _inproj_rglru(x2d, gain, w_bf16, seq, conv_w, conv_b, wr_bd, b_r, wi_bd, b_i, lam, tm=512):
    n, d = x2d.shape
    nc = w_bf16.shape[1]
    row = lambda v: v.reshape(1, B_WIDTH)
    full = lambda shp: pl.BlockSpec(shp, lambda i: (0,) * len(shp))
    return pl.pallas_call(
        functools.partial(_inproj_rglru_kernel, seq // tm),
        grid=(n // tm,),
        in_specs=[pl.BlockSpec((tm, d), lambda i: (i, 0)), full((1, d)), full((d, nc)),
                  full((B_CONV, B_WIDTH)), full((1, B_WIDTH)), full((4, LANES, LANES)), full((1, B_WIDTH)),
                  full((4, LANES, LANES)), full((1, B_WIDTH)), full((1, B_WIDTH))],
        out_specs=[pl.BlockSpec((tm, nc - EVEN_FUSED), lambda i: (i, 0)), pl.BlockSpec((tm, B_WIDTH), lambda i: (i, 0))],
        out_shape=[jax.ShapeDtypeStruct((n, nc - EVEN_FUSED), BF16), jax.ShapeDtypeStruct((n, B_WIDTH), BF16)],
        scratch_shapes=[pltpu.VMEM((tm + 8, B_WIDTH), F32), pltpu.VMEM((8, B_WIDTH), F32),
                        pltpu.VMEM((tm, B_WIDTH), F32), pltpu.VMEM((tm, B_WIDTH), F32), pltpu.VMEM((tm, B_WIDTH), F32)],
        compiler_params=_cparams(("arbitrary",)),
        name="inproj_rglru",
    )(x2d, gain.reshape(1, d), w_bf16, conv_w, row(conv_b), wr_bd, row(b_r), wi_bd, row(b_i), row(lam))


def _hgrn_consts():
    c = C_CHUNK
    t = np.arange(c)[:, None]
    s = np.arange(c)[None, :]
    mats = [(s <= t)]
    masks = []
    hs = c // 2
    while hs >= 1:
        mid = (t // (2 * hs)) * 2 * hs + hs - 1
        mats.append(s <= mid)
        same = (t // (2 * hs)) == (s // (2 * hs))
        masks.append(same & ((t // hs) % 2 == 1) & ((s // hs) % 2 == 0))
        hs //= 2
    pairs = [np.concatenate(masks[n:n + 2], axis=1) for n in range(0, len(masks), 2)]
    return np.concatenate(mats, axis=0).astype(np.float32), np.stack(pairs).astype(np.float32)


def _hgrn_kernel(q_ref, f_ref, i_ref, g_ref, lb_ref, og_ref, mst_ref, msk_ref, o_ref, st_scr):
    @pl.when(pl.program_id(1) == 0)
    def _():
        st_scr[...] = jnp.zeros(st_scr.shape, F32)

    c = C_CHUNK
    npair = msk_ref.shape[0]
    p = lb_ref[...]
    pm = jnp.maximum(p[0:1, :], p[1:2, :])
    e0, e1 = jnp.exp(p[0:1, :] - pm), jnp.exp(p[1:2, :] - pm)
    lb = e1 / (e0 + e1)
    zeros = jnp.zeros((c, LANES), BF16)
    for ch in range(q_ref.shape[1] // c):
        rows = slice(ch * c, (ch + 1) * c)
        f = lb + (1.0 - lb) * jax.nn.sigmoid(f_ref[0, rows, :].astype(F32))
        g = jnp.log2(f)
        g_hi, g_lo = _split(g)
        b_all = _dot(mst_ref[0:c, :], g_hi) + _dot(mst_ref[0:c, :], g_lo)
        bm_all = _dot(mst_ref[c:, :], g_hi)
        kk_all = 1.0 - f
        qf_all = _silu(q_ref[0, rows, :].astype(F32))
        v_all = i_ref[0, rows, :].astype(F32)
        kk_bf, qf_bf = kk_all.astype(BF16), qf_all.astype(BF16)
        heads = range(C_HEADS)
        cols = [slice(h * LANES, (h + 1) * LANES) for h in heads]
        sts = [st_scr[h] for h in heads]
        os = [_dot_nt((qf_all[:, cols[h]] * jnp.exp2(b_all[:, cols[h]])).astype(BF16), sts[h].astype(BF16)) for h in heads]
        att = [[] for _ in heads]
        for n in range(npair):
            lhs, rhs = [], []
            for h in heads:
                b = b_all[:, cols[h]]
                qt, kt = [], []
                for l in (2 * n, 2 * n + 1):
                    bm = bm_all[l * c:(l + 1) * c, cols[h]]
                    qt.append(qf_bf[:, cols[h]] * jnp.exp2(jnp.minimum(b - bm, 1.0)).astype(BF16))
                    kt.append(kk_bf[:, cols[h]] * jnp.exp2(jnp.minimum(bm - b, 1.0)).astype(BF16))
                lhs.append(jnp.concatenate(qt, axis=1))
                rhs.append(jnp.concatenate([jnp.concatenate([kt[0], zeros], axis=1),
                                            jnp.concatenate([zeros, kt[1]], axis=1)], axis=0))
            prods = [_dot_nt(lhs[h], rhs[h]) for h in heads]
            for h in heads:
                att[h].append(jnp.where(msk_ref[n] > 0.5, prods[h], 0.0).astype(BF16))
        vbs = [v_all[:, cols[h]].astype(BF16) for h in heads]
        intra = [_dot(jnp.concatenate(att[h], axis=1), jnp.concatenate([vbs[h]] * (2 * npair), axis=0)) for h in heads]
        b_last = b_all[c - 1:c, :]
        kd = (kk_all * jnp.exp2(b_last - b_all)).astype(BF16)
        upd = [_dot(v_all[:, cols[h]].T.astype(BF16), kd[:, cols[h]]) for h in heads]
        decay = jnp.exp2(b_last)
        diag = qf_all * kk_all
        outs = []
        for h in heads:
            st_scr[h] = sts[h] * decay[:, cols[h]] + upd[h]
            o = os[h] + intra[h] + jnp.sum(diag[:, cols[h]], axis=-1, keepdims=True) * v_all[:, cols[h]]
            ms = jnp.mean(o * o, axis=-1, keepdims=True)
            outs.append(o * lax.rsqrt(ms + EPS) * og_ref[...])
        o_ref[0, rows, :] = (jnp.concatenate(outs, axis=1) * _silu(g_ref[0, rows, :].astype(F32))).astype(BF16)


def _hgrn(z, c_lb, c_og, t=512):
    b, s, _ = z.shape
    mst, msk = _hgrn_consts()
    blk = lambda name: pl.BlockSpec((1, t, 512), lambda bi, i, o=ODD_OFF[name][0] // 512: (bi, i, o))
    full = lambda shp: pl.BlockSpec(shp, lambda bi, i: (0,) * len(shp))
    return pl.pallas_call(
        _hgrn_kernel,
        grid=(b, s // t),
        in_specs=[blk("qc"), blk("fc"), blk("ic"), blk("gc"), full(c_lb.shape), full((1, C_HEAD_DIM)),
                  full(mst.shape), full(msk.shape)],
        out_specs=pl.BlockSpec((1, t, 512), lambda bi, i: (bi, i, 0)),
        out_shape=jax.ShapeDtypeStruct((b, s, 512), BF16),
        scratch_shapes=[pltpu.VMEM((C_HEADS, C_HEAD_DIM, C_HEAD_DIM), F32)],
        compiler_params=_cparams(("arbitrary", "arbitrary")),
        name="hgrn2",
    )(z, z, z, z, c_lb, c_og.reshape(1, C_HEAD_DIM), jnp.asarray(mst, BF16), jnp.asarray(msk, F32))


def _nsa_prep_kernel(kcr_ref, vcr_ref, ks_ref, vs_ref, kw_ref, vw_ref, cos_ref, sin_ref, cosc_ref, sinc_ref,
                     kncmp_ref, knslc_ref, knwin_ref, pek_ref, pev_ref, w1k_ref, w2k_ref, w1v_ref, w2vt_ref,
                     kc_o, vct_o, ks_o, vst_o, kw_o, vwt_o):
    def hidden(xr_ref, pe_ref, w1_ref):
        xr = xr_ref[0].astype(F32)
        top = _dot((xr + pe_ref[0:1, :]).astype(BF16), w1_ref[0])
        bot = _dot((xr + pe_ref[1:2, :]).astype(BF16), w1_ref[1])
        pre = top + pltpu.roll(bot, bot.shape[0] - 1, 0)
        return _silu(pre).astype(BF16)

    kc = _dot(hidden(kcr_ref, pek_ref, w1k_ref), w2k_ref[...])
    kc = _rope(_head_rms(kc, kncmp_ref[...]), cosc_ref[...], sinc_ref[...])
    d0, d1 = _split_kv(kc)
    kc_o[0, 0] = d0.astype(BF16)
    kc_o[0, 1] = d1.astype(BF16)
    vct = _dot_nt(w2vt_ref[...], hidden(vcr_ref, pev_ref, w1v_ref))
    for kv in range(D_KV_HEADS):
        vct_o[0, kv] = _with_ones_row(vct[kv * HEAD_DIM:(kv + 1) * HEAD_DIM, :]).astype(BF16)

    s = ks_ref.shape[1]
    step = 256
    for r0 in range(0, s, step):
        rows = slice(r0, r0 + step)
        cos, sin = cos_ref[rows, :], sin_ref[rows, :]
        ks = _rope(_head_rms(ks_ref[0, rows, :].astype(F32), knslc_ref[...]), cos, sin)
        kw = _rope(_head_rms(kw_ref[0, rows, :].astype(F32), knwin_ref[...]), cos, sin)
        blk = (r0 + _row((step, LANES))) // SEL_LEN
        onehot = jnp.where(_lane((step, LANES)) - HEAD_DIM == blk, MASK_BIG, 0.0)
        for src, dst, fill in ((ks, ks_o, onehot), (kw, kw_o, 0.0)):
            d0, d1 = _split_kv(src, fill)
            dst[0, 0, rows, :] = d0.astype(BF16)
            dst[0, 1, rows, :] = d1.astype(BF16)
        for src, dst in ((vs_ref, vst_o), (vw_ref, vwt_o)):
            vt = src[0, rows, :].astype(F32).T
            for kv in range(D_KV_HEADS):
                dst[0, kv, :, rows] = _with_ones_row(vt[kv * HEAD_DIM:(kv + 1) * HEAD_DIM, :]).astype(BF16)


def _nsa_prep(z, kcr, vcr, cos, sin_s, cosc, sinc, kn_cmp, kn_slc, kn_win, pek, pev, w1k, w2k, w1v, w2v):
    b, s, _ = z.shape
    nseg = s // CMP_STRIDE
    zb = lambda name: pl.BlockSpec((1, s, 128), lambda bi, o=ODD_OFF[name][0] // 128: (bi, 0, o))
    full = lambda shp: pl.BlockSpec(shp, lambda bi: (0,) * len(shp))
    seg = pl.BlockSpec((1, nseg, 2048), lambda bi: (bi, 0, 0))
    g2 = lambda g: jnp.tile(g, 2).reshape(1, 128)
    dup = lambda n: (jax.ShapeDtypeStruct((b, 2, n, 128), BF16), pl.BlockSpec((1, 2, n, 128), lambda bi: (bi, 0, 0, 0)))
    tr = lambda n: (jax.ShapeDtypeStruct((b, 2, VT_ROWS, n), BF16),
                    pl.BlockSpec((1, 2, VT_ROWS, n), lambda bi: (bi, 0, 0, 0)))
    outs = [dup(nseg), tr(nseg), dup(s), tr(s), dup(s), tr(s)]
    return pl.pallas_call(
        _nsa_prep_kernel,
        grid=(b,),
        in_specs=[seg, seg, zb("ksd"), zb("vsd"), zb("kwd"), zb("vwd"),
                  full((s, 128)), full((s, 128)), full((nseg, 128)), full((nseg, 128)),
                  full((1, 128)), full((1, 128)), full((1, 128)), full((2, 2048)), full((2, 2048)),
                  full((2, 2048, 256)), full((256, 128)), full((2, 2048, 256)), full((128, 256))],
        out_specs=[o[1] for o in outs],
        out_shape=[o[0] for o in outs],
        compiler_params=_cparams(("arbitrary",)),
        name="nsa_prep",
    )(kcr, vcr, z, z, z, z, cos, sin_s, cosc, sinc, g2(kn_cmp), g2(kn_slc), g2(kn_win), pek, pev, w1k, w2k, w1v, w2v)


SEL_CHUNK = 512
SEL_PIECE = 128


def _nsa_kernel(q_ref, gd_ref, gate_ref, cos_ref, sin_ref, qn_ref, kc_ref, vct_ref, ks_ref, vst_ref, kw_ref, vwt_ref,
                ovlt_ref, o_ref):
    i = pl.program_id(1)
    t = q_ref.shape[1]
    grp = D_HEADS // D_KV_HEADS
    w = grp * t
    kvs = range(D_KV_HEADS)
    q = _rope(_head_rms(q_ref[0].astype(F32), qn_ref[...]), cos_ref[...], sin_ref[...]) * Q_SCALE
    stack = lambda heads: [jnp.concatenate(heads[kv * grp:(kv + 1) * grp], axis=0) for kv in kvs]
    qs = stack(_q_heads(q))
    tq1 = i * t + _lane((1, t))

    nwin = D_WINDOW + t
    win0 = pl.multiple_of(jnp.maximum(i * t + t - nwin, 0), t)
    s_cmp = [_dot_nt(kc_ref[0, kv], qs[kv]) for kv in kvs]
    s_win = [_dot_nt(kw_ref[0, kv, pl.ds(win0, nwin), :], qs[kv]) for kv in kvs]

    ncmp = kc_ref.shape[2] - 1
    nsel = ovlt_ref.shape[0]
    crow = _row((kc_ref.shape[2], t))
    bias_c = _tile4(jnp.where((crow * CMP_STRIDE + (CMP_LEN - 1) <= tq1) & (crow < ncmp), 0.0, NEG_INF))
    row_ok = _tile4(tq1 >= CMP_LEN - 1)
    es = [_exp_cols(s_cmp[kv] + bias_c)[0] for kv in kvs]
    accs = [_dot(vct_ref[0, kv], es[kv]) for kv in kvs]
    invs = [jnp.where(row_ok, 1.0 / accs[kv][HEAD_DIM:HEAD_DIM + 1, :], 0.0) for kv in kvs]
    oc_t = [accs[kv][0:HEAD_DIM, :] * invs[kv] for kv in kvs]
    psums = []
    for kv in kvs:
        p = es[kv].astype(F32) * invs[kv]
        psums.append(p[:, 0:t] + p[:, t:2 * t] + p[:, 2 * t:3 * t] + p[:, 3 * t:4 * t])
    imps = [_dot_split_rhs(ovlt_ref[...], psums[kv]) for kv in kvs]

    rel = tq1 - (win0 + _row((nwin, t)))
    bias_w = _tile4(jnp.where((rel >= 0) & (rel < D_WINDOW), 0.0, NEG_INF))
    ow_t = [_finish(_dot(vwt_ref[0, kv, :, pl.ds(win0, nwin)], _exp_cols(s_win[kv] + bias_w)[0])) for kv in kvs]

    jrow = _row((nsel, t))
    jrow_f = jrow.astype(F32)
    cur = (i * t + _lane((nsel, t))) // SEL_LEN
    forced = (jrow == 0) | (jrow == cur)
    fills = []
    for kv in kvs:
        score = jnp.where(forced, POS_INF, jnp.where(jrow <= cur, imps[kv], NEG_INF))
        chosen = jnp.zeros((nsel, t), F32)
        for _ in range(min(SEL_TOPK, nsel)):
            mx = jnp.max(score, axis=0, keepdims=True)
            first = jnp.min(jnp.where(score == mx, jrow_f, 1e9), axis=0, keepdims=True)
            hit = jrow_f == first
            chosen = jnp.where(hit, 1.0, chosen)
            score = jnp.where(hit, -3e38, score)
        frame = jnp.concatenate([jnp.zeros((HEAD_DIM, t), F32), chosen - 1.0,
                                 jnp.zeros((LANES - HEAD_DIM - nsel, t), F32)], axis=0)
        fills.extend([frame.T] * (grp // 2))

    init = (jnp.full((1, w), NEG_INF, F32), jnp.zeros((VT_ROWS, w), F32))
    pc = SEL_PIECE

    def sweep(states, k_ref, vt_ref, queries, offs, biases, n):
        scores = [[_dot_nt(k_ref[0, kv, pl.ds(off, n), :], queries[kv]) for kv in kvs] for off in offs]
        for off, sc, bias in zip(offs, scores, biases):
            states = tuple(_online_update(states[kv], sc[kv] if bias is None else sc[kv] + bias,
                                          vt_ref[0, kv, :, pl.ds(off, n)]) for kv in kvs)
        return states

    qsel = stack(_q_heads(q, fills))
    ch = SEL_CHUNK
    own = pl.multiple_of(((i * t) // ch) * ch, ch)
    st = sweep((init, init), ks_ref, vst_ref, qsel, [own],
               [_tile4(jnp.where(own + _row((ch, t)) <= tq1, 0.0, NEG_INF))], ch)

    def sel_chunk(c, states):
        base = pl.multiple_of(c * ch, ch)
        return sweep(states, ks_ref, vst_ref, qsel, [base + n * pc for n in range(ch // pc)], [None] * (ch // pc), pc)

    st = lax.fori_loop(0, (i * t) // ch, sel_chunk, st)
    os_t = [_finish(acc) for (_, acc) in st]

    g_t = jax.nn.sigmoid(gate_ref[0].astype(F32)).T
    rows = []
    for h in range(D_HEADS):
        kv, g = divmod(h, grp)
        cols = slice(g * t, (g + 1) * t)
        r = D_BRANCHES * h
        rows.append(g_t[r:r + 1, :] * oc_t[kv][:, cols] + g_t[r + 1:r + 2, :] * os_t[kv][:, cols]
                    + g_t[r + 2:r + 3, :] * ow_t[kv][:, cols])
    o = jnp.concatenate(rows, axis=0).T
    o_ref[0] = (o * _silu(gd_ref[0].astype(F32))).astype(BF16)


def _overlap_t(s):
    ncmp = (s - CMP_LEN) // CMP_STRIDE + 1
    nsel = s // SEL_LEN
    cs = np.arange(ncmp)[None, :] * CMP_STRIDE
    ss = np.arange(nsel)[:, None] * SEL_LEN
    ovl = np.zeros((nsel, s // CMP_STRIDE), np.float32)
    ovl[:, :ncmp] = (cs < ss + SEL_LEN) & (cs + CMP_LEN > ss)
    return ovl


def _nsa(z, cos, sin_s, qn, kc, vct, ks, vst, kw, vwt, t=256):
    b, s, _ = z.shape
    nseg = s // CMP_STRIDE
    assert nseg == LANES, "compressed-block scores are laid out on one 128-row tile"
    ovlt = _overlap_t(s)
    full = lambda shp: pl.BlockSpec(shp, lambda bi, i: (0,) * len(shp))
    dup = lambda n: pl.BlockSpec((1, 2, n, 128), lambda bi, i: (bi, 0, 0, 0))
    tr = lambda n: pl.BlockSpec((1, 2, VT_ROWS, n), lambda bi, i: (bi, 0, 0, 0))
    return pl.pallas_call(
        _nsa_kernel,
        grid=(b, s // t),
        in_specs=[pl.BlockSpec((1, t, 512), lambda bi, i: (bi, i, ODD_OFF["qd"][0] // 512)),
                  pl.BlockSpec((1, t, 512), lambda bi, i: (bi, i, ODD_OFF["gd"][0] // 512)),
                  pl.BlockSpec((1, t, 128), lambda bi, i: (bi, i, ODD_OFF["gate"][0] // 128)),
                  pl.BlockSpec((t, 128), lambda bi, i: (i, 0)),
                  pl.BlockSpec((t, 128), lambda bi, i: (i, 0)),
                  full((1, 512)), dup(nseg), tr(nseg), dup(s), tr(s), dup(s), tr(s), full(ovlt.shape)],
        out_specs=pl.BlockSpec((1, t, 512), lambda bi, i: (bi, i, 0)),
        out_shape=jax.ShapeDtypeStruct((b, s, 512), BF16),
        compiler_params=_cparams(("arbitrary", "arbitrary")),
        name="nsa",
    )(z, z, z, cos, sin_s, jnp.tile(qn, 8).reshape(1, 512), kc, vct, ks, vst, kw, vwt, jnp.asarray(ovlt, BF16))


def _permute_cols(w, order, src):
    cols = []
    for name, width in order:
        o, sw = src[name]
        blk = w[:, o:o + sw]
        if sw < width:
            blk = jnp.pad(blk, ((0, 0), (0, width - sw)))
        cols.append(blk)
    return jnp.concatenate(cols, axis=1).astype(BF16)


def _block_diag_pairs(w):
    z = jnp.zeros((4, LANES, LANES), w.dtype)
    z = z.at[:, 0:64, 0:64].set(w[0::2])
    z = z.at[:, 64:128, 64:128].set(w[1::2])
    return z.astype(BF16)


def _rope_tables(pos):
    half = HEAD_DIM // 2
    inv = ROPE_THETA ** (-jnp.arange(half, dtype=F32) / half)
    ang = pos.astype(F32)[:, None] * inv[None, :]
    cos, sin = jnp.cos(ang), jnp.sin(ang)
    cos_t = jnp.tile(cos, (1, 4))
    sin_t = jnp.tile(jnp.concatenate([-sin, sin], axis=1), (1, 2))
    return cos_t, sin_t


def _expand_compress_w1(w1):
    hdim = w1.shape[1]
    w = w1.reshape(2, CMP_STRIDE, HEAD_DIM, hdim)
    z = jnp.zeros((2, CMP_STRIDE, 2, HEAD_DIM, 2, hdim), w1.dtype)
    z = z.at[:, :, 0, :, 0, :].set(w)
    z = z.at[:, :, 1, :, 1, :].set(w)
    return z.reshape(2, CMP_STRIDE * 2 * HEAD_DIM, 2 * hdim).astype(BF16)


def _expand_compress_w2(w2):
    hdim, hd = w2.shape
    z = jnp.zeros((2 * hdim, 2 * hd), w2.dtype)
    z = z.at[0:hdim, 0:hd].set(w2)
    z = z.at[hdim:, hd:].set(w2)
    return z.astype(BF16)


def _expand_pe(pe):
    p = pe.reshape(2, CMP_STRIDE, 1, HEAD_DIM)
    return jnp.broadcast_to(p, (2, CMP_STRIDE, 2, HEAD_DIM)).reshape(2, CMP_STRIDE * 2 * HEAD_DIM)


def _even_layer(h, mem, g, mem_g, w_mem_kv, m_qn, m_kn, w_in, w_out, a_qn, a_kn, a_sinks,
                conv_w, conv_b, w_r, b_r, w_i, b_i, lam, cos, sin_s):
    b, s, d = h.shape
    h2 = h.reshape(b * s, d)
    z, ob = _inproj_rglru(h2, g, _permute_cols(w_in, EVEN_ORDER, EVEN_SRC), s, conv_w, conv_b,
                          _block_diag_pairs(w_r), b_r, _block_diag_pairs(w_i), b_i, lam)
    z = z.reshape(b, s, EVEN_COLS)
    oa = _swa(z, cos, sin_s, a_qn, a_kn, a_sinks)
    om = _mem_attention(z, EVEN_OFF["qm"][0], EVEN_OFF["gm"][0], mem, mem_g, w_mem_kv.astype(BF16), m_qn, m_kn)
    out = _outproj(h2, oa.reshape(b * s, 512), ob, om.reshape(b * s, 256), w_out.astype(BF16))
    return out.reshape(b, s, d)


def _odd_layer(h, mem, g, mem_g, w_mem_kv, m_qn, m_kn, w_in, w_out, c_lb, c_og,
               d_qn, d_kn_cmp, d_kn_slc, d_kn_win, pe_k, pe_v, w1k, w2k, w1v, w2v, cos, sin_s, cosc, sinc):
    b, s, d = h.shape
    h2 = h.reshape(b * s, d)
    z = _inproj(h2, g, _permute_cols(w_in, ODD_ORDER, ODD_SRC)).reshape(b, s, ODD_COLS)
    oc = _hgrn(z, c_lb, c_og)
    nseg = s // CMP_STRIDE
    seg = lambda name: z[:, :, ODD_OFF[name][0]:ODD_OFF[name][0] + 128].reshape(b, nseg, CMP_STRIDE * 128)
    kc, vc, ks, vs, kw, vw = _nsa_prep(
        z, seg("kcd"), seg("vcd"), cos, sin_s, cosc, sinc, d_kn_cmp, d_kn_slc, d_kn_win,
        _expand_pe(pe_k), _expand_pe(pe_v), _expand_compress_w1(w1k), _expand_compress_w2(w2k),
        _expand_compress_w1(w1v), _expand_compress_w2(w2v).T)
    od = _nsa(z, cos, sin_s, d_qn, kc, vc, ks, vs, kw, vw)
    om = _mem_attention(z, ODD_OFF["qm"][0], ODD_OFF["gm"][0], mem, mem_g, w_mem_kv.astype(BF16), m_qn, m_kn)
    out = _outproj(h2, oc.reshape(b * s, 512), od.reshape(b * s, 512), om.reshape(b * s, 256), w_out.astype(BF16))
    return out.reshape(b, s, d)


def kernel(x, mem, norm_g, mem_norm_g, mem_w_kv, mem_qn, mem_kn, ev_w_in, ev_w_out, a_qn, a_kn, a_sinks,
           b_conv_w, b_conv_b, b_w_r, b_b_r, b_w_i, b_b_i, b_lambda, od_w_in, od_w_out, c_lb, c_onorm,
           d_qn, d_kn_cmp, d_kn_slc, d_kn_win, d_pe_k, d_pe_v, d_w1k, d_w2k, d_w1v, d_w2v):
    depth = norm_g.shape[0]
    assert depth == 2 and c_lb.shape[0] == 2, "the HGRN2 lower-bound formula in the kernel is written for depth 2"
    s = x.shape[1]
    assert s % 256 == 0 and s >= D_WINDOW
    pos = jnp.arange(s)
    cos, sin_s = _rope_tables(pos)
    nseg = s // CMP_STRIDE
    cmp_end = jnp.minimum(jnp.arange(nseg) * CMP_STRIDE + CMP_LEN - 1, s - 1)
    cosc, sinc = _rope_tables(cmp_end)
    h = _even_layer(x, mem, norm_g[0], mem_norm_g[0], mem_w_kv[0], mem_qn[0], mem_kn[0], ev_w_in[0], ev_w_out[0],
                    a_qn[0], a_kn[0], a_sinks[0], b_conv_w[0], b_conv_b[0], b_w_r[0], b_b_r[0], b_w_i[0], b_b_i[0],
                    b_lambda[0], cos, sin_s)
    h = _odd_layer(h, mem, norm_g[1], mem_norm_g[1], mem_w_kv[1], mem_qn[1], mem_kn[1], od_w_in[0], od_w_out[0],
                   c_lb, c_onorm[0], d_qn[0], d_kn_cmp[0], d_kn_slc[0], d_kn_win[0], d_pe_k[0], d_pe_v[0],
                   d_w1k[0], d_w2k[0], d_w1v[0], d_w2v[0], cos, sin_s, cosc, sinc)
    return h
```

```python
import functools

import numpy as np
import jax
import jax.numpy as jnp
from jax import lax
from jax.experimental import pallas as pl
from jax.experimental.pallas import tpu as pltpu

F32 = jnp.float32
BF16 = jnp.bfloat16

D_MODEL = 1024
N_MEM = 256
HEAD_DIM = 64
ROPE_THETA = 10000.0
EPS = 1e-6
NEG_INF = -1e30
POS_INF = 1e30
MASK_BIG = 1e30
LANES = 128

A_HEADS, A_KV_HEADS, A_WINDOW = 8, 2, 128
B_WIDTH, B_BLOCKS, B_CONV, B_C = 512, 8, 4, 8.0
M_HEADS = 4
C_HEADS, C_HEAD_DIM, C_CHUNK = 4, 128, 64
D_HEADS, D_KV_HEADS = 8, 2
CMP_LEN, CMP_STRIDE, CMP_HIDDEN = 32, 16, 128
SEL_LEN, SEL_TOPK = 64, 4
D_WINDOW = 512
D_BRANCHES = 3
SCALE = HEAD_DIM ** -0.5
LOG2E = 1.4426950408889634
Q_SCALE = SCALE * LOG2E

EVEN_ORDER = [("xb", 512), ("gb", 512), ("qa", 512), ("ga", 512), ("qm", 256), ("gm", 256), ("ka", 128), ("va", 128)]
EVEN_FUSED = 1024
EVEN_SRC = {"qa": (0, 512), "ka": (512, 128), "va": (640, 128), "ga": (768, 512), "xb": (1280, 512),
            "gb": (1792, 512), "qm": (2304, 256), "gm": (2560, 256)}
ODD_ORDER = [("qc", 512), ("fc", 512), ("ic", 512), ("gc", 512), ("qd", 512), ("gd", 512), ("qm", 256), ("gm", 256),
             ("kcd", 128), ("vcd", 128), ("ksd", 128), ("vsd", 128), ("kwd", 128), ("vwd", 128), ("gate", 128)]
ODD_SRC = {"qc": (0, 512), "fc": (512, 512), "ic": (1024, 512), "gc": (1536, 512), "qd": (2048, 512),
           "kcd": (2560, 128), "vcd": (2688, 128), "ksd": (2816, 128), "vsd": (2944, 128), "kwd": (3072, 128),
           "vwd": (3200, 128), "gate": (3328, 24), "gd": (3352, 512), "qm": (3864, 256), "gm": (4120, 256)}

VMEM_LIMIT = 48 * 1024 * 1024


def _offsets(order):
    off, out = 0, {}
    for name, w in order:
        out[name] = (off, w)
        off += w
    return out, off


EVEN_OFF, EVEN_COLS = _offsets(EVEN_ORDER[2:])
ODD_OFF, ODD_COLS = _offsets(ODD_ORDER)


def _cparams(sem):
    return pltpu.CompilerParams(dimension_semantics=sem, vmem_limit_bytes=VMEM_LIMIT)


def _dot(a, b):
    return jnp.dot(a, b, preferred_element_type=F32)


def _dot_nt(a, b):
    return lax.dot_general(a, b, (((1,), (1,)), ((), ())), preferred_element_type=F32)


def _split(x):
    hi = x.astype(BF16)
    lo = (x - hi.astype(F32)).astype(BF16)
    return hi, lo


def _dot_split_lhs(x, m):
    hi, lo = _split(x)
    return _dot(hi, m) + _dot(lo, m)


def _dot_split_rhs(m, x):
    hi, lo = _split(x)
    return _dot(m, hi) + _dot(m, lo)


def _lane(shape):
    return lax.broadcasted_iota(jnp.int32, shape, len(shape) - 1)


def _row(shape):
    return lax.broadcasted_iota(jnp.int32, shape, len(shape) - 2)


def _silu(x):
    return x * jax.nn.sigmoid(x)


def _seg_ones():
    r = lax.broadcasted_iota(jnp.int32, (LANES, LANES), 0) >> 6
    c = lax.broadcasted_iota(jnp.int32, (LANES, LANES), 1) >> 6
    return jnp.where(r == c, 1.0, 0.0).astype(BF16)


def _head_rms(x, gain):
    seg = _seg_ones()
    cols = []
    for c in range(x.shape[1] // LANES):
        xc = x[:, c * LANES:(c + 1) * LANES]
        ms = _dot_split_lhs(xc * xc, seg) * (1.0 / HEAD_DIM)
        cols.append(xc * lax.rsqrt(ms + EPS))
    y = cols[0] if len(cols) == 1 else jnp.concatenate(cols, axis=1)
    return y * gain


def _rope(x, cos, sin_s):
    first = (_lane((x.shape[0], LANES)) & 63) < 32
    cols = []
    for c in range(x.shape[1] // LANES):
        xc = x[:, c * LANES:(c + 1) * LANES]
        sw = jnp.where(first, pltpu.roll(xc, 96, 1), pltpu.roll(xc, 32, 1))
        cols.append(xc * cos + sw * sin_s)
    return cols[0] if len(cols) == 1 else jnp.concatenate(cols, axis=1)


def _split_kv(k, fill=0.0):
    lo = _lane(k.shape) < 64
    return jnp.where(lo, k, fill), jnp.where(lo, pltpu.roll(k, 64, 1), fill)


def _q_heads(q, fills=None):
    lo = _lane((q.shape[0], LANES)) < 64
    out = []
    for c in range(q.shape[1] // LANES):
        qc = q[:, c * LANES:(c + 1) * LANES]
        fill = 0.0 if fills is None else fills[c]
        out.append(jnp.where(lo, qc, fill).astype(BF16))
        out.append(jnp.where(lo, pltpu.roll(qc, 64, 1), fill).astype(BF16))
    return out


VT_ROWS = HEAD_DIM + 16


def _with_ones_row(vt):
    n = vt.shape[1]
    pad = jnp.where(_row((VT_ROWS - HEAD_DIM, n)) == 0, 1.0, 0.0)
    return jnp.concatenate([vt, pad], axis=0)


MAX_SAFE_BOUND = 60.0


def _score_bound(q_gain, k_gain):
    return 1.02 * HEAD_DIM * SCALE * LOG2E * jnp.max(jnp.abs(q_gain)) * jnp.max(jnp.abs(k_gain))


def _exp_cols(s, extra=None, fixed=None):
    if fixed is not None:
        return jnp.exp2(s - fixed).astype(BF16), fixed
    m = jnp.max(s, axis=0, keepdims=True)
    if extra is not None:
        m = jnp.maximum(m, extra)
    return jnp.exp2(s - m).astype(BF16), m


def _finish(acc, extra_den=None):
    den = acc[HEAD_DIM:HEAD_DIM + 1, :]
    if extra_den is not None:
        den = den + extra_den
    return acc[0:HEAD_DIM, :] * (1.0 / den)


def _online_update(state, s, vt_tile, fixed=None):
    m, acc = state
    if fixed is not None:
        return m, acc + _dot(vt_tile, _exp_cols(s, fixed=fixed)[0])
    p, m_new = _exp_cols(s, m)
    acc = jnp.exp2(m - m_new) * acc + _dot(vt_tile, p)
    return m_new, acc


def _tile4(x):
    return jnp.concatenate([x, x, x, x], axis=1)


def _inproj_kernel(x_ref, g_ref, w_ref, o_ref):
    x = x_ref[...]
    ms = jnp.mean(x * x, axis=-1, keepdims=True)
    xn = (x * lax.rsqrt(ms + EPS) * g_ref[...]).astype(BF16)
    n = o_ref.shape[-1]
    for c in range(0, n, 512):
        w = min(512, n - c)
        o_ref[:, c:c + w] = _dot(xn, w_ref[:, c:c + w]).astype(BF16)


def _inproj(x2d, gain, w_bf16, tm=512):
    n, d = x2d.shape
    nc = w_bf16.shape[1]
    return pl.pallas_call(
        _inproj_kernel,
        grid=(n // tm,),
        in_specs=[pl.BlockSpec((tm, d), lambda i: (i, 0)),
                  pl.BlockSpec((1, d), lambda i: (0, 0)),
                  pl.BlockSpec((d, nc), lambda i: (0, 0))],
        out_specs=pl.BlockSpec((tm, nc), lambda i: (i, 0)),
        out_shape=jax.ShapeDtypeStruct((n, nc), BF16),
        compiler_params=_cparams(("arbitrary",)),
        name="inproj",
    )(x2d, gain.reshape(1, d), w_bf16)


def _outproj_kernel(h_ref, a_ref, b_ref, m_ref, w_ref, o_ref):
    a, b, m = a_ref[...], b_ref[...], m_ref[...]
    for c in range(0, D_MODEL, 256):
        acc = h_ref[:, c:c + 256]
        acc += _dot(a, w_ref[0:512, c:c + 256])
        acc += _dot(b, w_ref[512:1024, c:c + 256])
        acc += _dot(m, w_ref[1024:1280, c:c + 256])
        o_ref[:, c:c + 256] = acc


def _outproj(h2d, oa, ob, om, w_bf16, tm=1024):
    n, d = h2d.shape
    return pl.pallas_call(
        _outproj_kernel,
        grid=(n // tm,),
        in_specs=[pl.BlockSpec((tm, d), lambda i: (i, 0)),
                  pl.BlockSpec((tm, 512), lambda i: (i, 0)),
                  pl.BlockSpec((tm, 512), lambda i: (i, 0)),
                  pl.BlockSpec((tm, 256), lambda i: (i, 0)),
                  pl.BlockSpec((1280, d), lambda i: (0, 0))],
        out_specs=pl.BlockSpec((tm, d), lambda i: (i, 0)),
        out_shape=jax.ShapeDtypeStruct((n, d), F32),
        compiler_params=_cparams(("arbitrary",)),
        name="outproj",
    )(h2d, oa, ob, om, w_bf16)


def _mem_kernel(q_ref, gm_ref, mem_ref, mg_ref, wkv_ref, qn_ref, kn_ref, o_ref, k_scr, vt_scr):
    @pl.when(pl.program_id(1) == 0)
    def _():
        m = mem_ref[0]
        ms = jnp.mean(m * m, axis=-1, keepdims=True)
        mn = (m * lax.rsqrt(ms + EPS) * mg_ref[...]).astype(BF16)
        kv = _dot(mn, wkv_ref[...])
        km = _head_rms(kv[:, 0:256], kn_ref[...])
        for c in range(2):
            k0, k1 = _split_kv(km[:, c * LANES:(c + 1) * LANES])
            k_scr[2 * c] = k0.astype(BF16)
            k_scr[2 * c + 1] = k1.astype(BF16)
        vt = kv[:, 256:512].T
        for h in range(M_HEADS):
            vt_scr[h] = _with_ones_row(vt[h * HEAD_DIM:(h + 1) * HEAD_DIM, :]).astype(BF16)

    q = _head_rms(q_ref[0].astype(F32), qn_ref[...]) * Q_SCALE
    scores = [_dot_nt(k_scr[h], qh) for h, qh in enumerate(_q_heads(q))]
    rows = []
    for h, sc in enumerate(scores):
        p, _ = _exp_cols(sc)
        rows.append(_finish(_dot(vt_scr[h], p)))
    o = jnp.concatenate(rows, axis=0).T
    o_ref[0] = (o * _silu(gm_ref[0].astype(F32))).astype(BF16)


def _mem_attention(z, q_off, g_off, mem, mem_g, wkv_bf16, qn, kn, tq=512):
    b, s, _ = z.shape
    return pl.pallas_call(
        _mem_kernel,
        grid=(b, s // tq),
        in_specs=[pl.BlockSpec((1, tq, 256), lambda bi, i: (bi, i, q_off // 256)),
                  pl.BlockSpec((1, tq, 256), lambda bi, i: (bi, i, g_off // 256)),
                  pl.BlockSpec((1, N_MEM, D_MODEL), lambda bi, i: (bi, 0, 0)),
                  pl.BlockSpec((1, D_MODEL), lambda bi, i: (0, 0)),
                  pl.BlockSpec((D_MODEL, 512), lambda bi, i: (0, 0)),
                  pl.BlockSpec((1, 256), lambda bi, i: (0, 0)),
                  pl.BlockSpec((1, 256), lambda bi, i: (0, 0))],
        out_specs=pl.BlockSpec((1, tq, 256), lambda bi, i: (bi, i, 0)),
        out_shape=jax.ShapeDtypeStruct((b, s, 256), BF16),
        scratch_shapes=[pltpu.VMEM((M_HEADS, N_MEM, LANES), BF16), pltpu.VMEM((M_HEADS, VT_ROWS, N_MEM), BF16)],
        compiler_params=_cparams(("arbitrary", "arbitrary")),
        name="mem_attention",
    )(z, z, mem, mem_g.reshape(1, D_MODEL), wkv_bf16, jnp.tile(qn, 4).reshape(1, 256), jnp.tile(kn, 4).reshape(1, 256))


def _swa_kernel(q_ref, g_ref, kc_ref, kp_ref, vc_ref, vp_ref, cosc_ref, sinc_ref, cosp_ref, sinp_ref,
                qn_ref, kn_ref, sink_ref, o_ref):
    i = pl.program_id(1)
    t = q_ref.shape[1]
    tp = kp_ref.shape[1]
    grp = A_HEADS // A_KV_HEADS
    q = _rope(_head_rms(q_ref[0].astype(F32), qn_ref[...]), cosc_ref[...], sinc_ref[...]) * Q_SCALE
    qh = _q_heads(q)
    kc = _rope(_head_rms(kc_ref[0].astype(F32), kn_ref[...]), cosc_ref[...], sinc_ref[...])
    kp = _rope(_head_rms(kp_ref[0].astype(F32), kn_ref[...]), cosp_ref[...], sinp_ref[...])
    ks = [x.astype(BF16) for x in _split_kv(jnp.concatenate([kp, kc], axis=0))]
    vt = jnp.concatenate([vp_ref[0].astype(F32).T, vc_ref[0].astype(F32).T], axis=1)
    ts = i * t - tp + _row((tp + t, t))
    rel = i * t + _lane((tp + t, t)) - ts
    bias = _tile4(jnp.where((rel >= 0) & (rel < A_WINDOW) & (ts >= 0), 0.0, NEG_INF))
    scores = [_dot_nt(ks[kv], jnp.concatenate(qh[kv * grp:(kv + 1) * grp], axis=0)) for kv in range(A_KV_HEADS)]
    rows = []
    for kv in range(A_KV_HEADS):
        sink = jnp.concatenate([jnp.broadcast_to(sink_ref[kv * grp + g:kv * grp + g + 1, 0:1], (1, t))
                                for g in range(grp)], axis=1) * LOG2E
        p, m = _exp_cols(scores[kv] + bias, sink)
        vt_kv = _with_ones_row(vt[kv * HEAD_DIM:(kv + 1) * HEAD_DIM, :]).astype(BF16)
        o_t = _finish(_dot(vt_kv, p), jnp.exp2(sink - m))
        rows.extend(o_t[:, g * t:(g + 1) * t] for g in range(grp))
    o = jnp.concatenate(rows, axis=0).T
    o_ref[0] = (o * _silu(g_ref[0].astype(F32))).astype(BF16)


def _swa(z, cos, sin_s, qn, kn, sinks, t=256):
    b, s, _ = z.shape
    tp = A_WINDOW
    qo, go = EVEN_OFF["qa"][0] // 512, EVEN_OFF["ga"][0] // 512
    ko, vo = EVEN_OFF["ka"][0] // 128, EVEN_OFF["va"][0] // 128
    prev = lambda i: jnp.maximum(i * (t // tp) - 1, 0)
    return pl.pallas_call(
        _swa_kernel,
        grid=(b, s // t),
        in_specs=[pl.BlockSpec((1, t, 512), lambda bi, i: (bi, i, qo)),
                  pl.BlockSpec((1, t, 512), lambda bi, i: (bi, i, go)),
                  pl.BlockSpec((1, t, 128), lambda bi, i: (bi, i, ko)),
                  pl.BlockSpec((1, tp, 128), lambda bi, i: (bi, prev(i), ko)),
                  pl.BlockSpec((1, t, 128), lambda bi, i: (bi, i, vo)),
                  pl.BlockSpec((1, tp, 128), lambda bi, i: (bi, prev(i), vo)),
                  pl.BlockSpec((t, 128), lambda bi, i: (i, 0)),
                  pl.BlockSpec((t, 128), lambda bi, i: (i, 0)),
                  pl.BlockSpec((tp, 128), lambda bi, i: (prev(i), 0)),
                  pl.BlockSpec((tp, 128), lambda bi, i: (prev(i), 0)),
                  pl.BlockSpec((1, 512), lambda bi, i: (0, 0)),
                  pl.BlockSpec((1, 128), lambda bi, i: (0, 0)),
                  pl.BlockSpec((A_HEADS, 128), lambda bi, i: (0, 0))],
        out_specs=pl.BlockSpec((1, t, 512), lambda bi, i: (bi, i, 0)),
        out_shape=jax.ShapeDtypeStruct((b, s, 512), BF16),
        compiler_params=_cparams(("arbitrary", "arbitrary")),
        name="swa",
    )(z, z, z, z, z, z, cos, sin_s, cos, sin_s, jnp.tile(qn, 8).reshape(1, 512), jnp.tile(kn, 2).reshape(1, 128),
      jnp.broadcast_to(sinks.reshape(A_HEADS, 1), (A_HEADS, 128)))


def _rglru_tile(first, xb, gb, cw_ref, cb_ref, wr_ref, br_ref, wi_ref, bi_ref, lam_ref, xbuf, hcar, a_scr, u_scr, h_scr,
                between):
    t = xb.shape[0]

    @pl.when(first)
    def _():
        xbuf[0:8, :] = jnp.zeros((8, B_WIDTH), F32)
        hcar[...] = jnp.zeros((8, B_WIDTH), F32)

    xbuf[8:t + 8, :] = xb
    xc = cb_ref[...] + cw_ref[0:1, :] * xbuf[5:t + 5, :]
    for j in range(1, B_CONV):
        xc = xc + cw_ref[j:j + 1, :] * xbuf[5 + j:t + 5 + j, :]
    xbuf[0:8, :] = xbuf[t:t + 8, :]

    r_cols, i_cols = [], []
    for c in range(B_WIDTH // LANES):
        xcc = xc[:, c * LANES:(c + 1) * LANES].astype(BF16)
        r_cols.append(_dot(xcc, wr_ref[c]))
        i_cols.append(_dot(xcc, wi_ref[c]))
    between()
    r = jax.nn.sigmoid(jnp.concatenate(r_cols, axis=1) + br_ref[...])
    ig = jax.nn.sigmoid(jnp.concatenate(i_cols, axis=1) + bi_ref[...])
    nl = -lam_ref[...]
    softplus = jnp.maximum(nl, 0.0) + jnp.log(1.0 + jnp.exp(-jnp.abs(nl)))
    log_a = -B_C * r * softplus
    a = jnp.exp(log_a)
    om = 1.0 - a * a
    u = om * lax.rsqrt(jnp.maximum(om, 1e-30)) * (ig * xc)

    a = a.reshape(t // 8, 8, B_WIDTH)
    u = u.reshape(t // 8, 8, B_WIDTH)
    r8 = lax.broadcasted_iota(jnp.int32, a.shape, 1)
    for d in (1, 2, 4):
        a_sh = pltpu.roll(a, d, 1)
        u_sh = pltpu.roll(u, d, 1)
        m = r8 >= d
        u = jnp.where(m, a * u_sh + u, u)
        a = jnp.where(m, a * a_sh, a)
    a_scr[...] = a.reshape(t, B_WIDTH)
    u_scr[...] = u.reshape(t, B_WIDTH)

    def body(j, h):
        off = pl.multiple_of(j * 8, 8)
        hh = a_scr[pl.ds(off, 8), :] * h + u_scr[pl.ds(off, 8), :]
        h_scr[pl.ds(off, 8), :] = hh
        return hh[7:8, :]

    h_last = lax.fori_loop(0, t // 8, body, hcar[0:1, :], unroll=True)
    hcar[0:1, :] = h_last
    return h_scr[...] * _silu(gb)


def _inproj_rglru_kernel(tiles_per_seq, x_ref, g_ref, w_ref, cw_ref, cb_ref, wr_ref, br_ref, wi_ref, bi_ref, lam_ref,
                         z_ref, ob_ref, xbuf, hcar, a_scr, u_scr, h_scr):
    x = x_ref[...]
    ms = jnp.mean(x * x, axis=-1, keepdims=True)
    xn = (x * lax.rsqrt(ms + EPS) * g_ref[...]).astype(BF16)
    xb = _dot(xn, w_ref[:, 0:B_WIDTH])
    gb = _dot(xn, w_ref[:, B_WIDTH:2 * B_WIDTH])
    first = pl.program_id(0) % tiles_per_seq == 0

    def rest_of_projection():
        n = z_ref.shape[-1]
        for c in range(0, n, 512):
            w = min(512, n - c)
            z_ref[:, c:c + w] = _dot(xn, w_ref[:, EVEN_FUSED + c:EVEN_FUSED + c + w]).astype(BF16)

    ob_ref[...] = _rglru_tile(first, xb, gb, cw_ref, cb_ref, wr_ref, br_ref, wi_ref, bi_ref, lam_ref,
                              xbuf, hcar, a_scr, u_scr, h_scr, rest_of_projection).astype(BF16)


def _inproj_rglru(x2d, gain, w_bf16, seq, conv_w, conv_b, wr_bd, b_r, wi_bd, b_i, lam, tm=512):
    n, d = x2d.shape
    nc = w_bf16.shape[1]
    row = lambda v: v.reshape(1, B_WIDTH)
    full = lambda shp: pl.BlockSpec(shp, lambda i: (0,) * len(shp))
    return pl.pallas_call(
        functools.partial(_inproj_rglru_kernel, seq // tm),
        grid=(n // tm,),
        in_specs=[pl.BlockSpec((tm, d), lambda i: (i, 0)), full((1, d)), full((d, nc)),
                  full((B_CONV, B_WIDTH)), full((1, B_WIDTH)), full((4, LANES, LANES)), full((1, B_WIDTH)),
                  full((4, LANES, LANES)), full((1, B_WIDTH)), full((1, B_WIDTH))],
        out_specs=[pl.BlockSpec((tm, nc - EVEN_FUSED), lambda i: (i, 0)), pl.BlockSpec((tm, B_WIDTH), lambda i: (i, 0))],
        out_shape=[jax.ShapeDtypeStruct((n, nc - EVEN_FUSED), BF16), jax.ShapeDtypeStruct((n, B_WIDTH), BF16)],
        scratch_shapes=[pltpu.VMEM((tm + 8, B_WIDTH), F32), pltpu.VMEM((8, B_WIDTH), F32),
                        pltpu.VMEM((tm, B_WIDTH), F32), pltpu.VMEM((tm, B_WIDTH), F32), pltpu.VMEM((tm, B_WIDTH), F32)],
        compiler_params=_cparams(("arbitrary",)),
        name="inproj_rglru",
    )(x2d, gain.reshape(1, d), w_bf16, conv_w, row(conv_b), wr_bd, row(b_r), wi_bd, row(b_i), row(lam))


def _hgrn_consts():
    c = C_CHUNK
    t = np.arange(c)[:, None]
    s = np.arange(c)[None, :]
    mats = [(s <= t)]
    masks = []
    hs = c // 2
    while hs >= 1:
        mid = (t // (2 * hs)) * 2 * hs + hs - 1
        mats.append(s <= mid)
        same = (t // (2 * hs)) == (s // (2 * hs))
        masks.append(same & ((t // hs) % 2 == 1) & ((s // hs) % 2 == 0))
        hs //= 2
    pairs = [np.concatenate(masks[n:n + 2], axis=1) for n in range(0, len(masks), 2)]
    return np.concatenate(mats, axis=0).astype(np.float32), np.stack(pairs).astype(np.float32)


def _hgrn_kernel(q_ref, f_ref, i_ref, g_ref, lb_ref, og_ref, mst_ref, msk_ref, o_ref, st_scr):
    @pl.when(pl.program_id(1) == 0)
    def _():
        st_scr[...] = jnp.zeros(st_scr.shape, F32)

    c = C_CHUNK
    npair = msk_ref.shape[0]
    p = lb_ref[...]
    pm = jnp.maximum(p[0:1, :], p[1:2, :])
    e0, e1 = jnp.exp(p[0:1, :] - pm), jnp.exp(p[1:2, :] - pm)
    lb = e1 / (e0 + e1)
    zeros = jnp.zeros((c, LANES), BF16)
    for ch in range(q_ref.shape[1] // c):
        rows = slice(ch * c, (ch + 1) * c)
        f = lb + (1.0 - lb) * jax.nn.sigmoid(f_ref[0, rows, :].astype(F32))
        g = jnp.log2(f)
        g_hi, g_lo = _split(g)
        b_all = _dot(mst_ref[0:c, :], g_hi) + _dot(mst_ref[0:c, :], g_lo)
        bm_all = _dot(mst_ref[c:, :], g_hi)
        kk_all = 1.0 - f
        qf_all = _silu(q_ref[0, rows, :].astype(F32))
        v_all = i_ref[0, rows, :].astype(F32)
        kk_bf, qf_bf = kk_all.astype(BF16), qf_all.astype(BF16)
        heads = range(C_HEADS)
        cols = [slice(h * LANES, (h + 1) * LANES) for h in heads]
        sts = [st_scr[h] for h in heads]
        os = [_dot_nt((qf_all[:, cols[h]] * jnp.exp2(b_all[:, cols[h]])).astype(BF16), sts[h].astype(BF16)) for h in heads]
        att = [[] for _ in heads]
        for n in range(npair):
            lhs, rhs = [], []
            for h in heads:
                b = b_all[:, cols[h]]
                qt, kt = [], []
                for l in (2 * n, 2 * n + 1):
                    bm = bm_all[l * c:(l + 1) * c, cols[h]]
                    qt.append(qf_bf[:, cols[h]] * jnp.exp2(jnp.minimum(b - bm, 1.0)).astype(BF16))
                    kt.append(kk_bf[:, cols[h]] * jnp.exp2(jnp.minimum(bm - b, 1.0)).astype(BF16))
                lhs.append(jnp.concatenate(qt, axis=1))
                rhs.append(jnp.concatenate([jnp.concatenate([kt[0], zeros], axis=1),
                                            jnp.concatenate([zeros, kt[1]], axis=1)], axis=0))
            prods = [_dot_nt(lhs[h], rhs[h]) for h in heads]
            for h in heads:
                att[h].append(jnp.where(msk_ref[n] > 0.5, prods[h], 0.0).astype(BF16))
        vbs = [v_all[:, cols[h]].astype(BF16) for h in heads]
        intra = [_dot(jnp.concatenate(att[h], axis=1), jnp.concatenate([vbs[h]] * (2 * npair), axis=0)) for h in heads]
        b_last = b_all[c - 1:c, :]
        kd = (kk_all * jnp.exp2(b_last - b_all)).astype(BF16)
        upd = [_dot(v_all[:, cols[h]].T.astype(BF16), kd[:, cols[h]]) for h in heads]
        decay = jnp.exp2(b_last)
        diag = qf_all * kk_all
        outs = []
        for h in heads:
            st_scr[h] = sts[h] * decay[:, cols[h]] + upd[h]
            o = os[h] + intra[h] + jnp.sum(diag[:, cols[h]], axis=-1, keepdims=True) * v_all[:, cols[h]]
            ms = jnp.mean(o * o, axis=-1, keepdims=True)
            outs.append(o * lax.rsqrt(ms + EPS) * og_ref[...])
        o_ref[0, rows, :] = (jnp.concatenate(outs, axis=1) * _silu(g_ref[0, rows, :].astype(F32))).astype(BF16)


def _hgrn(z, c_lb, c_og, t=512):
    b, s, _ = z.shape
    mst, msk = _hgrn_consts()
    blk = lambda name: pl.BlockSpec((1, t, 512), lambda bi, i, o=ODD_OFF[name][0] // 512: (bi, i, o))
    full = lambda shp: pl.BlockSpec(shp, lambda bi, i: (0,) * len(shp))
    return pl.pallas_call(
        _hgrn_kernel,
        grid=(b, s // t),
        in_specs=[blk("qc"), blk("fc"), blk("ic"), blk("gc"), full(c_lb.shape), full((1, C_HEAD_DIM)),
                  full(mst.shape), full(msk.shape)],
        out_specs=pl.BlockSpec((1, t, 512), lambda bi, i: (bi, i, 0)),
        out_shape=jax.ShapeDtypeStruct((b, s, 512), BF16),
        scratch_shapes=[pltpu.VMEM((C_HEADS, C_HEAD_DIM, C_HEAD_DIM), F32)],
        compiler_params=_cparams(("arbitrary", "arbitrary")),
        name="hgrn2",
    )(z, z, z, z, c_lb, c_og.reshape(1, C_HEAD_DIM), jnp.asarray(mst, BF16), jnp.asarray(msk, F32))


def _nsa_prep_kernel(kcr_ref, vcr_ref, ks_ref, vs_ref, kw_ref, vw_ref, cos_ref, sin_ref, cosc_ref, sinc_ref,
                     kncmp_ref, knslc_ref, knwin_ref, pek_ref, pev_ref, w1k_ref, w2k_ref, w1v_ref, w2vt_ref,
                     kc_o, vct_o, ks_o, vst_o, kw_o, vwt_o):
    def hidden(xr_ref, pe_ref, w1_ref):
        xr = xr_ref[0].astype(F32)
        top = _dot((xr + pe_ref[0:1, :]).astype(BF16), w1_ref[0])
        bot = _dot((xr + pe_ref[1:2, :]).astype(BF16), w1_ref[1])
        pre = top + pltpu.roll(bot, bot.shape[0] - 1, 0)
        return _silu(pre).astype(BF16)

    kc = _dot(hidden(kcr_ref, pek_ref, w1k_ref), w2k_ref[...])
    kc = _rope(_head_rms(kc, kncmp_ref[...]), cosc_ref[...], sinc_ref[...])
    d0, d1 = _split_kv(kc)
    kc_o[0, 0] = d0.astype(BF16)
    kc_o[0, 1] = d1.astype(BF16)
    vct = _dot_nt(w2vt_ref[...], hidden(vcr_ref, pev_ref, w1v_ref))
    for kv in range(D_KV_HEADS):
        vct_o[0, kv] = _with_ones_row(vct[kv * HEAD_DIM:(kv + 1) * HEAD_DIM, :]).astype(BF16)

    s = ks_ref.shape[1]
    step = 256
    for r0 in range(0, s, step):
        rows = slice(r0, r0 + step)
        cos, sin = cos_ref[rows, :], sin_ref[rows, :]
        ks = _rope(_head_rms(ks_ref[0, rows, :].astype(F32), knslc_ref[...]), cos, sin)
        kw = _rope(_head_rms(kw_ref[0, rows, :].astype(F32), knwin_ref[...]), cos, sin)
        blk = (r0 + _row((step, LANES))) // SEL_LEN
        onehot = jnp.where(_lane((step, LANES)) - HEAD_DIM == blk, MASK_BIG, 0.0)
        for src, dst, fill in ((ks, ks_o, onehot), (kw, kw_o, 0.0)):
            d0, d1 = _split_kv(src, fill)
            dst[0, 0, rows, :] = d0.astype(BF16)
            dst[0, 1, rows, :] = d1.astype(BF16)
        for src, dst in ((vs_ref, vst_o), (vw_ref, vwt_o)):
            vt = src[0, rows, :].astype(F32).T
            for kv in range(D_KV_HEADS):
                dst[0, kv, :, rows] = _with_ones_row(vt[kv * HEAD_DIM:(kv + 1) * HEAD_DIM, :]).astype(BF16)


def _nsa_prep(z, kcr, vcr, cos, sin_s, cosc, sinc, kn_cmp, kn_slc, kn_win, pek, pev, w1k, w2k, w1v, w2v):
    b, s, _ = z.shape
    nseg = s // CMP_STRIDE
    zb = lambda name: pl.BlockSpec((1, s, 128), lambda bi, o=ODD_OFF[name][0] // 128: (bi, 0, o))
    full = lambda shp: pl.BlockSpec(shp, lambda bi: (0,) * len(shp))
    seg = pl.BlockSpec((1, nseg, 2048), lambda bi: (bi, 0, 0))
    g2 = lambda g: jnp.tile(g, 2).reshape(1, 128)
    dup = lambda n: (jax.ShapeDtypeStruct((b, 2, n, 128), BF16), pl.BlockSpec((1, 2, n, 128), lambda bi: (bi, 0, 0, 0)))
    tr = lambda n: (jax.ShapeDtypeStruct((b, 2, VT_ROWS, n), BF16),
                    pl.BlockSpec((1, 2, VT_ROWS, n), lambda bi: (bi, 0, 0, 0)))
    outs = [dup(nseg), tr(nseg), dup(s), tr(s), dup(s), tr(s)]
    return pl.pallas_call(
        _nsa_prep_kernel,
        grid=(b,),
        in_specs=[seg, seg, zb("ksd"), zb("vsd"), zb("kwd"), zb("vwd"),
                  full((s, 128)), full((s, 128)), full((nseg, 128)), full((nseg, 128)),
                  full((1, 128)), full((1, 128)), full((1, 128)), full((2, 2048)), full((2, 2048)),
                  full((2, 2048, 256)), full((256, 128)), full((2, 2048, 256)), full((128, 256))],
        out_specs=[o[1] for o in outs],
        out_shape=[o[0] for o in outs],
        compiler_params=_cparams(("arbitrary",)),
        name="nsa_prep",
    )(kcr, vcr, z, z, z, z, cos, sin_s, cosc, sinc, g2(kn_cmp), g2(kn_slc), g2(kn_win), pek, pev, w1k, w2k, w1v, w2v)


SEL_CHUNK = 512
SEL_PIECE = 128


def _nsa_kernel(bound_ref, *refs):
    small = jnp.maximum(jnp.maximum(bound_ref[0], bound_ref[1]), bound_ref[2]) <= MAX_SAFE_BOUND

    @pl.when(small)
    def _():
        _nsa_body((bound_ref[0], bound_ref[1], bound_ref[2]), *refs)

    @pl.when(jnp.logical_not(small))
    def _():
        _nsa_body((None, None, None), *refs)


def _nsa_body(fixed, q_ref, gd_ref, gate_ref, cos_ref, sin_ref, qn_ref, kc_ref, vct_ref, ks_ref, vst_ref, kw_ref, vwt_ref,
              ovlt_ref, o_ref):
    fix_c, fix_s, fix_w = fixed
    i = pl.program_id(1)
    t = q_ref.shape[1]
    grp = D_HEADS // D_KV_HEADS
    w = grp * t
    kvs = range(D_KV_HEADS)
    q = _rope(_head_rms(q_ref[0].astype(F32), qn_ref[...]), cos_ref[...], sin_ref[...]) * Q_SCALE
    stack = lambda heads: [jnp.concatenate(heads[kv * grp:(kv + 1) * grp], axis=0) for kv in kvs]
    qs = stack(_q_heads(q))
    tq1 = i * t + _lane((1, t))

    nwin = D_WINDOW + t
    win0 = pl.multiple_of(jnp.maximum(i * t + t - nwin, 0), t)
    s_cmp = [_dot_nt(kc_ref[0, kv], qs[kv]) for kv in kvs]
    s_win = [_dot_nt(kw_ref[0, kv, pl.ds(win0, nwin), :], qs[kv]) for kv in kvs]

    ncmp = kc_ref.shape[2] - 1
    nsel = ovlt_ref.shape[0]
    crow = _row((kc_ref.shape[2], t))
    bias_c = _tile4(jnp.where((crow * CMP_STRIDE + (CMP_LEN - 1) <= tq1) & (crow < ncmp), 0.0, NEG_INF))
    row_ok = _tile4(tq1 >= CMP_LEN - 1)
    es = [_exp_cols(s_cmp[kv] + bias_c, fixed=fix_c)[0] for kv in kvs]
    accs = [_dot(vct_ref[0, kv], es[kv]) for kv in kvs]
    invs = [jnp.where(row_ok, 1.0 / accs[kv][HEAD_DIM:HEAD_DIM + 1, :], 0.0) for kv in kvs]
    oc_t = [accs[kv][0:HEAD_DIM, :] * invs[kv] for kv in kvs]
    psums = []
    for kv in kvs:
        p = es[kv].astype(F32) * invs[kv]
        psums.append(p[:, 0:t] + p[:, t:2 * t] + p[:, 2 * t:3 * t] + p[:, 3 * t:4 * t])
    imps = [_dot_split_rhs(ovlt_ref[...], psums[kv]) for kv in kvs]

    rel = tq1 - (win0 + _row((nwin, t)))
    bias_w = _tile4(jnp.where((rel >= 0) & (rel < D_WINDOW), 0.0, NEG_INF))
    ow_t = [_finish(_dot(vwt_ref[0, kv, :, pl.ds(win0, nwin)], _exp_cols(s_win[kv] + bias_w, fixed=fix_w)[0]))
            for kv in kvs]

    jrow = _row((nsel, t))
    jrow_f = jrow.astype(F32)
    cur = (i * t + _lane((nsel, t))) // SEL_LEN
    forced = (jrow == 0) | (jrow == cur)
    fills = []
    for kv in kvs:
        score = jnp.where(forced, POS_INF, jnp.where(jrow <= cur, imps[kv], NEG_INF))
        chosen = jnp.zeros((nsel, t), F32)
        for _ in range(min(SEL_TOPK, nsel)):
            mx = jnp.max(score, axis=0, keepdims=True)
            first = jnp.min(jnp.where(score == mx, jrow_f, 1e9), axis=0, keepdims=True)
            hit = jrow_f == first
            chosen = jnp.where(hit, 1.0, chosen)
            score = jnp.where(hit, -3e38, score)
        frame = jnp.concatenate([jnp.zeros((HEAD_DIM, t), F32), chosen - 1.0,
                                 jnp.zeros((LANES - HEAD_DIM - nsel, t), F32)], axis=0)
        fills.extend([frame.T] * (grp // 2))

    init = (jnp.full((1, w), NEG_INF, F32), jnp.zeros((VT_ROWS, w), F32))
    pc = SEL_PIECE

    def sweep(states, k_ref, vt_ref, queries, offs, biases, n):
        scores = [[_dot_nt(k_ref[0, kv, pl.ds(off, n), :], queries[kv]) for kv in kvs] for off in offs]
        for off, sc, bias in zip(offs, scores, biases):
            states = tuple(_online_update(states[kv], sc[kv] if bias is None else sc[kv] + bias,
                                          vt_ref[0, kv, :, pl.ds(off, n)], fix_s) for kv in kvs)
        return states

    qsel = stack(_q_heads(q, fills))
    ch = SEL_CHUNK
    own = pl.multiple_of(((i * t) // ch) * ch, ch)
    st = sweep((init, init), ks_ref, vst_ref, qsel, [own],
               [_tile4(jnp.where(own + _row((ch, t)) <= tq1, 0.0, NEG_INF))], ch)

    def sel_chunk(c, states):
        base = pl.multiple_of(c * ch, ch)
        return sweep(states, ks_ref, vst_ref, qsel, [base + n * pc for n in range(ch // pc)], [None] * (ch // pc), pc)

    st = lax.fori_loop(0, (i * t) // ch, sel_chunk, st)
    os_t = [_finish(acc) for (_, acc) in st]

    g_t = jax.nn.sigmoid(gate_ref[0].astype(F32)).T
    rows = []
    for h in range(D_HEADS):
        kv, g = divmod(h, grp)
        cols = slice(g * t, (g + 1) * t)
        r = D_BRANCHES * h
        rows.append(g_t[r:r + 1, :] * oc_t[kv][:, cols] + g_t[r + 1:r + 2, :] * os_t[kv][:, cols]
                    + g_t[r + 2:r + 3, :] * ow_t[kv][:, cols])
    o = jnp.concatenate(rows, axis=0).T
    o_ref[0] = (o * _silu(gd_ref[0].astype(F32))).astype(BF16)


def _overlap_t(s):
    ncmp = (s - CMP_LEN) // CMP_STRIDE + 1
    nsel = s // SEL_LEN
    cs = np.arange(ncmp)[None, :] * CMP_STRIDE
    ss = np.arange(nsel)[:, None] * SEL_LEN
    ovl = np.zeros((nsel, s // CMP_STRIDE), np.float32)
    ovl[:, :ncmp] = (cs < ss + SEL_LEN) & (cs + CMP_LEN > ss)
    return ovl


def _nsa(z, cos, sin_s, qn, bounds, kc, vct, ks, vst, kw, vwt, t=256):
    b, s, _ = z.shape
    nseg = s // CMP_STRIDE
    assert nseg == LANES, "compressed-block scores are laid out on one 128-row tile"
    ovlt = _overlap_t(s)
    full = lambda shp: pl.BlockSpec(shp, lambda bi, i: (0,) * len(shp))
    dup = lambda n: pl.BlockSpec((1, 2, n, 128), lambda bi, i: (bi, 0, 0, 0))
    tr = lambda n: pl.BlockSpec((1, 2, VT_ROWS, n), lambda bi, i: (bi, 0, 0, 0))
    return pl.pallas_call(
        _nsa_kernel,
        grid=(b, s // t),
        in_specs=[pl.BlockSpec(memory_space=pltpu.SMEM),
                  pl.BlockSpec((1, t, 512), lambda bi, i: (bi, i, ODD_OFF["qd"][0] // 512)),
                  pl.BlockSpec((1, t, 512), lambda bi, i: (bi, i, ODD_OFF["gd"][0] // 512)),
                  pl.BlockSpec((1, t, 128), lambda bi, i: (bi, i, ODD_OFF["gate"][0] // 128)),
                  pl.BlockSpec((t, 128), lambda bi, i: (i, 0)),
                  pl.BlockSpec((t, 128), lambda bi, i: (i, 0)),
                  full((1, 512)), dup(nseg), tr(nseg), dup(s), tr(s), dup(s), tr(s), full(ovlt.shape)],
        out_specs=pl.BlockSpec((1, t, 512), lambda bi, i: (bi, i, 0)),
        out_shape=jax.ShapeDtypeStruct((b, s, 512), BF16),
        compiler_params=_cparams(("arbitrary", "arbitrary")),
        name="nsa",
    )(bounds, z, z, z, cos, sin_s, jnp.tile(qn, 8).reshape(1, 512), kc, vct, ks, vst, kw, vwt, jnp.asarray(ovlt, BF16))


def _permute_cols(w, order, src):
    cols = []
    for name, width in order:
        o, sw = src[name]
        blk = w[:, o:o + sw]
        if sw < width:
            blk = jnp.pad(blk, ((0, 0), (0, width - sw)))
        cols.append(blk)
    return jnp.concatenate(cols, axis=1).astype(BF16)


def _block_diag_pairs(w):
    z = jnp.zeros((4, LANES, LANES), w.dtype)
    z = z.at[:, 0:64, 0:64].set(w[0::2])
    z = z.at[:, 64:128, 64:128].set(w[1::2])
    return z.astype(BF16)


def _rope_tables(pos):
    half = HEAD_DIM // 2
    inv = ROPE_THETA ** (-jnp.arange(half, dtype=F32) / half)
    ang = pos.astype(F32)[:, None] * inv[None, :]
    cos, sin = jnp.cos(ang), jnp.sin(ang)
    cos_t = jnp.tile(cos, (1, 4))
    sin_t = jnp.tile(jnp.concatenate([-sin, sin], axis=1), (1, 2))
    return cos_t, sin_t


def _expand_compress_w1(w1):
    hdim = w1.shape[1]
    w = w1.reshape(2, CMP_STRIDE, HEAD_DIM, hdim)
    z = jnp.zeros((2, CMP_STRIDE, 2, HEAD_DIM, 2, hdim), w1.dtype)
    z = z.at[:, :, 0, :, 0, :].set(w)
    z = z.at[:, :, 1, :, 1, :].set(w)
    return z.reshape(2, CMP_STRIDE * 2 * HEAD_DIM, 2 * hdim).astype(BF16)


def _expand_compress_w2(w2):
    hdim, hd = w2.shape
    z = jnp.zeros((2 * hdim, 2 * hd), w2.dtype)
    z = z.at[0:hdim, 0:hd].set(w2)
    z = z.at[hdim:, hd:].set(w2)
    return z.astype(BF16)


def _expand_pe(pe):
    p = pe.reshape(2, CMP_STRIDE, 1, HEAD_DIM)
    return jnp.broadcast_to(p, (2, CMP_STRIDE, 2, HEAD_DIM)).reshape(2, CMP_STRIDE * 2 * HEAD_DIM)


def _even_layer(h, mem, g, mem_g, w_mem_kv, m_qn, m_kn, w_in, w_out, a_qn, a_kn, a_sinks,
                conv_w, conv_b, w_r, b_r, w_i, b_i, lam, cos, sin_s):
    b, s, d = h.shape
    h2 = h.reshape(b * s, d)
    z, ob = _inproj_rglru(h2, g, _permute_cols(w_in, EVEN_ORDER, EVEN_SRC), s, conv_w, conv_b,
                          _block_diag_pairs(w_r), b_r, _block_diag_pairs(w_i), b_i, lam)
    z = z.reshape(b, s, EVEN_COLS)
    oa = _swa(z, cos, sin_s, a_qn, a_kn, a_sinks)
    om = _mem_attention(z, EVEN_OFF["qm"][0], EVEN_OFF["gm"][0], mem, mem_g, w_mem_kv.astype(BF16), m_qn, m_kn)
    out = _outproj(h2, oa.reshape(b * s, 512), ob, om.reshape(b * s, 256), w_out.astype(BF16))
    return out.reshape(b, s, d)


def _odd_layer(h, mem, g, mem_g, w_mem_kv, m_qn, m_kn, w_in, w_out, c_lb, c_og,
               d_qn, d_kn_cmp, d_kn_slc, d_kn_win, pe_k, pe_v, w1k, w2k, w1v, w2v, cos, sin_s, cosc, sinc):
    b, s, d = h.shape
    h2 = h.reshape(b * s, d)
    z = _inproj(h2, g, _permute_cols(w_in, ODD_ORDER, ODD_SRC)).reshape(b, s, ODD_COLS)
    oc = _hgrn(z, c_lb, c_og)
    nseg = s // CMP_STRIDE
    seg = lambda name: z[:, :, ODD_OFF[name][0]:ODD_OFF[name][0] + 128].reshape(b, nseg, CMP_STRIDE * 128)
    kc, vc, ks, vs, kw, vw = _nsa_prep(
        z, seg("kcd"), seg("vcd"), cos, sin_s, cosc, sinc, d_kn_cmp, d_kn_slc, d_kn_win,
        _expand_pe(pe_k), _expand_pe(pe_v), _expand_compress_w1(w1k), _expand_compress_w2(w2k),
        _expand_compress_w1(w1v), _expand_compress_w2(w2v).T)
    bounds = jnp.stack([_score_bound(d_qn, kn) for kn in (d_kn_cmp, d_kn_slc, d_kn_win)]).astype(F32)
    od = _nsa(z, cos, sin_s, d_qn, bounds, kc, vc, ks, vs, kw, vw)
    om = _mem_attention(z, ODD_OFF["qm"][0], ODD_OFF["gm"][0], mem, mem_g, w_mem_kv.astype(BF16), m_qn, m_kn)
    out = _outproj(h2, oc.reshape(b * s, 512), od.reshape(b * s, 512), om.reshape(b * s, 256), w_out.astype(BF16))
    return out.reshape(b, s, d)


def kernel(x, mem, norm_g, mem_norm_g, mem_w_kv, mem_qn, mem_kn, ev_w_in, ev_w_out, a_qn, a_kn, a_sinks,
           b_conv_w, b_conv_b, b_w_r, b_b_r, b_w_i, b_b_i, b_lambda, od_w_in, od_w_out, c_lb, c_onorm,
           d_qn, d_kn_cmp, d_kn_slc, d_kn_win, d_pe_k, d_pe_v, d_w1k, d_w2k, d_w1v, d_w2v):
    depth = norm_g.shape[0]
    assert depth == 2 and c_lb.shape[0] == 2, "the HGRN2 lower-bound formula in the kernel is written for depth 2"
    s = x.shape[1]
    assert s % 256 == 0 and s >= D_WINDOW
    pos = jnp.arange(s)
    cos, sin_s = _rope_tables(pos)
    nseg = s // CMP_STRIDE
    cmp_end = jnp.minimum(jnp.arange(nseg) * CMP_STRIDE + CMP_LEN - 1, s - 1)
    cosc, sinc = _rope_tables(cmp_end)
    h = _even_layer(x, mem, norm_g[0], mem_norm_g[0], mem_w_kv[0], mem_qn[0], mem_kn[0], ev_w_in[0], ev_w_out[0],
                    a_qn[0], a_kn[0], a_sinks[0], b_conv_w[0], b_conv_b[0], b_w_r[0], b_b_r[0], b_w_i[0], b_b_i[0],
                    b_lambda[0], cos, sin_s)
    h = _odd_layer(h, mem, norm_g[1], mem_norm_g[1], mem_w_kv[1], mem_qn[1], mem_kn[1], od_w_in[0], od_w_out[0],
                   c_lb, c_onorm[0], d_qn[0], d_kn_cmp[0], d_kn_slc[0], d_kn_win[0], d_pe_k[0], d_pe_v[0],
                   d_w1k[0], d_w2k[0], d_w1v[0], d_w2v[0], cos, sin_s, cosc, sinc)
    return h
```

```python
import functools

import numpy as np
import jax
import jax.numpy as jnp
from jax import lax
from jax.experimental import pallas as pl
from jax.experimental.pallas import tpu as pltpu

F32 = jnp.float32
BF16 = jnp.bfloat16

D_MODEL = 1024
N_MEM = 256
HEAD_DIM = 64
ROPE_THETA = 10000.0
EPS = 1e-6
NEG_INF = -1e30
POS_INF = 1e30
MASK_BIG = 1e30
LANES = 128

A_HEADS, A_KV_HEADS, A_WINDOW = 8, 2, 128
B_WIDTH, B_BLOCKS, B_CONV, B_C = 512, 8, 4, 8.0
M_HEADS = 4
C_HEADS, C_HEAD_DIM, C_CHUNK = 4, 128, 64
D_HEADS, D_KV_HEADS = 8, 2
CMP_LEN, CMP_STRIDE, CMP_HIDDEN = 32, 16, 128
SEL_LEN, SEL_TOPK = 64, 4
D_WINDOW = 512
D_BRANCHES = 3
SCALE = HEAD_DIM ** -0.5
LOG2E = 1.4426950408889634
Q_SCALE = SCALE * LOG2E

EVEN_ORDER = [("xb", 512), ("gb", 512), ("qa", 512), ("ga", 512), ("qm", 256), ("gm", 256), ("ka", 128), ("va", 128)]
EVEN_FUSED = 1024
EVEN_SRC = {"qa": (0, 512), "ka": (512, 128), "va": (640, 128), "ga": (768, 512), "xb": (1280, 512),
            "gb": (1792, 512), "qm": (2304, 256), "gm": (2560, 256)}
ODD_ORDER = [("qc", 512), ("fc", 512), ("ic", 512), ("gc", 512), ("qd", 512), ("gd", 512), ("qm", 256), ("gm", 256),
             ("kcd", 128), ("vcd", 128), ("ksd", 128), ("vsd", 128), ("kwd", 128), ("vwd", 128), ("gate", 128)]
ODD_SRC = {"qc": (0, 512), "fc": (512, 512), "ic": (1024, 512), "gc": (1536, 512), "qd": (2048, 512),
           "kcd": (2560, 128), "vcd": (2688, 128), "ksd": (2816, 128), "vsd": (2944, 128), "kwd": (3072, 128),
           "vwd": (3200, 128), "gate": (3328, 24), "gd": (3352, 512), "qm": (3864, 256), "gm": (4120, 256)}

VMEM_LIMIT = 48 * 1024 * 1024


def _offsets(order):
    off, out = 0, {}
    for name, w in order:
        out[name] = (off, w)
        off += w
    return out, off


EVEN_OFF, EVEN_COLS = _offsets(EVEN_ORDER[2:])
ODD_OFF, ODD_COLS = _offsets(ODD_ORDER)


def _cparams(sem):
    return pltpu.CompilerParams(dimension_semantics=sem, vmem_limit_bytes=VMEM_LIMIT)


def _dot(a, b):
    return jnp.dot(a, b, preferred_element_type=F32)


def _dot_nt(a, b):
    return lax.dot_general(a, b, (((1,), (1,)), ((), ())), preferred_element_type=F32)


def _split(x):
    hi = x.astype(BF16)
    lo = (x - hi.astype(F32)).astype(BF16)
    return hi, lo


def _dot_split_lhs(x, m):
    hi, lo = _split(x)
    return _dot(hi, m) + _dot(lo, m)


def _dot_split_rhs(m, x):
    hi, lo = _split(x)
    return _dot(m, hi) + _dot(m, lo)


def _lane(shape):
    return lax.broadcasted_iota(jnp.int32, shape, len(shape) - 1)


def _row(shape):
    return lax.broadcasted_iota(jnp.int32, shape, len(shape) - 2)


def _silu(x):
    return x * jax.nn.sigmoid(x)


def _seg_ones():
    r = lax.broadcasted_iota(jnp.int32, (LANES, LANES), 0) >> 6
    c = lax.broadcasted_iota(jnp.int32, (LANES, LANES), 1) >> 6
    return jnp.where(r == c, 1.0, 0.0).astype(BF16)


def _head_rms(x, gain):
    seg = _seg_ones()
    cols = []
    for c in range(x.shape[1] // LANES):
        xc = x[:, c * LANES:(c + 1) * LANES]
        ms = _dot_split_lhs(xc * xc, seg) * (1.0 / HEAD_DIM)
        cols.append(xc * lax.rsqrt(ms + EPS))
    y = cols[0] if len(cols) == 1 else jnp.concatenate(cols, axis=1)
    return y * gain


def _rope(x, cos, sin_s):
    first = (_lane((x.shape[0], LANES)) & 63) < 32
    cols = []
    for c in range(x.shape[1] // LANES):
        xc = x[:, c * LANES:(c + 1) * LANES]
        sw = jnp.where(first, pltpu.roll(xc, 96, 1), pltpu.roll(xc, 32, 1))
        cols.append(xc * cos + sw * sin_s)
    return cols[0] if len(cols) == 1 else jnp.concatenate(cols, axis=1)


def _split_kv(k, fill=0.0):
    lo = _lane(k.shape) < 64
    return jnp.where(lo, k, fill), jnp.where(lo, pltpu.roll(k, 64, 1), fill)


def _q_heads(q, fills=None):
    lo = _lane((q.shape[0], LANES)) < 64
    out = []
    for c in range(q.shape[1] // LANES):
        qc = q[:, c * LANES:(c + 1) * LANES]
        fill = 0.0 if fills is None else fills[c]
        out.append(jnp.where(lo, qc, fill).astype(BF16))
        out.append(jnp.where(lo, pltpu.roll(qc, 64, 1), fill).astype(BF16))
    return out


VT_ROWS = HEAD_DIM + 16


def _with_ones_row(vt):
    n = vt.shape[1]
    pad = jnp.where(_row((VT_ROWS - HEAD_DIM, n)) == 0, 1.0, 0.0)
    return jnp.concatenate([vt, pad], axis=0)


MAX_SAFE_BOUND = 60.0


def _score_bound(q_gain, k_gain):
    return 1.02 * HEAD_DIM * SCALE * LOG2E * jnp.max(jnp.abs(q_gain)) * jnp.max(jnp.abs(k_gain))


def _with_fixed_reference(bound, body):
    small = bound <= MAX_SAFE_BOUND

    @pl.when(small)
    def _():
        body(bound)

    @pl.when(jnp.logical_not(small))
    def _():
        body(None)


def _exp_cols(s, extra=None, fixed=None):
    if fixed is not None:
        return jnp.exp2(s - fixed).astype(BF16), fixed
    m = jnp.max(s, axis=0, keepdims=True)
    if extra is not None:
        m = jnp.maximum(m, extra)
    return jnp.exp2(s - m).astype(BF16), m


def _finish(acc, extra_den=None):
    den = acc[HEAD_DIM:HEAD_DIM + 1, :]
    if extra_den is not None:
        den = den + extra_den
    return acc[0:HEAD_DIM, :] * (1.0 / den)


def _online_update(state, s, vt_tile, fixed=None):
    m, acc = state
    if fixed is not None:
        return m, acc + _dot(vt_tile, _exp_cols(s, fixed=fixed)[0])
    p, m_new = _exp_cols(s, m)
    acc = jnp.exp2(m - m_new) * acc + _dot(vt_tile, p)
    return m_new, acc


def _tile4(x):
    return jnp.concatenate([x, x, x, x], axis=1)


def _inproj_kernel(x_ref, g_ref, w_ref, o_ref):
    x = x_ref[...]
    ms = jnp.mean(x * x, axis=-1, keepdims=True)
    xn = (x * lax.rsqrt(ms + EPS) * g_ref[...]).astype(BF16)
    n = o_ref.shape[-1]
    for c in range(0, n, 512):
        w = min(512, n - c)
        o_ref[:, c:c + w] = _dot(xn, w_ref[:, c:c + w]).astype(BF16)


def _inproj(x2d, gain, w_bf16, tm=512):
    n, d = x2d.shape
    nc = w_bf16.shape[1]
    return pl.pallas_call(
        _inproj_kernel,
        grid=(n // tm,),
        in_specs=[pl.BlockSpec((tm, d), lambda i: (i, 0)),
                  pl.BlockSpec((1, d), lambda i: (0, 0)),
                  pl.BlockSpec((d, nc), lambda i: (0, 0))],
        out_specs=pl.BlockSpec((tm, nc), lambda i: (i, 0)),
        out_shape=jax.ShapeDtypeStruct((n, nc), BF16),
        compiler_params=_cparams(("arbitrary",)),
        name="inproj",
    )(x2d, gain.reshape(1, d), w_bf16)


def _outproj_kernel(h_ref, a_ref, b_ref, m_ref, w_ref, o_ref):
    a, b, m = a_ref[...], b_ref[...], m_ref[...]
    for c in range(0, D_MODEL, 256):
        acc = h_ref[:, c:c + 256]
        acc += _dot(a, w_ref[0:512, c:c + 256])
        acc += _dot(b, w_ref[512:1024, c:c + 256])
        acc += _dot(m, w_ref[1024:1280, c:c + 256])
        o_ref[:, c:c + 256] = acc


def _outproj(h2d, oa, ob, om, w_bf16, tm=1024):
    n, d = h2d.shape
    return pl.pallas_call(
        _outproj_kernel,
        grid=(n // tm,),
        in_specs=[pl.BlockSpec((tm, d), lambda i: (i, 0)),
                  pl.BlockSpec((tm, 512), lambda i: (i, 0)),
                  pl.BlockSpec((tm, 512), lambda i: (i, 0)),
                  pl.BlockSpec((tm, 256), lambda i: (i, 0)),
                  pl.BlockSpec((1280, d), lambda i: (0, 0))],
        out_specs=pl.BlockSpec((tm, d), lambda i: (i, 0)),
        out_shape=jax.ShapeDtypeStruct((n, d), F32),
        compiler_params=_cparams(("arbitrary",)),
        name="outproj",
    )(h2d, oa, ob, om, w_bf16)


def _mem_kernel(bound_ref, q_ref, gm_ref, mem_ref, mg_ref, wkv_ref, qn_ref, kn_ref, o_ref, k_scr, vt_scr):
    @pl.when(pl.program_id(1) == 0)
    def _():
        m = mem_ref[0]
        ms = jnp.mean(m * m, axis=-1, keepdims=True)
        mn = (m * lax.rsqrt(ms + EPS) * mg_ref[...]).astype(BF16)
        kv = _dot(mn, wkv_ref[...])
        km = _head_rms(kv[:, 0:256], kn_ref[...])
        for c in range(2):
            k0, k1 = _split_kv(km[:, c * LANES:(c + 1) * LANES])
            k_scr[2 * c] = k0.astype(BF16)
            k_scr[2 * c + 1] = k1.astype(BF16)
        vt = kv[:, 256:512].T
        for h in range(M_HEADS):
            vt_scr[h] = _with_ones_row(vt[h * HEAD_DIM:(h + 1) * HEAD_DIM, :]).astype(BF16)

    def body(fixed):
        q = _head_rms(q_ref[0].astype(F32), qn_ref[...]) * Q_SCALE
        scores = [_dot_nt(k_scr[h], qh) for h, qh in enumerate(_q_heads(q))]
        rows = [_finish(_dot(vt_scr[h], _exp_cols(sc, fixed=fixed)[0])) for h, sc in enumerate(scores)]
        o = jnp.concatenate(rows, axis=0).T
        o_ref[0] = (o * _silu(gm_ref[0].astype(F32))).astype(BF16)

    _with_fixed_reference(bound_ref[0], body)


def _mem_attention(z, q_off, g_off, mem, mem_g, wkv_bf16, qn, kn, tq=512):
    b, s, _ = z.shape
    return pl.pallas_call(
        _mem_kernel,
        grid=(b, s // tq),
        in_specs=[pl.BlockSpec(memory_space=pltpu.SMEM),
                  pl.BlockSpec((1, tq, 256), lambda bi, i: (bi, i, q_off // 256)),
                  pl.BlockSpec((1, tq, 256), lambda bi, i: (bi, i, g_off // 256)),
                  pl.BlockSpec((1, N_MEM, D_MODEL), lambda bi, i: (bi, 0, 0)),
                  pl.BlockSpec((1, D_MODEL), lambda bi, i: (0, 0)),
                  pl.BlockSpec((D_MODEL, 512), lambda bi, i: (0, 0)),
                  pl.BlockSpec((1, 256), lambda bi, i: (0, 0)),
                  pl.BlockSpec((1, 256), lambda bi, i: (0, 0))],
        out_specs=pl.BlockSpec((1, tq, 256), lambda bi, i: (bi, i, 0)),
        out_shape=jax.ShapeDtypeStruct((b, s, 256), BF16),
        scratch_shapes=[pltpu.VMEM((M_HEADS, N_MEM, LANES), BF16), pltpu.VMEM((M_HEADS, VT_ROWS, N_MEM), BF16)],
        compiler_params=_cparams(("arbitrary", "arbitrary")),
        name="mem_attention",
    )(_score_bound(qn, kn).reshape(1).astype(F32), z, z, mem, mem_g.reshape(1, D_MODEL), wkv_bf16,
      jnp.tile(qn, 4).reshape(1, 256), jnp.tile(kn, 4).reshape(1, 256))


def _swa_kernel(bound_ref, *refs):
    _with_fixed_reference(bound_ref[0], functools.partial(_swa_body, refs))


def _swa_body(refs, fixed):
    (q_ref, g_ref, kc_ref, kp_ref, vc_ref, vp_ref, cosc_ref, sinc_ref, cosp_ref, sinp_ref,
     qn_ref, kn_ref, sink_ref, o_ref) = refs
    i = pl.program_id(1)
    t = q_ref.shape[1]
    tp = kp_ref.shape[1]
    grp = A_HEADS // A_KV_HEADS
    q = _rope(_head_rms(q_ref[0].astype(F32), qn_ref[...]), cosc_ref[...], sinc_ref[...]) * Q_SCALE
    qh = _q_heads(q)
    kc = _rope(_head_rms(kc_ref[0].astype(F32), kn_ref[...]), cosc_ref[...], sinc_ref[...])
    kp = _rope(_head_rms(kp_ref[0].astype(F32), kn_ref[...]), cosp_ref[...], sinp_ref[...])
    ks = [x.astype(BF16) for x in _split_kv(jnp.concatenate([kp, kc], axis=0))]
    vt = jnp.concatenate([vp_ref[0].astype(F32).T, vc_ref[0].astype(F32).T], axis=1)
    ts = i * t - tp + _row((tp + t, t))
    rel = i * t + _lane((tp + t, t)) - ts
    bias = _tile4(jnp.where((rel >= 0) & (rel < A_WINDOW) & (ts >= 0), 0.0, NEG_INF))
    scores = [_dot_nt(ks[kv], jnp.concatenate(qh[kv * grp:(kv + 1) * grp], axis=0)) for kv in range(A_KV_HEADS)]
    rows = []
    for kv in range(A_KV_HEADS):
        sink = jnp.concatenate([jnp.broadcast_to(sink_ref[kv * grp + g:kv * grp + g + 1, 0:1], (1, t))
                                for g in range(grp)], axis=1) * LOG2E
        p, m = _exp_cols(scores[kv] + bias, sink, fixed)
        vt_kv = _with_ones_row(vt[kv * HEAD_DIM:(kv + 1) * HEAD_DIM, :]).astype(BF16)
        o_t = _finish(_dot(vt_kv, p), jnp.exp2(sink - m))
        rows.extend(o_t[:, g * t:(g + 1) * t] for g in range(grp))
    o = jnp.concatenate(rows, axis=0).T
    o_ref[0] = (o * _silu(g_ref[0].astype(F32))).astype(BF16)


def _swa(z, cos, sin_s, qn, kn, sinks, t=256):
    b, s, _ = z.shape
    tp = A_WINDOW
    qo, go = EVEN_OFF["qa"][0] // 512, EVEN_OFF["ga"][0] // 512
    ko, vo = EVEN_OFF["ka"][0] // 128, EVEN_OFF["va"][0] // 128
    prev = lambda i: jnp.maximum(i * (t // tp) - 1, 0)
    bound = jnp.maximum(_score_bound(qn, kn), jnp.max(sinks) * LOG2E).reshape(1).astype(F32)
    return pl.pallas_call(
        _swa_kernel,
        grid=(b, s // t),
        in_specs=[pl.BlockSpec(memory_space=pltpu.SMEM),
                  pl.BlockSpec((1, t, 512), lambda bi, i: (bi, i, qo)),
                  pl.BlockSpec((1, t, 512), lambda bi, i: (bi, i, go)),
                  pl.BlockSpec((1, t, 128), lambda bi, i: (bi, i, ko)),
                  pl.BlockSpec((1, tp, 128), lambda bi, i: (bi, prev(i), ko)),
                  pl.BlockSpec((1, t, 128), lambda bi, i: (bi, i, vo)),
                  pl.BlockSpec((1, tp, 128), lambda bi, i: (bi, prev(i), vo)),
                  pl.BlockSpec((t, 128), lambda bi, i: (i, 0)),
                  pl.BlockSpec((t, 128), lambda bi, i: (i, 0)),
                  pl.BlockSpec((tp, 128), lambda bi, i: (prev(i), 0)),
                  pl.BlockSpec((tp, 128), lambda bi, i: (prev(i), 0)),
                  pl.BlockSpec((1, 512), lambda bi, i: (0, 0)),
                  pl.BlockSpec((1, 128), lambda bi, i: (0, 0)),
                  pl.BlockSpec((A_HEADS, 128), lambda bi, i: (0, 0))],
        out_specs=pl.BlockSpec((1, t, 512), lambda bi, i: (bi, i, 0)),
        out_shape=jax.ShapeDtypeStruct((b, s, 512), BF16),
        compiler_params=_cparams(("arbitrary", "arbitrary")),
        name="swa",
    )(bound, z, z, z, z, z, z, cos, sin_s, cos, sin_s, jnp.tile(qn, 8).reshape(1, 512), jnp.tile(kn, 2).reshape(1, 128),
      jnp.broadcast_to(sinks.reshape(A_HEADS, 1), (A_HEADS, 128)))


def _rglru_tile(first, xb, gb, cw_ref, cb_ref, wr_ref, br_ref, wi_ref, bi_ref, lam_ref, xbuf, hcar, a_scr, u_scr, h_scr,
                between):
    t = xb.shape[0]

    @pl.when(first)
    def _():
        xbuf[0:8, :] = jnp.zeros((8, B_WIDTH), F32)
        hcar[...] = jnp.zeros((8, B_WIDTH), F32)

    xbuf[8:t + 8, :] = xb
    xc = cb_ref[...] + cw_ref[0:1, :] * xbuf[5:t + 5, :]
    for j in range(1, B_CONV):
        xc = xc + cw_ref[j:j + 1, :] * xbuf[5 + j:t + 5 + j, :]
    xbuf[0:8, :] = xbuf[t:t + 8, :]

    r_cols, i_cols = [], []
    for c in range(B_WIDTH // LANES):
        xcc = xc[:, c * LANES:(c + 1) * LANES].astype(BF16)
        r_cols.append(_dot(xcc, wr_ref[c]))
        i_cols.append(_dot(xcc, wi_ref[c]))
    between()
    r = jax.nn.sigmoid(jnp.concatenate(r_cols, axis=1) + br_ref[...])
    ig = jax.nn.sigmoid(jnp.concatenate(i_cols, axis=1) + bi_ref[...])
    nl = -lam_ref[...]
    softplus = jnp.maximum(nl, 0.0) + jnp.log(1.0 + jnp.exp(-jnp.abs(nl)))
    log_a = -B_C * r * softplus
    a = jnp.exp(log_a)
    om = 1.0 - a * a
    u = om * lax.rsqrt(jnp.maximum(om, 1e-30)) * (ig * xc)

    a = a.reshape(t // 8, 8, B_WIDTH)
    u = u.reshape(t // 8, 8, B_WIDTH)
    r8 = lax.broadcasted_iota(jnp.int32, a.shape, 1)
    for d in (1, 2, 4):
        a_sh = pltpu.roll(a, d, 1)
        u_sh = pltpu.roll(u, d, 1)
        m = r8 >= d
        u = jnp.where(m, a * u_sh + u, u)
        a = jnp.where(m, a * a_sh, a)
    a_scr[...] = a.reshape(t, B_WIDTH)
    u_scr[...] = u.reshape(t, B_WIDTH)

    def body(j, h):
        off = pl.multiple_of(j * 8, 8)
        hh = a_scr[pl.ds(off, 8), :] * h + u_scr[pl.ds(off, 8), :]
        h_scr[pl.ds(off, 8), :] = hh
        return hh[7:8, :]

    h_last = lax.fori_loop(0, t // 8, body, hcar[0:1, :], unroll=True)
    hcar[0:1, :] = h_last
    return h_scr[...] * _silu(gb)


def _inproj_rglru_kernel(tiles_per_seq, x_ref, g_ref, w_ref, cw_ref, cb_ref, wr_ref, br_ref, wi_ref, bi_ref, lam_ref,
                         z_ref, ob_ref, xbuf, hcar, a_scr, u_scr, h_scr):
    x = x_ref[...]
    ms = jnp.mean(x * x, axis=-1, keepdims=True)
    xn = (x * lax.rsqrt(ms + EPS) * g_ref[...]).astype(BF16)
    xb = _dot(xn, w_ref[:, 0:B_WIDTH])
    gb = _dot(xn, w_ref[:, B_WIDTH:2 * B_WIDTH])
    first = pl.program_id(0) % tiles_per_seq == 0

    def rest_of_projection():
        n = z_ref.shape[-1]
        for c in range(0, n, 512):
            w = min(512, n - c)
            z_ref[:, c:c + w] = _dot(xn, w_ref[:, EVEN_FUSED + c:EVEN_FUSED + c + w]).astype(BF16)

    ob_ref[...] = _rglru_tile(first, xb, gb, cw_ref, cb_ref, wr_ref, br_ref, wi_ref, bi_ref, lam_ref,
                              xbuf, hcar, a_scr, u_scr, h_scr, rest_of_projection).astype(BF16)


def _inproj_rglru(x2d, gain, w_bf16, seq, conv_w, conv_b, wr_bd, b_r, wi_bd, b_i, lam, tm=512):
    n, d = x2d.shape
    nc = w_bf16.shape[1]
    row = lambda v: v.reshape(1, B_WIDTH)
    full = lambda shp: pl.BlockSpec(shp, lambda i: (0,) * len(shp))
    return pl.pallas_call(
        functools.partial(_inproj_rglru_kernel, seq // tm),
        grid=(n // tm,),
        in_specs=[pl.BlockSpec((tm, d), lambda i: (i, 0)), full((1, d)), full((d, nc)),
                  full((B_CONV, B_WIDTH)), full((1, B_WIDTH)), full((4, LANES, LANES)), full((1, B_WIDTH)),
                  full((4, LANES, LANES)), full((1, B_WIDTH)), full((1, B_WIDTH))],
        out_specs=[pl.BlockSpec((tm, nc - EVEN_FUSED), lambda i: (i, 0)), pl.BlockSpec((tm, B_WIDTH), lambda i: (i, 0))],
        out_shape=[jax.ShapeDtypeStruct((n, nc - EVEN_FUSED), BF16), jax.ShapeDtypeStruct((n, B_WIDTH), BF16)],
        scratch_shapes=[pltpu.VMEM((tm + 8, B_WIDTH), F32), pltpu.VMEM((8, B_WIDTH), F32),
                        pltpu.VMEM((tm, B_WIDTH), F32), pltpu.VMEM((tm, B_WIDTH), F32), pltpu.VMEM((tm, B_WIDTH), F32)],
        compiler_params=_cparams(("arbitrary",)),
        name="inproj_rglru",
    )(x2d, gain.reshape(1, d), w_bf16, conv_w, row(conv_b), wr_bd, row(b_r), wi_bd, row(b_i), row(lam))


def _hgrn_consts():
    c = C_CHUNK
    t = np.arange(c)[:, None]
    s = np.arange(c)[None, :]
    mats = [(s <= t)]
    masks = []
    hs = c // 2
    while hs >= 1:
        mid = (t // (2 * hs)) * 2 * hs + hs - 1
        mats.append(s <= mid)
        same = (t // (2 * hs)) == (s // (2 * hs))
        masks.append(same & ((t // hs) % 2 == 1) & ((s // hs) % 2 == 0))
        hs //= 2
    pairs = [np.concatenate(masks[n:n + 2], axis=1) for n in range(0, len(masks), 2)]
    return np.concatenate(mats, axis=0).astype(np.float32), np.stack(pairs).astype(np.float32)


def _hgrn_kernel(q_ref, f_ref, i_ref, g_ref, lb_ref, og_ref, mst_ref, msk_ref, o_ref, st_scr):
    @pl.when(pl.program_id(1) == 0)
    def _():
        st_scr[...] = jnp.zeros(st_scr.shape, F32)

    c = C_CHUNK
    npair = msk_ref.shape[0]
    p = lb_ref[...]
    pm = jnp.maximum(p[0:1, :], p[1:2, :])
    e0, e1 = jnp.exp(p[0:1, :] - pm), jnp.exp(p[1:2, :] - pm)
    lb = e1 / (e0 + e1)
    zeros = jnp.zeros((c, LANES), BF16)
    for ch in range(q_ref.shape[1] // c):
        rows = slice(ch * c, (ch + 1) * c)
        f = lb + (1.0 - lb) * jax.nn.sigmoid(f_ref[0, rows, :].astype(F32))
        g = jnp.log2(f)
        g_hi, g_lo = _split(g)
        b_all = _dot(mst_ref[0:c, :], g_hi) + _dot(mst_ref[0:c, :], g_lo)
        bm_all = _dot(mst_ref[c:, :], g_hi)
        kk_all = 1.0 - f
        qf_all = _silu(q_ref[0, rows, :].astype(F32))
        v_all = i_ref[0, rows, :].astype(F32)
        kk_bf, qf_bf = kk_all.astype(BF16), qf_all.astype(BF16)
        heads = range(C_HEADS)
        cols = [slice(h * LANES, (h + 1) * LANES) for h in heads]
        sts = [st_scr[h] for h in heads]
        os = [_dot_nt((qf_all[:, cols[h]] * jnp.exp2(b_all[:, cols[h]])).astype(BF16), sts[h].astype(BF16)) for h in heads]
        att = [[] for _ in heads]
        for n in range(npair):
            lhs, rhs = [], []
            for h in heads:
                b = b_all[:, cols[h]]
                qt, kt = [], []
                for l in (2 * n, 2 * n + 1):
                    bm = bm_all[l * c:(l + 1) * c, cols[h]]
                    qt.append(qf_bf[:, cols[h]] * jnp.exp2(jnp.minimum(b - bm, 1.0)).astype(BF16))
                    kt.append(kk_bf[:, cols[h]] * jnp.exp2(jnp.minimum(bm - b, 1.0)).astype(BF16))
                lhs.append(jnp.concatenate(qt, axis=1))
                rhs.append(jnp.concatenate([jnp.concatenate([kt[0], zeros], axis=1),
                                            jnp.concatenate([zeros, kt[1]], axis=1)], axis=0))
            prods = [_dot_nt(lhs[h], rhs[h]) for h in heads]
            for h in heads:
                att[h].append(jnp.where(msk_ref[n] > 0.5, prods[h], 0.0).astype(BF16))
        vbs = [v_all[:, cols[h]].astype(BF16) for h in heads]
        intra = [_dot(jnp.concatenate(att[h], axis=1), jnp.concatenate([vbs[h]] * (2 * npair), axis=0)) for h in heads]
        b_last = b_all[c - 1:c, :]
        kd = (kk_all * jnp.exp2(b_last - b_all)).astype(BF16)
        upd = [_dot(v_all[:, cols[h]].T.astype(BF16), kd[:, cols[h]]) for h in heads]
        decay = jnp.exp2(b_last)
        diag = qf_all * kk_all
        outs = []
        for h in heads:
            st_scr[h] = sts[h] * decay[:, cols[h]] + upd[h]
            o = os[h] + intra[h] + jnp.sum(diag[:, cols[h]], axis=-1, keepdims=True) * v_all[:, cols[h]]
            ms = jnp.mean(o * o, axis=-1, keepdims=True)
            outs.append(o * lax.rsqrt(ms + EPS) * og_ref[...])
        o_ref[0, rows, :] = (jnp.concatenate(outs, axis=1) * _silu(g_ref[0, rows, :].astype(F32))).astype(BF16)


def _hgrn(z, c_lb, c_og, t=512):
    b, s, _ = z.shape
    mst, msk = _hgrn_consts()
    blk = lambda name: pl.BlockSpec((1, t, 512), lambda bi, i, o=ODD_OFF[name][0] // 512: (bi, i, o))
    full = lambda shp: pl.BlockSpec(shp, lambda bi, i: (0,) * len(shp))
    return pl.pallas_call(
        _hgrn_kernel,
        grid=(b, s // t),
        in_specs=[blk("qc"), blk("fc"), blk("ic"), blk("gc"), full(c_lb.shape), full((1, C_HEAD_DIM)),
                  full(mst.shape), full(msk.shape)],
        out_specs=pl.BlockSpec((1, t, 512), lambda bi, i: (bi, i, 0)),
        out_shape=jax.ShapeDtypeStruct((b, s, 512), BF16),
        scratch_shapes=[pltpu.VMEM((C_HEADS, C_HEAD_DIM, C_HEAD_DIM), F32)],
        compiler_params=_cparams(("arbitrary", "arbitrary")),
        name="hgrn2",
    )(z, z, z, z, c_lb, c_og.reshape(1, C_HEAD_DIM), jnp.asarray(mst, BF16), jnp.asarray(msk, F32))


def _nsa_prep_kernel(kcr_ref, vcr_ref, ks_ref, vs_ref, kw_ref, vw_ref, cos_ref, sin_ref, cosc_ref, sinc_ref,
                     kncmp_ref, knslc_ref, knwin_ref, pek_ref, pev_ref, w1k_ref, w2k_ref, w1v_ref, w2vt_ref,
                     kc_o, vct_o, ks_o, vst_o, kw_o, vwt_o):
    def hidden(xr_ref, pe_ref, w1_ref):
        xr = xr_ref[0].astype(F32)
        top = _dot((xr + pe_ref[0:1, :]).astype(BF16), w1_ref[0])
        bot = _dot((xr + pe_ref[1:2, :]).astype(BF16), w1_ref[1])
        pre = top + pltpu.roll(bot, bot.shape[0] - 1, 0)
        return _silu(pre).astype(BF16)

    kc = _dot(hidden(kcr_ref, pek_ref, w1k_ref), w2k_ref[...])
    kc = _rope(_head_rms(kc, kncmp_ref[...]), cosc_ref[...], sinc_ref[...])
    d0, d1 = _split_kv(kc)
    kc_o[0, 0] = d0.astype(BF16)
    kc_o[0, 1] = d1.astype(BF16)
    vct = _dot_nt(w2vt_ref[...], hidden(vcr_ref, pev_ref, w1v_ref))
    for kv in range(D_KV_HEADS):
        vct_o[0, kv] = _with_ones_row(vct[kv * HEAD_DIM:(kv + 1) * HEAD_DIM, :]).astype(BF16)

    s = ks_ref.shape[1]
    step = 256
    for r0 in range(0, s, step):
        rows = slice(r0, r0 + step)
        cos, sin = cos_ref[rows, :], sin_ref[rows, :]
        ks = _rope(_head_rms(ks_ref[0, rows, :].astype(F32), knslc_ref[...]), cos, sin)
        kw = _rope(_head_rms(kw_ref[0, rows, :].astype(F32), knwin_ref[...]), cos, sin)
        blk = (r0 + _row((step, LANES))) // SEL_LEN
        onehot = jnp.where(_lane((step, LANES)) - HEAD_DIM == blk, MASK_BIG, 0.0)
        for src, dst, fill in ((ks, ks_o, onehot), (kw, kw_o, 0.0)):
            d0, d1 = _split_kv(src, fill)
            dst[0, 0, rows, :] = d0.astype(BF16)
            dst[0, 1, rows, :] = d1.astype(BF16)
        for src, dst in ((vs_ref, vst_o), (vw_ref, vwt_o)):
            vt = src[0, rows, :].astype(F32).T
            for kv in range(D_KV_HEADS):
                dst[0, kv, :, rows] = _with_ones_row(vt[kv * HEAD_DIM:(kv + 1) * HEAD_DIM, :]).astype(BF16)


def _nsa_prep(z, kcr, vcr, cos, sin_s, cosc, sinc, kn_cmp, kn_slc, kn_win, pek, pev, w1k, w2k, w1v, w2v):
    b, s, _ = z.shape
    nseg = s // CMP_STRIDE
    zb = lambda name: pl.BlockSpec((1, s, 128), lambda bi, o=ODD_OFF[name][0] // 128: (bi, 0, o))
    full = lambda shp: pl.BlockSpec(shp, lambda bi: (0,) * len(shp))
    seg = pl.BlockSpec((1, nseg, 2048), lambda bi: (bi, 0, 0))
    g2 = lambda g: jnp.tile(g, 2).reshape(1, 128)
    dup = lambda n: (jax.ShapeDtypeStruct((b, 2, n, 128), BF16), pl.BlockSpec((1, 2, n, 128), lambda bi: (bi, 0, 0, 0)))
    tr = lambda n: (jax.ShapeDtypeStruct((b, 2, VT_ROWS, n), BF16),
                    pl.BlockSpec((1, 2, VT_ROWS, n), lambda bi: (bi, 0, 0, 0)))
    outs = [dup(nseg), tr(nseg), dup(s), tr(s), dup(s), tr(s)]
    return pl.pallas_call(
        _nsa_prep_kernel,
        grid=(b,),
        in_specs=[seg, seg, zb("ksd"), zb("vsd"), zb("kwd"), zb("vwd"),
                  full((s, 128)), full((s, 128)), full((nseg, 128)), full((nseg, 128)),
                  full((1, 128)), full((1, 128)), full((1, 128)), full((2, 2048)), full((2, 2048)),
                  full((2, 2048, 256)), full((256, 128)), full((2, 2048, 256)), full((128, 256))],
        out_specs=[o[1] for o in outs],
        out_shape=[o[0] for o in outs],
        compiler_params=_cparams(("arbitrary",)),
        name="nsa_prep",
    )(kcr, vcr, z, z, z, z, cos, sin_s, cosc, sinc, g2(kn_cmp), g2(kn_slc), g2(kn_win), pek, pev, w1k, w2k, w1v, w2v)


SEL_CHUNK = 512
SEL_PIECE = 512


def _nsa_kernel(bound_ref, *refs):
    small = jnp.maximum(jnp.maximum(bound_ref[0], bound_ref[1]), bound_ref[2]) <= MAX_SAFE_BOUND

    @pl.when(small)
    def _():
        _nsa_body((bound_ref[0], bound_ref[1], bound_ref[2]), *refs)

    @pl.when(jnp.logical_not(small))
    def _():
        _nsa_body((None, None, None), *refs)


def _nsa_body(fixed, q_ref, gd_ref, gate_ref, cos_ref, sin_ref, qn_ref, kc_ref, vct_ref, ks_ref, vst_ref, kw_ref, vwt_ref,
              ovlt_ref, o_ref):
    fix_c, fix_s, fix_w = fixed
    i = pl.program_id(1)
    t = q_ref.shape[1]
    grp = D_HEADS // D_KV_HEADS
    w = grp * t
    kvs = range(D_KV_HEADS)
    q = _rope(_head_rms(q_ref[0].astype(F32), qn_ref[...]), cos_ref[...], sin_ref[...]) * Q_SCALE
    stack = lambda heads: [jnp.concatenate(heads[kv * grp:(kv + 1) * grp], axis=0) for kv in kvs]
    qs = stack(_q_heads(q))
    tq1 = i * t + _lane((1, t))

    nwin = D_WINDOW + t
    win0 = pl.multiple_of(jnp.maximum(i * t + t - nwin, 0), t)
    s_cmp = [_dot_nt(kc_ref[0, kv], qs[kv]) for kv in kvs]
    s_win = [_dot_nt(kw_ref[0, kv, pl.ds(win0, nwin), :], qs[kv]) for kv in kvs]

    ncmp = kc_ref.shape[2] - 1
    nsel = ovlt_ref.shape[0]
    crow = _row((kc_ref.shape[2], t))
    bias_c = _tile4(jnp.where((crow * CMP_STRIDE + (CMP_LEN - 1) <= tq1) & (crow < ncmp), 0.0, NEG_INF))
    row_ok = _tile4(tq1 >= CMP_LEN - 1)
    es = [_exp_cols(s_cmp[kv] + bias_c, fixed=fix_c)[0] for kv in kvs]
    accs = [_dot(vct_ref[0, kv], es[kv]) for kv in kvs]
    invs = [jnp.where(row_ok, 1.0 / accs[kv][HEAD_DIM:HEAD_DIM + 1, :], 0.0) for kv in kvs]
    oc_t = [accs[kv][0:HEAD_DIM, :] * invs[kv] for kv in kvs]
    psums = []
    for kv in kvs:
        p = es[kv].astype(F32) * invs[kv]
        psums.append(p[:, 0:t] + p[:, t:2 * t] + p[:, 2 * t:3 * t] + p[:, 3 * t:4 * t])
    imps = [_dot_split_rhs(ovlt_ref[...], psums[kv]) for kv in kvs]

    rel = tq1 - (win0 + _row((nwin, t)))
    bias_w = _tile4(jnp.where((rel >= 0) & (rel < D_WINDOW), 0.0, NEG_INF))
    ow_t = [_finish(_dot(vwt_ref[0, kv, :, pl.ds(win0, nwin)], _exp_cols(s_win[kv] + bias_w, fixed=fix_w)[0]))
            for kv in kvs]

    jrow = _row((nsel, t))
    jrow_f = jrow.astype(F32)
    cur = (i * t + _lane((nsel, t))) // SEL_LEN
    forced = (jrow == 0) | (jrow == cur)
    fills = []
    for kv in kvs:
        score = jnp.where(forced, POS_INF, jnp.where(jrow <= cur, imps[kv], NEG_INF))
        chosen = jnp.zeros((nsel, t), F32)
        for _ in range(min(SEL_TOPK, nsel)):
            mx = jnp.max(score, axis=0, keepdims=True)
            first = jnp.min(jnp.where(score == mx, jrow_f, 1e9), axis=0, keepdims=True)
            hit = jrow_f == first
            chosen = jnp.where(hit, 1.0, chosen)
            score = jnp.where(hit, -3e38, score)
        frame = jnp.concatenate([jnp.zeros((HEAD_DIM, t), F32), chosen - 1.0,
                                 jnp.zeros((LANES - HEAD_DIM - nsel, t), F32)], axis=0)
        fills.extend([frame.T] * (grp // 2))

    init = (jnp.full((1, w), NEG_INF, F32), jnp.zeros((VT_ROWS, w), F32))
    pc = SEL_PIECE

    def sweep(states, k_ref, vt_ref, queries, offs, biases, n):
        scores = [[_dot_nt(k_ref[0, kv, pl.ds(off, n), :], queries[kv]) for kv in kvs] for off in offs]
        for off, sc, bias in zip(offs, scores, biases):
            states = tuple(_online_update(states[kv], sc[kv] if bias is None else sc[kv] + bias,
                                          vt_ref[0, kv, :, pl.ds(off, n)], fix_s) for kv in kvs)
        return states

    qsel = stack(_q_heads(q, fills))
    ch = SEL_CHUNK
    own = pl.multiple_of(((i * t) // ch) * ch, ch)
    st = sweep((init, init), ks_ref, vst_ref, qsel, [own],
               [_tile4(jnp.where(own + _row((ch, t)) <= tq1, 0.0, NEG_INF))], ch)

    def sel_chunk(c, states):
        base = pl.multiple_of(c * ch, ch)
        return sweep(states, ks_ref, vst_ref, qsel, [base + n * pc for n in range(ch // pc)], [None] * (ch // pc), pc)

    st = lax.fori_loop(0, (i * t) // ch, sel_chunk, st)
    os_t = [_finish(acc) for (_, acc) in st]

    g_t = jax.nn.sigmoid(gate_ref[0].astype(F32)).T
    rows = []
    for h in range(D_HEADS):
        kv, g = divmod(h, grp)
        cols = slice(g * t, (g + 1) * t)
        r = D_BRANCHES * h
        rows.append(g_t[r:r + 1, :] * oc_t[kv][:, cols] + g_t[r + 1:r + 2, :] * os_t[kv][:, cols]
                    + g_t[r + 2:r + 3, :] * ow_t[kv][:, cols])
    o = jnp.concatenate(rows, axis=0).T
    o_ref[0] = (o * _silu(gd_ref[0].astype(F32))).astype(BF16)


def _overlap_t(s):
    ncmp = (s - CMP_LEN) // CMP_STRIDE + 1
    nsel = s // SEL_LEN
    cs = np.arange(ncmp)[None, :] * CMP_STRIDE
    ss = np.arange(nsel)[:, None] * SEL_LEN
    ovl = np.zeros((nsel, s // CMP_STRIDE), np.float32)
    ovl[:, :ncmp] = (cs < ss + SEL_LEN) & (cs + CMP_LEN > ss)
    return ovl


def _nsa(z, cos, sin_s, qn, bounds, kc, vct, ks, vst, kw, vwt, t=256):
    b, s, _ = z.shape
    nseg = s // CMP_STRIDE
    assert nseg == LANES, "compressed-block scores are laid out on one 128-row tile"
    ovlt = _overlap_t(s)
    full = lambda shp: pl.BlockSpec(shp, lambda bi, i: (0,) * len(shp))
    dup = lambda n: pl.BlockSpec((1, 2, n, 128), lambda bi, i: (bi, 0, 0, 0))
    tr = lambda n: pl.BlockSpec((1, 2, VT_ROWS, n), lambda bi, i: (bi, 0, 0, 0))
    return pl.pallas_call(
        _nsa_kernel,
        grid=(b, s // t),
        in_specs=[pl.BlockSpec(memory_space=pltpu.SMEM),
                  pl.BlockSpec((1, t, 512), lambda bi, i: (bi, i, ODD_OFF["qd"][0] // 512)),
                  pl.BlockSpec((1, t, 512), lambda bi, i: (bi, i, ODD_OFF["gd"][0] // 512)),
                  pl.BlockSpec((1, t, 128), lambda bi, i: (bi, i, ODD_OFF["gate"][0] // 128)),
                  pl.BlockSpec((t, 128), lambda bi, i: (i, 0)),
                  pl.BlockSpec((t, 128), lambda bi, i: (i, 0)),
                  full((1, 512)), dup(nseg), tr(nseg), dup(s), tr(s), dup(s), tr(s), full(ovlt.shape)],
        out_specs=pl.BlockSpec((1, t, 512), lambda bi, i: (bi, i, 0)),
        out_shape=jax.ShapeDtypeStruct((b, s, 512), BF16),
        compiler_params=_cparams(("arbitrary", "arbitrary")),
        name="nsa",
    )(bounds, z, z, z, cos, sin_s, jnp.tile(qn, 8).reshape(1, 512), kc, vct, ks, vst, kw, vwt, jnp.asarray(ovlt, BF16))


def _permute_cols(w, order, src):
    cols = []
    for name, width in order:
        o, sw = src[name]
        blk = w[:, o:o + sw]
        if sw < width:
            blk = jnp.pad(blk, ((0, 0), (0, width - sw)))
        cols.append(blk)
    return jnp.concatenate(cols, axis=1).astype(BF16)


def _block_diag_pairs(w):
    z = jnp.zeros((4, LANES, LANES), w.dtype)
    z = z.at[:, 0:64, 0:64].set(w[0::2])
    z = z.at[:, 64:128, 64:128].set(w[1::2])
    return z.astype(BF16)


def _rope_tables(pos):
    half = HEAD_DIM // 2
    inv = ROPE_THETA ** (-jnp.arange(half, dtype=F32) / half)
    ang = pos.astype(F32)[:, None] * inv[None, :]
    cos, sin = jnp.cos(ang), jnp.sin(ang)
    cos_t = jnp.tile(cos, (1, 4))
    sin_t = jnp.tile(jnp.concatenate([-sin, sin], axis=1), (1, 2))
    return cos_t, sin_t


def _expand_compress_w1(w1):
    hdim = w1.shape[1]
    w = w1.reshape(2, CMP_STRIDE, HEAD_DIM, hdim)
    z = jnp.zeros((2, CMP_STRIDE, 2, HEAD_DIM, 2, hdim), w1.dtype)
    z = z.at[:, :, 0, :, 0, :].set(w)
    z = z.at[:, :, 1, :, 1, :].set(w)
    return z.reshape(2, CMP_STRIDE * 2 * HEAD_DIM, 2 * hdim).astype(BF16)


def _expand_compress_w2(w2):
    hdim, hd = w2.shape
    z = jnp.zeros((2 * hdim, 2 * hd), w2.dtype)
    z = z.at[0:hdim, 0:hd].set(w2)
    z = z.at[hdim:, hd:].set(w2)
    return z.astype(BF16)


def _expand_pe(pe):
    p = pe.reshape(2, CMP_STRIDE, 1, HEAD_DIM)
    return jnp.broadcast_to(p, (2, CMP_STRIDE, 2, HEAD_DIM)).reshape(2, CMP_STRIDE * 2 * HEAD_DIM)


def _even_layer(h, mem, g, mem_g, w_mem_kv, m_qn, m_kn, w_in, w_out, a_qn, a_kn, a_sinks,
                conv_w, conv_b, w_r, b_r, w_i, b_i, lam, cos, sin_s):
    b, s, d = h.shape
    h2 = h.reshape(b * s, d)
    z, ob = _inproj_rglru(h2, g, _permute_cols(w_in, EVEN_ORDER, EVEN_SRC), s, conv_w, conv_b,
                          _block_diag_pairs(w_r), b_r, _block_diag_pairs(w_i), b_i, lam)
    z = z.reshape(b, s, EVEN_COLS)
    oa = _swa(z, cos, sin_s, a_qn, a_kn, a_sinks)
    om = _mem_attention(z, EVEN_OFF["qm"][0], EVEN_OFF["gm"][0], mem, mem_g, w_mem_kv.astype(BF16), m_qn, m_kn)
    out = _outproj(h2, oa.reshape(b * s, 512), ob, om.reshape(b * s, 256), w_out.astype(BF16))
    return out.reshape(b, s, d)


def _odd_layer(h, mem, g, mem_g, w_mem_kv, m_qn, m_kn, w_in, w_out, c_lb, c_og,
               d_qn, d_kn_cmp, d_kn_slc, d_kn_win, pe_k, pe_v, w1k, w2k, w1v, w2v, cos, sin_s, cosc, sinc):
    b, s, d = h.shape
    h2 = h.reshape(b * s, d)
    z = _inproj(h2, g, _permute_cols(w_in, ODD_ORDER, ODD_SRC)).reshape(b, s, ODD_COLS)
    oc = _hgrn(z, c_lb, c_og)
    nseg = s // CMP_STRIDE
    seg = lambda name: z[:, :, ODD_OFF[name][0]:ODD_OFF[name][0] + 128].reshape(b, nseg, CMP_STRIDE * 128)
    kc, vc, ks, vs, kw, vw = _nsa_prep(
        z, seg("kcd"), seg("vcd"), cos, sin_s, cosc, sinc, d_kn_cmp, d_kn_slc, d_kn_win,
        _expand_pe(pe_k), _expand_pe(pe_v), _expand_compress_w1(w1k), _expand_compress_w2(w2k),
        _expand_compress_w1(w1v), _expand_compress_w2(w2v).T)
    bounds = jnp.stack([_score_bound(d_qn, kn) for kn in (d_kn_cmp, d_kn_slc, d_kn_win)]).astype(F32)
    od = _nsa(z, cos, sin_s, d_qn, bounds, kc, vc, ks, vs, kw, vw)
    om = _mem_attention(z, ODD_OFF["qm"][0], ODD_OFF["gm"][0], mem, mem_g, w_mem_kv.astype(BF16), m_qn, m_kn)
    out = _outproj(h2, oc.reshape(b * s, 512), od.reshape(b * s, 512), om.reshape(b * s, 256), w_out.astype(BF16))
    return out.reshape(b, s, d)


def kernel(x, mem, norm_g, mem_norm_g, mem_w_kv, mem_qn, mem_kn, ev_w_in, ev_w_out, a_qn, a_kn, a_sinks,
           b_conv_w, b_conv_b, b_w_r, b_b_r, b_w_i, b_b_i, b_lambda, od_w_in, od_w_out, c_lb, c_onorm,
           d_qn, d_kn_cmp, d_kn_slc, d_kn_win, d_pe_k, d_pe_v, d_w1k, d_w2k, d_w1v, d_w2v):
    depth = norm_g.shape[0]
    assert depth == 2 and c_lb.shape[0] == 2, "the HGRN2 lower-bound formula in the kernel is written for depth 2"
    s = x.shape[1]
    assert s % 256 == 0 and s >= D_WINDOW
    pos = jnp.arange(s)
    cos, sin_s = _rope_tables(pos)
    nseg = s // CMP_STRIDE
    cmp_end = jnp.minimum(jnp.arange(nseg) * CMP_STRIDE + CMP_LEN - 1, s - 1)
    cosc, sinc = _rope_tables(cmp_end)
    h = _even_layer(x, mem, norm_g[0], mem_norm_g[0], mem_w_kv[0], mem_qn[0], mem_kn[0], ev_w_in[0], ev_w_out[0],
                    a_qn[0], a_kn[0], a_sinks[0], b_conv_w[0], b_conv_b[0], b_w_r[0], b_b_r[0], b_w_i[0], b_b_i[0],
                    b_lambda[0], cos, sin_s)
    h = _odd_layer(h, mem, norm_g[1], mem_norm_g[1], mem_w_kv[1], mem_qn[1], mem_kn[1], od_w_in[0], od_w_out[0],
                   c_lb, c_onorm[0], d_qn[0], d_kn_cmp[0], d_kn_slc[0], d_kn_win[0], d_pe_k[0], d_pe_v[0],
                   d_w1k[0], d_w2k[0], d_w1v[0], d_w2v[0], cos, sin_s, cosc, sinc)
    return h
```

```python
import functools

import numpy as np
import jax
import jax.numpy as jnp
from jax import lax
from jax.experimental import pallas as pl
from jax.experimental.pallas import tpu as pltpu

F32 = jnp.float32
BF16 = jnp.bfloat16

D_MODEL = 1024
N_MEM = 256
HEAD_DIM = 64
ROPE_THETA = 10000.0
EPS = 1e-6
NEG_INF = -1e30
POS_INF = 1e30
MASK_BIG = 1e30
LANES = 128

A_HEADS, A_KV_HEADS, A_WINDOW = 8, 2, 128
B_WIDTH, B_BLOCKS, B_CONV, B_C = 512, 8, 4, 8.0
M_HEADS = 4
C_HEADS, C_HEAD_DIM, C_CHUNK = 4, 128, 64
D_HEADS, D_KV_HEADS = 8, 2
CMP_LEN, CMP_STRIDE, CMP_HIDDEN = 32, 16, 128
SEL_LEN, SEL_TOPK = 64, 4
D_WINDOW = 512
D_BRANCHES = 3
SCALE = HEAD_DIM ** -0.5
LOG2E = 1.4426950408889634
Q_SCALE = SCALE * LOG2E

EVEN_ORDER = [("xb", 512), ("gb", 512), ("qa", 512), ("ga", 512), ("qm", 256), ("gm", 256), ("ka", 128), ("va", 128)]
EVEN_FUSED = 1024
EVEN_SRC = {"qa": (0, 512), "ka": (512, 128), "va": (640, 128), "ga": (768, 512), "xb": (1280, 512),
            "gb": (1792, 512), "qm": (2304, 256), "gm": (2560, 256)}
ODD_ORDER = [("qc", 512), ("fc", 512), ("ic", 512), ("gc", 512), ("qd", 512), ("gd", 512), ("qm", 256), ("gm", 256),
             ("ksd", 128), ("vsd", 128), ("kwd", 128), ("vwd", 128), ("gate", 128), ("kcd", 128), ("vcd", 128)]
ODD_SPLIT = (4224, 128, 128)
ODD_SRC = {"qc": (0, 512), "fc": (512, 512), "ic": (1024, 512), "gc": (1536, 512), "qd": (2048, 512),
           "kcd": (2560, 128), "vcd": (2688, 128), "ksd": (2816, 128), "vsd": (2944, 128), "kwd": (3072, 128),
           "vwd": (3200, 128), "gate": (3328, 24), "gd": (3352, 512), "qm": (3864, 256), "gm": (4120, 256)}

VMEM_LIMIT = 48 * 1024 * 1024


def _offsets(order):
    off, out = 0, {}
    for name, w in order:
        out[name] = (off, w)
        off += w
    return out, off


EVEN_OFF, EVEN_COLS = _offsets(EVEN_ORDER[2:])
ODD_OFF, ODD_COLS = _offsets(ODD_ORDER)


def _cparams(sem):
    return pltpu.CompilerParams(dimension_semantics=sem, vmem_limit_bytes=VMEM_LIMIT)


def _dot(a, b):
    return jnp.dot(a, b, preferred_element_type=F32)


def _dot_nt(a, b):
    return lax.dot_general(a, b, (((1,), (1,)), ((), ())), preferred_element_type=F32)


def _split(x):
    hi = x.astype(BF16)
    lo = (x - hi.astype(F32)).astype(BF16)
    return hi, lo


def _dot_split_lhs(x, m):
    hi, lo = _split(x)
    return _dot(hi, m) + _dot(lo, m)


def _dot_split_rhs(m, x):
    hi, lo = _split(x)
    return _dot(m, hi) + _dot(m, lo)


def _lane(shape):
    return lax.broadcasted_iota(jnp.int32, shape, len(shape) - 1)


def _row(shape):
    return lax.broadcasted_iota(jnp.int32, shape, len(shape) - 2)


def _silu(x):
    return x * jax.nn.sigmoid(x)


def _seg_ones():
    r = lax.broadcasted_iota(jnp.int32, (LANES, LANES), 0) >> 6
    c = lax.broadcasted_iota(jnp.int32, (LANES, LANES), 1) >> 6
    return jnp.where(r == c, 1.0, 0.0).astype(BF16)


def _head_rms(x, gain):
    seg = _seg_ones()
    cols = []
    for c in range(x.shape[1] // LANES):
        xc = x[:, c * LANES:(c + 1) * LANES]
        ms = _dot_split_lhs(xc * xc, seg) * (1.0 / HEAD_DIM)
        cols.append(xc * lax.rsqrt(ms + EPS))
    y = cols[0] if len(cols) == 1 else jnp.concatenate(cols, axis=1)
    return y * gain


def _rope(x, cos, sin_s):
    first = (_lane((x.shape[0], LANES)) & 63) < 32
    cols = []
    for c in range(x.shape[1] // LANES):
        xc = x[:, c * LANES:(c + 1) * LANES]
        sw = jnp.where(first, pltpu.roll(xc, 96, 1), pltpu.roll(xc, 32, 1))
        cols.append(xc * cos + sw * sin_s)
    return cols[0] if len(cols) == 1 else jnp.concatenate(cols, axis=1)


def _split_kv(k, fill=0.0):
    lo = _lane(k.shape) < 64
    return jnp.where(lo, k, fill), jnp.where(lo, pltpu.roll(k, 64, 1), fill)


def _q_heads(q, fills=None):
    lo = _lane((q.shape[0], LANES)) < 64
    out = []
    for c in range(q.shape[1] // LANES):
        qc = q[:, c * LANES:(c + 1) * LANES]
        fill = 0.0 if fills is None else fills[c]
        out.append(jnp.where(lo, qc, fill).astype(BF16))
        out.append(jnp.where(lo, pltpu.roll(qc, 64, 1), fill).astype(BF16))
    return out


VT_ROWS = HEAD_DIM + 16


def _with_ones_row(vt):
    n = vt.shape[1]
    pad = jnp.where(_row((VT_ROWS - HEAD_DIM, n)) == 0, 1.0, 0.0)
    return jnp.concatenate([vt, pad], axis=0)


MAX_SAFE_BOUND = 60.0


def _score_bound(q_gain, k_gain):
    return 1.02 * HEAD_DIM * SCALE * LOG2E * jnp.max(jnp.abs(q_gain)) * jnp.max(jnp.abs(k_gain))


def _with_fixed_reference(bound, body):
    small = bound <= MAX_SAFE_BOUND

    @pl.when(small)
    def _():
        body(bound)

    @pl.when(jnp.logical_not(small))
    def _():
        body(None)


def _exp_cols(s, extra=None, fixed=None):
    if fixed is not None:
        return jnp.exp2(s - fixed).astype(BF16), fixed
    m = jnp.max(s, axis=0, keepdims=True)
    if extra is not None:
        m = jnp.maximum(m, extra)
    return jnp.exp2(s - m).astype(BF16), m


def _finish(acc, extra_den=None):
    den = acc[HEAD_DIM:HEAD_DIM + 1, :]
    if extra_den is not None:
        den = den + extra_den
    return acc[0:HEAD_DIM, :] * (1.0 / den)


def _online_update(state, s, vt_tile, fixed=None):
    m, acc = state
    if fixed is not None:
        return m, acc + _dot(vt_tile, _exp_cols(s, fixed=fixed)[0])
    p, m_new = _exp_cols(s, m)
    acc = jnp.exp2(m - m_new) * acc + _dot(vt_tile, p)
    return m_new, acc


def _tile4(x):
    return jnp.concatenate([x, x, x, x], axis=1)


def _inproj_kernel(x_ref, g_ref, w_ref, *o_refs):
    x = x_ref[...]
    ms = jnp.mean(x * x, axis=-1, keepdims=True)
    xn = (x * lax.rsqrt(ms + EPS) * g_ref[...]).astype(BF16)
    starts = np.cumsum([0] + [o.shape[-1] for o in o_refs])
    n = int(starts[-1])
    for c in range(0, n, 512):
        w = min(512, n - c)
        r = _dot(xn, w_ref[:, c:c + w]).astype(BF16)
        for o_ref, lo, hi in zip(o_refs, starts[:-1], starts[1:]):
            a, b = max(c, int(lo)), min(c + w, int(hi))
            if a < b:
                o_ref[:, a - int(lo):b - int(lo)] = r[:, a - c:b - c]


def _inproj(x2d, gain, w_bf16, split, tm=512):
    n, d = x2d.shape
    nc = w_bf16.shape[1]
    assert sum(split) == nc
    return pl.pallas_call(
        _inproj_kernel,
        grid=(n // tm,),
        in_specs=[pl.BlockSpec((tm, d), lambda i: (i, 0)),
                  pl.BlockSpec((1, d), lambda i: (0, 0)),
                  pl.BlockSpec((d, nc), lambda i: (0, 0))],
        out_specs=[pl.BlockSpec((tm, w), lambda i: (i, 0)) for w in split],
        out_shape=[jax.ShapeDtypeStruct((n, w), BF16) for w in split],
        compiler_params=_cparams(("arbitrary",)),
        name="inproj",
    )(x2d, gain.reshape(1, d), w_bf16)


def _outproj_kernel(h_ref, a_ref, b_ref, m_ref, w_ref, o_ref):
    a, b, m = a_ref[...], b_ref[...], m_ref[...]
    for c in range(0, D_MODEL, 256):
        acc = h_ref[:, c:c + 256]
        acc += _dot(a, w_ref[0:512, c:c + 256])
        acc += _dot(b, w_ref[512:1024, c:c + 256])
        acc += _dot(m, w_ref[1024:1280, c:c + 256])
        o_ref[:, c:c + 256] = acc


def _outproj(h2d, oa, ob, om, w_bf16, tm=1024):
    n, d = h2d.shape
    return pl.pallas_call(
        _outproj_kernel,
        grid=(n // tm,),
        in_specs=[pl.BlockSpec((tm, d), lambda i: (i, 0)),
                  pl.BlockSpec((tm, 512), lambda i: (i, 0)),
                  pl.BlockSpec((tm, 512), lambda i: (i, 0)),
                  pl.BlockSpec((tm, 256), lambda i: (i, 0)),
                  pl.BlockSpec((1280, d), lambda i: (0, 0))],
        out_specs=pl.BlockSpec((tm, d), lambda i: (i, 0)),
        out_shape=jax.ShapeDtypeStruct((n, d), F32),
        compiler_params=_cparams(("arbitrary",)),
        name="outproj",
    )(h2d, oa, ob, om, w_bf16)


def _mem_kernel(bound_ref, q_ref, gm_ref, mem_ref, mg_ref, wkv_ref, qn_ref, kn_ref, o_ref, k_scr, vt_scr):
    @pl.when(pl.program_id(1) == 0)
    def _():
        m = mem_ref[0]
        ms = jnp.mean(m * m, axis=-1, keepdims=True)
        mn = (m * lax.rsqrt(ms + EPS) * mg_ref[...]).astype(BF16)
        kv = _dot(mn, wkv_ref[...])
        km = _head_rms(kv[:, 0:256], kn_ref[...])
        for c in range(2):
            k0, k1 = _split_kv(km[:, c * LANES:(c + 1) * LANES])
            k_scr[2 * c] = k0.astype(BF16)
            k_scr[2 * c + 1] = k1.astype(BF16)
        vt = kv[:, 256:512].T
        for h in range(M_HEADS):
            vt_scr[h] = _with_ones_row(vt[h * HEAD_DIM:(h + 1) * HEAD_DIM, :]).astype(BF16)

    def body(fixed):
        q = _head_rms(q_ref[0].astype(F32), qn_ref[...]) * Q_SCALE
        scores = [_dot_nt(k_scr[h], qh) for h, qh in enumerate(_q_heads(q))]
        rows = [_finish(_dot(vt_scr[h], _exp_cols(sc, fixed=fixed)[0])) for h, sc in enumerate(scores)]
        o = jnp.concatenate(rows, axis=0).T
        o_ref[0] = (o * _silu(gm_ref[0].astype(F32))).astype(BF16)

    _with_fixed_reference(bound_ref[0], body)


def _mem_attention(z, q_off, g_off, mem, mem_g, wkv_bf16, qn, kn, tq=1024):
    b, s, _ = z.shape
    return pl.pallas_call(
        _mem_kernel,
        grid=(b, s // tq),
        in_specs=[pl.BlockSpec(memory_space=pltpu.SMEM),
                  pl.BlockSpec((1, tq, 256), lambda bi, i: (bi, i, q_off // 256)),
                  pl.BlockSpec((1, tq, 256), lambda bi, i: (bi, i, g_off // 256)),
                  pl.BlockSpec((1, N_MEM, D_MODEL), lambda bi, i: (bi, 0, 0)),
                  pl.BlockSpec((1, D_MODEL), lambda bi, i: (0, 0)),
                  pl.BlockSpec((D_MODEL, 512), lambda bi, i: (0, 0)),
                  pl.BlockSpec((1, 256), lambda bi, i: (0, 0)),
                  pl.BlockSpec((1, 256), lambda bi, i: (0, 0))],
        out_specs=pl.BlockSpec((1, tq, 256), lambda bi, i: (bi, i, 0)),
        out_shape=jax.ShapeDtypeStruct((b, s, 256), BF16),
        scratch_shapes=[pltpu.VMEM((M_HEADS, N_MEM, LANES), BF16), pltpu.VMEM((M_HEADS, VT_ROWS, N_MEM), BF16)],
        compiler_params=_cparams(("arbitrary", "arbitrary")),
        name="mem_attention",
    )(_score_bound(qn, kn).reshape(1).astype(F32), z, z, mem, mem_g.reshape(1, D_MODEL), wkv_bf16,
      jnp.tile(qn, 4).reshape(1, 256), jnp.tile(kn, 4).reshape(1, 256))


def _swa_kernel(bound_ref, *refs):
    _with_fixed_reference(bound_ref[0], functools.partial(_swa_body, refs))


def _swa_body(refs, fixed):
    (q_ref, g_ref, kc_ref, kp_ref, vc_ref, vp_ref, cosc_ref, sinc_ref, cosp_ref, sinp_ref,
     qn_ref, kn_ref, sink_ref, o_ref) = refs
    i = pl.program_id(1)
    t = q_ref.shape[1]
    tp = kp_ref.shape[1]
    grp = A_HEADS // A_KV_HEADS
    q = _rope(_head_rms(q_ref[0].astype(F32), qn_ref[...]), cosc_ref[...], sinc_ref[...]) * Q_SCALE
    qh = _q_heads(q)
    kc = _rope(_head_rms(kc_ref[0].astype(F32), kn_ref[...]), cosc_ref[...], sinc_ref[...])
    kp = _rope(_head_rms(kp_ref[0].astype(F32), kn_ref[...]), cosp_ref[...], sinp_ref[...])
    ks = [x.astype(BF16) for x in _split_kv(jnp.concatenate([kp, kc], axis=0))]
    vt = jnp.concatenate([vp_ref[0].astype(F32).T, vc_ref[0].astype(F32).T], axis=1)
    ts = i * t - tp + _row((tp + t, t))
    rel = i * t + _lane((tp + t, t)) - ts
    bias = _tile4(jnp.where((rel >= 0) & (rel < A_WINDOW) & (ts >= 0), 0.0, NEG_INF))
    scores = [_dot_nt(ks[kv], jnp.concatenate(qh[kv * grp:(kv + 1) * grp], axis=0)) for kv in range(A_KV_HEADS)]
    rows = []
    for kv in range(A_KV_HEADS):
        sink = jnp.concatenate([jnp.broadcast_to(sink_ref[kv * grp + g:kv * grp + g + 1, 0:1], (1, t))
                                for g in range(grp)], axis=1) * LOG2E
        p, m = _exp_cols(scores[kv] + bias, sink, fixed)
        vt_kv = _with_ones_row(vt[kv * HEAD_DIM:(kv + 1) * HEAD_DIM, :]).astype(BF16)
        o_t = _finish(_dot(vt_kv, p), jnp.exp2(sink - m))
        rows.extend(o_t[:, g * t:(g + 1) * t] for g in range(grp))
    o = jnp.concatenate(rows, axis=0).T
    o_ref[0] = (o * _silu(g_ref[0].astype(F32))).astype(BF16)


def _swa(z, cos, sin_s, qn, kn, sinks, t=256):
    b, s, _ = z.shape
    tp = A_WINDOW
    qo, go = EVEN_OFF["qa"][0] // 512, EVEN_OFF["ga"][0] // 512
    ko, vo = EVEN_OFF["ka"][0] // 128, EVEN_OFF["va"][0] // 128
    prev = lambda i: jnp.maximum(i * (t // tp) - 1, 0)
    bound = jnp.maximum(_score_bound(qn, kn), jnp.max(sinks) * LOG2E).reshape(1).astype(F32)
    return pl.pallas_call(
        _swa_kernel,
        grid=(b, s // t),
        in_specs=[pl.BlockSpec(memory_space=pltpu.SMEM),
                  pl.BlockSpec((1, t, 512), lambda bi, i: (bi, i, qo)),
                  pl.BlockSpec((1, t, 512), lambda bi, i: (bi, i, go)),
                  pl.BlockSpec((1, t, 128), lambda bi, i: (bi, i, ko)),
                  pl.BlockSpec((1, tp, 128), lambda bi, i: (bi, prev(i), ko)),
                  pl.BlockSpec((1, t, 128), lambda bi, i: (bi, i, vo)),
                  pl.BlockSpec((1, tp, 128), lambda bi, i: (bi, prev(i), vo)),
                  pl.BlockSpec((t, 128), lambda bi, i: (i, 0)),
                  pl.BlockSpec((t, 128), lambda bi, i: (i, 0)),
                  pl.BlockSpec((tp, 128), lambda bi, i: (prev(i), 0)),
                  pl.BlockSpec((tp, 128), lambda bi, i: (prev(i), 0)),
                  pl.BlockSpec((1, 512), lambda bi, i: (0, 0)),
                  pl.BlockSpec((1, 128), lambda bi, i: (0, 0)),
                  pl.BlockSpec((A_HEADS, 128), lambda bi, i: (0, 0))],
        out_specs=pl.BlockSpec((1, t, 512), lambda bi, i: (bi, i, 0)),
        out_shape=jax.ShapeDtypeStruct((b, s, 512), BF16),
        compiler_params=_cparams(("arbitrary", "arbitrary")),
        name="swa",
    )(bound, z, z, z, z, z, z, cos, sin_s, cos, sin_s, jnp.tile(qn, 8).reshape(1, 512), jnp.tile(kn, 2).reshape(1, 128),
      jnp.broadcast_to(sinks.reshape(A_HEADS, 1), (A_HEADS, 128)))


def _rglru_tile(first, xb, gb, cw_ref, cb_ref, wr_ref, br_ref, wi_ref, bi_ref, lam_ref, xbuf, hcar, a_scr, u_scr, h_scr,
                between):
    t = xb.shape[0]

    @pl.when(first)
    def _():
        xbuf[0:8, :] = jnp.zeros((8, B_WIDTH), F32)
        hcar[...] = jnp.zeros((8, B_WIDTH), F32)

    xbuf[8:t + 8, :] = xb
    xc = cb_ref[...] + cw_ref[0:1, :] * xbuf[5:t + 5, :]
    for j in range(1, B_CONV):
        xc = xc + cw_ref[j:j + 1, :] * xbuf[5 + j:t + 5 + j, :]
    xbuf[0:8, :] = xbuf[t:t + 8, :]

    r_cols, i_cols = [], []
    for c in range(B_WIDTH // LANES):
        xcc = xc[:, c * LANES:(c + 1) * LANES].astype(BF16)
        r_cols.append(_dot(xcc, wr_ref[c]))
        i_cols.append(_dot(xcc, wi_ref[c]))
    between()
    r = jax.nn.sigmoid(jnp.concatenate(r_cols, axis=1) + br_ref[...])
    ig = jax.nn.sigmoid(jnp.concatenate(i_cols, axis=1) + bi_ref[...])
    nl = -lam_ref[...]
    softplus = jnp.maximum(nl, 0.0) + jnp.log(1.0 + jnp.exp(-jnp.abs(nl)))
    log_a = -B_C * r * softplus
    a = jnp.exp(log_a)
    om = 1.0 - a * a
    u = om * lax.rsqrt(jnp.maximum(om, 1e-30)) * (ig * xc)

    a = a.reshape(t // 8, 8, B_WIDTH)
    u = u.reshape(t // 8, 8, B_WIDTH)
    r8 = lax.broadcasted_iota(jnp.int32, a.shape, 1)
    for d in (1, 2, 4):
        a_sh = pltpu.roll(a, d, 1)
        u_sh = pltpu.roll(u, d, 1)
        m = r8 >= d
        u = jnp.where(m, a * u_sh + u, u)
        a = jnp.where(m, a * a_sh, a)
    a_scr[...] = a.reshape(t, B_WIDTH)
    u_scr[...] = u.reshape(t, B_WIDTH)

    def body(j, h):
        off = pl.multiple_of(j * 8, 8)
        hh = a_scr[pl.ds(off, 8), :] * h + u_scr[pl.ds(off, 8), :]
        h_scr[pl.ds(off, 8), :] = hh
        return hh[7:8, :]

    h_last = lax.fori_loop(0, t // 8, body, hcar[0:1, :], unroll=True)
    hcar[0:1, :] = h_last
    return h_scr[...] * _silu(gb)


def _inproj_rglru_kernel(tiles_per_seq, x_ref, g_ref, w_ref, cw_ref, cb_ref, wr_ref, br_ref, wi_ref, bi_ref, lam_ref,
                         z_ref, ob_ref, xbuf, hcar, a_scr, u_scr, h_scr):
    x = x_ref[...]
    ms = jnp.mean(x * x, axis=-1, keepdims=True)
    xn = (x * lax.rsqrt(ms + EPS) * g_ref[...]).astype(BF16)
    xb = _dot(xn, w_ref[:, 0:B_WIDTH])
    gb = _dot(xn, w_ref[:, B_WIDTH:2 * B_WIDTH])
    first = pl.program_id(0) % tiles_per_seq == 0

    def rest_of_projection():
        n = z_ref.shape[-1]
        for c in range(0, n, 512):
            w = min(512, n - c)
            z_ref[:, c:c + w] = _dot(xn, w_ref[:, EVEN_FUSED + c:EVEN_FUSED + c + w]).astype(BF16)

    ob_ref[...] = _rglru_tile(first, xb, gb, cw_ref, cb_ref, wr_ref, br_ref, wi_ref, bi_ref, lam_ref,
                              xbuf, hcar, a_scr, u_scr, h_scr, rest_of_projection).astype(BF16)


def _inproj_rglru(x2d, gain, w_bf16, seq, conv_w, conv_b, wr_bd, b_r, wi_bd, b_i, lam, tm=512):
    n, d = x2d.shape
    nc = w_bf16.shape[1]
    row = lambda v: v.reshape(1, B_WIDTH)
    full = lambda shp: pl.BlockSpec(shp, lambda i: (0,) * len(shp))
    return pl.pallas_call(
        functools.partial(_inproj_rglru_kernel, seq // tm),
        grid=(n // tm,),
        in_specs=[pl.BlockSpec((tm, d), lambda i: (i, 0)), full((1, d)), full((d, nc)),
                  full((B_CONV, B_WIDTH)), full((1, B_WIDTH)), full((4, LANES, LANES)), full((1, B_WIDTH)),
                  full((4, LANES, LANES)), full((1, B_WIDTH)), full((1, B_WIDTH))],
        out_specs=[pl.BlockSpec((tm, nc - EVEN_FUSED), lambda i: (i, 0)), pl.BlockSpec((tm, B_WIDTH), lambda i: (i, 0))],
        out_shape=[jax.ShapeDtypeStruct((n, nc - EVEN_FUSED), BF16), jax.ShapeDtypeStruct((n, B_WIDTH), BF16)],
        scratch_shapes=[pltpu.VMEM((tm + 8, B_WIDTH), F32), pltpu.VMEM((8, B_WIDTH), F32),
                        pltpu.VMEM((tm, B_WIDTH), F32), pltpu.VMEM((tm, B_WIDTH), F32), pltpu.VMEM((tm, B_WIDTH), F32)],
        compiler_params=_cparams(("arbitrary",)),
        name="inproj_rglru",
    )(x2d, gain.reshape(1, d), w_bf16, conv_w, row(conv_b), wr_bd, row(b_r), wi_bd, row(b_i), row(lam))


def _hgrn_consts():
    c = C_CHUNK
    t = np.arange(c)[:, None]
    s = np.arange(c)[None, :]
    mats = [(s <= t)]
    masks = []
    hs = c // 2
    while hs >= 1:
        mid = (t // (2 * hs)) * 2 * hs + hs - 1
        mats.append(s <= mid)
        same = (t // (2 * hs)) == (s // (2 * hs))
        masks.append(same & ((t // hs) % 2 == 1) & ((s // hs) % 2 == 0))
        hs //= 2
    pairs = [np.concatenate(masks[n:n + 2], axis=1) for n in range(0, len(masks), 2)]
    return np.concatenate(mats, axis=0).astype(np.float32), np.stack(pairs).astype(np.float32)


def _hgrn_kernel(q_ref, f_ref, i_ref, g_ref, lb_ref, og_ref, mst_ref, msk_ref, o_ref, st_scr):
    @pl.when(pl.program_id(1) == 0)
    def _():
        st_scr[...] = jnp.zeros(st_scr.shape, F32)

    c = C_CHUNK
    npair = msk_ref.shape[0]
    p = lb_ref[...]
    pm = jnp.maximum(p[0:1, :], p[1:2, :])
    e0, e1 = jnp.exp(p[0:1, :] - pm), jnp.exp(p[1:2, :] - pm)
    lb = e1 / (e0 + e1)
    zeros = jnp.zeros((c, LANES), BF16)
    for ch in range(q_ref.shape[1] // c):
        rows = slice(ch * c, (ch + 1) * c)
        f = lb + (1.0 - lb) * jax.nn.sigmoid(f_ref[0, rows, :].astype(F32))
        g = jnp.log2(f)
        g_hi, g_lo = _split(g)
        b_all = _dot(mst_ref[0:c, :], g_hi) + _dot(mst_ref[0:c, :], g_lo)
        bm_all = _dot(mst_ref[c:, :], g_hi)
        kk_all = 1.0 - f
        qf_all = _silu(q_ref[0, rows, :].astype(F32))
        v_all = i_ref[0, rows, :].astype(F32)
        kk_bf, qf_bf = kk_all.astype(BF16), qf_all.astype(BF16)
        heads = range(C_HEADS)
        cols = [slice(h * LANES, (h + 1) * LANES) for h in heads]
        sts = [st_scr[h] for h in heads]
        os = [_dot_nt((qf_all[:, cols[h]] * jnp.exp2(b_all[:, cols[h]])).astype(BF16), sts[h].astype(BF16)) for h in heads]
        att = [[] for _ in heads]
        for n in range(npair):
            lhs, rhs = [], []
            for h in heads:
                b = b_all[:, cols[h]]
                qt, kt = [], []
                for l in (2 * n, 2 * n + 1):
                    bm = bm_all[l * c:(l + 1) * c, cols[h]]
                    qt.append(qf_bf[:, cols[h]] * jnp.exp2(jnp.minimum(b - bm, 1.0)).astype(BF16))
                    kt.append(kk_bf[:, cols[h]] * jnp.exp2(jnp.minimum(bm - b, 1.0)).astype(BF16))
                lhs.append(jnp.concatenate(qt, axis=1))
                rhs.append(jnp.concatenate([jnp.concatenate([kt[0], zeros], axis=1),
                                            jnp.concatenate([zeros, kt[1]], axis=1)], axis=0))
            prods = [_dot_nt(lhs[h], rhs[h]) for h in heads]
            for h in heads:
                att[h].append(jnp.where(msk_ref[n] > 0.5, prods[h], 0.0).astype(BF16))
        vbs = [v_all[:, cols[h]].astype(BF16) for h in heads]
        intra = [_dot(jnp.concatenate(att[h], axis=1), jnp.concatenate([vbs[h]] * (2 * npair), axis=0)) for h in heads]
        b_last = b_all[c - 1:c, :]
        kd = (kk_all * jnp.exp2(b_last - b_all)).astype(BF16)
        upd = [_dot(v_all[:, cols[h]].T.astype(BF16), kd[:, cols[h]]) for h in heads]
        decay = jnp.exp2(b_last)
        diag = qf_all * kk_all
        outs = []
        for h in heads:
            st_scr[h] = sts[h] * decay[:, cols[h]] + upd[h]
            o = os[h] + intra[h] + jnp.sum(diag[:, cols[h]], axis=-1, keepdims=True) * v_all[:, cols[h]]
            ms = jnp.mean(o * o, axis=-1, keepdims=True)
            outs.append(o * lax.rsqrt(ms + EPS) * og_ref[...])
        o_ref[0, rows, :] = (jnp.concatenate(outs, axis=1) * _silu(g_ref[0, rows, :].astype(F32))).astype(BF16)


def _hgrn(z, c_lb, c_og, t=512):
    b, s, _ = z.shape
    mst, msk = _hgrn_consts()
    blk = lambda name: pl.BlockSpec((1, t, 512), lambda bi, i, o=ODD_OFF[name][0] // 512: (bi, i, o))
    full = lambda shp: pl.BlockSpec(shp, lambda bi, i: (0,) * len(shp))
    return pl.pallas_call(
        _hgrn_kernel,
        grid=(b, s // t),
        in_specs=[blk("qc"), blk("fc"), blk("ic"), blk("gc"), full(c_lb.shape), full((1, C_HEAD_DIM)),
                  full(mst.shape), full(msk.shape)],
        out_specs=pl.BlockSpec((1, t, 512), lambda bi, i: (bi, i, 0)),
        out_shape=jax.ShapeDtypeStruct((b, s, 512), BF16),
        scratch_shapes=[pltpu.VMEM((C_HEADS, C_HEAD_DIM, C_HEAD_DIM), F32)],
        compiler_params=_cparams(("arbitrary", "arbitrary")),
        name="hgrn2",
    )(z, z, z, z, c_lb, c_og.reshape(1, C_HEAD_DIM), jnp.asarray(mst, BF16), jnp.asarray(msk, F32))


def _nsa_prep_kernel(kcr_ref, vcr_ref, ks_ref, vs_ref, kw_ref, vw_ref, cos_ref, sin_ref, cosc_ref, sinc_ref,
                     kncmp_ref, knslc_ref, knwin_ref, pek_ref, pev_ref, w1k_ref, w2k_ref, w1v_ref, w2vt_ref,
                     kc_o, vct_o, ks_o, vst_o, kw_o, vwt_o):
    def hidden(xr_ref, pe_ref, w1_ref):
        xr = xr_ref[0].astype(F32)
        top = _dot((xr + pe_ref[0:1, :]).astype(BF16), w1_ref[0])
        bot = _dot((xr + pe_ref[1:2, :]).astype(BF16), w1_ref[1])
        pre = top + pltpu.roll(bot, bot.shape[0] - 1, 0)
        return _silu(pre).astype(BF16)

    kc = _dot(hidden(kcr_ref, pek_ref, w1k_ref), w2k_ref[...])
    kc = _rope(_head_rms(kc, kncmp_ref[...]), cosc_ref[...], sinc_ref[...])
    d0, d1 = _split_kv(kc)
    kc_o[0, 0] = d0.astype(BF16)
    kc_o[0, 1] = d1.astype(BF16)
    vct = _dot_nt(w2vt_ref[...], hidden(vcr_ref, pev_ref, w1v_ref))
    for kv in range(D_KV_HEADS):
        vct_o[0, kv] = _with_ones_row(vct[kv * HEAD_DIM:(kv + 1) * HEAD_DIM, :]).astype(BF16)

    s = ks_ref.shape[1]
    step = 256
    for r0 in range(0, s, step):
        rows = slice(r0, r0 + step)
        cos, sin = cos_ref[rows, :], sin_ref[rows, :]
        ks = _rope(_head_rms(ks_ref[0, rows, :].astype(F32), knslc_ref[...]), cos, sin)
        kw = _rope(_head_rms(kw_ref[0, rows, :].astype(F32), knwin_ref[...]), cos, sin)
        blk = (r0 + _row((step, LANES))) // SEL_LEN
        onehot = jnp.where(_lane((step, LANES)) - HEAD_DIM == blk, MASK_BIG, 0.0)
        for src, dst, fill in ((ks, ks_o, onehot), (kw, kw_o, 0.0)):
            d0, d1 = _split_kv(src, fill)
            dst[0, 0, rows, :] = d0.astype(BF16)
            dst[0, 1, rows, :] = d1.astype(BF16)
        for src, dst in ((vs_ref, vst_o), (vw_ref, vwt_o)):
            vt = src[0, rows, :].astype(F32).T
            for kv in range(D_KV_HEADS):
                dst[0, kv, :, rows] = _with_ones_row(vt[kv * HEAD_DIM:(kv + 1) * HEAD_DIM, :]).astype(BF16)


def _nsa_prep(z, kcr, vcr, cos, sin_s, cosc, sinc, kn_cmp, kn_slc, kn_win, pek, pev, w1k, w2k, w1v, w2v):
    b, s, _ = z.shape
    nseg = s // CMP_STRIDE
    zb = lambda name: pl.BlockSpec((1, s, 128), lambda bi, o=ODD_OFF[name][0] // 128: (bi, 0, o))
    full = lambda shp: pl.BlockSpec(shp, lambda bi: (0,) * len(shp))
    seg = pl.BlockSpec((1, nseg, 2048), lambda bi: (bi, 0, 0))
    g2 = lambda g: jnp.tile(g, 2).reshape(1, 128)
    dup = lambda n: (jax.ShapeDtypeStruct((b, 2, n, 128), BF16), pl.BlockSpec((1, 2, n, 128), lambda bi: (bi, 0, 0, 0)))
    tr = lambda n: (jax.ShapeDtypeStruct((b, 2, VT_ROWS, n), BF16),
                    pl.BlockSpec((1, 2, VT_ROWS, n), lambda bi: (bi, 0, 0, 0)))
    outs = [dup(nseg), tr(nseg), dup(s), tr(s), dup(s), tr(s)]
    return pl.pallas_call(
        _nsa_prep_kernel,
        grid=(b,),
        in_specs=[seg, seg, zb("ksd"), zb("vsd"), zb("kwd"), zb("vwd"),
                  full((s, 128)), full((s, 128)), full((nseg, 128)), full((nseg, 128)),
                  full((1, 128)), full((1, 128)), full((1, 128)), full((2, 2048)), full((2, 2048)),
                  full((2, 2048, 256)), full((256, 128)), full((2, 2048, 256)), full((128, 256))],
        out_specs=[o[1] for o in outs],
        out_shape=[o[0] for o in outs],
        compiler_params=_cparams(("arbitrary",)),
        name="nsa_prep",
    )(kcr, vcr, z, z, z, z, cos, sin_s, cosc, sinc, g2(kn_cmp), g2(kn_slc), g2(kn_win), pek, pev, w1k, w2k, w1v, w2v)


SEL_CHUNK = 512
SEL_PIECE = 512


def _nsa_kernel(bound_ref, *refs):
    small = jnp.maximum(jnp.maximum(bound_ref[0], bound_ref[1]), bound_ref[2]) <= MAX_SAFE_BOUND

    @pl.when(small)
    def _():
        _nsa_body((bound_ref[0], bound_ref[1], bound_ref[2]), *refs)

    @pl.when(jnp.logical_not(small))
    def _():
        _nsa_body((None, None, None), *refs)


def _nsa_body(fixed, q_ref, gd_ref, gate_ref, cos_ref, sin_ref, qn_ref, kc_ref, vct_ref, ks_ref, vst_ref, kw_ref, vwt_ref,
              ovlt_ref, o_ref):
    fix_c, fix_s, fix_w = fixed
    i = pl.program_id(1)
    t = q_ref.shape[1]
    grp = D_HEADS // D_KV_HEADS
    w = grp * t
    kvs = range(D_KV_HEADS)
    q = _rope(_head_rms(q_ref[0].astype(F32), qn_ref[...]), cos_ref[...], sin_ref[...]) * Q_SCALE
    stack = lambda heads: [jnp.concatenate(heads[kv * grp:(kv + 1) * grp], axis=0) for kv in kvs]
    qs = stack(_q_heads(q))
    tq1 = i * t + _lane((1, t))

    nwin = D_WINDOW + t
    win0 = pl.multiple_of(jnp.maximum(i * t + t - nwin, 0), t)
    s_cmp = [_dot_nt(kc_ref[0, kv], qs[kv]) for kv in kvs]
    s_win = [_dot_nt(kw_ref[0, kv, pl.ds(win0, nwin), :], qs[kv]) for kv in kvs]

    ncmp = kc_ref.shape[2] - 1
    nsel = ovlt_ref.shape[0]
    crow = _row((kc_ref.shape[2], t))
    bias_c = _tile4(jnp.where((crow * CMP_STRIDE + (CMP_LEN - 1) <= tq1) & (crow < ncmp), 0.0, NEG_INF))
    row_ok = _tile4(tq1 >= CMP_LEN - 1)
    es = [_exp_cols(s_cmp[kv] + bias_c, fixed=fix_c)[0] for kv in kvs]
    accs = [_dot(vct_ref[0, kv], es[kv]) for kv in kvs]
    invs = [jnp.where(row_ok, 1.0 / accs[kv][HEAD_DIM:HEAD_DIM + 1, :], 0.0) for kv in kvs]
    oc_t = [accs[kv][0:HEAD_DIM, :] * invs[kv] for kv in kvs]
    psums = []
    for kv in kvs:
        p = es[kv].astype(F32) * invs[kv]
        psums.append(p[:, 0:t] + p[:, t:2 * t] + p[:, 2 * t:3 * t] + p[:, 3 * t:4 * t])
    imps = [_dot_split_rhs(ovlt_ref[...], psums[kv]) for kv in kvs]

    rel = tq1 - (win0 + _row((nwin, t)))
    bias_w = _tile4(jnp.where((rel >= 0) & (rel < D_WINDOW), 0.0, NEG_INF))
    ow_t = [_finish(_dot(vwt_ref[0, kv, :, pl.ds(win0, nwin)], _exp_cols(s_win[kv] + bias_w, fixed=fix_w)[0]))
            for kv in kvs]

    jrow = _row((nsel, t))
    jrow_f = jrow.astype(F32)
    cur = (i * t + _lane((nsel, t))) // SEL_LEN
    forced = (jrow == 0) | (jrow == cur)
    fills = []
    for kv in kvs:
        score = jnp.where(forced, POS_INF, jnp.where(jrow <= cur, imps[kv], NEG_INF))
        chosen = jnp.zeros((nsel, t), F32)
        for _ in range(min(SEL_TOPK, nsel)):
            mx = jnp.max(score, axis=0, keepdims=True)
            first = jnp.min(jnp.where(score == mx, jrow_f, 1e9), axis=0, keepdims=True)
            hit = jrow_f == first
            chosen = jnp.where(hit, 1.0, chosen)
            score = jnp.where(hit, -3e38, score)
        frame = jnp.concatenate([jnp.zeros((HEAD_DIM, t), F32), chosen - 1.0,
                                 jnp.zeros((LANES - HEAD_DIM - nsel, t), F32)], axis=0)
        fills.extend([frame.T] * (grp // 2))

    init = (jnp.full((1, w), NEG_INF, F32), jnp.zeros((VT_ROWS, w), F32))
    pc = SEL_PIECE

    def sweep(states, k_ref, vt_ref, queries, offs, biases, n):
        scores = [[_dot_nt(k_ref[0, kv, pl.ds(off, n), :], queries[kv]) for kv in kvs] for off in offs]
        for off, sc, bias in zip(offs, scores, biases):
            states = tuple(_online_update(states[kv], sc[kv] if bias is None else sc[kv] + bias,
                                          vt_ref[0, kv, :, pl.ds(off, n)], fix_s) for kv in kvs)
        return states

    qsel = stack(_q_heads(q, fills))
    ch = SEL_CHUNK
    own = pl.multiple_of(((i * t) // ch) * ch, ch)
    st = sweep((init, init), ks_ref, vst_ref, qsel, [own],
               [_tile4(jnp.where(own + _row((ch, t)) <= tq1, 0.0, NEG_INF))], ch)

    def sel_chunk(c, states):
        base = pl.multiple_of(c * ch, ch)
        return sweep(states, ks_ref, vst_ref, qsel, [base + n * pc for n in range(ch // pc)], [None] * (ch // pc), pc)

    st = lax.fori_loop(0, (i * t) // ch, sel_chunk, st)
    os_t = [_finish(acc) for (_, acc) in st]

    g_t = jax.nn.sigmoid(gate_ref[0].astype(F32)).T
    rows = []
    for h in range(D_HEADS):
        kv, g = divmod(h, grp)
        cols = slice(g * t, (g + 1) * t)
        r = D_BRANCHES * h
        rows.append(g_t[r:r + 1, :] * oc_t[kv][:, cols] + g_t[r + 1:r + 2, :] * os_t[kv][:, cols]
                    + g_t[r + 2:r + 3, :] * ow_t[kv][:, cols])
    o = jnp.concatenate(rows, axis=0).T
    o_ref[0] = (o * _silu(gd_ref[0].astype(F32))).astype(BF16)


def _overlap_t(s):
    ncmp = (s - CMP_LEN) // CMP_STRIDE + 1
    nsel = s // SEL_LEN
    cs = np.arange(ncmp)[None, :] * CMP_STRIDE
    ss = np.arange(nsel)[:, None] * SEL_LEN
    ovl = np.zeros((nsel, s // CMP_STRIDE), np.float32)
    ovl[:, :ncmp] = (cs < ss + SEL_LEN) & (cs + CMP_LEN > ss)
    return ovl


def _nsa(z, cos, sin_s, qn, bounds, kc, vct, ks, vst, kw, vwt, t=256):
    b, s, _ = z.shape
    nseg = s // CMP_STRIDE
    assert nseg == LANES, "compressed-block scores are laid out on one 128-row tile"
    ovlt = _overlap_t(s)
    full = lambda shp: pl.BlockSpec(shp, lambda bi, i: (0,) * len(shp))
    dup = lambda n: pl.BlockSpec((1, 2, n, 128), lambda bi, i: (bi, 0, 0, 0))
    tr = lambda n: pl.BlockSpec((1, 2, VT_ROWS, n), lambda bi, i: (bi, 0, 0, 0))
    return pl.pallas_call(
        _nsa_kernel,
        grid=(b, s // t),
        in_specs=[pl.BlockSpec(memory_space=pltpu.SMEM),
                  pl.BlockSpec((1, t, 512), lambda bi, i: (bi, i, ODD_OFF["qd"][0] // 512)),
                  pl.BlockSpec((1, t, 512), lambda bi, i: (bi, i, ODD_OFF["gd"][0] // 512)),
                  pl.BlockSpec((1, t, 128), lambda bi, i: (bi, i, ODD_OFF["gate"][0] // 128)),
                  pl.BlockSpec((t, 128), lambda bi, i: (i, 0)),
                  pl.BlockSpec((t, 128), lambda bi, i: (i, 0)),
                  full((1, 512)), dup(nseg), tr(nseg), dup(s), tr(s), dup(s), tr(s), full(ovlt.shape)],
        out_specs=pl.BlockSpec((1, t, 512), lambda bi, i: (bi, i, 0)),
        out_shape=jax.ShapeDtypeStruct((b, s, 512), BF16),
        compiler_params=_cparams(("arbitrary", "arbitrary")),
        name="nsa",
    )(bounds, z, z, z, cos, sin_s, jnp.tile(qn, 8).reshape(1, 512), kc, vct, ks, vst, kw, vwt, jnp.asarray(ovlt, BF16))


def _permute_cols(w, order, src):
    cols = []
    for name, width in order:
        o, sw = src[name]
        blk = w[:, o:o + sw]
        if sw < width:
            blk = jnp.pad(blk, ((0, 0), (0, width - sw)))
        cols.append(blk)
    return jnp.concatenate(cols, axis=1).astype(BF16)


def _block_diag_pairs(w):
    z = jnp.zeros((4, LANES, LANES), w.dtype)
    z = z.at[:, 0:64, 0:64].set(w[0::2])
    z = z.at[:, 64:128, 64:128].set(w[1::2])
    return z.astype(BF16)


def _rope_tables(pos):
    half = HEAD_DIM // 2
    inv = ROPE_THETA ** (-jnp.arange(half, dtype=F32) / half)
    ang = pos.astype(F32)[:, None] * inv[None, :]
    cos, sin = jnp.cos(ang), jnp.sin(ang)
    cos_t = jnp.tile(cos, (1, 4))
    sin_t = jnp.tile(jnp.concatenate([-sin, sin], axis=1), (1, 2))
    return cos_t, sin_t


def _expand_compress_w1(w1):
    hdim = w1.shape[1]
    w = w1.reshape(2, CMP_STRIDE, 1, HEAD_DIM, 1, hdim)
    same_head = jnp.eye(2, dtype=w1.dtype).reshape(1, 1, 2, 1, 2, 1)
    return (w * same_head).reshape(2, CMP_STRIDE * 2 * HEAD_DIM, 2 * hdim).astype(BF16)


def _expand_compress_w2(w2):
    hdim, hd = w2.shape
    z = jnp.zeros((2 * hdim, 2 * hd), w2.dtype)
    z = z.at[0:hdim, 0:hd].set(w2)
    z = z.at[hdim:, hd:].set(w2)
    return z.astype(BF16)


def _expand_pe(pe):
    p = pe.reshape(2, CMP_STRIDE, 1, HEAD_DIM)
    return jnp.broadcast_to(p, (2, CMP_STRIDE, 2, HEAD_DIM)).reshape(2, CMP_STRIDE * 2 * HEAD_DIM)


def _even_layer(h, mem, g, mem_g, w_mem_kv, m_qn, m_kn, w_in, w_out, a_qn, a_kn, a_sinks,
                conv_w, conv_b, w_r, b_r, w_i, b_i, lam, cos, sin_s):
    b, s, d = h.shape
    h2 = h.reshape(b * s, d)
    z, ob = _inproj_rglru(h2, g, _permute_cols(w_in, EVEN_ORDER, EVEN_SRC), s, conv_w, conv_b,
                          _block_diag_pairs(w_r), b_r, _block_diag_pairs(w_i), b_i, lam)
    z = z.reshape(b, s, EVEN_COLS)
    oa = _swa(z, cos, sin_s, a_qn, a_kn, a_sinks)
    om = _mem_attention(z, EVEN_OFF["qm"][0], EVEN_OFF["gm"][0], mem, mem_g, w_mem_kv.astype(BF16), m_qn, m_kn)
    out = _outproj(h2, oa.reshape(b * s, 512), ob, om.reshape(b * s, 256), w_out.astype(BF16))
    return out.reshape(b, s, d)


def _odd_layer(h, mem, g, mem_g, w_mem_kv, m_qn, m_kn, w_in, w_out, c_lb, c_og,
               d_qn, d_kn_cmp, d_kn_slc, d_kn_win, pe_k, pe_v, w1k, w2k, w1v, w2v, cos, sin_s, cosc, sinc):
    b, s, d = h.shape
    h2 = h.reshape(b * s, d)
    z, kcd, vcd = _inproj(h2, g, _permute_cols(w_in, ODD_ORDER, ODD_SRC), ODD_SPLIT)
    z = z.reshape(b, s, ODD_SPLIT[0])
    oc = _hgrn(z, c_lb, c_og)
    nseg = s // CMP_STRIDE
    seg = lambda a: a.reshape(b, nseg, CMP_STRIDE * 128)
    kc, vc, ks, vs, kw, vw = _nsa_prep(
        z, seg(kcd), seg(vcd), cos, sin_s, cosc, sinc, d_kn_cmp, d_kn_slc, d_kn_win,
        _expand_pe(pe_k), _expand_pe(pe_v), _expand_compress_w1(w1k), _expand_compress_w2(w2k),
        _expand_compress_w1(w1v), _expand_compress_w2(w2v).T)
    bounds = jnp.stack([_score_bound(d_qn, kn) for kn in (d_kn_cmp, d_kn_slc, d_kn_win)]).astype(F32)
    od = _nsa(z, cos, sin_s, d_qn, bounds, kc, vc, ks, vs, kw, vw)
    om = _mem_attention(z, ODD_OFF["qm"][0], ODD_OFF["gm"][0], mem, mem_g, w_mem_kv.astype(BF16), m_qn, m_kn)
    out = _outproj(h2, oc.reshape(b * s, 512), od.reshape(b * s, 512), om.reshape(b * s, 256), w_out.astype(BF16))
    return out.reshape(b, s, d)


def kernel(x, mem, norm_g, mem_norm_g, mem_w_kv, mem_qn, mem_kn, ev_w_in, ev_w_out, a_qn, a_kn, a_sinks,
           b_conv_w, b_conv_b, b_w_r, b_b_r, b_w_i, b_b_i, b_lambda, od_w_in, od_w_out, c_lb, c_onorm,
           d_qn, d_kn_cmp, d_kn_slc, d_kn_win, d_pe_k, d_pe_v, d_w1k, d_w2k, d_w1v, d_w2v):
    depth = norm_g.shape[0]
    assert depth == 2 and c_lb.shape[0] == 2, "the HGRN2 lower-bound formula in the kernel is written for depth 2"
    s = x.shape[1]
    assert s % 256 == 0 and s >= D_WINDOW
    pos = jnp.arange(s)
    cos, sin_s = _rope_tables(pos)
    nseg = s // CMP_STRIDE
    cmp_end = jnp.minimum(jnp.arange(nseg) * CMP_STRIDE + CMP_LEN - 1, s - 1)
    cosc, sinc = _rope_tables(cmp_end)
    h = _even_layer(x, mem, norm_g[0], mem_norm_g[0], mem_w_kv[0], mem_qn[0], mem_kn[0], ev_w_in[0], ev_w_out[0],
                    a_qn[0], a_kn[0], a_sinks[0], b_conv_w[0], b_conv_b[0], b_w_r[0], b_b_r[0], b_w_i[0], b_b_i[0],
                    b_lambda[0], cos, sin_s)
    h = _odd_layer(h, mem, norm_g[1], mem_norm_g[1], mem_w_kv[1], mem_qn[1], mem_kn[1], od_w_in[0], od_w_out[0],
                   c_lb, c_onorm[0], d_qn[0], d_kn_cmp[0], d_kn_slc[0], d_kn_win[0], d_pe_k[0], d_pe_v[0],
                   d_w1k[0], d_w2k[0], d_w1v[0], d_w2v[0], cos, sin_s, cosc, sinc)
    return h
```

```python
import functools

import numpy as np
import jax
import jax.numpy as jnp
from jax import lax
from jax.experimental import pallas as pl
from jax.experimental.pallas import tpu as pltpu

F32 = jnp.float32
BF16 = jnp.bfloat16

D_MODEL = 1024
N_MEM = 256
HEAD_DIM = 64
ROPE_THETA = 10000.0
EPS = 1e-6
NEG_INF = -1e30
POS_INF = 1e30
MASK_BIG = 1e30
LANES = 128

A_HEADS, A_KV_HEADS, A_WINDOW = 8, 2, 128
B_WIDTH, B_BLOCKS, B_CONV, B_C = 512, 8, 4, 8.0
M_HEADS = 4
C_HEADS, C_HEAD_DIM, C_CHUNK = 4, 128, 64
D_HEADS, D_KV_HEADS = 8, 2
CMP_LEN, CMP_STRIDE, CMP_HIDDEN = 32, 16, 128
SEL_LEN, SEL_TOPK = 64, 4
D_WINDOW = 512
D_BRANCHES = 3
SCALE = HEAD_DIM ** -0.5
LOG2E = 1.4426950408889634
Q_SCALE = SCALE * LOG2E

EVEN_ORDER = [("xb", 512), ("gb", 512), ("qa", 512), ("ga", 512), ("qm", 256), ("gm", 256), ("ka", 128), ("va", 128)]
EVEN_FUSED = 1024
EVEN_SRC = {"qa": (0, 512), "ka": (512, 128), "va": (640, 128), "ga": (768, 512), "xb": (1280, 512),
            "gb": (1792, 512), "qm": (2304, 256), "gm": (2560, 256)}
ODD_ORDER = [("qc", 512), ("fc", 512), ("ic", 512), ("gc", 512), ("qd", 512), ("gd", 512), ("qm", 256), ("gm", 256),
             ("ksd", 128), ("vsd", 128), ("kwd", 128), ("vwd", 128), ("gate", 128), ("kcd", 128), ("vcd", 128)]
ODD_SPLIT = (4224, 128, 128)
ODD_SRC = {"qc": (0, 512), "fc": (512, 512), "ic": (1024, 512), "gc": (1536, 512), "qd": (2048, 512),
           "kcd": (2560, 128), "vcd": (2688, 128), "ksd": (2816, 128), "vsd": (2944, 128), "kwd": (3072, 128),
           "vwd": (3200, 128), "gate": (3328, 24), "gd": (3352, 512), "qm": (3864, 256), "gm": (4120, 256)}

VMEM_LIMIT = 48 * 1024 * 1024


def _offsets(order):
    off, out = 0, {}
    for name, w in order:
        out[name] = (off, w)
        off += w
    return out, off


EVEN_OFF, EVEN_COLS = _offsets(EVEN_ORDER[2:])
ODD_OFF, ODD_COLS = _offsets(ODD_ORDER)


def _cparams(sem):
    return pltpu.CompilerParams(dimension_semantics=sem, vmem_limit_bytes=VMEM_LIMIT)


def _dot(a, b):
    return jnp.dot(a, b, preferred_element_type=F32)


def _dot_nt(a, b):
    return lax.dot_general(a, b, (((1,), (1,)), ((), ())), preferred_element_type=F32)


def _split(x):
    hi = x.astype(BF16)
    lo = (x - hi.astype(F32)).astype(BF16)
    return hi, lo


def _dot_split_lhs(x, m):
    hi, lo = _split(x)
    return _dot(hi, m) + _dot(lo, m)


def _dot_split_rhs(m, x):
    hi, lo = _split(x)
    return _dot(m, hi) + _dot(m, lo)


def _lane(shape):
    return lax.broadcasted_iota(jnp.int32, shape, len(shape) - 1)


def _row(shape):
    return lax.broadcasted_iota(jnp.int32, shape, len(shape) - 2)


def _silu(x):
    return x * jax.nn.sigmoid(x)


def _seg_ones():
    r = lax.broadcasted_iota(jnp.int32, (LANES, LANES), 0) >> 6
    c = lax.broadcasted_iota(jnp.int32, (LANES, LANES), 1) >> 6
    return jnp.where(r == c, 1.0, 0.0).astype(BF16)


def _head_rms(x, gain):
    seg = _seg_ones()
    cols = []
    for c in range(x.shape[1] // LANES):
        xc = x[:, c * LANES:(c + 1) * LANES]
        ms = _dot_split_lhs(xc * xc, seg) * (1.0 / HEAD_DIM)
        cols.append(xc * lax.rsqrt(ms + EPS))
    y = cols[0] if len(cols) == 1 else jnp.concatenate(cols, axis=1)
    return y * gain


def _rope(x, cos, sin_s):
    first = (_lane((x.shape[0], LANES)) & 63) < 32
    cols = []
    for c in range(x.shape[1] // LANES):
        xc = x[:, c * LANES:(c + 1) * LANES]
        sw = jnp.where(first, pltpu.roll(xc, 96, 1), pltpu.roll(xc, 32, 1))
        cols.append(xc * cos + sw * sin_s)
    return cols[0] if len(cols) == 1 else jnp.concatenate(cols, axis=1)


def _split_kv(k, fill=0.0):
    lo = _lane(k.shape) < 64
    return jnp.where(lo, k, fill), jnp.where(lo, pltpu.roll(k, 64, 1), fill)


def _q_heads(q, fills=None):
    lo = _lane((q.shape[0], LANES)) < 64
    out = []
    for c in range(q.shape[1] // LANES):
        qc = q[:, c * LANES:(c + 1) * LANES]
        fill = 0.0 if fills is None else fills[c]
        out.append(jnp.where(lo, qc, fill).astype(BF16))
        out.append(jnp.where(lo, pltpu.roll(qc, 64, 1), fill).astype(BF16))
    return out


VT_ROWS = HEAD_DIM + 16


def _with_ones_row(vt):
    n = vt.shape[1]
    pad = jnp.where(_row((VT_ROWS - HEAD_DIM, n)) == 0, 1.0, 0.0)
    return jnp.concatenate([vt, pad], axis=0)


MAX_SAFE_BOUND = 60.0


def _score_bound(q_gain, k_gain):
    return 1.02 * HEAD_DIM * SCALE * LOG2E * jnp.max(jnp.abs(q_gain)) * jnp.max(jnp.abs(k_gain))


def _with_fixed_reference(bound, body):
    small = bound <= MAX_SAFE_BOUND

    @pl.when(small)
    def _():
        body(bound)

    @pl.when(jnp.logical_not(small))
    def _():
        body(None)


def _exp_cols(s, extra=None, fixed=None):
    if fixed is not None:
        return jnp.exp2(s - fixed).astype(BF16), fixed
    m = jnp.max(s, axis=0, keepdims=True)
    if extra is not None:
        m = jnp.maximum(m, extra)
    return jnp.exp2(s - m).astype(BF16), m


def _finish(acc, extra_den=None):
    den = acc[HEAD_DIM:HEAD_DIM + 1, :]
    if extra_den is not None:
        den = den + extra_den
    return acc[0:HEAD_DIM, :] * (1.0 / den)


def _online_update(state, s, vt_tile, fixed=None):
    m, acc = state
    if fixed is not None:
        return m, acc + _dot(vt_tile, _exp_cols(s, fixed=fixed)[0])
    p, m_new = _exp_cols(s, m)
    acc = jnp.exp2(m - m_new) * acc + _dot(vt_tile, p)
    return m_new, acc


def _tile4(x):
    return jnp.concatenate([x, x, x, x], axis=1)


def _inproj_kernel(x_ref, g_ref, w_ref, *o_refs):
    x = x_ref[...]
    ms = jnp.mean(x * x, axis=-1, keepdims=True)
    xn = (x * lax.rsqrt(ms + EPS) * g_ref[...]).astype(BF16)
    starts = np.cumsum([0] + [o.shape[-1] for o in o_refs])
    n = int(starts[-1])
    for c in range(0, n, 512):
        w = min(512, n - c)
        r = _dot(xn, w_ref[:, c:c + w]).astype(BF16)
        for o_ref, lo, hi in zip(o_refs, starts[:-1], starts[1:]):
            a, b = max(c, int(lo)), min(c + w, int(hi))
            if a < b:
                o_ref[:, a - int(lo):b - int(lo)] = r[:, a - c:b - c]


def _inproj(x2d, gain, w_bf16, split, tm=512):
    n, d = x2d.shape
    nc = w_bf16.shape[1]
    assert sum(split) == nc
    return pl.pallas_call(
        _inproj_kernel,
        grid=(n // tm,),
        in_specs=[pl.BlockSpec((tm, d), lambda i: (i, 0)),
                  pl.BlockSpec((1, d), lambda i: (0, 0)),
                  pl.BlockSpec((d, nc), lambda i: (0, 0))],
        out_specs=[pl.BlockSpec((tm, w), lambda i: (i, 0)) for w in split],
        out_shape=[jax.ShapeDtypeStruct((n, w), BF16) for w in split],
        compiler_params=_cparams(("arbitrary",)),
        name="inproj",
    )(x2d, gain.reshape(1, d), w_bf16)


def _outproj_kernel(h_ref, a_ref, b_ref, m_ref, w_ref, o_ref):
    a, b, m = a_ref[...], b_ref[...], m_ref[...]
    for c in range(0, D_MODEL, 256):
        acc = h_ref[:, c:c + 256]
        acc += _dot(a, w_ref[0:512, c:c + 256])
        acc += _dot(b, w_ref[512:1024, c:c + 256])
        acc += _dot(m, w_ref[1024:1280, c:c + 256])
        o_ref[:, c:c + 256] = acc


def _outproj(h2d, oa, ob, om, w_bf16, tm=1024):
    n, d = h2d.shape
    return pl.pallas_call(
        _outproj_kernel,
        grid=(n // tm,),
        in_specs=[pl.BlockSpec((tm, d), lambda i: (i, 0)),
                  pl.BlockSpec((tm, 512), lambda i: (i, 0)),
                  pl.BlockSpec((tm, 512), lambda i: (i, 0)),
                  pl.BlockSpec((tm, 256), lambda i: (i, 0)),
                  pl.BlockSpec((1280, d), lambda i: (0, 0))],
        out_specs=pl.BlockSpec((tm, d), lambda i: (i, 0)),
        out_shape=jax.ShapeDtypeStruct((n, d), F32),
        compiler_params=_cparams(("arbitrary",)),
        name="outproj",
    )(h2d, oa, ob, om, w_bf16)


def _mem_kernel(bound_ref, q_ref, gm_ref, mem_ref, mg_ref, wkv_ref, qn_ref, kn_ref, o_ref, k_scr, vt_scr):
    @pl.when(pl.program_id(1) == 0)
    def _():
        m = mem_ref[0]
        ms = jnp.mean(m * m, axis=-1, keepdims=True)
        mn = (m * lax.rsqrt(ms + EPS) * mg_ref[...]).astype(BF16)
        kv = _dot(mn, wkv_ref[...])
        km = _head_rms(kv[:, 0:256], kn_ref[...])
        for c in range(2):
            k0, k1 = _split_kv(km[:, c * LANES:(c + 1) * LANES])
            k_scr[2 * c] = k0.astype(BF16)
            k_scr[2 * c + 1] = k1.astype(BF16)
        vt = kv[:, 256:512].T
        for h in range(M_HEADS):
            vt_scr[h] = _with_ones_row(vt[h * HEAD_DIM:(h + 1) * HEAD_DIM, :]).astype(BF16)

    def body(fixed):
        q = _head_rms(q_ref[0].astype(F32), qn_ref[...]) * Q_SCALE
        scores = [_dot_nt(k_scr[h], qh) for h, qh in enumerate(_q_heads(q))]
        rows = [_finish(_dot(vt_scr[h], _exp_cols(sc, fixed=fixed)[0])) for h, sc in enumerate(scores)]
        o = jnp.concatenate(rows, axis=0).T
        o_ref[0] = (o * _silu(gm_ref[0].astype(F32))).astype(BF16)

    _with_fixed_reference(bound_ref[0], body)


def _mem_attention(z, q_off, g_off, mem, mem_g, wkv_bf16, qn, kn, tq=1024):
    b, s, _ = z.shape
    return pl.pallas_call(
        _mem_kernel,
        grid=(b, s // tq),
        in_specs=[pl.BlockSpec(memory_space=pltpu.SMEM),
                  pl.BlockSpec((1, tq, 256), lambda bi, i: (bi, i, q_off // 256)),
                  pl.BlockSpec((1, tq, 256), lambda bi, i: (bi, i, g_off // 256)),
                  pl.BlockSpec((1, N_MEM, D_MODEL), lambda bi, i: (bi, 0, 0)),
                  pl.BlockSpec((1, D_MODEL), lambda bi, i: (0, 0)),
                  pl.BlockSpec((D_MODEL, 512), lambda bi, i: (0, 0)),
                  pl.BlockSpec((1, 256), lambda bi, i: (0, 0)),
                  pl.BlockSpec((1, 256), lambda bi, i: (0, 0))],
        out_specs=pl.BlockSpec((1, tq, 256), lambda bi, i: (bi, i, 0)),
        out_shape=jax.ShapeDtypeStruct((b, s, 256), BF16),
        scratch_shapes=[pltpu.VMEM((M_HEADS, N_MEM, LANES), BF16), pltpu.VMEM((M_HEADS, VT_ROWS, N_MEM), BF16)],
        compiler_params=_cparams(("arbitrary", "arbitrary")),
        name="mem_attention",
    )(_score_bound(qn, kn).reshape(1).astype(F32), z, z, mem, mem_g.reshape(1, D_MODEL), wkv_bf16,
      jnp.tile(qn, 4).reshape(1, 256), jnp.tile(kn, 4).reshape(1, 256))


def _swa_kernel(bound_ref, *refs):
    _with_fixed_reference(bound_ref[0], functools.partial(_swa_body, refs))


def _swa_body(refs, fixed):
    (q_ref, g_ref, kc_ref, kp_ref, vc_ref, vp_ref, cosc_ref, sinc_ref, cosp_ref, sinp_ref,
     qn_ref, kn_ref, sink_ref, o_ref) = refs
    i = pl.program_id(1)
    t = q_ref.shape[1]
    tp = kp_ref.shape[1]
    grp = A_HEADS // A_KV_HEADS
    q = _rope(_head_rms(q_ref[0].astype(F32), qn_ref[...]), cosc_ref[...], sinc_ref[...]) * Q_SCALE
    qh = _q_heads(q)
    kc = _rope(_head_rms(kc_ref[0].astype(F32), kn_ref[...]), cosc_ref[...], sinc_ref[...])
    kp = _rope(_head_rms(kp_ref[0].astype(F32), kn_ref[...]), cosp_ref[...], sinp_ref[...])
    ks = [x.astype(BF16) for x in _split_kv(jnp.concatenate([kp, kc], axis=0))]
    vt = jnp.concatenate([vp_ref[0].astype(F32).T, vc_ref[0].astype(F32).T], axis=1)
    ts = i * t - tp + _row((tp + t, t))
    rel = i * t + _lane((tp + t, t)) - ts
    bias = _tile4(jnp.where((rel >= 0) & (rel < A_WINDOW) & (ts >= 0), 0.0, NEG_INF))
    scores = [_dot_nt(ks[kv], jnp.concatenate(qh[kv * grp:(kv + 1) * grp], axis=0)) for kv in range(A_KV_HEADS)]
    rows = []
    for kv in range(A_KV_HEADS):
        sink = jnp.concatenate([jnp.broadcast_to(sink_ref[kv * grp + g:kv * grp + g + 1, 0:1], (1, t))
                                for g in range(grp)], axis=1) * LOG2E
        p, m = _exp_cols(scores[kv] + bias, sink, fixed)
        vt_kv = _with_ones_row(vt[kv * HEAD_DIM:(kv + 1) * HEAD_DIM, :]).astype(BF16)
        o_t = _finish(_dot(vt_kv, p), jnp.exp2(sink - m))
        rows.extend(o_t[:, g * t:(g + 1) * t] for g in range(grp))
    o = jnp.concatenate(rows, axis=0).T
    o_ref[0] = (o * _silu(g_ref[0].astype(F32))).astype(BF16)


def _swa(z, cos, sin_s, qn, kn, sinks, t=256):
    b, s, _ = z.shape
    tp = A_WINDOW
    qo, go = EVEN_OFF["qa"][0] // 512, EVEN_OFF["ga"][0] // 512
    ko, vo = EVEN_OFF["ka"][0] // 128, EVEN_OFF["va"][0] // 128
    prev = lambda i: jnp.maximum(i * (t // tp) - 1, 0)
    bound = jnp.maximum(_score_bound(qn, kn), jnp.max(sinks) * LOG2E).reshape(1).astype(F32)
    return pl.pallas_call(
        _swa_kernel,
        grid=(b, s // t),
        in_specs=[pl.BlockSpec(memory_space=pltpu.SMEM),
                  pl.BlockSpec((1, t, 512), lambda bi, i: (bi, i, qo)),
                  pl.BlockSpec((1, t, 512), lambda bi, i: (bi, i, go)),
                  pl.BlockSpec((1, t, 128), lambda bi, i: (bi, i, ko)),
                  pl.BlockSpec((1, tp, 128), lambda bi, i: (bi, prev(i), ko)),
                  pl.BlockSpec((1, t, 128), lambda bi, i: (bi, i, vo)),
                  pl.BlockSpec((1, tp, 128), lambda bi, i: (bi, prev(i), vo)),
                  pl.BlockSpec((t, 128), lambda bi, i: (i, 0)),
                  pl.BlockSpec((t, 128), lambda bi, i: (i, 0)),
                  pl.BlockSpec((tp, 128), lambda bi, i: (prev(i), 0)),
                  pl.BlockSpec((tp, 128), lambda bi, i: (prev(i), 0)),
                  pl.BlockSpec((1, 512), lambda bi, i: (0, 0)),
                  pl.BlockSpec((1, 128), lambda bi, i: (0, 0)),
                  pl.BlockSpec((A_HEADS, 128), lambda bi, i: (0, 0))],
        out_specs=pl.BlockSpec((1, t, 512), lambda bi, i: (bi, i, 0)),
        out_shape=jax.ShapeDtypeStruct((b, s, 512), BF16),
        compiler_params=_cparams(("arbitrary", "arbitrary")),
        name="swa",
    )(bound, z, z, z, z, z, z, cos, sin_s, cos, sin_s, jnp.tile(qn, 8).reshape(1, 512), jnp.tile(kn, 2).reshape(1, 128),
      jnp.broadcast_to(sinks.reshape(A_HEADS, 1), (A_HEADS, 128)))


def _rglru_half(cols, xb, gb, emit, cw_ref, cb_ref, wr_ref, br_ref, wi_ref, bi_ref, lam_ref, xbuf, hcar, a_scr, u_scr, h_scr):
    t, wdt = xb.shape
    xbuf[8:t + 8, cols] = xb
    xc = cb_ref[:, cols] + cw_ref[0:1, cols] * xbuf[5:t + 5, cols]
    for j in range(1, B_CONV):
        xc = xc + cw_ref[j:j + 1, cols] * xbuf[5 + j:t + 5 + j, cols]
    xbuf[0:8, cols] = xbuf[t:t + 8, cols]
    emit()

    r_cols, i_cols = [], []
    for c in range(wdt // LANES):
        blk = cols.start // LANES + c
        xcc = xc[:, c * LANES:(c + 1) * LANES].astype(BF16)
        r_cols.append(_dot(xcc, wr_ref[blk]))
        i_cols.append(_dot(xcc, wi_ref[blk]))
    emit()
    r = jax.nn.sigmoid(jnp.concatenate(r_cols, axis=1) + br_ref[:, cols])
    ig = jax.nn.sigmoid(jnp.concatenate(i_cols, axis=1) + bi_ref[:, cols])
    nl = -lam_ref[:, cols]
    softplus = jnp.maximum(nl, 0.0) + jnp.log(1.0 + jnp.exp(-jnp.abs(nl)))
    log_a = -B_C * r * softplus
    a = jnp.exp(log_a)
    om = 1.0 - a * a
    u = om * lax.rsqrt(jnp.maximum(om, 1e-30)) * (ig * xc)
    emit()

    a = a.reshape(t // 8, 8, wdt)
    u = u.reshape(t // 8, 8, wdt)
    r8 = lax.broadcasted_iota(jnp.int32, a.shape, 1)
    for d in (1, 2, 4):
        a_sh = pltpu.roll(a, d, 1)
        u_sh = pltpu.roll(u, d, 1)
        m = r8 >= d
        u = jnp.where(m, a * u_sh + u, u)
        a = jnp.where(m, a * a_sh, a)
    a_scr[:, cols] = a.reshape(t, wdt)
    u_scr[:, cols] = u.reshape(t, wdt)
    emit()

    def body(j, h):
        off = pl.multiple_of(j * 8, 8)
        hh = a_scr[pl.ds(off, 8), cols] * h + u_scr[pl.ds(off, 8), cols]
        h_scr[pl.ds(off, 8), cols] = hh
        return hh[7:8, :]

    h_last = lax.fori_loop(0, t // 8, body, hcar[0:1, cols], unroll=True)
    hcar[0:1, cols] = h_last
    emit()
    return h_scr[:, cols] * _silu(gb)


def _inproj_rglru_kernel(tiles_per_seq, x_ref, g_ref, w_ref, cw_ref, cb_ref, wr_ref, br_ref, wi_ref, bi_ref, lam_ref,
                         z_ref, ob_ref, xbuf, hcar, a_scr, u_scr, h_scr):
    @pl.when(pl.program_id(0) % tiles_per_seq == 0)
    def _():
        xbuf[0:8, :] = jnp.zeros((8, B_WIDTH), F32)
        hcar[...] = jnp.zeros((8, B_WIDTH), F32)

    x = x_ref[...]
    ms = jnp.mean(x * x, axis=-1, keepdims=True)
    xn = (x * lax.rsqrt(ms + EPS) * g_ref[...]).astype(BF16)
    half = B_WIDTH // 2
    proj = {}

    def project(name, hp):
        base = (B_WIDTH if name == "gb" else 0) + hp * half
        proj[name, hp] = _dot(xn, w_ref[:, base:base + half])

    def store_z(c):
        z_ref[:, c:c + 256] = _dot(xn, w_ref[:, EVEN_FUSED + c:EVEN_FUSED + c + 256]).astype(BF16)

    pending = [functools.partial(store_z, c) for c in range(0, z_ref.shape[-1], 256)]
    pending.insert(1, functools.partial(project, "xb", 1))
    pending.insert(3, functools.partial(project, "gb", 1))

    def emit():
        if pending:
            pending.pop(0)()

    project("xb", 0)
    project("gb", 0)
    for hp in range(2):
        cols = slice(hp * half, (hp + 1) * half)
        ob_ref[:, cols] = _rglru_half(cols, proj["xb", hp], proj["gb", hp], emit, cw_ref, cb_ref, wr_ref, br_ref,
                                      wi_ref, bi_ref, lam_ref, xbuf, hcar, a_scr, u_scr, h_scr).astype(BF16)
    while pending:
        emit()


def _inproj_rglru(x2d, gain, w_bf16, seq, conv_w, conv_b, wr_bd, b_r, wi_bd, b_i, lam, tm=512):
    n, d = x2d.shape
    nc = w_bf16.shape[1]
    row = lambda v: v.reshape(1, B_WIDTH)
    full = lambda shp: pl.BlockSpec(shp, lambda i: (0,) * len(shp))
    return pl.pallas_call(
        functools.partial(_inproj_rglru_kernel, seq // tm),
        grid=(n // tm,),
        in_specs=[pl.BlockSpec((tm, d), lambda i: (i, 0)), full((1, d)), full((d, nc)),
                  full((B_CONV, B_WIDTH)), full((1, B_WIDTH)), full((4, LANES, LANES)), full((1, B_WIDTH)),
                  full((4, LANES, LANES)), full((1, B_WIDTH)), full((1, B_WIDTH))],
        out_specs=[pl.BlockSpec((tm, nc - EVEN_FUSED), lambda i: (i, 0)), pl.BlockSpec((tm, B_WIDTH), lambda i: (i, 0))],
        out_shape=[jax.ShapeDtypeStruct((n, nc - EVEN_FUSED), BF16), jax.ShapeDtypeStruct((n, B_WIDTH), BF16)],
        scratch_shapes=[pltpu.VMEM((tm + 8, B_WIDTH), F32), pltpu.VMEM((8, B_WIDTH), F32),
                        pltpu.VMEM((tm, B_WIDTH), F32), pltpu.VMEM((tm, B_WIDTH), F32), pltpu.VMEM((tm, B_WIDTH), F32)],
        compiler_params=_cparams(("arbitrary",)),
        name="inproj_rglru",
    )(x2d, gain.reshape(1, d), w_bf16, conv_w, row(conv_b), wr_bd, row(b_r), wi_bd, row(b_i), row(lam))


def _hgrn_consts():
    c = C_CHUNK
    t = np.arange(c)[:, None]
    s = np.arange(c)[None, :]
    mats = [(s <= t)]
    masks = []
    hs = c // 2
    while hs >= 1:
        mid = (t // (2 * hs)) * 2 * hs + hs - 1
        mats.append(s <= mid)
        same = (t // (2 * hs)) == (s // (2 * hs))
        masks.append(same & ((t // hs) % 2 == 1) & ((s // hs) % 2 == 0))
        hs //= 2
    pairs = [np.concatenate(masks[n:n + 2], axis=1) for n in range(0, len(masks), 2)]
    return np.concatenate(mats, axis=0).astype(np.float32), np.stack(pairs).astype(np.float32)


def _hgrn_kernel(q_ref, f_ref, i_ref, g_ref, lb_ref, og_ref, mst_ref, msk_ref, o_ref, st_scr):
    @pl.when(pl.program_id(1) == 0)
    def _():
        st_scr[...] = jnp.zeros(st_scr.shape, F32)

    c = C_CHUNK
    npair = msk_ref.shape[0]
    p = lb_ref[...]
    pm = jnp.maximum(p[0:1, :], p[1:2, :])
    e0, e1 = jnp.exp(p[0:1, :] - pm), jnp.exp(p[1:2, :] - pm)
    lb = e1 / (e0 + e1)
    zeros = jnp.zeros((c, LANES), BF16)
    for ch in range(q_ref.shape[1] // c):
        rows = slice(ch * c, (ch + 1) * c)
        f = lb + (1.0 - lb) * jax.nn.sigmoid(f_ref[0, rows, :].astype(F32))
        g = jnp.log2(f)
        g_hi, g_lo = _split(g)
        b_all = _dot(mst_ref[0:c, :], g_hi) + _dot(mst_ref[0:c, :], g_lo)
        bm_all = _dot(mst_ref[c:, :], g_hi)
        kk_all = 1.0 - f
        qf_all = _silu(q_ref[0, rows, :].astype(F32))
        v_all = i_ref[0, rows, :].astype(F32)
        kk_bf, qf_bf = kk_all.astype(BF16), qf_all.astype(BF16)
        heads = range(C_HEADS)
        cols = [slice(h * LANES, (h + 1) * LANES) for h in heads]
        sts = [st_scr[h] for h in heads]
        os = [_dot_nt((qf_all[:, cols[h]] * jnp.exp2(b_all[:, cols[h]])).astype(BF16), sts[h].astype(BF16)) for h in heads]
        att = [[] for _ in heads]
        for n in range(npair):
            lhs, rhs = [], []
            for h in heads:
                b = b_all[:, cols[h]]
                qt, kt = [], []
                for l in (2 * n, 2 * n + 1):
                    bm = bm_all[l * c:(l + 1) * c, cols[h]]
                    qt.append(qf_bf[:, cols[h]] * jnp.exp2(jnp.minimum(b - bm, 1.0)).astype(BF16))
                    kt.append(kk_bf[:, cols[h]] * jnp.exp2(jnp.minimum(bm - b, 1.0)).astype(BF16))
                lhs.append(jnp.concatenate(qt, axis=1))
                rhs.append(jnp.concatenate([jnp.concatenate([kt[0], zeros], axis=1),
                                            jnp.concatenate([zeros, kt[1]], axis=1)], axis=0))
            prods = [_dot_nt(lhs[h], rhs[h]) for h in heads]
            for h in heads:
                att[h].append(jnp.where(msk_ref[n] > 0.5, prods[h], 0.0).astype(BF16))
        vbs = [v_all[:, cols[h]].astype(BF16) for h in heads]
        intra = [_dot(jnp.concatenate(att[h], axis=1), jnp.concatenate([vbs[h]] * (2 * npair), axis=0)) for h in heads]
        b_last = b_all[c - 1:c, :]
        kd = (kk_all * jnp.exp2(b_last - b_all)).astype(BF16)
        upd = [_dot(v_all[:, cols[h]].T.astype(BF16), kd[:, cols[h]]) for h in heads]
        decay = jnp.exp2(b_last)
        diag = qf_all * kk_all
        outs = []
        for h in heads:
            st_scr[h] = sts[h] * decay[:, cols[h]] + upd[h]
            o = os[h] + intra[h] + jnp.sum(diag[:, cols[h]], axis=-1, keepdims=True) * v_all[:, cols[h]]
            ms = jnp.mean(o * o, axis=-1, keepdims=True)
            outs.append(o * lax.rsqrt(ms + EPS) * og_ref[...])
        o_ref[0, rows, :] = (jnp.concatenate(outs, axis=1) * _silu(g_ref[0, rows, :].astype(F32))).astype(BF16)


def _hgrn(z, c_lb, c_og, t=512):
    b, s, _ = z.shape
    mst, msk = _hgrn_consts()
    blk = lambda name: pl.BlockSpec((1, t, 512), lambda bi, i, o=ODD_OFF[name][0] // 512: (bi, i, o))
    full = lambda shp: pl.BlockSpec(shp, lambda bi, i: (0,) * len(shp))
    return pl.pallas_call(
        _hgrn_kernel,
        grid=(b, s // t),
        in_specs=[blk("qc"), blk("fc"), blk("ic"), blk("gc"), full(c_lb.shape), full((1, C_HEAD_DIM)),
                  full(mst.shape), full(msk.shape)],
        out_specs=pl.BlockSpec((1, t, 512), lambda bi, i: (bi, i, 0)),
        out_shape=jax.ShapeDtypeStruct((b, s, 512), BF16),
        scratch_shapes=[pltpu.VMEM((C_HEADS, C_HEAD_DIM, C_HEAD_DIM), F32)],
        compiler_params=_cparams(("arbitrary", "arbitrary")),
        name="hgrn2",
    )(z, z, z, z, c_lb, c_og.reshape(1, C_HEAD_DIM), jnp.asarray(mst, BF16), jnp.asarray(msk, F32))


def _nsa_prep_kernel(kcr_ref, vcr_ref, ks_ref, vs_ref, kw_ref, vw_ref, cos_ref, sin_ref, cosc_ref, sinc_ref,
                     kncmp_ref, knslc_ref, knwin_ref, pek_ref, pev_ref, w1k_ref, w2k_ref, w1v_ref, w2vt_ref,
                     kc_o, vct_o, ks_o, vst_o, kw_o, vwt_o):
    def hidden(xr_ref, pe_ref, w1_ref):
        xr = xr_ref[0].astype(F32)
        top = _dot((xr + pe_ref[0:1, :]).astype(BF16), w1_ref[0])
        bot = _dot((xr + pe_ref[1:2, :]).astype(BF16), w1_ref[1])
        pre = top + pltpu.roll(bot, bot.shape[0] - 1, 0)
        return _silu(pre).astype(BF16)

    kc = _dot(hidden(kcr_ref, pek_ref, w1k_ref), w2k_ref[...])
    kc = _rope(_head_rms(kc, kncmp_ref[...]), cosc_ref[...], sinc_ref[...])
    d0, d1 = _split_kv(kc)
    kc_o[0, 0] = d0.astype(BF16)
    kc_o[0, 1] = d1.astype(BF16)
    vct = _dot_nt(w2vt_ref[...], hidden(vcr_ref, pev_ref, w1v_ref))
    for kv in range(D_KV_HEADS):
        vct_o[0, kv] = _with_ones_row(vct[kv * HEAD_DIM:(kv + 1) * HEAD_DIM, :]).astype(BF16)

    s = ks_ref.shape[1]
    step = 256
    for r0 in range(0, s, step):
        rows = slice(r0, r0 + step)
        cos, sin = cos_ref[rows, :], sin_ref[rows, :]
        ks = _rope(_head_rms(ks_ref[0, rows, :].astype(F32), knslc_ref[...]), cos, sin)
        kw = _rope(_head_rms(kw_ref[0, rows, :].astype(F32), knwin_ref[...]), cos, sin)
        blk = (r0 + _row((step, LANES))) // SEL_LEN
        onehot = jnp.where(_lane((step, LANES)) - HEAD_DIM == blk, MASK_BIG, 0.0)
        for src, dst, fill in ((ks, ks_o, onehot), (kw, kw_o, 0.0)):
            d0, d1 = _split_kv(src, fill)
            dst[0, 0, rows, :] = d0.astype(BF16)
            dst[0, 1, rows, :] = d1.astype(BF16)
        for src, dst in ((vs_ref, vst_o), (vw_ref, vwt_o)):
            vt = src[0, rows, :].astype(F32).T
            for kv in range(D_KV_HEADS):
                dst[0, kv, :, rows] = _with_ones_row(vt[kv * HEAD_DIM:(kv + 1) * HEAD_DIM, :]).astype(BF16)


def _nsa_prep(z, kcr, vcr, cos, sin_s, cosc, sinc, kn_cmp, kn_slc, kn_win, pek, pev, w1k, w2k, w1v, w2v):
    b, s, _ = z.shape
    nseg = s // CMP_STRIDE
    zb = lambda name: pl.BlockSpec((1, s, 128), lambda bi, o=ODD_OFF[name][0] // 128: (bi, 0, o))
    full = lambda shp: pl.BlockSpec(shp, lambda bi: (0,) * len(shp))
    seg = pl.BlockSpec((1, nseg, 2048), lambda bi: (bi, 0, 0))
    g2 = lambda g: jnp.tile(g, 2).reshape(1, 128)
    dup = lambda n: (jax.ShapeDtypeStruct((b, 2, n, 128), BF16), pl.BlockSpec((1, 2, n, 128), lambda bi: (bi, 0, 0, 0)))
    tr = lambda n: (jax.ShapeDtypeStruct((b, 2, VT_ROWS, n), BF16),
                    pl.BlockSpec((1, 2, VT_ROWS, n), lambda bi: (bi, 0, 0, 0)))
    outs = [dup(nseg), tr(nseg), dup(s), tr(s), dup(s), tr(s)]
    return pl.pallas_call(
        _nsa_prep_kernel,
        grid=(b,),
        in_specs=[seg, seg, zb("ksd"), zb("vsd"), zb("kwd"), zb("vwd"),
                  full((s, 128)), full((s, 128)), full((nseg, 128)), full((nseg, 128)),
                  full((1, 128)), full((1, 128)), full((1, 128)), full((2, 2048)), full((2, 2048)),
                  full((2, 2048, 256)), full((256, 128)), full((2, 2048, 256)), full((128, 256))],
        out_specs=[o[1] for o in outs],
        out_shape=[o[0] for o in outs],
        compiler_params=_cparams(("arbitrary",)),
        name="nsa_prep",
    )(kcr, vcr, z, z, z, z, cos, sin_s, cosc, sinc, g2(kn_cmp), g2(kn_slc), g2(kn_win), pek, pev, w1k, w2k, w1v, w2v)


SEL_CHUNK = 512
SEL_PIECE = 512


def _nsa_kernel(bound_ref, *refs):
    small = jnp.maximum(jnp.maximum(bound_ref[0], bound_ref[1]), bound_ref[2]) <= MAX_SAFE_BOUND

    @pl.when(small)
    def _():
        _nsa_body((bound_ref[0], bound_ref[1], bound_ref[2]), *refs)

    @pl.when(jnp.logical_not(small))
    def _():
        _nsa_body((None, None, None), *refs)


def _nsa_body(fixed, q_ref, gd_ref, gate_ref, cos_ref, sin_ref, qn_ref, kc_ref, vct_ref, ks_ref, vst_ref, kw_ref, vwt_ref,
              ovlt_ref, o_ref):
    fix_c, fix_s, fix_w = fixed
    i = pl.program_id(1)
    t = q_ref.shape[1]
    grp = D_HEADS // D_KV_HEADS
    w = grp * t
    kvs = range(D_KV_HEADS)
    q = _rope(_head_rms(q_ref[0].astype(F32), qn_ref[...]), cos_ref[...], sin_ref[...]) * Q_SCALE
    stack = lambda heads: [jnp.concatenate(heads[kv * grp:(kv + 1) * grp], axis=0) for kv in kvs]
    qs = stack(_q_heads(q))
    tq1 = i * t + _lane((1, t))

    nwin = D_WINDOW + t
    win0 = pl.multiple_of(jnp.maximum(i * t + t - nwin, 0), t)
    s_cmp = [_dot_nt(kc_ref[0, kv], qs[kv]) for kv in kvs]
    s_win = [_dot_nt(kw_ref[0, kv, pl.ds(win0, nwin), :], qs[kv]) for kv in kvs]

    ncmp = kc_ref.shape[2] - 1
    nsel = ovlt_ref.shape[0]
    crow = _row((kc_ref.shape[2], t))
    bias_c = _tile4(jnp.where((crow * CMP_STRIDE + (CMP_LEN - 1) <= tq1) & (crow < ncmp), 0.0, NEG_INF))
    row_ok = _tile4(tq1 >= CMP_LEN - 1)
    es = [_exp_cols(s_cmp[kv] + bias_c, fixed=fix_c)[0] for kv in kvs]
    accs = [_dot(vct_ref[0, kv], es[kv]) for kv in kvs]
    invs = [jnp.where(row_ok, 1.0 / accs[kv][HEAD_DIM:HEAD_DIM + 1, :], 0.0) for kv in kvs]
    oc_t = [accs[kv][0:HEAD_DIM, :] * invs[kv] for kv in kvs]
    psums = []
    for kv in kvs:
        p = es[kv].astype(F32) * invs[kv]
        psums.append(p[:, 0:t] + p[:, t:2 * t] + p[:, 2 * t:3 * t] + p[:, 3 * t:4 * t])
    imps = [_dot_split_rhs(ovlt_ref[...], psums[kv]) for kv in kvs]

    rel = tq1 - (win0 + _row((nwin, t)))
    bias_w = _tile4(jnp.where((rel >= 0) & (rel < D_WINDOW), 0.0, NEG_INF))
    ow_t = [_finish(_dot(vwt_ref[0, kv, :, pl.ds(win0, nwin)], _exp_cols(s_win[kv] + bias_w, fixed=fix_w)[0]))
            for kv in kvs]

    jrow = _row((nsel, t))
    jrow_f = jrow.astype(F32)
    cur = (i * t + _lane((nsel, t))) // SEL_LEN
    forced = (jrow == 0) | (jrow == cur)
    fills = []
    for kv in kvs:
        score = jnp.where(forced, POS_INF, jnp.where(jrow <= cur, imps[kv], NEG_INF))
        chosen = jnp.zeros((nsel, t), F32)
        for _ in range(min(SEL_TOPK, nsel)):
            mx = jnp.max(score, axis=0, keepdims=True)
            first = jnp.min(jnp.where(score == mx, jrow_f, 1e9), axis=0, keepdims=True)
            hit = jrow_f == first
            chosen = jnp.where(hit, 1.0, chosen)
            score = jnp.where(hit, -3e38, score)
        frame = jnp.concatenate([jnp.zeros((HEAD_DIM, t), F32), chosen - 1.0,
                                 jnp.zeros((LANES - HEAD_DIM - nsel, t), F32)], axis=0)
        fills.extend([frame.T] * (grp // 2))

    init = (jnp.full((1, w), NEG_INF, F32), jnp.zeros((VT_ROWS, w), F32))
    pc = SEL_PIECE

    def sweep(states, k_ref, vt_ref, queries, offs, biases, n):
        scores = [[_dot_nt(k_ref[0, kv, pl.ds(off, n), :], queries[kv]) for kv in kvs] for off in offs]
        for off, sc, bias in zip(offs, scores, biases):
            states = tuple(_online_update(states[kv], sc[kv] if bias is None else sc[kv] + bias,
                                          vt_ref[0, kv, :, pl.ds(off, n)], fix_s) for kv in kvs)
        return states

    qsel = stack(_q_heads(q, fills))
    ch = SEL_CHUNK
    own = pl.multiple_of(((i * t) // ch) * ch, ch)
    st = sweep((init, init), ks_ref, vst_ref, qsel, [own],
               [_tile4(jnp.where(own + _row((ch, t)) <= tq1, 0.0, NEG_INF))], ch)

    def sel_chunk(c, states):
        base = pl.multiple_of(c * ch, ch)
        return sweep(states, ks_ref, vst_ref, qsel, [base + n * pc for n in range(ch // pc)], [None] * (ch // pc), pc)

    st = lax.fori_loop(0, (i * t) // ch, sel_chunk, st)
    os_t = [_finish(acc) for (_, acc) in st]

    g_t = jax.nn.sigmoid(gate_ref[0].astype(F32)).T
    rows = []
    for h in range(D_HEADS):
        kv, g = divmod(h, grp)
        cols = slice(g * t, (g + 1) * t)
        r = D_BRANCHES * h
        rows.append(g_t[r:r + 1, :] * oc_t[kv][:, cols] + g_t[r + 1:r + 2, :] * os_t[kv][:, cols]
                    + g_t[r + 2:r + 3, :] * ow_t[kv][:, cols])
    o = jnp.concatenate(rows, axis=0).T
    o_ref[0] = (o * _silu(gd_ref[0].astype(F32))).astype(BF16)


def _overlap_t(s):
    ncmp = (s - CMP_LEN) // CMP_STRIDE + 1
    nsel = s // SEL_LEN
    cs = np.arange(ncmp)[None, :] * CMP_STRIDE
    ss = np.arange(nsel)[:, None] * SEL_LEN
    ovl = np.zeros((nsel, s // CMP_STRIDE), np.float32)
    ovl[:, :ncmp] = (cs < ss + SEL_LEN) & (cs + CMP_LEN > ss)
    return ovl


def _nsa(z, cos, sin_s, qn, bounds, kc, vct, ks, vst, kw, vwt, t=256):
    b, s, _ = z.shape
    nseg = s // CMP_STRIDE
    assert nseg == LANES, "compressed-block scores are laid out on one 128-row tile"
    ovlt = _overlap_t(s)
    full = lambda shp: pl.BlockSpec(shp, lambda bi, i: (0,) * len(shp))
    dup = lambda n: pl.BlockSpec((1, 2, n, 128), lambda bi, i: (bi, 0, 0, 0))
    tr = lambda n: pl.BlockSpec((1, 2, VT_ROWS, n), lambda bi, i: (bi, 0, 0, 0))
    return pl.pallas_call(
        _nsa_kernel,
        grid=(b, s // t),
        in_specs=[pl.BlockSpec(memory_space=pltpu.SMEM),
                  pl.BlockSpec((1, t, 512), lambda bi, i: (bi, i, ODD_OFF["qd"][0] // 512)),
                  pl.BlockSpec((1, t, 512), lambda bi, i: (bi, i, ODD_OFF["gd"][0] // 512)),
                  pl.BlockSpec((1, t, 128), lambda bi, i: (bi, i, ODD_OFF["gate"][0] // 128)),
                  pl.BlockSpec((t, 128), lambda bi, i: (i, 0)),
                  pl.BlockSpec((t, 128), lambda bi, i: (i, 0)),
                  full((1, 512)), dup(nseg), tr(nseg), dup(s), tr(s), dup(s), tr(s), full(ovlt.shape)],
        out_specs=pl.BlockSpec((1, t, 512), lambda bi, i: (bi, i, 0)),
        out_shape=jax.ShapeDtypeStruct((b, s, 512), BF16),
        compiler_params=_cparams(("arbitrary", "arbitrary")),
        name="nsa",
    )(bounds, z, z, z, cos, sin_s, jnp.tile(qn, 8).reshape(1, 512), kc, vct, ks, vst, kw, vwt, jnp.asarray(ovlt, BF16))


def _permute_cols(w, order, src):
    cols = []
    for name, width in order:
        o, sw = src[name]
        blk = w[:, o:o + sw]
        if sw < width:
            blk = jnp.pad(blk, ((0, 0), (0, width - sw)))
        cols.append(blk)
    return jnp.concatenate(cols, axis=1).astype(BF16)


def _block_diag_pairs(w):
    z = jnp.zeros((4, LANES, LANES), w.dtype)
    z = z.at[:, 0:64, 0:64].set(w[0::2])
    z = z.at[:, 64:128, 64:128].set(w[1::2])
    return z.astype(BF16)


def _rope_tables(pos):
    half = HEAD_DIM // 2
    inv = ROPE_THETA ** (-jnp.arange(half, dtype=F32) / half)
    ang = pos.astype(F32)[:, None] * inv[None, :]
    cos, sin = jnp.cos(ang), jnp.sin(ang)
    cos_t = jnp.tile(cos, (1, 4))
    sin_t = jnp.tile(jnp.concatenate([-sin, sin], axis=1), (1, 2))
    return cos_t, sin_t


def _expand_compress_w1(w1):
    hdim = w1.shape[1]
    w = w1.reshape(2, CMP_STRIDE, 1, HEAD_DIM, 1, hdim)
    same_head = jnp.eye(2, dtype=w1.dtype).reshape(1, 1, 2, 1, 2, 1)
    return (w * same_head).reshape(2, CMP_STRIDE * 2 * HEAD_DIM, 2 * hdim).astype(BF16)


def _expand_compress_w2(w2):
    hdim, hd = w2.shape
    z = jnp.zeros((2 * hdim, 2 * hd), w2.dtype)
    z = z.at[0:hdim, 0:hd].set(w2)
    z = z.at[hdim:, hd:].set(w2)
    return z.astype(BF16)


def _expand_pe(pe):
    p = pe.reshape(2, CMP_STRIDE, 1, HEAD_DIM)
    return jnp.broadcast_to(p, (2, CMP_STRIDE, 2, HEAD_DIM)).reshape(2, CMP_STRIDE * 2 * HEAD_DIM)


def _even_layer(h, mem, g, mem_g, w_mem_kv, m_qn, m_kn, w_in, w_out, a_qn, a_kn, a_sinks,
                conv_w, conv_b, w_r, b_r, w_i, b_i, lam, cos, sin_s):
    b, s, d = h.shape
    h2 = h.reshape(b * s, d)
    z, ob = _inproj_rglru(h2, g, _permute_cols(w_in, EVEN_ORDER, EVEN_SRC), s, conv_w, conv_b,
                          _block_diag_pairs(w_r), b_r, _block_diag_pairs(w_i), b_i, lam)
    z = z.reshape(b, s, EVEN_COLS)
    oa = _swa(z, cos, sin_s, a_qn, a_kn, a_sinks)
    om = _mem_attention(z, EVEN_OFF["qm"][0], EVEN_OFF["gm"][0], mem, mem_g, w_mem_kv.astype(BF16), m_qn, m_kn)
    out = _outproj(h2, oa.reshape(b * s, 512), ob, om.reshape(b * s, 256), w_out.astype(BF16))
    return out.reshape(b, s, d)


def _odd_layer(h, mem, g, mem_g, w_mem_kv, m_qn, m_kn, w_in, w_out, c_lb, c_og,
               d_qn, d_kn_cmp, d_kn_slc, d_kn_win, pe_k, pe_v, w1k, w2k, w1v, w2v, cos, sin_s, cosc, sinc):
    b, s, d = h.shape
    h2 = h.reshape(b * s, d)
    z, kcd, vcd = _inproj(h2, g, _permute_cols(w_in, ODD_ORDER, ODD_SRC), ODD_SPLIT)
    z = z.reshape(b, s, ODD_SPLIT[0])
    oc = _hgrn(z, c_lb, c_og)
    nseg = s // CMP_STRIDE
    seg = lambda a: a.reshape(b, nseg, CMP_STRIDE * 128)
    kc, vc, ks, vs, kw, vw = _nsa_prep(
        z, seg(kcd), seg(vcd), cos, sin_s, cosc, sinc, d_kn_cmp, d_kn_slc, d_kn_win,
        _expand_pe(pe_k), _expand_pe(pe_v), _expand_compress_w1(w1k), _expand_compress_w2(w2k),
        _expand_compress_w1(w1v), _expand_compress_w2(w2v).T)
    bounds = jnp.stack([_score_bound(d_qn, kn) for kn in (d_kn_cmp, d_kn_slc, d_kn_win)]).astype(F32)
    od = _nsa(z, cos, sin_s, d_qn, bounds, kc, vc, ks, vs, kw, vw)
    om = _mem_attention(z, ODD_OFF["qm"][0], ODD_OFF["gm"][0], mem, mem_g, w_mem_kv.astype(BF16), m_qn, m_kn)
    out = _outproj(h2, oc.reshape(b * s, 512), od.reshape(b * s, 512), om.reshape(b * s, 256), w_out.astype(BF16))
    return out.reshape(b, s, d)


def kernel(x, mem, norm_g, mem_norm_g, mem_w_kv, mem_qn, mem_kn, ev_w_in, ev_w_out, a_qn, a_kn, a_sinks,
           b_conv_w, b_conv_b, b_w_r, b_b_r, b_w_i, b_b_i, b_lambda, od_w_in, od_w_out, c_lb, c_onorm,
           d_qn, d_kn_cmp, d_kn_slc, d_kn_win, d_pe_k, d_pe_v, d_w1k, d_w2k, d_w1v, d_w2v):
    depth = norm_g.shape[0]
    assert depth == 2 and c_lb.shape[0] == 2, "the HGRN2 lower-bound formula in the kernel is written for depth 2"
    s = x.shape[1]
    assert s % 256 == 0 and s >= D_WINDOW
    pos = jnp.arange(s)
    cos, sin_s = _rope_tables(pos)
    nseg = s // CMP_STRIDE
    cmp_end = jnp.minimum(jnp.arange(nseg) * CMP_STRIDE + CMP_LEN - 1, s - 1)
    cosc, sinc = _rope_tables(cmp_end)
    h = _even_layer(x, mem, norm_g[0], mem_norm_g[0], mem_w_kv[0], mem_qn[0], mem_kn[0], ev_w_in[0], ev_w_out[0],
                    a_qn[0], a_kn[0], a_sinks[0], b_conv_w[0], b_conv_b[0], b_w_r[0], b_b_r[0], b_w_i[0], b_b_i[0],
                    b_lambda[0], cos, sin_s)
    h = _odd_layer(h, mem, norm_g[1], mem_norm_g[1], mem_w_kv[1], mem_qn[1], mem_kn[1], od_w_in[0], od_w_out[0],
                   c_lb, c_onorm[0], d_qn[0], d_kn_cmp[0], d_kn_slc[0], d_kn_win[0], d_pe_k[0], d_pe_v[0],
                   d_w1k[0], d_w2k[0], d_w1v[0], d_w2v[0], cos, sin_s, cosc, sinc)
    return h
```

```python
import functools

import numpy as np
import jax
import jax.numpy as jnp
from jax import lax
from jax.experimental import pallas as pl
from jax.experimental.pallas import tpu as pltpu

F32 = jnp.float32
BF16 = jnp.bfloat16

D_MODEL = 1024
N_MEM = 256
HEAD_DIM = 64
ROPE_THETA = 10000.0
EPS = 1e-6
NEG_INF = -1e30
POS_INF = 1e30
MASK_BIG = 1e30
LANES = 128

A_HEADS, A_KV_HEADS, A_WINDOW = 8, 2, 128
B_WIDTH, B_BLOCKS, B_CONV, B_C = 512, 8, 4, 8.0
M_HEADS = 4
C_HEADS, C_HEAD_DIM, C_CHUNK = 4, 128, 64
C_WIDTH = C_HEADS * C_HEAD_DIM
D_HEADS, D_KV_HEADS = 8, 2
CMP_LEN, CMP_STRIDE, CMP_HIDDEN = 32, 16, 128
SEL_LEN, SEL_TOPK = 64, 4
D_WINDOW = 512
D_BRANCHES = 3
SCALE = HEAD_DIM ** -0.5
LOG2E = 1.4426950408889634
Q_SCALE = SCALE * LOG2E

EVEN_ORDER = [("xb", 512), ("gb", 512), ("qa", 512), ("ga", 512), ("qm", 256), ("gm", 256), ("ka", 128), ("va", 128)]
EVEN_FUSED = 1024
EVEN_SRC = {"qa": (0, 512), "ka": (512, 128), "va": (640, 128), "ga": (768, 512), "xb": (1280, 512),
            "gb": (1792, 512), "qm": (2304, 256), "gm": (2560, 256)}
ODD_ORDER = [("qc", 512), ("fc", 512), ("ic", 512), ("gc", 512), ("qd", 512), ("gd", 512), ("qm", 256), ("gm", 256),
             ("ksd", 128), ("vsd", 128), ("kwd", 128), ("vwd", 128), ("gate", 128), ("kcd", 128), ("vcd", 128)]
ODD_FUSED = 2048
ODD_SPLIT = (2176, 128, 128)
ODD_SRC = {"qc": (0, 512), "fc": (512, 512), "ic": (1024, 512), "gc": (1536, 512), "qd": (2048, 512),
           "kcd": (2560, 128), "vcd": (2688, 128), "ksd": (2816, 128), "vsd": (2944, 128), "kwd": (3072, 128),
           "vwd": (3200, 128), "gate": (3328, 24), "gd": (3352, 512), "qm": (3864, 256), "gm": (4120, 256)}

VMEM_LIMIT = 48 * 1024 * 1024


def _offsets(order):
    off, out = 0, {}
    for name, w in order:
        out[name] = (off, w)
        off += w
    return out, off


EVEN_OFF, EVEN_COLS = _offsets(EVEN_ORDER[2:])
ODD_OFF, ODD_COLS = _offsets(ODD_ORDER[4:])


def _cparams(sem):
    return pltpu.CompilerParams(dimension_semantics=sem, vmem_limit_bytes=VMEM_LIMIT)


def _dot(a, b):
    return jnp.dot(a, b, preferred_element_type=F32)


def _dot_nt(a, b):
    return lax.dot_general(a, b, (((1,), (1,)), ((), ())), preferred_element_type=F32)


def _split(x):
    hi = x.astype(BF16)
    lo = (x - hi.astype(F32)).astype(BF16)
    return hi, lo


def _dot_split_lhs(x, m):
    hi, lo = _split(x)
    return _dot(hi, m) + _dot(lo, m)


def _dot_split_rhs(m, x):
    hi, lo = _split(x)
    return _dot(m, hi) + _dot(m, lo)


def _lane(shape):
    return lax.broadcasted_iota(jnp.int32, shape, len(shape) - 1)


def _row(shape):
    return lax.broadcasted_iota(jnp.int32, shape, len(shape) - 2)


def _silu(x):
    return x * jax.nn.sigmoid(x)


def _seg_ones():
    r = lax.broadcasted_iota(jnp.int32, (LANES, LANES), 0) >> 6
    c = lax.broadcasted_iota(jnp.int32, (LANES, LANES), 1) >> 6
    return jnp.where(r == c, 1.0, 0.0).astype(BF16)


def _head_rms(x, gain):
    seg = _seg_ones()
    cols = []
    for c in range(x.shape[1] // LANES):
        xc = x[:, c * LANES:(c + 1) * LANES]
        ms = _dot_split_lhs(xc * xc, seg) * (1.0 / HEAD_DIM)
        cols.append(xc * lax.rsqrt(ms + EPS))
    y = cols[0] if len(cols) == 1 else jnp.concatenate(cols, axis=1)
    return y * gain


def _rope(x, cos, sin_s):
    first = (_lane((x.shape[0], LANES)) & 63) < 32
    cols = []
    for c in range(x.shape[1] // LANES):
        xc = x[:, c * LANES:(c + 1) * LANES]
        sw = jnp.where(first, pltpu.roll(xc, 96, 1), pltpu.roll(xc, 32, 1))
        cols.append(xc * cos + sw * sin_s)
    return cols[0] if len(cols) == 1 else jnp.concatenate(cols, axis=1)


def _split_kv(k, fill=0.0):
    lo = _lane(k.shape) < 64
    return jnp.where(lo, k, fill), jnp.where(lo, pltpu.roll(k, 64, 1), fill)


def _q_heads(q, fills=None):
    lo = _lane((q.shape[0], LANES)) < 64
    out = []
    for c in range(q.shape[1] // LANES):
        qc = q[:, c * LANES:(c + 1) * LANES]
        fill = 0.0 if fills is None else fills[c]
        out.append(jnp.where(lo, qc, fill).astype(BF16))
        out.append(jnp.where(lo, pltpu.roll(qc, 64, 1), fill).astype(BF16))
    return out


VT_ROWS = HEAD_DIM + 16


def _with_ones_row(vt):
    n = vt.shape[1]
    pad = jnp.where(_row((VT_ROWS - HEAD_DIM, n)) == 0, 1.0, 0.0)
    return jnp.concatenate([vt, pad], axis=0)


MAX_SAFE_BOUND = 60.0


def _score_bound(q_gain, k_gain):
    return 1.02 * HEAD_DIM * SCALE * LOG2E * jnp.max(jnp.abs(q_gain)) * jnp.max(jnp.abs(k_gain))


def _with_fixed_reference(bound, body):
    small = bound <= MAX_SAFE_BOUND

    @pl.when(small)
    def _():
        body(bound)

    @pl.when(jnp.logical_not(small))
    def _():
        body(None)


def _exp_cols(s, extra=None, fixed=None):
    if fixed is not None:
        return jnp.exp2(s - fixed).astype(BF16), fixed
    m = jnp.max(s, axis=0, keepdims=True)
    if extra is not None:
        m = jnp.maximum(m, extra)
    return jnp.exp2(s - m).astype(BF16), m


def _finish(acc, extra_den=None):
    den = acc[HEAD_DIM:HEAD_DIM + 1, :]
    if extra_den is not None:
        den = den + extra_den
    return acc[0:HEAD_DIM, :] * (1.0 / den)


def _online_update(state, s, vt_tile, fixed=None):
    m, acc = state
    if fixed is not None:
        return m, acc + _dot(vt_tile, _exp_cols(s, fixed=fixed)[0])
    p, m_new = _exp_cols(s, m)
    acc = jnp.exp2(m - m_new) * acc + _dot(vt_tile, p)
    return m_new, acc


def _tile4(x):
    return jnp.concatenate([x, x, x, x], axis=1)


def _outproj_kernel(h_ref, a_ref, b_ref, m_ref, w_ref, o_ref):
    a, b, m = a_ref[...], b_ref[...], m_ref[...]
    for c in range(0, D_MODEL, 256):
        acc = h_ref[:, c:c + 256]
        acc += _dot(a, w_ref[0:512, c:c + 256])
        acc += _dot(b, w_ref[512:1024, c:c + 256])
        acc += _dot(m, w_ref[1024:1280, c:c + 256])
        o_ref[:, c:c + 256] = acc


def _outproj(h2d, oa, ob, om, w_bf16, tm=1024):
    n, d = h2d.shape
    return pl.pallas_call(
        _outproj_kernel,
        grid=(n // tm,),
        in_specs=[pl.BlockSpec((tm, d), lambda i: (i, 0)),
                  pl.BlockSpec((tm, 512), lambda i: (i, 0)),
                  pl.BlockSpec((tm, 512), lambda i: (i, 0)),
                  pl.BlockSpec((tm, 256), lambda i: (i, 0)),
                  pl.BlockSpec((1280, d), lambda i: (0, 0))],
        out_specs=pl.BlockSpec((tm, d), lambda i: (i, 0)),
        out_shape=jax.ShapeDtypeStruct((n, d), F32),
        compiler_params=_cparams(("arbitrary",)),
        name="outproj",
    )(h2d, oa, ob, om, w_bf16)


def _mem_kernel(bound_ref, q_ref, gm_ref, mem_ref, mg_ref, wkv_ref, qn_ref, kn_ref, o_ref, k_scr, vt_scr):
    @pl.when(pl.program_id(1) == 0)
    def _():
        m = mem_ref[0]
        ms = jnp.mean(m * m, axis=-1, keepdims=True)
        mn = (m * lax.rsqrt(ms + EPS) * mg_ref[...]).astype(BF16)
        kv = _dot(mn, wkv_ref[...])
        km = _head_rms(kv[:, 0:256], kn_ref[...])
        for c in range(2):
            k0, k1 = _split_kv(km[:, c * LANES:(c + 1) * LANES])
            k_scr[2 * c] = k0.astype(BF16)
            k_scr[2 * c + 1] = k1.astype(BF16)
        vt = kv[:, 256:512].T
        for h in range(M_HEADS):
            vt_scr[h] = _with_ones_row(vt[h * HEAD_DIM:(h + 1) * HEAD_DIM, :]).astype(BF16)

    def body(fixed):
        q = _head_rms(q_ref[0].astype(F32), qn_ref[...]) * Q_SCALE
        scores = [_dot_nt(k_scr[h], qh) for h, qh in enumerate(_q_heads(q))]
        rows = [_finish(_dot(vt_scr[h], _exp_cols(sc, fixed=fixed)[0])) for h, sc in enumerate(scores)]
        o = jnp.concatenate(rows, axis=0).T
        o_ref[0] = (o * _silu(gm_ref[0].astype(F32))).astype(BF16)

    _with_fixed_reference(bound_ref[0], body)


def _mem_attention(z, q_off, g_off, mem, mem_g, wkv_bf16, qn, kn, tq=1024):
    b, s, _ = z.shape
    return pl.pallas_call(
        _mem_kernel,
        grid=(b, s // tq),
        in_specs=[pl.BlockSpec(memory_space=pltpu.SMEM),
                  pl.BlockSpec((1, tq, 256), lambda bi, i: (bi, i, q_off // 256)),
                  pl.BlockSpec((1, tq, 256), lambda bi, i: (bi, i, g_off // 256)),
                  pl.BlockSpec((1, N_MEM, D_MODEL), lambda bi, i: (bi, 0, 0)),
                  pl.BlockSpec((1, D_MODEL), lambda bi, i: (0, 0)),
                  pl.BlockSpec((D_MODEL, 512), lambda bi, i: (0, 0)),
                  pl.BlockSpec((1, 256), lambda bi, i: (0, 0)),
                  pl.BlockSpec((1, 256), lambda bi, i: (0, 0))],
        out_specs=pl.BlockSpec((1, tq, 256), lambda bi, i: (bi, i, 0)),
        out_shape=jax.ShapeDtypeStruct((b, s, 256), BF16),
        scratch_shapes=[pltpu.VMEM((M_HEADS, N_MEM, LANES), BF16), pltpu.VMEM((M_HEADS, VT_ROWS, N_MEM), BF16)],
        compiler_params=_cparams(("arbitrary", "arbitrary")),
        name="mem_attention",
    )(_score_bound(qn, kn).reshape(1).astype(F32), z, z, mem, mem_g.reshape(1, D_MODEL), wkv_bf16,
      jnp.tile(qn, 4).reshape(1, 256), jnp.tile(kn, 4).reshape(1, 256))


def _swa_kernel(bound_ref, *refs):
    _with_fixed_reference(bound_ref[0], functools.partial(_swa_body, refs))


def _swa_body(refs, fixed):
    (q_ref, g_ref, kc_ref, kp_ref, vc_ref, vp_ref, cosc_ref, sinc_ref, cosp_ref, sinp_ref,
     qn_ref, kn_ref, sink_ref, o_ref) = refs
    i = pl.program_id(1)
    t = q_ref.shape[1]
    tp = kp_ref.shape[1]
    grp = A_HEADS // A_KV_HEADS
    q = _rope(_head_rms(q_ref[0].astype(F32), qn_ref[...]), cosc_ref[...], sinc_ref[...]) * Q_SCALE
    qh = _q_heads(q)
    kc = _rope(_head_rms(kc_ref[0].astype(F32), kn_ref[...]), cosc_ref[...], sinc_ref[...])
    kp = _rope(_head_rms(kp_ref[0].astype(F32), kn_ref[...]), cosp_ref[...], sinp_ref[...])
    ks = [x.astype(BF16) for x in _split_kv(jnp.concatenate([kp, kc], axis=0))]
    vt = jnp.concatenate([vp_ref[0].astype(F32).T, vc_ref[0].astype(F32).T], axis=1)
    ts = i * t - tp + _row((tp + t, t))
    rel = i * t + _lane((tp + t, t)) - ts
    bias = _tile4(jnp.where((rel >= 0) & (rel < A_WINDOW) & (ts >= 0), 0.0, NEG_INF))
    scores = [_dot_nt(ks[kv], jnp.concatenate(qh[kv * grp:(kv + 1) * grp], axis=0)) for kv in range(A_KV_HEADS)]
    rows = []
    for kv in range(A_KV_HEADS):
        sink = jnp.concatenate([jnp.broadcast_to(sink_ref[kv * grp + g:kv * grp + g + 1, 0:1], (1, t))
                                for g in range(grp)], axis=1) * LOG2E
        p, m = _exp_cols(scores[kv] + bias, sink, fixed)
        vt_kv = _with_ones_row(vt[kv * HEAD_DIM:(kv + 1) * HEAD_DIM, :]).astype(BF16)
        o_t = _finish(_dot(vt_kv, p), jnp.exp2(sink - m))
        rows.extend(o_t[:, g * t:(g + 1) * t] for g in range(grp))
    o = jnp.concatenate(rows, axis=0).T
    o_ref[0] = (o * _silu(g_ref[0].astype(F32))).astype(BF16)


def _swa(z, cos, sin_s, qn, kn, sinks, t=256):
    b, s, _ = z.shape
    tp = A_WINDOW
    qo, go = EVEN_OFF["qa"][0] // 512, EVEN_OFF["ga"][0] // 512
    ko, vo = EVEN_OFF["ka"][0] // 128, EVEN_OFF["va"][0] // 128
    prev = lambda i: jnp.maximum(i * (t // tp) - 1, 0)
    bound = jnp.maximum(_score_bound(qn, kn), jnp.max(sinks) * LOG2E).reshape(1).astype(F32)
    return pl.pallas_call(
        _swa_kernel,
        grid=(b, s // t),
        in_specs=[pl.BlockSpec(memory_space=pltpu.SMEM),
                  pl.BlockSpec((1, t, 512), lambda bi, i: (bi, i, qo)),
                  pl.BlockSpec((1, t, 512), lambda bi, i: (bi, i, go)),
                  pl.BlockSpec((1, t, 128), lambda bi, i: (bi, i, ko)),
                  pl.BlockSpec((1, tp, 128), lambda bi, i: (bi, prev(i), ko)),
                  pl.BlockSpec((1, t, 128), lambda bi, i: (bi, i, vo)),
                  pl.BlockSpec((1, tp, 128), lambda bi, i: (bi, prev(i), vo)),
                  pl.BlockSpec((t, 128), lambda bi, i: (i, 0)),
                  pl.BlockSpec((t, 128), lambda bi, i: (i, 0)),
                  pl.BlockSpec((tp, 128), lambda bi, i: (prev(i), 0)),
                  pl.BlockSpec((tp, 128), lambda bi, i: (prev(i), 0)),
                  pl.BlockSpec((1, 512), lambda bi, i: (0, 0)),
                  pl.BlockSpec((1, 128), lambda bi, i: (0, 0)),
                  pl.BlockSpec((A_HEADS, 128), lambda bi, i: (0, 0))],
        out_specs=pl.BlockSpec((1, t, 512), lambda bi, i: (bi, i, 0)),
        out_shape=jax.ShapeDtypeStruct((b, s, 512), BF16),
        compiler_params=_cparams(("arbitrary", "arbitrary")),
        name="swa",
    )(bound, z, z, z, z, z, z, cos, sin_s, cos, sin_s, jnp.tile(qn, 8).reshape(1, 512), jnp.tile(kn, 2).reshape(1, 128),
      jnp.broadcast_to(sinks.reshape(A_HEADS, 1), (A_HEADS, 128)))


def _rglru_half(cols, xb, gb, emit, cw_ref, cb_ref, wr_ref, br_ref, wi_ref, bi_ref, lam_ref, xbuf, hcar, a_scr, u_scr, h_scr):
    t, wdt = xb.shape
    xbuf[8:t + 8, cols] = xb
    xc = cb_ref[:, cols] + cw_ref[0:1, cols] * xbuf[5:t + 5, cols]
    for j in range(1, B_CONV):
        xc = xc + cw_ref[j:j + 1, cols] * xbuf[5 + j:t + 5 + j, cols]
    xbuf[0:8, cols] = xbuf[t:t + 8, cols]
    emit()

    r_cols, i_cols = [], []
    for c in range(wdt // LANES):
        blk = cols.start // LANES + c
        xcc = xc[:, c * LANES:(c + 1) * LANES].astype(BF16)
        r_cols.append(_dot(xcc, wr_ref[blk]))
        i_cols.append(_dot(xcc, wi_ref[blk]))
    emit()
    r = jax.nn.sigmoid(jnp.concatenate(r_cols, axis=1) + br_ref[:, cols])
    ig = jax.nn.sigmoid(jnp.concatenate(i_cols, axis=1) + bi_ref[:, cols])
    nl = -lam_ref[:, cols]
    softplus = jnp.maximum(nl, 0.0) + jnp.log(1.0 + jnp.exp(-jnp.abs(nl)))
    log_a = -B_C * r * softplus
    a = jnp.exp(log_a)
    om = 1.0 - a * a
    u = om * lax.rsqrt(jnp.maximum(om, 1e-30)) * (ig * xc)
    emit()

    a = a.reshape(t // 8, 8, wdt)
    u = u.reshape(t // 8, 8, wdt)
    r8 = lax.broadcasted_iota(jnp.int32, a.shape, 1)
    for d in (1, 2, 4):
        a_sh = pltpu.roll(a, d, 1)
        u_sh = pltpu.roll(u, d, 1)
        m = r8 >= d
        u = jnp.where(m, a * u_sh + u, u)
        a = jnp.where(m, a * a_sh, a)
    a_scr[:, cols] = a.reshape(t, wdt)
    u_scr[:, cols] = u.reshape(t, wdt)
    emit()

    def body(j, h):
        off = pl.multiple_of(j * 8, 8)
        hh = a_scr[pl.ds(off, 8), cols] * h + u_scr[pl.ds(off, 8), cols]
        h_scr[pl.ds(off, 8), cols] = hh
        return hh[7:8, :]

    h_last = lax.fori_loop(0, t // 8, body, hcar[0:1, cols], unroll=True)
    hcar[0:1, cols] = h_last
    emit()
    return h_scr[:, cols] * _silu(gb)


def _inproj_rglru_kernel(tiles_per_seq, x_ref, g_ref, w_ref, cw_ref, cb_ref, wr_ref, br_ref, wi_ref, bi_ref, lam_ref,
                         z_ref, ob_ref, xbuf, hcar, a_scr, u_scr, h_scr):
    @pl.when(pl.program_id(0) % tiles_per_seq == 0)
    def _():
        xbuf[0:8, :] = jnp.zeros((8, B_WIDTH), F32)
        hcar[...] = jnp.zeros((8, B_WIDTH), F32)

    x = x_ref[...]
    ms = jnp.mean(x * x, axis=-1, keepdims=True)
    xn = (x * lax.rsqrt(ms + EPS) * g_ref[...]).astype(BF16)
    half = B_WIDTH // 2
    proj = {}

    def project(name, hp):
        base = (B_WIDTH if name == "gb" else 0) + hp * half
        proj[name, hp] = _dot(xn, w_ref[:, base:base + half])

    def store_z(c):
        z_ref[:, c:c + 256] = _dot(xn, w_ref[:, EVEN_FUSED + c:EVEN_FUSED + c + 256]).astype(BF16)

    pending = [functools.partial(store_z, c) for c in range(0, z_ref.shape[-1], 256)]
    pending.insert(1, functools.partial(project, "xb", 1))
    pending.insert(3, functools.partial(project, "gb", 1))

    def emit():
        if pending:
            pending.pop(0)()

    project("xb", 0)
    project("gb", 0)
    for hp in range(2):
        cols = slice(hp * half, (hp + 1) * half)
        ob_ref[:, cols] = _rglru_half(cols, proj["xb", hp], proj["gb", hp], emit, cw_ref, cb_ref, wr_ref, br_ref,
                                      wi_ref, bi_ref, lam_ref, xbuf, hcar, a_scr, u_scr, h_scr).astype(BF16)
    while pending:
        emit()


def _inproj_rglru(x2d, gain, w_bf16, seq, conv_w, conv_b, wr_bd, b_r, wi_bd, b_i, lam, tm=512):
    n, d = x2d.shape
    nc = w_bf16.shape[1]
    row = lambda v: v.reshape(1, B_WIDTH)
    full = lambda shp: pl.BlockSpec(shp, lambda i: (0,) * len(shp))
    return pl.pallas_call(
        functools.partial(_inproj_rglru_kernel, seq // tm),
        grid=(n // tm,),
        in_specs=[pl.BlockSpec((tm, d), lambda i: (i, 0)), full((1, d)), full((d, nc)),
                  full((B_CONV, B_WIDTH)), full((1, B_WIDTH)), full((4, LANES, LANES)), full((1, B_WIDTH)),
                  full((4, LANES, LANES)), full((1, B_WIDTH)), full((1, B_WIDTH))],
        out_specs=[pl.BlockSpec((tm, nc - EVEN_FUSED), lambda i: (i, 0)), pl.BlockSpec((tm, B_WIDTH), lambda i: (i, 0))],
        out_shape=[jax.ShapeDtypeStruct((n, nc - EVEN_FUSED), BF16), jax.ShapeDtypeStruct((n, B_WIDTH), BF16)],
        scratch_shapes=[pltpu.VMEM((tm + 8, B_WIDTH), F32), pltpu.VMEM((8, B_WIDTH), F32),
                        pltpu.VMEM((tm, B_WIDTH), F32), pltpu.VMEM((tm, B_WIDTH), F32), pltpu.VMEM((tm, B_WIDTH), F32)],
        compiler_params=_cparams(("arbitrary",)),
        name="inproj_rglru",
    )(x2d, gain.reshape(1, d), w_bf16, conv_w, row(conv_b), wr_bd, row(b_r), wi_bd, row(b_i), row(lam))


def _hgrn_consts():
    c = C_CHUNK
    t = np.arange(c)[:, None]
    s = np.arange(c)[None, :]
    mats = [(s <= t)]
    masks = []
    hs = c // 2
    while hs >= 1:
        mid = (t // (2 * hs)) * 2 * hs + hs - 1
        mats.append(s <= mid)
        same = (t // (2 * hs)) == (s // (2 * hs))
        masks.append(same & ((t // hs) % 2 == 1) & ((s // hs) % 2 == 0))
        hs //= 2
    pairs = [np.concatenate(masks[n:n + 2], axis=1) for n in range(0, len(masks), 2)]
    return np.concatenate(mats, axis=0).astype(np.float32), np.stack(pairs).astype(np.float32)


def _hgrn_chunk(lb, q_in, f_in, v_in, g_in, og_ref, mst_ref, msk_ref, st_scr, emit):
    c = C_CHUNK
    npair = msk_ref.shape[0]
    zeros = jnp.zeros((c, LANES), BF16)
    f = lb + (1.0 - lb) * jax.nn.sigmoid(f_in)
    g = jnp.log2(f)
    g_hi, g_lo = _split(g)
    b_all = _dot(mst_ref[0:c, :], g_hi) + _dot(mst_ref[0:c, :], g_lo)
    bm_all = _dot(mst_ref[c:, :], g_hi)
    emit()
    kk_all = 1.0 - f
    qf_all = _silu(q_in)
    v_all = v_in
    kk_bf, qf_bf = kk_all.astype(BF16), qf_all.astype(BF16)
    heads = range(C_HEADS)
    cols = [slice(h * LANES, (h + 1) * LANES) for h in heads]
    sts = [st_scr[h] for h in heads]
    os = [_dot_nt((qf_all[:, cols[h]] * jnp.exp2(b_all[:, cols[h]])).astype(BF16), sts[h].astype(BF16)) for h in heads]
    att = [[] for _ in heads]
    for n in range(npair):
        lhs, rhs = [], []
        for h in heads:
            b = b_all[:, cols[h]]
            qt, kt = [], []
            for l in (2 * n, 2 * n + 1):
                bm = bm_all[l * c:(l + 1) * c, cols[h]]
                qt.append(qf_bf[:, cols[h]] * jnp.exp2(jnp.minimum(b - bm, 1.0)).astype(BF16))
                kt.append(kk_bf[:, cols[h]] * jnp.exp2(jnp.minimum(bm - b, 1.0)).astype(BF16))
            lhs.append(jnp.concatenate(qt, axis=1))
            rhs.append(jnp.concatenate([jnp.concatenate([kt[0], zeros], axis=1),
                                        jnp.concatenate([zeros, kt[1]], axis=1)], axis=0))
        prods = [_dot_nt(lhs[h], rhs[h]) for h in heads]
        if n == 1:
            emit()
        for h in heads:
            att[h].append(jnp.where(msk_ref[n] > 0.5, prods[h], 0.0).astype(BF16))
    vbs = [v_all[:, cols[h]].astype(BF16) for h in heads]
    intra = [_dot(jnp.concatenate(att[h], axis=1), jnp.concatenate([vbs[h]] * (2 * npair), axis=0)) for h in heads]
    b_last = b_all[c - 1:c, :]
    kd = (kk_all * jnp.exp2(b_last - b_all)).astype(BF16)
    upd = [_dot(v_all[:, cols[h]].T.astype(BF16), kd[:, cols[h]]) for h in heads]
    decay = jnp.exp2(b_last)
    diag = qf_all * kk_all
    outs = []
    for h in heads:
        st_scr[h] = sts[h] * decay[:, cols[h]] + upd[h]
        o = os[h] + intra[h] + jnp.sum(diag[:, cols[h]], axis=-1, keepdims=True) * v_all[:, cols[h]]
        ms = jnp.mean(o * o, axis=-1, keepdims=True)
        outs.append(o * lax.rsqrt(ms + EPS) * og_ref[...])
    return jnp.concatenate(outs, axis=1) * _silu(g_in)


def _inproj_hgrn_kernel(tiles_per_seq, x_ref, g_ref, w_ref, lb_ref, og_ref, mst_ref, msk_ref,
                        z_ref, kc_ref, vc_ref, oc_ref, st_scr):
    @pl.when(pl.program_id(0) % tiles_per_seq == 0)
    def _():
        st_scr[...] = jnp.zeros(st_scr.shape, F32)

    x = x_ref[...]
    ms = jnp.mean(x * x, axis=-1, keepdims=True)
    xn = (x * lax.rsqrt(ms + EPS) * g_ref[...]).astype(BF16)
    own = [_dot(xn, w_ref[:, k * C_WIDTH:(k + 1) * C_WIDTH]) for k in range(4)]

    o_refs = (z_ref, kc_ref, vc_ref)
    starts = np.cumsum([0] + [o.shape[-1] for o in o_refs])
    pending = list(range(0, int(starts[-1]), 256))

    def emit():
        if not pending:
            return
        c = pending.pop(0)
        w = min(256, int(starts[-1]) - c)
        r = _dot(xn, w_ref[:, ODD_FUSED + c:ODD_FUSED + c + w]).astype(BF16)
        for o_ref, lo, hi in zip(o_refs, starts[:-1], starts[1:]):
            a, b = max(c, int(lo)), min(c + w, int(hi))
            if a < b:
                o_ref[:, a - int(lo):b - int(lo)] = r[:, a - c:b - c]

    p = lb_ref[...]
    pm = jnp.maximum(p[0:1, :], p[1:2, :])
    e0, e1 = jnp.exp(p[0:1, :] - pm), jnp.exp(p[1:2, :] - pm)
    lb = e1 / (e0 + e1)
    for ch in range(x.shape[0] // C_CHUNK):
        rows = slice(ch * C_CHUNK, (ch + 1) * C_CHUNK)
        q_in, f_in, v_in, g_in = (a[rows, :] for a in own)
        oc_ref[rows, :] = _hgrn_chunk(lb, q_in, f_in, v_in, g_in, og_ref, mst_ref, msk_ref, st_scr, emit).astype(BF16)
    while pending:
        emit()


def _inproj_hgrn(x2d, gain, w_bf16, seq, c_lb, c_og, tm=512):
    n, d = x2d.shape
    nc = w_bf16.shape[1]
    assert ODD_FUSED + sum(ODD_SPLIT) == nc
    mst, msk = _hgrn_consts()
    full = lambda shp: pl.BlockSpec(shp, lambda i: (0,) * len(shp))
    widths = ODD_SPLIT + (C_WIDTH,)
    return pl.pallas_call(
        functools.partial(_inproj_hgrn_kernel, seq // tm),
        grid=(n // tm,),
        in_specs=[pl.BlockSpec((tm, d), lambda i: (i, 0)), full((1, d)), full((d, nc)),
                  full(c_lb.shape), full((1, C_HEAD_DIM)), full(mst.shape), full(msk.shape)],
        out_specs=[pl.BlockSpec((tm, w), lambda i: (i, 0)) for w in widths],
        out_shape=[jax.ShapeDtypeStruct((n, w), BF16) for w in widths],
        scratch_shapes=[pltpu.VMEM((C_HEADS, C_HEAD_DIM, C_HEAD_DIM), F32)],
        compiler_params=_cparams(("arbitrary",)),
        name="inproj_hgrn",
    )(x2d, gain.reshape(1, d), w_bf16, c_lb, c_og.reshape(1, C_HEAD_DIM), jnp.asarray(mst, BF16), jnp.asarray(msk, F32))


def _nsa_prep_kernel(kcr_ref, vcr_ref, ks_ref, vs_ref, kw_ref, vw_ref, cos_ref, sin_ref, cosc_ref, sinc_ref,
                     kncmp_ref, knslc_ref, knwin_ref, pek_ref, pev_ref, w1k_ref, w2k_ref, w1v_ref, w2vt_ref,
                     kc_o, vct_o, ks_o, vst_o, kw_o, vwt_o):
    def hidden(xr_ref, pe_ref, w1_ref):
        xr = xr_ref[0].astype(F32)
        top = _dot((xr + pe_ref[0:1, :]).astype(BF16), w1_ref[0])
        bot = _dot((xr + pe_ref[1:2, :]).astype(BF16), w1_ref[1])
        pre = top + pltpu.roll(bot, bot.shape[0] - 1, 0)
        return _silu(pre).astype(BF16)

    kc = _dot(hidden(kcr_ref, pek_ref, w1k_ref), w2k_ref[...])
    kc = _rope(_head_rms(kc, kncmp_ref[...]), cosc_ref[...], sinc_ref[...])
    d0, d1 = _split_kv(kc)
    kc_o[0, 0] = d0.astype(BF16)
    kc_o[0, 1] = d1.astype(BF16)
    vct = _dot_nt(w2vt_ref[...], hidden(vcr_ref, pev_ref, w1v_ref))
    for kv in range(D_KV_HEADS):
        vct_o[0, kv] = _with_ones_row(vct[kv * HEAD_DIM:(kv + 1) * HEAD_DIM, :]).astype(BF16)

    s = ks_ref.shape[1]
    step = 256
    for r0 in range(0, s, step):
        rows = slice(r0, r0 + step)
        cos, sin = cos_ref[rows, :], sin_ref[rows, :]
        ks = _rope(_head_rms(ks_ref[0, rows, :].astype(F32), knslc_ref[...]), cos, sin)
        kw = _rope(_head_rms(kw_ref[0, rows, :].astype(F32), knwin_ref[...]), cos, sin)
        blk = (r0 + _row((step, LANES))) // SEL_LEN
        onehot = jnp.where(_lane((step, LANES)) - HEAD_DIM == blk, MASK_BIG, 0.0)
        for src, dst, fill in ((ks, ks_o, onehot), (kw, kw_o, 0.0)):
            d0, d1 = _split_kv(src, fill)
            dst[0, 0, rows, :] = d0.astype(BF16)
            dst[0, 1, rows, :] = d1.astype(BF16)
        for src, dst in ((vs_ref, vst_o), (vw_ref, vwt_o)):
            vt = src[0, rows, :].astype(F32).T
            for kv in range(D_KV_HEADS):
                dst[0, kv, :, rows] = _with_ones_row(vt[kv * HEAD_DIM:(kv + 1) * HEAD_DIM, :]).astype(BF16)


def _nsa_prep(z, kcr, vcr, cos, sin_s, cosc, sinc, kn_cmp, kn_slc, kn_win, pek, pev, w1k, w2k, w1v, w2v):
    b, s, _ = z.shape
    nseg = s // CMP_STRIDE
    zb = lambda name: pl.BlockSpec((1, s, 128), lambda bi, o=ODD_OFF[name][0] // 128: (bi, 0, o))
    full = lambda shp: pl.BlockSpec(shp, lambda bi: (0,) * len(shp))
    seg = pl.BlockSpec((1, nseg, 2048), lambda bi: (bi, 0, 0))
    g2 = lambda g: jnp.tile(g, 2).reshape(1, 128)
    dup = lambda n: (jax.ShapeDtypeStruct((b, 2, n, 128), BF16), pl.BlockSpec((1, 2, n, 128), lambda bi: (bi, 0, 0, 0)))
    tr = lambda n: (jax.ShapeDtypeStruct((b, 2, VT_ROWS, n), BF16),
                    pl.BlockSpec((1, 2, VT_ROWS, n), lambda bi: (bi, 0, 0, 0)))
    outs = [dup(nseg), tr(nseg), dup(s), tr(s), dup(s), tr(s)]
    return pl.pallas_call(
        _nsa_prep_kernel,
        grid=(b,),
        in_specs=[seg, seg, zb("ksd"), zb("vsd"), zb("kwd"), zb("vwd"),
                  full((s, 128)), full((s, 128)), full((nseg, 128)), full((nseg, 128)),
                  full((1, 128)), full((1, 128)), full((1, 128)), full((2, 2048)), full((2, 2048)),
                  full((2, 2048, 256)), full((256, 128)), full((2, 2048, 256)), full((128, 256))],
        out_specs=[o[1] for o in outs],
        out_shape=[o[0] for o in outs],
        compiler_params=_cparams(("arbitrary",)),
        name="nsa_prep",
    )(kcr, vcr, z, z, z, z, cos, sin_s, cosc, sinc, g2(kn_cmp), g2(kn_slc), g2(kn_win), pek, pev, w1k, w2k, w1v, w2v)


SEL_CHUNK = 512
SEL_PIECE = 512


def _nsa_kernel(bound_ref, *refs):
    small = jnp.maximum(jnp.maximum(bound_ref[0], bound_ref[1]), bound_ref[2]) <= MAX_SAFE_BOUND

    @pl.when(small)
    def _():
        _nsa_body((bound_ref[0], bound_ref[1], bound_ref[2]), *refs)

    @pl.when(jnp.logical_not(small))
    def _():
        _nsa_body((None, None, None), *refs)


def _nsa_body(fixed, q_ref, gd_ref, gate_ref, cos_ref, sin_ref, qn_ref, kc_ref, vct_ref, ks_ref, vst_ref, kw_ref, vwt_ref,
              ovlt_ref, o_ref):
    fix_c, fix_s, fix_w = fixed
    i = pl.program_id(1)
    t = q_ref.shape[1]
    grp = D_HEADS // D_KV_HEADS
    w = grp * t
    kvs = range(D_KV_HEADS)
    q = _rope(_head_rms(q_ref[0].astype(F32), qn_ref[...]), cos_ref[...], sin_ref[...]) * Q_SCALE
    stack = lambda heads: [jnp.concatenate(heads[kv * grp:(kv + 1) * grp], axis=0) for kv in kvs]
    qs = stack(_q_heads(q))
    tq1 = i * t + _lane((1, t))

    nwin = D_WINDOW + t
    win0 = pl.multiple_of(jnp.maximum(i * t + t - nwin, 0), t)
    s_cmp = [_dot_nt(kc_ref[0, kv], qs[kv]) for kv in kvs]
    s_win = [_dot_nt(kw_ref[0, kv, pl.ds(win0, nwin), :], qs[kv]) for kv in kvs]

    ncmp = kc_ref.shape[2] - 1
    nsel = ovlt_ref.shape[0]
    crow = _row((kc_ref.shape[2], t))
    bias_c = _tile4(jnp.where((crow * CMP_STRIDE + (CMP_LEN - 1) <= tq1) & (crow < ncmp), 0.0, NEG_INF))
    row_ok = _tile4(tq1 >= CMP_LEN - 1)
    es = [_exp_cols(s_cmp[kv] + bias_c, fixed=fix_c)[0] for kv in kvs]
    accs = [_dot(vct_ref[0, kv], es[kv]) for kv in kvs]
    invs = [jnp.where(row_ok, 1.0 / accs[kv][HEAD_DIM:HEAD_DIM + 1, :], 0.0) for kv in kvs]
    oc_t = [accs[kv][0:HEAD_DIM, :] * invs[kv] for kv in kvs]
    psums = []
    for kv in kvs:
        p = es[kv].astype(F32) * invs[kv]
        psums.append(p[:, 0:t] + p[:, t:2 * t] + p[:, 2 * t:3 * t] + p[:, 3 * t:4 * t])
    imps = [_dot_split_rhs(ovlt_ref[...], psums[kv]) for kv in kvs]

    rel = tq1 - (win0 + _row((nwin, t)))
    bias_w = _tile4(jnp.where((rel >= 0) & (rel < D_WINDOW), 0.0, NEG_INF))
    ow_t = [_finish(_dot(vwt_ref[0, kv, :, pl.ds(win0, nwin)], _exp_cols(s_win[kv] + bias_w, fixed=fix_w)[0]))
            for kv in kvs]

    jrow = _row((nsel, t))
    jrow_f = jrow.astype(F32)
    cur = (i * t + _lane((nsel, t))) // SEL_LEN
    forced = (jrow == 0) | (jrow == cur)
    fills = []
    for kv in kvs:
        score = jnp.where(forced, POS_INF, jnp.where(jrow <= cur, imps[kv], NEG_INF))
        chosen = jnp.zeros((nsel, t), F32)
        for _ in range(min(SEL_TOPK, nsel)):
            mx = jnp.max(score, axis=0, keepdims=True)
            first = jnp.min(jnp.where(score == mx, jrow_f, 1e9), axis=0, keepdims=True)
            hit = jrow_f == first
            chosen = jnp.where(hit, 1.0, chosen)
            score = jnp.where(hit, -3e38, score)
        frame = jnp.concatenate([jnp.zeros((HEAD_DIM, t), F32), chosen - 1.0,
                                 jnp.zeros((LANES - HEAD_DIM - nsel, t), F32)], axis=0)
        fills.extend([frame.T] * (grp // 2))

    init = (jnp.full((1, w), NEG_INF, F32), jnp.zeros((VT_ROWS, w), F32))
    pc = SEL_PIECE

    def sweep(states, k_ref, vt_ref, queries, offs, biases, n):
        scores = [[_dot_nt(k_ref[0, kv, pl.ds(off, n), :], queries[kv]) for kv in kvs] for off in offs]
        for off, sc, bias in zip(offs, scores, biases):
            states = tuple(_online_update(states[kv], sc[kv] if bias is None else sc[kv] + bias,
                                          vt_ref[0, kv, :, pl.ds(off, n)], fix_s) for kv in kvs)
        return states

    qsel = stack(_q_heads(q, fills))
    ch = SEL_CHUNK
    own = pl.multiple_of(((i * t) // ch) * ch, ch)
    st = sweep((init, init), ks_ref, vst_ref, qsel, [own],
               [_tile4(jnp.where(own + _row((ch, t)) <= tq1, 0.0, NEG_INF))], ch)

    def sel_chunk(c, states):
        base = pl.multiple_of(c * ch, ch)
        return sweep(states, ks_ref, vst_ref, qsel, [base + n * pc for n in range(ch // pc)], [None] * (ch // pc), pc)

    st = lax.fori_loop(0, (i * t) // ch, sel_chunk, st)
    os_t = [_finish(acc) for (_, acc) in st]

    g_t = jax.nn.sigmoid(gate_ref[0].astype(F32)).T
    rows = []
    for h in range(D_HEADS):
        kv, g = divmod(h, grp)
        cols = slice(g * t, (g + 1) * t)
        r = D_BRANCHES * h
        rows.append(g_t[r:r + 1, :] * oc_t[kv][:, cols] + g_t[r + 1:r + 2, :] * os_t[kv][:, cols]
                    + g_t[r + 2:r + 3, :] * ow_t[kv][:, cols])
    o = jnp.concatenate(rows, axis=0).T
    o_ref[0] = (o * _silu(gd_ref[0].astype(F32))).astype(BF16)


def _overlap_t(s):
    ncmp = (s - CMP_LEN) // CMP_STRIDE + 1
    nsel = s // SEL_LEN
    cs = np.arange(ncmp)[None, :] * CMP_STRIDE
    ss = np.arange(nsel)[:, None] * SEL_LEN
    ovl = np.zeros((nsel, s // CMP_STRIDE), np.float32)
    ovl[:, :ncmp] = (cs < ss + SEL_LEN) & (cs + CMP_LEN > ss)
    return ovl


def _nsa(z, cos, sin_s, qn, bounds, kc, vct, ks, vst, kw, vwt, t=256):
    b, s, _ = z.shape
    nseg = s // CMP_STRIDE
    assert nseg == LANES, "compressed-block scores are laid out on one 128-row tile"
    ovlt = _overlap_t(s)
    full = lambda shp: pl.BlockSpec(shp, lambda bi, i: (0,) * len(shp))
    dup = lambda n: pl.BlockSpec((1, 2, n, 128), lambda bi, i: (bi, 0, 0, 0))
    tr = lambda n: pl.BlockSpec((1, 2, VT_ROWS, n), lambda bi, i: (bi, 0, 0, 0))
    return pl.pallas_call(
        _nsa_kernel,
        grid=(b, s // t),
        in_specs=[pl.BlockSpec(memory_space=pltpu.SMEM),
                  pl.BlockSpec((1, t, 512), lambda bi, i: (bi, i, ODD_OFF["qd"][0] // 512)),
                  pl.BlockSpec((1, t, 512), lambda bi, i: (bi, i, ODD_OFF["gd"][0] // 512)),
                  pl.BlockSpec((1, t, 128), lambda bi, i: (bi, i, ODD_OFF["gate"][0] // 128)),
                  pl.BlockSpec((t, 128), lambda bi, i: (i, 0)),
                  pl.BlockSpec((t, 128), lambda bi, i: (i, 0)),
                  full((1, 512)), dup(nseg), tr(nseg), dup(s), tr(s), dup(s), tr(s), full(ovlt.shape)],
        out_specs=pl.BlockSpec((1, t, 512), lambda bi, i: (bi, i, 0)),
        out_shape=jax.ShapeDtypeStruct((b, s, 512), BF16),
        compiler_params=_cparams(("arbitrary", "arbitrary")),
        name="nsa",
    )(bounds, z, z, z, cos, sin_s, jnp.tile(qn, 8).reshape(1, 512), kc, vct, ks, vst, kw, vwt, jnp.asarray(ovlt, BF16))


def _permute_cols(w, order, src):
    cols = []
    for name, width in order:
        o, sw = src[name]
        blk = w[:, o:o + sw]
        if sw < width:
            blk = jnp.pad(blk, ((0, 0), (0, width - sw)))
        cols.append(blk)
    return jnp.concatenate(cols, axis=1).astype(BF16)


def _block_diag_pairs(w):
    z = jnp.zeros((4, LANES, LANES), w.dtype)
    z = z.at[:, 0:64, 0:64].set(w[0::2])
    z = z.at[:, 64:128, 64:128].set(w[1::2])
    return z.astype(BF16)


def _rope_tables(pos):
    half = HEAD_DIM // 2
    inv = ROPE_THETA ** (-jnp.arange(half, dtype=F32) / half)
    ang = pos.astype(F32)[:, None] * inv[None, :]
    cos, sin = jnp.cos(ang), jnp.sin(ang)
    cos_t = jnp.tile(cos, (1, 4))
    sin_t = jnp.tile(jnp.concatenate([-sin, sin], axis=1), (1, 2))
    return cos_t, sin_t


def _expand_compress_w1(w1):
    hdim = w1.shape[1]
    w = w1.reshape(2, CMP_STRIDE, 1, HEAD_DIM, 1, hdim)
    same_head = jnp.eye(2, dtype=w1.dtype).reshape(1, 1, 2, 1, 2, 1)
    return (w * same_head).reshape(2, CMP_STRIDE * 2 * HEAD_DIM, 2 * hdim).astype(BF16)


def _expand_compress_w2(w2):
    hdim, hd = w2.shape
    z = jnp.zeros((2 * hdim, 2 * hd), w2.dtype)
    z = z.at[0:hdim, 0:hd].set(w2)
    z = z.at[hdim:, hd:].set(w2)
    return z.astype(BF16)


def _expand_pe(pe):
    p = pe.reshape(2, CMP_STRIDE, 1, HEAD_DIM)
    return jnp.broadcast_to(p, (2, CMP_STRIDE, 2, HEAD_DIM)).reshape(2, CMP_STRIDE * 2 * HEAD_DIM)


def _even_layer(h, mem, g, mem_g, w_mem_kv, m_qn, m_kn, w_in, w_out, a_qn, a_kn, a_sinks,
                conv_w, conv_b, w_r, b_r, w_i, b_i, lam, cos, sin_s):
    b, s, d = h.shape
    h2 = h.reshape(b * s, d)
    z, ob = _inproj_rglru(h2, g, _permute_cols(w_in, EVEN_ORDER, EVEN_SRC), s, conv_w, conv_b,
                          _block_diag_pairs(w_r), b_r, _block_diag_pairs(w_i), b_i, lam)
    z = z.reshape(b, s, EVEN_COLS)
    oa = _swa(z, cos, sin_s, a_qn, a_kn, a_sinks)
    om = _mem_attention(z, EVEN_OFF["qm"][0], EVEN_OFF["gm"][0], mem, mem_g, w_mem_kv.astype(BF16), m_qn, m_kn)
    out = _outproj(h2, oa.reshape(b * s, 512), ob, om.reshape(b * s, 256), w_out.astype(BF16))
    return out.reshape(b, s, d)


def _odd_layer(h, mem, g, mem_g, w_mem_kv, m_qn, m_kn, w_in, w_out, c_lb, c_og,
               d_qn, d_kn_cmp, d_kn_slc, d_kn_win, pe_k, pe_v, w1k, w2k, w1v, w2v, cos, sin_s, cosc, sinc):
    b, s, d = h.shape
    h2 = h.reshape(b * s, d)
    z, kcd, vcd, oc = _inproj_hgrn(h2, g, _permute_cols(w_in, ODD_ORDER, ODD_SRC), s, c_lb, c_og)
    z = z.reshape(b, s, ODD_SPLIT[0])
    nseg = s // CMP_STRIDE
    seg = lambda a: a.reshape(b, nseg, CMP_STRIDE * 128)
    kc, vc, ks, vs, kw, vw = _nsa_prep(
        z, seg(kcd), seg(vcd), cos, sin_s, cosc, sinc, d_kn_cmp, d_kn_slc, d_kn_win,
        _expand_pe(pe_k), _expand_pe(pe_v), _expand_compress_w1(w1k), _expand_compress_w2(w2k),
        _expand_compress_w1(w1v), _expand_compress_w2(w2v).T)
    bounds = jnp.stack([_score_bound(d_qn, kn) for kn in (d_kn_cmp, d_kn_slc, d_kn_win)]).astype(F32)
    od = _nsa(z, cos, sin_s, d_qn, bounds, kc, vc, ks, vs, kw, vw)
    om = _mem_attention(z, ODD_OFF["qm"][0], ODD_OFF["gm"][0], mem, mem_g, w_mem_kv.astype(BF16), m_qn, m_kn)
    out = _outproj(h2, oc, od.reshape(b * s, 512), om.reshape(b * s, 256), w_out.astype(BF16))
    return out.reshape(b, s, d)


def kernel(x, mem, norm_g, mem_norm_g, mem_w_kv, mem_qn, mem_kn, ev_w_in, ev_w_out, a_qn, a_kn, a_sinks,
           b_conv_w, b_conv_b, b_w_r, b_b_r, b_w_i, b_b_i, b_lambda, od_w_in, od_w_out, c_lb, c_onorm,
           d_qn, d_kn_cmp, d_kn_slc, d_kn_win, d_pe_k, d_pe_v, d_w1k, d_w2k, d_w1v, d_w2v):
    depth = norm_g.shape[0]
    assert depth == 2 and c_lb.shape[0] == 2, "the HGRN2 lower-bound formula in the kernel is written for depth 2"
    s = x.shape[1]
    assert s % 256 == 0 and s >= D_WINDOW
    pos = jnp.arange(s)
    cos, sin_s = _rope_tables(pos)
    nseg = s // CMP_STRIDE
    cmp_end = jnp.minimum(jnp.arange(nseg) * CMP_STRIDE + CMP_LEN - 1, s - 1)
    cosc, sinc = _rope_tables(cmp_end)
    h = _even_layer(x, mem, norm_g[0], mem_norm_g[0], mem_w_kv[0], mem_qn[0], mem_kn[0], ev_w_in[0], ev_w_out[0],
                    a_qn[0], a_kn[0], a_sinks[0], b_conv_w[0], b_conv_b[0], b_w_r[0], b_b_r[0], b_w_i[0], b_b_i[0],
                    b_lambda[0], cos, sin_s)
    h = _odd_layer(h, mem, norm_g[1], mem_norm_g[1], mem_w_kv[1], mem_qn[1], mem_kn[1], od_w_in[0], od_w_out[0],
                   c_lb, c_onorm[0], d_qn[0], d_kn_cmp[0], d_kn_slc[0], d_kn_win[0], d_pe_k[0], d_pe_v[0],
                   d_w1k[0], d_w2k[0], d_w1v[0], d_w2v[0], cos, sin_s, cosc, sinc)
    return h
```

```python
import functools

import numpy as np
import jax
import jax.numpy as jnp
from jax import lax
from jax.experimental import pallas as pl
from jax.experimental.pallas import tpu as pltpu

F32 = jnp.float32
BF16 = jnp.bfloat16

D_MODEL = 1024
N_MEM = 256
HEAD_DIM = 64
ROPE_THETA = 10000.0
EPS = 1e-6
NEG_INF = -1e30
POS_INF = 1e30
MASK_BIG = 1e30
LANES = 128

A_HEADS, A_KV_HEADS, A_WINDOW = 8, 2, 128
B_WIDTH, B_BLOCKS, B_CONV, B_C = 512, 8, 4, 8.0
M_HEADS = 4
C_HEADS, C_HEAD_DIM, C_CHUNK = 4, 128, 64
C_WIDTH = C_HEADS * C_HEAD_DIM
D_HEADS, D_KV_HEADS = 8, 2
CMP_LEN, CMP_STRIDE, CMP_HIDDEN = 32, 16, 128
SEL_LEN, SEL_TOPK = 64, 4
D_WINDOW = 512
D_BRANCHES = 3
SCALE = HEAD_DIM ** -0.5
LOG2E = 1.4426950408889634
Q_SCALE = SCALE * LOG2E

EVEN_ORDER = [("xb", 512), ("gb", 512), ("qa", 512), ("ga", 512), ("qm", 256), ("gm", 256), ("ka", 128), ("va", 128)]
EVEN_FUSED = 1024
EVEN_SRC = {"qa": (0, 512), "ka": (512, 128), "va": (640, 128), "ga": (768, 512), "xb": (1280, 512),
            "gb": (1792, 512), "qm": (2304, 256), "gm": (2560, 256)}
ODD_ORDER = [("qc", 512), ("fc", 512), ("ic", 512), ("gc", 512), ("qd", 512), ("gd", 512), ("qm", 256), ("gm", 256),
             ("ksd", 128), ("vsd", 128), ("kwd", 128), ("vwd", 128), ("gate", 128), ("kcd", 128), ("vcd", 128)]
ODD_FUSED = 2048
ODD_SPLIT = (2176, 128, 128)
ODD_SRC = {"qc": (0, 512), "fc": (512, 512), "ic": (1024, 512), "gc": (1536, 512), "qd": (2048, 512),
           "kcd": (2560, 128), "vcd": (2688, 128), "ksd": (2816, 128), "vsd": (2944, 128), "kwd": (3072, 128),
           "vwd": (3200, 128), "gate": (3328, 24), "gd": (3352, 512), "qm": (3864, 256), "gm": (4120, 256)}

VMEM_LIMIT = 48 * 1024 * 1024


def _offsets(order):
    off, out = 0, {}
    for name, w in order:
        out[name] = (off, w)
        off += w
    return out, off


EVEN_OFF, EVEN_COLS = _offsets(EVEN_ORDER[2:])
ODD_OFF, ODD_COLS = _offsets(ODD_ORDER[4:])


def _cparams(sem):
    return pltpu.CompilerParams(dimension_semantics=sem, vmem_limit_bytes=VMEM_LIMIT)


def _dot(a, b):
    return jnp.dot(a, b, preferred_element_type=F32)


def _dot_nt(a, b):
    return lax.dot_general(a, b, (((1,), (1,)), ((), ())), preferred_element_type=F32)


def _split(x):
    hi = x.astype(BF16)
    lo = (x - hi.astype(F32)).astype(BF16)
    return hi, lo


def _dot_split_lhs(x, m):
    hi, lo = _split(x)
    return _dot(hi, m) + _dot(lo, m)


def _dot_split_rhs(m, x):
    hi, lo = _split(x)
    return _dot(m, hi) + _dot(m, lo)


def _lane(shape):
    return lax.broadcasted_iota(jnp.int32, shape, len(shape) - 1)


def _row(shape):
    return lax.broadcasted_iota(jnp.int32, shape, len(shape) - 2)


def _silu(x):
    return x * jax.nn.sigmoid(x)


def _seg_ones():
    r = lax.broadcasted_iota(jnp.int32, (LANES, LANES), 0) >> 6
    c = lax.broadcasted_iota(jnp.int32, (LANES, LANES), 1) >> 6
    return jnp.where(r == c, 1.0, 0.0).astype(BF16)


def _head_rms(x, gain):
    seg = _seg_ones()
    cols = []
    for c in range(x.shape[1] // LANES):
        xc = x[:, c * LANES:(c + 1) * LANES]
        ms = _dot_split_lhs(xc * xc, seg) * (1.0 / HEAD_DIM)
        cols.append(xc * lax.rsqrt(ms + EPS))
    y = cols[0] if len(cols) == 1 else jnp.concatenate(cols, axis=1)
    return y * gain


def _rope(x, cos, sin_s):
    first = (_lane((x.shape[0], LANES)) & 63) < 32
    cols = []
    for c in range(x.shape[1] // LANES):
        xc = x[:, c * LANES:(c + 1) * LANES]
        sw = jnp.where(first, pltpu.roll(xc, 96, 1), pltpu.roll(xc, 32, 1))
        cols.append(xc * cos + sw * sin_s)
    return cols[0] if len(cols) == 1 else jnp.concatenate(cols, axis=1)


def _split_kv(k, fill=0.0):
    lo = _lane(k.shape) < 64
    return jnp.where(lo, k, fill), jnp.where(lo, pltpu.roll(k, 64, 1), fill)


def _q_heads(q, fills=None):
    lo = _lane((q.shape[0], LANES)) < 64
    out = []
    for c in range(q.shape[1] // LANES):
        qc = q[:, c * LANES:(c + 1) * LANES]
        fill = 0.0 if fills is None else fills[c]
        out.append(jnp.where(lo, qc, fill).astype(BF16))
        out.append(jnp.where(lo, pltpu.roll(qc, 64, 1), fill).astype(BF16))
    return out


VT_ROWS = HEAD_DIM + 16


def _with_ones_row(vt):
    n = vt.shape[1]
    pad = jnp.where(_row((VT_ROWS - HEAD_DIM, n)) == 0, 1.0, 0.0)
    return jnp.concatenate([vt, pad], axis=0)


MAX_SAFE_BOUND = 60.0


def _score_bound(q_gain, k_gain):
    return 1.02 * HEAD_DIM * SCALE * LOG2E * jnp.max(jnp.abs(q_gain)) * jnp.max(jnp.abs(k_gain))


def _with_fixed_reference(bound, body):
    small = bound <= MAX_SAFE_BOUND

    @pl.when(small)
    def _():
        body(bound)

    @pl.when(jnp.logical_not(small))
    def _():
        body(None)


def _exp_cols(s, extra=None, fixed=None):
    if fixed is not None:
        return jnp.exp2(s - fixed).astype(BF16), fixed
    m = jnp.max(s, axis=0, keepdims=True)
    if extra is not None:
        m = jnp.maximum(m, extra)
    return jnp.exp2(s - m).astype(BF16), m


def _finish(acc, extra_den=None):
    den = acc[HEAD_DIM:HEAD_DIM + 1, :]
    if extra_den is not None:
        den = den + extra_den
    return acc[0:HEAD_DIM, :] * (1.0 / den)


def _online_update(state, s, vt_tile, fixed=None):
    m, acc = state
    if fixed is not None:
        return m, acc + _dot(vt_tile, _exp_cols(s, fixed=fixed)[0])
    p, m_new = _exp_cols(s, m)
    acc = jnp.exp2(m - m_new) * acc + _dot(vt_tile, p)
    return m_new, acc


def _tile4(x):
    return jnp.concatenate([x, x, x, x], axis=1)


def _outproj_kernel(h_ref, a_ref, b_ref, m_ref, w_ref, o_ref):
    a, b, m = a_ref[...], b_ref[...], m_ref[...]
    for c in range(0, D_MODEL, 256):
        acc = h_ref[:, c:c + 256]
        acc += _dot(a, w_ref[0:512, c:c + 256])
        acc += _dot(b, w_ref[512:1024, c:c + 256])
        acc += _dot(m, w_ref[1024:1280, c:c + 256])
        o_ref[:, c:c + 256] = acc


def _outproj(h2d, oa, ob, om, w_bf16, tm=1024):
    n, d = h2d.shape
    return pl.pallas_call(
        _outproj_kernel,
        grid=(n // tm,),
        in_specs=[pl.BlockSpec((tm, d), lambda i: (i, 0)),
                  pl.BlockSpec((tm, 512), lambda i: (i, 0)),
                  pl.BlockSpec((tm, 512), lambda i: (i, 0)),
                  pl.BlockSpec((tm, 256), lambda i: (i, 0)),
                  pl.BlockSpec((1280, d), lambda i: (0, 0))],
        out_specs=pl.BlockSpec((tm, d), lambda i: (i, 0)),
        out_shape=jax.ShapeDtypeStruct((n, d), F32),
        compiler_params=_cparams(("arbitrary",)),
        name="outproj",
    )(h2d, oa, ob, om, w_bf16)


def _mem_kernel(bound_ref, q_ref, gm_ref, mem_ref, mg_ref, wkv_ref, qn_ref, kn_ref, o_ref, k_scr, vt_scr):
    @pl.when(pl.program_id(1) == 0)
    def _():
        m = mem_ref[0]
        ms = jnp.mean(m * m, axis=-1, keepdims=True)
        mn = (m * lax.rsqrt(ms + EPS) * mg_ref[...]).astype(BF16)
        kv = _dot(mn, wkv_ref[...])
        km = _head_rms(kv[:, 0:256], kn_ref[...])
        for c in range(2):
            k0, k1 = _split_kv(km[:, c * LANES:(c + 1) * LANES])
            k_scr[2 * c] = k0.astype(BF16)
            k_scr[2 * c + 1] = k1.astype(BF16)
        vt = kv[:, 256:512].T
        for h in range(M_HEADS):
            vt_scr[h] = _with_ones_row(vt[h * HEAD_DIM:(h + 1) * HEAD_DIM, :]).astype(BF16)

    def body(fixed):
        q = _head_rms(q_ref[0].astype(F32), qn_ref[...]) * Q_SCALE
        scores = [_dot_nt(k_scr[h], qh) for h, qh in enumerate(_q_heads(q))]
        rows = [_finish(_dot(vt_scr[h], _exp_cols(sc, fixed=fixed)[0])) for h, sc in enumerate(scores)]
        o = jnp.concatenate(rows, axis=0).T
        o_ref[0] = (o * _silu(gm_ref[0].astype(F32))).astype(BF16)

    _with_fixed_reference(bound_ref[0], body)


def _mem_attention(z, q_off, g_off, mem, mem_g, wkv_bf16, qn, kn, tq=1024):
    b, s, _ = z.shape
    return pl.pallas_call(
        _mem_kernel,
        grid=(b, s // tq),
        in_specs=[pl.BlockSpec(memory_space=pltpu.SMEM),
                  pl.BlockSpec((1, tq, 256), lambda bi, i: (bi, i, q_off // 256)),
                  pl.BlockSpec((1, tq, 256), lambda bi, i: (bi, i, g_off // 256)),
                  pl.BlockSpec((1, N_MEM, D_MODEL), lambda bi, i: (bi, 0, 0)),
                  pl.BlockSpec((1, D_MODEL), lambda bi, i: (0, 0)),
                  pl.BlockSpec((D_MODEL, 512), lambda bi, i: (0, 0)),
                  pl.BlockSpec((1, 256), lambda bi, i: (0, 0)),
                  pl.BlockSpec((1, 256), lambda bi, i: (0, 0))],
        out_specs=pl.BlockSpec((1, tq, 256), lambda bi, i: (bi, i, 0)),
        out_shape=jax.ShapeDtypeStruct((b, s, 256), BF16),
        scratch_shapes=[pltpu.VMEM((M_HEADS, N_MEM, LANES), BF16), pltpu.VMEM((M_HEADS, VT_ROWS, N_MEM), BF16)],
        compiler_params=_cparams(("arbitrary", "arbitrary")),
        name="mem_attention",
    )(_score_bound(qn, kn).reshape(1).astype(F32), z, z, mem, mem_g.reshape(1, D_MODEL), wkv_bf16,
      jnp.tile(qn, 4).reshape(1, 256), jnp.tile(kn, 4).reshape(1, 256))


def _swa_kernel(bound_ref, *refs):
    _with_fixed_reference(bound_ref[0], functools.partial(_swa_body, refs))


def _swa_body(refs, fixed):
    (q_ref, g_ref, kc_ref, kp_ref, vc_ref, vp_ref, cosc_ref, sinc_ref, cosp_ref, sinp_ref,
     qn_ref, kn_ref, sink_ref, o_ref) = refs
    i = pl.program_id(1)
    t = q_ref.shape[1]
    tp = kp_ref.shape[1]
    grp = A_HEADS // A_KV_HEADS
    q = _rope(_head_rms(q_ref[0].astype(F32), qn_ref[...]), cosc_ref[...], sinc_ref[...]) * Q_SCALE
    qh = _q_heads(q)
    kc = _rope(_head_rms(kc_ref[0].astype(F32), kn_ref[...]), cosc_ref[...], sinc_ref[...])
    kp = _rope(_head_rms(kp_ref[0].astype(F32), kn_ref[...]), cosp_ref[...], sinp_ref[...])
    ks = [x.astype(BF16) for x in _split_kv(jnp.concatenate([kp, kc], axis=0))]
    vt = jnp.concatenate([vp_ref[0].astype(F32).T, vc_ref[0].astype(F32).T], axis=1)
    ts = i * t - tp + _row((tp + t, t))
    rel = i * t + _lane((tp + t, t)) - ts
    bias = _tile4(jnp.where((rel >= 0) & (rel < A_WINDOW) & (ts >= 0), 0.0, NEG_INF))
    scores = [_dot_nt(ks[kv], jnp.concatenate(qh[kv * grp:(kv + 1) * grp], axis=0)) for kv in range(A_KV_HEADS)]
    rows = []
    for kv in range(A_KV_HEADS):
        sink = jnp.concatenate([jnp.broadcast_to(sink_ref[kv * grp + g:kv * grp + g + 1, 0:1], (1, t))
                                for g in range(grp)], axis=1) * LOG2E
        p, m = _exp_cols(scores[kv] + bias, sink, fixed)
        vt_kv = _with_ones_row(vt[kv * HEAD_DIM:(kv + 1) * HEAD_DIM, :]).astype(BF16)
        o_t = _finish(_dot(vt_kv, p), jnp.exp2(sink - m))
        rows.extend(o_t[:, g * t:(g + 1) * t] for g in range(grp))
    o = jnp.concatenate(rows, axis=0).T
    o_ref[0] = (o * _silu(g_ref[0].astype(F32))).astype(BF16)


def _swa(z, cos, sin_s, qn, kn, sinks, t=256):
    b, s, _ = z.shape
    tp = A_WINDOW
    qo, go = EVEN_OFF["qa"][0] // 512, EVEN_OFF["ga"][0] // 512
    ko, vo = EVEN_OFF["ka"][0] // 128, EVEN_OFF["va"][0] // 128
    prev = lambda i: jnp.maximum(i * (t // tp) - 1, 0)
    bound = jnp.maximum(_score_bound(qn, kn), jnp.max(sinks) * LOG2E).reshape(1).astype(F32)
    return pl.pallas_call(
        _swa_kernel,
        grid=(b, s // t),
        in_specs=[pl.BlockSpec(memory_space=pltpu.SMEM),
                  pl.BlockSpec((1, t, 512), lambda bi, i: (bi, i, qo)),
                  pl.BlockSpec((1, t, 512), lambda bi, i: (bi, i, go)),
                  pl.BlockSpec((1, t, 128), lambda bi, i: (bi, i, ko)),
                  pl.BlockSpec((1, tp, 128), lambda bi, i: (bi, prev(i), ko)),
                  pl.BlockSpec((1, t, 128), lambda bi, i: (bi, i, vo)),
                  pl.BlockSpec((1, tp, 128), lambda bi, i: (bi, prev(i), vo)),
                  pl.BlockSpec((t, 128), lambda bi, i: (i, 0)),
                  pl.BlockSpec((t, 128), lambda bi, i: (i, 0)),
                  pl.BlockSpec((tp, 128), lambda bi, i: (prev(i), 0)),
                  pl.BlockSpec((tp, 128), lambda bi, i: (prev(i), 0)),
                  pl.BlockSpec((1, 512), lambda bi, i: (0, 0)),
                  pl.BlockSpec((1, 128), lambda bi, i: (0, 0)),
                  pl.BlockSpec((A_HEADS, 128), lambda bi, i: (0, 0))],
        out_specs=pl.BlockSpec((1, t, 512), lambda bi, i: (bi, i, 0)),
        out_shape=jax.ShapeDtypeStruct((b, s, 512), BF16),
        compiler_params=_cparams(("arbitrary", "arbitrary")),
        name="swa",
    )(bound, z, z, z, z, z, z, cos, sin_s, cos, sin_s, jnp.tile(qn, 8).reshape(1, 512), jnp.tile(kn, 2).reshape(1, 128),
      jnp.broadcast_to(sinks.reshape(A_HEADS, 1), (A_HEADS, 128)))


def _rglru_half(cols, xb, gb, emit, cw_ref, cb_ref, wr_ref, br_ref, wi_ref, bi_ref, lam_ref, xbuf, hcar, a_scr, u_scr, h_scr):
    t, wdt = xb.shape
    xbuf[8:t + 8, cols] = xb
    xc = cb_ref[:, cols] + cw_ref[0:1, cols] * xbuf[5:t + 5, cols]
    for j in range(1, B_CONV):
        xc = xc + cw_ref[j:j + 1, cols] * xbuf[5 + j:t + 5 + j, cols]
    xbuf[0:8, cols] = xbuf[t:t + 8, cols]
    emit()

    r_cols, i_cols = [], []
    for c in range(wdt // LANES):
        blk = cols.start // LANES + c
        xcc = xc[:, c * LANES:(c + 1) * LANES].astype(BF16)
        r_cols.append(_dot(xcc, wr_ref[blk]))
        i_cols.append(_dot(xcc, wi_ref[blk]))
    emit()
    r = jax.nn.sigmoid(jnp.concatenate(r_cols, axis=1) + br_ref[:, cols])
    ig = jax.nn.sigmoid(jnp.concatenate(i_cols, axis=1) + bi_ref[:, cols])
    nl = -lam_ref[:, cols]
    softplus = jnp.maximum(nl, 0.0) + jnp.log(1.0 + jnp.exp(-jnp.abs(nl)))
    log_a = -B_C * r * softplus
    a = jnp.exp(log_a)
    om = 1.0 - a * a
    u = om * lax.rsqrt(jnp.maximum(om, 1e-30)) * (ig * xc)
    emit()

    a = a.reshape(t // 8, 8, wdt)
    u = u.reshape(t // 8, 8, wdt)
    r8 = lax.broadcasted_iota(jnp.int32, a.shape, 1)
    for d in (1, 2, 4):
        a_sh = pltpu.roll(a, d, 1)
        u_sh = pltpu.roll(u, d, 1)
        m = r8 >= d
        u = jnp.where(m, a * u_sh + u, u)
        a = jnp.where(m, a * a_sh, a)
    a_scr[:, cols] = a.reshape(t, wdt)
    u_scr[:, cols] = u.reshape(t, wdt)
    emit()

    def body(j, h):
        off = pl.multiple_of(j * 8, 8)
        hh = a_scr[pl.ds(off, 8), cols] * h + u_scr[pl.ds(off, 8), cols]
        h_scr[pl.ds(off, 8), cols] = hh
        return hh[7:8, :]

    h_last = lax.fori_loop(0, t // 8, body, hcar[0:1, cols], unroll=True)
    hcar[0:1, cols] = h_last
    emit()
    return h_scr[:, cols] * _silu(gb)


def _inproj_rglru_kernel(tiles_per_seq, x_ref, g_ref, w_ref, cw_ref, cb_ref, wr_ref, br_ref, wi_ref, bi_ref, lam_ref,
                         z_ref, ob_ref, xbuf, hcar, a_scr, u_scr, h_scr):
    @pl.when(pl.program_id(0) % tiles_per_seq == 0)
    def _():
        xbuf[0:8, :] = jnp.zeros((8, B_WIDTH), F32)
        hcar[...] = jnp.zeros((8, B_WIDTH), F32)

    x = x_ref[...]
    ms = jnp.mean(x * x, axis=-1, keepdims=True)
    xn = (x * lax.rsqrt(ms + EPS) * g_ref[...]).astype(BF16)
    half = B_WIDTH // 2
    proj = {}

    def project(name, hp):
        base = (B_WIDTH if name == "gb" else 0) + hp * half
        proj[name, hp] = _dot(xn, w_ref[:, base:base + half])

    def store_z(c):
        z_ref[:, c:c + 256] = _dot(xn, w_ref[:, EVEN_FUSED + c:EVEN_FUSED + c + 256]).astype(BF16)

    pending = [functools.partial(store_z, c) for c in range(0, z_ref.shape[-1], 256)]
    pending.insert(1, functools.partial(project, "xb", 1))
    pending.insert(3, functools.partial(project, "gb", 1))

    def emit():
        if pending:
            pending.pop(0)()

    project("xb", 0)
    project("gb", 0)
    for hp in range(2):
        cols = slice(hp * half, (hp + 1) * half)
        ob_ref[:, cols] = _rglru_half(cols, proj["xb", hp], proj["gb", hp], emit, cw_ref, cb_ref, wr_ref, br_ref,
                                      wi_ref, bi_ref, lam_ref, xbuf, hcar, a_scr, u_scr, h_scr).astype(BF16)
    while pending:
        emit()


def _inproj_rglru(x2d, gain, w_bf16, seq, conv_w, conv_b, wr_bd, b_r, wi_bd, b_i, lam, tm=512):
    n, d = x2d.shape
    nc = w_bf16.shape[1]
    row = lambda v: v.reshape(1, B_WIDTH)
    full = lambda shp: pl.BlockSpec(shp, lambda i: (0,) * len(shp))
    return pl.pallas_call(
        functools.partial(_inproj_rglru_kernel, seq // tm),
        grid=(n // tm,),
        in_specs=[pl.BlockSpec((tm, d), lambda i: (i, 0)), full((1, d)), full((d, nc)),
                  full((B_CONV, B_WIDTH)), full((1, B_WIDTH)), full((4, LANES, LANES)), full((1, B_WIDTH)),
                  full((4, LANES, LANES)), full((1, B_WIDTH)), full((1, B_WIDTH))],
        out_specs=[pl.BlockSpec((tm, nc - EVEN_FUSED), lambda i: (i, 0)), pl.BlockSpec((tm, B_WIDTH), lambda i: (i, 0))],
        out_shape=[jax.ShapeDtypeStruct((n, nc - EVEN_FUSED), BF16), jax.ShapeDtypeStruct((n, B_WIDTH), BF16)],
        scratch_shapes=[pltpu.VMEM((tm + 8, B_WIDTH), F32), pltpu.VMEM((8, B_WIDTH), F32),
                        pltpu.VMEM((tm, B_WIDTH), F32), pltpu.VMEM((tm, B_WIDTH), F32), pltpu.VMEM((tm, B_WIDTH), F32)],
        compiler_params=_cparams(("arbitrary",)),
        name="inproj_rglru",
    )(x2d, gain.reshape(1, d), w_bf16, conv_w, row(conv_b), wr_bd, row(b_r), wi_bd, row(b_i), row(lam))


def _hgrn_consts():
    c = C_CHUNK
    t = np.arange(c)[:, None]
    s = np.arange(c)[None, :]
    masks = []
    hs = c // 2
    while hs >= 1:
        same = (t // (2 * hs)) == (s // (2 * hs))
        masks.append(same & ((t // hs) % 2 == 1) & ((s // hs) % 2 == 0))
        hs //= 2
    pairs = [np.concatenate(masks[n:n + 2], axis=1) for n in range(0, len(masks), 2)]
    return (s <= t).astype(np.float32), np.stack(pairs).astype(np.float32)


def _split_points(b, level):
    c, wdt = b.shape
    hs = c >> (level + 1)
    blk = 2 * hs
    if blk >= 8:
        return jnp.concatenate([jnp.broadcast_to(b[m * blk + hs - 1:m * blk + hs, :], (blk, wdt))
                                for m in range(c // blk)], axis=0)
    b3 = b.reshape(c // 8, 8, wdt)
    sub = lax.broadcasted_iota(jnp.int32, b3.shape, 1)
    if hs == 2:
        out = jnp.where(sub < 4, jnp.broadcast_to(b3[:, 1:2, :], b3.shape), jnp.broadcast_to(b3[:, 5:6, :], b3.shape))
    else:
        out = jnp.where((sub & 1) == 0, b3, pltpu.roll(b3, 1, 1))
    return out.reshape(c, wdt)


def _hgrn_chunk(lb, q_in, f_in, v_in, g_in, og_ref, mst_ref, msk_ref, st_scr, emit):
    c = C_CHUNK
    npair = msk_ref.shape[0]
    zeros = jnp.zeros((c, LANES), BF16)
    f = lb + (1.0 - lb) * jax.nn.sigmoid(f_in)
    g = jnp.log2(f)
    g_hi, g_lo = _split(g)
    b_all = _dot(mst_ref[...], g_hi) + _dot(mst_ref[...], g_lo)
    emit()
    kk_all = 1.0 - f
    qf_all = _silu(q_in)
    v_all = v_in
    kk_bf, qf_bf = kk_all.astype(BF16), qf_all.astype(BF16)
    heads = range(C_HEADS)
    cols = [slice(h * LANES, (h + 1) * LANES) for h in heads]
    sts = [st_scr[h] for h in heads]
    os = [_dot_nt((qf_all[:, cols[h]] * jnp.exp2(b_all[:, cols[h]])).astype(BF16), sts[h].astype(BF16)) for h in heads]
    att = [[] for _ in heads]
    bms = [_split_points(b_all, l) for l in range(2 * npair)]
    for n in range(npair):
        lhs, rhs = [], []
        for h in heads:
            b = b_all[:, cols[h]]
            qt, kt = [], []
            for l in (2 * n, 2 * n + 1):
                bm = bms[l][:, cols[h]]
                qt.append(qf_bf[:, cols[h]] * jnp.exp2(jnp.minimum(b - bm, 1.0)).astype(BF16))
                kt.append(kk_bf[:, cols[h]] * jnp.exp2(jnp.minimum(bm - b, 1.0)).astype(BF16))
            lhs.append(jnp.concatenate(qt, axis=1))
            rhs.append(jnp.concatenate([jnp.concatenate([kt[0], zeros], axis=1),
                                        jnp.concatenate([zeros, kt[1]], axis=1)], axis=0))
        prods = [_dot_nt(lhs[h], rhs[h]) for h in heads]
        if n == 1:
            emit()
        for h in heads:
            att[h].append(jnp.where(msk_ref[n] > 0.5, prods[h], 0.0).astype(BF16))
    vbs = [v_all[:, cols[h]].astype(BF16) for h in heads]
    intra = [_dot(jnp.concatenate(att[h], axis=1), jnp.concatenate([vbs[h]] * (2 * npair), axis=0)) for h in heads]
    b_last = b_all[c - 1:c, :]
    kd = (kk_all * jnp.exp2(b_last - b_all)).astype(BF16)
    upd = [_dot(v_all[:, cols[h]].T.astype(BF16), kd[:, cols[h]]) for h in heads]
    decay = jnp.exp2(b_last)
    diag = qf_all * kk_all
    outs = []
    for h in heads:
        st_scr[h] = sts[h] * decay[:, cols[h]] + upd[h]
        o = os[h] + intra[h] + jnp.sum(diag[:, cols[h]], axis=-1, keepdims=True) * v_all[:, cols[h]]
        ms = jnp.mean(o * o, axis=-1, keepdims=True)
        outs.append(o * lax.rsqrt(ms + EPS) * og_ref[...])
    return jnp.concatenate(outs, axis=1) * _silu(g_in)


def _inproj_hgrn_kernel(tiles_per_seq, x_ref, g_ref, w_ref, lb_ref, og_ref, mst_ref, msk_ref,
                        z_ref, kc_ref, vc_ref, oc_ref, st_scr):
    @pl.when(pl.program_id(0) % tiles_per_seq == 0)
    def _():
        st_scr[...] = jnp.zeros(st_scr.shape, F32)

    x = x_ref[...]
    ms = jnp.mean(x * x, axis=-1, keepdims=True)
    xn = (x * lax.rsqrt(ms + EPS) * g_ref[...]).astype(BF16)
    own = [_dot(xn, w_ref[:, k * C_WIDTH:(k + 1) * C_WIDTH]) for k in range(4)]

    o_refs = (z_ref, kc_ref, vc_ref)
    starts = np.cumsum([0] + [o.shape[-1] for o in o_refs])
    pending = list(range(0, int(starts[-1]), 256))

    def emit():
        if not pending:
            return
        c = pending.pop(0)
        w = min(256, int(starts[-1]) - c)
        r = _dot(xn, w_ref[:, ODD_FUSED + c:ODD_FUSED + c + w]).astype(BF16)
        for o_ref, lo, hi in zip(o_refs, starts[:-1], starts[1:]):
            a, b = max(c, int(lo)), min(c + w, int(hi))
            if a < b:
                o_ref[:, a - int(lo):b - int(lo)] = r[:, a - c:b - c]

    p = lb_ref[...]
    pm = jnp.maximum(p[0:1, :], p[1:2, :])
    e0, e1 = jnp.exp(p[0:1, :] - pm), jnp.exp(p[1:2, :] - pm)
    lb = e1 / (e0 + e1)
    for ch in range(x.shape[0] // C_CHUNK):
        rows = slice(ch * C_CHUNK, (ch + 1) * C_CHUNK)
        q_in, f_in, v_in, g_in = (a[rows, :] for a in own)
        oc_ref[rows, :] = _hgrn_chunk(lb, q_in, f_in, v_in, g_in, og_ref, mst_ref, msk_ref, st_scr, emit).astype(BF16)
    while pending:
        emit()


def _inproj_hgrn(x2d, gain, w_bf16, seq, c_lb, c_og, tm=512):
    n, d = x2d.shape
    nc = w_bf16.shape[1]
    assert ODD_FUSED + sum(ODD_SPLIT) == nc
    mst, msk = _hgrn_consts()
    full = lambda shp: pl.BlockSpec(shp, lambda i: (0,) * len(shp))
    widths = ODD_SPLIT + (C_WIDTH,)
    return pl.pallas_call(
        functools.partial(_inproj_hgrn_kernel, seq // tm),
        grid=(n // tm,),
        in_specs=[pl.BlockSpec((tm, d), lambda i: (i, 0)), full((1, d)), full((d, nc)),
                  full(c_lb.shape), full((1, C_HEAD_DIM)), full(mst.shape), full(msk.shape)],
        out_specs=[pl.BlockSpec((tm, w), lambda i: (i, 0)) for w in widths],
        out_shape=[jax.ShapeDtypeStruct((n, w), BF16) for w in widths],
        scratch_shapes=[pltpu.VMEM((C_HEADS, C_HEAD_DIM, C_HEAD_DIM), F32)],
        compiler_params=_cparams(("arbitrary",)),
        name="inproj_hgrn",
    )(x2d, gain.reshape(1, d), w_bf16, c_lb, c_og.reshape(1, C_HEAD_DIM), jnp.asarray(mst, BF16), jnp.asarray(msk, F32))


def _nsa_prep_kernel(kcr_ref, vcr_ref, ks_ref, vs_ref, kw_ref, vw_ref, cos_ref, sin_ref, cosc_ref, sinc_ref,
                     kncmp_ref, knslc_ref, knwin_ref, pek_ref, pev_ref, w1k_ref, w2k_ref, w1v_ref, w2vt_ref,
                     kc_o, vct_o, ks_o, vst_o, kw_o, vwt_o):
    def hidden(xr_ref, pe_ref, w1_ref):
        xr = xr_ref[0].astype(F32)
        top = _dot((xr + pe_ref[0:1, :]).astype(BF16), w1_ref[0])
        bot = _dot((xr + pe_ref[1:2, :]).astype(BF16), w1_ref[1])
        pre = top + pltpu.roll(bot, bot.shape[0] - 1, 0)
        return _silu(pre).astype(BF16)

    kc = _dot(hidden(kcr_ref, pek_ref, w1k_ref), w2k_ref[...])
    kc = _rope(_head_rms(kc, kncmp_ref[...]), cosc_ref[...], sinc_ref[...])
    d0, d1 = _split_kv(kc)
    kc_o[0, 0] = d0.astype(BF16)
    kc_o[0, 1] = d1.astype(BF16)
    vct = _dot_nt(w2vt_ref[...], hidden(vcr_ref, pev_ref, w1v_ref))
    for kv in range(D_KV_HEADS):
        vct_o[0, kv] = _with_ones_row(vct[kv * HEAD_DIM:(kv + 1) * HEAD_DIM, :]).astype(BF16)

    s = ks_ref.shape[1]
    step = 256
    for r0 in range(0, s, step):
        rows = slice(r0, r0 + step)
        cos, sin = cos_ref[rows, :], sin_ref[rows, :]
        ks = _rope(_head_rms(ks_ref[0, rows, :].astype(F32), knslc_ref[...]), cos, sin)
        kw = _rope(_head_rms(kw_ref[0, rows, :].astype(F32), knwin_ref[...]), cos, sin)
        blk = (r0 + _row((step, LANES))) // SEL_LEN
        onehot = jnp.where(_lane((step, LANES)) - HEAD_DIM == blk, MASK_BIG, 0.0)
        for src, dst, fill in ((ks, ks_o, onehot), (kw, kw_o, 0.0)):
            d0, d1 = _split_kv(src, fill)
            dst[0, 0, rows, :] = d0.astype(BF16)
            dst[0, 1, rows, :] = d1.astype(BF16)
        for src, dst in ((vs_ref, vst_o), (vw_ref, vwt_o)):
            vt = src[0, rows, :].astype(F32).T
            for kv in range(D_KV_HEADS):
                dst[0, kv, :, rows] = _with_ones_row(vt[kv * HEAD_DIM:(kv + 1) * HEAD_DIM, :]).astype(BF16)


def _nsa_prep(z, kcr, vcr, cos, sin_s, cosc, sinc, kn_cmp, kn_slc, kn_win, pek, pev, w1k, w2k, w1v, w2v):
    b, s, _ = z.shape
    nseg = s // CMP_STRIDE
    zb = lambda name: pl.BlockSpec((1, s, 128), lambda bi, o=ODD_OFF[name][0] // 128: (bi, 0, o))
    full = lambda shp: pl.BlockSpec(shp, lambda bi: (0,) * len(shp))
    seg = pl.BlockSpec((1, nseg, 2048), lambda bi: (bi, 0, 0))
    g2 = lambda g: jnp.tile(g, 2).reshape(1, 128)
    dup = lambda n: (jax.ShapeDtypeStruct((b, 2, n, 128), BF16), pl.BlockSpec((1, 2, n, 128), lambda bi: (bi, 0, 0, 0)))
    tr = lambda n: (jax.ShapeDtypeStruct((b, 2, VT_ROWS, n), BF16),
                    pl.BlockSpec((1, 2, VT_ROWS, n), lambda bi: (bi, 0, 0, 0)))
    outs = [dup(nseg), tr(nseg), dup(s), tr(s), dup(s), tr(s)]
    return pl.pallas_call(
        _nsa_prep_kernel,
        grid=(b,),
        in_specs=[seg, seg, zb("ksd"), zb("vsd"), zb("kwd"), zb("vwd"),
                  full((s, 128)), full((s, 128)), full((nseg, 128)), full((nseg, 128)),
                  full((1, 128)), full((1, 128)), full((1, 128)), full((2, 2048)), full((2, 2048)),
                  full((2, 2048, 256)), full((256, 128)), full((2, 2048, 256)), full((128, 256))],
        out_specs=[o[1] for o in outs],
        out_shape=[o[0] for o in outs],
        compiler_params=_cparams(("arbitrary",)),
        name="nsa_prep",
    )(kcr, vcr, z, z, z, z, cos, sin_s, cosc, sinc, g2(kn_cmp), g2(kn_slc), g2(kn_win), pek, pev, w1k, w2k, w1v, w2v)


SEL_CHUNK = 512
SEL_PIECE = 512


def _nsa_kernel(bound_ref, *refs):
    small = jnp.maximum(jnp.maximum(bound_ref[0], bound_ref[1]), bound_ref[2]) <= MAX_SAFE_BOUND

    @pl.when(small)
    def _():
        _nsa_body((bound_ref[0], bound_ref[1], bound_ref[2]), *refs)

    @pl.when(jnp.logical_not(small))
    def _():
        _nsa_body((None, None, None), *refs)


def _nsa_body(fixed, q_ref, gd_ref, gate_ref, cos_ref, sin_ref, qn_ref, kc_ref, vct_ref, ks_ref, vst_ref, kw_ref, vwt_ref,
              ovlt_ref, o_ref):
    fix_c, fix_s, fix_w = fixed
    i = pl.program_id(1)
    t = q_ref.shape[1]
    grp = D_HEADS // D_KV_HEADS
    w = grp * t
    kvs = range(D_KV_HEADS)
    q = _rope(_head_rms(q_ref[0].astype(F32), qn_ref[...]), cos_ref[...], sin_ref[...]) * Q_SCALE
    stack = lambda heads: [jnp.concatenate(heads[kv * grp:(kv + 1) * grp], axis=0) for kv in kvs]
    qs = stack(_q_heads(q))
    tq1 = i * t + _lane((1, t))

    nwin = D_WINDOW + t
    win0 = pl.multiple_of(jnp.maximum(i * t + t - nwin, 0), t)
    s_cmp = [_dot_nt(kc_ref[0, kv], qs[kv]) for kv in kvs]
    s_win = [_dot_nt(kw_ref[0, kv, pl.ds(win0, nwin), :], qs[kv]) for kv in kvs]

    ncmp = kc_ref.shape[2] - 1
    nsel = ovlt_ref.shape[0]
    crow = _row((kc_ref.shape[2], t))
    bias_c = _tile4(jnp.where((crow * CMP_STRIDE + (CMP_LEN - 1) <= tq1) & (crow < ncmp), 0.0, NEG_INF))
    row_ok = _tile4(tq1 >= CMP_LEN - 1)
    es = [_exp_cols(s_cmp[kv] + bias_c, fixed=fix_c)[0] for kv in kvs]
    accs = [_dot(vct_ref[0, kv], es[kv]) for kv in kvs]
    invs = [jnp.where(row_ok, 1.0 / accs[kv][HEAD_DIM:HEAD_DIM + 1, :], 0.0) for kv in kvs]
    oc_t = [accs[kv][0:HEAD_DIM, :] * invs[kv] for kv in kvs]
    psums = []
    for kv in kvs:
        p = es[kv].astype(F32) * invs[kv]
        psums.append(p[:, 0:t] + p[:, t:2 * t] + p[:, 2 * t:3 * t] + p[:, 3 * t:4 * t])
    imps = [_dot_split_rhs(ovlt_ref[...], psums[kv]) for kv in kvs]

    rel = tq1 - (win0 + _row((nwin, t)))
    bias_w = _tile4(jnp.where((rel >= 0) & (rel < D_WINDOW), 0.0, NEG_INF))
    ow_t = [_finish(_dot(vwt_ref[0, kv, :, pl.ds(win0, nwin)], _exp_cols(s_win[kv] + bias_w, fixed=fix_w)[0]))
            for kv in kvs]

    jrow = _row((nsel, t))
    jrow_f = jrow.astype(F32)
    cur = (i * t + _lane((nsel, t))) // SEL_LEN
    forced = (jrow == 0) | (jrow == cur)
    fills = []
    for kv in kvs:
        score = jnp.where(forced, POS_INF, jnp.where(jrow <= cur, imps[kv], NEG_INF))
        chosen = jnp.zeros((nsel, t), F32)
        for _ in range(min(SEL_TOPK, nsel)):
            mx = jnp.max(score, axis=0, keepdims=True)
            first = jnp.min(jnp.where(score == mx, jrow_f, 1e9), axis=0, keepdims=True)
            hit = jrow_f == first
            chosen = jnp.where(hit, 1.0, chosen)
            score = jnp.where(hit, -3e38, score)
        frame = jnp.concatenate([jnp.zeros((HEAD_DIM, t), F32), chosen - 1.0,
                                 jnp.zeros((LANES - HEAD_DIM - nsel, t), F32)], axis=0)
        fills.extend([frame.T] * (grp // 2))

    init = (jnp.full((1, w), NEG_INF, F32), jnp.zeros((VT_ROWS, w), F32))
    pc = SEL_PIECE

    def sweep(states, k_ref, vt_ref, queries, offs, biases, n):
        scores = [[_dot_nt(k_ref[0, kv, pl.ds(off, n), :], queries[kv]) for kv in kvs] for off in offs]
        for off, sc, bias in zip(offs, scores, biases):
            states = tuple(_online_update(states[kv], sc[kv] if bias is None else sc[kv] + bias,
                                          vt_ref[0, kv, :, pl.ds(off, n)], fix_s) for kv in kvs)
        return states

    qsel = stack(_q_heads(q, fills))
    ch = SEL_CHUNK
    own = pl.multiple_of(((i * t) // ch) * ch, ch)
    st = sweep((init, init), ks_ref, vst_ref, qsel, [own],
               [_tile4(jnp.where(own + _row((ch, t)) <= tq1, 0.0, NEG_INF))], ch)

    def sel_chunk(c, states):
        base = pl.multiple_of(c * ch, ch)
        return sweep(states, ks_ref, vst_ref, qsel, [base + n * pc for n in range(ch // pc)], [None] * (ch // pc), pc)

    st = lax.fori_loop(0, (i * t) // ch, sel_chunk, st)
    os_t = [_finish(acc) for (_, acc) in st]

    g_t = jax.nn.sigmoid(gate_ref[0].astype(F32)).T
    rows = []
    for h in range(D_HEADS):
        kv, g = divmod(h, grp)
        cols = slice(g * t, (g + 1) * t)
        r = D_BRANCHES * h
        rows.append(g_t[r:r + 1, :] * oc_t[kv][:, cols] + g_t[r + 1:r + 2, :] * os_t[kv][:, cols]
                    + g_t[r + 2:r + 3, :] * ow_t[kv][:, cols])
    o = jnp.concatenate(rows, axis=0).T
    o_ref[0] = (o * _silu(gd_ref[0].astype(F32))).astype(BF16)


def _overlap_t(s):
    ncmp = (s - CMP_LEN) // CMP_STRIDE + 1
    nsel = s // SEL_LEN
    cs = np.arange(ncmp)[None, :] * CMP_STRIDE
    ss = np.arange(nsel)[:, None] * SEL_LEN
    ovl = np.zeros((nsel, s // CMP_STRIDE), np.float32)
    ovl[:, :ncmp] = (cs < ss + SEL_LEN) & (cs + CMP_LEN > ss)
    return ovl


def _nsa(z, cos, sin_s, qn, bounds, kc, vct, ks, vst, kw, vwt, t=256):
    b, s, _ = z.shape
    nseg = s // CMP_STRIDE
    assert nseg == LANES, "compressed-block scores are laid out on one 128-row tile"
    ovlt = _overlap_t(s)
    full = lambda shp: pl.BlockSpec(shp, lambda bi, i: (0,) * len(shp))
    dup = lambda n: pl.BlockSpec((1, 2, n, 128), lambda bi, i: (bi, 0, 0, 0))
    tr = lambda n: pl.BlockSpec((1, 2, VT_ROWS, n), lambda bi, i: (bi, 0, 0, 0))
    return pl.pallas_call(
        _nsa_kernel,
        grid=(b, s // t),
        in_specs=[pl.BlockSpec(memory_space=pltpu.SMEM),
                  pl.BlockSpec((1, t, 512), lambda bi, i: (bi, i, ODD_OFF["qd"][0] // 512)),
                  pl.BlockSpec((1, t, 512), lambda bi, i: (bi, i, ODD_OFF["gd"][0] // 512)),
                  pl.BlockSpec((1, t, 128), lambda bi, i: (bi, i, ODD_OFF["gate"][0] // 128)),
                  pl.BlockSpec((t, 128), lambda bi, i: (i, 0)),
                  pl.BlockSpec((t, 128), lambda bi, i: (i, 0)),
                  full((1, 512)), dup(nseg), tr(nseg), dup(s), tr(s), dup(s), tr(s), full(ovlt.shape)],
        out_specs=pl.BlockSpec((1, t, 512), lambda bi, i: (bi, i, 0)),
        out_shape=jax.ShapeDtypeStruct((b, s, 512), BF16),
        compiler_params=_cparams(("arbitrary", "arbitrary")),
        name="nsa",
    )(bounds, z, z, z, cos, sin_s, jnp.tile(qn, 8).reshape(1, 512), kc, vct, ks, vst, kw, vwt, jnp.asarray(ovlt, BF16))


def _permute_cols(w, order, src):
    cols = []
    for name, width in order:
        o, sw = src[name]
        blk = w[:, o:o + sw]
        if sw < width:
            blk = jnp.pad(blk, ((0, 0), (0, width - sw)))
        cols.append(blk)
    return jnp.concatenate(cols, axis=1).astype(BF16)


def _block_diag_pairs(w):
    z = jnp.zeros((4, LANES, LANES), w.dtype)
    z = z.at[:, 0:64, 0:64].set(w[0::2])
    z = z.at[:, 64:128, 64:128].set(w[1::2])
    return z.astype(BF16)


def _rope_tables(pos):
    half = HEAD_DIM // 2
    inv = ROPE_THETA ** (-jnp.arange(half, dtype=F32) / half)
    ang = pos.astype(F32)[:, None] * inv[None, :]
    cos, sin = jnp.cos(ang), jnp.sin(ang)
    cos_t = jnp.tile(cos, (1, 4))
    sin_t = jnp.tile(jnp.concatenate([-sin, sin], axis=1), (1, 2))
    return cos_t, sin_t


def _expand_compress_w1(w1):
    hdim = w1.shape[1]
    w = w1.reshape(2, CMP_STRIDE, 1, HEAD_DIM, 1, hdim)
    same_head = jnp.eye(2, dtype=w1.dtype).reshape(1, 1, 2, 1, 2, 1)
    return (w * same_head).reshape(2, CMP_STRIDE * 2 * HEAD_DIM, 2 * hdim).astype(BF16)


def _expand_compress_w2(w2):
    hdim, hd = w2.shape
    z = jnp.zeros((2 * hdim, 2 * hd), w2.dtype)
    z = z.at[0:hdim, 0:hd].set(w2)
    z = z.at[hdim:, hd:].set(w2)
    return z.astype(BF16)


def _expand_pe(pe):
    p = pe.reshape(2, CMP_STRIDE, 1, HEAD_DIM)
    return jnp.broadcast_to(p, (2, CMP_STRIDE, 2, HEAD_DIM)).reshape(2, CMP_STRIDE * 2 * HEAD_DIM)


def _even_layer(h, mem, g, mem_g, w_mem_kv, m_qn, m_kn, w_in, w_out, a_qn, a_kn, a_sinks,
                conv_w, conv_b, w_r, b_r, w_i, b_i, lam, cos, sin_s):
    b, s, d = h.shape
    h2 = h.reshape(b * s, d)
    z, ob = _inproj_rglru(h2, g, _permute_cols(w_in, EVEN_ORDER, EVEN_SRC), s, conv_w, conv_b,
                          _block_diag_pairs(w_r), b_r, _block_diag_pairs(w_i), b_i, lam)
    z = z.reshape(b, s, EVEN_COLS)
    oa = _swa(z, cos, sin_s, a_qn, a_kn, a_sinks)
    om = _mem_attention(z, EVEN_OFF["qm"][0], EVEN_OFF["gm"][0], mem, mem_g, w_mem_kv.astype(BF16), m_qn, m_kn)
    out = _outproj(h2, oa.reshape(b * s, 512), ob, om.reshape(b * s, 256), w_out.astype(BF16))
    return out.reshape(b, s, d)


def _odd_layer(h, mem, g, mem_g, w_mem_kv, m_qn, m_kn, w_in, w_out, c_lb, c_og,
               d_qn, d_kn_cmp, d_kn_slc, d_kn_win, pe_k, pe_v, w1k, w2k, w1v, w2v, cos, sin_s, cosc, sinc):
    b, s, d = h.shape
    h2 = h.reshape(b * s, d)
    z, kcd, vcd, oc = _inproj_hgrn(h2, g, _permute_cols(w_in, ODD_ORDER, ODD_SRC), s, c_lb, c_og)
    z = z.reshape(b, s, ODD_SPLIT[0])
    nseg = s // CMP_STRIDE
    seg = lambda a: a.reshape(b, nseg, CMP_STRIDE * 128)
    kc, vc, ks, vs, kw, vw = _nsa_prep(
        z, seg(kcd), seg(vcd), cos, sin_s, cosc, sinc, d_kn_cmp, d_kn_slc, d_kn_win,
        _expand_pe(pe_k), _expand_pe(pe_v), _expand_compress_w1(w1k), _expand_compress_w2(w2k),
        _expand_compress_w1(w1v), _expand_compress_w2(w2v).T)
    bounds = jnp.stack([_score_bound(d_qn, kn) for kn in (d_kn_cmp, d_kn_slc, d_kn_win)]).astype(F32)
    od = _nsa(z, cos, sin_s, d_qn, bounds, kc, vc, ks, vs, kw, vw)
    om = _mem_attention(z, ODD_OFF["qm"][0], ODD_OFF["gm"][0], mem, mem_g, w_mem_kv.astype(BF16), m_qn, m_kn)
    out = _outproj(h2, oc, od.reshape(b * s, 512), om.reshape(b * s, 256), w_out.astype(BF16))
    return out.reshape(b, s, d)


def kernel(x, mem, norm_g, mem_norm_g, mem_w_kv, mem_qn, mem_kn, ev_w_in, ev_w_out, a_qn, a_kn, a_sinks,
           b_conv_w, b_conv_b, b_w_r, b_b_r, b_w_i, b_b_i, b_lambda, od_w_in, od_w_out, c_lb, c_onorm,
           d_qn, d_kn_cmp, d_kn_slc, d_kn_win, d_pe_k, d_pe_v, d_w1k, d_w2k, d_w1v, d_w2v):
    depth = norm_g.shape[0]
    assert depth == 2 and c_lb.shape[0] == 2, "the HGRN2 lower-bound formula in the kernel is written for depth 2"
    s = x.shape[1]
    assert s % 256 == 0 and s >= D_WINDOW
    pos = jnp.arange(s)
    cos, sin_s = _rope_tables(pos)
    nseg = s // CMP_STRIDE
    cmp_end = jnp.minimum(jnp.arange(nseg) * CMP_STRIDE + CMP_LEN - 1, s - 1)
    cosc, sinc = _rope_tables(cmp_end)
    h = _even_layer(x, mem, norm_g[0], mem_norm_g[0], mem_w_kv[0], mem_qn[0], mem_kn[0], ev_w_in[0], ev_w_out[0],
                    a_qn[0], a_kn[0], a_sinks[0], b_conv_w[0], b_conv_b[0], b_w_r[0], b_b_r[0], b_w_i[0], b_b_i[0],
                    b_lambda[0], cos, sin_s)
    h = _odd_layer(h, mem, norm_g[1], mem_norm_g[1], mem_w_kv[1], mem_qn[1], mem_kn[1], od_w_in[0], od_w_out[0],
                   c_lb, c_onorm[0], d_qn[0], d_kn_cmp[0], d_kn_slc[0], d_kn_win[0], d_pe_k[0], d_pe_v[0],
                   d_w1k[0], d_w2k[0], d_w1v[0], d_w2v[0], cos, sin_s, cosc, sinc)
    return h
```

```python
import functools

import numpy as np
import jax
import jax.numpy as jnp
from jax import lax
from jax.experimental import pallas as pl
from jax.experimental.pallas import tpu as pltpu

F32 = jnp.float32
BF16 = jnp.bfloat16

D_MODEL = 1024
N_MEM = 256
HEAD_DIM = 64
ROPE_THETA = 10000.0
EPS = 1e-6
NEG_INF = -1e30
POS_INF = 1e30
MASK_BIG = 1e30
LANES = 128

A_HEADS, A_KV_HEADS, A_WINDOW = 8, 2, 128
B_WIDTH, B_BLOCKS, B_CONV, B_C = 512, 8, 4, 8.0
M_HEADS = 4
C_HEADS, C_HEAD_DIM, C_CHUNK = 4, 128, 64
C_WIDTH = C_HEADS * C_HEAD_DIM
D_HEADS, D_KV_HEADS = 8, 2
CMP_LEN, CMP_STRIDE, CMP_HIDDEN = 32, 16, 128
SEL_LEN, SEL_TOPK = 64, 4
D_WINDOW = 512
D_BRANCHES = 3
SCALE = HEAD_DIM ** -0.5
LOG2E = 1.4426950408889634
Q_SCALE = SCALE * LOG2E

EVEN_ORDER = [("xb", 512), ("gb", 512), ("qa", 512), ("ga", 512), ("qm", 256), ("gm", 256), ("ka", 128), ("va", 128)]
EVEN_FUSED = 1024
EVEN_SRC = {"qa": (0, 512), "ka": (512, 128), "va": (640, 128), "ga": (768, 512), "xb": (1280, 512),
            "gb": (1792, 512), "qm": (2304, 256), "gm": (2560, 256)}
ODD_ORDER = [("qc", 512), ("fc", 512), ("ic", 512), ("gc", 512), ("qd", 512), ("gd", 512), ("qm", 256), ("gm", 256),
             ("ksd", 128), ("vsd", 128), ("kwd", 128), ("vwd", 128), ("gate", 128), ("kcd", 128), ("vcd", 128)]
ODD_FUSED = 2048
ODD_SPLIT = (2176, 128, 128)
ODD_SRC = {"qc": (0, 512), "fc": (512, 512), "ic": (1024, 512), "gc": (1536, 512), "qd": (2048, 512),
           "kcd": (2560, 128), "vcd": (2688, 128), "ksd": (2816, 128), "vsd": (2944, 128), "kwd": (3072, 128),
           "vwd": (3200, 128), "gate": (3328, 24), "gd": (3352, 512), "qm": (3864, 256), "gm": (4120, 256)}

VMEM_LIMIT = 48 * 1024 * 1024


def _offsets(order):
    off, out = 0, {}
    for name, w in order:
        out[name] = (off, w)
        off += w
    return out, off


EVEN_OFF, EVEN_COLS = _offsets(EVEN_ORDER[2:])
ODD_OFF, ODD_COLS = _offsets(ODD_ORDER[4:])


def _cparams(sem):
    return pltpu.CompilerParams(dimension_semantics=sem, vmem_limit_bytes=VMEM_LIMIT)


def _dot(a, b):
    return jnp.dot(a, b, preferred_element_type=F32)


def _dot_nt(a, b):
    return lax.dot_general(a, b, (((1,), (1,)), ((), ())), preferred_element_type=F32)


def _split(x):
    hi = x.astype(BF16)
    lo = (x - hi.astype(F32)).astype(BF16)
    return hi, lo


def _dot_split_lhs(x, m):
    hi, lo = _split(x)
    return _dot(hi, m) + _dot(lo, m)


def _dot_split_rhs(m, x):
    hi, lo = _split(x)
    return _dot(m, hi) + _dot(m, lo)


def _lane(shape):
    return lax.broadcasted_iota(jnp.int32, shape, len(shape) - 1)


def _row(shape):
    return lax.broadcasted_iota(jnp.int32, shape, len(shape) - 2)


def _silu(x):
    return x * jax.nn.sigmoid(x)


def _seg_ones():
    r = lax.broadcasted_iota(jnp.int32, (LANES, LANES), 0) >> 6
    c = lax.broadcasted_iota(jnp.int32, (LANES, LANES), 1) >> 6
    return jnp.where(r == c, 1.0, 0.0).astype(BF16)


def _head_rms(x, gain):
    seg = _seg_ones()
    cols = []
    for c in range(x.shape[1] // LANES):
        xc = x[:, c * LANES:(c + 1) * LANES]
        ms = _dot_split_lhs(xc * xc, seg) * (1.0 / HEAD_DIM)
        cols.append(xc * lax.rsqrt(ms + EPS))
    y = cols[0] if len(cols) == 1 else jnp.concatenate(cols, axis=1)
    return y * gain


def _rope(x, cos, sin_s):
    first = (_lane((x.shape[0], LANES)) & 63) < 32
    cols = []
    for c in range(x.shape[1] // LANES):
        xc = x[:, c * LANES:(c + 1) * LANES]
        sw = jnp.where(first, pltpu.roll(xc, 96, 1), pltpu.roll(xc, 32, 1))
        cols.append(xc * cos + sw * sin_s)
    return cols[0] if len(cols) == 1 else jnp.concatenate(cols, axis=1)


def _split_kv(k, fill=0.0):
    lo = _lane(k.shape) < 64
    return jnp.where(lo, k, fill), jnp.where(lo, pltpu.roll(k, 64, 1), fill)


def _q_heads(q, fills=None):
    lo = _lane((q.shape[0], LANES)) < 64
    out = []
    for c in range(q.shape[1] // LANES):
        qc = q[:, c * LANES:(c + 1) * LANES]
        fill = 0.0 if fills is None else fills[c]
        out.append(jnp.where(lo, qc, fill).astype(BF16))
        out.append(jnp.where(lo, pltpu.roll(qc, 64, 1), fill).astype(BF16))
    return out


VT_ROWS = HEAD_DIM + 16


def _with_ones_row(vt):
    n = vt.shape[1]
    pad = jnp.where(_row((VT_ROWS - HEAD_DIM, n)) == 0, 1.0, 0.0)
    return jnp.concatenate([vt, pad], axis=0)


MAX_SAFE_BOUND = 60.0


def _score_bound(q_gain, k_gain):
    return 1.02 * HEAD_DIM * SCALE * LOG2E * jnp.max(jnp.abs(q_gain)) * jnp.max(jnp.abs(k_gain))


def _with_fixed_reference(bound, body):
    small = bound <= MAX_SAFE_BOUND

    @pl.when(small)
    def _():
        body(bound)

    @pl.when(jnp.logical_not(small))
    def _():
        body(None)


def _exp_cols(s, extra=None, fixed=None):
    if fixed is not None:
        return jnp.exp2(s - fixed).astype(BF16), fixed
    m = jnp.max(s, axis=0, keepdims=True)
    if extra is not None:
        m = jnp.maximum(m, extra)
    return jnp.exp2(s - m).astype(BF16), m


def _finish(acc, extra_den=None):
    den = acc[HEAD_DIM:HEAD_DIM + 1, :]
    if extra_den is not None:
        den = den + extra_den
    return acc[0:HEAD_DIM, :] * (1.0 / den)


def _online_update(state, s, vt_tile, fixed=None):
    m, acc = state
    if fixed is not None:
        return m, acc + _dot(vt_tile, _exp_cols(s, fixed=fixed)[0])
    p, m_new = _exp_cols(s, m)
    acc = jnp.exp2(m - m_new) * acc + _dot(vt_tile, p)
    return m_new, acc


def _tile4(x):
    return jnp.concatenate([x, x, x, x], axis=1)


def _outproj_mem_kernel(tiles_per_seq, bound_ref, h_ref, a_ref, b_ref, q_ref, gm_ref, mem_ref, mg_ref, wkv_ref,
                        qn_ref, kn_ref, w_ref, o_ref, k_scr, vt_scr, acc_scr):
    @pl.when(pl.program_id(0) % tiles_per_seq == 0)
    def _():
        m = mem_ref[0]
        ms = jnp.mean(m * m, axis=-1, keepdims=True)
        mn = (m * lax.rsqrt(ms + EPS) * mg_ref[...]).astype(BF16)
        kv = _dot(mn, wkv_ref[...])
        km = _head_rms(kv[:, 0:256], kn_ref[...])
        for c in range(2):
            k0, k1 = _split_kv(km[:, c * LANES:(c + 1) * LANES])
            k_scr[2 * c] = k0.astype(BF16)
            k_scr[2 * c + 1] = k1.astype(BF16)
        vt = kv[:, 256:512].T
        for h in range(M_HEADS):
            vt_scr[h] = _with_ones_row(vt[h * HEAD_DIM:(h + 1) * HEAD_DIM, :]).astype(BF16)

    def body(fixed):
        a, b = a_ref[...], b_ref[...]
        pending = list(range(0, D_MODEL, 256))

        def emit():
            if pending:
                c = pending.pop(0)
                acc_scr[:, c:c + 256] = (h_ref[:, c:c + 256] + _dot(a, w_ref[0:512, c:c + 256])
                                         + _dot(b, w_ref[512:1024, c:c + 256]))

        q = _head_rms(q_ref[...].astype(F32), qn_ref[...]) * Q_SCALE
        scores = [_dot_nt(k_scr[h], qh) for h, qh in enumerate(_q_heads(q))]
        rows = []
        for h, sc in enumerate(scores):
            emit()
            rows.append(_finish(_dot(vt_scr[h], _exp_cols(sc, fixed=fixed)[0])))
        om = (jnp.concatenate(rows, axis=0).T * _silu(gm_ref[...].astype(F32))).astype(BF16)
        while pending:
            emit()
        for c in range(0, D_MODEL, 256):
            o_ref[:, c:c + 256] = acc_scr[:, c:c + 256] + _dot(om, w_ref[1024:1280, c:c + 256])

    _with_fixed_reference(bound_ref[0], body)


def _outproj_mem(h2d, oa, ob, z2d, q_off, g_off, seq, mem, mem_g, wkv_bf16, qn, kn, w_bf16, tm=1024):
    n, d = h2d.shape
    full = lambda shp: pl.BlockSpec(shp, lambda i: (0,) * len(shp))
    tps = seq // tm
    return pl.pallas_call(
        functools.partial(_outproj_mem_kernel, tps),
        grid=(n // tm,),
        in_specs=[pl.BlockSpec(memory_space=pltpu.SMEM),
                  pl.BlockSpec((tm, d), lambda i: (i, 0)),
                  pl.BlockSpec((tm, 512), lambda i: (i, 0)),
                  pl.BlockSpec((tm, 512), lambda i: (i, 0)),
                  pl.BlockSpec((tm, 256), lambda i: (i, q_off // 256)),
                  pl.BlockSpec((tm, 256), lambda i: (i, g_off // 256)),
                  pl.BlockSpec((1, N_MEM, D_MODEL), lambda i: (i // tps, 0, 0)),
                  full((1, D_MODEL)), full((D_MODEL, 512)), full((1, 256)), full((1, 256)), full((1280, d))],
        out_specs=pl.BlockSpec((tm, d), lambda i: (i, 0)),
        out_shape=jax.ShapeDtypeStruct((n, d), F32),
        scratch_shapes=[pltpu.VMEM((M_HEADS, N_MEM, LANES), BF16), pltpu.VMEM((M_HEADS, VT_ROWS, N_MEM), BF16),
                        pltpu.VMEM((tm, d), F32)],
        compiler_params=_cparams(("arbitrary",)),
        name="outproj_mem",
    )(_score_bound(qn, kn).reshape(1).astype(F32), h2d, oa, ob, z2d, z2d, mem, mem_g.reshape(1, D_MODEL), wkv_bf16,
      jnp.tile(qn, 4).reshape(1, 256), jnp.tile(kn, 4).reshape(1, 256), w_bf16)


def _swa_kernel(bound_ref, *refs):
    _with_fixed_reference(bound_ref[0], functools.partial(_swa_body, refs))


def _swa_body(refs, fixed):
    (q_ref, g_ref, kc_ref, kp_ref, vc_ref, vp_ref, cosc_ref, sinc_ref, cosp_ref, sinp_ref,
     qn_ref, kn_ref, sink_ref, o_ref) = refs
    i = pl.program_id(1)
    t = q_ref.shape[1]
    tp = kp_ref.shape[1]
    grp = A_HEADS // A_KV_HEADS
    q = _rope(_head_rms(q_ref[0].astype(F32), qn_ref[...]), cosc_ref[...], sinc_ref[...]) * Q_SCALE
    qh = _q_heads(q)
    kc = _rope(_head_rms(kc_ref[0].astype(F32), kn_ref[...]), cosc_ref[...], sinc_ref[...])
    kp = _rope(_head_rms(kp_ref[0].astype(F32), kn_ref[...]), cosp_ref[...], sinp_ref[...])
    ks = [x.astype(BF16) for x in _split_kv(jnp.concatenate([kp, kc], axis=0))]
    vt = jnp.concatenate([vp_ref[0].astype(F32).T, vc_ref[0].astype(F32).T], axis=1)
    ts = i * t - tp + _row((tp + t, t))
    rel = i * t + _lane((tp + t, t)) - ts
    bias = _tile4(jnp.where((rel >= 0) & (rel < A_WINDOW) & (ts >= 0), 0.0, NEG_INF))
    scores = [_dot_nt(ks[kv], jnp.concatenate(qh[kv * grp:(kv + 1) * grp], axis=0)) for kv in range(A_KV_HEADS)]
    rows = []
    for kv in range(A_KV_HEADS):
        sink = jnp.concatenate([jnp.broadcast_to(sink_ref[kv * grp + g:kv * grp + g + 1, 0:1], (1, t))
                                for g in range(grp)], axis=1) * LOG2E
        p, m = _exp_cols(scores[kv] + bias, sink, fixed)
        vt_kv = _with_ones_row(vt[kv * HEAD_DIM:(kv + 1) * HEAD_DIM, :]).astype(BF16)
        o_t = _finish(_dot(vt_kv, p), jnp.exp2(sink - m))
        rows.extend(o_t[:, g * t:(g + 1) * t] for g in range(grp))
    o = jnp.concatenate(rows, axis=0).T
    o_ref[0] = (o * _silu(g_ref[0].astype(F32))).astype(BF16)


def _swa(z, cos, sin_s, qn, kn, sinks, t=256):
    b, s, _ = z.shape
    tp = A_WINDOW
    qo, go = EVEN_OFF["qa"][0] // 512, EVEN_OFF["ga"][0] // 512
    ko, vo = EVEN_OFF["ka"][0] // 128, EVEN_OFF["va"][0] // 128
    prev = lambda i: jnp.maximum(i * (t // tp) - 1, 0)
    bound = jnp.maximum(_score_bound(qn, kn), jnp.max(sinks) * LOG2E).reshape(1).astype(F32)
    return pl.pallas_call(
        _swa_kernel,
        grid=(b, s // t),
        in_specs=[pl.BlockSpec(memory_space=pltpu.SMEM),
                  pl.BlockSpec((1, t, 512), lambda bi, i: (bi, i, qo)),
                  pl.BlockSpec((1, t, 512), lambda bi, i: (bi, i, go)),
                  pl.BlockSpec((1, t, 128), lambda bi, i: (bi, i, ko)),
                  pl.BlockSpec((1, tp, 128), lambda bi, i: (bi, prev(i), ko)),
                  pl.BlockSpec((1, t, 128), lambda bi, i: (bi, i, vo)),
                  pl.BlockSpec((1, tp, 128), lambda bi, i: (bi, prev(i), vo)),
                  pl.BlockSpec((t, 128), lambda bi, i: (i, 0)),
                  pl.BlockSpec((t, 128), lambda bi, i: (i, 0)),
                  pl.BlockSpec((tp, 128), lambda bi, i: (prev(i), 0)),
                  pl.BlockSpec((tp, 128), lambda bi, i: (prev(i), 0)),
                  pl.BlockSpec((1, 512), lambda bi, i: (0, 0)),
                  pl.BlockSpec((1, 128), lambda bi, i: (0, 0)),
                  pl.BlockSpec((A_HEADS, 128), lambda bi, i: (0, 0))],
        out_specs=pl.BlockSpec((1, t, 512), lambda bi, i: (bi, i, 0)),
        out_shape=jax.ShapeDtypeStruct((b, s, 512), BF16),
        compiler_params=_cparams(("arbitrary", "arbitrary")),
        name="swa",
    )(bound, z, z, z, z, z, z, cos, sin_s, cos, sin_s, jnp.tile(qn, 8).reshape(1, 512), jnp.tile(kn, 2).reshape(1, 128),
      jnp.broadcast_to(sinks.reshape(A_HEADS, 1), (A_HEADS, 128)))


def _rglru_half(cols, xb, gb, emit, cw_ref, cb_ref, wr_ref, br_ref, wi_ref, bi_ref, lam_ref, xbuf, hcar, a_scr, u_scr, h_scr):
    t, wdt = xb.shape
    xbuf[8:t + 8, cols] = xb
    xc = cb_ref[:, cols] + cw_ref[0:1, cols] * xbuf[5:t + 5, cols]
    for j in range(1, B_CONV):
        xc = xc + cw_ref[j:j + 1, cols] * xbuf[5 + j:t + 5 + j, cols]
    xbuf[0:8, cols] = xbuf[t:t + 8, cols]
    emit()

    r_cols, i_cols = [], []
    for c in range(wdt // LANES):
        blk = cols.start // LANES + c
        xcc = xc[:, c * LANES:(c + 1) * LANES].astype(BF16)
        r_cols.append(_dot(xcc, wr_ref[blk]))
        i_cols.append(_dot(xcc, wi_ref[blk]))
    emit()
    r = jax.nn.sigmoid(jnp.concatenate(r_cols, axis=1) + br_ref[:, cols])
    ig = jax.nn.sigmoid(jnp.concatenate(i_cols, axis=1) + bi_ref[:, cols])
    nl = -lam_ref[:, cols]
    softplus = jnp.maximum(nl, 0.0) + jnp.log(1.0 + jnp.exp(-jnp.abs(nl)))
    log_a = -B_C * r * softplus
    a = jnp.exp(log_a)
    om = 1.0 - a * a
    u = om * lax.rsqrt(jnp.maximum(om, 1e-30)) * (ig * xc)
    emit()

    a = a.reshape(t // 8, 8, wdt)
    u = u.reshape(t // 8, 8, wdt)
    r8 = lax.broadcasted_iota(jnp.int32, a.shape, 1)
    for d in (1, 2, 4):
        a_sh = pltpu.roll(a, d, 1)
        u_sh = pltpu.roll(u, d, 1)
        m = r8 >= d
        u = jnp.where(m, a * u_sh + u, u)
        a = jnp.where(m, a * a_sh, a)
    a_scr[:, cols] = a.reshape(t, wdt)
    u_scr[:, cols] = u.reshape(t, wdt)
    emit()

    def body(j, h):
        off = pl.multiple_of(j * 8, 8)
        hh = a_scr[pl.ds(off, 8), cols] * h + u_scr[pl.ds(off, 8), cols]
        h_scr[pl.ds(off, 8), cols] = hh
        return hh[7:8, :]

    h_last = lax.fori_loop(0, t // 8, body, hcar[0:1, cols], unroll=True)
    hcar[0:1, cols] = h_last
    emit()
    return h_scr[:, cols] * _silu(gb)


def _inproj_rglru_kernel(tiles_per_seq, x_ref, g_ref, w_ref, cw_ref, cb_ref, wr_ref, br_ref, wi_ref, bi_ref, lam_ref,
                         z_ref, ob_ref, xbuf, hcar, a_scr, u_scr, h_scr):
    @pl.when(pl.program_id(0) % tiles_per_seq == 0)
    def _():
        xbuf[0:8, :] = jnp.zeros((8, B_WIDTH), F32)
        hcar[...] = jnp.zeros((8, B_WIDTH), F32)

    x = x_ref[...]
    ms = jnp.mean(x * x, axis=-1, keepdims=True)
    xn = (x * lax.rsqrt(ms + EPS) * g_ref[...]).astype(BF16)
    half = B_WIDTH // 2
    proj = {}

    def project(name, hp):
        base = (B_WIDTH if name == "gb" else 0) + hp * half
        proj[name, hp] = _dot(xn, w_ref[:, base:base + half])

    def store_z(c):
        z_ref[:, c:c + 256] = _dot(xn, w_ref[:, EVEN_FUSED + c:EVEN_FUSED + c + 256]).astype(BF16)

    pending = [functools.partial(store_z, c) for c in range(0, z_ref.shape[-1], 256)]
    pending.insert(1, functools.partial(project, "xb", 1))
    pending.insert(3, functools.partial(project, "gb", 1))

    def emit():
        if pending:
            pending.pop(0)()

    project("xb", 0)
    project("gb", 0)
    for hp in range(2):
        cols = slice(hp * half, (hp + 1) * half)
        ob_ref[:, cols] = _rglru_half(cols, proj["xb", hp], proj["gb", hp], emit, cw_ref, cb_ref, wr_ref, br_ref,
                                      wi_ref, bi_ref, lam_ref, xbuf, hcar, a_scr, u_scr, h_scr).astype(BF16)
    while pending:
        emit()


def _inproj_rglru(x2d, gain, w_bf16, seq, conv_w, conv_b, wr_bd, b_r, wi_bd, b_i, lam, tm=512):
    n, d = x2d.shape
    nc = w_bf16.shape[1]
    row = lambda v: v.reshape(1, B_WIDTH)
    full = lambda shp: pl.BlockSpec(shp, lambda i: (0,) * len(shp))
    return pl.pallas_call(
        functools.partial(_inproj_rglru_kernel, seq // tm),
        grid=(n // tm,),
        in_specs=[pl.BlockSpec((tm, d), lambda i: (i, 0)), full((1, d)), full((d, nc)),
                  full((B_CONV, B_WIDTH)), full((1, B_WIDTH)), full((4, LANES, LANES)), full((1, B_WIDTH)),
                  full((4, LANES, LANES)), full((1, B_WIDTH)), full((1, B_WIDTH))],
        out_specs=[pl.BlockSpec((tm, nc - EVEN_FUSED), lambda i: (i, 0)), pl.BlockSpec((tm, B_WIDTH), lambda i: (i, 0))],
        out_shape=[jax.ShapeDtypeStruct((n, nc - EVEN_FUSED), BF16), jax.ShapeDtypeStruct((n, B_WIDTH), BF16)],
        scratch_shapes=[pltpu.VMEM((tm + 8, B_WIDTH), F32), pltpu.VMEM((8, B_WIDTH), F32),
                        pltpu.VMEM((tm, B_WIDTH), F32), pltpu.VMEM((tm, B_WIDTH), F32), pltpu.VMEM((tm, B_WIDTH), F32)],
        compiler_params=_cparams(("arbitrary",)),
        name="inproj_rglru",
    )(x2d, gain.reshape(1, d), w_bf16, conv_w, row(conv_b), wr_bd, row(b_r), wi_bd, row(b_i), row(lam))


def _hgrn_consts():
    c = C_CHUNK
    t = np.arange(c)[:, None]
    s = np.arange(c)[None, :]
    masks = []
    hs = c // 2
    while hs >= 1:
        same = (t // (2 * hs)) == (s // (2 * hs))
        masks.append(same & ((t // hs) % 2 == 1) & ((s // hs) % 2 == 0))
        hs //= 2
    pairs = [np.concatenate(masks[n:n + 2], axis=1) for n in range(0, len(masks), 2)]
    return (s <= t).astype(np.float32), np.stack(pairs).astype(np.float32)


def _split_points(b, level):
    c, wdt = b.shape
    hs = c >> (level + 1)
    blk = 2 * hs
    if blk >= 8:
        return jnp.concatenate([jnp.broadcast_to(b[m * blk + hs - 1:m * blk + hs, :], (blk, wdt))
                                for m in range(c // blk)], axis=0)
    b3 = b.reshape(c // 8, 8, wdt)
    sub = lax.broadcasted_iota(jnp.int32, b3.shape, 1)
    if hs == 2:
        out = jnp.where(sub < 4, jnp.broadcast_to(b3[:, 1:2, :], b3.shape), jnp.broadcast_to(b3[:, 5:6, :], b3.shape))
    else:
        out = jnp.where((sub & 1) == 0, b3, pltpu.roll(b3, 1, 1))
    return out.reshape(c, wdt)


def _hgrn_chunk(lb, q_in, f_in, v_in, g_in, og_ref, mst_ref, msk_ref, st_scr, emit):
    c = C_CHUNK
    npair = msk_ref.shape[0]
    zeros = jnp.zeros((c, LANES), BF16)
    f = lb + (1.0 - lb) * jax.nn.sigmoid(f_in)
    g = jnp.log2(f)
    g_hi, g_lo = _split(g)
    b_all = _dot(mst_ref[...], g_hi) + _dot(mst_ref[...], g_lo)
    emit()
    kk_all = 1.0 - f
    qf_all = _silu(q_in)
    v_all = v_in
    kk_bf, qf_bf = kk_all.astype(BF16), qf_all.astype(BF16)
    heads = range(C_HEADS)
    cols = [slice(h * LANES, (h + 1) * LANES) for h in heads]
    sts = [st_scr[h] for h in heads]
    os = [_dot_nt((qf_all[:, cols[h]] * jnp.exp2(b_all[:, cols[h]])).astype(BF16), sts[h].astype(BF16)) for h in heads]
    att = [[] for _ in heads]
    bms = [_split_points(b_all, l) for l in range(2 * npair)]
    for n in range(npair):
        lhs, rhs = [], []
        for h in heads:
            b = b_all[:, cols[h]]
            qt, kt = [], []
            for l in (2 * n, 2 * n + 1):
                bm = bms[l][:, cols[h]]
                qt.append(qf_bf[:, cols[h]] * jnp.exp2(jnp.minimum(b - bm, 1.0)).astype(BF16))
                kt.append(kk_bf[:, cols[h]] * jnp.exp2(jnp.minimum(bm - b, 1.0)).astype(BF16))
            lhs.append(jnp.concatenate(qt, axis=1))
            rhs.append(jnp.concatenate([jnp.concatenate([kt[0], zeros], axis=1),
                                        jnp.concatenate([zeros, kt[1]], axis=1)], axis=0))
        prods = [_dot_nt(lhs[h], rhs[h]) for h in heads]
        if n == 1:
            emit()
        for h in heads:
            att[h].append(jnp.where(msk_ref[n] > 0.5, prods[h], 0.0).astype(BF16))
    vbs = [v_all[:, cols[h]].astype(BF16) for h in heads]
    intra = [_dot(jnp.concatenate(att[h], axis=1), jnp.concatenate([vbs[h]] * (2 * npair), axis=0)) for h in heads]
    b_last = b_all[c - 1:c, :]
    kd = (kk_all * jnp.exp2(b_last - b_all)).astype(BF16)
    upd = [_dot(v_all[:, cols[h]].T.astype(BF16), kd[:, cols[h]]) for h in heads]
    decay = jnp.exp2(b_last)
    diag = qf_all * kk_all
    outs = []
    for h in heads:
        st_scr[h] = sts[h] * decay[:, cols[h]] + upd[h]
        o = os[h] + intra[h] + jnp.sum(diag[:, cols[h]], axis=-1, keepdims=True) * v_all[:, cols[h]]
        ms = jnp.mean(o * o, axis=-1, keepdims=True)
        outs.append(o * lax.rsqrt(ms + EPS) * og_ref[...])
    return jnp.concatenate(outs, axis=1) * _silu(g_in)


def _inproj_hgrn_kernel(tiles_per_seq, x_ref, g_ref, w_ref, lb_ref, og_ref, mst_ref, msk_ref,
                        z_ref, kc_ref, vc_ref, oc_ref, st_scr):
    @pl.when(pl.program_id(0) % tiles_per_seq == 0)
    def _():
        st_scr[...] = jnp.zeros(st_scr.shape, F32)

    x = x_ref[...]
    ms = jnp.mean(x * x, axis=-1, keepdims=True)
    xn = (x * lax.rsqrt(ms + EPS) * g_ref[...]).astype(BF16)
    own = [_dot(xn, w_ref[:, k * C_WIDTH:(k + 1) * C_WIDTH]) for k in range(4)]

    o_refs = (z_ref, kc_ref, vc_ref)
    starts = np.cumsum([0] + [o.shape[-1] for o in o_refs])
    pending = list(range(0, int(starts[-1]), 256))

    def emit():
        if not pending:
            return
        c = pending.pop(0)
        w = min(256, int(starts[-1]) - c)
        r = _dot(xn, w_ref[:, ODD_FUSED + c:ODD_FUSED + c + w]).astype(BF16)
        for o_ref, lo, hi in zip(o_refs, starts[:-1], starts[1:]):
            a, b = max(c, int(lo)), min(c + w, int(hi))
            if a < b:
                o_ref[:, a - int(lo):b - int(lo)] = r[:, a - c:b - c]

    p = lb_ref[...]
    pm = jnp.maximum(p[0:1, :], p[1:2, :])
    e0, e1 = jnp.exp(p[0:1, :] - pm), jnp.exp(p[1:2, :] - pm)
    lb = e1 / (e0 + e1)
    for ch in range(x.shape[0] // C_CHUNK):
        rows = slice(ch * C_CHUNK, (ch + 1) * C_CHUNK)
        q_in, f_in, v_in, g_in = (a[rows, :] for a in own)
        oc_ref[rows, :] = _hgrn_chunk(lb, q_in, f_in, v_in, g_in, og_ref, mst_ref, msk_ref, st_scr, emit).astype(BF16)
    while pending:
        emit()


def _inproj_hgrn(x2d, gain, w_bf16, seq, c_lb, c_og, tm=512):
    n, d = x2d.shape
    nc = w_bf16.shape[1]
    assert ODD_FUSED + sum(ODD_SPLIT) == nc
    mst, msk = _hgrn_consts()
    full = lambda shp: pl.BlockSpec(shp, lambda i: (0,) * len(shp))
    widths = ODD_SPLIT + (C_WIDTH,)
    return pl.pallas_call(
        functools.partial(_inproj_hgrn_kernel, seq // tm),
        grid=(n // tm,),
        in_specs=[pl.BlockSpec((tm, d), lambda i: (i, 0)), full((1, d)), full((d, nc)),
                  full(c_lb.shape), full((1, C_HEAD_DIM)), full(mst.shape), full(msk.shape)],
        out_specs=[pl.BlockSpec((tm, w), lambda i: (i, 0)) for w in widths],
        out_shape=[jax.ShapeDtypeStruct((n, w), BF16) for w in widths],
        scratch_shapes=[pltpu.VMEM((C_HEADS, C_HEAD_DIM, C_HEAD_DIM), F32)],
        compiler_params=_cparams(("arbitrary",)),
        name="inproj_hgrn",
    )(x2d, gain.reshape(1, d), w_bf16, c_lb, c_og.reshape(1, C_HEAD_DIM), jnp.asarray(mst, BF16), jnp.asarray(msk, F32))


def _nsa_prep_kernel(kcr_ref, vcr_ref, ks_ref, vs_ref, kw_ref, vw_ref, cos_ref, sin_ref, cosc_ref, sinc_ref,
                     kncmp_ref, knslc_ref, knwin_ref, pek_ref, pev_ref, w1k_ref, w2k_ref, w1v_ref, w2vt_ref,
                     kc_o, vct_o, ks_o, vst_o, kw_o, vwt_o):
    def hidden(xr_ref, pe_ref, w1_ref):
        xr = xr_ref[0].astype(F32)
        top = _dot((xr + pe_ref[0:1, :]).astype(BF16), w1_ref[0])
        bot = _dot((xr + pe_ref[1:2, :]).astype(BF16), w1_ref[1])
        pre = top + pltpu.roll(bot, bot.shape[0] - 1, 0)
        return _silu(pre).astype(BF16)

    kc = _dot(hidden(kcr_ref, pek_ref, w1k_ref), w2k_ref[...])
    kc = _rope(_head_rms(kc, kncmp_ref[...]), cosc_ref[...], sinc_ref[...])
    d0, d1 = _split_kv(kc)
    kc_o[0, 0] = d0.astype(BF16)
    kc_o[0, 1] = d1.astype(BF16)
    vct = _dot_nt(w2vt_ref[...], hidden(vcr_ref, pev_ref, w1v_ref))
    for kv in range(D_KV_HEADS):
        vct_o[0, kv] = _with_ones_row(vct[kv * HEAD_DIM:(kv + 1) * HEAD_DIM, :]).astype(BF16)

    s = ks_ref.shape[1]
    step = 256
    for r0 in range(0, s, step):
        rows = slice(r0, r0 + step)
        cos, sin = cos_ref[rows, :], sin_ref[rows, :]
        ks = _rope(_head_rms(ks_ref[0, rows, :].astype(F32), knslc_ref[...]), cos, sin)
        kw = _rope(_head_rms(kw_ref[0, rows, :].astype(F32), knwin_ref[...]), cos, sin)
        blk = (r0 + _row((step, LANES))) // SEL_LEN
        onehot = jnp.where(_lane((step, LANES)) - HEAD_DIM == blk, MASK_BIG, 0.0)
        for src, dst, fill in ((ks, ks_o, onehot), (kw, kw_o, 0.0)):
            d0, d1 = _split_kv(src, fill)
            dst[0, 0, rows, :] = d0.astype(BF16)
            dst[0, 1, rows, :] = d1.astype(BF16)
        for src, dst in ((vs_ref, vst_o), (vw_ref, vwt_o)):
            vt = src[0, rows, :].astype(F32).T
            for kv in range(D_KV_HEADS):
                dst[0, kv, :, rows] = _with_ones_row(vt[kv * HEAD_DIM:(kv + 1) * HEAD_DIM, :]).astype(BF16)


def _nsa_prep(z, kcr, vcr, cos, sin_s, cosc, sinc, kn_cmp, kn_slc, kn_win, pek, pev, w1k, w2k, w1v, w2v):
    b, s, _ = z.shape
    nseg = s // CMP_STRIDE
    zb = lambda name: pl.BlockSpec((1, s, 128), lambda bi, o=ODD_OFF[name][0] // 128: (bi, 0, o))
    full = lambda shp: pl.BlockSpec(shp, lambda bi: (0,) * len(shp))
    seg = pl.BlockSpec((1, nseg, 2048), lambda bi: (bi, 0, 0))
    g2 = lambda g: jnp.tile(g, 2).reshape(1, 128)
    dup = lambda n: (jax.ShapeDtypeStruct((b, 2, n, 128), BF16), pl.BlockSpec((1, 2, n, 128), lambda bi: (bi, 0, 0, 0)))
    tr = lambda n: (jax.ShapeDtypeStruct((b, 2, VT_ROWS, n), BF16),
                    pl.BlockSpec((1, 2, VT_ROWS, n), lambda bi: (bi, 0, 0, 0)))
    outs = [dup(nseg), tr(nseg), dup(s), tr(s), dup(s), tr(s)]
    return pl.pallas_call(
        _nsa_prep_kernel,
        grid=(b,),
        in_specs=[seg, seg, zb("ksd"), zb("vsd"), zb("kwd"), zb("vwd"),
                  full((s, 128)), full((s, 128)), full((nseg, 128)), full((nseg, 128)),
                  full((1, 128)), full((1, 128)), full((1, 128)), full((2, 2048)), full((2, 2048)),
                  full((2, 2048, 256)), full((256, 128)), full((2, 2048, 256)), full((128, 256))],
        out_specs=[o[1] for o in outs],
        out_shape=[o[0] for o in outs],
        compiler_params=_cparams(("arbitrary",)),
        name="nsa_prep",
    )(kcr, vcr, z, z, z, z, cos, sin_s, cosc, sinc, g2(kn_cmp), g2(kn_slc), g2(kn_win), pek, pev, w1k, w2k, w1v, w2v)


SEL_CHUNK = 512
SEL_PIECE = 512


def _nsa_kernel(bound_ref, *refs):
    small = jnp.maximum(jnp.maximum(bound_ref[0], bound_ref[1]), bound_ref[2]) <= MAX_SAFE_BOUND

    @pl.when(small)
    def _():
        _nsa_body((bound_ref[0], bound_ref[1], bound_ref[2]), *refs)

    @pl.when(jnp.logical_not(small))
    def _():
        _nsa_body((None, None, None), *refs)


def _nsa_body(fixed, q_ref, gd_ref, gate_ref, cos_ref, sin_ref, qn_ref, kc_ref, vct_ref, ks_ref, vst_ref, kw_ref, vwt_ref,
              ovlt_ref, o_ref):
    fix_c, fix_s, fix_w = fixed
    i = pl.program_id(1)
    t = q_ref.shape[1]
    grp = D_HEADS // D_KV_HEADS
    w = grp * t
    kvs = range(D_KV_HEADS)
    q = _rope(_head_rms(q_ref[0].astype(F32), qn_ref[...]), cos_ref[...], sin_ref[...]) * Q_SCALE
    stack = lambda heads: [jnp.concatenate(heads[kv * grp:(kv + 1) * grp], axis=0) for kv in kvs]
    qs = stack(_q_heads(q))
    tq1 = i * t + _lane((1, t))

    nwin = D_WINDOW + t
    win0 = pl.multiple_of(jnp.maximum(i * t + t - nwin, 0), t)
    s_cmp = [_dot_nt(kc_ref[0, kv], qs[kv]) for kv in kvs]
    s_win = [_dot_nt(kw_ref[0, kv, pl.ds(win0, nwin), :], qs[kv]) for kv in kvs]

    ncmp = kc_ref.shape[2] - 1
    nsel = ovlt_ref.shape[0]
    crow = _row((kc_ref.shape[2], t))
    bias_c = _tile4(jnp.where((crow * CMP_STRIDE + (CMP_LEN - 1) <= tq1) & (crow < ncmp), 0.0, NEG_INF))
    row_ok = _tile4(tq1 >= CMP_LEN - 1)
    es = [_exp_cols(s_cmp[kv] + bias_c, fixed=fix_c)[0] for kv in kvs]
    accs = [_dot(vct_ref[0, kv], es[kv]) for kv in kvs]
    invs = [jnp.where(row_ok, 1.0 / accs[kv][HEAD_DIM:HEAD_DIM + 1, :], 0.0) for kv in kvs]
    oc_t = [accs[kv][0:HEAD_DIM, :] * invs[kv] for kv in kvs]
    psums = []
    for kv in kvs:
        p = es[kv].astype(F32) * invs[kv]
        psums.append(p[:, 0:t] + p[:, t:2 * t] + p[:, 2 * t:3 * t] + p[:, 3 * t:4 * t])
    imps = [_dot_split_rhs(ovlt_ref[...], psums[kv]) for kv in kvs]

    rel = tq1 - (win0 + _row((nwin, t)))
    bias_w = _tile4(jnp.where((rel >= 0) & (rel < D_WINDOW), 0.0, NEG_INF))
    ow_t = [_finish(_dot(vwt_ref[0, kv, :, pl.ds(win0, nwin)], _exp_cols(s_win[kv] + bias_w, fixed=fix_w)[0]))
            for kv in kvs]

    jrow = _row((nsel, t))
    jrow_f = jrow.astype(F32)
    cur = (i * t + _lane((nsel, t))) // SEL_LEN
    forced = (jrow == 0) | (jrow == cur)
    fills = []
    for kv in kvs:
        score = jnp.where(forced, POS_INF, jnp.where(jrow <= cur, imps[kv], NEG_INF))
        chosen = jnp.zeros((nsel, t), F32)
        for _ in range(min(SEL_TOPK, nsel)):
            mx = jnp.max(score, axis=0, keepdims=True)
            first = jnp.min(jnp.where(score == mx, jrow_f, 1e9), axis=0, keepdims=True)
            hit = jrow_f == first
            chosen = jnp.where(hit, 1.0, chosen)
            score = jnp.where(hit, -3e38, score)
        frame = jnp.concatenate([jnp.zeros((HEAD_DIM, t), F32), chosen - 1.0,
                                 jnp.zeros((LANES - HEAD_DIM - nsel, t), F32)], axis=0)
        fills.extend([frame.T] * (grp // 2))

    init = (jnp.full((1, w), NEG_INF, F32), jnp.zeros((VT_ROWS, w), F32))
    pc = SEL_PIECE

    def sweep(states, k_ref, vt_ref, queries, offs, biases, n):
        scores = [[_dot_nt(k_ref[0, kv, pl.ds(off, n), :], queries[kv]) for kv in kvs] for off in offs]
        for off, sc, bias in zip(offs, scores, biases):
            states = tuple(_online_update(states[kv], sc[kv] if bias is None else sc[kv] + bias,
                                          vt_ref[0, kv, :, pl.ds(off, n)], fix_s) for kv in kvs)
        return states

    qsel = stack(_q_heads(q, fills))
    ch = SEL_CHUNK
    own = pl.multiple_of(((i * t) // ch) * ch, ch)
    st = sweep((init, init), ks_ref, vst_ref, qsel, [own],
               [_tile4(jnp.where(own + _row((ch, t)) <= tq1, 0.0, NEG_INF))], ch)

    def sel_chunk(c, states):
        base = pl.multiple_of(c * ch, ch)
        return sweep(states, ks_ref, vst_ref, qsel, [base + n * pc for n in range(ch // pc)], [None] * (ch // pc), pc)

    st = lax.fori_loop(0, (i * t) // ch, sel_chunk, st)
    os_t = [_finish(acc) for (_, acc) in st]

    g_t = jax.nn.sigmoid(gate_ref[0].astype(F32)).T
    rows = []
    for h in range(D_HEADS):
        kv, g = divmod(h, grp)
        cols = slice(g * t, (g + 1) * t)
        r = D_BRANCHES * h
        rows.append(g_t[r:r + 1, :] * oc_t[kv][:, cols] + g_t[r + 1:r + 2, :] * os_t[kv][:, cols]
                    + g_t[r + 2:r + 3, :] * ow_t[kv][:, cols])
    o = jnp.concatenate(rows, axis=0).T
    o_ref[0] = (o * _silu(gd_ref[0].astype(F32))).astype(BF16)


def _overlap_t(s):
    ncmp = (s - CMP_LEN) // CMP_STRIDE + 1
    nsel = s // SEL_LEN
    cs = np.arange(ncmp)[None, :] * CMP_STRIDE
    ss = np.arange(nsel)[:, None] * SEL_LEN
    ovl = np.zeros((nsel, s // CMP_STRIDE), np.float32)
    ovl[:, :ncmp] = (cs < ss + SEL_LEN) & (cs + CMP_LEN > ss)
    return ovl


def _nsa(z, cos, sin_s, qn, bounds, kc, vct, ks, vst, kw, vwt, t=256):
    b, s, _ = z.shape
    nseg = s // CMP_STRIDE
    assert nseg == LANES, "compressed-block scores are laid out on one 128-row tile"
    ovlt = _overlap_t(s)
    full = lambda shp: pl.BlockSpec(shp, lambda bi, i: (0,) * len(shp))
    dup = lambda n: pl.BlockSpec((1, 2, n, 128), lambda bi, i: (bi, 0, 0, 0))
    tr = lambda n: pl.BlockSpec((1, 2, VT_ROWS, n), lambda bi, i: (bi, 0, 0, 0))
    return pl.pallas_call(
        _nsa_kernel,
        grid=(b, s // t),
        in_specs=[pl.BlockSpec(memory_space=pltpu.SMEM),
                  pl.BlockSpec((1, t, 512), lambda bi, i: (bi, i, ODD_OFF["qd"][0] // 512)),
                  pl.BlockSpec((1, t, 512), lambda bi, i: (bi, i, ODD_OFF["gd"][0] // 512)),
                  pl.BlockSpec((1, t, 128), lambda bi, i: (bi, i, ODD_OFF["gate"][0] // 128)),
                  pl.BlockSpec((t, 128), lambda bi, i: (i, 0)),
                  pl.BlockSpec((t, 128), lambda bi, i: (i, 0)),
                  full((1, 512)), dup(nseg), tr(nseg), dup(s), tr(s), dup(s), tr(s), full(ovlt.shape)],
        out_specs=pl.BlockSpec((1, t, 512), lambda bi, i: (bi, i, 0)),
        out_shape=jax.ShapeDtypeStruct((b, s, 512), BF16),
        compiler_params=_cparams(("arbitrary", "arbitrary")),
        name="nsa",
    )(bounds, z, z, z, cos, sin_s, jnp.tile(qn, 8).reshape(1, 512), kc, vct, ks, vst, kw, vwt, jnp.asarray(ovlt, BF16))


def _permute_cols(w, order, src):
    cols = []
    for name, width in order:
        o, sw = src[name]
        blk = w[:, o:o + sw]
        if sw < width:
            blk = jnp.pad(blk, ((0, 0), (0, width - sw)))
        cols.append(blk)
    return jnp.concatenate(cols, axis=1).astype(BF16)


def _block_diag_pairs(w):
    z = jnp.zeros((4, LANES, LANES), w.dtype)
    z = z.at[:, 0:64, 0:64].set(w[0::2])
    z = z.at[:, 64:128, 64:128].set(w[1::2])
    return z.astype(BF16)


def _rope_tables(pos):
    half = HEAD_DIM // 2
    inv = ROPE_THETA ** (-jnp.arange(half, dtype=F32) / half)
    ang = pos.astype(F32)[:, None] * inv[None, :]
    cos, sin = jnp.cos(ang), jnp.sin(ang)
    cos_t = jnp.tile(cos, (1, 4))
    sin_t = jnp.tile(jnp.concatenate([-sin, sin], axis=1), (1, 2))
    return cos_t, sin_t


def _expand_compress_w1(w1):
    hdim = w1.shape[1]
    w = w1.reshape(2, CMP_STRIDE, 1, HEAD_DIM, 1, hdim)
    same_head = jnp.eye(2, dtype=w1.dtype).reshape(1, 1, 2, 1, 2, 1)
    return (w * same_head).reshape(2, CMP_STRIDE * 2 * HEAD_DIM, 2 * hdim).astype(BF16)


def _expand_compress_w2(w2):
    hdim, hd = w2.shape
    z = jnp.zeros((2 * hdim, 2 * hd), w2.dtype)
    z = z.at[0:hdim, 0:hd].set(w2)
    z = z.at[hdim:, hd:].set(w2)
    return z.astype(BF16)


def _expand_pe(pe):
    p = pe.reshape(2, CMP_STRIDE, 1, HEAD_DIM)
    return jnp.broadcast_to(p, (2, CMP_STRIDE, 2, HEAD_DIM)).reshape(2, CMP_STRIDE * 2 * HEAD_DIM)


def _even_layer(h, mem, g, mem_g, w_mem_kv, m_qn, m_kn, w_in, w_out, a_qn, a_kn, a_sinks,
                conv_w, conv_b, w_r, b_r, w_i, b_i, lam, cos, sin_s):
    b, s, d = h.shape
    h2 = h.reshape(b * s, d)
    z, ob = _inproj_rglru(h2, g, _permute_cols(w_in, EVEN_ORDER, EVEN_SRC), s, conv_w, conv_b,
                          _block_diag_pairs(w_r), b_r, _block_diag_pairs(w_i), b_i, lam)
    z = z.reshape(b, s, EVEN_COLS)
    oa = _swa(z, cos, sin_s, a_qn, a_kn, a_sinks)
    out = _outproj_mem(h2, oa.reshape(b * s, 512), ob, z.reshape(b * s, EVEN_COLS), EVEN_OFF["qm"][0], EVEN_OFF["gm"][0],
                       s, mem, mem_g, w_mem_kv.astype(BF16), m_qn, m_kn, w_out.astype(BF16))
    return out.reshape(b, s, d)


def _odd_layer(h, mem, g, mem_g, w_mem_kv, m_qn, m_kn, w_in, w_out, c_lb, c_og,
               d_qn, d_kn_cmp, d_kn_slc, d_kn_win, pe_k, pe_v, w1k, w2k, w1v, w2v, cos, sin_s, cosc, sinc):
    b, s, d = h.shape
    h2 = h.reshape(b * s, d)
    z, kcd, vcd, oc = _inproj_hgrn(h2, g, _permute_cols(w_in, ODD_ORDER, ODD_SRC), s, c_lb, c_og)
    z = z.reshape(b, s, ODD_SPLIT[0])
    nseg = s // CMP_STRIDE
    seg = lambda a: a.reshape(b, nseg, CMP_STRIDE * 128)
    kc, vc, ks, vs, kw, vw = _nsa_prep(
        z, seg(kcd), seg(vcd), cos, sin_s, cosc, sinc, d_kn_cmp, d_kn_slc, d_kn_win,
        _expand_pe(pe_k), _expand_pe(pe_v), _expand_compress_w1(w1k), _expand_compress_w2(w2k),
        _expand_compress_w1(w1v), _expand_compress_w2(w2v).T)
    bounds = jnp.stack([_score_bound(d_qn, kn) for kn in (d_kn_cmp, d_kn_slc, d_kn_win)]).astype(F32)
    od = _nsa(z, cos, sin_s, d_qn, bounds, kc, vc, ks, vs, kw, vw)
    out = _outproj_mem(h2, oc, od.reshape(b * s, 512), z.reshape(b * s, ODD_SPLIT[0]), ODD_OFF["qm"][0], ODD_OFF["gm"][0],
                       s, mem, mem_g, w_mem_kv.astype(BF16), m_qn, m_kn, w_out.astype(BF16))
    return out.reshape(b, s, d)


def kernel(x, mem, norm_g, mem_norm_g, mem_w_kv, mem_qn, mem_kn, ev_w_in, ev_w_out, a_qn, a_kn, a_sinks,
           b_conv_w, b_conv_b, b_w_r, b_b_r, b_w_i, b_b_i, b_lambda, od_w_in, od_w_out, c_lb, c_onorm,
           d_qn, d_kn_cmp, d_kn_slc, d_kn_win, d_pe_k, d_pe_v, d_w1k, d_w2k, d_w1v, d_w2v):
    depth = norm_g.shape[0]
    assert depth == 2 and c_lb.shape[0] == 2, "the HGRN2 lower-bound formula in the kernel is written for depth 2"
    s = x.shape[1]
    assert s % 256 == 0 and s >= D_WINDOW
    pos = jnp.arange(s)
    cos, sin_s = _rope_tables(pos)
    nseg = s // CMP_STRIDE
    cmp_end = jnp.minimum(jnp.arange(nseg) * CMP_STRIDE + CMP_LEN - 1, s - 1)
    cosc, sinc = _rope_tables(cmp_end)
    h = _even_layer(x, mem, norm_g[0], mem_norm_g[0], mem_w_kv[0], mem_qn[0], mem_kn[0], ev_w_in[0], ev_w_out[0],
                    a_qn[0], a_kn[0], a_sinks[0], b_conv_w[0], b_conv_b[0], b_w_r[0], b_b_r[0], b_w_i[0], b_b_i[0],
                    b_lambda[0], cos, sin_s)
    h = _odd_layer(h, mem, norm_g[1], mem_norm_g[1], mem_w_kv[1], mem_qn[1], mem_kn[1], od_w_in[0], od_w_out[0],
                   c_lb, c_onorm[0], d_qn[0], d_kn_cmp[0], d_kn_slc[0], d_kn_win[0], d_pe_k[0], d_pe_v[0],
                   d_w1k[0], d_w2k[0], d_w1v[0], d_w2v[0], cos, sin_s, cosc, sinc)
    return h
```

```python
import functools

import numpy as np
import jax
import jax.numpy as jnp
from jax import lax
from jax.experimental import pallas as pl
from jax.experimental.pallas import tpu as pltpu

F32 = jnp.float32
BF16 = jnp.bfloat16

D_MODEL = 1024
N_MEM = 256
HEAD_DIM = 64
ROPE_THETA = 10000.0
EPS = 1e-6
NEG_INF = -1e30
POS_INF = 1e30
MASK_BIG = 1e30
LANES = 128

A_HEADS, A_KV_HEADS, A_WINDOW = 8, 2, 128
B_WIDTH, B_BLOCKS, B_CONV, B_C = 512, 8, 4, 8.0
M_HEADS = 4
C_HEADS, C_HEAD_DIM, C_CHUNK = 4, 128, 64
C_WIDTH = C_HEADS * C_HEAD_DIM
D_HEADS, D_KV_HEADS = 8, 2
CMP_LEN, CMP_STRIDE, CMP_HIDDEN = 32, 16, 128
SEL_LEN, SEL_TOPK = 64, 4
D_WINDOW = 512
D_BRANCHES = 3
SCALE = HEAD_DIM ** -0.5
LOG2E = 1.4426950408889634
Q_SCALE = SCALE * LOG2E

EVEN_ORDER = [("xb", 512), ("gb", 512), ("qa", 512), ("ga", 512), ("qm", 256), ("gm", 256), ("ka", 128), ("va", 128)]
EVEN_FUSED = 1024
EVEN_SRC = {"qa": (0, 512), "ka": (512, 128), "va": (640, 128), "ga": (768, 512), "xb": (1280, 512),
            "gb": (1792, 512), "qm": (2304, 256), "gm": (2560, 256)}
ODD_ORDER = [("qc", 512), ("fc", 512), ("ic", 512), ("gc", 512), ("qd", 512), ("gd", 512), ("qm", 256), ("gm", 256),
             ("ksd", 128), ("vsd", 128), ("kwd", 128), ("vwd", 128), ("gate", 128), ("kcd", 128), ("vcd", 128)]
ODD_FUSED = 2048
ODD_SPLIT = (2176, 128, 128)
ODD_SRC = {"qc": (0, 512), "fc": (512, 512), "ic": (1024, 512), "gc": (1536, 512), "qd": (2048, 512),
           "kcd": (2560, 128), "vcd": (2688, 128), "ksd": (2816, 128), "vsd": (2944, 128), "kwd": (3072, 128),
           "vwd": (3200, 128), "gate": (3328, 24), "gd": (3352, 512), "qm": (3864, 256), "gm": (4120, 256)}

VMEM_LIMIT = 48 * 1024 * 1024


def _offsets(order):
    off, out = 0, {}
    for name, w in order:
        out[name] = (off, w)
        off += w
    return out, off


EVEN_OFF, EVEN_COLS = _offsets(EVEN_ORDER[2:])
ODD_OFF, ODD_COLS = _offsets(ODD_ORDER[4:])


def _cparams(sem):
    return pltpu.CompilerParams(dimension_semantics=sem, vmem_limit_bytes=VMEM_LIMIT)


def _dot(a, b):
    return jnp.dot(a, b, preferred_element_type=F32)


def _dot_nt(a, b):
    return lax.dot_general(a, b, (((1,), (1,)), ((), ())), preferred_element_type=F32)


def _split(x):
    hi = x.astype(BF16)
    lo = (x - hi.astype(F32)).astype(BF16)
    return hi, lo


def _dot_split_rhs(m, x):
    hi, lo = _split(x)
    return _dot(m, hi) + _dot(m, lo)


def _lane(shape):
    return lax.broadcasted_iota(jnp.int32, shape, len(shape) - 1)


def _row(shape):
    return lax.broadcasted_iota(jnp.int32, shape, len(shape) - 2)


def _silu(x):
    return x * jax.nn.sigmoid(x)


def _seg_ones():
    r = lax.broadcasted_iota(jnp.int32, (LANES, LANES), 0) >> 6
    c = lax.broadcasted_iota(jnp.int32, (LANES, LANES), 1) >> 6
    return jnp.where(r == c, 1.0, 0.0).astype(BF16)


def _head_rms(x, gain):
    seg = _seg_ones()
    cols = []
    for c in range(x.shape[1] // LANES):
        xc = x[:, c * LANES:(c + 1) * LANES]
        ms = _dot((xc * xc).astype(BF16), seg) * (1.0 / HEAD_DIM)
        cols.append(xc * lax.rsqrt(ms + EPS))
    y = cols[0] if len(cols) == 1 else jnp.concatenate(cols, axis=1)
    return y * gain


def _rope(x, cos, sin_s):
    first = (_lane((x.shape[0], LANES)) & 63) < 32
    cols = []
    for c in range(x.shape[1] // LANES):
        xc = x[:, c * LANES:(c + 1) * LANES]
        sw = jnp.where(first, pltpu.roll(xc, 96, 1), pltpu.roll(xc, 32, 1))
        cols.append(xc * cos + sw * sin_s)
    return cols[0] if len(cols) == 1 else jnp.concatenate(cols, axis=1)


def _split_kv(k, fill=0.0):
    lo = _lane(k.shape) < 64
    return jnp.where(lo, k, fill), jnp.where(lo, pltpu.roll(k, 64, 1), fill)


def _q_heads(q, fills=None):
    lo = _lane((q.shape[0], LANES)) < 64
    out = []
    for c in range(q.shape[1] // LANES):
        qc = q[:, c * LANES:(c + 1) * LANES]
        fill = 0.0 if fills is None else fills[c]
        out.append(jnp.where(lo, qc, fill).astype(BF16))
        out.append(jnp.where(lo, pltpu.roll(qc, 64, 1), fill).astype(BF16))
    return out


VT_ROWS = HEAD_DIM + 16


def _with_ones_row(vt):
    n = vt.shape[1]
    pad = jnp.where(_row((VT_ROWS - HEAD_DIM, n)) == 0, 1.0, 0.0)
    return jnp.concatenate([vt, pad], axis=0)


MAX_SAFE_BOUND = 60.0


def _score_bound(q_gain, k_gain):
    return 1.02 * HEAD_DIM * SCALE * LOG2E * jnp.max(jnp.abs(q_gain)) * jnp.max(jnp.abs(k_gain))


def _with_fixed_reference(bound, body):
    small = bound <= MAX_SAFE_BOUND

    @pl.when(small)
    def _():
        body(bound)

    @pl.when(jnp.logical_not(small))
    def _():
        body(None)


def _exp_cols(s, extra=None, fixed=None):
    if fixed is not None:
        return jnp.exp2(s - fixed).astype(BF16), fixed
    m = jnp.max(s, axis=0, keepdims=True)
    if extra is not None:
        m = jnp.maximum(m, extra)
    return jnp.exp2(s - m).astype(BF16), m


def _finish(acc, extra_den=None):
    den = acc[HEAD_DIM:HEAD_DIM + 1, :]
    if extra_den is not None:
        den = den + extra_den
    return acc[0:HEAD_DIM, :] * (1.0 / den)


def _online_update(state, s, vt_tile, fixed=None):
    m, acc = state
    if fixed is not None:
        return m, acc + _dot(vt_tile, _exp_cols(s, fixed=fixed)[0])
    p, m_new = _exp_cols(s, m)
    acc = jnp.exp2(m - m_new) * acc + _dot(vt_tile, p)
    return m_new, acc


def _tile4(x):
    return jnp.concatenate([x, x, x, x], axis=1)


def _outproj_mem_kernel(tiles_per_seq, bound_ref, h_ref, a_ref, b_ref, q_ref, gm_ref, mem_ref, mg_ref, wkv_ref,
                        qn_ref, kn_ref, w_ref, o_ref, k_scr, vt_scr, acc_scr):
    @pl.when(pl.program_id(0) % tiles_per_seq == 0)
    def _():
        m = mem_ref[0]
        ms = jnp.mean(m * m, axis=-1, keepdims=True)
        mn = (m * lax.rsqrt(ms + EPS) * mg_ref[...]).astype(BF16)
        kv = _dot(mn, wkv_ref[...])
        km = _head_rms(kv[:, 0:256], kn_ref[...])
        for c in range(2):
            k0, k1 = _split_kv(km[:, c * LANES:(c + 1) * LANES])
            k_scr[2 * c] = k0.astype(BF16)
            k_scr[2 * c + 1] = k1.astype(BF16)
        vt = kv[:, 256:512].T
        for h in range(M_HEADS):
            vt_scr[h] = _with_ones_row(vt[h * HEAD_DIM:(h + 1) * HEAD_DIM, :]).astype(BF16)

    def body(fixed):
        a, b = a_ref[...], b_ref[...]
        pending = list(range(0, D_MODEL, 256))

        def emit():
            if pending:
                c = pending.pop(0)
                acc_scr[:, c:c + 256] = (h_ref[:, c:c + 256] + _dot(a, w_ref[0:512, c:c + 256])
                                         + _dot(b, w_ref[512:1024, c:c + 256]))

        q = _head_rms(q_ref[...].astype(F32), qn_ref[...]) * Q_SCALE
        scores = [_dot_nt(k_scr[h], qh) for h, qh in enumerate(_q_heads(q))]
        rows = []
        for h, sc in enumerate(scores):
            emit()
            rows.append(_finish(_dot(vt_scr[h], _exp_cols(sc, fixed=fixed)[0])))
        om = (jnp.concatenate(rows, axis=0).T * _silu(gm_ref[...].astype(F32))).astype(BF16)
        while pending:
            emit()
        for c in range(0, D_MODEL, 256):
            o_ref[:, c:c + 256] = acc_scr[:, c:c + 256] + _dot(om, w_ref[1024:1280, c:c + 256])

    _with_fixed_reference(bound_ref[0], body)


def _outproj_mem(h2d, oa, ob, z2d, q_off, g_off, seq, mem, mem_g, wkv_bf16, qn, kn, w_bf16, tm=1024):
    n, d = h2d.shape
    full = lambda shp: pl.BlockSpec(shp, lambda i: (0,) * len(shp))
    tps = seq // tm
    return pl.pallas_call(
        functools.partial(_outproj_mem_kernel, tps),
        grid=(n // tm,),
        in_specs=[pl.BlockSpec(memory_space=pltpu.SMEM),
                  pl.BlockSpec((tm, d), lambda i: (i, 0)),
                  pl.BlockSpec((tm, 512), lambda i: (i, 0)),
                  pl.BlockSpec((tm, 512), lambda i: (i, 0)),
                  pl.BlockSpec((tm, 256), lambda i: (i, q_off // 256)),
                  pl.BlockSpec((tm, 256), lambda i: (i, g_off // 256)),
                  pl.BlockSpec((1, N_MEM, D_MODEL), lambda i: (i // tps, 0, 0)),
                  full((1, D_MODEL)), full((D_MODEL, 512)), full((1, 256)), full((1, 256)), full((1280, d))],
        out_specs=pl.BlockSpec((tm, d), lambda i: (i, 0)),
        out_shape=jax.ShapeDtypeStruct((n, d), F32),
        scratch_shapes=[pltpu.VMEM((M_HEADS, N_MEM, LANES), BF16), pltpu.VMEM((M_HEADS, VT_ROWS, N_MEM), BF16),
                        pltpu.VMEM((tm, d), F32)],
        compiler_params=_cparams(("arbitrary",)),
        name="outproj_mem",
    )(_score_bound(qn, kn).reshape(1).astype(F32), h2d, oa, ob, z2d, z2d, mem, mem_g.reshape(1, D_MODEL), wkv_bf16,
      jnp.tile(qn, 4).reshape(1, 256), jnp.tile(kn, 4).reshape(1, 256), w_bf16)


def _swa_kernel(bound_ref, *refs):
    _with_fixed_reference(bound_ref[0], functools.partial(_swa_body, refs))


def _swa_body(refs, fixed):
    (q_ref, g_ref, kc_ref, kp_ref, vc_ref, vp_ref, cosc_ref, sinc_ref, cosp_ref, sinp_ref,
     qn_ref, kn_ref, sink_ref, o_ref) = refs
    i = pl.program_id(1)
    t = q_ref.shape[1]
    tp = kp_ref.shape[1]
    grp = A_HEADS // A_KV_HEADS
    q = _rope(_head_rms(q_ref[0].astype(F32), qn_ref[...]), cosc_ref[...], sinc_ref[...]) * Q_SCALE
    qh = _q_heads(q)
    kc = _rope(_head_rms(kc_ref[0].astype(F32), kn_ref[...]), cosc_ref[...], sinc_ref[...])
    kp = _rope(_head_rms(kp_ref[0].astype(F32), kn_ref[...]), cosp_ref[...], sinp_ref[...])
    ks = [x.astype(BF16) for x in _split_kv(jnp.concatenate([kp, kc], axis=0))]
    vt = jnp.concatenate([vp_ref[0].astype(F32).T, vc_ref[0].astype(F32).T], axis=1)
    ts = i * t - tp + _row((tp + t, t))
    rel = i * t + _lane((tp + t, t)) - ts
    bias = _tile4(jnp.where((rel >= 0) & (rel < A_WINDOW) & (ts >= 0), 0.0, NEG_INF))
    scores = [_dot_nt(ks[kv], jnp.concatenate(qh[kv * grp:(kv + 1) * grp], axis=0)) for kv in range(A_KV_HEADS)]
    rows = []
    for kv in range(A_KV_HEADS):
        sink = jnp.concatenate([jnp.broadcast_to(sink_ref[kv * grp + g:kv * grp + g + 1, 0:1], (1, t))
                                for g in range(grp)], axis=1) * LOG2E
        p, m = _exp_cols(scores[kv] + bias, sink, fixed)
        vt_kv = _with_ones_row(vt[kv * HEAD_DIM:(kv + 1) * HEAD_DIM, :]).astype(BF16)
        o_t = _finish(_dot(vt_kv, p), jnp.exp2(sink - m))
        rows.extend(o_t[:, g * t:(g + 1) * t] for g in range(grp))
    o = jnp.concatenate(rows, axis=0).T
    o_ref[0] = (o * _silu(g_ref[0].astype(F32))).astype(BF16)


def _swa(z, cos, sin_s, qn, kn, sinks, t=256):
    b, s, _ = z.shape
    tp = A_WINDOW
    qo, go = EVEN_OFF["qa"][0] // 512, EVEN_OFF["ga"][0] // 512
    ko, vo = EVEN_OFF["ka"][0] // 128, EVEN_OFF["va"][0] // 128
    prev = lambda i: jnp.maximum(i * (t // tp) - 1, 0)
    bound = jnp.maximum(_score_bound(qn, kn), jnp.max(sinks) * LOG2E).reshape(1).astype(F32)
    return pl.pallas_call(
        _swa_kernel,
        grid=(b, s // t),
        in_specs=[pl.BlockSpec(memory_space=pltpu.SMEM),
                  pl.BlockSpec((1, t, 512), lambda bi, i: (bi, i, qo)),
                  pl.BlockSpec((1, t, 512), lambda bi, i: (bi, i, go)),
                  pl.BlockSpec((1, t, 128), lambda bi, i: (bi, i, ko)),
                  pl.BlockSpec((1, tp, 128), lambda bi, i: (bi, prev(i), ko)),
                  pl.BlockSpec((1, t, 128), lambda bi, i: (bi, i, vo)),
                  pl.BlockSpec((1, tp, 128), lambda bi, i: (bi, prev(i), vo)),
                  pl.BlockSpec((t, 128), lambda bi, i: (i, 0)),
                  pl.BlockSpec((t, 128), lambda bi, i: (i, 0)),
                  pl.BlockSpec((tp, 128), lambda bi, i: (prev(i), 0)),
                  pl.BlockSpec((tp, 128), lambda bi, i: (prev(i), 0)),
                  pl.BlockSpec((1, 512), lambda bi, i: (0, 0)),
                  pl.BlockSpec((1, 128), lambda bi, i: (0, 0)),
                  pl.BlockSpec((A_HEADS, 128), lambda bi, i: (0, 0))],
        out_specs=pl.BlockSpec((1, t, 512), lambda bi, i: (bi, i, 0)),
        out_shape=jax.ShapeDtypeStruct((b, s, 512), BF16),
        compiler_params=_cparams(("arbitrary", "arbitrary")),
        name="swa",
    )(bound, z, z, z, z, z, z, cos, sin_s, cos, sin_s, jnp.tile(qn, 8).reshape(1, 512), jnp.tile(kn, 2).reshape(1, 128),
      jnp.broadcast_to(sinks.reshape(A_HEADS, 1), (A_HEADS, 128)))


def _rglru_half(cols, xb, gb, emit, cw_ref, cb_ref, wr_ref, br_ref, wi_ref, bi_ref, lam_ref, xbuf, hcar, a_scr, u_scr, h_scr):
    t, wdt = xb.shape
    xbuf[8:t + 8, cols] = xb
    xc = cb_ref[:, cols] + cw_ref[0:1, cols] * xbuf[5:t + 5, cols]
    for j in range(1, B_CONV):
        xc = xc + cw_ref[j:j + 1, cols] * xbuf[5 + j:t + 5 + j, cols]
    xbuf[0:8, cols] = xbuf[t:t + 8, cols]
    emit()

    r_cols, i_cols = [], []
    for c in range(wdt // LANES):
        blk = cols.start // LANES + c
        xcc = xc[:, c * LANES:(c + 1) * LANES].astype(BF16)
        r_cols.append(_dot(xcc, wr_ref[blk]))
        i_cols.append(_dot(xcc, wi_ref[blk]))
    emit()
    r = jax.nn.sigmoid(jnp.concatenate(r_cols, axis=1) + br_ref[:, cols])
    ig = jax.nn.sigmoid(jnp.concatenate(i_cols, axis=1) + bi_ref[:, cols])
    nl = -lam_ref[:, cols]
    softplus = jnp.maximum(nl, 0.0) + jnp.log(1.0 + jnp.exp(-jnp.abs(nl)))
    log_a = -B_C * r * softplus
    a = jnp.exp(log_a)
    om = 1.0 - a * a
    u = om * lax.rsqrt(jnp.maximum(om, 1e-30)) * (ig * xc)
    emit()

    a = a.reshape(t // 8, 8, wdt)
    u = u.reshape(t // 8, 8, wdt)
    r8 = lax.broadcasted_iota(jnp.int32, a.shape, 1)
    for d in (1, 2, 4):
        a_sh = pltpu.roll(a, d, 1)
        u_sh = pltpu.roll(u, d, 1)
        m = r8 >= d
        u = jnp.where(m, a * u_sh + u, u)
        a = jnp.where(m, a * a_sh, a)
    a_scr[:, cols] = a.reshape(t, wdt)
    u_scr[:, cols] = u.reshape(t, wdt)
    emit()

    def body(j, h):
        off = pl.multiple_of(j * 8, 8)
        hh = a_scr[pl.ds(off, 8), cols] * h + u_scr[pl.ds(off, 8), cols]
        h_scr[pl.ds(off, 8), cols] = hh
        return hh[7:8, :]

    h_last = lax.fori_loop(0, t // 8, body, hcar[0:1, cols], unroll=True)
    hcar[0:1, cols] = h_last
    emit()
    return h_scr[:, cols] * _silu(gb)


def _inproj_rglru_kernel(tiles_per_seq, x_ref, g_ref, w_ref, cw_ref, cb_ref, wr_ref, br_ref, wi_ref, bi_ref, lam_ref,
                         z_ref, ob_ref, xbuf, hcar, a_scr, u_scr, h_scr):
    @pl.when(pl.program_id(0) % tiles_per_seq == 0)
    def _():
        xbuf[0:8, :] = jnp.zeros((8, B_WIDTH), F32)
        hcar[...] = jnp.zeros((8, B_WIDTH), F32)

    x = x_ref[...]
    ms = jnp.mean(x * x, axis=-1, keepdims=True)
    xn = (x * lax.rsqrt(ms + EPS) * g_ref[...]).astype(BF16)
    half = B_WIDTH // 2
    proj = {}

    def project(name, hp):
        base = (B_WIDTH if name == "gb" else 0) + hp * half
        proj[name, hp] = _dot(xn, w_ref[:, base:base + half])

    def store_z(c):
        z_ref[:, c:c + 256] = _dot(xn, w_ref[:, EVEN_FUSED + c:EVEN_FUSED + c + 256]).astype(BF16)

    pending = [functools.partial(store_z, c) for c in range(0, z_ref.shape[-1], 256)]
    pending.insert(1, functools.partial(project, "xb", 1))
    pending.insert(3, functools.partial(project, "gb", 1))

    def emit():
        if pending:
            pending.pop(0)()

    project("xb", 0)
    project("gb", 0)
    for hp in range(2):
        cols = slice(hp * half, (hp + 1) * half)
        ob_ref[:, cols] = _rglru_half(cols, proj["xb", hp], proj["gb", hp], emit, cw_ref, cb_ref, wr_ref, br_ref,
                                      wi_ref, bi_ref, lam_ref, xbuf, hcar, a_scr, u_scr, h_scr).astype(BF16)
    while pending:
        emit()


def _inproj_rglru(x2d, gain, w_bf16, seq, conv_w, conv_b, wr_bd, b_r, wi_bd, b_i, lam, tm=512):
    n, d = x2d.shape
    nc = w_bf16.shape[1]
    row = lambda v: v.reshape(1, B_WIDTH)
    full = lambda shp: pl.BlockSpec(shp, lambda i: (0,) * len(shp))
    return pl.pallas_call(
        functools.partial(_inproj_rglru_kernel, seq // tm),
        grid=(n // tm,),
        in_specs=[pl.BlockSpec((tm, d), lambda i: (i, 0)), full((1, d)), full((d, nc)),
                  full((B_CONV, B_WIDTH)), full((1, B_WIDTH)), full((4, LANES, LANES)), full((1, B_WIDTH)),
                  full((4, LANES, LANES)), full((1, B_WIDTH)), full((1, B_WIDTH))],
        out_specs=[pl.BlockSpec((tm, nc - EVEN_FUSED), lambda i: (i, 0)), pl.BlockSpec((tm, B_WIDTH), lambda i: (i, 0))],
        out_shape=[jax.ShapeDtypeStruct((n, nc - EVEN_FUSED), BF16), jax.ShapeDtypeStruct((n, B_WIDTH), BF16)],
        scratch_shapes=[pltpu.VMEM((tm + 8, B_WIDTH), F32), pltpu.VMEM((8, B_WIDTH), F32),
                        pltpu.VMEM((tm, B_WIDTH), F32), pltpu.VMEM((tm, B_WIDTH), F32), pltpu.VMEM((tm, B_WIDTH), F32)],
        compiler_params=_cparams(("arbitrary",)),
        name="inproj_rglru",
    )(x2d, gain.reshape(1, d), w_bf16, conv_w, row(conv_b), wr_bd, row(b_r), wi_bd, row(b_i), row(lam))


def _hgrn_consts():
    c = C_CHUNK
    t = np.arange(c)[:, None]
    s = np.arange(c)[None, :]
    masks = []
    hs = c // 2
    while hs >= 1:
        same = (t // (2 * hs)) == (s // (2 * hs))
        masks.append(same & ((t // hs) % 2 == 1) & ((s // hs) % 2 == 0))
        hs //= 2
    pairs = [np.concatenate(masks[n:n + 2], axis=1) for n in range(0, len(masks), 2)]
    return (s <= t).astype(np.float32), np.stack(pairs).astype(np.float32)


def _split_points(b, level):
    c, wdt = b.shape
    hs = c >> (level + 1)
    blk = 2 * hs
    if blk >= 8:
        return jnp.concatenate([jnp.broadcast_to(b[m * blk + hs - 1:m * blk + hs, :], (blk, wdt))
                                for m in range(c // blk)], axis=0)
    b3 = b.reshape(c // 8, 8, wdt)
    sub = lax.broadcasted_iota(jnp.int32, b3.shape, 1)
    if hs == 2:
        out = jnp.where(sub < 4, jnp.broadcast_to(b3[:, 1:2, :], b3.shape), jnp.broadcast_to(b3[:, 5:6, :], b3.shape))
    else:
        out = jnp.where((sub & 1) == 0, b3, pltpu.roll(b3, 1, 1))
    return out.reshape(c, wdt)


def _hgrn_chunk(lb, q_in, f_in, v_in, g_in, og_ref, mst_ref, msk_ref, st_scr, emit):
    c = C_CHUNK
    npair = msk_ref.shape[0]
    zeros = jnp.zeros((c, LANES), BF16)
    f = lb + (1.0 - lb) * jax.nn.sigmoid(f_in)
    g = jnp.log2(f)
    g_hi, g_lo = _split(g)
    b_all = _dot(mst_ref[...], g_hi) + _dot(mst_ref[...], g_lo)
    emit()
    kk_all = 1.0 - f
    qf_all = _silu(q_in)
    v_all = v_in
    kk_bf, qf_bf = kk_all.astype(BF16), qf_all.astype(BF16)
    heads = range(C_HEADS)
    cols = [slice(h * LANES, (h + 1) * LANES) for h in heads]
    sts = [st_scr[h] for h in heads]
    os = [_dot_nt((qf_all[:, cols[h]] * jnp.exp2(b_all[:, cols[h]])).astype(BF16), sts[h].astype(BF16)) for h in heads]
    att = [[] for _ in heads]
    bms = [_split_points(b_all, l) for l in range(2 * npair)]
    for n in range(npair):
        lhs, rhs = [], []
        for h in heads:
            b = b_all[:, cols[h]]
            qt, kt = [], []
            for l in (2 * n, 2 * n + 1):
                bm = bms[l][:, cols[h]]
                qt.append(qf_bf[:, cols[h]] * jnp.exp2(jnp.minimum(b - bm, 1.0)).astype(BF16))
                kt.append(kk_bf[:, cols[h]] * jnp.exp2(jnp.minimum(bm - b, 1.0)).astype(BF16))
            lhs.append(jnp.concatenate(qt, axis=1))
            rhs.append(jnp.concatenate([jnp.concatenate([kt[0], zeros], axis=1),
                                        jnp.concatenate([zeros, kt[1]], axis=1)], axis=0))
        prods = [_dot_nt(lhs[h], rhs[h]) for h in heads]
        if n == 1:
            emit()
        for h in heads:
            att[h].append(jnp.where(msk_ref[n] > 0.5, prods[h], 0.0).astype(BF16))
    vbs = [v_all[:, cols[h]].astype(BF16) for h in heads]
    intra = [_dot(jnp.concatenate(att[h], axis=1), jnp.concatenate([vbs[h]] * (2 * npair), axis=0)) for h in heads]
    b_last = b_all[c - 1:c, :]
    kd = (kk_all * jnp.exp2(b_last - b_all)).astype(BF16)
    upd = [_dot(v_all[:, cols[h]].T.astype(BF16), kd[:, cols[h]]) for h in heads]
    decay = jnp.exp2(b_last)
    diag = qf_all * kk_all
    outs = []
    for h in heads:
        st_scr[h] = sts[h] * decay[:, cols[h]] + upd[h]
        o = os[h] + intra[h] + jnp.sum(diag[:, cols[h]], axis=-1, keepdims=True) * v_all[:, cols[h]]
        ms = jnp.mean(o * o, axis=-1, keepdims=True)
        outs.append(o * lax.rsqrt(ms + EPS) * og_ref[...])
    return jnp.concatenate(outs, axis=1) * _silu(g_in)


def _inproj_hgrn_kernel(tiles_per_seq, x_ref, g_ref, w_ref, lb_ref, og_ref, mst_ref, msk_ref,
                        z_ref, kc_ref, vc_ref, oc_ref, st_scr):
    @pl.when(pl.program_id(0) % tiles_per_seq == 0)
    def _():
        st_scr[...] = jnp.zeros(st_scr.shape, F32)

    x = x_ref[...]
    ms = jnp.mean(x * x, axis=-1, keepdims=True)
    xn = (x * lax.rsqrt(ms + EPS) * g_ref[...]).astype(BF16)
    own = [_dot(xn, w_ref[:, k * C_WIDTH:(k + 1) * C_WIDTH]) for k in range(4)]

    o_refs = (z_ref, kc_ref, vc_ref)
    starts = np.cumsum([0] + [o.shape[-1] for o in o_refs])
    pending = list(range(0, int(starts[-1]), 256))

    def emit():
        if not pending:
            return
        c = pending.pop(0)
        w = min(256, int(starts[-1]) - c)
        r = _dot(xn, w_ref[:, ODD_FUSED + c:ODD_FUSED + c + w]).astype(BF16)
        for o_ref, lo, hi in zip(o_refs, starts[:-1], starts[1:]):
            a, b = max(c, int(lo)), min(c + w, int(hi))
            if a < b:
                o_ref[:, a - int(lo):b - int(lo)] = r[:, a - c:b - c]

    p = lb_ref[...]
    pm = jnp.maximum(p[0:1, :], p[1:2, :])
    e0, e1 = jnp.exp(p[0:1, :] - pm), jnp.exp(p[1:2, :] - pm)
    lb = e1 / (e0 + e1)
    for ch in range(x.shape[0] // C_CHUNK):
        rows = slice(ch * C_CHUNK, (ch + 1) * C_CHUNK)
        q_in, f_in, v_in, g_in = (a[rows, :] for a in own)
        oc_ref[rows, :] = _hgrn_chunk(lb, q_in, f_in, v_in, g_in, og_ref, mst_ref, msk_ref, st_scr, emit).astype(BF16)
    while pending:
        emit()


def _inproj_hgrn(x2d, gain, w_bf16, seq, c_lb, c_og, tm=512):
    n, d = x2d.shape
    nc = w_bf16.shape[1]
    assert ODD_FUSED + sum(ODD_SPLIT) == nc
    mst, msk = _hgrn_consts()
    full = lambda shp: pl.BlockSpec(shp, lambda i: (0,) * len(shp))
    widths = ODD_SPLIT + (C_WIDTH,)
    return pl.pallas_call(
        functools.partial(_inproj_hgrn_kernel, seq // tm),
        grid=(n // tm,),
        in_specs=[pl.BlockSpec((tm, d), lambda i: (i, 0)), full((1, d)), full((d, nc)),
                  full(c_lb.shape), full((1, C_HEAD_DIM)), full(mst.shape), full(msk.shape)],
        out_specs=[pl.BlockSpec((tm, w), lambda i: (i, 0)) for w in widths],
        out_shape=[jax.ShapeDtypeStruct((n, w), BF16) for w in widths],
        scratch_shapes=[pltpu.VMEM((C_HEADS, C_HEAD_DIM, C_HEAD_DIM), F32)],
        compiler_params=_cparams(("arbitrary",)),
        name="inproj_hgrn",
    )(x2d, gain.reshape(1, d), w_bf16, c_lb, c_og.reshape(1, C_HEAD_DIM), jnp.asarray(mst, BF16), jnp.asarray(msk, F32))


def _nsa_prep_kernel(kcr_ref, vcr_ref, ks_ref, vs_ref, kw_ref, vw_ref, cos_ref, sin_ref, cosc_ref, sinc_ref,
                     kncmp_ref, knslc_ref, knwin_ref, pek_ref, pev_ref, w1k_ref, w2k_ref, w1v_ref, w2vt_ref,
                     kc_o, vct_o, ks_o, vst_o, kw_o, vwt_o):
    def hidden(xr_ref, pe_ref, w1_ref):
        xr = xr_ref[0].astype(F32)
        top = _dot((xr + pe_ref[0:1, :]).astype(BF16), w1_ref[0])
        bot = _dot((xr + pe_ref[1:2, :]).astype(BF16), w1_ref[1])
        pre = top + pltpu.roll(bot, bot.shape[0] - 1, 0)
        return _silu(pre).astype(BF16)

    kc = _dot(hidden(kcr_ref, pek_ref, w1k_ref), w2k_ref[...])
    kc = _rope(_head_rms(kc, kncmp_ref[...]), cosc_ref[...], sinc_ref[...])
    d0, d1 = _split_kv(kc)
    kc_o[0, 0] = d0.astype(BF16)
    kc_o[0, 1] = d1.astype(BF16)
    vct = _dot_nt(w2vt_ref[...], hidden(vcr_ref, pev_ref, w1v_ref))
    for kv in range(D_KV_HEADS):
        vct_o[0, kv] = _with_ones_row(vct[kv * HEAD_DIM:(kv + 1) * HEAD_DIM, :]).astype(BF16)

    s = ks_ref.shape[1]
    step = 256
    for r0 in range(0, s, step):
        rows = slice(r0, r0 + step)
        cos, sin = cos_ref[rows, :], sin_ref[rows, :]
        ks = _rope(_head_rms(ks_ref[0, rows, :].astype(F32), knslc_ref[...]), cos, sin)
        kw = _rope(_head_rms(kw_ref[0, rows, :].astype(F32), knwin_ref[...]), cos, sin)
        blk = (r0 + _row((step, LANES))) // SEL_LEN
        onehot = jnp.where(_lane((step, LANES)) - HEAD_DIM == blk, MASK_BIG, 0.0)
        for src, dst, fill in ((ks, ks_o, onehot), (kw, kw_o, 0.0)):
            d0, d1 = _split_kv(src, fill)
            dst[0, 0, rows, :] = d0.astype(BF16)
            dst[0, 1, rows, :] = d1.astype(BF16)
        for src, dst in ((vs_ref, vst_o), (vw_ref, vwt_o)):
            vt = src[0, rows, :].astype(F32).T
            for kv in range(D_KV_HEADS):
                dst[0, kv, :, rows] = _with_ones_row(vt[kv * HEAD_DIM:(kv + 1) * HEAD_DIM, :]).astype(BF16)


def _nsa_prep(z, kcr, vcr, cos, sin_s, cosc, sinc, kn_cmp, kn_slc, kn_win, pek, pev, w1k, w2k, w1v, w2v):
    b, s, _ = z.shape
    nseg = s // CMP_STRIDE
    zb = lambda name: pl.BlockSpec((1, s, 128), lambda bi, o=ODD_OFF[name][0] // 128: (bi, 0, o))
    full = lambda shp: pl.BlockSpec(shp, lambda bi: (0,) * len(shp))
    seg = pl.BlockSpec((1, nseg, 2048), lambda bi: (bi, 0, 0))
    g2 = lambda g: jnp.tile(g, 2).reshape(1, 128)
    dup = lambda n: (jax.ShapeDtypeStruct((b, 2, n, 128), BF16), pl.BlockSpec((1, 2, n, 128), lambda bi: (bi, 0, 0, 0)))
    tr = lambda n: (jax.ShapeDtypeStruct((b, 2, VT_ROWS, n), BF16),
                    pl.BlockSpec((1, 2, VT_ROWS, n), lambda bi: (bi, 0, 0, 0)))
    outs = [dup(nseg), tr(nseg), dup(s), tr(s), dup(s), tr(s)]
    return pl.pallas_call(
        _nsa_prep_kernel,
        grid=(b,),
        in_specs=[seg, seg, zb("ksd"), zb("vsd"), zb("kwd"), zb("vwd"),
                  full((s, 128)), full((s, 128)), full((nseg, 128)), full((nseg, 128)),
                  full((1, 128)), full((1, 128)), full((1, 128)), full((2, 2048)), full((2, 2048)),
                  full((2, 2048, 256)), full((256, 128)), full((2, 2048, 256)), full((128, 256))],
        out_specs=[o[1] for o in outs],
        out_shape=[o[0] for o in outs],
        compiler_params=_cparams(("arbitrary",)),
        name="nsa_prep",
    )(kcr, vcr, z, z, z, z, cos, sin_s, cosc, sinc, g2(kn_cmp), g2(kn_slc), g2(kn_win), pek, pev, w1k, w2k, w1v, w2v)


SEL_CHUNK = 512
SEL_PIECE = 512


def _nsa_kernel(bound_ref, *refs):
    small = jnp.maximum(jnp.maximum(bound_ref[0], bound_ref[1]), bound_ref[2]) <= MAX_SAFE_BOUND

    @pl.when(small)
    def _():
        _nsa_body((bound_ref[0], bound_ref[1], bound_ref[2]), *refs)

    @pl.when(jnp.logical_not(small))
    def _():
        _nsa_body((None, None, None), *refs)


def _nsa_body(fixed, q_ref, gd_ref, gate_ref, cos_ref, sin_ref, qn_ref, kc_ref, vct_ref, ks_ref, vst_ref, kw_ref, vwt_ref,
              ovlt_ref, o_ref):
    fix_c, fix_s, fix_w = fixed
    i = pl.program_id(1)
    t = q_ref.shape[1]
    grp = D_HEADS // D_KV_HEADS
    w = grp * t
    kvs = range(D_KV_HEADS)
    q = _rope(_head_rms(q_ref[0].astype(F32), qn_ref[...]), cos_ref[...], sin_ref[...]) * Q_SCALE
    stack = lambda heads: [jnp.concatenate(heads[kv * grp:(kv + 1) * grp], axis=0) for kv in kvs]
    qs = stack(_q_heads(q))
    tq1 = i * t + _lane((1, t))

    nwin = D_WINDOW + t
    win0 = pl.multiple_of(jnp.maximum(i * t + t - nwin, 0), t)
    s_cmp = [_dot_nt(kc_ref[0, kv], qs[kv]) for kv in kvs]
    s_win = [_dot_nt(kw_ref[0, kv, pl.ds(win0, nwin), :], qs[kv]) for kv in kvs]

    ncmp = kc_ref.shape[2] - 1
    nsel = ovlt_ref.shape[0]
    crow = _row((kc_ref.shape[2], t))
    bias_c = _tile4(jnp.where((crow * CMP_STRIDE + (CMP_LEN - 1) <= tq1) & (crow < ncmp), 0.0, NEG_INF))
    row_ok = _tile4(tq1 >= CMP_LEN - 1)
    es = [_exp_cols(s_cmp[kv] + bias_c, fixed=fix_c)[0] for kv in kvs]
    accs = [_dot(vct_ref[0, kv], es[kv]) for kv in kvs]
    invs = [jnp.where(row_ok, 1.0 / accs[kv][HEAD_DIM:HEAD_DIM + 1, :], 0.0) for kv in kvs]
    oc_t = [accs[kv][0:HEAD_DIM, :] * invs[kv] for kv in kvs]
    psums = []
    for kv in kvs:
        p = es[kv].astype(F32) * invs[kv]
        psums.append(p[:, 0:t] + p[:, t:2 * t] + p[:, 2 * t:3 * t] + p[:, 3 * t:4 * t])
    imps = [_dot_split_rhs(ovlt_ref[...], psums[kv]) for kv in kvs]

    rel = tq1 - (win0 + _row((nwin, t)))
    bias_w = _tile4(jnp.where((rel >= 0) & (rel < D_WINDOW), 0.0, NEG_INF))
    ow_t = [_finish(_dot(vwt_ref[0, kv, :, pl.ds(win0, nwin)], _exp_cols(s_win[kv] + bias_w, fixed=fix_w)[0]))
            for kv in kvs]

    jrow = _row((nsel, t))
    jrow_f = jrow.astype(F32)
    cur = (i * t + _lane((nsel, t))) // SEL_LEN
    forced = (jrow == 0) | (jrow == cur)
    fills = []
    for kv in kvs:
        score = jnp.where(forced, POS_INF, jnp.where(jrow <= cur, imps[kv], NEG_INF))
        chosen = jnp.zeros((nsel, t), F32)
        for _ in range(min(SEL_TOPK, nsel)):
            mx = jnp.max(score, axis=0, keepdims=True)
            first = jnp.min(jnp.where(score == mx, jrow_f, 1e9), axis=0, keepdims=True)
            hit = jrow_f == first
            chosen = jnp.where(hit, 1.0, chosen)
            score = jnp.where(hit, -3e38, score)
        frame = jnp.concatenate([jnp.zeros((HEAD_DIM, t), F32), chosen - 1.0,
                                 jnp.zeros((LANES - HEAD_DIM - nsel, t), F32)], axis=0)
        fills.extend([frame.T] * (grp // 2))

    init = (jnp.full((1, w), NEG_INF, F32), jnp.zeros((VT_ROWS, w), F32))
    pc = SEL_PIECE

    def sweep(states, k_ref, vt_ref, queries, offs, biases, n):
        scores = [[_dot_nt(k_ref[0, kv, pl.ds(off, n), :], queries[kv]) for kv in kvs] for off in offs]
        for off, sc, bias in zip(offs, scores, biases):
            states = tuple(_online_update(states[kv], sc[kv] if bias is None else sc[kv] + bias,
                                          vt_ref[0, kv, :, pl.ds(off, n)], fix_s) for kv in kvs)
        return states

    qsel = stack(_q_heads(q, fills))
    ch = SEL_CHUNK
    own = pl.multiple_of(((i * t) // ch) * ch, ch)
    st = sweep((init, init), ks_ref, vst_ref, qsel, [own],
               [_tile4(jnp.where(own + _row((ch, t)) <= tq1, 0.0, NEG_INF))], ch)

    def sel_chunk(c, states):
        base = pl.multiple_of(c * ch, ch)
        return sweep(states, ks_ref, vst_ref, qsel, [base + n * pc for n in range(ch // pc)], [None] * (ch // pc), pc)

    st = lax.fori_loop(0, (i * t) // ch, sel_chunk, st)
    os_t = [_finish(acc) for (_, acc) in st]

    g_t = jax.nn.sigmoid(gate_ref[0].astype(F32)).T
    rows = []
    for h in range(D_HEADS):
        kv, g = divmod(h, grp)
        cols = slice(g * t, (g + 1) * t)
        r = D_BRANCHES * h
        rows.append(g_t[r:r + 1, :] * oc_t[kv][:, cols] + g_t[r + 1:r + 2, :] * os_t[kv][:, cols]
                    + g_t[r + 2:r + 3, :] * ow_t[kv][:, cols])
    o = jnp.concatenate(rows, axis=0).T
    o_ref[0] = (o * _silu(gd_ref[0].astype(F32))).astype(BF16)


def _overlap_t(s):
    ncmp = (s - CMP_LEN) // CMP_STRIDE + 1
    nsel = s // SEL_LEN
    cs = np.arange(ncmp)[None, :] * CMP_STRIDE
    ss = np.arange(nsel)[:, None] * SEL_LEN
    ovl = np.zeros((nsel, s // CMP_STRIDE), np.float32)
    ovl[:, :ncmp] = (cs < ss + SEL_LEN) & (cs + CMP_LEN > ss)
    return ovl


def _nsa(z, cos, sin_s, qn, bounds, kc, vct, ks, vst, kw, vwt, t=256):
    b, s, _ = z.shape
    nseg = s // CMP_STRIDE
    assert nseg == LANES, "compressed-block scores are laid out on one 128-row tile"
    ovlt = _overlap_t(s)
    full = lambda shp: pl.BlockSpec(shp, lambda bi, i: (0,) * len(shp))
    dup = lambda n: pl.BlockSpec((1, 2, n, 128), lambda bi, i: (bi, 0, 0, 0))
    tr = lambda n: pl.BlockSpec((1, 2, VT_ROWS, n), lambda bi, i: (bi, 0, 0, 0))
    return pl.pallas_call(
        _nsa_kernel,
        grid=(b, s // t),
        in_specs=[pl.BlockSpec(memory_space=pltpu.SMEM),
                  pl.BlockSpec((1, t, 512), lambda bi, i: (bi, i, ODD_OFF["qd"][0] // 512)),
                  pl.BlockSpec((1, t, 512), lambda bi, i: (bi, i, ODD_OFF["gd"][0] // 512)),
                  pl.BlockSpec((1, t, 128), lambda bi, i: (bi, i, ODD_OFF["gate"][0] // 128)),
                  pl.BlockSpec((t, 128), lambda bi, i: (i, 0)),
                  pl.BlockSpec((t, 128), lambda bi, i: (i, 0)),
                  full((1, 512)), dup(nseg), tr(nseg), dup(s), tr(s), dup(s), tr(s), full(ovlt.shape)],
        out_specs=pl.BlockSpec((1, t, 512), lambda bi, i: (bi, i, 0)),
        out_shape=jax.ShapeDtypeStruct((b, s, 512), BF16),
        compiler_params=_cparams(("arbitrary", "arbitrary")),
        name="nsa",
    )(bounds, z, z, z, cos, sin_s, jnp.tile(qn, 8).reshape(1, 512), kc, vct, ks, vst, kw, vwt, jnp.asarray(ovlt, BF16))


def _permute_cols(w, order, src):
    cols = []
    for name, width in order:
        o, sw = src[name]
        blk = w[:, o:o + sw]
        if sw < width:
            blk = jnp.pad(blk, ((0, 0), (0, width - sw)))
        cols.append(blk)
    return jnp.concatenate(cols, axis=1).astype(BF16)


def _block_diag_pairs(w):
    z = jnp.zeros((4, LANES, LANES), w.dtype)
    z = z.at[:, 0:64, 0:64].set(w[0::2])
    z = z.at[:, 64:128, 64:128].set(w[1::2])
    return z.astype(BF16)


def _rope_tables(pos):
    half = HEAD_DIM // 2
    inv = ROPE_THETA ** (-jnp.arange(half, dtype=F32) / half)
    ang = pos.astype(F32)[:, None] * inv[None, :]
    cos, sin = jnp.cos(ang), jnp.sin(ang)
    cos_t = jnp.tile(cos, (1, 4))
    sin_t = jnp.tile(jnp.concatenate([-sin, sin], axis=1), (1, 2))
    return cos_t, sin_t


def _expand_compress_w1(w1):
    hdim = w1.shape[1]
    w = w1.reshape(2, CMP_STRIDE, 1, HEAD_DIM, 1, hdim)
    same_head = jnp.eye(2, dtype=w1.dtype).reshape(1, 1, 2, 1, 2, 1)
    return (w * same_head).reshape(2, CMP_STRIDE * 2 * HEAD_DIM, 2 * hdim).astype(BF16)


def _expand_compress_w2(w2):
    hdim, hd = w2.shape
    z = jnp.zeros((2 * hdim, 2 * hd), w2.dtype)
    z = z.at[0:hdim, 0:hd].set(w2)
    z = z.at[hdim:, hd:].set(w2)
    return z.astype(BF16)


def _expand_pe(pe):
    p = pe.reshape(2, CMP_STRIDE, 1, HEAD_DIM)
    return jnp.broadcast_to(p, (2, CMP_STRIDE, 2, HEAD_DIM)).reshape(2, CMP_STRIDE * 2 * HEAD_DIM)


def _even_layer(h, mem, g, mem_g, w_mem_kv, m_qn, m_kn, w_in, w_out, a_qn, a_kn, a_sinks,
                conv_w, conv_b, w_r, b_r, w_i, b_i, lam, cos, sin_s):
    b, s, d = h.shape
    h2 = h.reshape(b * s, d)
    z, ob = _inproj_rglru(h2, g, _permute_cols(w_in, EVEN_ORDER, EVEN_SRC), s, conv_w, conv_b,
                          _block_diag_pairs(w_r), b_r, _block_diag_pairs(w_i), b_i, lam)
    z = z.reshape(b, s, EVEN_COLS)
    oa = _swa(z, cos, sin_s, a_qn, a_kn, a_sinks)
    out = _outproj_mem(h2, oa.reshape(b * s, 512), ob, z.reshape(b * s, EVEN_COLS), EVEN_OFF["qm"][0], EVEN_OFF["gm"][0],
                       s, mem, mem_g, w_mem_kv.astype(BF16), m_qn, m_kn, w_out.astype(BF16))
    return out.reshape(b, s, d)


def _odd_layer(h, mem, g, mem_g, w_mem_kv, m_qn, m_kn, w_in, w_out, c_lb, c_og,
               d_qn, d_kn_cmp, d_kn_slc, d_kn_win, pe_k, pe_v, w1k, w2k, w1v, w2v, cos, sin_s, cosc, sinc):
    b, s, d = h.shape
    h2 = h.reshape(b * s, d)
    z, kcd, vcd, oc = _inproj_hgrn(h2, g, _permute_cols(w_in, ODD_ORDER, ODD_SRC), s, c_lb, c_og)
    z = z.reshape(b, s, ODD_SPLIT[0])
    nseg = s // CMP_STRIDE
    seg = lambda a: a.reshape(b, nseg, CMP_STRIDE * 128)
    kc, vc, ks, vs, kw, vw = _nsa_prep(
        z, seg(kcd), seg(vcd), cos, sin_s, cosc, sinc, d_kn_cmp, d_kn_slc, d_kn_win,
        _expand_pe(pe_k), _expand_pe(pe_v), _expand_compress_w1(w1k), _expand_compress_w2(w2k),
        _expand_compress_w1(w1v), _expand_compress_w2(w2v).T)
    bounds = jnp.stack([_score_bound(d_qn, kn) for kn in (d_kn_cmp, d_kn_slc, d_kn_win)]).astype(F32)
    od = _nsa(z, cos, sin_s, d_qn, bounds, kc, vc, ks, vs, kw, vw)
    out = _outproj_mem(h2, oc, od.reshape(b * s, 512), z.reshape(b * s, ODD_SPLIT[0]), ODD_OFF["qm"][0], ODD_OFF["gm"][0],
                       s, mem, mem_g, w_mem_kv.astype(BF16), m_qn, m_kn, w_out.astype(BF16))
    return out.reshape(b, s, d)


def kernel(x, mem, norm_g, mem_norm_g, mem_w_kv, mem_qn, mem_kn, ev_w_in, ev_w_out, a_qn, a_kn, a_sinks,
           b_conv_w, b_conv_b, b_w_r, b_b_r, b_w_i, b_b_i, b_lambda, od_w_in, od_w_out, c_lb, c_onorm,
           d_qn, d_kn_cmp, d_kn_slc, d_kn_win, d_pe_k, d_pe_v, d_w1k, d_w2k, d_w1v, d_w2v):
    depth = norm_g.shape[0]
    assert depth == 2 and c_lb.shape[0] == 2, "the HGRN2 lower-bound formula in the kernel is written for depth 2"
    s = x.shape[1]
    assert s % 256 == 0 and s >= D_WINDOW
    pos = jnp.arange(s)
    cos, sin_s = _rope_tables(pos)
    nseg = s // CMP_STRIDE
    cmp_end = jnp.minimum(jnp.arange(nseg) * CMP_STRIDE + CMP_LEN - 1, s - 1)
    cosc, sinc = _rope_tables(cmp_end)
    h = _even_layer(x, mem, norm_g[0], mem_norm_g[0], mem_w_kv[0], mem_qn[0], mem_kn[0], ev_w_in[0], ev_w_out[0],
                    a_qn[0], a_kn[0], a_sinks[0], b_conv_w[0], b_conv_b[0], b_w_r[0], b_b_r[0], b_w_i[0], b_b_i[0],
                    b_lambda[0], cos, sin_s)
    h = _odd_layer(h, mem, norm_g[1], mem_norm_g[1], mem_w_kv[1], mem_qn[1], mem_kn[1], od_w_in[0], od_w_out[0],
                   c_lb, c_onorm[0], d_qn[0], d_kn_cmp[0], d_kn_slc[0], d_kn_win[0], d_pe_k[0], d_pe_v[0],
                   d_w1k[0], d_w2k[0], d_w1v[0], d_w2v[0], cos, sin_s, cosc, sinc)
    return h
```

```python
import functools

import numpy as np
import jax
import jax.numpy as jnp
from jax import lax
from jax.experimental import pallas as pl
from jax.experimental.pallas import tpu as pltpu

F32 = jnp.float32
BF16 = jnp.bfloat16

D_MODEL = 1024
N_MEM = 256
HEAD_DIM = 64
ROPE_THETA = 10000.0
EPS = 1e-6
NEG_INF = -1e30
POS_INF = 1e30
MASK_BIG = 1e30
LANES = 128

A_HEADS, A_KV_HEADS, A_WINDOW = 8, 2, 128
B_WIDTH, B_BLOCKS, B_CONV, B_C = 512, 8, 4, 8.0
M_HEADS = 4
C_HEADS, C_HEAD_DIM, C_CHUNK = 4, 128, 64
C_WIDTH = C_HEADS * C_HEAD_DIM
D_HEADS, D_KV_HEADS = 8, 2
CMP_LEN, CMP_STRIDE, CMP_HIDDEN = 32, 16, 128
SEL_LEN, SEL_TOPK = 64, 4
D_WINDOW = 512
D_BRANCHES = 3
SCALE = HEAD_DIM ** -0.5
LOG2E = 1.4426950408889634
Q_SCALE = SCALE * LOG2E

EVEN_ORDER = [("xb", 512), ("gb", 512), ("qa", 512), ("ga", 512), ("qm", 256), ("gm", 256), ("ka", 128), ("va", 128)]
EVEN_FUSED = 1024
EVEN_SRC = {"qa": (0, 512), "ka": (512, 128), "va": (640, 128), "ga": (768, 512), "xb": (1280, 512),
            "gb": (1792, 512), "qm": (2304, 256), "gm": (2560, 256)}
ODD_ORDER = [("qc", 512), ("fc", 512), ("ic", 512), ("gc", 512), ("qd", 512), ("gd", 512), ("qm", 256), ("gm", 256),
             ("ksd", 128), ("vsd", 128), ("kwd", 128), ("vwd", 128), ("gate", 128), ("kcd", 128), ("vcd", 128)]
ODD_FUSED = 2048
ODD_SPLIT = (2176, 128, 128)
ODD_SRC = {"qc": (0, 512), "fc": (512, 512), "ic": (1024, 512), "gc": (1536, 512), "qd": (2048, 512),
           "kcd": (2560, 128), "vcd": (2688, 128), "ksd": (2816, 128), "vsd": (2944, 128), "kwd": (3072, 128),
           "vwd": (3200, 128), "gate": (3328, 24), "gd": (3352, 512), "qm": (3864, 256), "gm": (4120, 256)}

VMEM_LIMIT = 48 * 1024 * 1024


def _offsets(order):
    off, out = 0, {}
    for name, w in order:
        out[name] = (off, w)
        off += w
    return out, off


EVEN_OFF, EVEN_COLS = _offsets(EVEN_ORDER[2:])
ODD_OFF, ODD_COLS = _offsets(ODD_ORDER[4:])


def _cparams(sem):
    return pltpu.CompilerParams(dimension_semantics=sem, vmem_limit_bytes=VMEM_LIMIT)


def _dot(a, b):
    return jnp.dot(a, b, preferred_element_type=F32)


def _dot_nt(a, b):
    return lax.dot_general(a, b, (((1,), (1,)), ((), ())), preferred_element_type=F32)


def _split(x):
    hi = x.astype(BF16)
    lo = (x - hi.astype(F32)).astype(BF16)
    return hi, lo


def _dot_split_rhs(m, x):
    hi, lo = _split(x)
    return _dot(m, hi) + _dot(m, lo)


def _lane(shape):
    return lax.broadcasted_iota(jnp.int32, shape, len(shape) - 1)


def _row(shape):
    return lax.broadcasted_iota(jnp.int32, shape, len(shape) - 2)


def _silu(x):
    return x * jax.nn.sigmoid(x)


def _seg_ones():
    r = lax.broadcasted_iota(jnp.int32, (LANES, LANES), 0) >> 6
    c = lax.broadcasted_iota(jnp.int32, (LANES, LANES), 1) >> 6
    return jnp.where(r == c, 1.0, 0.0).astype(BF16)


def _head_rms(x, gain):
    seg = _seg_ones()
    cols = []
    for c in range(x.shape[1] // LANES):
        xc = x[:, c * LANES:(c + 1) * LANES]
        ms = _dot((xc * xc).astype(BF16), seg) * (1.0 / HEAD_DIM)
        cols.append(xc * lax.rsqrt(ms + EPS))
    y = cols[0] if len(cols) == 1 else jnp.concatenate(cols, axis=1)
    return y * gain


def _rope(x, cos, sin_s):
    first = (_lane((x.shape[0], LANES)) & 63) < 32
    cols = []
    for c in range(x.shape[1] // LANES):
        xc = x[:, c * LANES:(c + 1) * LANES]
        sw = jnp.where(first, pltpu.roll(xc, 96, 1), pltpu.roll(xc, 32, 1))
        cols.append(xc * cos + sw * sin_s)
    return cols[0] if len(cols) == 1 else jnp.concatenate(cols, axis=1)


def _split_kv(k, fill=0.0):
    lo = _lane(k.shape) < 64
    return jnp.where(lo, k, fill), jnp.where(lo, pltpu.roll(k, 64, 1), fill)


def _q_heads(q, fills=None):
    lo = _lane((q.shape[0], LANES)) < 64
    out = []
    for c in range(q.shape[1] // LANES):
        qc = q[:, c * LANES:(c + 1) * LANES]
        fill = 0.0 if fills is None else fills[c]
        out.append(jnp.where(lo, qc, fill).astype(BF16))
        out.append(jnp.where(lo, pltpu.roll(qc, 64, 1), fill).astype(BF16))
    return out


VT_ROWS = HEAD_DIM + 16


def _with_ones_row(vt):
    n = vt.shape[1]
    pad = jnp.where(_row((VT_ROWS - HEAD_DIM, n)) == 0, 1.0, 0.0)
    return jnp.concatenate([vt, pad], axis=0)


MAX_SAFE_BOUND = 60.0


def _score_bound(q_gain, k_gain):
    return 1.02 * HEAD_DIM * SCALE * LOG2E * jnp.max(jnp.abs(q_gain)) * jnp.max(jnp.abs(k_gain))


def _with_fixed_reference(bound, body):
    small = bound <= MAX_SAFE_BOUND

    @pl.when(small)
    def _():
        body(bound)

    @pl.when(jnp.logical_not(small))
    def _():
        body(None)


def _exp_cols(s, extra=None, fixed=None):
    if fixed is not None:
        return jnp.exp2(s - fixed).astype(BF16), fixed
    m = jnp.max(s, axis=0, keepdims=True)
    if extra is not None:
        m = jnp.maximum(m, extra)
    return jnp.exp2(s - m).astype(BF16), m


def _finish(acc, extra_den=None):
    den = acc[HEAD_DIM:HEAD_DIM + 1, :]
    if extra_den is not None:
        den = den + extra_den
    return acc[0:HEAD_DIM, :] * (1.0 / den)


def _online_update(state, s, vt_tile, fixed=None):
    m, acc = state
    if fixed is not None:
        return m, acc + _dot(vt_tile, _exp_cols(s, fixed=fixed)[0])
    p, m_new = _exp_cols(s, m)
    acc = jnp.exp2(m - m_new) * acc + _dot(vt_tile, p)
    return m_new, acc


def _tile4(x):
    return jnp.concatenate([x, x, x, x], axis=1)


def _outproj_mem_kernel(tiles_per_seq, bound_ref, h_ref, a_ref, b_ref, q_ref, gm_ref, mem_ref, mg_ref, wkv_ref,
                        qn_ref, kn_ref, w_ref, o_ref, k_scr, vt_scr, acc_scr):
    @pl.when(pl.program_id(0) % tiles_per_seq == 0)
    def _():
        m = mem_ref[0]
        ms = jnp.mean(m * m, axis=-1, keepdims=True)
        mn = (m * lax.rsqrt(ms + EPS) * mg_ref[...]).astype(BF16)
        kv = _dot(mn, wkv_ref[...])
        km = _head_rms(kv[:, 0:256], kn_ref[...])
        for c in range(2):
            k0, k1 = _split_kv(km[:, c * LANES:(c + 1) * LANES])
            k_scr[2 * c] = k0.astype(BF16)
            k_scr[2 * c + 1] = k1.astype(BF16)
        vt = kv[:, 256:512].T
        for h in range(M_HEADS):
            vt_scr[h] = _with_ones_row(vt[h * HEAD_DIM:(h + 1) * HEAD_DIM, :]).astype(BF16)

    def body(fixed):
        a, b = a_ref[...], b_ref[...]
        pending = list(range(0, D_MODEL, 256))

        def emit():
            if pending:
                c = pending.pop(0)
                acc_scr[:, c:c + 256] = (h_ref[:, c:c + 256] + _dot(a, w_ref[0:512, c:c + 256])
                                         + _dot(b, w_ref[512:1024, c:c + 256]))

        q = _head_rms(q_ref[...].astype(F32), qn_ref[...]) * Q_SCALE
        scores = [_dot_nt(k_scr[h], qh) for h, qh in enumerate(_q_heads(q))]
        rows = []
        for h, sc in enumerate(scores):
            emit()
            rows.append(_finish(_dot(vt_scr[h], _exp_cols(sc, fixed=fixed)[0])))
        om = (jnp.concatenate(rows, axis=0).T * _silu(gm_ref[...].astype(F32))).astype(BF16)
        while pending:
            emit()
        for c in range(0, D_MODEL, 256):
            o_ref[:, c:c + 256] = acc_scr[:, c:c + 256] + _dot(om, w_ref[1024:1280, c:c + 256])

    _with_fixed_reference(bound_ref[0], body)


def _outproj_mem(h2d, oa, ob, z2d, q_off, g_off, seq, mem, mem_g, wkv_bf16, qn, kn, w_bf16, tm=1024):
    n, d = h2d.shape
    full = lambda shp: pl.BlockSpec(shp, lambda i: (0,) * len(shp))
    tps = seq // tm
    return pl.pallas_call(
        functools.partial(_outproj_mem_kernel, tps),
        grid=(n // tm,),
        in_specs=[pl.BlockSpec(memory_space=pltpu.SMEM),
                  pl.BlockSpec((tm, d), lambda i: (i, 0)),
                  pl.BlockSpec((tm, 512), lambda i: (i, 0)),
                  pl.BlockSpec((tm, 512), lambda i: (i, 0)),
                  pl.BlockSpec((tm, 256), lambda i: (i, q_off // 256)),
                  pl.BlockSpec((tm, 256), lambda i: (i, g_off // 256)),
                  pl.BlockSpec((1, N_MEM, D_MODEL), lambda i: (i // tps, 0, 0)),
                  full((1, D_MODEL)), full((D_MODEL, 512)), full((1, 256)), full((1, 256)), full((1280, d))],
        out_specs=pl.BlockSpec((tm, d), lambda i: (i, 0)),
        out_shape=jax.ShapeDtypeStruct((n, d), F32),
        scratch_shapes=[pltpu.VMEM((M_HEADS, N_MEM, LANES), BF16), pltpu.VMEM((M_HEADS, VT_ROWS, N_MEM), BF16),
                        pltpu.VMEM((tm, d), F32)],
        compiler_params=_cparams(("arbitrary",)),
        name="outproj_mem",
    )(_score_bound(qn, kn).reshape(1).astype(F32), h2d, oa, ob, z2d, z2d, mem, mem_g.reshape(1, D_MODEL), wkv_bf16,
      jnp.tile(qn, 4).reshape(1, 256), jnp.tile(kn, 4).reshape(1, 256), w_bf16)


def _swa_kernel(bound_ref, *refs):
    _with_fixed_reference(bound_ref[0], functools.partial(_swa_body, refs))


def _swa_body(refs, fixed):
    (q_ref, g_ref, kc_ref, kp_ref, vc_ref, vp_ref, cosc_ref, sinc_ref, cosp_ref, sinp_ref,
     qn_ref, kn_ref, sink_ref, o_ref) = refs
    i = pl.program_id(1)
    t = q_ref.shape[1]
    tp = kp_ref.shape[1]
    grp = A_HEADS // A_KV_HEADS
    q = _rope(_head_rms(q_ref[0].astype(F32), qn_ref[...]), cosc_ref[...], sinc_ref[...]) * Q_SCALE
    qh = _q_heads(q)
    kc = _rope(_head_rms(kc_ref[0].astype(F32), kn_ref[...]), cosc_ref[...], sinc_ref[...])
    kp = _rope(_head_rms(kp_ref[0].astype(F32), kn_ref[...]), cosp_ref[...], sinp_ref[...])
    ks = [x.astype(BF16) for x in _split_kv(jnp.concatenate([kp, kc], axis=0))]
    vt = jnp.concatenate([vp_ref[0].astype(F32).T, vc_ref[0].astype(F32).T], axis=1)
    ts = i * t - tp + _row((tp + t, t))
    rel = i * t + _lane((tp + t, t)) - ts
    bias = _tile4(jnp.where((rel >= 0) & (rel < A_WINDOW) & (ts >= 0), 0.0, NEG_INF))
    scores = [_dot_nt(ks[kv], jnp.concatenate(qh[kv * grp:(kv + 1) * grp], axis=0)) for kv in range(A_KV_HEADS)]
    rows = []
    for kv in range(A_KV_HEADS):
        sink = jnp.concatenate([jnp.broadcast_to(sink_ref[kv * grp + g:kv * grp + g + 1, 0:1], (1, t))
                                for g in range(grp)], axis=1) * LOG2E
        p, m = _exp_cols(scores[kv] + bias, sink, fixed)
        vt_kv = _with_ones_row(vt[kv * HEAD_DIM:(kv + 1) * HEAD_DIM, :]).astype(BF16)
        o_t = _finish(_dot(vt_kv, p), jnp.exp2(sink - m))
        rows.extend(o_t[:, g * t:(g + 1) * t] for g in range(grp))
    o = jnp.concatenate(rows, axis=0).T
    o_ref[0] = (o * _silu(g_ref[0].astype(F32))).astype(BF16)


def _swa(z, cos, sin_s, qn, kn, sinks, t=256):
    b, s, _ = z.shape
    tp = A_WINDOW
    qo, go = EVEN_OFF["qa"][0] // 512, EVEN_OFF["ga"][0] // 512
    ko, vo = EVEN_OFF["ka"][0] // 128, EVEN_OFF["va"][0] // 128
    prev = lambda i: jnp.maximum(i * (t // tp) - 1, 0)
    bound = jnp.maximum(_score_bound(qn, kn), jnp.max(sinks) * LOG2E).reshape(1).astype(F32)
    return pl.pallas_call(
        _swa_kernel,
        grid=(b, s // t),
        in_specs=[pl.BlockSpec(memory_space=pltpu.SMEM),
                  pl.BlockSpec((1, t, 512), lambda bi, i: (bi, i, qo)),
                  pl.BlockSpec((1, t, 512), lambda bi, i: (bi, i, go)),
                  pl.BlockSpec((1, t, 128), lambda bi, i: (bi, i, ko)),
                  pl.BlockSpec((1, tp, 128), lambda bi, i: (bi, prev(i), ko)),
                  pl.BlockSpec((1, t, 128), lambda bi, i: (bi, i, vo)),
                  pl.BlockSpec((1, tp, 128), lambda bi, i: (bi, prev(i), vo)),
                  pl.BlockSpec((t, 128), lambda bi, i: (i, 0)),
                  pl.BlockSpec((t, 128), lambda bi, i: (i, 0)),
                  pl.BlockSpec((tp, 128), lambda bi, i: (prev(i), 0)),
                  pl.BlockSpec((tp, 128), lambda bi, i: (prev(i), 0)),
                  pl.BlockSpec((1, 512), lambda bi, i: (0, 0)),
                  pl.BlockSpec((1, 128), lambda bi, i: (0, 0)),
                  pl.BlockSpec((A_HEADS, 128), lambda bi, i: (0, 0))],
        out_specs=pl.BlockSpec((1, t, 512), lambda bi, i: (bi, i, 0)),
        out_shape=jax.ShapeDtypeStruct((b, s, 512), BF16),
        compiler_params=_cparams(("arbitrary", "arbitrary")),
        name="swa",
    )(bound, z, z, z, z, z, z, cos, sin_s, cos, sin_s, jnp.tile(qn, 8).reshape(1, 512), jnp.tile(kn, 2).reshape(1, 128),
      jnp.broadcast_to(sinks.reshape(A_HEADS, 1), (A_HEADS, 128)))


def _rglru_half(cols, xb, gb, emit, cw_ref, cb_ref, wr_ref, br_ref, wi_ref, bi_ref, lam_ref, xbuf, hcar, a_scr, u_scr, h_scr):
    t, wdt = xb.shape
    xbuf[8:t + 8, cols] = xb
    xc = cb_ref[:, cols] + cw_ref[0:1, cols] * xbuf[5:t + 5, cols]
    for j in range(1, B_CONV):
        xc = xc + cw_ref[j:j + 1, cols] * xbuf[5 + j:t + 5 + j, cols]
    xbuf[0:8, cols] = xbuf[t:t + 8, cols]
    emit()

    r_cols, i_cols = [], []
    for c in range(wdt // LANES):
        blk = cols.start // LANES + c
        xcc = xc[:, c * LANES:(c + 1) * LANES].astype(BF16)
        r_cols.append(_dot(xcc, wr_ref[blk]))
        i_cols.append(_dot(xcc, wi_ref[blk]))
    emit()
    r = jax.nn.sigmoid(jnp.concatenate(r_cols, axis=1) + br_ref[:, cols])
    ig = jax.nn.sigmoid(jnp.concatenate(i_cols, axis=1) + bi_ref[:, cols])
    nl = -lam_ref[:, cols]
    softplus = jnp.maximum(nl, 0.0) + jnp.log(1.0 + jnp.exp(-jnp.abs(nl)))
    log_a = -B_C * r * softplus
    a = jnp.exp(log_a)
    om = 1.0 - a * a
    u = om * lax.rsqrt(jnp.maximum(om, 1e-30)) * (ig * xc)
    emit()

    a = a.reshape(t // 8, 8, wdt)
    u = u.reshape(t // 8, 8, wdt)
    r8 = lax.broadcasted_iota(jnp.int32, a.shape, 1)
    for d in (1, 2, 4):
        a_sh = pltpu.roll(a, d, 1)
        u_sh = pltpu.roll(u, d, 1)
        m = r8 >= d
        u = jnp.where(m, a * u_sh + u, u)
        a = jnp.where(m, a * a_sh, a)
    a_scr[:, cols] = a.reshape(t, wdt)
    u_scr[:, cols] = u.reshape(t, wdt)
    emit()

    def body(j, h):
        off = pl.multiple_of(j * 8, 8)
        hh = a_scr[pl.ds(off, 8), cols] * h + u_scr[pl.ds(off, 8), cols]
        h_scr[pl.ds(off, 8), cols] = hh
        return hh[7:8, :]

    h_last = lax.fori_loop(0, t // 8, body, hcar[0:1, cols], unroll=True)
    hcar[0:1, cols] = h_last
    emit()
    return h_scr[:, cols] * _silu(gb)


def _inproj_rglru_kernel(tiles_per_seq, x_ref, g_ref, w_ref, cw_ref, cb_ref, wr_ref, br_ref, wi_ref, bi_ref, lam_ref,
                         z_ref, ob_ref, xbuf, hcar, a_scr, u_scr, h_scr):
    @pl.when(pl.program_id(0) % tiles_per_seq == 0)
    def _():
        xbuf[0:8, :] = jnp.zeros((8, B_WIDTH), F32)
        hcar[...] = jnp.zeros((8, B_WIDTH), F32)

    x = x_ref[...]
    ms = jnp.mean(x * x, axis=-1, keepdims=True)
    xn = (x * lax.rsqrt(ms + EPS) * g_ref[...]).astype(BF16)
    half = B_WIDTH // 2
    proj = {}

    def project(name, hp):
        base = (B_WIDTH if name == "gb" else 0) + hp * half
        proj[name, hp] = _dot(xn, w_ref[:, base:base + half])

    def store_z(c):
        z_ref[:, c:c + 256] = _dot(xn, w_ref[:, EVEN_FUSED + c:EVEN_FUSED + c + 256]).astype(BF16)

    pending = [functools.partial(store_z, c) for c in range(0, z_ref.shape[-1], 256)]
    pending.insert(1, functools.partial(project, "xb", 1))
    pending.insert(3, functools.partial(project, "gb", 1))

    def emit():
        if pending:
            pending.pop(0)()

    project("xb", 0)
    project("gb", 0)
    for hp in range(2):
        cols = slice(hp * half, (hp + 1) * half)
        ob_ref[:, cols] = _rglru_half(cols, proj["xb", hp], proj["gb", hp], emit, cw_ref, cb_ref, wr_ref, br_ref,
                                      wi_ref, bi_ref, lam_ref, xbuf, hcar, a_scr, u_scr, h_scr).astype(BF16)
    while pending:
        emit()


def _inproj_rglru(x2d, gain, w_bf16, seq, conv_w, conv_b, wr_bd, b_r, wi_bd, b_i, lam, tm=512):
    n, d = x2d.shape
    nc = w_bf16.shape[1]
    row = lambda v: v.reshape(1, B_WIDTH)
    full = lambda shp: pl.BlockSpec(shp, lambda i: (0,) * len(shp))
    return pl.pallas_call(
        functools.partial(_inproj_rglru_kernel, seq // tm),
        grid=(n // tm,),
        in_specs=[pl.BlockSpec((tm, d), lambda i: (i, 0)), full((1, d)), full((d, nc)),
                  full((B_CONV, B_WIDTH)), full((1, B_WIDTH)), full((4, LANES, LANES)), full((1, B_WIDTH)),
                  full((4, LANES, LANES)), full((1, B_WIDTH)), full((1, B_WIDTH))],
        out_specs=[pl.BlockSpec((tm, nc - EVEN_FUSED), lambda i: (i, 0)), pl.BlockSpec((tm, B_WIDTH), lambda i: (i, 0))],
        out_shape=[jax.ShapeDtypeStruct((n, nc - EVEN_FUSED), BF16), jax.ShapeDtypeStruct((n, B_WIDTH), BF16)],
        scratch_shapes=[pltpu.VMEM((tm + 8, B_WIDTH), F32), pltpu.VMEM((8, B_WIDTH), F32),
                        pltpu.VMEM((tm, B_WIDTH), F32), pltpu.VMEM((tm, B_WIDTH), F32), pltpu.VMEM((tm, B_WIDTH), F32)],
        compiler_params=_cparams(("arbitrary",)),
        name="inproj_rglru",
    )(x2d, gain.reshape(1, d), w_bf16, conv_w, row(conv_b), wr_bd, row(b_r), wi_bd, row(b_i), row(lam))


def _hgrn_consts():
    c = C_CHUNK
    t = np.arange(c)[:, None]
    s = np.arange(c)[None, :]
    masks = []
    hs = c // 2
    while hs >= 1:
        same = (t // (2 * hs)) == (s // (2 * hs))
        masks.append(same & ((t // hs) % 2 == 1) & ((s // hs) % 2 == 0))
        hs //= 2
    pairs = [np.concatenate(masks[n:n + 2], axis=1) for n in range(0, len(masks), 2)]
    return (s <= t).astype(np.float32), np.stack(pairs).astype(np.float32)


def _split_points(b, level):
    c, wdt = b.shape
    hs = c >> (level + 1)
    blk = 2 * hs
    if blk >= 8:
        return jnp.concatenate([jnp.broadcast_to(b[m * blk + hs - 1:m * blk + hs, :], (blk, wdt))
                                for m in range(c // blk)], axis=0)
    b3 = b.reshape(c // 8, 8, wdt)
    sub = lax.broadcasted_iota(jnp.int32, b3.shape, 1)
    if hs == 2:
        out = jnp.where(sub < 4, jnp.broadcast_to(b3[:, 1:2, :], b3.shape), jnp.broadcast_to(b3[:, 5:6, :], b3.shape))
    else:
        out = jnp.where((sub & 1) == 0, b3, pltpu.roll(b3, 1, 1))
    return out.reshape(c, wdt)


def _hgrn_chunk(lb, q_in, f_in, v_in, g_in, og_ref, mst_ref, msk_ref, st_scr, emit, emit2):
    c = C_CHUNK
    npair = msk_ref.shape[0]
    zeros = jnp.zeros((c, LANES), BF16)
    f = lb + (1.0 - lb) * jax.nn.sigmoid(f_in)
    g = jnp.log2(f)
    g_hi, g_lo = _split(g)
    b_all = _dot(mst_ref[...], g_hi) + _dot(mst_ref[...], g_lo)
    emit()
    kk_all = 1.0 - f
    qf_all = _silu(q_in)
    v_all = v_in
    kk_bf, qf_bf = kk_all.astype(BF16), qf_all.astype(BF16)
    heads = range(C_HEADS)
    cols = [slice(h * LANES, (h + 1) * LANES) for h in heads]
    sts = [st_scr[h] for h in heads]
    os = [_dot_nt((qf_all[:, cols[h]] * jnp.exp2(b_all[:, cols[h]])).astype(BF16), sts[h].astype(BF16)) for h in heads]
    att = [[] for _ in heads]
    bms = [_split_points(b_all, l) for l in range(2 * npair)]
    for n in range(npair):
        lhs, rhs = [], []
        for h in heads:
            b = b_all[:, cols[h]]
            qt, kt = [], []
            for l in (2 * n, 2 * n + 1):
                bm = bms[l][:, cols[h]]
                qt.append(qf_bf[:, cols[h]] * jnp.exp2(jnp.minimum(b - bm, 1.0)).astype(BF16))
                kt.append(kk_bf[:, cols[h]] * jnp.exp2(jnp.minimum(bm - b, 1.0)).astype(BF16))
            lhs.append(jnp.concatenate(qt, axis=1))
            rhs.append(jnp.concatenate([jnp.concatenate([kt[0], zeros], axis=1),
                                        jnp.concatenate([zeros, kt[1]], axis=1)], axis=0))
        prods = [_dot_nt(lhs[h], rhs[h]) for h in heads]
        if n == 1:
            emit2()
        for h in heads:
            att[h].append(jnp.where(msk_ref[n] > 0.5, prods[h], 0.0).astype(BF16))
    vbs = [v_all[:, cols[h]].astype(BF16) for h in heads]
    intra = [_dot(jnp.concatenate(att[h], axis=1), jnp.concatenate([vbs[h]] * (2 * npair), axis=0)) for h in heads]
    b_last = b_all[c - 1:c, :]
    kd = (kk_all * jnp.exp2(b_last - b_all)).astype(BF16)
    upd = [_dot(v_all[:, cols[h]].T.astype(BF16), kd[:, cols[h]]) for h in heads]
    decay = jnp.exp2(b_last)
    diag = qf_all * kk_all
    outs = []
    for h in heads:
        st_scr[h] = sts[h] * decay[:, cols[h]] + upd[h]
        o = os[h] + intra[h] + jnp.sum(diag[:, cols[h]], axis=-1, keepdims=True) * v_all[:, cols[h]]
        ms = jnp.mean(o * o, axis=-1, keepdims=True)
        outs.append(o * lax.rsqrt(ms + EPS) * og_ref[...])
    return jnp.concatenate(outs, axis=1) * _silu(g_in)


def _inproj_hgrn_kernel(tiles_per_seq, x_ref, g_ref, w_ref, lb_ref, og_ref, mst_ref, msk_ref,
                        z_ref, kc_ref, vc_ref, oc_ref, st_scr):
    @pl.when(pl.program_id(0) % tiles_per_seq == 0)
    def _():
        st_scr[...] = jnp.zeros(st_scr.shape, F32)

    x = x_ref[...]
    ms = jnp.mean(x * x, axis=-1, keepdims=True)
    xn = (x * lax.rsqrt(ms + EPS) * g_ref[...]).astype(BF16)
    rows_half = x.shape[0] // 2
    own = {}

    def project_own(k, half):
        r = slice(half * rows_half, (half + 1) * rows_half)
        own[k, half] = _dot(xn[r, :], w_ref[:, k * C_WIDTH:(k + 1) * C_WIDTH])

    o_refs = (z_ref, kc_ref, vc_ref)
    starts = np.cumsum([0] + [o.shape[-1] for o in o_refs])

    def project_rest(c):
        w = min(256, int(starts[-1]) - c)
        r = _dot(xn, w_ref[:, ODD_FUSED + c:ODD_FUSED + c + w]).astype(BF16)
        for o_ref, lo, hi in zip(o_refs, starts[:-1], starts[1:]):
            a, b = max(c, int(lo)), min(c + w, int(hi))
            if a < b:
                o_ref[:, a - int(lo):b - int(lo)] = r[:, a - c:b - c]

    pending = [functools.partial(project_own, k, 1) for k in range(4)]
    pending += [functools.partial(project_rest, c) for c in range(0, int(starts[-1]), 256)]

    def emit():
        if pending:
            pending.pop(0)()

    for k in range(4):
        project_own(k, 0)

    p = lb_ref[...]
    pm = jnp.maximum(p[0:1, :], p[1:2, :])
    e0, e1 = jnp.exp(p[0:1, :] - pm), jnp.exp(p[1:2, :] - pm)
    lb = e1 / (e0 + e1)
    nchunk = x.shape[0] // C_CHUNK
    for ch in range(nchunk):
        half, sub = divmod(ch, nchunk // 2)
        rows = slice(ch * C_CHUNK, (ch + 1) * C_CHUNK)
        part = slice(sub * C_CHUNK, (sub + 1) * C_CHUNK)
        q_in, f_in, v_in, g_in = (own[k, half][part, :] for k in range(4))
        second = emit if ch < 6 else (lambda: None)
        oc_ref[rows, :] = _hgrn_chunk(lb, q_in, f_in, v_in, g_in, og_ref, mst_ref, msk_ref, st_scr, emit,
                                      second).astype(BF16)
    while pending:
        emit()


def _inproj_hgrn(x2d, gain, w_bf16, seq, c_lb, c_og, tm=512):
    n, d = x2d.shape
    nc = w_bf16.shape[1]
    assert ODD_FUSED + sum(ODD_SPLIT) == nc
    mst, msk = _hgrn_consts()
    full = lambda shp: pl.BlockSpec(shp, lambda i: (0,) * len(shp))
    widths = ODD_SPLIT + (C_WIDTH,)
    return pl.pallas_call(
        functools.partial(_inproj_hgrn_kernel, seq // tm),
        grid=(n // tm,),
        in_specs=[pl.BlockSpec((tm, d), lambda i: (i, 0)), full((1, d)), full((d, nc)),
                  full(c_lb.shape), full((1, C_HEAD_DIM)), full(mst.shape), full(msk.shape)],
        out_specs=[pl.BlockSpec((tm, w), lambda i: (i, 0)) for w in widths],
        out_shape=[jax.ShapeDtypeStruct((n, w), BF16) for w in widths],
        scratch_shapes=[pltpu.VMEM((C_HEADS, C_HEAD_DIM, C_HEAD_DIM), F32)],
        compiler_params=_cparams(("arbitrary",)),
        name="inproj_hgrn",
    )(x2d, gain.reshape(1, d), w_bf16, c_lb, c_og.reshape(1, C_HEAD_DIM), jnp.asarray(mst, BF16), jnp.asarray(msk, F32))


def _nsa_prep_kernel(kcr_ref, vcr_ref, ks_ref, vs_ref, kw_ref, vw_ref, cos_ref, sin_ref, cosc_ref, sinc_ref,
                     kncmp_ref, knslc_ref, knwin_ref, pek_ref, pev_ref, w1k_ref, w2k_ref, w1v_ref, w2vt_ref,
                     kc_o, vct_o, ks_o, vst_o, kw_o, vwt_o):
    def hidden(xr_ref, pe_ref, w1_ref):
        xr = xr_ref[0].astype(F32)
        top = _dot((xr + pe_ref[0:1, :]).astype(BF16), w1_ref[0])
        bot = _dot((xr + pe_ref[1:2, :]).astype(BF16), w1_ref[1])
        pre = top + pltpu.roll(bot, bot.shape[0] - 1, 0)
        return _silu(pre).astype(BF16)

    kc = _dot(hidden(kcr_ref, pek_ref, w1k_ref), w2k_ref[...])
    kc = _rope(_head_rms(kc, kncmp_ref[...]), cosc_ref[...], sinc_ref[...])
    d0, d1 = _split_kv(kc)
    kc_o[0, 0] = d0.astype(BF16)
    kc_o[0, 1] = d1.astype(BF16)
    vct = _dot_nt(w2vt_ref[...], hidden(vcr_ref, pev_ref, w1v_ref))
    for kv in range(D_KV_HEADS):
        vct_o[0, kv] = _with_ones_row(vct[kv * HEAD_DIM:(kv + 1) * HEAD_DIM, :]).astype(BF16)

    s = ks_ref.shape[1]
    step = 256
    for r0 in range(0, s, step):
        rows = slice(r0, r0 + step)
        cos, sin = cos_ref[rows, :], sin_ref[rows, :]
        ks = _rope(_head_rms(ks_ref[0, rows, :].astype(F32), knslc_ref[...]), cos, sin)
        kw = _rope(_head_rms(kw_ref[0, rows, :].astype(F32), knwin_ref[...]), cos, sin)
        blk = (r0 + _row((step, LANES))) // SEL_LEN
        onehot = jnp.where(_lane((step, LANES)) - HEAD_DIM == blk, MASK_BIG, 0.0)
        for src, dst, fill in ((ks, ks_o, onehot), (kw, kw_o, 0.0)):
            d0, d1 = _split_kv(src, fill)
            dst[0, 0, rows, :] = d0.astype(BF16)
            dst[0, 1, rows, :] = d1.astype(BF16)
        for src, dst in ((vs_ref, vst_o), (vw_ref, vwt_o)):
            vt = src[0, rows, :].astype(F32).T
            for kv in range(D_KV_HEADS):
                dst[0, kv, :, rows] = _with_ones_row(vt[kv * HEAD_DIM:(kv + 1) * HEAD_DIM, :]).astype(BF16)


def _nsa_prep(z, kcr, vcr, cos, sin_s, cosc, sinc, kn_cmp, kn_slc, kn_win, pek, pev, w1k, w2k, w1v, w2v):
    b, s, _ = z.shape
    nseg = s // CMP_STRIDE
    zb = lambda name: pl.BlockSpec((1, s, 128), lambda bi, o=ODD_OFF[name][0] // 128: (bi, 0, o))
    full = lambda shp: pl.BlockSpec(shp, lambda bi: (0,) * len(shp))
    seg = pl.BlockSpec((1, nseg, 2048), lambda bi: (bi, 0, 0))
    g2 = lambda g: jnp.tile(g, 2).reshape(1, 128)
    dup = lambda n: (jax.ShapeDtypeStruct((b, 2, n, 128), BF16), pl.BlockSpec((1, 2, n, 128), lambda bi: (bi, 0, 0, 0)))
    tr = lambda n: (jax.ShapeDtypeStruct((b, 2, VT_ROWS, n), BF16),
                    pl.BlockSpec((1, 2, VT_ROWS, n), lambda bi: (bi, 0, 0, 0)))
    outs = [dup(nseg), tr(nseg), dup(s), tr(s), dup(s), tr(s)]
    return pl.pallas_call(
        _nsa_prep_kernel,
        grid=(b,),
        in_specs=[seg, seg, zb("ksd"), zb("vsd"), zb("kwd"), zb("vwd"),
                  full((s, 128)), full((s, 128)), full((nseg, 128)), full((nseg, 128)),
                  full((1, 128)), full((1, 128)), full((1, 128)), full((2, 2048)), full((2, 2048)),
                  full((2, 2048, 256)), full((256, 128)), full((2, 2048, 256)), full((128, 256))],
        out_specs=[o[1] for o in outs],
        out_shape=[o[0] for o in outs],
        compiler_params=_cparams(("arbitrary",)),
        name="nsa_prep",
    )(kcr, vcr, z, z, z, z, cos, sin_s, cosc, sinc, g2(kn_cmp), g2(kn_slc), g2(kn_win), pek, pev, w1k, w2k, w1v, w2v)


SEL_CHUNK = 512
SEL_PIECE = 512


def _nsa_kernel(bound_ref, *refs):
    small = jnp.maximum(jnp.maximum(bound_ref[0], bound_ref[1]), bound_ref[2]) <= MAX_SAFE_BOUND

    @pl.when(small)
    def _():
        _nsa_body((bound_ref[0], bound_ref[1], bound_ref[2]), *refs)

    @pl.when(jnp.logical_not(small))
    def _():
        _nsa_body((None, None, None), *refs)


def _nsa_body(fixed, q_ref, gd_ref, gate_ref, cos_ref, sin_ref, qn_ref, kc_ref, vct_ref, ks_ref, vst_ref, kw_ref, vwt_ref,
              ovlt_ref, o_ref):
    fix_c, fix_s, fix_w = fixed
    i = pl.program_id(1)
    t = q_ref.shape[1]
    grp = D_HEADS // D_KV_HEADS
    w = grp * t
    kvs = range(D_KV_HEADS)
    q = _rope(_head_rms(q_ref[0].astype(F32), qn_ref[...]), cos_ref[...], sin_ref[...]) * Q_SCALE
    stack = lambda heads: [jnp.concatenate(heads[kv * grp:(kv + 1) * grp], axis=0) for kv in kvs]
    qs = stack(_q_heads(q))
    tq1 = i * t + _lane((1, t))

    nwin = D_WINDOW + t
    win0 = pl.multiple_of(jnp.maximum(i * t + t - nwin, 0), t)
    s_cmp = [_dot_nt(kc_ref[0, kv], qs[kv]) for kv in kvs]
    s_win = [_dot_nt(kw_ref[0, kv, pl.ds(win0, nwin), :], qs[kv]) for kv in kvs]

    ncmp = kc_ref.shape[2] - 1
    nsel = ovlt_ref.shape[0]
    crow = _row((kc_ref.shape[2], t))
    bias_c = _tile4(jnp.where((crow * CMP_STRIDE + (CMP_LEN - 1) <= tq1) & (crow < ncmp), 0.0, NEG_INF))
    row_ok = _tile4(tq1 >= CMP_LEN - 1)
    es = [_exp_cols(s_cmp[kv] + bias_c, fixed=fix_c)[0] for kv in kvs]
    accs = [_dot(vct_ref[0, kv], es[kv]) for kv in kvs]
    invs = [jnp.where(row_ok, 1.0 / accs[kv][HEAD_DIM:HEAD_DIM + 1, :], 0.0) for kv in kvs]
    oc_t = [accs[kv][0:HEAD_DIM, :] * invs[kv] for kv in kvs]
    psums = []
    for kv in kvs:
        p = es[kv].astype(F32) * invs[kv]
        psums.append(p[:, 0:t] + p[:, t:2 * t] + p[:, 2 * t:3 * t] + p[:, 3 * t:4 * t])
    imps = [_dot_split_rhs(ovlt_ref[...], psums[kv]) for kv in kvs]

    rel = tq1 - (win0 + _row((nwin, t)))
    bias_w = _tile4(jnp.where((rel >= 0) & (rel < D_WINDOW), 0.0, NEG_INF))
    ow_t = [_finish(_dot(vwt_ref[0, kv, :, pl.ds(win0, nwin)], _exp_cols(s_win[kv] + bias_w, fixed=fix_w)[0]))
            for kv in kvs]

    jrow = _row((nsel, t))
    jrow_f = jrow.astype(F32)
    cur = (i * t + _lane((nsel, t))) // SEL_LEN
    forced = (jrow == 0) | (jrow == cur)
    fills = []
    for kv in kvs:
        score = jnp.where(forced, POS_INF, jnp.where(jrow <= cur, imps[kv], NEG_INF))
        chosen = jnp.zeros((nsel, t), F32)
        for _ in range(min(SEL_TOPK, nsel)):
            mx = jnp.max(score, axis=0, keepdims=True)
            first = jnp.min(jnp.where(score == mx, jrow_f, 1e9), axis=0, keepdims=True)
            hit = jrow_f == first
            chosen = jnp.where(hit, 1.0, chosen)
            score = jnp.where(hit, -3e38, score)
        frame = jnp.concatenate([jnp.zeros((HEAD_DIM, t), F32), chosen - 1.0,
                                 jnp.zeros((LANES - HEAD_DIM - nsel, t), F32)], axis=0)
        fills.extend([frame.T] * (grp // 2))

    init = (jnp.full((1, w), NEG_INF, F32), jnp.zeros((VT_ROWS, w), F32))
    pc = SEL_PIECE

    def sweep(states, k_ref, vt_ref, queries, offs, biases, n):
        scores = [[_dot_nt(k_ref[0, kv, pl.ds(off, n), :], queries[kv]) for kv in kvs] for off in offs]
        for off, sc, bias in zip(offs, scores, biases):
            states = tuple(_online_update(states[kv], sc[kv] if bias is None else sc[kv] + bias,
                                          vt_ref[0, kv, :, pl.ds(off, n)], fix_s) for kv in kvs)
        return states

    qsel = stack(_q_heads(q, fills))
    ch = SEL_CHUNK
    own = pl.multiple_of(((i * t) // ch) * ch, ch)
    st = sweep((init, init), ks_ref, vst_ref, qsel, [own],
               [_tile4(jnp.where(own + _row((ch, t)) <= tq1, 0.0, NEG_INF))], ch)

    def sel_chunk(c, states):
        base = pl.multiple_of(c * ch, ch)
        return sweep(states, ks_ref, vst_ref, qsel, [base + n * pc for n in range(ch // pc)], [None] * (ch // pc), pc)

    st = lax.fori_loop(0, (i * t) // ch, sel_chunk, st)
    os_t = [_finish(acc) for (_, acc) in st]

    g_t = jax.nn.sigmoid(gate_ref[0].astype(F32)).T
    rows = []
    for h in range(D_HEADS):
        kv, g = divmod(h, grp)
        cols = slice(g * t, (g + 1) * t)
        r = D_BRANCHES * h
        rows.append(g_t[r:r + 1, :] * oc_t[kv][:, cols] + g_t[r + 1:r + 2, :] * os_t[kv][:, cols]
                    + g_t[r + 2:r + 3, :] * ow_t[kv][:, cols])
    o = jnp.concatenate(rows, axis=0).T
    o_ref[0] = (o * _silu(gd_ref[0].astype(F32))).astype(BF16)


def _overlap_t(s):
    ncmp = (s - CMP_LEN) // CMP_STRIDE + 1
    nsel = s // SEL_LEN
    cs = np.arange(ncmp)[None, :] * CMP_STRIDE
    ss = np.arange(nsel)[:, None] * SEL_LEN
    ovl = np.zeros((nsel, s // CMP_STRIDE), np.float32)
    ovl[:, :ncmp] = (cs < ss + SEL_LEN) & (cs + CMP_LEN > ss)
    return ovl


def _nsa(z, cos, sin_s, qn, bounds, kc, vct, ks, vst, kw, vwt, t=256):
    b, s, _ = z.shape
    nseg = s // CMP_STRIDE
    assert nseg == LANES, "compressed-block scores are laid out on one 128-row tile"
    ovlt = _overlap_t(s)
    full = lambda shp: pl.BlockSpec(shp, lambda bi, i: (0,) * len(shp))
    dup = lambda n: pl.BlockSpec((1, 2, n, 128), lambda bi, i: (bi, 0, 0, 0))
    tr = lambda n: pl.BlockSpec((1, 2, VT_ROWS, n), lambda bi, i: (bi, 0, 0, 0))
    return pl.pallas_call(
        _nsa_kernel,
        grid=(b, s // t),
        in_specs=[pl.BlockSpec(memory_space=pltpu.SMEM),
                  pl.BlockSpec((1, t, 512), lambda bi, i: (bi, i, ODD_OFF["qd"][0] // 512)),
                  pl.BlockSpec((1, t, 512), lambda bi, i: (bi, i, ODD_OFF["gd"][0] // 512)),
                  pl.BlockSpec((1, t, 128), lambda bi, i: (bi, i, ODD_OFF["gate"][0] // 128)),
                  pl.BlockSpec((t, 128), lambda bi, i: (i, 0)),
                  pl.BlockSpec((t, 128), lambda bi, i: (i, 0)),
                  full((1, 512)), dup(nseg), tr(nseg), dup(s), tr(s), dup(s), tr(s), full(ovlt.shape)],
        out_specs=pl.BlockSpec((1, t, 512), lambda bi, i: (bi, i, 0)),
        out_shape=jax.ShapeDtypeStruct((b, s, 512), BF16),
        compiler_params=_cparams(("arbitrary", "arbitrary")),
        name="nsa",
    )(bounds, z, z, z, cos, sin_s, jnp.tile(qn, 8).reshape(1, 512), kc, vct, ks, vst, kw, vwt, jnp.asarray(ovlt, BF16))


def _permute_cols(w, order, src):
    cols = []
    for name, width in order:
        o, sw = src[name]
        blk = w[:, o:o + sw]
        if sw < width:
            blk = jnp.pad(blk, ((0, 0), (0, width - sw)))
        cols.append(blk)
    return jnp.concatenate(cols, axis=1).astype(BF16)


def _block_diag_pairs(w):
    z = jnp.zeros((4, LANES, LANES), w.dtype)
    z = z.at[:, 0:64, 0:64].set(w[0::2])
    z = z.at[:, 64:128, 64:128].set(w[1::2])
    return z.astype(BF16)


def _rope_tables(pos):
    half = HEAD_DIM // 2
    inv = ROPE_THETA ** (-jnp.arange(half, dtype=F32) / half)
    ang = pos.astype(F32)[:, None] * inv[None, :]
    cos, sin = jnp.cos(ang), jnp.sin(ang)
    cos_t = jnp.tile(cos, (1, 4))
    sin_t = jnp.tile(jnp.concatenate([-sin, sin], axis=1), (1, 2))
    return cos_t, sin_t


def _expand_compress_w1(w1):
    hdim = w1.shape[1]
    w = w1.reshape(2, CMP_STRIDE, 1, HEAD_DIM, 1, hdim)
    same_head = jnp.eye(2, dtype=w1.dtype).reshape(1, 1, 2, 1, 2, 1)
    return (w * same_head).reshape(2, CMP_STRIDE * 2 * HEAD_DIM, 2 * hdim).astype(BF16)


def _expand_compress_w2(w2):
    hdim, hd = w2.shape
    z = jnp.zeros((2 * hdim, 2 * hd), w2.dtype)
    z = z.at[0:hdim, 0:hd].set(w2)
    z = z.at[hdim:, hd:].set(w2)
    return z.astype(BF16)


def _expand_pe(pe):
    p = pe.reshape(2, CMP_STRIDE, 1, HEAD_DIM)
    return jnp.broadcast_to(p, (2, CMP_STRIDE, 2, HEAD_DIM)).reshape(2, CMP_STRIDE * 2 * HEAD_DIM)


def _even_layer(h, mem, g, mem_g, w_mem_kv, m_qn, m_kn, w_in, w_out, a_qn, a_kn, a_sinks,
                conv_w, conv_b, w_r, b_r, w_i, b_i, lam, cos, sin_s):
    b, s, d = h.shape
    h2 = h.reshape(b * s, d)
    z, ob = _inproj_rglru(h2, g, _permute_cols(w_in, EVEN_ORDER, EVEN_SRC), s, conv_w, conv_b,
                          _block_diag_pairs(w_r), b_r, _block_diag_pairs(w_i), b_i, lam)
    z = z.reshape(b, s, EVEN_COLS)
    oa = _swa(z, cos, sin_s, a_qn, a_kn, a_sinks)
    out = _outproj_mem(h2, oa.reshape(b * s, 512), ob, z.reshape(b * s, EVEN_COLS), EVEN_OFF["qm"][0], EVEN_OFF["gm"][0],
                       s, mem, mem_g, w_mem_kv.astype(BF16), m_qn, m_kn, w_out.astype(BF16))
    return out.reshape(b, s, d)


def _odd_layer(h, mem, g, mem_g, w_mem_kv, m_qn, m_kn, w_in, w_out, c_lb, c_og,
               d_qn, d_kn_cmp, d_kn_slc, d_kn_win, pe_k, pe_v, w1k, w2k, w1v, w2v, cos, sin_s, cosc, sinc):
    b, s, d = h.shape
    h2 = h.reshape(b * s, d)
    z, kcd, vcd, oc = _inproj_hgrn(h2, g, _permute_cols(w_in, ODD_ORDER, ODD_SRC), s, c_lb, c_og)
    z = z.reshape(b, s, ODD_SPLIT[0])
    nseg = s // CMP_STRIDE
    seg = lambda a: a.reshape(b, nseg, CMP_STRIDE * 128)
    kc, vc, ks, vs, kw, vw = _nsa_prep(
        z, seg(kcd), seg(vcd), cos, sin_s, cosc, sinc, d_kn_cmp, d_kn_slc, d_kn_win,
        _expand_pe(pe_k), _expand_pe(pe_v), _expand_compress_w1(w1k), _expand_compress_w2(w2k),
        _expand_compress_w1(w1v), _expand_compress_w2(w2v).T)
    bounds = jnp.stack([_score_bound(d_qn, kn) for kn in (d_kn_cmp, d_kn_slc, d_kn_win)]).astype(F32)
    od = _nsa(z, cos, sin_s, d_qn, bounds, kc, vc, ks, vs, kw, vw)
    out = _outproj_mem(h2, oc, od.reshape(b * s, 512), z.reshape(b * s, ODD_SPLIT[0]), ODD_OFF["qm"][0], ODD_OFF["gm"][0],
                       s, mem, mem_g, w_mem_kv.astype(BF16), m_qn, m_kn, w_out.astype(BF16))
    return out.reshape(b, s, d)


def kernel(x, mem, norm_g, mem_norm_g, mem_w_kv, mem_qn, mem_kn, ev_w_in, ev_w_out, a_qn, a_kn, a_sinks,
           b_conv_w, b_conv_b, b_w_r, b_b_r, b_w_i, b_b_i, b_lambda, od_w_in, od_w_out, c_lb, c_onorm,
           d_qn, d_kn_cmp, d_kn_slc, d_kn_win, d_pe_k, d_pe_v, d_w1k, d_w2k, d_w1v, d_w2v):
    depth = norm_g.shape[0]
    assert depth == 2 and c_lb.shape[0] == 2, "the HGRN2 lower-bound formula in the kernel is written for depth 2"
    s = x.shape[1]
    assert s % 256 == 0 and s >= D_WINDOW
    pos = jnp.arange(s)
    cos, sin_s = _rope_tables(pos)
    nseg = s // CMP_STRIDE
    cmp_end = jnp.minimum(jnp.arange(nseg) * CMP_STRIDE + CMP_LEN - 1, s - 1)
    cosc, sinc = _rope_tables(cmp_end)
    h = _even_layer(x, mem, norm_g[0], mem_norm_g[0], mem_w_kv[0], mem_qn[0], mem_kn[0], ev_w_in[0], ev_w_out[0],
                    a_qn[0], a_kn[0], a_sinks[0], b_conv_w[0], b_conv_b[0], b_w_r[0], b_b_r[0], b_w_i[0], b_b_i[0],
                    b_lambda[0], cos, sin_s)
    h = _odd_layer(h, mem, norm_g[1], mem_norm_g[1], mem_w_kv[1], mem_qn[1], mem_kn[1], od_w_in[0], od_w_out[0],
                   c_lb, c_onorm[0], d_qn[0], d_kn_cmp[0], d_kn_slc[0], d_kn_win[0], d_pe_k[0], d_pe_v[0],
                   d_w1k[0], d_w2k[0], d_w1v[0], d_w2v[0], cos, sin_s, cosc, sinc)
    return h
```

```python
import functools

import numpy as np
import jax
import jax.numpy as jnp
from jax import lax
from jax.experimental import pallas as pl
from jax.experimental.pallas import tpu as pltpu

F32 = jnp.float32
BF16 = jnp.bfloat16

D_MODEL = 1024
N_MEM = 256
HEAD_DIM = 64
ROPE_THETA = 10000.0
EPS = 1e-6
NEG_INF = -1e30
POS_INF = 1e30
MASK_BIG = 1e30
LANES = 128

A_HEADS, A_KV_HEADS, A_WINDOW = 8, 2, 128
B_WIDTH, B_BLOCKS, B_CONV, B_C = 512, 8, 4, 8.0
M_HEADS = 4
C_HEADS, C_HEAD_DIM, C_CHUNK = 4, 128, 64
C_WIDTH = C_HEADS * C_HEAD_DIM
D_HEADS, D_KV_HEADS = 8, 2
CMP_LEN, CMP_STRIDE, CMP_HIDDEN = 32, 16, 128
SEL_LEN, SEL_TOPK = 64, 4
D_WINDOW = 512
D_BRANCHES = 3
SCALE = HEAD_DIM ** -0.5
LOG2E = 1.4426950408889634
Q_SCALE = SCALE * LOG2E

EVEN_ORDER = [("xb", 512), ("gb", 512), ("qa", 512), ("ga", 512), ("qm", 256), ("gm", 256), ("ka", 128), ("va", 128)]
EVEN_FUSED = 1024
EVEN_SRC = {"qa": (0, 512), "ka": (512, 128), "va": (640, 128), "ga": (768, 512), "xb": (1280, 512),
            "gb": (1792, 512), "qm": (2304, 256), "gm": (2560, 256)}
ODD_ORDER = [("qc", 512), ("fc", 512), ("ic", 512), ("gc", 512), ("qd", 512), ("gd", 512), ("qm", 256), ("gm", 256),
             ("ksd", 128), ("vsd", 128), ("kwd", 128), ("vwd", 128), ("gate", 128), ("kcd", 128), ("vcd", 128)]
ODD_FUSED = 2048
ODD_SPLIT = (2176, 128, 128)
ODD_SRC = {"qc": (0, 512), "fc": (512, 512), "ic": (1024, 512), "gc": (1536, 512), "qd": (2048, 512),
           "kcd": (2560, 128), "vcd": (2688, 128), "ksd": (2816, 128), "vsd": (2944, 128), "kwd": (3072, 128),
           "vwd": (3200, 128), "gate": (3328, 24), "gd": (3352, 512), "qm": (3864, 256), "gm": (4120, 256)}

VMEM_LIMIT = 48 * 1024 * 1024


def _offsets(order):
    off, out = 0, {}
    for name, w in order:
        out[name] = (off, w)
        off += w
    return out, off


EVEN_OFF, EVEN_COLS = _offsets(EVEN_ORDER[2:])
ODD_OFF, ODD_COLS = _offsets(ODD_ORDER[4:])


def _cparams(sem):
    return pltpu.CompilerParams(dimension_semantics=sem, vmem_limit_bytes=VMEM_LIMIT)


def _dot(a, b):
    return jnp.dot(a, b, preferred_element_type=F32)


def _dot_nt(a, b):
    return lax.dot_general(a, b, (((1,), (1,)), ((), ())), preferred_element_type=F32)


def _split(x):
    hi = x.astype(BF16)
    lo = (x - hi.astype(F32)).astype(BF16)
    return hi, lo


def _dot_split_rhs(m, x):
    hi, lo = _split(x)
    return _dot(m, hi) + _dot(m, lo)


def _lane(shape):
    return lax.broadcasted_iota(jnp.int32, shape, len(shape) - 1)


def _row(shape):
    return lax.broadcasted_iota(jnp.int32, shape, len(shape) - 2)


def _silu(x):
    return x * jax.nn.sigmoid(x)


def _seg_ones():
    r = lax.broadcasted_iota(jnp.int32, (LANES, LANES), 0) >> 6
    c = lax.broadcasted_iota(jnp.int32, (LANES, LANES), 1) >> 6
    return jnp.where(r == c, 1.0, 0.0).astype(BF16)


def _head_rms(x, gain):
    seg = _seg_ones()
    cols = []
    for c in range(x.shape[1] // LANES):
        xc = x[:, c * LANES:(c + 1) * LANES]
        ms = _dot((xc * xc).astype(BF16), seg) * (1.0 / HEAD_DIM)
        cols.append(xc * lax.rsqrt(ms + EPS))
    y = cols[0] if len(cols) == 1 else jnp.concatenate(cols, axis=1)
    return y * gain


def _rope(x, cos, sin_s):
    first = (_lane((x.shape[0], LANES)) & 63) < 32
    cols = []
    for c in range(x.shape[1] // LANES):
        xc = x[:, c * LANES:(c + 1) * LANES]
        sw = jnp.where(first, pltpu.roll(xc, 96, 1), pltpu.roll(xc, 32, 1))
        cols.append(xc * cos + sw * sin_s)
    return cols[0] if len(cols) == 1 else jnp.concatenate(cols, axis=1)


def _split_kv(k, fill=0.0):
    lo = _lane(k.shape) < 64
    return jnp.where(lo, k, fill), jnp.where(lo, pltpu.roll(k, 64, 1), fill)


def _q_heads(q, fills=None):
    lo = _lane((q.shape[0], LANES)) < 64
    out = []
    for c in range(q.shape[1] // LANES):
        qc = q[:, c * LANES:(c + 1) * LANES]
        fill = 0.0 if fills is None else fills[c]
        out.append(jnp.where(lo, qc, fill).astype(BF16))
        out.append(jnp.where(lo, pltpu.roll(qc, 64, 1), fill).astype(BF16))
    return out


VT_ROWS = HEAD_DIM + 16


def _with_ones_row(vt):
    n = vt.shape[1]
    pad = jnp.where(_row((VT_ROWS - HEAD_DIM, n)) == 0, 1.0, 0.0)
    return jnp.concatenate([vt, pad], axis=0)


MAX_SAFE_BOUND = 60.0


def _score_bound(q_gain, k_gain):
    return 1.02 * HEAD_DIM * SCALE * LOG2E * jnp.max(jnp.abs(q_gain)) * jnp.max(jnp.abs(k_gain))


def _with_fixed_reference(bound, body):
    small = bound <= MAX_SAFE_BOUND

    @pl.when(small)
    def _():
        body(bound)

    @pl.when(jnp.logical_not(small))
    def _():
        body(None)


def _exp_cols(s, extra=None, fixed=None):
    if fixed is not None:
        return jnp.exp2(s - fixed).astype(BF16), fixed
    m = jnp.max(s, axis=0, keepdims=True)
    if extra is not None:
        m = jnp.maximum(m, extra)
    return jnp.exp2(s - m).astype(BF16), m


def _finish(acc, extra_den=None):
    den = acc[HEAD_DIM:HEAD_DIM + 1, :]
    if extra_den is not None:
        den = den + extra_den
    return acc[0:HEAD_DIM, :] * (1.0 / den)


def _online_update(state, s, vt_tile, fixed=None):
    m, acc = state
    if fixed is not None:
        return m, acc + _dot(vt_tile, _exp_cols(s, fixed=fixed)[0])
    p, m_new = _exp_cols(s, m)
    acc = jnp.exp2(m - m_new) * acc + _dot(vt_tile, p)
    return m_new, acc


def _tile4(x):
    return jnp.concatenate([x, x, x, x], axis=1)


def _outproj_mem_kernel(tiles_per_seq, bound_ref, h_ref, a_ref, b_ref, q_ref, gm_ref, mem_ref, mg_ref, wkv_ref,
                        qn_ref, kn_ref, w_ref, o_ref, k_scr, vt_scr, acc_scr):
    @pl.when(pl.program_id(0) % tiles_per_seq == 0)
    def _():
        m = mem_ref[0]
        ms = jnp.mean(m * m, axis=-1, keepdims=True)
        mn = (m * lax.rsqrt(ms + EPS) * mg_ref[...]).astype(BF16)
        kv = _dot(mn, wkv_ref[...])
        km = _head_rms(kv[:, 0:256], kn_ref[...])
        for c in range(2):
            k0, k1 = _split_kv(km[:, c * LANES:(c + 1) * LANES])
            k_scr[2 * c] = k0.astype(BF16)
            k_scr[2 * c + 1] = k1.astype(BF16)
        vt = kv[:, 256:512].T
        for h in range(M_HEADS):
            vt_scr[h] = _with_ones_row(vt[h * HEAD_DIM:(h + 1) * HEAD_DIM, :]).astype(BF16)

    def body(fixed):
        a, b = a_ref[...], b_ref[...]
        pending = list(range(0, D_MODEL, 256))

        def emit():
            if pending:
                c = pending.pop(0)
                acc_scr[:, c:c + 256] = (h_ref[:, c:c + 256] + _dot(a, w_ref[0:512, c:c + 256])
                                         + _dot(b, w_ref[512:1024, c:c + 256]))

        q = _head_rms(q_ref[...].astype(F32), qn_ref[...]) * Q_SCALE
        scores = [_dot_nt(k_scr[h], qh) for h, qh in enumerate(_q_heads(q))]
        rows = []
        for h, sc in enumerate(scores):
            emit()
            rows.append(_finish(_dot(vt_scr[h], _exp_cols(sc, fixed=fixed)[0])))
        om = (jnp.concatenate(rows, axis=0).T * _silu(gm_ref[...].astype(F32))).astype(BF16)
        while pending:
            emit()
        for c in range(0, D_MODEL, 256):
            o_ref[:, c:c + 256] = acc_scr[:, c:c + 256] + _dot(om, w_ref[1024:1280, c:c + 256])

    _with_fixed_reference(bound_ref[0], body)


def _outproj_mem(h2d, oa, ob, z2d, q_off, g_off, seq, mem, mem_g, wkv_bf16, qn, kn, w_bf16, tm=1024):
    n, d = h2d.shape
    full = lambda shp: pl.BlockSpec(shp, lambda i: (0,) * len(shp))
    tps = seq // tm
    return pl.pallas_call(
        functools.partial(_outproj_mem_kernel, tps),
        grid=(n // tm,),
        in_specs=[pl.BlockSpec(memory_space=pltpu.SMEM),
                  pl.BlockSpec((tm, d), lambda i: (i, 0)),
                  pl.BlockSpec((tm, 512), lambda i: (i, 0)),
                  pl.BlockSpec((tm, 512), lambda i: (i, 0)),
                  pl.BlockSpec((tm, 256), lambda i: (i, q_off // 256)),
                  pl.BlockSpec((tm, 256), lambda i: (i, g_off // 256)),
                  pl.BlockSpec((1, N_MEM, D_MODEL), lambda i: (i // tps, 0, 0)),
                  full((1, D_MODEL)), full((D_MODEL, 512)), full((1, 256)), full((1, 256)), full((1280, d))],
        out_specs=pl.BlockSpec((tm, d), lambda i: (i, 0)),
        out_shape=jax.ShapeDtypeStruct((n, d), F32),
        scratch_shapes=[pltpu.VMEM((M_HEADS, N_MEM, LANES), BF16), pltpu.VMEM((M_HEADS, VT_ROWS, N_MEM), BF16),
                        pltpu.VMEM((tm, d), F32)],
        compiler_params=_cparams(("arbitrary",)),
        name="outproj_mem",
    )(_score_bound(qn, kn).reshape(1).astype(F32), h2d, oa, ob, z2d, z2d, mem, mem_g.reshape(1, D_MODEL), wkv_bf16,
      jnp.tile(qn, 4).reshape(1, 256), jnp.tile(kn, 4).reshape(1, 256), w_bf16)


def _swa_kernel(bound_ref, *refs):
    _with_fixed_reference(bound_ref[0], functools.partial(_swa_body, refs))


def _swa_body(refs, fixed):
    (q_ref, g_ref, kc_ref, kp_ref, vc_ref, vp_ref, cosc_ref, sinc_ref, cosp_ref, sinp_ref,
     qn_ref, kn_ref, sink_ref, o_ref) = refs
    i = pl.program_id(1)
    t = q_ref.shape[1]
    tp = kp_ref.shape[1]
    grp = A_HEADS // A_KV_HEADS
    q = _rope(_head_rms(q_ref[0].astype(F32), qn_ref[...]), cosc_ref[...], sinc_ref[...]) * Q_SCALE
    qh = _q_heads(q)
    kc = _rope(_head_rms(kc_ref[0].astype(F32), kn_ref[...]), cosc_ref[...], sinc_ref[...])
    kp = _rope(_head_rms(kp_ref[0].astype(F32), kn_ref[...]), cosp_ref[...], sinp_ref[...])
    ks = [x.astype(BF16) for x in _split_kv(jnp.concatenate([kp, kc], axis=0))]
    vt = jnp.concatenate([vp_ref[0].astype(F32).T, vc_ref[0].astype(F32).T], axis=1)
    ts = i * t - tp + _row((tp + t, t))
    rel = i * t + _lane((tp + t, t)) - ts
    bias = _tile4(jnp.where((rel >= 0) & (rel < A_WINDOW) & (ts >= 0), 0.0, NEG_INF))
    scores = [_dot_nt(ks[kv], jnp.concatenate(qh[kv * grp:(kv + 1) * grp], axis=0)) for kv in range(A_KV_HEADS)]
    rows = []
    for kv in range(A_KV_HEADS):
        sink = jnp.concatenate([jnp.broadcast_to(sink_ref[kv * grp + g:kv * grp + g + 1, 0:1], (1, t))
                                for g in range(grp)], axis=1) * LOG2E
        p, m = _exp_cols(scores[kv] + bias, sink, fixed)
        vt_kv = _with_ones_row(vt[kv * HEAD_DIM:(kv + 1) * HEAD_DIM, :]).astype(BF16)
        o_t = _finish(_dot(vt_kv, p), jnp.exp2(sink - m))
        rows.extend(o_t[:, g * t:(g + 1) * t] for g in range(grp))
    o = jnp.concatenate(rows, axis=0).T
    o_ref[0] = (o * _silu(g_ref[0].astype(F32))).astype(BF16)


def _swa(z, cos, sin_s, qn, kn, sinks, t=256):
    b, s, _ = z.shape
    tp = A_WINDOW
    qo, go = EVEN_OFF["qa"][0] // 512, EVEN_OFF["ga"][0] // 512
    ko, vo = EVEN_OFF["ka"][0] // 128, EVEN_OFF["va"][0] // 128
    prev = lambda i: jnp.maximum(i * (t // tp) - 1, 0)
    bound = jnp.maximum(_score_bound(qn, kn), jnp.max(sinks) * LOG2E).reshape(1).astype(F32)
    return pl.pallas_call(
        _swa_kernel,
        grid=(b, s // t),
        in_specs=[pl.BlockSpec(memory_space=pltpu.SMEM),
                  pl.BlockSpec((1, t, 512), lambda bi, i: (bi, i, qo)),
                  pl.BlockSpec((1, t, 512), lambda bi, i: (bi, i, go)),
                  pl.BlockSpec((1, t, 128), lambda bi, i: (bi, i, ko)),
                  pl.BlockSpec((1, tp, 128), lambda bi, i: (bi, prev(i), ko)),
                  pl.BlockSpec((1, t, 128), lambda bi, i: (bi, i, vo)),
                  pl.BlockSpec((1, tp, 128), lambda bi, i: (bi, prev(i), vo)),
                  pl.BlockSpec((t, 128), lambda bi, i: (i, 0)),
                  pl.BlockSpec((t, 128), lambda bi, i: (i, 0)),
                  pl.BlockSpec((tp, 128), lambda bi, i: (prev(i), 0)),
                  pl.BlockSpec((tp, 128), lambda bi, i: (prev(i), 0)),
                  pl.BlockSpec((1, 512), lambda bi, i: (0, 0)),
                  pl.BlockSpec((1, 128), lambda bi, i: (0, 0)),
                  pl.BlockSpec((A_HEADS, 128), lambda bi, i: (0, 0))],
        out_specs=pl.BlockSpec((1, t, 512), lambda bi, i: (bi, i, 0)),
        out_shape=jax.ShapeDtypeStruct((b, s, 512), BF16),
        compiler_params=_cparams(("arbitrary", "arbitrary")),
        name="swa",
    )(bound, z, z, z, z, z, z, cos, sin_s, cos, sin_s, jnp.tile(qn, 8).reshape(1, 512), jnp.tile(kn, 2).reshape(1, 128),
      jnp.broadcast_to(sinks.reshape(A_HEADS, 1), (A_HEADS, 128)))


def _rglru_half(cols, xb, gb, emit, cw_ref, cb_ref, wr_ref, br_ref, wi_ref, bi_ref, lam_ref, xbuf, hcar, a_scr, u_scr, h_scr):
    t, wdt = xb.shape
    xbuf[8:t + 8, cols] = xb
    xc = cb_ref[:, cols] + cw_ref[0:1, cols] * xbuf[5:t + 5, cols]
    for j in range(1, B_CONV):
        xc = xc + cw_ref[j:j + 1, cols] * xbuf[5 + j:t + 5 + j, cols]
    xbuf[0:8, cols] = xbuf[t:t + 8, cols]
    emit()

    r_cols, i_cols = [], []
    for c in range(wdt // LANES):
        blk = cols.start // LANES + c
        xcc = xc[:, c * LANES:(c + 1) * LANES].astype(BF16)
        r_cols.append(_dot(xcc, wr_ref[blk]))
        i_cols.append(_dot(xcc, wi_ref[blk]))
    emit()
    r = jax.nn.sigmoid(jnp.concatenate(r_cols, axis=1) + br_ref[:, cols])
    ig = jax.nn.sigmoid(jnp.concatenate(i_cols, axis=1) + bi_ref[:, cols])
    nl = -lam_ref[:, cols]
    softplus = jnp.maximum(nl, 0.0) + jnp.log(1.0 + jnp.exp(-jnp.abs(nl)))
    log_a = -B_C * r * softplus
    a = jnp.exp(log_a)
    om = 1.0 - a * a
    u = om * lax.rsqrt(jnp.maximum(om, 1e-30)) * (ig * xc)
    emit()

    a = a.reshape(t // 8, 8, wdt)
    u = u.reshape(t // 8, 8, wdt)
    r8 = lax.broadcasted_iota(jnp.int32, a.shape, 1)
    for d in (1, 2, 4):
        a_sh = pltpu.roll(a, d, 1)
        u_sh = pltpu.roll(u, d, 1)
        m = r8 >= d
        u = jnp.where(m, a * u_sh + u, u)
        a = jnp.where(m, a * a_sh, a)
    a_scr[:, cols] = a.reshape(t, wdt)
    u_scr[:, cols] = u.reshape(t, wdt)
    emit()

    def body(j, h):
        off = pl.multiple_of(j * 8, 8)
        hh = a_scr[pl.ds(off, 8), cols] * h + u_scr[pl.ds(off, 8), cols]
        h_scr[pl.ds(off, 8), cols] = hh
        return hh[7:8, :]

    h_last = lax.fori_loop(0, t // 8, body, hcar[0:1, cols], unroll=True)
    hcar[0:1, cols] = h_last
    emit()
    return h_scr[:, cols] * _silu(gb)


def _inproj_rglru_kernel(tiles_per_seq, x_ref, g_ref, w_ref, cw_ref, cb_ref, wr_ref, br_ref, wi_ref, bi_ref, lam_ref,
                         z_ref, ob_ref, xbuf, hcar, a_scr, u_scr, h_scr):
    @pl.when(pl.program_id(0) % tiles_per_seq == 0)
    def _():
        xbuf[0:8, :] = jnp.zeros((8, B_WIDTH), F32)
        hcar[...] = jnp.zeros((8, B_WIDTH), F32)

    x = x_ref[...]
    ms = jnp.mean(x * x, axis=-1, keepdims=True)
    xn = (x * lax.rsqrt(ms + EPS) * g_ref[...]).astype(BF16)
    half = B_WIDTH // 2
    proj = {}

    def project(name, hp):
        base = (B_WIDTH if name == "gb" else 0) + hp * half
        proj[name, hp] = _dot(xn, w_ref[:, base:base + half])

    def store_z(c):
        z_ref[:, c:c + 256] = _dot(xn, w_ref[:, EVEN_FUSED + c:EVEN_FUSED + c + 256]).astype(BF16)

    pending = [functools.partial(store_z, c) for c in range(0, z_ref.shape[-1], 256)]
    pending.insert(1, functools.partial(project, "xb", 1))
    pending.insert(3, functools.partial(project, "gb", 1))

    def emit():
        if pending:
            pending.pop(0)()

    project("xb", 0)
    project("gb", 0)
    for hp in range(2):
        cols = slice(hp * half, (hp + 1) * half)
        ob_ref[:, cols] = _rglru_half(cols, proj["xb", hp], proj["gb", hp], emit, cw_ref, cb_ref, wr_ref, br_ref,
                                      wi_ref, bi_ref, lam_ref, xbuf, hcar, a_scr, u_scr, h_scr).astype(BF16)
    while pending:
        emit()


def _inproj_rglru(x2d, gain, w_bf16, seq, conv_w, conv_b, wr_bd, b_r, wi_bd, b_i, lam, tm=1024):
    n, d = x2d.shape
    nc = w_bf16.shape[1]
    row = lambda v: v.reshape(1, B_WIDTH)
    full = lambda shp: pl.BlockSpec(shp, lambda i: (0,) * len(shp))
    return pl.pallas_call(
        functools.partial(_inproj_rglru_kernel, seq // tm),
        grid=(n // tm,),
        in_specs=[pl.BlockSpec((tm, d), lambda i: (i, 0)), full((1, d)), full((d, nc)),
                  full((B_CONV, B_WIDTH)), full((1, B_WIDTH)), full((4, LANES, LANES)), full((1, B_WIDTH)),
                  full((4, LANES, LANES)), full((1, B_WIDTH)), full((1, B_WIDTH))],
        out_specs=[pl.BlockSpec((tm, nc - EVEN_FUSED), lambda i: (i, 0)), pl.BlockSpec((tm, B_WIDTH), lambda i: (i, 0))],
        out_shape=[jax.ShapeDtypeStruct((n, nc - EVEN_FUSED), BF16), jax.ShapeDtypeStruct((n, B_WIDTH), BF16)],
        scratch_shapes=[pltpu.VMEM((tm + 8, B_WIDTH), F32), pltpu.VMEM((8, B_WIDTH), F32),
                        pltpu.VMEM((tm, B_WIDTH), F32), pltpu.VMEM((tm, B_WIDTH), F32), pltpu.VMEM((tm, B_WIDTH), F32)],
        compiler_params=_cparams(("arbitrary",)),
        name="inproj_rglru",
    )(x2d, gain.reshape(1, d), w_bf16, conv_w, row(conv_b), wr_bd, row(b_r), wi_bd, row(b_i), row(lam))


def _hgrn_consts():
    c = C_CHUNK
    t = np.arange(c)[:, None]
    s = np.arange(c)[None, :]
    masks = []
    hs = c // 2
    while hs >= 1:
        same = (t // (2 * hs)) == (s // (2 * hs))
        masks.append(same & ((t // hs) % 2 == 1) & ((s // hs) % 2 == 0))
        hs //= 2
    pairs = [np.concatenate(masks[n:n + 2], axis=1) for n in range(0, len(masks), 2)]
    return (s <= t).astype(np.float32), np.stack(pairs).astype(np.float32)


def _split_points(b, level):
    c, wdt = b.shape
    hs = c >> (level + 1)
    blk = 2 * hs
    if blk >= 8:
        return jnp.concatenate([jnp.broadcast_to(b[m * blk + hs - 1:m * blk + hs, :], (blk, wdt))
                                for m in range(c // blk)], axis=0)
    b3 = b.reshape(c // 8, 8, wdt)
    sub = lax.broadcasted_iota(jnp.int32, b3.shape, 1)
    if hs == 2:
        out = jnp.where(sub < 4, jnp.broadcast_to(b3[:, 1:2, :], b3.shape), jnp.broadcast_to(b3[:, 5:6, :], b3.shape))
    else:
        out = jnp.where((sub & 1) == 0, b3, pltpu.roll(b3, 1, 1))
    return out.reshape(c, wdt)


def _hgrn_chunk(lb, q_in, f_in, v_in, g_in, og_ref, mst_ref, msk_ref, st_scr, emit, emit2):
    c = C_CHUNK
    npair = msk_ref.shape[0]
    zeros = jnp.zeros((c, LANES), BF16)
    f = lb + (1.0 - lb) * jax.nn.sigmoid(f_in)
    g = jnp.log2(f)
    g_hi, g_lo = _split(g)
    b_all = _dot(mst_ref[...], g_hi) + _dot(mst_ref[...], g_lo)
    emit()
    kk_all = 1.0 - f
    qf_all = _silu(q_in)
    v_all = v_in
    kk_bf, qf_bf = kk_all.astype(BF16), qf_all.astype(BF16)
    heads = range(C_HEADS)
    cols = [slice(h * LANES, (h + 1) * LANES) for h in heads]
    sts = [st_scr[h] for h in heads]
    os = [_dot_nt((qf_all[:, cols[h]] * jnp.exp2(b_all[:, cols[h]])).astype(BF16), sts[h].astype(BF16)) for h in heads]
    att = [[] for _ in heads]
    bms = [_split_points(b_all, l) for l in range(2 * npair)]
    for n in range(npair):
        lhs, rhs = [], []
        for h in heads:
            b = b_all[:, cols[h]]
            qt, kt = [], []
            for l in (2 * n, 2 * n + 1):
                bm = bms[l][:, cols[h]]
                qt.append(qf_bf[:, cols[h]] * jnp.exp2(jnp.minimum(b - bm, 1.0)).astype(BF16))
                kt.append(kk_bf[:, cols[h]] * jnp.exp2(jnp.minimum(bm - b, 1.0)).astype(BF16))
            lhs.append(jnp.concatenate(qt, axis=1))
            rhs.append(jnp.concatenate([jnp.concatenate([kt[0], zeros], axis=1),
                                        jnp.concatenate([zeros, kt[1]], axis=1)], axis=0))
        prods = [_dot_nt(lhs[h], rhs[h]) for h in heads]
        if n == 1:
            emit2()
        for h in heads:
            att[h].append(jnp.where(msk_ref[n] > 0.5, prods[h], 0.0).astype(BF16))
    vbs = [v_all[:, cols[h]].astype(BF16) for h in heads]
    intra = [_dot(jnp.concatenate(att[h], axis=1), jnp.concatenate([vbs[h]] * (2 * npair), axis=0)) for h in heads]
    b_last = b_all[c - 1:c, :]
    kd = (kk_all * jnp.exp2(b_last - b_all)).astype(BF16)
    upd = [_dot(v_all[:, cols[h]].T.astype(BF16), kd[:, cols[h]]) for h in heads]
    decay = jnp.exp2(b_last)
    diag = qf_all * kk_all
    outs = []
    for h in heads:
        st_scr[h] = sts[h] * decay[:, cols[h]] + upd[h]
        o = os[h] + intra[h] + jnp.sum(diag[:, cols[h]], axis=-1, keepdims=True) * v_all[:, cols[h]]
        ms = jnp.mean(o * o, axis=-1, keepdims=True)
        outs.append(o * lax.rsqrt(ms + EPS) * og_ref[...])
    return jnp.concatenate(outs, axis=1) * _silu(g_in)


def _inproj_hgrn_kernel(tiles_per_seq, x_ref, g_ref, w_ref, lb_ref, og_ref, mst_ref, msk_ref,
                        z_ref, kc_ref, vc_ref, oc_ref, st_scr):
    @pl.when(pl.program_id(0) % tiles_per_seq == 0)
    def _():
        st_scr[...] = jnp.zeros(st_scr.shape, F32)

    x = x_ref[...]
    ms = jnp.mean(x * x, axis=-1, keepdims=True)
    xn = (x * lax.rsqrt(ms + EPS) * g_ref[...]).astype(BF16)
    rows_half = x.shape[0] // 2
    own = {}

    def project_own(k, half):
        r = slice(half * rows_half, (half + 1) * rows_half)
        own[k, half] = _dot(xn[r, :], w_ref[:, k * C_WIDTH:(k + 1) * C_WIDTH])

    o_refs = (z_ref, kc_ref, vc_ref)
    starts = np.cumsum([0] + [o.shape[-1] for o in o_refs])

    def project_rest(c):
        w = min(256, int(starts[-1]) - c)
        r = _dot(xn, w_ref[:, ODD_FUSED + c:ODD_FUSED + c + w]).astype(BF16)
        for o_ref, lo, hi in zip(o_refs, starts[:-1], starts[1:]):
            a, b = max(c, int(lo)), min(c + w, int(hi))
            if a < b:
                o_ref[:, a - int(lo):b - int(lo)] = r[:, a - c:b - c]

    pending = [functools.partial(project_own, k, 1) for k in range(4)]
    pending += [functools.partial(project_rest, c) for c in range(0, int(starts[-1]), 256)]

    def emit():
        if pending:
            pending.pop(0)()

    for k in range(4):
        project_own(k, 0)

    p = lb_ref[...]
    pm = jnp.maximum(p[0:1, :], p[1:2, :])
    e0, e1 = jnp.exp(p[0:1, :] - pm), jnp.exp(p[1:2, :] - pm)
    lb = e1 / (e0 + e1)
    nchunk = x.shape[0] // C_CHUNK
    for ch in range(nchunk):
        half, sub = divmod(ch, nchunk // 2)
        rows = slice(ch * C_CHUNK, (ch + 1) * C_CHUNK)
        part = slice(sub * C_CHUNK, (sub + 1) * C_CHUNK)
        q_in, f_in, v_in, g_in = (own[k, half][part, :] for k in range(4))
        second = emit if ch < 6 else (lambda: None)
        oc_ref[rows, :] = _hgrn_chunk(lb, q_in, f_in, v_in, g_in, og_ref, mst_ref, msk_ref, st_scr, emit,
                                      second).astype(BF16)
    while pending:
        emit()


def _inproj_hgrn(x2d, gain, w_bf16, seq, c_lb, c_og, tm=512):
    n, d = x2d.shape
    nc = w_bf16.shape[1]
    assert ODD_FUSED + sum(ODD_SPLIT) == nc
    mst, msk = _hgrn_consts()
    full = lambda shp: pl.BlockSpec(shp, lambda i: (0,) * len(shp))
    widths = ODD_SPLIT + (C_WIDTH,)
    return pl.pallas_call(
        functools.partial(_inproj_hgrn_kernel, seq // tm),
        grid=(n // tm,),
        in_specs=[pl.BlockSpec((tm, d), lambda i: (i, 0)), full((1, d)), full((d, nc)),
                  full(c_lb.shape), full((1, C_HEAD_DIM)), full(mst.shape), full(msk.shape)],
        out_specs=[pl.BlockSpec((tm, w), lambda i: (i, 0)) for w in widths],
        out_shape=[jax.ShapeDtypeStruct((n, w), BF16) for w in widths],
        scratch_shapes=[pltpu.VMEM((C_HEADS, C_HEAD_DIM, C_HEAD_DIM), F32)],
        compiler_params=_cparams(("arbitrary",)),
        name="inproj_hgrn",
    )(x2d, gain.reshape(1, d), w_bf16, c_lb, c_og.reshape(1, C_HEAD_DIM), jnp.asarray(mst, BF16), jnp.asarray(msk, F32))


def _nsa_prep_kernel(kcr_ref, vcr_ref, ks_ref, vs_ref, kw_ref, vw_ref, cos_ref, sin_ref, cosc_ref, sinc_ref,
                     kncmp_ref, knslc_ref, knwin_ref, pek_ref, pev_ref, w1k_ref, w2k_ref, w1v_ref, w2vt_ref,
                     kc_o, vct_o, ks_o, vst_o, kw_o, vwt_o):
    def hidden(xr_ref, pe_ref, w1_ref):
        xr = xr_ref[0].astype(F32)
        top = _dot((xr + pe_ref[0:1, :]).astype(BF16), w1_ref[0])
        bot = _dot((xr + pe_ref[1:2, :]).astype(BF16), w1_ref[1])
        pre = top + pltpu.roll(bot, bot.shape[0] - 1, 0)
        return _silu(pre).astype(BF16)

    kc = _dot(hidden(kcr_ref, pek_ref, w1k_ref), w2k_ref[...])
    kc = _rope(_head_rms(kc, kncmp_ref[...]), cosc_ref[...], sinc_ref[...])
    d0, d1 = _split_kv(kc)
    kc_o[0, 0] = d0.astype(BF16)
    kc_o[0, 1] = d1.astype(BF16)
    vct = _dot_nt(w2vt_ref[...], hidden(vcr_ref, pev_ref, w1v_ref))
    for kv in range(D_KV_HEADS):
        vct_o[0, kv] = _with_ones_row(vct[kv * HEAD_DIM:(kv + 1) * HEAD_DIM, :]).astype(BF16)

    s = ks_ref.shape[1]
    step = 256
    for r0 in range(0, s, step):
        rows = slice(r0, r0 + step)
        cos, sin = cos_ref[rows, :], sin_ref[rows, :]
        ks = _rope(_head_rms(ks_ref[0, rows, :].astype(F32), knslc_ref[...]), cos, sin)
        kw = _rope(_head_rms(kw_ref[0, rows, :].astype(F32), knwin_ref[...]), cos, sin)
        blk = (r0 + _row((step, LANES))) // SEL_LEN
        onehot = jnp.where(_lane((step, LANES)) - HEAD_DIM == blk, MASK_BIG, 0.0)
        for src, dst, fill in ((ks, ks_o, onehot), (kw, kw_o, 0.0)):
            d0, d1 = _split_kv(src, fill)
            dst[0, 0, rows, :] = d0.astype(BF16)
            dst[0, 1, rows, :] = d1.astype(BF16)
        for src, dst in ((vs_ref, vst_o), (vw_ref, vwt_o)):
            vt = src[0, rows, :].astype(F32).T
            for kv in range(D_KV_HEADS):
                dst[0, kv, :, rows] = _with_ones_row(vt[kv * HEAD_DIM:(kv + 1) * HEAD_DIM, :]).astype(BF16)


def _nsa_prep(z, kcr, vcr, cos, sin_s, cosc, sinc, kn_cmp, kn_slc, kn_win, pek, pev, w1k, w2k, w1v, w2v):
    b, s, _ = z.shape
    nseg = s // CMP_STRIDE
    zb = lambda name: pl.BlockSpec((1, s, 128), lambda bi, o=ODD_OFF[name][0] // 128: (bi, 0, o))
    full = lambda shp: pl.BlockSpec(shp, lambda bi: (0,) * len(shp))
    seg = pl.BlockSpec((1, nseg, 2048), lambda bi: (bi, 0, 0))
    g2 = lambda g: jnp.tile(g, 2).reshape(1, 128)
    dup = lambda n: (jax.ShapeDtypeStruct((b, 2, n, 128), BF16), pl.BlockSpec((1, 2, n, 128), lambda bi: (bi, 0, 0, 0)))
    tr = lambda n: (jax.ShapeDtypeStruct((b, 2, VT_ROWS, n), BF16),
                    pl.BlockSpec((1, 2, VT_ROWS, n), lambda bi: (bi, 0, 0, 0)))
    outs = [dup(nseg), tr(nseg), dup(s), tr(s), dup(s), tr(s)]
    return pl.pallas_call(
        _nsa_prep_kernel,
        grid=(b,),
        in_specs=[seg, seg, zb("ksd"), zb("vsd"), zb("kwd"), zb("vwd"),
                  full((s, 128)), full((s, 128)), full((nseg, 128)), full((nseg, 128)),
                  full((1, 128)), full((1, 128)), full((1, 128)), full((2, 2048)), full((2, 2048)),
                  full((2, 2048, 256)), full((256, 128)), full((2, 2048, 256)), full((128, 256))],
        out_specs=[o[1] for o in outs],
        out_shape=[o[0] for o in outs],
        compiler_params=_cparams(("arbitrary",)),
        name="nsa_prep",
    )(kcr, vcr, z, z, z, z, cos, sin_s, cosc, sinc, g2(kn_cmp), g2(kn_slc), g2(kn_win), pek, pev, w1k, w2k, w1v, w2v)


SEL_CHUNK = 512
SEL_PIECE = 512


def _nsa_kernel(bound_ref, *refs):
    small = jnp.maximum(jnp.maximum(bound_ref[0], bound_ref[1]), bound_ref[2]) <= MAX_SAFE_BOUND

    @pl.when(small)
    def _():
        _nsa_body((bound_ref[0], bound_ref[1], bound_ref[2]), *refs)

    @pl.when(jnp.logical_not(small))
    def _():
        _nsa_body((None, None, None), *refs)


def _nsa_body(fixed, q_ref, gd_ref, gate_ref, cos_ref, sin_ref, qn_ref, kc_ref, vct_ref, ks_ref, vst_ref, kw_ref, vwt_ref,
              ovlt_ref, o_ref):
    fix_c, fix_s, fix_w = fixed
    i = pl.program_id(1)
    t = q_ref.shape[1]
    grp = D_HEADS // D_KV_HEADS
    w = grp * t
    kvs = range(D_KV_HEADS)
    q = _rope(_head_rms(q_ref[0].astype(F32), qn_ref[...]), cos_ref[...], sin_ref[...]) * Q_SCALE
    stack = lambda heads: [jnp.concatenate(heads[kv * grp:(kv + 1) * grp], axis=0) for kv in kvs]
    qs = stack(_q_heads(q))
    tq1 = i * t + _lane((1, t))

    nwin = D_WINDOW + t
    win0 = pl.multiple_of(jnp.maximum(i * t + t - nwin, 0), t)
    s_cmp = [_dot_nt(kc_ref[0, kv], qs[kv]) for kv in kvs]
    s_win = [_dot_nt(kw_ref[0, kv, pl.ds(win0, nwin), :], qs[kv]) for kv in kvs]

    ncmp = kc_ref.shape[2] - 1
    nsel = ovlt_ref.shape[0]
    crow = _row((kc_ref.shape[2], t))
    bias_c = _tile4(jnp.where((crow * CMP_STRIDE + (CMP_LEN - 1) <= tq1) & (crow < ncmp), 0.0, NEG_INF))
    row_ok = _tile4(tq1 >= CMP_LEN - 1)
    es = [_exp_cols(s_cmp[kv] + bias_c, fixed=fix_c)[0] for kv in kvs]
    accs = [_dot(vct_ref[0, kv], es[kv]) for kv in kvs]
    invs = [jnp.where(row_ok, 1.0 / accs[kv][HEAD_DIM:HEAD_DIM + 1, :], 0.0) for kv in kvs]
    oc_t = [accs[kv][0:HEAD_DIM, :] * invs[kv] for kv in kvs]
    psums = []
    for kv in kvs:
        p = es[kv].astype(F32) * invs[kv]
        psums.append(p[:, 0:t] + p[:, t:2 * t] + p[:, 2 * t:3 * t] + p[:, 3 * t:4 * t])
    imps = [_dot_split_rhs(ovlt_ref[...], psums[kv]) for kv in kvs]

    rel = tq1 - (win0 + _row((nwin, t)))
    bias_w = _tile4(jnp.where((rel >= 0) & (rel < D_WINDOW), 0.0, NEG_INF))
    ow_t = [_finish(_dot(vwt_ref[0, kv, :, pl.ds(win0, nwin)], _exp_cols(s_win[kv] + bias_w, fixed=fix_w)[0]))
            for kv in kvs]

    jrow = _row((nsel, t))
    jrow_f = jrow.astype(F32)
    cur = (i * t + _lane((nsel, t))) // SEL_LEN
    forced = (jrow == 0) | (jrow == cur)
    fills = []
    for kv in kvs:
        score = jnp.where(forced, POS_INF, jnp.where(jrow <= cur, imps[kv], NEG_INF))
        chosen = jnp.zeros((nsel, t), F32)
        for _ in range(min(SEL_TOPK, nsel)):
            mx = jnp.max(score, axis=0, keepdims=True)
            first = jnp.min(jnp.where(score == mx, jrow_f, 1e9), axis=0, keepdims=True)
            hit = jrow_f == first
            chosen = jnp.where(hit, 1.0, chosen)
            score = jnp.where(hit, -3e38, score)
        frame = jnp.concatenate([jnp.zeros((HEAD_DIM, t), F32), chosen - 1.0,
                                 jnp.zeros((LANES - HEAD_DIM - nsel, t), F32)], axis=0)
        fills.extend([frame.T] * (grp // 2))

    init = (jnp.full((1, w), NEG_INF, F32), jnp.zeros((VT_ROWS, w), F32))
    pc = SEL_PIECE

    def sweep(states, k_ref, vt_ref, queries, offs, biases, n):
        scores = [[_dot_nt(k_ref[0, kv, pl.ds(off, n), :], queries[kv]) for kv in kvs] for off in offs]
        for off, sc, bias in zip(offs, scores, biases):
            states = tuple(_online_update(states[kv], sc[kv] if bias is None else sc[kv] + bias,
                                          vt_ref[0, kv, :, pl.ds(off, n)], fix_s) for kv in kvs)
        return states

    qsel = stack(_q_heads(q, fills))
    ch = SEL_CHUNK
    own = pl.multiple_of(((i * t) // ch) * ch, ch)
    st = sweep((init, init), ks_ref, vst_ref, qsel, [own],
               [_tile4(jnp.where(own + _row((ch, t)) <= tq1, 0.0, NEG_INF))], ch)

    def sel_chunk(c, states):
        base = pl.multiple_of(c * ch, ch)
        return sweep(states, ks_ref, vst_ref, qsel, [base + n * pc for n in range(ch // pc)], [None] * (ch // pc), pc)

    st = lax.fori_loop(0, (i * t) // ch, sel_chunk, st)
    os_t = [_finish(acc) for (_, acc) in st]

    g_t = jax.nn.sigmoid(gate_ref[0].astype(F32)).T
    rows = []
    for h in range(D_HEADS):
        kv, g = divmod(h, grp)
        cols = slice(g * t, (g + 1) * t)
        r = D_BRANCHES * h
        rows.append(g_t[r:r + 1, :] * oc_t[kv][:, cols] + g_t[r + 1:r + 2, :] * os_t[kv][:, cols]
                    + g_t[r + 2:r + 3, :] * ow_t[kv][:, cols])
    o = jnp.concatenate(rows, axis=0).T
    o_ref[0] = (o * _silu(gd_ref[0].astype(F32))).astype(BF16)


def _overlap_t(s):
    ncmp = (s - CMP_LEN) // CMP_STRIDE + 1
    nsel = s // SEL_LEN
    cs = np.arange(ncmp)[None, :] * CMP_STRIDE
    ss = np.arange(nsel)[:, None] * SEL_LEN
    ovl = np.zeros((nsel, s // CMP_STRIDE), np.float32)
    ovl[:, :ncmp] = (cs < ss + SEL_LEN) & (cs + CMP_LEN > ss)
    return ovl


def _nsa(z, cos, sin_s, qn, bounds, kc, vct, ks, vst, kw, vwt, t=256):
    b, s, _ = z.shape
    nseg = s // CMP_STRIDE
    assert nseg == LANES, "compressed-block scores are laid out on one 128-row tile"
    ovlt = _overlap_t(s)
    full = lambda shp: pl.BlockSpec(shp, lambda bi, i: (0,) * len(shp))
    dup = lambda n: pl.BlockSpec((1, 2, n, 128), lambda bi, i: (bi, 0, 0, 0))
    tr = lambda n: pl.BlockSpec((1, 2, VT_ROWS, n), lambda bi, i: (bi, 0, 0, 0))
    return pl.pallas_call(
        _nsa_kernel,
        grid=(b, s // t),
        in_specs=[pl.BlockSpec(memory_space=pltpu.SMEM),
                  pl.BlockSpec((1, t, 512), lambda bi, i: (bi, i, ODD_OFF["qd"][0] // 512)),
                  pl.BlockSpec((1, t, 512), lambda bi, i: (bi, i, ODD_OFF["gd"][0] // 512)),
                  pl.BlockSpec((1, t, 128), lambda bi, i: (bi, i, ODD_OFF["gate"][0] // 128)),
                  pl.BlockSpec((t, 128), lambda bi, i: (i, 0)),
                  pl.BlockSpec((t, 128), lambda bi, i: (i, 0)),
                  full((1, 512)), dup(nseg), tr(nseg), dup(s), tr(s), dup(s), tr(s), full(ovlt.shape)],
        out_specs=pl.BlockSpec((1, t, 512), lambda bi, i: (bi, i, 0)),
        out_shape=jax.ShapeDtypeStruct((b, s, 512), BF16),
        compiler_params=_cparams(("arbitrary", "arbitrary")),
        name="nsa",
    )(bounds, z, z, z, cos, sin_s, jnp.tile(qn, 8).reshape(1, 512), kc, vct, ks, vst, kw, vwt, jnp.asarray(ovlt, BF16))


def _permute_cols(w, order, src):
    cols = []
    for name, width in order:
        o, sw = src[name]
        blk = w[:, o:o + sw]
        if sw < width:
            blk = jnp.pad(blk, ((0, 0), (0, width - sw)))
        cols.append(blk)
    return jnp.concatenate(cols, axis=1).astype(BF16)


def _block_diag_pairs(w):
    z = jnp.zeros((4, LANES, LANES), w.dtype)
    z = z.at[:, 0:64, 0:64].set(w[0::2])
    z = z.at[:, 64:128, 64:128].set(w[1::2])
    return z.astype(BF16)


def _rope_tables(pos):
    half = HEAD_DIM // 2
    inv = ROPE_THETA ** (-jnp.arange(half, dtype=F32) / half)
    ang = pos.astype(F32)[:, None] * inv[None, :]
    cos, sin = jnp.cos(ang), jnp.sin(ang)
    cos_t = jnp.tile(cos, (1, 4))
    sin_t = jnp.tile(jnp.concatenate([-sin, sin], axis=1), (1, 2))
    return cos_t, sin_t


def _expand_compress_w1(w1):
    hdim = w1.shape[1]
    w = w1.reshape(2, CMP_STRIDE, 1, HEAD_DIM, 1, hdim)
    same_head = jnp.eye(2, dtype=w1.dtype).reshape(1, 1, 2, 1, 2, 1)
    return (w * same_head).reshape(2, CMP_STRIDE * 2 * HEAD_DIM, 2 * hdim).astype(BF16)


def _expand_compress_w2(w2):
    hdim, hd = w2.shape
    z = jnp.zeros((2 * hdim, 2 * hd), w2.dtype)
    z = z.at[0:hdim, 0:hd].set(w2)
    z = z.at[hdim:, hd:].set(w2)
    return z.astype(BF16)


def _expand_pe(pe):
    p = pe.reshape(2, CMP_STRIDE, 1, HEAD_DIM)
    return jnp.broadcast_to(p, (2, CMP_STRIDE, 2, HEAD_DIM)).reshape(2, CMP_STRIDE * 2 * HEAD_DIM)


def _even_layer(h, mem, g, mem_g, w_mem_kv, m_qn, m_kn, w_in, w_out, a_qn, a_kn, a_sinks,
                conv_w, conv_b, w_r, b_r, w_i, b_i, lam, cos, sin_s):
    b, s, d = h.shape
    h2 = h.reshape(b * s, d)
    z, ob = _inproj_rglru(h2, g, _permute_cols(w_in, EVEN_ORDER, EVEN_SRC), s, conv_w, conv_b,
                          _block_diag_pairs(w_r), b_r, _block_diag_pairs(w_i), b_i, lam)
    z = z.reshape(b, s, EVEN_COLS)
    oa = _swa(z, cos, sin_s, a_qn, a_kn, a_sinks)
    out = _outproj_mem(h2, oa.reshape(b * s, 512), ob, z.reshape(b * s, EVEN_COLS), EVEN_OFF["qm"][0], EVEN_OFF["gm"][0],
                       s, mem, mem_g, w_mem_kv.astype(BF16), m_qn, m_kn, w_out.astype(BF16))
    return out.reshape(b, s, d)


def _odd_layer(h, mem, g, mem_g, w_mem_kv, m_qn, m_kn, w_in, w_out, c_lb, c_og,
               d_qn, d_kn_cmp, d_kn_slc, d_kn_win, pe_k, pe_v, w1k, w2k, w1v, w2v, cos, sin_s, cosc, sinc):
    b, s, d = h.shape
    h2 = h.reshape(b * s, d)
    z, kcd, vcd, oc = _inproj_hgrn(h2, g, _permute_cols(w_in, ODD_ORDER, ODD_SRC), s, c_lb, c_og)
    z = z.reshape(b, s, ODD_SPLIT[0])
    nseg = s // CMP_STRIDE
    seg = lambda a: a.reshape(b, nseg, CMP_STRIDE * 128)
    kc, vc, ks, vs, kw, vw = _nsa_prep(
        z, seg(kcd), seg(vcd), cos, sin_s, cosc, sinc, d_kn_cmp, d_kn_slc, d_kn_win,
        _expand_pe(pe_k), _expand_pe(pe_v), _expand_compress_w1(w1k), _expand_compress_w2(w2k),
        _expand_compress_w1(w1v), _expand_compress_w2(w2v).T)
    bounds = jnp.stack([_score_bound(d_qn, kn) for kn in (d_kn_cmp, d_kn_slc, d_kn_win)]).astype(F32)
    od = _nsa(z, cos, sin_s, d_qn, bounds, kc, vc, ks, vs, kw, vw)
    out = _outproj_mem(h2, oc, od.reshape(b * s, 512), z.reshape(b * s, ODD_SPLIT[0]), ODD_OFF["qm"][0], ODD_OFF["gm"][0],
                       s, mem, mem_g, w_mem_kv.astype(BF16), m_qn, m_kn, w_out.astype(BF16))
    return out.reshape(b, s, d)


def kernel(x, mem, norm_g, mem_norm_g, mem_w_kv, mem_qn, mem_kn, ev_w_in, ev_w_out, a_qn, a_kn, a_sinks,
           b_conv_w, b_conv_b, b_w_r, b_b_r, b_w_i, b_b_i, b_lambda, od_w_in, od_w_out, c_lb, c_onorm,
           d_qn, d_kn_cmp, d_kn_slc, d_kn_win, d_pe_k, d_pe_v, d_w1k, d_w2k, d_w1v, d_w2v):
    depth = norm_g.shape[0]
    assert depth == 2 and c_lb.shape[0] == 2, "the HGRN2 lower-bound formula in the kernel is written for depth 2"
    s = x.shape[1]
    assert s % 256 == 0 and s >= D_WINDOW
    pos = jnp.arange(s)
    cos, sin_s = _rope_tables(pos)
    nseg = s // CMP_STRIDE
    cmp_end = jnp.minimum(jnp.arange(nseg) * CMP_STRIDE + CMP_LEN - 1, s - 1)
    cosc, sinc = _rope_tables(cmp_end)
    h = _even_layer(x, mem, norm_g[0], mem_norm_g[0], mem_w_kv[0], mem_qn[0], mem_kn[0], ev_w_in[0], ev_w_out[0],
                    a_qn[0], a_kn[0], a_sinks[0], b_conv_w[0], b_conv_b[0], b_w_r[0], b_b_r[0], b_w_i[0], b_b_i[0],
                    b_lambda[0], cos, sin_s)
    h = _odd_layer(h, mem, norm_g[1], mem_norm_g[1], mem_w_kv[1], mem_qn[1], mem_kn[1], od_w_in[0], od_w_out[0],
                   c_lb, c_onorm[0], d_qn[0], d_kn_cmp[0], d_kn_slc[0], d_kn_win[0], d_pe_k[0], d_pe_v[0],
                   d_w1k[0], d_w2k[0], d_w1v[0], d_w2v[0], cos, sin_s, cosc, sinc)
    return h
```

```python
import functools

import numpy as np
import jax
import jax.numpy as jnp
from jax import lax
from jax.experimental import pallas as pl
from jax.experimental.pallas import tpu as pltpu

F32 = jnp.float32
BF16 = jnp.bfloat16

D_MODEL = 1024
N_MEM = 256
HEAD_DIM = 64
ROPE_THETA = 10000.0
EPS = 1e-6
NEG_INF = -1e30
POS_INF = 1e30
MASK_BIG = 1e30
LANES = 128

A_HEADS, A_KV_HEADS, A_WINDOW = 8, 2, 128
B_WIDTH, B_BLOCKS, B_CONV, B_C = 512, 8, 4, 8.0
M_HEADS = 4
C_HEADS, C_HEAD_DIM, C_CHUNK = 4, 128, 64
C_WIDTH = C_HEADS * C_HEAD_DIM
D_HEADS, D_KV_HEADS = 8, 2
CMP_LEN, CMP_STRIDE, CMP_HIDDEN = 32, 16, 128
SEL_LEN, SEL_TOPK = 64, 4
D_WINDOW = 512
D_BRANCHES = 3
SCALE = HEAD_DIM ** -0.5
LOG2E = 1.4426950408889634
Q_SCALE = SCALE * LOG2E

EVEN_ORDER = [("xb", 512), ("gb", 512), ("qa", 512), ("ga", 512), ("qm", 256), ("gm", 256), ("ka", 128), ("va", 128)]
EVEN_FUSED = 1024
EVEN_SRC = {"qa": (0, 512), "ka": (512, 128), "va": (640, 128), "ga": (768, 512), "xb": (1280, 512),
            "gb": (1792, 512), "qm": (2304, 256), "gm": (2560, 256)}
ODD_ORDER = [("qc", 512), ("fc", 512), ("ic", 512), ("gc", 512), ("qd", 512), ("gd", 512), ("qm", 256), ("gm", 256),
             ("ksd", 128), ("vsd", 128), ("kwd", 128), ("vwd", 128), ("gate", 128), ("kcd", 128), ("vcd", 128)]
ODD_FUSED = 2048
ODD_SPLIT = (2176, 128, 128)
ODD_SRC = {"qc": (0, 512), "fc": (512, 512), "ic": (1024, 512), "gc": (1536, 512), "qd": (2048, 512),
           "kcd": (2560, 128), "vcd": (2688, 128), "ksd": (2816, 128), "vsd": (2944, 128), "kwd": (3072, 128),
           "vwd": (3200, 128), "gate": (3328, 24), "gd": (3352, 512), "qm": (3864, 256), "gm": (4120, 256)}

VMEM_LIMIT = 48 * 1024 * 1024


def _offsets(order):
    off, out = 0, {}
    for name, w in order:
        out[name] = (off, w)
        off += w
    return out, off


EVEN_OFF, EVEN_COLS = _offsets(EVEN_ORDER[2:])
ODD_OFF, ODD_COLS = _offsets(ODD_ORDER[4:])


def _cparams(sem):
    return pltpu.CompilerParams(dimension_semantics=sem, vmem_limit_bytes=VMEM_LIMIT)


def _dot(a, b):
    return jnp.dot(a, b, preferred_element_type=F32)


def _dot_nt(a, b):
    return lax.dot_general(a, b, (((1,), (1,)), ((), ())), preferred_element_type=F32)


def _split(x):
    hi = x.astype(BF16)
    lo = (x - hi.astype(F32)).astype(BF16)
    return hi, lo


def _dot_split_rhs(m, x):
    hi, lo = _split(x)
    return _dot(m, hi) + _dot(m, lo)


def _lane(shape):
    return lax.broadcasted_iota(jnp.int32, shape, len(shape) - 1)


def _row(shape):
    return lax.broadcasted_iota(jnp.int32, shape, len(shape) - 2)


def _silu(x):
    return x * jax.nn.sigmoid(x)


def _seg_ones():
    r = lax.broadcasted_iota(jnp.int32, (LANES, LANES), 0) >> 6
    c = lax.broadcasted_iota(jnp.int32, (LANES, LANES), 1) >> 6
    return jnp.where(r == c, 1.0, 0.0).astype(BF16)


def _head_rms(x, gain):
    seg = _seg_ones()
    cols = []
    for c in range(x.shape[1] // LANES):
        xc = x[:, c * LANES:(c + 1) * LANES]
        ms = _dot((xc * xc).astype(BF16), seg) * (1.0 / HEAD_DIM)
        cols.append(xc * lax.rsqrt(ms + EPS))
    y = cols[0] if len(cols) == 1 else jnp.concatenate(cols, axis=1)
    return y * gain


def _rope(x, cos, sin_s):
    first = (_lane((x.shape[0], LANES)) & 63) < 32
    cols = []
    for c in range(x.shape[1] // LANES):
        xc = x[:, c * LANES:(c + 1) * LANES]
        sw = jnp.where(first, pltpu.roll(xc, 96, 1), pltpu.roll(xc, 32, 1))
        cols.append(xc * cos + sw * sin_s)
    return cols[0] if len(cols) == 1 else jnp.concatenate(cols, axis=1)


def _split_kv(k, fill=0.0):
    lo = _lane(k.shape) < 64
    return jnp.where(lo, k, fill), jnp.where(lo, pltpu.roll(k, 64, 1), fill)


def _q_heads(q, fills=None):
    lo = _lane((q.shape[0], LANES)) < 64
    out = []
    for c in range(q.shape[1] // LANES):
        qc = q[:, c * LANES:(c + 1) * LANES]
        fill = 0.0 if fills is None else fills[c]
        out.append(jnp.where(lo, qc, fill).astype(BF16))
        out.append(jnp.where(lo, pltpu.roll(qc, 64, 1), fill).astype(BF16))
    return out


VT_ROWS = HEAD_DIM + 16


def _with_ones_row(vt):
    n = vt.shape[1]
    pad = jnp.where(_row((VT_ROWS - HEAD_DIM, n)) == 0, 1.0, 0.0)
    return jnp.concatenate([vt, pad], axis=0)


MAX_SAFE_BOUND = 60.0


def _score_bound(q_gain, k_gain):
    return 1.02 * HEAD_DIM * SCALE * LOG2E * jnp.max(jnp.abs(q_gain)) * jnp.max(jnp.abs(k_gain))


def _with_fixed_reference(bound, body):
    small = bound <= MAX_SAFE_BOUND

    @pl.when(small)
    def _():
        body(bound)

    @pl.when(jnp.logical_not(small))
    def _():
        body(None)


def _exp_cols(s, extra=None, fixed=None):
    if fixed is not None:
        return jnp.exp2(s - fixed).astype(BF16), fixed
    m = jnp.max(s, axis=0, keepdims=True)
    if extra is not None:
        m = jnp.maximum(m, extra)
    return jnp.exp2(s - m).astype(BF16), m


def _finish(acc, extra_den=None):
    den = acc[HEAD_DIM:HEAD_DIM + 1, :]
    if extra_den is not None:
        den = den + extra_den
    return acc[0:HEAD_DIM, :] * (1.0 / den)


def _online_update(state, s, vt_tile, fixed=None):
    m, acc = state
    if fixed is not None:
        return m, acc + _dot(vt_tile, _exp_cols(s, fixed=fixed)[0])
    p, m_new = _exp_cols(s, m)
    acc = jnp.exp2(m - m_new) * acc + _dot(vt_tile, p)
    return m_new, acc


def _tile4(x):
    return jnp.concatenate([x, x, x, x], axis=1)


def _outproj_mem_kernel(tiles_per_seq, bound_ref, h_ref, a_ref, b_ref, q_ref, gm_ref, mem_ref, mg_ref, wkv_ref,
                        qn_ref, kn_ref, w_ref, o_ref, k_scr, vt_scr, acc_scr):
    @pl.when(pl.program_id(0) % tiles_per_seq == 0)
    def _():
        m = mem_ref[0]
        ms = jnp.mean(m * m, axis=-1, keepdims=True)
        mn = (m * lax.rsqrt(ms + EPS) * mg_ref[...]).astype(BF16)
        kv = _dot(mn, wkv_ref[...])
        km = _head_rms(kv[:, 0:256], kn_ref[...])
        for c in range(2):
            k0, k1 = _split_kv(km[:, c * LANES:(c + 1) * LANES])
            k_scr[2 * c] = k0.astype(BF16)
            k_scr[2 * c + 1] = k1.astype(BF16)
        vt = kv[:, 256:512].T
        for h in range(M_HEADS):
            vt_scr[h] = _with_ones_row(vt[h * HEAD_DIM:(h + 1) * HEAD_DIM, :]).astype(BF16)

    def body(fixed):
        a, b = a_ref[...], b_ref[...]
        pending = list(range(0, D_MODEL, 256))

        def emit():
            if pending:
                c = pending.pop(0)
                acc_scr[:, c:c + 256] = (h_ref[:, c:c + 256] + _dot(a, w_ref[0:512, c:c + 256])
                                         + _dot(b, w_ref[512:1024, c:c + 256]))

        q = _head_rms(q_ref[...].astype(F32), qn_ref[...]) * Q_SCALE
        scores = [_dot_nt(k_scr[h], qh) for h, qh in enumerate(_q_heads(q))]
        rows = []
        for h, sc in enumerate(scores):
            emit()
            rows.append(_finish(_dot(vt_scr[h], _exp_cols(sc, fixed=fixed)[0])))
        om = (jnp.concatenate(rows, axis=0).T * _silu(gm_ref[...].astype(F32))).astype(BF16)
        while pending:
            emit()
        for c in range(0, D_MODEL, 256):
            o_ref[:, c:c + 256] = acc_scr[:, c:c + 256] + _dot(om, w_ref[1024:1280, c:c + 256])

    _with_fixed_reference(bound_ref[0], body)


def _outproj_mem(h2d, oa, ob, z2d, q_off, g_off, seq, mem, mem_g, wkv_bf16, qn, kn, w_bf16, tm=1024):
    n, d = h2d.shape
    full = lambda shp: pl.BlockSpec(shp, lambda i: (0,) * len(shp))
    tps = seq // tm
    return pl.pallas_call(
        functools.partial(_outproj_mem_kernel, tps),
        grid=(n // tm,),
        in_specs=[pl.BlockSpec(memory_space=pltpu.SMEM),
                  pl.BlockSpec((tm, d), lambda i: (i, 0)),
                  pl.BlockSpec((tm, 512), lambda i: (i, 0)),
                  pl.BlockSpec((tm, 512), lambda i: (i, 0)),
                  pl.BlockSpec((tm, 256), lambda i: (i, q_off // 256)),
                  pl.BlockSpec((tm, 256), lambda i: (i, g_off // 256)),
                  pl.BlockSpec((1, N_MEM, D_MODEL), lambda i: (i // tps, 0, 0)),
                  full((1, D_MODEL)), full((D_MODEL, 512)), full((1, 256)), full((1, 256)), full((1280, d))],
        out_specs=pl.BlockSpec((tm, d), lambda i: (i, 0)),
        out_shape=jax.ShapeDtypeStruct((n, d), F32),
        scratch_shapes=[pltpu.VMEM((M_HEADS, N_MEM, LANES), BF16), pltpu.VMEM((M_HEADS, VT_ROWS, N_MEM), BF16),
                        pltpu.VMEM((tm, d), F32)],
        compiler_params=_cparams(("arbitrary",)),
        name="outproj_mem",
    )(_score_bound(qn, kn).reshape(1).astype(F32), h2d, oa, ob, z2d, z2d, mem, mem_g.reshape(1, D_MODEL), wkv_bf16,
      jnp.tile(qn, 4).reshape(1, 256), jnp.tile(kn, 4).reshape(1, 256), w_bf16)


SWA_SUB = 256


def _swa_kernel(bound_ref, *refs):
    _with_fixed_reference(bound_ref[0], functools.partial(_swa_body, refs))


def _swa_body(refs, fixed):
    (q_ref, g_ref, kc_ref, kp_ref, vc_ref, vp_ref, cosc_ref, sinc_ref, cosp_ref, sinp_ref,
     qn_ref, kn_ref, sink_ref, o_ref) = refs
    i = pl.program_id(1)
    t = q_ref.shape[1]
    tp = kp_ref.shape[1]
    grp = A_HEADS // A_KV_HEADS
    q = _rope(_head_rms(q_ref[0].astype(F32), qn_ref[...]), cosc_ref[...], sinc_ref[...]) * Q_SCALE
    qh = _q_heads(q)
    kc = _rope(_head_rms(kc_ref[0].astype(F32), kn_ref[...]), cosc_ref[...], sinc_ref[...])
    kp = _rope(_head_rms(kp_ref[0].astype(F32), kn_ref[...]), cosp_ref[...], sinp_ref[...])
    ks = [x.astype(BF16) for x in _split_kv(jnp.concatenate([kp, kc], axis=0))]
    vt = jnp.concatenate([vp_ref[0].astype(F32).T, vc_ref[0].astype(F32).T], axis=1)
    ts_ = min(SWA_SUB, t)
    krow = _row((tp + ts_, ts_))
    qcol = _lane((tp + ts_, ts_))
    subs = range(t // ts_)
    scores, biases = [], []
    for j in subs:
        ts = i * t + j * ts_ - tp + krow
        rel = i * t + j * ts_ + qcol - ts
        biases.append(_tile4(jnp.where((rel >= 0) & (rel < A_WINDOW) & (ts >= 0), 0.0, NEG_INF)))
        scores.append([_dot_nt(ks[kv][j * ts_:j * ts_ + tp + ts_, :],
                               jnp.concatenate([qh[kv * grp + g][j * ts_:(j + 1) * ts_, :] for g in range(grp)], axis=0))
                       for kv in range(A_KV_HEADS)])
    blocks = []
    for j in subs:
        rows = []
        for kv in range(A_KV_HEADS):
            sink = jnp.concatenate([jnp.broadcast_to(sink_ref[kv * grp + g:kv * grp + g + 1, 0:1], (1, ts_))
                                    for g in range(grp)], axis=1) * LOG2E
            p, m = _exp_cols(scores[j][kv] + biases[j], sink, fixed)
            vt_kv = _with_ones_row(vt[kv * HEAD_DIM:(kv + 1) * HEAD_DIM, j * ts_:j * ts_ + tp + ts_]).astype(BF16)
            o_t = _finish(_dot(vt_kv, p), jnp.exp2(sink - m))
            rows.extend(o_t[:, g * ts_:(g + 1) * ts_] for g in range(grp))
        blocks.append(jnp.concatenate(rows, axis=0))
    o = (blocks[0] if len(blocks) == 1 else jnp.concatenate(blocks, axis=1)).T
    o_ref[0] = (o * _silu(g_ref[0].astype(F32))).astype(BF16)


def _swa(z, cos, sin_s, qn, kn, sinks, t=512):
    b, s, _ = z.shape
    tp = A_WINDOW
    qo, go = EVEN_OFF["qa"][0] // 512, EVEN_OFF["ga"][0] // 512
    ko, vo = EVEN_OFF["ka"][0] // 128, EVEN_OFF["va"][0] // 128
    prev = lambda i: jnp.maximum(i * (t // tp) - 1, 0)
    bound = jnp.maximum(_score_bound(qn, kn), jnp.max(sinks) * LOG2E).reshape(1).astype(F32)
    return pl.pallas_call(
        _swa_kernel,
        grid=(b, s // t),
        in_specs=[pl.BlockSpec(memory_space=pltpu.SMEM),
                  pl.BlockSpec((1, t, 512), lambda bi, i: (bi, i, qo)),
                  pl.BlockSpec((1, t, 512), lambda bi, i: (bi, i, go)),
                  pl.BlockSpec((1, t, 128), lambda bi, i: (bi, i, ko)),
                  pl.BlockSpec((1, tp, 128), lambda bi, i: (bi, prev(i), ko)),
                  pl.BlockSpec((1, t, 128), lambda bi, i: (bi, i, vo)),
                  pl.BlockSpec((1, tp, 128), lambda bi, i: (bi, prev(i), vo)),
                  pl.BlockSpec((t, 128), lambda bi, i: (i, 0)),
                  pl.BlockSpec((t, 128), lambda bi, i: (i, 0)),
                  pl.BlockSpec((tp, 128), lambda bi, i: (prev(i), 0)),
                  pl.BlockSpec((tp, 128), lambda bi, i: (prev(i), 0)),
                  pl.BlockSpec((1, 512), lambda bi, i: (0, 0)),
                  pl.BlockSpec((1, 128), lambda bi, i: (0, 0)),
                  pl.BlockSpec((A_HEADS, 128), lambda bi, i: (0, 0))],
        out_specs=pl.BlockSpec((1, t, 512), lambda bi, i: (bi, i, 0)),
        out_shape=jax.ShapeDtypeStruct((b, s, 512), BF16),
        compiler_params=_cparams(("arbitrary", "arbitrary")),
        name="swa",
    )(bound, z, z, z, z, z, z, cos, sin_s, cos, sin_s, jnp.tile(qn, 8).reshape(1, 512), jnp.tile(kn, 2).reshape(1, 128),
      jnp.broadcast_to(sinks.reshape(A_HEADS, 1), (A_HEADS, 128)))


def _rglru_half(cols, xb, gb, emit, cw_ref, cb_ref, wr_ref, br_ref, wi_ref, bi_ref, lam_ref, xbuf, hcar, a_scr, u_scr, h_scr):
    t, wdt = xb.shape
    xbuf[8:t + 8, cols] = xb
    xc = cb_ref[:, cols] + cw_ref[0:1, cols] * xbuf[5:t + 5, cols]
    for j in range(1, B_CONV):
        xc = xc + cw_ref[j:j + 1, cols] * xbuf[5 + j:t + 5 + j, cols]
    xbuf[0:8, cols] = xbuf[t:t + 8, cols]
    emit()

    r_cols, i_cols = [], []
    for c in range(wdt // LANES):
        blk = cols.start // LANES + c
        xcc = xc[:, c * LANES:(c + 1) * LANES].astype(BF16)
        r_cols.append(_dot(xcc, wr_ref[blk]))
        i_cols.append(_dot(xcc, wi_ref[blk]))
    emit()
    r = jax.nn.sigmoid(jnp.concatenate(r_cols, axis=1) + br_ref[:, cols])
    ig = jax.nn.sigmoid(jnp.concatenate(i_cols, axis=1) + bi_ref[:, cols])
    nl = -lam_ref[:, cols]
    softplus = jnp.maximum(nl, 0.0) + jnp.log(1.0 + jnp.exp(-jnp.abs(nl)))
    log_a = -B_C * r * softplus
    a = jnp.exp(log_a)
    om = 1.0 - a * a
    u = om * lax.rsqrt(jnp.maximum(om, 1e-30)) * (ig * xc)
    emit()

    a = a.reshape(t // 8, 8, wdt)
    u = u.reshape(t // 8, 8, wdt)
    r8 = lax.broadcasted_iota(jnp.int32, a.shape, 1)
    for d in (1, 2, 4):
        a_sh = pltpu.roll(a, d, 1)
        u_sh = pltpu.roll(u, d, 1)
        m = r8 >= d
        u = jnp.where(m, a * u_sh + u, u)
        a = jnp.where(m, a * a_sh, a)
    a_scr[:, cols] = a.reshape(t, wdt)
    u_scr[:, cols] = u.reshape(t, wdt)
    emit()

    def body(j, h):
        off = pl.multiple_of(j * 8, 8)
        hh = a_scr[pl.ds(off, 8), cols] * h + u_scr[pl.ds(off, 8), cols]
        h_scr[pl.ds(off, 8), cols] = hh
        return hh[7:8, :]

    h_last = lax.fori_loop(0, t // 8, body, hcar[0:1, cols], unroll=True)
    hcar[0:1, cols] = h_last
    emit()
    return h_scr[:, cols] * _silu(gb)


def _inproj_rglru_kernel(tiles_per_seq, x_ref, g_ref, w_ref, cw_ref, cb_ref, wr_ref, br_ref, wi_ref, bi_ref, lam_ref,
                         z_ref, ob_ref, xbuf, hcar, a_scr, u_scr, h_scr):
    @pl.when(pl.program_id(0) % tiles_per_seq == 0)
    def _():
        xbuf[0:8, :] = jnp.zeros((8, B_WIDTH), F32)
        hcar[...] = jnp.zeros((8, B_WIDTH), F32)

    x = x_ref[...]
    ms = jnp.mean(x * x, axis=-1, keepdims=True)
    xn = (x * lax.rsqrt(ms + EPS) * g_ref[...]).astype(BF16)
    half = B_WIDTH // 2
    proj = {}

    def project(name, hp):
        base = (B_WIDTH if name == "gb" else 0) + hp * half
        proj[name, hp] = _dot(xn, w_ref[:, base:base + half])

    def store_z(c):
        z_ref[:, c:c + 256] = _dot(xn, w_ref[:, EVEN_FUSED + c:EVEN_FUSED + c + 256]).astype(BF16)

    pending = [functools.partial(store_z, c) for c in range(0, z_ref.shape[-1], 256)]
    pending.insert(1, functools.partial(project, "xb", 1))
    pending.insert(3, functools.partial(project, "gb", 1))

    def emit():
        if pending:
            pending.pop(0)()

    project("xb", 0)
    project("gb", 0)
    for hp in range(2):
        cols = slice(hp * half, (hp + 1) * half)
        ob_ref[:, cols] = _rglru_half(cols, proj["xb", hp], proj["gb", hp], emit, cw_ref, cb_ref, wr_ref, br_ref,
                                      wi_ref, bi_ref, lam_ref, xbuf, hcar, a_scr, u_scr, h_scr).astype(BF16)
    while pending:
        emit()


def _inproj_rglru(x2d, gain, w_bf16, seq, conv_w, conv_b, wr_bd, b_r, wi_bd, b_i, lam, tm=1024):
    n, d = x2d.shape
    nc = w_bf16.shape[1]
    row = lambda v: v.reshape(1, B_WIDTH)
    full = lambda shp: pl.BlockSpec(shp, lambda i: (0,) * len(shp))
    return pl.pallas_call(
        functools.partial(_inproj_rglru_kernel, seq // tm),
        grid=(n // tm,),
        in_specs=[pl.BlockSpec((tm, d), lambda i: (i, 0)), full((1, d)), full((d, nc)),
                  full((B_CONV, B_WIDTH)), full((1, B_WIDTH)), full((4, LANES, LANES)), full((1, B_WIDTH)),
                  full((4, LANES, LANES)), full((1, B_WIDTH)), full((1, B_WIDTH))],
        out_specs=[pl.BlockSpec((tm, nc - EVEN_FUSED), lambda i: (i, 0)), pl.BlockSpec((tm, B_WIDTH), lambda i: (i, 0))],
        out_shape=[jax.ShapeDtypeStruct((n, nc - EVEN_FUSED), BF16), jax.ShapeDtypeStruct((n, B_WIDTH), BF16)],
        scratch_shapes=[pltpu.VMEM((tm + 8, B_WIDTH), F32), pltpu.VMEM((8, B_WIDTH), F32),
                        pltpu.VMEM((tm, B_WIDTH), F32), pltpu.VMEM((tm, B_WIDTH), F32), pltpu.VMEM((tm, B_WIDTH), F32)],
        compiler_params=_cparams(("arbitrary",)),
        name="inproj_rglru",
    )(x2d, gain.reshape(1, d), w_bf16, conv_w, row(conv_b), wr_bd, row(b_r), wi_bd, row(b_i), row(lam))


def _hgrn_consts():
    c = C_CHUNK
    t = np.arange(c)[:, None]
    s = np.arange(c)[None, :]
    masks = []
    hs = c // 2
    while hs >= 1:
        same = (t // (2 * hs)) == (s // (2 * hs))
        masks.append(same & ((t // hs) % 2 == 1) & ((s // hs) % 2 == 0))
        hs //= 2
    pairs = [np.concatenate(masks[n:n + 2], axis=1) for n in range(0, len(masks), 2)]
    return (s <= t).astype(np.float32), np.stack(pairs).astype(np.float32)


def _split_points(b, level):
    c, wdt = b.shape
    hs = c >> (level + 1)
    blk = 2 * hs
    if blk >= 8:
        return jnp.concatenate([jnp.broadcast_to(b[m * blk + hs - 1:m * blk + hs, :], (blk, wdt))
                                for m in range(c // blk)], axis=0)
    b3 = b.reshape(c // 8, 8, wdt)
    sub = lax.broadcasted_iota(jnp.int32, b3.shape, 1)
    if hs == 2:
        out = jnp.where(sub < 4, jnp.broadcast_to(b3[:, 1:2, :], b3.shape), jnp.broadcast_to(b3[:, 5:6, :], b3.shape))
    else:
        out = jnp.where((sub & 1) == 0, b3, pltpu.roll(b3, 1, 1))
    return out.reshape(c, wdt)


def _hgrn_chunk(lb, q_in, f_in, v_in, g_in, og_ref, mst_ref, msk_ref, st_scr, emit, emit2):
    c = C_CHUNK
    npair = msk_ref.shape[0]
    zeros = jnp.zeros((c, LANES), BF16)
    f = lb + (1.0 - lb) * jax.nn.sigmoid(f_in)
    g = jnp.log2(f)
    g_hi, g_lo = _split(g)
    b_all = _dot(mst_ref[...], g_hi) + _dot(mst_ref[...], g_lo)
    emit()
    kk_all = 1.0 - f
    qf_all = _silu(q_in)
    v_all = v_in
    kk_bf, qf_bf = kk_all.astype(BF16), qf_all.astype(BF16)
    heads = range(C_HEADS)
    cols = [slice(h * LANES, (h + 1) * LANES) for h in heads]
    sts = [st_scr[h] for h in heads]
    os = [_dot_nt((qf_all[:, cols[h]] * jnp.exp2(b_all[:, cols[h]])).astype(BF16), sts[h].astype(BF16)) for h in heads]
    att = [[] for _ in heads]
    bms = [_split_points(b_all, l) for l in range(2 * npair)]
    for n in range(npair):
        lhs, rhs = [], []
        for h in heads:
            b = b_all[:, cols[h]]
            qt, kt = [], []
            for l in (2 * n, 2 * n + 1):
                bm = bms[l][:, cols[h]]
                qt.append(qf_bf[:, cols[h]] * jnp.exp2(jnp.minimum(b - bm, 1.0)).astype(BF16))
                kt.append(kk_bf[:, cols[h]] * jnp.exp2(jnp.minimum(bm - b, 1.0)).astype(BF16))
            lhs.append(jnp.concatenate(qt, axis=1))
            rhs.append(jnp.concatenate([jnp.concatenate([kt[0], zeros], axis=1),
                                        jnp.concatenate([zeros, kt[1]], axis=1)], axis=0))
        prods = [_dot_nt(lhs[h], rhs[h]) for h in heads]
        if n == 1:
            emit2()
        for h in heads:
            att[h].append(jnp.where(msk_ref[n] > 0.5, prods[h], 0.0).astype(BF16))
    vbs = [v_all[:, cols[h]].astype(BF16) for h in heads]
    intra = [_dot(jnp.concatenate(att[h], axis=1), jnp.concatenate([vbs[h]] * (2 * npair), axis=0)) for h in heads]
    b_last = b_all[c - 1:c, :]
    kd = (kk_all * jnp.exp2(b_last - b_all)).astype(BF16)
    upd = [_dot(v_all[:, cols[h]].T.astype(BF16), kd[:, cols[h]]) for h in heads]
    decay = jnp.exp2(b_last)
    diag = qf_all * kk_all
    outs = []
    for h in heads:
        st_scr[h] = sts[h] * decay[:, cols[h]] + upd[h]
        o = os[h] + intra[h] + jnp.sum(diag[:, cols[h]], axis=-1, keepdims=True) * v_all[:, cols[h]]
        ms = jnp.mean(o * o, axis=-1, keepdims=True)
        outs.append(o * lax.rsqrt(ms + EPS) * og_ref[...])
    return jnp.concatenate(outs, axis=1) * _silu(g_in)


def _inproj_hgrn_kernel(tiles_per_seq, x_ref, g_ref, w_ref, lb_ref, og_ref, mst_ref, msk_ref,
                        z_ref, kc_ref, vc_ref, oc_ref, st_scr):
    @pl.when(pl.program_id(0) % tiles_per_seq == 0)
    def _():
        st_scr[...] = jnp.zeros(st_scr.shape, F32)

    x = x_ref[...]
    ms = jnp.mean(x * x, axis=-1, keepdims=True)
    xn = (x * lax.rsqrt(ms + EPS) * g_ref[...]).astype(BF16)
    rows_half = x.shape[0] // 2
    own = {}

    def project_own(k, half):
        r = slice(half * rows_half, (half + 1) * rows_half)
        own[k, half] = _dot(xn[r, :], w_ref[:, k * C_WIDTH:(k + 1) * C_WIDTH])

    o_refs = (z_ref, kc_ref, vc_ref)
    starts = np.cumsum([0] + [o.shape[-1] for o in o_refs])

    def project_rest(c):
        w = min(256, int(starts[-1]) - c)
        r = _dot(xn, w_ref[:, ODD_FUSED + c:ODD_FUSED + c + w]).astype(BF16)
        for o_ref, lo, hi in zip(o_refs, starts[:-1], starts[1:]):
            a, b = max(c, int(lo)), min(c + w, int(hi))
            if a < b:
                o_ref[:, a - int(lo):b - int(lo)] = r[:, a - c:b - c]

    pending = [functools.partial(project_own, k, 1) for k in range(4)]
    pending += [functools.partial(project_rest, c) for c in range(0, int(starts[-1]), 256)]

    def emit():
        if pending:
            pending.pop(0)()

    for k in range(4):
        project_own(k, 0)

    p = lb_ref[...]
    pm = jnp.maximum(p[0:1, :], p[1:2, :])
    e0, e1 = jnp.exp(p[0:1, :] - pm), jnp.exp(p[1:2, :] - pm)
    lb = e1 / (e0 + e1)
    nchunk = x.shape[0] // C_CHUNK
    for ch in range(nchunk):
        half, sub = divmod(ch, nchunk // 2)
        rows = slice(ch * C_CHUNK, (ch + 1) * C_CHUNK)
        part = slice(sub * C_CHUNK, (sub + 1) * C_CHUNK)
        q_in, f_in, v_in, g_in = (own[k, half][part, :] for k in range(4))
        second = emit if ch < 6 else (lambda: None)
        oc_ref[rows, :] = _hgrn_chunk(lb, q_in, f_in, v_in, g_in, og_ref, mst_ref, msk_ref, st_scr, emit,
                                      second).astype(BF16)
    while pending:
        emit()


def _inproj_hgrn(x2d, gain, w_bf16, seq, c_lb, c_og, tm=512):
    n, d = x2d.shape
    nc = w_bf16.shape[1]
    assert ODD_FUSED + sum(ODD_SPLIT) == nc
    mst, msk = _hgrn_consts()
    full = lambda shp: pl.BlockSpec(shp, lambda i: (0,) * len(shp))
    widths = ODD_SPLIT + (C_WIDTH,)
    return pl.pallas_call(
        functools.partial(_inproj_hgrn_kernel, seq // tm),
        grid=(n // tm,),
        in_specs=[pl.BlockSpec((tm, d), lambda i: (i, 0)), full((1, d)), full((d, nc)),
                  full(c_lb.shape), full((1, C_HEAD_DIM)), full(mst.shape), full(msk.shape)],
        out_specs=[pl.BlockSpec((tm, w), lambda i: (i, 0)) for w in widths],
        out_shape=[jax.ShapeDtypeStruct((n, w), BF16) for w in widths],
        scratch_shapes=[pltpu.VMEM((C_HEADS, C_HEAD_DIM, C_HEAD_DIM), F32)],
        compiler_params=_cparams(("arbitrary",)),
        name="inproj_hgrn",
    )(x2d, gain.reshape(1, d), w_bf16, c_lb, c_og.reshape(1, C_HEAD_DIM), jnp.asarray(mst, BF16), jnp.asarray(msk, F32))


def _nsa_prep_kernel(kcr_ref, vcr_ref, ks_ref, vs_ref, kw_ref, vw_ref, cos_ref, sin_ref, cosc_ref, sinc_ref,
                     kncmp_ref, knslc_ref, knwin_ref, pek_ref, pev_ref, w1k_ref, w2k_ref, w1v_ref, w2vt_ref,
                     kc_o, vct_o, ks_o, vst_o, kw_o, vwt_o):
    def hidden(xr_ref, pe_ref, w1_ref):
        xr = xr_ref[0].astype(F32)
        top = _dot((xr + pe_ref[0:1, :]).astype(BF16), w1_ref[0])
        bot = _dot((xr + pe_ref[1:2, :]).astype(BF16), w1_ref[1])
        pre = top + pltpu.roll(bot, bot.shape[0] - 1, 0)
        return _silu(pre).astype(BF16)

    kc = _dot(hidden(kcr_ref, pek_ref, w1k_ref), w2k_ref[...])
    kc = _rope(_head_rms(kc, kncmp_ref[...]), cosc_ref[...], sinc_ref[...])
    d0, d1 = _split_kv(kc)
    kc_o[0, 0] = d0.astype(BF16)
    kc_o[0, 1] = d1.astype(BF16)
    vct = _dot_nt(w2vt_ref[...], hidden(vcr_ref, pev_ref, w1v_ref))
    for kv in range(D_KV_HEADS):
        vct_o[0, kv] = _with_ones_row(vct[kv * HEAD_DIM:(kv + 1) * HEAD_DIM, :]).astype(BF16)

    s = ks_ref.shape[1]
    step = 256
    for r0 in range(0, s, step):
        rows = slice(r0, r0 + step)
        cos, sin = cos_ref[rows, :], sin_ref[rows, :]
        ks = _rope(_head_rms(ks_ref[0, rows, :].astype(F32), knslc_ref[...]), cos, sin)
        kw = _rope(_head_rms(kw_ref[0, rows, :].astype(F32), knwin_ref[...]), cos, sin)
        blk = (r0 + _row((step, LANES))) // SEL_LEN
        onehot = jnp.where(_lane((step, LANES)) - HEAD_DIM == blk, MASK_BIG, 0.0)
        for src, dst, fill in ((ks, ks_o, onehot), (kw, kw_o, 0.0)):
            d0, d1 = _split_kv(src, fill)
            dst[0, 0, rows, :] = d0.astype(BF16)
            dst[0, 1, rows, :] = d1.astype(BF16)
        for src, dst in ((vs_ref, vst_o), (vw_ref, vwt_o)):
            vt = src[0, rows, :].astype(F32).T
            for kv in range(D_KV_HEADS):
                dst[0, kv, :, rows] = _with_ones_row(vt[kv * HEAD_DIM:(kv + 1) * HEAD_DIM, :]).astype(BF16)


def _nsa_prep(z, kcr, vcr, cos, sin_s, cosc, sinc, kn_cmp, kn_slc, kn_win, pek, pev, w1k, w2k, w1v, w2v):
    b, s, _ = z.shape
    nseg = s // CMP_STRIDE
    zb = lambda name: pl.BlockSpec((1, s, 128), lambda bi, o=ODD_OFF[name][0] // 128: (bi, 0, o))
    full = lambda shp: pl.BlockSpec(shp, lambda bi: (0,) * len(shp))
    seg = pl.BlockSpec((1, nseg, 2048), lambda bi: (bi, 0, 0))
    g2 = lambda g: jnp.tile(g, 2).reshape(1, 128)
    dup = lambda n: (jax.ShapeDtypeStruct((b, 2, n, 128), BF16), pl.BlockSpec((1, 2, n, 128), lambda bi: (bi, 0, 0, 0)))
    tr = lambda n: (jax.ShapeDtypeStruct((b, 2, VT_ROWS, n), BF16),
                    pl.BlockSpec((1, 2, VT_ROWS, n), lambda bi: (bi, 0, 0, 0)))
    outs = [dup(nseg), tr(nseg), dup(s), tr(s), dup(s), tr(s)]
    return pl.pallas_call(
        _nsa_prep_kernel,
        grid=(b,),
        in_specs=[seg, seg, zb("ksd"), zb("vsd"), zb("kwd"), zb("vwd"),
                  full((s, 128)), full((s, 128)), full((nseg, 128)), full((nseg, 128)),
                  full((1, 128)), full((1, 128)), full((1, 128)), full((2, 2048)), full((2, 2048)),
                  full((2, 2048, 256)), full((256, 128)), full((2, 2048, 256)), full((128, 256))],
        out_specs=[o[1] for o in outs],
        out_shape=[o[0] for o in outs],
        compiler_params=_cparams(("arbitrary",)),
        name="nsa_prep",
    )(kcr, vcr, z, z, z, z, cos, sin_s, cosc, sinc, g2(kn_cmp), g2(kn_slc), g2(kn_win), pek, pev, w1k, w2k, w1v, w2v)


SEL_CHUNK = 512
SEL_PIECE = 512


def _nsa_kernel(bound_ref, *refs):
    small = jnp.maximum(jnp.maximum(bound_ref[0], bound_ref[1]), bound_ref[2]) <= MAX_SAFE_BOUND

    @pl.when(small)
    def _():
        _nsa_body((bound_ref[0], bound_ref[1], bound_ref[2]), *refs)

    @pl.when(jnp.logical_not(small))
    def _():
        _nsa_body((None, None, None), *refs)


def _nsa_body(fixed, q_ref, gd_ref, gate_ref, cos_ref, sin_ref, qn_ref, kc_ref, vct_ref, ks_ref, vst_ref, kw_ref, vwt_ref,
              ovlt_ref, o_ref):
    fix_c, fix_s, fix_w = fixed
    i = pl.program_id(1)
    t = q_ref.shape[1]
    grp = D_HEADS // D_KV_HEADS
    w = grp * t
    kvs = range(D_KV_HEADS)
    q = _rope(_head_rms(q_ref[0].astype(F32), qn_ref[...]), cos_ref[...], sin_ref[...]) * Q_SCALE
    stack = lambda heads: [jnp.concatenate(heads[kv * grp:(kv + 1) * grp], axis=0) for kv in kvs]
    qs = stack(_q_heads(q))
    tq1 = i * t + _lane((1, t))

    nwin = D_WINDOW + t
    win0 = pl.multiple_of(jnp.maximum(i * t + t - nwin, 0), t)
    s_cmp = [_dot_nt(kc_ref[0, kv], qs[kv]) for kv in kvs]
    s_win = [_dot_nt(kw_ref[0, kv, pl.ds(win0, nwin), :], qs[kv]) for kv in kvs]

    ncmp = kc_ref.shape[2] - 1
    nsel = ovlt_ref.shape[0]
    crow = _row((kc_ref.shape[2], t))
    bias_c = _tile4(jnp.where((crow * CMP_STRIDE + (CMP_LEN - 1) <= tq1) & (crow < ncmp), 0.0, NEG_INF))
    row_ok = _tile4(tq1 >= CMP_LEN - 1)
    es = [_exp_cols(s_cmp[kv] + bias_c, fixed=fix_c)[0] for kv in kvs]
    accs = [_dot(vct_ref[0, kv], es[kv]) for kv in kvs]
    invs = [jnp.where(row_ok, 1.0 / accs[kv][HEAD_DIM:HEAD_DIM + 1, :], 0.0) for kv in kvs]
    oc_t = [accs[kv][0:HEAD_DIM, :] * invs[kv] for kv in kvs]
    psums = []
    for kv in kvs:
        p = es[kv].astype(F32) * invs[kv]
        psums.append(p[:, 0:t] + p[:, t:2 * t] + p[:, 2 * t:3 * t] + p[:, 3 * t:4 * t])
    imps = [_dot_split_rhs(ovlt_ref[...], psums[kv]) for kv in kvs]

    rel = tq1 - (win0 + _row((nwin, t)))
    bias_w = _tile4(jnp.where((rel >= 0) & (rel < D_WINDOW), 0.0, NEG_INF))
    ow_t = [_finish(_dot(vwt_ref[0, kv, :, pl.ds(win0, nwin)], _exp_cols(s_win[kv] + bias_w, fixed=fix_w)[0]))
            for kv in kvs]

    jrow = _row((nsel, t))
    jrow_f = jrow.astype(F32)
    cur = (i * t + _lane((nsel, t))) // SEL_LEN
    forced = (jrow == 0) | (jrow == cur)
    fills = []
    for kv in kvs:
        score = jnp.where(forced, POS_INF, jnp.where(jrow <= cur, imps[kv], NEG_INF))
        chosen = jnp.zeros((nsel, t), F32)
        for _ in range(min(SEL_TOPK, nsel)):
            mx = jnp.max(score, axis=0, keepdims=True)
            first = jnp.min(jnp.where(score == mx, jrow_f, 1e9), axis=0, keepdims=True)
            hit = jrow_f == first
            chosen = jnp.where(hit, 1.0, chosen)
            score = jnp.where(hit, -3e38, score)
        frame = jnp.concatenate([jnp.zeros((HEAD_DIM, t), F32), chosen - 1.0,
                                 jnp.zeros((LANES - HEAD_DIM - nsel, t), F32)], axis=0)
        fills.extend([frame.T] * (grp // 2))

    init = (jnp.full((1, w), NEG_INF, F32), jnp.zeros((VT_ROWS, w), F32))
    pc = SEL_PIECE

    def sweep(states, k_ref, vt_ref, queries, offs, biases, n):
        scores = [[_dot_nt(k_ref[0, kv, pl.ds(off, n), :], queries[kv]) for kv in kvs] for off in offs]
        for off, sc, bias in zip(offs, scores, biases):
            states = tuple(_online_update(states[kv], sc[kv] if bias is None else sc[kv] + bias,
                                          vt_ref[0, kv, :, pl.ds(off, n)], fix_s) for kv in kvs)
        return states

    qsel = stack(_q_heads(q, fills))
    ch = SEL_CHUNK
    own = pl.multiple_of(((i * t) // ch) * ch, ch)
    st = sweep((init, init), ks_ref, vst_ref, qsel, [own],
               [_tile4(jnp.where(own + _row((ch, t)) <= tq1, 0.0, NEG_INF))], ch)

    def sel_chunk(c, states):
        base = pl.multiple_of(c * ch, ch)
        return sweep(states, ks_ref, vst_ref, qsel, [base + n * pc for n in range(ch // pc)], [None] * (ch // pc), pc)

    st = lax.fori_loop(0, (i * t) // ch, sel_chunk, st)
    os_t = [_finish(acc) for (_, acc) in st]

    g_t = jax.nn.sigmoid(gate_ref[0].astype(F32)).T
    rows = []
    for h in range(D_HEADS):
        kv, g = divmod(h, grp)
        cols = slice(g * t, (g + 1) * t)
        r = D_BRANCHES * h
        rows.append(g_t[r:r + 1, :] * oc_t[kv][:, cols] + g_t[r + 1:r + 2, :] * os_t[kv][:, cols]
                    + g_t[r + 2:r + 3, :] * ow_t[kv][:, cols])
    o = jnp.concatenate(rows, axis=0).T
    o_ref[0] = (o * _silu(gd_ref[0].astype(F32))).astype(BF16)


def _overlap_t(s):
    ncmp = (s - CMP_LEN) // CMP_STRIDE + 1
    nsel = s // SEL_LEN
    cs = np.arange(ncmp)[None, :] * CMP_STRIDE
    ss = np.arange(nsel)[:, None] * SEL_LEN
    ovl = np.zeros((nsel, s // CMP_STRIDE), np.float32)
    ovl[:, :ncmp] = (cs < ss + SEL_LEN) & (cs + CMP_LEN > ss)
    return ovl


def _nsa(z, cos, sin_s, qn, bounds, kc, vct, ks, vst, kw, vwt, t=256):
    b, s, _ = z.shape
    nseg = s // CMP_STRIDE
    assert nseg == LANES, "compressed-block scores are laid out on one 128-row tile"
    ovlt = _overlap_t(s)
    full = lambda shp: pl.BlockSpec(shp, lambda bi, i: (0,) * len(shp))
    dup = lambda n: pl.BlockSpec((1, 2, n, 128), lambda bi, i: (bi, 0, 0, 0))
    tr = lambda n: pl.BlockSpec((1, 2, VT_ROWS, n), lambda bi, i: (bi, 0, 0, 0))
    return pl.pallas_call(
        _nsa_kernel,
        grid=(b, s // t),
        in_specs=[pl.BlockSpec(memory_space=pltpu.SMEM),
                  pl.BlockSpec((1, t, 512), lambda bi, i: (bi, i, ODD_OFF["qd"][0] // 512)),
                  pl.BlockSpec((1, t, 512), lambda bi, i: (bi, i, ODD_OFF["gd"][0] // 512)),
                  pl.BlockSpec((1, t, 128), lambda bi, i: (bi, i, ODD_OFF["gate"][0] // 128)),
                  pl.BlockSpec((t, 128), lambda bi, i: (i, 0)),
                  pl.BlockSpec((t, 128), lambda bi, i: (i, 0)),
                  full((1, 512)), dup(nseg), tr(nseg), dup(s), tr(s), dup(s), tr(s), full(ovlt.shape)],
        out_specs=pl.BlockSpec((1, t, 512), lambda bi, i: (bi, i, 0)),
        out_shape=jax.ShapeDtypeStruct((b, s, 512), BF16),
        compiler_params=_cparams(("arbitrary", "arbitrary")),
        name="nsa",
    )(bounds, z, z, z, cos, sin_s, jnp.tile(qn, 8).reshape(1, 512), kc, vct, ks, vst, kw, vwt, jnp.asarray(ovlt, BF16))


def _permute_cols(w, order, src):
    cols = []
    for name, width in order:
        o, sw = src[name]
        blk = w[:, o:o + sw]
        if sw < width:
            blk = jnp.pad(blk, ((0, 0), (0, width - sw)))
        cols.append(blk)
    return jnp.concatenate(cols, axis=1).astype(BF16)


def _block_diag_pairs(w):
    z = jnp.zeros((4, LANES, LANES), w.dtype)
    z = z.at[:, 0:64, 0:64].set(w[0::2])
    z = z.at[:, 64:128, 64:128].set(w[1::2])
    return z.astype(BF16)


def _rope_tables(pos):
    half = HEAD_DIM // 2
    inv = ROPE_THETA ** (-jnp.arange(half, dtype=F32) / half)
    ang = pos.astype(F32)[:, None] * inv[None, :]
    cos, sin = jnp.cos(ang), jnp.sin(ang)
    cos_t = jnp.tile(cos, (1, 4))
    sin_t = jnp.tile(jnp.concatenate([-sin, sin], axis=1), (1, 2))
    return cos_t, sin_t


def _expand_compress_w1(w1):
    hdim = w1.shape[1]
    w = w1.reshape(2, CMP_STRIDE, 1, HEAD_DIM, 1, hdim)
    same_head = jnp.eye(2, dtype=w1.dtype).reshape(1, 1, 2, 1, 2, 1)
    return (w * same_head).reshape(2, CMP_STRIDE * 2 * HEAD_DIM, 2 * hdim).astype(BF16)


def _expand_compress_w2(w2):
    hdim, hd = w2.shape
    z = jnp.zeros((2 * hdim, 2 * hd), w2.dtype)
    z = z.at[0:hdim, 0:hd].set(w2)
    z = z.at[hdim:, hd:].set(w2)
    return z.astype(BF16)


def _expand_pe(pe):
    p = pe.reshape(2, CMP_STRIDE, 1, HEAD_DIM)
    return jnp.broadcast_to(p, (2, CMP_STRIDE, 2, HEAD_DIM)).reshape(2, CMP_STRIDE * 2 * HEAD_DIM)


def _even_layer(h, mem, g, mem_g, w_mem_kv, m_qn, m_kn, w_in, w_out, a_qn, a_kn, a_sinks,
                conv_w, conv_b, w_r, b_r, w_i, b_i, lam, cos, sin_s):
    b, s, d = h.shape
    h2 = h.reshape(b * s, d)
    z, ob = _inproj_rglru(h2, g, _permute_cols(w_in, EVEN_ORDER, EVEN_SRC), s, conv_w, conv_b,
                          _block_diag_pairs(w_r), b_r, _block_diag_pairs(w_i), b_i, lam)
    z = z.reshape(b, s, EVEN_COLS)
    oa = _swa(z, cos, sin_s, a_qn, a_kn, a_sinks)
    out = _outproj_mem(h2, oa.reshape(b * s, 512), ob, z.reshape(b * s, EVEN_COLS), EVEN_OFF["qm"][0], EVEN_OFF["gm"][0],
                       s, mem, mem_g, w_mem_kv.astype(BF16), m_qn, m_kn, w_out.astype(BF16))
    return out.reshape(b, s, d)


def _odd_layer(h, mem, g, mem_g, w_mem_kv, m_qn, m_kn, w_in, w_out, c_lb, c_og,
               d_qn, d_kn_cmp, d_kn_slc, d_kn_win, pe_k, pe_v, w1k, w2k, w1v, w2v, cos, sin_s, cosc, sinc):
    b, s, d = h.shape
    h2 = h.reshape(b * s, d)
    z, kcd, vcd, oc = _inproj_hgrn(h2, g, _permute_cols(w_in, ODD_ORDER, ODD_SRC), s, c_lb, c_og)
    z = z.reshape(b, s, ODD_SPLIT[0])
    nseg = s // CMP_STRIDE
    seg = lambda a: a.reshape(b, nseg, CMP_STRIDE * 128)
    kc, vc, ks, vs, kw, vw = _nsa_prep(
        z, seg(kcd), seg(vcd), cos, sin_s, cosc, sinc, d_kn_cmp, d_kn_slc, d_kn_win,
        _expand_pe(pe_k), _expand_pe(pe_v), _expand_compress_w1(w1k), _expand_compress_w2(w2k),
        _expand_compress_w1(w1v), _expand_compress_w2(w2v).T)
    bounds = jnp.stack([_score_bound(d_qn, kn) for kn in (d_kn_cmp, d_kn_slc, d_kn_win)]).astype(F32)
    od = _nsa(z, cos, sin_s, d_qn, bounds, kc, vc, ks, vs, kw, vw)
    out = _outproj_mem(h2, oc, od.reshape(b * s, 512), z.reshape(b * s, ODD_SPLIT[0]), ODD_OFF["qm"][0], ODD_OFF["gm"][0],
                       s, mem, mem_g, w_mem_kv.astype(BF16), m_qn, m_kn, w_out.astype(BF16))
    return out.reshape(b, s, d)


def kernel(x, mem, norm_g, mem_norm_g, mem_w_kv, mem_qn, mem_kn, ev_w_in, ev_w_out, a_qn, a_kn, a_sinks,
           b_conv_w, b_conv_b, b_w_r, b_b_r, b_w_i, b_b_i, b_lambda, od_w_in, od_w_out, c_lb, c_onorm,
           d_qn, d_kn_cmp, d_kn_slc, d_kn_win, d_pe_k, d_pe_v, d_w1k, d_w2k, d_w1v, d_w2v):
    depth = norm_g.shape[0]
    assert depth == 2 and c_lb.shape[0] == 2, "the HGRN2 lower-bound formula in the kernel is written for depth 2"
    s = x.shape[1]
    assert s % 256 == 0 and s >= D_WINDOW
    pos = jnp.arange(s)
    cos, sin_s = _rope_tables(pos)
    nseg = s // CMP_STRIDE
    cmp_end = jnp.minimum(jnp.arange(nseg) * CMP_STRIDE + CMP_LEN - 1, s - 1)
    cosc, sinc = _rope_tables(cmp_end)
    h = _even_layer(x, mem, norm_g[0], mem_norm_g[0], mem_w_kv[0], mem_qn[0], mem_kn[0], ev_w_in[0], ev_w_out[0],
                    a_qn[0], a_kn[0], a_sinks[0], b_conv_w[0], b_conv_b[0], b_w_r[0], b_b_r[0], b_w_i[0], b_b_i[0],
                    b_lambda[0], cos, sin_s)
    h = _odd_layer(h, mem, norm_g[1], mem_norm_g[1], mem_w_kv[1], mem_qn[1], mem_kn[1], od_w_in[0], od_w_out[0],
                   c_lb, c_onorm[0], d_qn[0], d_kn_cmp[0], d_kn_slc[0], d_kn_win[0], d_pe_k[0], d_pe_v[0],
                   d_w1k[0], d_w2k[0], d_w1v[0], d_w2v[0], cos, sin_s, cosc, sinc)
    return h
```

```python
import functools

import numpy as np
import jax
import jax.numpy as jnp
from jax import lax
from jax.experimental import pallas as pl
from jax.experimental.pallas import tpu as pltpu

F32 = jnp.float32
BF16 = jnp.bfloat16

D_MODEL = 1024
N_MEM = 256
HEAD_DIM = 64
ROPE_THETA = 10000.0
EPS = 1e-6
NEG_INF = -1e30
POS_INF = 1e30
MASK_BIG = 1e30
LANES = 128

A_HEADS, A_KV_HEADS, A_WINDOW = 8, 2, 128
B_WIDTH, B_BLOCKS, B_CONV, B_C = 512, 8, 4, 8.0
M_HEADS = 4
C_HEADS, C_HEAD_DIM, C_CHUNK = 4, 128, 64
C_WIDTH = C_HEADS * C_HEAD_DIM
D_HEADS, D_KV_HEADS = 8, 2
CMP_LEN, CMP_STRIDE, CMP_HIDDEN = 32, 16, 128
SEL_LEN, SEL_TOPK = 64, 4
D_WINDOW = 512
D_BRANCHES = 3
SCALE = HEAD_DIM ** -0.5
LOG2E = 1.4426950408889634
Q_SCALE = SCALE * LOG2E

EVEN_ORDER = [("xb", 512), ("gb", 512), ("qa", 512), ("ga", 512), ("qm", 256), ("gm", 256), ("ka", 128), ("va", 128)]
EVEN_FUSED = 1024
EVEN_SRC = {"qa": (0, 512), "ka": (512, 128), "va": (640, 128), "ga": (768, 512), "xb": (1280, 512),
            "gb": (1792, 512), "qm": (2304, 256), "gm": (2560, 256)}
ODD_ORDER = [("qc", 512), ("fc", 512), ("ic", 512), ("gc", 512), ("qd", 512), ("gd", 512), ("qm", 256), ("gm", 256),
             ("ksd", 128), ("vsd", 128), ("kwd", 128), ("vwd", 128), ("gate", 128), ("kcd", 128), ("vcd", 128)]
ODD_FUSED = 2048
ODD_SPLIT = (2176, 128, 128)
ODD_SRC = {"qc": (0, 512), "fc": (512, 512), "ic": (1024, 512), "gc": (1536, 512), "qd": (2048, 512),
           "kcd": (2560, 128), "vcd": (2688, 128), "ksd": (2816, 128), "vsd": (2944, 128), "kwd": (3072, 128),
           "vwd": (3200, 128), "gate": (3328, 24), "gd": (3352, 512), "qm": (3864, 256), "gm": (4120, 256)}

VMEM_LIMIT = 48 * 1024 * 1024


def _offsets(order):
    off, out = 0, {}
    for name, w in order:
        out[name] = (off, w)
        off += w
    return out, off


EVEN_OFF, EVEN_COLS = _offsets(EVEN_ORDER[2:])
ODD_OFF, ODD_COLS = _offsets(ODD_ORDER[4:])


def _cparams(sem):
    return pltpu.CompilerParams(dimension_semantics=sem, vmem_limit_bytes=VMEM_LIMIT)


def _dot(a, b):
    return jnp.dot(a, b, preferred_element_type=F32)


def _dot_nt(a, b):
    return lax.dot_general(a, b, (((1,), (1,)), ((), ())), preferred_element_type=F32)


def _split(x):
    hi = x.astype(BF16)
    lo = (x - hi.astype(F32)).astype(BF16)
    return hi, lo


def _dot_split_rhs(m, x):
    hi, lo = _split(x)
    return _dot(m, hi) + _dot(m, lo)


def _lane(shape):
    return lax.broadcasted_iota(jnp.int32, shape, len(shape) - 1)


def _row(shape):
    return lax.broadcasted_iota(jnp.int32, shape, len(shape) - 2)


def _silu(x):
    return x * jax.nn.sigmoid(x)


def _seg_ones():
    r = lax.broadcasted_iota(jnp.int32, (LANES, LANES), 0) >> 6
    c = lax.broadcasted_iota(jnp.int32, (LANES, LANES), 1) >> 6
    return jnp.where(r == c, 1.0, 0.0).astype(BF16)


def _head_rms(x, gain):
    seg = _seg_ones()
    cols = []
    for c in range(x.shape[1] // LANES):
        xc = x[:, c * LANES:(c + 1) * LANES]
        ms = _dot((xc * xc).astype(BF16), seg) * (1.0 / HEAD_DIM)
        cols.append(xc * lax.rsqrt(ms + EPS))
    y = cols[0] if len(cols) == 1 else jnp.concatenate(cols, axis=1)
    return y * gain


def _rope(x, cos, sin_s):
    first = (_lane((x.shape[0], LANES)) & 63) < 32
    cols = []
    for c in range(x.shape[1] // LANES):
        xc = x[:, c * LANES:(c + 1) * LANES]
        sw = jnp.where(first, pltpu.roll(xc, 96, 1), pltpu.roll(xc, 32, 1))
        cols.append(xc * cos + sw * sin_s)
    return cols[0] if len(cols) == 1 else jnp.concatenate(cols, axis=1)


def _split_kv(k, fill=0.0):
    lo = _lane(k.shape) < 64
    return jnp.where(lo, k, fill), jnp.where(lo, pltpu.roll(k, 64, 1), fill)


def _q_heads(q, fills=None):
    lo = _lane((q.shape[0], LANES)) < 64
    out = []
    for c in range(q.shape[1] // LANES):
        qc = q[:, c * LANES:(c + 1) * LANES]
        fill = 0.0 if fills is None else fills[c]
        out.append(jnp.where(lo, qc, fill).astype(BF16))
        out.append(jnp.where(lo, pltpu.roll(qc, 64, 1), fill).astype(BF16))
    return out


VT_ROWS = HEAD_DIM + 16


def _with_ones_row(vt):
    n = vt.shape[1]
    pad = jnp.where(_row((VT_ROWS - HEAD_DIM, n)) == 0, 1.0, 0.0)
    return jnp.concatenate([vt, pad], axis=0)


MAX_SAFE_BOUND = 60.0


def _score_bound(q_gain, k_gain):
    return 1.02 * HEAD_DIM * SCALE * LOG2E * jnp.max(jnp.abs(q_gain)) * jnp.max(jnp.abs(k_gain))


def _with_fixed_reference(bound, body):
    small = bound <= MAX_SAFE_BOUND

    @pl.when(small)
    def _():
        body(bound)

    @pl.when(jnp.logical_not(small))
    def _():
        body(None)


def _exp_cols(s, extra=None, fixed=None):
    if fixed is not None:
        return jnp.exp2(s - fixed).astype(BF16), fixed
    m = jnp.max(s, axis=0, keepdims=True)
    if extra is not None:
        m = jnp.maximum(m, extra)
    return jnp.exp2(s - m).astype(BF16), m


def _finish(acc, extra_den=None):
    den = acc[HEAD_DIM:HEAD_DIM + 1, :]
    if extra_den is not None:
        den = den + extra_den
    return acc[0:HEAD_DIM, :] * (1.0 / den)


def _online_update(state, s, vt_tile, fixed=None):
    m, acc = state
    if fixed is not None:
        return m, acc + _dot(vt_tile, _exp_cols(s, fixed=fixed)[0])
    p, m_new = _exp_cols(s, m)
    acc = jnp.exp2(m - m_new) * acc + _dot(vt_tile, p)
    return m_new, acc


def _tile4(x):
    return jnp.concatenate([x, x, x, x], axis=1)


def _outproj_mem_kernel(tiles_per_seq, bound_ref, h_ref, a_ref, b_ref, q_ref, gm_ref, mem_ref, mg_ref, wkv_ref,
                        qn_ref, kn_ref, w_ref, o_ref, k_scr, vt_scr, acc_scr):
    @pl.when(pl.program_id(0) % tiles_per_seq == 0)
    def _():
        m = mem_ref[0]
        ms = jnp.mean(m * m, axis=-1, keepdims=True)
        mn = (m * lax.rsqrt(ms + EPS) * mg_ref[...]).astype(BF16)
        kv = _dot(mn, wkv_ref[...])
        km = _head_rms(kv[:, 0:256], kn_ref[...])
        for c in range(2):
            k0, k1 = _split_kv(km[:, c * LANES:(c + 1) * LANES])
            k_scr[2 * c] = k0.astype(BF16)
            k_scr[2 * c + 1] = k1.astype(BF16)
        vt = kv[:, 256:512].T
        for h in range(M_HEADS):
            vt_scr[h] = _with_ones_row(vt[h * HEAD_DIM:(h + 1) * HEAD_DIM, :]).astype(BF16)

    def body(fixed):
        a, b = a_ref[...], b_ref[...]
        pending = list(range(0, D_MODEL, 256))

        def emit():
            if pending:
                c = pending.pop(0)
                acc_scr[:, c:c + 256] = (h_ref[:, c:c + 256] + _dot(a, w_ref[0:512, c:c + 256])
                                         + _dot(b, w_ref[512:1024, c:c + 256]))

        q = _head_rms(q_ref[...].astype(F32), qn_ref[...]) * Q_SCALE
        scores = [_dot_nt(k_scr[h], qh) for h, qh in enumerate(_q_heads(q))]
        rows = []
        for h, sc in enumerate(scores):
            emit()
            rows.append(_finish(_dot(vt_scr[h], _exp_cols(sc, fixed=fixed)[0])))
        om = (jnp.concatenate(rows, axis=0).T * _silu(gm_ref[...].astype(F32))).astype(BF16)
        while pending:
            emit()
        for c in range(0, D_MODEL, 256):
            o_ref[:, c:c + 256] = acc_scr[:, c:c + 256] + _dot(om, w_ref[1024:1280, c:c + 256])

    _with_fixed_reference(bound_ref[0], body)


def _outproj_mem(h2d, oa, ob, z2d, q_off, g_off, seq, mem, mem_g, wkv_bf16, qn, kn, w_bf16, tm=1024):
    n, d = h2d.shape
    full = lambda shp: pl.BlockSpec(shp, lambda i: (0,) * len(shp))
    tps = seq // tm
    return pl.pallas_call(
        functools.partial(_outproj_mem_kernel, tps),
        grid=(n // tm,),
        in_specs=[pl.BlockSpec(memory_space=pltpu.SMEM),
                  pl.BlockSpec((tm, d), lambda i: (i, 0)),
                  pl.BlockSpec((tm, 512), lambda i: (i, 0)),
                  pl.BlockSpec((tm, 512), lambda i: (i, 0)),
                  pl.BlockSpec((tm, 256), lambda i: (i, q_off // 256)),
                  pl.BlockSpec((tm, 256), lambda i: (i, g_off // 256)),
                  pl.BlockSpec((1, N_MEM, D_MODEL), lambda i: (i // tps, 0, 0)),
                  full((1, D_MODEL)), full((D_MODEL, 512)), full((1, 256)), full((1, 256)), full((1280, d))],
        out_specs=pl.BlockSpec((tm, d), lambda i: (i, 0)),
        out_shape=jax.ShapeDtypeStruct((n, d), F32),
        scratch_shapes=[pltpu.VMEM((M_HEADS, N_MEM, LANES), BF16), pltpu.VMEM((M_HEADS, VT_ROWS, N_MEM), BF16),
                        pltpu.VMEM((tm, d), F32)],
        compiler_params=_cparams(("arbitrary",)),
        name="outproj_mem",
    )(_score_bound(qn, kn).reshape(1).astype(F32), h2d, oa, ob, z2d, z2d, mem, mem_g.reshape(1, D_MODEL), wkv_bf16,
      jnp.tile(qn, 4).reshape(1, 256), jnp.tile(kn, 4).reshape(1, 256), w_bf16)


SWA_SUB = 256


def _swa_kernel(bound_ref, *refs):
    _with_fixed_reference(bound_ref[0], functools.partial(_swa_body, refs))


def _swa_body(refs, fixed):
    (q_ref, g_ref, kc_ref, kp_ref, vc_ref, vp_ref, cosc_ref, sinc_ref, cosp_ref, sinp_ref,
     qn_ref, kn_ref, sink_ref, o_ref) = refs
    i = pl.program_id(1)
    t = q_ref.shape[1]
    tp = kp_ref.shape[1]
    grp = A_HEADS // A_KV_HEADS
    q = _rope(_head_rms(q_ref[0].astype(F32), qn_ref[...]), cosc_ref[...], sinc_ref[...]) * Q_SCALE
    qh = _q_heads(q)
    kc = _rope(_head_rms(kc_ref[0].astype(F32), kn_ref[...]), cosc_ref[...], sinc_ref[...])
    kp = _rope(_head_rms(kp_ref[0].astype(F32), kn_ref[...]), cosp_ref[...], sinp_ref[...])
    ks = [x.astype(BF16) for x in _split_kv(jnp.concatenate([kp, kc], axis=0))]
    vt = jnp.concatenate([vp_ref[0].astype(F32).T, vc_ref[0].astype(F32).T], axis=1)
    ts_ = min(SWA_SUB, t)
    krow = _row((tp + ts_, ts_))
    qcol = _lane((tp + ts_, ts_))
    subs = range(t // ts_)
    scores, biases = [], []
    for j in subs:
        ts = i * t + j * ts_ - tp + krow
        rel = i * t + j * ts_ + qcol - ts
        biases.append(_tile4(jnp.where((rel >= 0) & (rel < A_WINDOW) & (ts >= 0), 0.0, NEG_INF)))
        scores.append([_dot_nt(ks[kv][j * ts_:j * ts_ + tp + ts_, :],
                               jnp.concatenate([qh[kv * grp + g][j * ts_:(j + 1) * ts_, :] for g in range(grp)], axis=0))
                       for kv in range(A_KV_HEADS)])
    blocks = []
    for j in subs:
        rows = []
        for kv in range(A_KV_HEADS):
            sink = jnp.concatenate([jnp.broadcast_to(sink_ref[kv * grp + g:kv * grp + g + 1, 0:1], (1, ts_))
                                    for g in range(grp)], axis=1) * LOG2E
            p, m = _exp_cols(scores[j][kv] + biases[j], sink, fixed)
            vt_kv = _with_ones_row(vt[kv * HEAD_DIM:(kv + 1) * HEAD_DIM, j * ts_:j * ts_ + tp + ts_]).astype(BF16)
            o_t = _finish(_dot(vt_kv, p), jnp.exp2(sink - m))
            rows.extend(o_t[:, g * ts_:(g + 1) * ts_] for g in range(grp))
        blocks.append(jnp.concatenate(rows, axis=0))
    o = (blocks[0] if len(blocks) == 1 else jnp.concatenate(blocks, axis=1)).T
    o_ref[0] = (o * _silu(g_ref[0].astype(F32))).astype(BF16)


def _swa(z, cos, sin_s, qn, kn, sinks, t=1024):
    b, s, _ = z.shape
    tp = A_WINDOW
    qo, go = EVEN_OFF["qa"][0] // 512, EVEN_OFF["ga"][0] // 512
    ko, vo = EVEN_OFF["ka"][0] // 128, EVEN_OFF["va"][0] // 128
    prev = lambda i: jnp.maximum(i * (t // tp) - 1, 0)
    bound = jnp.maximum(_score_bound(qn, kn), jnp.max(sinks) * LOG2E).reshape(1).astype(F32)
    return pl.pallas_call(
        _swa_kernel,
        grid=(b, s // t),
        in_specs=[pl.BlockSpec(memory_space=pltpu.SMEM),
                  pl.BlockSpec((1, t, 512), lambda bi, i: (bi, i, qo)),
                  pl.BlockSpec((1, t, 512), lambda bi, i: (bi, i, go)),
                  pl.BlockSpec((1, t, 128), lambda bi, i: (bi, i, ko)),
                  pl.BlockSpec((1, tp, 128), lambda bi, i: (bi, prev(i), ko)),
                  pl.BlockSpec((1, t, 128), lambda bi, i: (bi, i, vo)),
                  pl.BlockSpec((1, tp, 128), lambda bi, i: (bi, prev(i), vo)),
                  pl.BlockSpec((t, 128), lambda bi, i: (i, 0)),
                  pl.BlockSpec((t, 128), lambda bi, i: (i, 0)),
                  pl.BlockSpec((tp, 128), lambda bi, i: (prev(i), 0)),
                  pl.BlockSpec((tp, 128), lambda bi, i: (prev(i), 0)),
                  pl.BlockSpec((1, 512), lambda bi, i: (0, 0)),
                  pl.BlockSpec((1, 128), lambda bi, i: (0, 0)),
                  pl.BlockSpec((A_HEADS, 128), lambda bi, i: (0, 0))],
        out_specs=pl.BlockSpec((1, t, 512), lambda bi, i: (bi, i, 0)),
        out_shape=jax.ShapeDtypeStruct((b, s, 512), BF16),
        compiler_params=_cparams(("arbitrary", "arbitrary")),
        name="swa",
    )(bound, z, z, z, z, z, z, cos, sin_s, cos, sin_s, jnp.tile(qn, 8).reshape(1, 512), jnp.tile(kn, 2).reshape(1, 128),
      jnp.broadcast_to(sinks.reshape(A_HEADS, 1), (A_HEADS, 128)))


def _rglru_half(cols, xb, gb, emit, cw_ref, cb_ref, wr_ref, br_ref, wi_ref, bi_ref, lam_ref, xbuf, hcar, a_scr, u_scr, h_scr):
    t, wdt = xb.shape
    xbuf[8:t + 8, cols] = xb
    xc = cb_ref[:, cols] + cw_ref[0:1, cols] * xbuf[5:t + 5, cols]
    for j in range(1, B_CONV):
        xc = xc + cw_ref[j:j + 1, cols] * xbuf[5 + j:t + 5 + j, cols]
    xbuf[0:8, cols] = xbuf[t:t + 8, cols]
    emit()

    r_cols, i_cols = [], []
    for c in range(wdt // LANES):
        blk = cols.start // LANES + c
        xcc = xc[:, c * LANES:(c + 1) * LANES].astype(BF16)
        r_cols.append(_dot(xcc, wr_ref[blk]))
        i_cols.append(_dot(xcc, wi_ref[blk]))
    emit()
    r = jax.nn.sigmoid(jnp.concatenate(r_cols, axis=1) + br_ref[:, cols])
    ig = jax.nn.sigmoid(jnp.concatenate(i_cols, axis=1) + bi_ref[:, cols])
    nl = -lam_ref[:, cols]
    softplus = jnp.maximum(nl, 0.0) + jnp.log(1.0 + jnp.exp(-jnp.abs(nl)))
    log_a = -B_C * r * softplus
    a = jnp.exp(log_a)
    om = 1.0 - a * a
    u = om * lax.rsqrt(jnp.maximum(om, 1e-30)) * (ig * xc)
    emit()

    a = a.reshape(t // 8, 8, wdt)
    u = u.reshape(t // 8, 8, wdt)
    r8 = lax.broadcasted_iota(jnp.int32, a.shape, 1)
    for d in (1, 2, 4):
        a_sh = pltpu.roll(a, d, 1)
        u_sh = pltpu.roll(u, d, 1)
        m = r8 >= d
        u = jnp.where(m, a * u_sh + u, u)
        a = jnp.where(m, a * a_sh, a)
    a_scr[:, cols] = a.reshape(t, wdt)
    u_scr[:, cols] = u.reshape(t, wdt)
    emit()

    def body(j, h):
        off = pl.multiple_of(j * 8, 8)
        hh = a_scr[pl.ds(off, 8), cols] * h + u_scr[pl.ds(off, 8), cols]
        h_scr[pl.ds(off, 8), cols] = hh
        return hh[7:8, :]

    h_last = lax.fori_loop(0, t // 8, body, hcar[0:1, cols], unroll=True)
    hcar[0:1, cols] = h_last
    emit()
    return h_scr[:, cols] * _silu(gb)


def _inproj_rglru_kernel(tiles_per_seq, x_ref, g_ref, w_ref, cw_ref, cb_ref, wr_ref, br_ref, wi_ref, bi_ref, lam_ref,
                         z_ref, ob_ref, xbuf, hcar, a_scr, u_scr, h_scr):
    @pl.when(pl.program_id(0) % tiles_per_seq == 0)
    def _():
        xbuf[0:8, :] = jnp.zeros((8, B_WIDTH), F32)
        hcar[...] = jnp.zeros((8, B_WIDTH), F32)

    x = x_ref[...]
    ms = jnp.mean(x * x, axis=-1, keepdims=True)
    xn = (x * lax.rsqrt(ms + EPS) * g_ref[...]).astype(BF16)
    half = B_WIDTH // 2
    proj = {}

    def project(name, hp):
        base = (B_WIDTH if name == "gb" else 0) + hp * half
        proj[name, hp] = _dot(xn, w_ref[:, base:base + half])

    def store_z(c):
        z_ref[:, c:c + 256] = _dot(xn, w_ref[:, EVEN_FUSED + c:EVEN_FUSED + c + 256]).astype(BF16)

    pending = [functools.partial(store_z, c) for c in range(0, z_ref.shape[-1], 256)]
    pending.insert(1, functools.partial(project, "xb", 1))
    pending.insert(3, functools.partial(project, "gb", 1))

    def emit():
        if pending:
            pending.pop(0)()

    project("xb", 0)
    project("gb", 0)
    for hp in range(2):
        cols = slice(hp * half, (hp + 1) * half)
        ob_ref[:, cols] = _rglru_half(cols, proj["xb", hp], proj["gb", hp], emit, cw_ref, cb_ref, wr_ref, br_ref,
                                      wi_ref, bi_ref, lam_ref, xbuf, hcar, a_scr, u_scr, h_scr).astype(BF16)
    while pending:
        emit()


def _inproj_rglru(x2d, gain, w_bf16, seq, conv_w, conv_b, wr_bd, b_r, wi_bd, b_i, lam, tm=1024):
    n, d = x2d.shape
    nc = w_bf16.shape[1]
    row = lambda v: v.reshape(1, B_WIDTH)
    full = lambda shp: pl.BlockSpec(shp, lambda i: (0,) * len(shp))
    return pl.pallas_call(
        functools.partial(_inproj_rglru_kernel, seq // tm),
        grid=(n // tm,),
        in_specs=[pl.BlockSpec((tm, d), lambda i: (i, 0)), full((1, d)), full((d, nc)),
                  full((B_CONV, B_WIDTH)), full((1, B_WIDTH)), full((4, LANES, LANES)), full((1, B_WIDTH)),
                  full((4, LANES, LANES)), full((1, B_WIDTH)), full((1, B_WIDTH))],
        out_specs=[pl.BlockSpec((tm, nc - EVEN_FUSED), lambda i: (i, 0)), pl.BlockSpec((tm, B_WIDTH), lambda i: (i, 0))],
        out_shape=[jax.ShapeDtypeStruct((n, nc - EVEN_FUSED), BF16), jax.ShapeDtypeStruct((n, B_WIDTH), BF16)],
        scratch_shapes=[pltpu.VMEM((tm + 8, B_WIDTH), F32), pltpu.VMEM((8, B_WIDTH), F32),
                        pltpu.VMEM((tm, B_WIDTH), F32), pltpu.VMEM((tm, B_WIDTH), F32), pltpu.VMEM((tm, B_WIDTH), F32)],
        compiler_params=_cparams(("arbitrary",)),
        name="inproj_rglru",
    )(x2d, gain.reshape(1, d), w_bf16, conv_w, row(conv_b), wr_bd, row(b_r), wi_bd, row(b_i), row(lam))


def _hgrn_consts():
    c = C_CHUNK
    t = np.arange(c)[:, None]
    s = np.arange(c)[None, :]
    masks = []
    hs = c // 2
    while hs >= 1:
        same = (t // (2 * hs)) == (s // (2 * hs))
        masks.append(same & ((t // hs) % 2 == 1) & ((s // hs) % 2 == 0))
        hs //= 2
    pairs = [np.concatenate(masks[n:n + 2], axis=1) for n in range(0, len(masks), 2)]
    return (s <= t).astype(np.float32), np.stack(pairs).astype(np.float32)


def _split_points(b, level):
    c, wdt = b.shape
    hs = c >> (level + 1)
    blk = 2 * hs
    if blk >= 8:
        return jnp.concatenate([jnp.broadcast_to(b[m * blk + hs - 1:m * blk + hs, :], (blk, wdt))
                                for m in range(c // blk)], axis=0)
    b3 = b.reshape(c // 8, 8, wdt)
    sub = lax.broadcasted_iota(jnp.int32, b3.shape, 1)
    if hs == 2:
        out = jnp.where(sub < 4, jnp.broadcast_to(b3[:, 1:2, :], b3.shape), jnp.broadcast_to(b3[:, 5:6, :], b3.shape))
    else:
        out = jnp.where((sub & 1) == 0, b3, pltpu.roll(b3, 1, 1))
    return out.reshape(c, wdt)


def _hgrn_chunk(lb, q_in, f_in, v_in, g_in, og_ref, mst_ref, msk_ref, st_scr, emit, emit2):
    c = C_CHUNK
    npair = msk_ref.shape[0]
    zeros = jnp.zeros((c, LANES), BF16)
    f = lb + (1.0 - lb) * jax.nn.sigmoid(f_in)
    g = jnp.log2(f)
    g_hi, g_lo = _split(g)
    b_all = _dot(mst_ref[...], g_hi) + _dot(mst_ref[...], g_lo)
    emit()
    kk_all = 1.0 - f
    qf_all = _silu(q_in)
    v_all = v_in
    kk_bf, qf_bf = kk_all.astype(BF16), qf_all.astype(BF16)
    heads = range(C_HEADS)
    cols = [slice(h * LANES, (h + 1) * LANES) for h in heads]
    sts = [st_scr[h] for h in heads]
    os = [_dot_nt((qf_all[:, cols[h]] * jnp.exp2(b_all[:, cols[h]])).astype(BF16), sts[h].astype(BF16)) for h in heads]
    att = [[] for _ in heads]
    bms = [_split_points(b_all, l) for l in range(2 * npair)]
    for n in range(npair):
        lhs, rhs = [], []
        for h in heads:
            b = b_all[:, cols[h]]
            qt, kt = [], []
            for l in (2 * n, 2 * n + 1):
                bm = bms[l][:, cols[h]]
                qt.append(qf_bf[:, cols[h]] * jnp.exp2(jnp.minimum(b - bm, 1.0)).astype(BF16))
                kt.append(kk_bf[:, cols[h]] * jnp.exp2(jnp.minimum(bm - b, 1.0)).astype(BF16))
            lhs.append(jnp.concatenate(qt, axis=1))
            rhs.append(jnp.concatenate([jnp.concatenate([kt[0], zeros], axis=1),
                                        jnp.concatenate([zeros, kt[1]], axis=1)], axis=0))
        prods = [_dot_nt(lhs[h], rhs[h]) for h in heads]
        if n == 1:
            emit2()
        for h in heads:
            att[h].append(jnp.where(msk_ref[n] > 0.5, prods[h], 0.0).astype(BF16))
    vbs = [v_all[:, cols[h]].astype(BF16) for h in heads]
    intra = [_dot(jnp.concatenate(att[h], axis=1), jnp.concatenate([vbs[h]] * (2 * npair), axis=0)) for h in heads]
    b_last = b_all[c - 1:c, :]
    kd = (kk_all * jnp.exp2(b_last - b_all)).astype(BF16)
    upd = [_dot(v_all[:, cols[h]].T.astype(BF16), kd[:, cols[h]]) for h in heads]
    decay = jnp.exp2(b_last)
    diag = qf_all * kk_all
    outs = []
    for h in heads:
        st_scr[h] = sts[h] * decay[:, cols[h]] + upd[h]
        o = os[h] + intra[h] + jnp.sum(diag[:, cols[h]], axis=-1, keepdims=True) * v_all[:, cols[h]]
        ms = jnp.mean(o * o, axis=-1, keepdims=True)
        outs.append(o * lax.rsqrt(ms + EPS) * og_ref[...])
    return jnp.concatenate(outs, axis=1) * _silu(g_in)


def _inproj_hgrn_kernel(tiles_per_seq, x_ref, g_ref, w_ref, lb_ref, og_ref, mst_ref, msk_ref,
                        z_ref, kc_ref, vc_ref, oc_ref, st_scr):
    @pl.when(pl.program_id(0) % tiles_per_seq == 0)
    def _():
        st_scr[...] = jnp.zeros(st_scr.shape, F32)

    x = x_ref[...]
    ms = jnp.mean(x * x, axis=-1, keepdims=True)
    xn = (x * lax.rsqrt(ms + EPS) * g_ref[...]).astype(BF16)
    rows_half = x.shape[0] // 2
    own = {}

    def project_own(k, half):
        r = slice(half * rows_half, (half + 1) * rows_half)
        own[k, half] = _dot(xn[r, :], w_ref[:, k * C_WIDTH:(k + 1) * C_WIDTH])

    o_refs = (z_ref, kc_ref, vc_ref)
    starts = np.cumsum([0] + [o.shape[-1] for o in o_refs])

    def project_rest(c):
        w = min(256, int(starts[-1]) - c)
        r = _dot(xn, w_ref[:, ODD_FUSED + c:ODD_FUSED + c + w]).astype(BF16)
        for o_ref, lo, hi in zip(o_refs, starts[:-1], starts[1:]):
            a, b = max(c, int(lo)), min(c + w, int(hi))
            if a < b:
                o_ref[:, a - int(lo):b - int(lo)] = r[:, a - c:b - c]

    pending = [functools.partial(project_own, k, 1) for k in range(4)]
    pending += [functools.partial(project_rest, c) for c in range(0, int(starts[-1]), 256)]

    def emit():
        if pending:
            pending.pop(0)()

    for k in range(4):
        project_own(k, 0)

    p = lb_ref[...]
    pm = jnp.maximum(p[0:1, :], p[1:2, :])
    e0, e1 = jnp.exp(p[0:1, :] - pm), jnp.exp(p[1:2, :] - pm)
    lb = e1 / (e0 + e1)
    nchunk = x.shape[0] // C_CHUNK
    for ch in range(nchunk):
        half, sub = divmod(ch, nchunk // 2)
        rows = slice(ch * C_CHUNK, (ch + 1) * C_CHUNK)
        part = slice(sub * C_CHUNK, (sub + 1) * C_CHUNK)
        q_in, f_in, v_in, g_in = (own[k, half][part, :] for k in range(4))
        second = emit if ch < 6 else (lambda: None)
        oc_ref[rows, :] = _hgrn_chunk(lb, q_in, f_in, v_in, g_in, og_ref, mst_ref, msk_ref, st_scr, emit,
                                      second).astype(BF16)
    while pending:
        emit()


def _inproj_hgrn(x2d, gain, w_bf16, seq, c_lb, c_og, tm=512):
    n, d = x2d.shape
    nc = w_bf16.shape[1]
    assert ODD_FUSED + sum(ODD_SPLIT) == nc
    mst, msk = _hgrn_consts()
    full = lambda shp: pl.BlockSpec(shp, lambda i: (0,) * len(shp))
    widths = ODD_SPLIT + (C_WIDTH,)
    return pl.pallas_call(
        functools.partial(_inproj_hgrn_kernel, seq // tm),
        grid=(n // tm,),
        in_specs=[pl.BlockSpec((tm, d), lambda i: (i, 0)), full((1, d)), full((d, nc)),
                  full(c_lb.shape), full((1, C_HEAD_DIM)), full(mst.shape), full(msk.shape)],
        out_specs=[pl.BlockSpec((tm, w), lambda i: (i, 0)) for w in widths],
        out_shape=[jax.ShapeDtypeStruct((n, w), BF16) for w in widths],
        scratch_shapes=[pltpu.VMEM((C_HEADS, C_HEAD_DIM, C_HEAD_DIM), F32)],
        compiler_params=_cparams(("arbitrary",)),
        name="inproj_hgrn",
    )(x2d, gain.reshape(1, d), w_bf16, c_lb, c_og.reshape(1, C_HEAD_DIM), jnp.asarray(mst, BF16), jnp.asarray(msk, F32))


def _nsa_prep_kernel(kcr_ref, vcr_ref, ks_ref, vs_ref, kw_ref, vw_ref, cos_ref, sin_ref, cosc_ref, sinc_ref,
                     kncmp_ref, knslc_ref, knwin_ref, pek_ref, pev_ref, w1k_ref, w2k_ref, w1v_ref, w2vt_ref,
                     kc_o, vct_o, ks_o, vst_o, kw_o, vwt_o):
    def hidden(xr_ref, pe_ref, w1_ref):
        xr = xr_ref[0].astype(F32)
        top = _dot((xr + pe_ref[0:1, :]).astype(BF16), w1_ref[0])
        bot = _dot((xr + pe_ref[1:2, :]).astype(BF16), w1_ref[1])
        pre = top + pltpu.roll(bot, bot.shape[0] - 1, 0)
        return _silu(pre).astype(BF16)

    kc = _dot(hidden(kcr_ref, pek_ref, w1k_ref), w2k_ref[...])
    kc = _rope(_head_rms(kc, kncmp_ref[...]), cosc_ref[...], sinc_ref[...])
    d0, d1 = _split_kv(kc)
    kc_o[0, 0] = d0.astype(BF16)
    kc_o[0, 1] = d1.astype(BF16)
    vct = _dot_nt(w2vt_ref[...], hidden(vcr_ref, pev_ref, w1v_ref))
    for kv in range(D_KV_HEADS):
        vct_o[0, kv] = _with_ones_row(vct[kv * HEAD_DIM:(kv + 1) * HEAD_DIM, :]).astype(BF16)

    s = ks_ref.shape[1]
    step = 256
    for r0 in range(0, s, step):
        rows = slice(r0, r0 + step)
        cos, sin = cos_ref[rows, :], sin_ref[rows, :]
        ks = _rope(_head_rms(ks_ref[0, rows, :].astype(F32), knslc_ref[...]), cos, sin)
        kw = _rope(_head_rms(kw_ref[0, rows, :].astype(F32), knwin_ref[...]), cos, sin)
        blk = (r0 + _row((step, LANES))) // SEL_LEN
        onehot = jnp.where(_lane((step, LANES)) - HEAD_DIM == blk, MASK_BIG, 0.0)
        for src, dst, fill in ((ks, ks_o, onehot), (kw, kw_o, 0.0)):
            d0, d1 = _split_kv(src, fill)
            dst[0, 0, rows, :] = d0.astype(BF16)
            dst[0, 1, rows, :] = d1.astype(BF16)
        for src, dst in ((vs_ref, vst_o), (vw_ref, vwt_o)):
            vt = src[0, rows, :].astype(F32).T
            for kv in range(D_KV_HEADS):
                dst[0, kv, :, rows] = _with_ones_row(vt[kv * HEAD_DIM:(kv + 1) * HEAD_DIM, :]).astype(BF16)


def _nsa_prep(z, kcr, vcr, cos, sin_s, cosc, sinc, kn_cmp, kn_slc, kn_win, pek, pev, w1k, w2k, w1v, w2v):
    b, s, _ = z.shape
    nseg = s // CMP_STRIDE
    zb = lambda name: pl.BlockSpec((1, s, 128), lambda bi, o=ODD_OFF[name][0] // 128: (bi, 0, o))
    full = lambda shp: pl.BlockSpec(shp, lambda bi: (0,) * len(shp))
    seg = pl.BlockSpec((1, nseg, 2048), lambda bi: (bi, 0, 0))
    g2 = lambda g: jnp.tile(g, 2).reshape(1, 128)
    dup = lambda n: (jax.ShapeDtypeStruct((b, 2, n, 128), BF16), pl.BlockSpec((1, 2, n, 128), lambda bi: (bi, 0, 0, 0)))
    tr = lambda n: (jax.ShapeDtypeStruct((b, 2, VT_ROWS, n), BF16),
                    pl.BlockSpec((1, 2, VT_ROWS, n), lambda bi: (bi, 0, 0, 0)))
    outs = [dup(nseg), tr(nseg), dup(s), tr(s), dup(s), tr(s)]
    return pl.pallas_call(
        _nsa_prep_kernel,
        grid=(b,),
        in_specs=[seg, seg, zb("ksd"), zb("vsd"), zb("kwd"), zb("vwd"),
                  full((s, 128)), full((s, 128)), full((nseg, 128)), full((nseg, 128)),
                  full((1, 128)), full((1, 128)), full((1, 128)), full((2, 2048)), full((2, 2048)),
                  full((2, 2048, 256)), full((256, 128)), full((2, 2048, 256)), full((128, 256))],
        out_specs=[o[1] for o in outs],
        out_shape=[o[0] for o in outs],
        compiler_params=_cparams(("arbitrary",)),
        name="nsa_prep",
    )(kcr, vcr, z, z, z, z, cos, sin_s, cosc, sinc, g2(kn_cmp), g2(kn_slc), g2(kn_win), pek, pev, w1k, w2k, w1v, w2v)


SEL_CHUNK = 512
SEL_PIECE = 512


def _nsa_kernel(bound_ref, *refs):
    small = jnp.maximum(jnp.maximum(bound_ref[0], bound_ref[1]), bound_ref[2]) <= MAX_SAFE_BOUND

    @pl.when(small)
    def _():
        _nsa_body((bound_ref[0], bound_ref[1], bound_ref[2]), *refs)

    @pl.when(jnp.logical_not(small))
    def _():
        _nsa_body((None, None, None), *refs)


def _nsa_body(fixed, q_ref, gd_ref, gate_ref, cos_ref, sin_ref, qn_ref, kc_ref, vct_ref, ks_ref, vst_ref, kw_ref, vwt_ref,
              ovlt_ref, o_ref):
    fix_c, fix_s, fix_w = fixed
    i = pl.program_id(1)
    t = q_ref.shape[1]
    grp = D_HEADS // D_KV_HEADS
    w = grp * t
    kvs = range(D_KV_HEADS)
    q = _rope(_head_rms(q_ref[0].astype(F32), qn_ref[...]), cos_ref[...], sin_ref[...]) * Q_SCALE
    stack = lambda heads: [jnp.concatenate(heads[kv * grp:(kv + 1) * grp], axis=0) for kv in kvs]
    qs = stack(_q_heads(q))
    tq1 = i * t + _lane((1, t))

    nwin = D_WINDOW + t
    win0 = pl.multiple_of(jnp.maximum(i * t + t - nwin, 0), t)
    s_cmp = [_dot_nt(kc_ref[0, kv], qs[kv]) for kv in kvs]
    s_win = [_dot_nt(kw_ref[0, kv, pl.ds(win0, nwin), :], qs[kv]) for kv in kvs]

    ncmp = kc_ref.shape[2] - 1
    nsel = ovlt_ref.shape[0]
    crow = _row((kc_ref.shape[2], t))
    bias_c = _tile4(jnp.where((crow * CMP_STRIDE + (CMP_LEN - 1) <= tq1) & (crow < ncmp), 0.0, NEG_INF))
    row_ok = _tile4(tq1 >= CMP_LEN - 1)
    es = [_exp_cols(s_cmp[kv] + bias_c, fixed=fix_c)[0] for kv in kvs]
    accs = [_dot(vct_ref[0, kv], es[kv]) for kv in kvs]
    invs = [jnp.where(row_ok, 1.0 / accs[kv][HEAD_DIM:HEAD_DIM + 1, :], 0.0) for kv in kvs]
    oc_t = [accs[kv][0:HEAD_DIM, :] * invs[kv] for kv in kvs]
    psums = []
    for kv in kvs:
        p = es[kv].astype(F32) * invs[kv]
        psums.append(p[:, 0:t] + p[:, t:2 * t] + p[:, 2 * t:3 * t] + p[:, 3 * t:4 * t])
    imps = [_dot_split_rhs(ovlt_ref[...], psums[kv]) for kv in kvs]

    rel = tq1 - (win0 + _row((nwin, t)))
    bias_w = _tile4(jnp.where((rel >= 0) & (rel < D_WINDOW), 0.0, NEG_INF))
    ow_t = [_finish(_dot(vwt_ref[0, kv, :, pl.ds(win0, nwin)], _exp_cols(s_win[kv] + bias_w, fixed=fix_w)[0]))
            for kv in kvs]

    jrow = _row((nsel, t))
    jrow_f = jrow.astype(F32)
    cur = (i * t + _lane((nsel, t))) // SEL_LEN
    forced = (jrow == 0) | (jrow == cur)
    fills = []
    for kv in kvs:
        score = jnp.where(forced, POS_INF, jnp.where(jrow <= cur, imps[kv], NEG_INF))
        chosen = jnp.zeros((nsel, t), F32)
        for _ in range(min(SEL_TOPK, nsel)):
            mx = jnp.max(score, axis=0, keepdims=True)
            first = jnp.min(jnp.where(score == mx, jrow_f, 1e9), axis=0, keepdims=True)
            hit = jrow_f == first
            chosen = jnp.where(hit, 1.0, chosen)
            score = jnp.where(hit, -3e38, score)
        frame = jnp.concatenate([jnp.zeros((HEAD_DIM, t), F32), chosen - 1.0,
                                 jnp.zeros((LANES - HEAD_DIM - nsel, t), F32)], axis=0)
        fills.extend([frame.T] * (grp // 2))

    init = (jnp.full((1, w), NEG_INF, F32), jnp.zeros((VT_ROWS, w), F32))
    pc = SEL_PIECE

    def sweep(states, k_ref, vt_ref, queries, offs, biases, n):
        scores = [[_dot_nt(k_ref[0, kv, pl.ds(off, n), :], queries[kv]) for kv in kvs] for off in offs]
        for off, sc, bias in zip(offs, scores, biases):
            states = tuple(_online_update(states[kv], sc[kv] if bias is None else sc[kv] + bias,
                                          vt_ref[0, kv, :, pl.ds(off, n)], fix_s) for kv in kvs)
        return states

    qsel = stack(_q_heads(q, fills))
    ch = SEL_CHUNK
    own = pl.multiple_of(((i * t) // ch) * ch, ch)
    st = sweep((init, init), ks_ref, vst_ref, qsel, [own],
               [_tile4(jnp.where(own + _row((ch, t)) <= tq1, 0.0, NEG_INF))], ch)

    def sel_chunk(c, states):
        base = pl.multiple_of(c * ch, ch)
        return sweep(states, ks_ref, vst_ref, qsel, [base + n * pc for n in range(ch // pc)], [None] * (ch // pc), pc)

    st = lax.fori_loop(0, (i * t) // ch, sel_chunk, st)
    os_t = [_finish(acc) for (_, acc) in st]

    g_t = jax.nn.sigmoid(gate_ref[0].astype(F32)).T
    rows = []
    for h in range(D_HEADS):
        kv, g = divmod(h, grp)
        cols = slice(g * t, (g + 1) * t)
        r = D_BRANCHES * h
        rows.append(g_t[r:r + 1, :] * oc_t[kv][:, cols] + g_t[r + 1:r + 2, :] * os_t[kv][:, cols]
                    + g_t[r + 2:r + 3, :] * ow_t[kv][:, cols])
    o = jnp.concatenate(rows, axis=0).T
    o_ref[0] = (o * _silu(gd_ref[0].astype(F32))).astype(BF16)


def _overlap_t(s):
    ncmp = (s - CMP_LEN) // CMP_STRIDE + 1
    nsel = s // SEL_LEN
    cs = np.arange(ncmp)[None, :] * CMP_STRIDE
    ss = np.arange(nsel)[:, None] * SEL_LEN
    ovl = np.zeros((nsel, s // CMP_STRIDE), np.float32)
    ovl[:, :ncmp] = (cs < ss + SEL_LEN) & (cs + CMP_LEN > ss)
    return ovl


def _nsa(z, cos, sin_s, qn, bounds, kc, vct, ks, vst, kw, vwt, t=256):
    b, s, _ = z.shape
    nseg = s // CMP_STRIDE
    assert nseg == LANES, "compressed-block scores are laid out on one 128-row tile"
    ovlt = _overlap_t(s)
    full = lambda shp: pl.BlockSpec(shp, lambda bi, i: (0,) * len(shp))
    dup = lambda n: pl.BlockSpec((1, 2, n, 128), lambda bi, i: (bi, 0, 0, 0))
    tr = lambda n: pl.BlockSpec((1, 2, VT_ROWS, n), lambda bi, i: (bi, 0, 0, 0))
    return pl.pallas_call(
        _nsa_kernel,
        grid=(b, s // t),
        in_specs=[pl.BlockSpec(memory_space=pltpu.SMEM),
                  pl.BlockSpec((1, t, 512), lambda bi, i: (bi, i, ODD_OFF["qd"][0] // 512)),
                  pl.BlockSpec((1, t, 512), lambda bi, i: (bi, i, ODD_OFF["gd"][0] // 512)),
                  pl.BlockSpec((1, t, 128), lambda bi, i: (bi, i, ODD_OFF["gate"][0] // 128)),
                  pl.BlockSpec((t, 128), lambda bi, i: (i, 0)),
                  pl.BlockSpec((t, 128), lambda bi, i: (i, 0)),
                  full((1, 512)), dup(nseg), tr(nseg), dup(s), tr(s), dup(s), tr(s), full(ovlt.shape)],
        out_specs=pl.BlockSpec((1, t, 512), lambda bi, i: (bi, i, 0)),
        out_shape=jax.ShapeDtypeStruct((b, s, 512), BF16),
        compiler_params=_cparams(("arbitrary", "arbitrary")),
        name="nsa",
    )(bounds, z, z, z, cos, sin_s, jnp.tile(qn, 8).reshape(1, 512), kc, vct, ks, vst, kw, vwt, jnp.asarray(ovlt, BF16))


def _permute_cols(w, order, src):
    cols = []
    for name, width in order:
        o, sw = src[name]
        blk = w[:, o:o + sw]
        if sw < width:
            blk = jnp.pad(blk, ((0, 0), (0, width - sw)))
        cols.append(blk)
    return jnp.concatenate(cols, axis=1).astype(BF16)


def _block_diag_pairs(w):
    z = jnp.zeros((4, LANES, LANES), w.dtype)
    z = z.at[:, 0:64, 0:64].set(w[0::2])
    z = z.at[:, 64:128, 64:128].set(w[1::2])
    return z.astype(BF16)


def _rope_tables(pos):
    half = HEAD_DIM // 2
    inv = ROPE_THETA ** (-jnp.arange(half, dtype=F32) / half)
    ang = pos.astype(F32)[:, None] * inv[None, :]
    cos, sin = jnp.cos(ang), jnp.sin(ang)
    cos_t = jnp.tile(cos, (1, 4))
    sin_t = jnp.tile(jnp.concatenate([-sin, sin], axis=1), (1, 2))
    return cos_t, sin_t


def _expand_compress_w1(w1):
    hdim = w1.shape[1]
    w = w1.reshape(2, CMP_STRIDE, 1, HEAD_DIM, 1, hdim)
    same_head = jnp.eye(2, dtype=w1.dtype).reshape(1, 1, 2, 1, 2, 1)
    return (w * same_head).reshape(2, CMP_STRIDE * 2 * HEAD_DIM, 2 * hdim).astype(BF16)


def _expand_compress_w2(w2):
    hdim, hd = w2.shape
    z = jnp.zeros((2 * hdim, 2 * hd), w2.dtype)
    z = z.at[0:hdim, 0:hd].set(w2)
    z = z.at[hdim:, hd:].set(w2)
    return z.astype(BF16)


def _expand_pe(pe):
    p = pe.reshape(2, CMP_STRIDE, 1, HEAD_DIM)
    return jnp.broadcast_to(p, (2, CMP_STRIDE, 2, HEAD_DIM)).reshape(2, CMP_STRIDE * 2 * HEAD_DIM)


def _even_layer(h, mem, g, mem_g, w_mem_kv, m_qn, m_kn, w_in, w_out, a_qn, a_kn, a_sinks,
                conv_w, conv_b, w_r, b_r, w_i, b_i, lam, cos, sin_s):
    b, s, d = h.shape
    h2 = h.reshape(b * s, d)
    z, ob = _inproj_rglru(h2, g, _permute_cols(w_in, EVEN_ORDER, EVEN_SRC), s, conv_w, conv_b,
                          _block_diag_pairs(w_r), b_r, _block_diag_pairs(w_i), b_i, lam)
    z = z.reshape(b, s, EVEN_COLS)
    oa = _swa(z, cos, sin_s, a_qn, a_kn, a_sinks)
    out = _outproj_mem(h2, oa.reshape(b * s, 512), ob, z.reshape(b * s, EVEN_COLS), EVEN_OFF["qm"][0], EVEN_OFF["gm"][0],
                       s, mem, mem_g, w_mem_kv.astype(BF16), m_qn, m_kn, w_out.astype(BF16))
    return out.reshape(b, s, d)


def _odd_layer(h, mem, g, mem_g, w_mem_kv, m_qn, m_kn, w_in, w_out, c_lb, c_og,
               d_qn, d_kn_cmp, d_kn_slc, d_kn_win, pe_k, pe_v, w1k, w2k, w1v, w2v, cos, sin_s, cosc, sinc):
    b, s, d = h.shape
    h2 = h.reshape(b * s, d)
    z, kcd, vcd, oc = _inproj_hgrn(h2, g, _permute_cols(w_in, ODD_ORDER, ODD_SRC), s, c_lb, c_og)
    z = z.reshape(b, s, ODD_SPLIT[0])
    nseg = s // CMP_STRIDE
    seg = lambda a: a.reshape(b, nseg, CMP_STRIDE * 128)
    kc, vc, ks, vs, kw, vw = _nsa_prep(
        z, seg(kcd), seg(vcd), cos, sin_s, cosc, sinc, d_kn_cmp, d_kn_slc, d_kn_win,
        _expand_pe(pe_k), _expand_pe(pe_v), _expand_compress_w1(w1k), _expand_compress_w2(w2k),
        _expand_compress_w1(w1v), _expand_compress_w2(w2v).T)
    bounds = jnp.stack([_score_bound(d_qn, kn) for kn in (d_kn_cmp, d_kn_slc, d_kn_win)]).astype(F32)
    od = _nsa(z, cos, sin_s, d_qn, bounds, kc, vc, ks, vs, kw, vw)
    out = _outproj_mem(h2, oc, od.reshape(b * s, 512), z.reshape(b * s, ODD_SPLIT[0]), ODD_OFF["qm"][0], ODD_OFF["gm"][0],
                       s, mem, mem_g, w_mem_kv.astype(BF16), m_qn, m_kn, w_out.astype(BF16))
    return out.reshape(b, s, d)


def kernel(x, mem, norm_g, mem_norm_g, mem_w_kv, mem_qn, mem_kn, ev_w_in, ev_w_out, a_qn, a_kn, a_sinks,
           b_conv_w, b_conv_b, b_w_r, b_b_r, b_w_i, b_b_i, b_lambda, od_w_in, od_w_out, c_lb, c_onorm,
           d_qn, d_kn_cmp, d_kn_slc, d_kn_win, d_pe_k, d_pe_v, d_w1k, d_w2k, d_w1v, d_w2v):
    depth = norm_g.shape[0]
    assert depth == 2 and c_lb.shape[0] == 2, "the HGRN2 lower-bound formula in the kernel is written for depth 2"
    s = x.shape[1]
    assert s % 256 == 0 and s >= D_WINDOW
    pos = jnp.arange(s)
    cos, sin_s = _rope_tables(pos)
    nseg = s // CMP_STRIDE
    cmp_end = jnp.minimum(jnp.arange(nseg) * CMP_STRIDE + CMP_LEN - 1, s - 1)
    cosc, sinc = _rope_tables(cmp_end)
    h = _even_layer(x, mem, norm_g[0], mem_norm_g[0], mem_w_kv[0], mem_qn[0], mem_kn[0], ev_w_in[0], ev_w_out[0],
                    a_qn[0], a_kn[0], a_sinks[0], b_conv_w[0], b_conv_b[0], b_w_r[0], b_b_r[0], b_w_i[0], b_b_i[0],
                    b_lambda[0], cos, sin_s)
    h = _odd_layer(h, mem, norm_g[1], mem_norm_g[1], mem_w_kv[1], mem_qn[1], mem_kn[1], od_w_in[0], od_w_out[0],
                   c_lb, c_onorm[0], d_qn[0], d_kn_cmp[0], d_kn_slc[0], d_kn_win[0], d_pe_k[0], d_pe_v[0],
                   d_w1k[0], d_w2k[0], d_w1v[0], d_w2v[0], cos, sin_s, cosc, sinc)
    return h
```

```python
import functools

import numpy as np
import jax
import jax.numpy as jnp
from jax import lax
from jax.experimental import pallas as pl
from jax.experimental.pallas import tpu as pltpu

F32 = jnp.float32
BF16 = jnp.bfloat16

D_MODEL = 1024
N_MEM = 256
HEAD_DIM = 64
ROPE_THETA = 10000.0
EPS = 1e-6
NEG_INF = -1e30
POS_INF = 1e30
MASK_BIG = 1e30
LANES = 128

A_HEADS, A_KV_HEADS, A_WINDOW = 8, 2, 128
B_WIDTH, B_BLOCKS, B_CONV, B_C = 512, 8, 4, 8.0
M_HEADS = 4
C_HEADS, C_HEAD_DIM, C_CHUNK = 4, 128, 64
C_WIDTH = C_HEADS * C_HEAD_DIM
D_HEADS, D_KV_HEADS = 8, 2
CMP_LEN, CMP_STRIDE, CMP_HIDDEN = 32, 16, 128
SEL_LEN, SEL_TOPK = 64, 4
D_WINDOW = 512
D_BRANCHES = 3
SCALE = HEAD_DIM ** -0.5
LOG2E = 1.4426950408889634
Q_SCALE = SCALE * LOG2E

EVEN_ORDER = [("xb", 512), ("gb", 512), ("qa", 512), ("ga", 512), ("qm", 256), ("gm", 256), ("ka", 128), ("va", 128)]
EVEN_FUSED = 1024
EVEN_SRC = {"qa": (0, 512), "ka": (512, 128), "va": (640, 128), "ga": (768, 512), "xb": (1280, 512),
            "gb": (1792, 512), "qm": (2304, 256), "gm": (2560, 256)}
ODD_ORDER = [("qc", 512), ("fc", 512), ("ic", 512), ("gc", 512), ("qd", 512), ("gd", 512), ("qm", 256), ("gm", 256),
             ("ksd", 128), ("vsd", 128), ("kwd", 128), ("vwd", 128), ("gate", 128), ("kcd", 128), ("vcd", 128)]
ODD_FUSED = 2048
ODD_SPLIT = (2176, 128, 128)
ODD_SRC = {"qc": (0, 512), "fc": (512, 512), "ic": (1024, 512), "gc": (1536, 512), "qd": (2048, 512),
           "kcd": (2560, 128), "vcd": (2688, 128), "ksd": (2816, 128), "vsd": (2944, 128), "kwd": (3072, 128),
           "vwd": (3200, 128), "gate": (3328, 24), "gd": (3352, 512), "qm": (3864, 256), "gm": (4120, 256)}

VMEM_LIMIT = 48 * 1024 * 1024


def _offsets(order):
    off, out = 0, {}
    for name, w in order:
        out[name] = (off, w)
        off += w
    return out, off


EVEN_OFF, EVEN_COLS = _offsets(EVEN_ORDER[2:])
ODD_OFF, ODD_COLS = _offsets(ODD_ORDER[4:])


def _cparams(sem):
    return pltpu.CompilerParams(dimension_semantics=sem, vmem_limit_bytes=VMEM_LIMIT)


def _dot(a, b):
    return jnp.dot(a, b, preferred_element_type=F32)


def _dot_nt(a, b):
    return lax.dot_general(a, b, (((1,), (1,)), ((), ())), preferred_element_type=F32)


def _split(x):
    hi = x.astype(BF16)
    lo = (x - hi.astype(F32)).astype(BF16)
    return hi, lo


def _dot_split_rhs(m, x):
    hi, lo = _split(x)
    return _dot(m, hi) + _dot(m, lo)


def _lane(shape):
    return lax.broadcasted_iota(jnp.int32, shape, len(shape) - 1)


def _row(shape):
    return lax.broadcasted_iota(jnp.int32, shape, len(shape) - 2)


def _silu(x):
    return x * jax.nn.sigmoid(x)


def _seg_ones():
    r = lax.broadcasted_iota(jnp.int32, (LANES, LANES), 0) >> 6
    c = lax.broadcasted_iota(jnp.int32, (LANES, LANES), 1) >> 6
    return jnp.where(r == c, 1.0, 0.0).astype(BF16)


def _head_rms(x, gain):
    seg = _seg_ones()
    cols = []
    for c in range(x.shape[1] // LANES):
        xc = x[:, c * LANES:(c + 1) * LANES]
        ms = _dot((xc * xc).astype(BF16), seg) * (1.0 / HEAD_DIM)
        cols.append(xc * lax.rsqrt(ms + EPS))
    y = cols[0] if len(cols) == 1 else jnp.concatenate(cols, axis=1)
    return y * gain


def _rope(x, cos, sin_s):
    first = (_lane((x.shape[0], LANES)) & 63) < 32
    cols = []
    for c in range(x.shape[1] // LANES):
        xc = x[:, c * LANES:(c + 1) * LANES]
        sw = jnp.where(first, pltpu.roll(xc, 96, 1), pltpu.roll(xc, 32, 1))
        cols.append(xc * cos + sw * sin_s)
    return cols[0] if len(cols) == 1 else jnp.concatenate(cols, axis=1)


def _split_kv(k, fill=0.0):
    lo = _lane(k.shape) < 64
    return jnp.where(lo, k, fill), jnp.where(lo, pltpu.roll(k, 64, 1), fill)


def _q_heads(q, fills=None):
    lo = _lane((q.shape[0], LANES)) < 64
    out = []
    for c in range(q.shape[1] // LANES):
        qc = q[:, c * LANES:(c + 1) * LANES]
        fill = 0.0 if fills is None else fills[c]
        out.append(jnp.where(lo, qc, fill).astype(BF16))
        out.append(jnp.where(lo, pltpu.roll(qc, 64, 1), fill).astype(BF16))
    return out


VT_ROWS = HEAD_DIM + 16


def _with_ones_row(vt):
    n = vt.shape[1]
    pad = jnp.where(_row((VT_ROWS - HEAD_DIM, n)) == 0, 1.0, 0.0)
    return jnp.concatenate([vt, pad], axis=0)


MAX_SAFE_BOUND = 60.0


def _score_bound(q_gain, k_gain):
    return 1.02 * HEAD_DIM * SCALE * LOG2E * jnp.max(jnp.abs(q_gain)) * jnp.max(jnp.abs(k_gain))


def _with_fixed_reference(bound, body):
    small = bound <= MAX_SAFE_BOUND

    @pl.when(small)
    def _():
        body(bound)

    @pl.when(jnp.logical_not(small))
    def _():
        body(None)


def _exp_cols(s, extra=None, fixed=None):
    if fixed is not None:
        return jnp.exp2(s - fixed).astype(BF16), fixed
    m = jnp.max(s, axis=0, keepdims=True)
    if extra is not None:
        m = jnp.maximum(m, extra)
    return jnp.exp2(s - m).astype(BF16), m


def _finish(acc, extra_den=None):
    den = acc[HEAD_DIM:HEAD_DIM + 1, :]
    if extra_den is not None:
        den = den + extra_den
    return acc[0:HEAD_DIM, :] * (1.0 / den)


def _online_update(state, s, vt_tile, fixed=None):
    m, acc = state
    if fixed is not None:
        return m, acc + _dot(vt_tile, _exp_cols(s, fixed=fixed)[0])
    p, m_new = _exp_cols(s, m)
    acc = jnp.exp2(m - m_new) * acc + _dot(vt_tile, p)
    return m_new, acc


def _tile4(x):
    return jnp.concatenate([x, x, x, x], axis=1)


def _outproj_mem_kernel(tiles_per_seq, bound_ref, h_ref, a_ref, b_ref, q_ref, gm_ref, mem_ref, mg_ref, wkv_ref,
                        qn_ref, kn_ref, w_ref, o_ref, k_scr, vt_scr, acc_scr):
    @pl.when(pl.program_id(0) % tiles_per_seq == 0)
    def _():
        m = mem_ref[0]
        ms = jnp.mean(m * m, axis=-1, keepdims=True)
        mn = (m * lax.rsqrt(ms + EPS) * mg_ref[...]).astype(BF16)
        kv = _dot(mn, wkv_ref[...])
        km = _head_rms(kv[:, 0:256], kn_ref[...])
        for c in range(2):
            k0, k1 = _split_kv(km[:, c * LANES:(c + 1) * LANES])
            k_scr[2 * c] = k0.astype(BF16)
            k_scr[2 * c + 1] = k1.astype(BF16)
        vt = kv[:, 256:512].T
        for h in range(M_HEADS):
            vt_scr[h] = _with_ones_row(vt[h * HEAD_DIM:(h + 1) * HEAD_DIM, :]).astype(BF16)

    def body(fixed):
        a, b = a_ref[...], b_ref[...]
        pending = list(range(0, D_MODEL, 256))

        def emit():
            if pending:
                c = pending.pop(0)
                acc_scr[:, c:c + 256] = (h_ref[:, c:c + 256] + _dot(a, w_ref[0:512, c:c + 256])
                                         + _dot(b, w_ref[512:1024, c:c + 256]))

        q = _head_rms(q_ref[...].astype(F32), qn_ref[...]) * Q_SCALE
        scores = [_dot_nt(k_scr[h], qh) for h, qh in enumerate(_q_heads(q))]
        rows = []
        for h, sc in enumerate(scores):
            emit()
            rows.append(_finish(_dot(vt_scr[h], _exp_cols(sc, fixed=fixed)[0])))
        om = (jnp.concatenate(rows, axis=0).T * _silu(gm_ref[...].astype(F32))).astype(BF16)
        while pending:
            emit()
        for c in range(0, D_MODEL, 256):
            o_ref[:, c:c + 256] = acc_scr[:, c:c + 256] + _dot(om, w_ref[1024:1280, c:c + 256])

    _with_fixed_reference(bound_ref[0], body)


def _outproj_mem(h2d, oa, ob, z2d, q_off, g_off, seq, mem, mem_g, wkv_bf16, qn, kn, w_bf16, tm=1024):
    n, d = h2d.shape
    full = lambda shp: pl.BlockSpec(shp, lambda i: (0,) * len(shp))
    tps = seq // tm
    return pl.pallas_call(
        functools.partial(_outproj_mem_kernel, tps),
        grid=(n // tm,),
        in_specs=[pl.BlockSpec(memory_space=pltpu.SMEM),
                  pl.BlockSpec((tm, d), lambda i: (i, 0)),
                  pl.BlockSpec((tm, 512), lambda i: (i, 0)),
                  pl.BlockSpec((tm, 512), lambda i: (i, 0)),
                  pl.BlockSpec((tm, 256), lambda i: (i, q_off // 256)),
                  pl.BlockSpec((tm, 256), lambda i: (i, g_off // 256)),
                  pl.BlockSpec((1, N_MEM, D_MODEL), lambda i: (i // tps, 0, 0)),
                  full((1, D_MODEL)), full((D_MODEL, 512)), full((1, 256)), full((1, 256)), full((1280, d))],
        out_specs=pl.BlockSpec((tm, d), lambda i: (i, 0)),
        out_shape=jax.ShapeDtypeStruct((n, d), F32),
        scratch_shapes=[pltpu.VMEM((M_HEADS, N_MEM, LANES), BF16), pltpu.VMEM((M_HEADS, VT_ROWS, N_MEM), BF16),
                        pltpu.VMEM((tm, d), F32)],
        compiler_params=_cparams(("arbitrary",)),
        name="outproj_mem",
    )(_score_bound(qn, kn).reshape(1).astype(F32), h2d, oa, ob, z2d, z2d, mem, mem_g.reshape(1, D_MODEL), wkv_bf16,
      jnp.tile(qn, 4).reshape(1, 256), jnp.tile(kn, 4).reshape(1, 256), w_bf16)


SWA_SUB = 256


def _swa_kernel(bound_ref, *refs):
    _with_fixed_reference(bound_ref[0], functools.partial(_swa_body, refs))


def _swa_body(refs, fixed):
    (q_ref, g_ref, kc_ref, kp_ref, vc_ref, vp_ref, cosc_ref, sinc_ref, cosp_ref, sinp_ref,
     qn_ref, kn_ref, sink_ref, o_ref) = refs
    i = pl.program_id(1)
    t = q_ref.shape[1]
    tp = kp_ref.shape[1]
    grp = A_HEADS // A_KV_HEADS
    q = _rope(_head_rms(q_ref[0].astype(F32), qn_ref[...]), cosc_ref[...], sinc_ref[...]) * Q_SCALE
    qh = _q_heads(q)
    kc = _rope(_head_rms(kc_ref[0].astype(F32), kn_ref[...]), cosc_ref[...], sinc_ref[...])
    kp = _rope(_head_rms(kp_ref[0].astype(F32), kn_ref[...]), cosp_ref[...], sinp_ref[...])
    ks = [x.astype(BF16) for x in _split_kv(jnp.concatenate([kp, kc], axis=0))]
    vt = jnp.concatenate([vp_ref[0].astype(F32).T, vc_ref[0].astype(F32).T], axis=1)
    ts_ = min(SWA_SUB, t)
    krow = _row((tp + ts_, ts_))
    qcol = _lane((tp + ts_, ts_))
    subs = range(t // ts_)
    scores, biases = [], []
    for j in subs:
        ts = i * t + j * ts_ - tp + krow
        rel = i * t + j * ts_ + qcol - ts
        biases.append(_tile4(jnp.where((rel >= 0) & (rel < A_WINDOW) & (ts >= 0), 0.0, NEG_INF)))
        scores.append([_dot_nt(ks[kv][j * ts_:j * ts_ + tp + ts_, :],
                               jnp.concatenate([qh[kv * grp + g][j * ts_:(j + 1) * ts_, :] for g in range(grp)], axis=0))
                       for kv in range(A_KV_HEADS)])
    blocks = []
    for j in subs:
        rows = []
        for kv in range(A_KV_HEADS):
            sink = jnp.concatenate([jnp.broadcast_to(sink_ref[kv * grp + g:kv * grp + g + 1, 0:1], (1, ts_))
                                    for g in range(grp)], axis=1) * LOG2E
            p, m = _exp_cols(scores[j][kv] + biases[j], sink, fixed)
            vt_kv = _with_ones_row(vt[kv * HEAD_DIM:(kv + 1) * HEAD_DIM, j * ts_:j * ts_ + tp + ts_]).astype(BF16)
            o_t = _finish(_dot(vt_kv, p), jnp.exp2(sink - m))
            rows.extend(o_t[:, g * ts_:(g + 1) * ts_] for g in range(grp))
        blocks.append(jnp.concatenate(rows, axis=0))
    o = (blocks[0] if len(blocks) == 1 else jnp.concatenate(blocks, axis=1)).T
    o_ref[0] = (o * _silu(g_ref[0].astype(F32))).astype(BF16)


def _swa(z, cos, sin_s, qn, kn, sinks, t=1024):
    b, s, _ = z.shape
    tp = A_WINDOW
    qo, go = EVEN_OFF["qa"][0] // 512, EVEN_OFF["ga"][0] // 512
    ko, vo = EVEN_OFF["ka"][0] // 128, EVEN_OFF["va"][0] // 128
    prev = lambda i: jnp.maximum(i * (t // tp) - 1, 0)
    bound = jnp.maximum(_score_bound(qn, kn), jnp.max(sinks) * LOG2E).reshape(1).astype(F32)
    return pl.pallas_call(
        _swa_kernel,
        grid=(b, s // t),
        in_specs=[pl.BlockSpec(memory_space=pltpu.SMEM),
                  pl.BlockSpec((1, t, 512), lambda bi, i: (bi, i, qo)),
                  pl.BlockSpec((1, t, 512), lambda bi, i: (bi, i, go)),
                  pl.BlockSpec((1, t, 128), lambda bi, i: (bi, i, ko)),
                  pl.BlockSpec((1, tp, 128), lambda bi, i: (bi, prev(i), ko)),
                  pl.BlockSpec((1, t, 128), lambda bi, i: (bi, i, vo)),
                  pl.BlockSpec((1, tp, 128), lambda bi, i: (bi, prev(i), vo)),
                  pl.BlockSpec((t, 128), lambda bi, i: (i, 0)),
                  pl.BlockSpec((t, 128), lambda bi, i: (i, 0)),
                  pl.BlockSpec((tp, 128), lambda bi, i: (prev(i), 0)),
                  pl.BlockSpec((tp, 128), lambda bi, i: (prev(i), 0)),
                  pl.BlockSpec((1, 512), lambda bi, i: (0, 0)),
                  pl.BlockSpec((1, 128), lambda bi, i: (0, 0)),
                  pl.BlockSpec((A_HEADS, 128), lambda bi, i: (0, 0))],
        out_specs=pl.BlockSpec((1, t, 512), lambda bi, i: (bi, i, 0)),
        out_shape=jax.ShapeDtypeStruct((b, s, 512), BF16),
        compiler_params=_cparams(("arbitrary", "arbitrary")),
        name="swa",
    )(bound, z, z, z, z, z, z, cos, sin_s, cos, sin_s, jnp.tile(qn, 8).reshape(1, 512), jnp.tile(kn, 2).reshape(1, 128),
      jnp.broadcast_to(sinks.reshape(A_HEADS, 1), (A_HEADS, 128)))


def _rglru_half(cols, xb, gb, emit, cw_ref, cb_ref, wr_ref, br_ref, wi_ref, bi_ref, lam_ref, xbuf, hcar, a_scr, u_scr, h_scr):
    t, wdt = xb.shape
    xbuf[8:t + 8, cols] = xb
    xc = cb_ref[:, cols] + cw_ref[0:1, cols] * xbuf[5:t + 5, cols]
    for j in range(1, B_CONV):
        xc = xc + cw_ref[j:j + 1, cols] * xbuf[5 + j:t + 5 + j, cols]
    xbuf[0:8, cols] = xbuf[t:t + 8, cols]
    emit()

    r_cols, i_cols = [], []
    for c in range(wdt // LANES):
        blk = cols.start // LANES + c
        xcc = xc[:, c * LANES:(c + 1) * LANES].astype(BF16)
        r_cols.append(_dot(xcc, wr_ref[blk]))
        i_cols.append(_dot(xcc, wi_ref[blk]))
    emit()
    r = jax.nn.sigmoid(jnp.concatenate(r_cols, axis=1) + br_ref[:, cols])
    ig = jax.nn.sigmoid(jnp.concatenate(i_cols, axis=1) + bi_ref[:, cols])
    nl = -lam_ref[:, cols]
    softplus = jnp.maximum(nl, 0.0) + jnp.log(1.0 + jnp.exp(-jnp.abs(nl)))
    log_a = -B_C * r * softplus
    a = jnp.exp(log_a)
    om = 1.0 - a * a
    u = om * lax.rsqrt(jnp.maximum(om, 1e-30)) * (ig * xc)
    emit()

    a = a.reshape(t // 8, 8, wdt)
    u = u.reshape(t // 8, 8, wdt)
    r8 = lax.broadcasted_iota(jnp.int32, a.shape, 1)
    for d in (1, 2, 4):
        a_sh = pltpu.roll(a, d, 1)
        u_sh = pltpu.roll(u, d, 1)
        m = r8 >= d
        u = jnp.where(m, a * u_sh + u, u)
        a = jnp.where(m, a * a_sh, a)
    a_scr[:, cols] = a.reshape(t, wdt)
    u_scr[:, cols] = u.reshape(t, wdt)
    emit()

    def body(j, h):
        off = pl.multiple_of(j * 8, 8)
        hh = a_scr[pl.ds(off, 8), cols] * h + u_scr[pl.ds(off, 8), cols]
        h_scr[pl.ds(off, 8), cols] = hh
        return hh[7:8, :]

    h_last = lax.fori_loop(0, t // 8, body, hcar[0:1, cols], unroll=True)
    hcar[0:1, cols] = h_last
    emit()
    return h_scr[:, cols] * _silu(gb)


def _inproj_rglru_kernel(tiles_per_seq, x_ref, g_ref, w_ref, cw_ref, cb_ref, wr_ref, br_ref, wi_ref, bi_ref, lam_ref,
                         z_ref, ob_ref, xbuf, hcar, a_scr, u_scr, h_scr):
    @pl.when(pl.program_id(0) % tiles_per_seq == 0)
    def _():
        xbuf[0:8, :] = jnp.zeros((8, B_WIDTH), F32)
        hcar[...] = jnp.zeros((8, B_WIDTH), F32)

    x = x_ref[...]
    ms = jnp.mean(x * x, axis=-1, keepdims=True)
    xn = (x * lax.rsqrt(ms + EPS) * g_ref[...]).astype(BF16)
    half = B_WIDTH // 2
    proj = {}

    def project(name, hp):
        base = (B_WIDTH if name == "gb" else 0) + hp * half
        proj[name, hp] = _dot(xn, w_ref[:, base:base + half])

    def store_z(c):
        z_ref[:, c:c + 256] = _dot(xn, w_ref[:, EVEN_FUSED + c:EVEN_FUSED + c + 256]).astype(BF16)

    pending = [functools.partial(store_z, c) for c in range(0, z_ref.shape[-1], 256)]
    pending.insert(1, functools.partial(project, "xb", 1))
    pending.insert(3, functools.partial(project, "gb", 1))

    def emit():
        if pending:
            pending.pop(0)()

    project("xb", 0)
    project("gb", 0)
    for hp in range(2):
        cols = slice(hp * half, (hp + 1) * half)
        ob_ref[:, cols] = _rglru_half(cols, proj["xb", hp], proj["gb", hp], emit, cw_ref, cb_ref, wr_ref, br_ref,
                                      wi_ref, bi_ref, lam_ref, xbuf, hcar, a_scr, u_scr, h_scr).astype(BF16)
    while pending:
        emit()


def _inproj_rglru(x2d, gain, w_bf16, seq, conv_w, conv_b, wr_bd, b_r, wi_bd, b_i, lam, tm=1024):
    n, d = x2d.shape
    nc = w_bf16.shape[1]
    row = lambda v: v.reshape(1, B_WIDTH)
    full = lambda shp: pl.BlockSpec(shp, lambda i: (0,) * len(shp))
    return pl.pallas_call(
        functools.partial(_inproj_rglru_kernel, seq // tm),
        grid=(n // tm,),
        in_specs=[pl.BlockSpec((tm, d), lambda i: (i, 0)), full((1, d)), full((d, nc)),
                  full((B_CONV, B_WIDTH)), full((1, B_WIDTH)), full((4, LANES, LANES)), full((1, B_WIDTH)),
                  full((4, LANES, LANES)), full((1, B_WIDTH)), full((1, B_WIDTH))],
        out_specs=[pl.BlockSpec((tm, nc - EVEN_FUSED), lambda i: (i, 0)), pl.BlockSpec((tm, B_WIDTH), lambda i: (i, 0))],
        out_shape=[jax.ShapeDtypeStruct((n, nc - EVEN_FUSED), BF16), jax.ShapeDtypeStruct((n, B_WIDTH), BF16)],
        scratch_shapes=[pltpu.VMEM((tm + 8, B_WIDTH), F32), pltpu.VMEM((8, B_WIDTH), F32),
                        pltpu.VMEM((tm, B_WIDTH), F32), pltpu.VMEM((tm, B_WIDTH), F32), pltpu.VMEM((tm, B_WIDTH), F32)],
        compiler_params=_cparams(("arbitrary",)),
        name="inproj_rglru",
    )(x2d, gain.reshape(1, d), w_bf16, conv_w, row(conv_b), wr_bd, row(b_r), wi_bd, row(b_i), row(lam))


def _hgrn_consts():
    c = C_CHUNK
    t = np.arange(c)[:, None]
    s = np.arange(c)[None, :]
    masks = []
    hs = c // 2
    while hs >= 1:
        same = (t // (2 * hs)) == (s // (2 * hs))
        masks.append(same & ((t // hs) % 2 == 1) & ((s // hs) % 2 == 0))
        hs //= 2
    pairs = [np.concatenate(masks[n:n + 2], axis=1) for n in range(0, len(masks), 2)]
    return (s <= t).astype(np.float32), np.stack(pairs).astype(np.float32)


def _split_points(b, level):
    c, wdt = b.shape
    hs = c >> (level + 1)
    blk = 2 * hs
    if blk >= 8:
        return jnp.concatenate([jnp.broadcast_to(b[m * blk + hs - 1:m * blk + hs, :], (blk, wdt))
                                for m in range(c // blk)], axis=0)
    b3 = b.reshape(c // 8, 8, wdt)
    sub = lax.broadcasted_iota(jnp.int32, b3.shape, 1)
    if hs == 2:
        out = jnp.where(sub < 4, jnp.broadcast_to(b3[:, 1:2, :], b3.shape), jnp.broadcast_to(b3[:, 5:6, :], b3.shape))
    else:
        out = jnp.where((sub & 1) == 0, b3, pltpu.roll(b3, 1, 1))
    return out.reshape(c, wdt)


def _hgrn_chunk(lb, q_in, f_in, v_in, g_in, og_ref, mst_ref, msk_ref, st_scr, emit, emit2):
    c = C_CHUNK
    npair = msk_ref.shape[0]
    zeros = jnp.zeros((c, LANES), BF16)
    f = lb + (1.0 - lb) * jax.nn.sigmoid(f_in)
    g = jnp.log2(f)
    g_hi, g_lo = _split(g)
    b_all = _dot(mst_ref[...], g_hi) + _dot(mst_ref[...], g_lo)
    emit()
    kk_all = 1.0 - f
    qf_all = _silu(q_in)
    v_all = v_in
    kk_bf, qf_bf = kk_all.astype(BF16), qf_all.astype(BF16)
    heads = range(C_HEADS)
    cols = [slice(h * LANES, (h + 1) * LANES) for h in heads]
    sts = [st_scr[h] for h in heads]
    os = [_dot_nt((qf_all[:, cols[h]] * jnp.exp2(b_all[:, cols[h]])).astype(BF16), sts[h].astype(BF16)) for h in heads]
    att = [[] for _ in heads]
    bms = [_split_points(b_all, l) for l in range(2 * npair)]
    for n in range(npair):
        lhs, rhs = [], []
        for h in heads:
            b = b_all[:, cols[h]]
            qt, kt = [], []
            for l in (2 * n, 2 * n + 1):
                bm = bms[l][:, cols[h]]
                qt.append(qf_bf[:, cols[h]] * jnp.exp2(jnp.minimum(b - bm, 1.0)).astype(BF16))
                kt.append(kk_bf[:, cols[h]] * jnp.exp2(jnp.minimum(bm - b, 1.0)).astype(BF16))
            lhs.append(jnp.concatenate(qt, axis=1))
            rhs.append(jnp.concatenate([jnp.concatenate([kt[0], zeros], axis=1),
                                        jnp.concatenate([zeros, kt[1]], axis=1)], axis=0))
        prods = [_dot_nt(lhs[h], rhs[h]) for h in heads]
        if n == 1:
            emit2()
        for h in heads:
            att[h].append(jnp.where(msk_ref[n] > 0.5, prods[h], 0.0).astype(BF16))
    vbs = [v_all[:, cols[h]].astype(BF16) for h in heads]
    intra = [_dot(jnp.concatenate(att[h], axis=1), jnp.concatenate([vbs[h]] * (2 * npair), axis=0)) for h in heads]
    b_last = b_all[c - 1:c, :]
    kd = (kk_all * jnp.exp2(b_last - b_all)).astype(BF16)
    upd = [_dot(v_all[:, cols[h]].T.astype(BF16), kd[:, cols[h]]) for h in heads]
    decay = jnp.exp2(b_last)
    diag = qf_all * kk_all
    outs = []
    for h in heads:
        st_scr[h] = sts[h] * decay[:, cols[h]] + upd[h]
        o = os[h] + intra[h] + jnp.sum(diag[:, cols[h]], axis=-1, keepdims=True) * v_all[:, cols[h]]
        ms = jnp.mean(o * o, axis=-1, keepdims=True)
        outs.append(o * lax.rsqrt(ms + EPS) * og_ref[...])
    return jnp.concatenate(outs, axis=1) * _silu(g_in)


def _inproj_hgrn_kernel(tiles_per_seq, x_ref, g_ref, w_ref, lb_ref, og_ref, mst_ref, msk_ref,
                        z_ref, kc_ref, vc_ref, oc_ref, st_scr):
    @pl.when(pl.program_id(0) % tiles_per_seq == 0)
    def _():
        st_scr[...] = jnp.zeros(st_scr.shape, F32)

    x = x_ref[...]
    ms = jnp.mean(x * x, axis=-1, keepdims=True)
    xn = (x * lax.rsqrt(ms + EPS) * g_ref[...]).astype(BF16)
    rows_half = x.shape[0] // 2
    own = {}

    def project_own(k, half):
        r = slice(half * rows_half, (half + 1) * rows_half)
        own[k, half] = _dot(xn[r, :], w_ref[:, k * C_WIDTH:(k + 1) * C_WIDTH])

    o_refs = (z_ref, kc_ref, vc_ref)
    starts = np.cumsum([0] + [o.shape[-1] for o in o_refs])

    def project_rest(c):
        w = min(256, int(starts[-1]) - c)
        r = _dot(xn, w_ref[:, ODD_FUSED + c:ODD_FUSED + c + w]).astype(BF16)
        for o_ref, lo, hi in zip(o_refs, starts[:-1], starts[1:]):
            a, b = max(c, int(lo)), min(c + w, int(hi))
            if a < b:
                o_ref[:, a - int(lo):b - int(lo)] = r[:, a - c:b - c]

    pending = [functools.partial(project_own, k, 1) for k in range(4)]
    pending += [functools.partial(project_rest, c) for c in range(0, int(starts[-1]), 256)]

    def emit():
        if pending:
            pending.pop(0)()

    for k in range(4):
        project_own(k, 0)

    p = lb_ref[...]
    pm = jnp.maximum(p[0:1, :], p[1:2, :])
    e0, e1 = jnp.exp(p[0:1, :] - pm), jnp.exp(p[1:2, :] - pm)
    lb = e1 / (e0 + e1)
    nchunk = x.shape[0] // C_CHUNK
    for ch in range(nchunk):
        half, sub = divmod(ch, nchunk // 2)
        rows = slice(ch * C_CHUNK, (ch + 1) * C_CHUNK)
        part = slice(sub * C_CHUNK, (sub + 1) * C_CHUNK)
        q_in, f_in, v_in, g_in = (own[k, half][part, :] for k in range(4))
        second = emit if ch < 6 else (lambda: None)
        oc_ref[rows, :] = _hgrn_chunk(lb, q_in, f_in, v_in, g_in, og_ref, mst_ref, msk_ref, st_scr, emit,
                                      second).astype(BF16)
    while pending:
        emit()


def _inproj_hgrn(x2d, gain, w_bf16, seq, c_lb, c_og, tm=512):
    n, d = x2d.shape
    nc = w_bf16.shape[1]
    assert ODD_FUSED + sum(ODD_SPLIT) == nc
    mst, msk = _hgrn_consts()
    full = lambda shp: pl.BlockSpec(shp, lambda i: (0,) * len(shp))
    widths = ODD_SPLIT + (C_WIDTH,)
    return pl.pallas_call(
        functools.partial(_inproj_hgrn_kernel, seq // tm),
        grid=(n // tm,),
        in_specs=[pl.BlockSpec((tm, d), lambda i: (i, 0)), full((1, d)), full((d, nc)),
                  full(c_lb.shape), full((1, C_HEAD_DIM)), full(mst.shape), full(msk.shape)],
        out_specs=[pl.BlockSpec((tm, w), lambda i: (i, 0)) for w in widths],
        out_shape=[jax.ShapeDtypeStruct((n, w), BF16) for w in widths],
        scratch_shapes=[pltpu.VMEM((C_HEADS, C_HEAD_DIM, C_HEAD_DIM), F32)],
        compiler_params=_cparams(("arbitrary",)),
        name="inproj_hgrn",
    )(x2d, gain.reshape(1, d), w_bf16, c_lb, c_og.reshape(1, C_HEAD_DIM), jnp.asarray(mst, BF16), jnp.asarray(msk, F32))


def _nsa_prep_kernel(kcr_ref, vcr_ref, ks_ref, vs_ref, kw_ref, vw_ref, cos_ref, sin_ref, cosc_ref, sinc_ref,
                     kncmp_ref, knslc_ref, knwin_ref, pek_ref, pev_ref, w1k_ref, w2k_ref, w1v_ref, w2vt_ref,
                     kc_o, vct_o, ks_o, vst_o, kw_o, vwt_o):
    def hidden(xr_ref, pe_ref, w1_ref):
        xr = xr_ref[0].astype(F32)
        top = _dot((xr + pe_ref[0:1, :]).astype(BF16), w1_ref[0])
        bot = _dot((xr + pe_ref[1:2, :]).astype(BF16), w1_ref[1])
        pre = top + pltpu.roll(bot, bot.shape[0] - 1, 0)
        return _silu(pre).astype(BF16)

    kc = _dot(hidden(kcr_ref, pek_ref, w1k_ref), w2k_ref[...])
    kc = _rope(_head_rms(kc, kncmp_ref[...]), cosc_ref[...], sinc_ref[...])
    d0, d1 = _split_kv(kc)
    kc_o[0, 0] = d0.astype(BF16)
    kc_o[0, 1] = d1.astype(BF16)
    vct = _dot_nt(w2vt_ref[...], hidden(vcr_ref, pev_ref, w1v_ref))
    for kv in range(D_KV_HEADS):
        vct_o[0, kv] = _with_ones_row(vct[kv * HEAD_DIM:(kv + 1) * HEAD_DIM, :]).astype(BF16)

    s = ks_ref.shape[1]
    step = 256
    for r0 in range(0, s, step):
        rows = slice(r0, r0 + step)
        cos, sin = cos_ref[rows, :], sin_ref[rows, :]
        ks = _rope(_head_rms(ks_ref[0, rows, :].astype(F32), knslc_ref[...]), cos, sin)
        kw = _rope(_head_rms(kw_ref[0, rows, :].astype(F32), knwin_ref[...]), cos, sin)
        blk = (r0 + _row((step, LANES))) // SEL_LEN
        onehot = jnp.where(_lane((step, LANES)) - HEAD_DIM == blk, MASK_BIG, 0.0)
        for src, dst, fill in ((ks, ks_o, onehot), (kw, kw_o, 0.0)):
            d0, d1 = _split_kv(src, fill)
            dst[0, 0, rows, :] = d0.astype(BF16)
            dst[0, 1, rows, :] = d1.astype(BF16)
        for src, dst in ((vs_ref, vst_o), (vw_ref, vwt_o)):
            vt = src[0, rows, :].astype(F32).T
            for kv in range(D_KV_HEADS):
                dst[0, kv, :, rows] = _with_ones_row(vt[kv * HEAD_DIM:(kv + 1) * HEAD_DIM, :]).astype(BF16)


def _nsa_prep(z, kcr, vcr, cos, sin_s, cosc, sinc, kn_cmp, kn_slc, kn_win, pek, pev, w1k, w2k, w1v, w2v):
    b, s, _ = z.shape
    nseg = s // CMP_STRIDE
    zb = lambda name: pl.BlockSpec((1, s, 128), lambda bi, o=ODD_OFF[name][0] // 128: (bi, 0, o))
    full = lambda shp: pl.BlockSpec(shp, lambda bi: (0,) * len(shp))
    seg = pl.BlockSpec((1, nseg, 2048), lambda bi: (bi, 0, 0))
    g2 = lambda g: jnp.tile(g, 2).reshape(1, 128)
    dup = lambda n: (jax.ShapeDtypeStruct((b, 2, n, 128), BF16), pl.BlockSpec((1, 2, n, 128), lambda bi: (bi, 0, 0, 0)))
    tr = lambda n: (jax.ShapeDtypeStruct((b, 2, VT_ROWS, n), BF16),
                    pl.BlockSpec((1, 2, VT_ROWS, n), lambda bi: (bi, 0, 0, 0)))
    outs = [dup(nseg), tr(nseg), dup(s), tr(s), dup(s), tr(s)]
    return pl.pallas_call(
        _nsa_prep_kernel,
        grid=(b,),
        in_specs=[seg, seg, zb("ksd"), zb("vsd"), zb("kwd"), zb("vwd"),
                  full((s, 128)), full((s, 128)), full((nseg, 128)), full((nseg, 128)),
                  full((1, 128)), full((1, 128)), full((1, 128)), full((2, 2048)), full((2, 2048)),
                  full((2, 2048, 256)), full((256, 128)), full((2, 2048, 256)), full((128, 256))],
        out_specs=[o[1] for o in outs],
        out_shape=[o[0] for o in outs],
        compiler_params=_cparams(("arbitrary",)),
        name="nsa_prep",
    )(kcr, vcr, z, z, z, z, cos, sin_s, cosc, sinc, g2(kn_cmp), g2(kn_slc), g2(kn_win), pek, pev, w1k, w2k, w1v, w2v)


NSA_SUB = 256
SEL_CHUNK = 512
SEL_PIECE = 512


def _nsa_kernel(bound_ref, *refs):
    small = jnp.maximum(jnp.maximum(bound_ref[0], bound_ref[1]), bound_ref[2]) <= MAX_SAFE_BOUND

    @pl.when(small)
    def _():
        _nsa_body((bound_ref[0], bound_ref[1], bound_ref[2]), *refs)

    @pl.when(jnp.logical_not(small))
    def _():
        _nsa_body((None, None, None), *refs)


def _nsa_body(fixed, q_ref, gd_ref, gate_ref, cos_ref, sin_ref, qn_ref, kc_ref, vct_ref, ks_ref, vst_ref, kw_ref, vwt_ref,
              ovlt_ref, o_ref):
    t = min(NSA_SUB, q_ref.shape[1])
    nsub = q_ref.shape[1] // t
    for j in range(nsub):
        rows = slice(j * t, (j + 1) * t)
        _nsa_tile(fixed, pl.program_id(1) * nsub + j, t, q_ref.at[0, rows, :], gd_ref.at[0, rows, :],
                  gate_ref.at[0, rows, :], cos_ref.at[rows, :], sin_ref.at[rows, :], qn_ref, kc_ref, vct_ref, ks_ref,
                  vst_ref, kw_ref, vwt_ref, ovlt_ref, o_ref.at[0, rows, :])


def _nsa_tile(fixed, i, t, q_ref, gd_ref, gate_ref, cos_ref, sin_ref, qn_ref, kc_ref, vct_ref, ks_ref, vst_ref, kw_ref,
              vwt_ref, ovlt_ref, o_ref):
    fix_c, fix_s, fix_w = fixed
    grp = D_HEADS // D_KV_HEADS
    w = grp * t
    kvs = range(D_KV_HEADS)
    q = _rope(_head_rms(q_ref[...].astype(F32), qn_ref[...]), cos_ref[...], sin_ref[...]) * Q_SCALE
    stack = lambda heads: [jnp.concatenate(heads[kv * grp:(kv + 1) * grp], axis=0) for kv in kvs]
    qs = stack(_q_heads(q))
    tq1 = i * t + _lane((1, t))

    nwin = D_WINDOW + t
    win0 = pl.multiple_of(jnp.maximum(i * t + t - nwin, 0), t)
    s_cmp = [_dot_nt(kc_ref[0, kv], qs[kv]) for kv in kvs]
    s_win = [_dot_nt(kw_ref[0, kv, pl.ds(win0, nwin), :], qs[kv]) for kv in kvs]

    ncmp = kc_ref.shape[2] - 1
    nsel = ovlt_ref.shape[0]
    crow = _row((kc_ref.shape[2], t))
    bias_c = _tile4(jnp.where((crow * CMP_STRIDE + (CMP_LEN - 1) <= tq1) & (crow < ncmp), 0.0, NEG_INF))
    row_ok = _tile4(tq1 >= CMP_LEN - 1)
    es = [_exp_cols(s_cmp[kv] + bias_c, fixed=fix_c)[0] for kv in kvs]
    accs = [_dot(vct_ref[0, kv], es[kv]) for kv in kvs]
    invs = [jnp.where(row_ok, 1.0 / accs[kv][HEAD_DIM:HEAD_DIM + 1, :], 0.0) for kv in kvs]
    oc_t = [accs[kv][0:HEAD_DIM, :] * invs[kv] for kv in kvs]
    psums = []
    for kv in kvs:
        p = es[kv].astype(F32) * invs[kv]
        psums.append(p[:, 0:t] + p[:, t:2 * t] + p[:, 2 * t:3 * t] + p[:, 3 * t:4 * t])
    imps = [_dot_split_rhs(ovlt_ref[...], psums[kv]) for kv in kvs]

    rel = tq1 - (win0 + _row((nwin, t)))
    bias_w = _tile4(jnp.where((rel >= 0) & (rel < D_WINDOW), 0.0, NEG_INF))
    ow_t = [_finish(_dot(vwt_ref[0, kv, :, pl.ds(win0, nwin)], _exp_cols(s_win[kv] + bias_w, fixed=fix_w)[0]))
            for kv in kvs]

    jrow = _row((nsel, t))
    jrow_f = jrow.astype(F32)
    cur = (i * t + _lane((nsel, t))) // SEL_LEN
    forced = (jrow == 0) | (jrow == cur)
    fills = []
    for kv in kvs:
        score = jnp.where(forced, POS_INF, jnp.where(jrow <= cur, imps[kv], NEG_INF))
        chosen = jnp.zeros((nsel, t), F32)
        for _ in range(min(SEL_TOPK, nsel)):
            mx = jnp.max(score, axis=0, keepdims=True)
            first = jnp.min(jnp.where(score == mx, jrow_f, 1e9), axis=0, keepdims=True)
            hit = jrow_f == first
            chosen = jnp.where(hit, 1.0, chosen)
            score = jnp.where(hit, -3e38, score)
        frame = jnp.concatenate([jnp.zeros((HEAD_DIM, t), F32), chosen - 1.0,
                                 jnp.zeros((LANES - HEAD_DIM - nsel, t), F32)], axis=0)
        fills.extend([frame.T] * (grp // 2))

    init = (jnp.full((1, w), NEG_INF, F32), jnp.zeros((VT_ROWS, w), F32))
    pc = SEL_PIECE

    def sweep(states, k_ref, vt_ref, queries, offs, biases, n):
        scores = [[_dot_nt(k_ref[0, kv, pl.ds(off, n), :], queries[kv]) for kv in kvs] for off in offs]
        for off, sc, bias in zip(offs, scores, biases):
            states = tuple(_online_update(states[kv], sc[kv] if bias is None else sc[kv] + bias,
                                          vt_ref[0, kv, :, pl.ds(off, n)], fix_s) for kv in kvs)
        return states

    qsel = stack(_q_heads(q, fills))
    ch = SEL_CHUNK
    own = pl.multiple_of(((i * t) // ch) * ch, ch)
    st = sweep((init, init), ks_ref, vst_ref, qsel, [own],
               [_tile4(jnp.where(own + _row((ch, t)) <= tq1, 0.0, NEG_INF))], ch)

    def sel_chunk(c, states):
        base = pl.multiple_of(c * ch, ch)
        return sweep(states, ks_ref, vst_ref, qsel, [base + n * pc for n in range(ch // pc)], [None] * (ch // pc), pc)

    st = lax.fori_loop(0, (i * t) // ch, sel_chunk, st)
    os_t = [_finish(acc) for (_, acc) in st]

    g_t = jax.nn.sigmoid(gate_ref[...].astype(F32)).T
    rows = []
    for h in range(D_HEADS):
        kv, g = divmod(h, grp)
        cols = slice(g * t, (g + 1) * t)
        r = D_BRANCHES * h
        rows.append(g_t[r:r + 1, :] * oc_t[kv][:, cols] + g_t[r + 1:r + 2, :] * os_t[kv][:, cols]
                    + g_t[r + 2:r + 3, :] * ow_t[kv][:, cols])
    o = jnp.concatenate(rows, axis=0).T
    o_ref[...] = (o * _silu(gd_ref[...].astype(F32))).astype(BF16)


def _overlap_t(s):
    ncmp = (s - CMP_LEN) // CMP_STRIDE + 1
    nsel = s // SEL_LEN
    cs = np.arange(ncmp)[None, :] * CMP_STRIDE
    ss = np.arange(nsel)[:, None] * SEL_LEN
    ovl = np.zeros((nsel, s // CMP_STRIDE), np.float32)
    ovl[:, :ncmp] = (cs < ss + SEL_LEN) & (cs + CMP_LEN > ss)
    return ovl


def _nsa(z, cos, sin_s, qn, bounds, kc, vct, ks, vst, kw, vwt, t=512):
    b, s, _ = z.shape
    nseg = s // CMP_STRIDE
    assert nseg == LANES, "compressed-block scores are laid out on one 128-row tile"
    ovlt = _overlap_t(s)
    full = lambda shp: pl.BlockSpec(shp, lambda bi, i: (0,) * len(shp))
    dup = lambda n: pl.BlockSpec((1, 2, n, 128), lambda bi, i: (bi, 0, 0, 0))
    tr = lambda n: pl.BlockSpec((1, 2, VT_ROWS, n), lambda bi, i: (bi, 0, 0, 0))
    return pl.pallas_call(
        _nsa_kernel,
        grid=(b, s // t),
        in_specs=[pl.BlockSpec(memory_space=pltpu.SMEM),
                  pl.BlockSpec((1, t, 512), lambda bi, i: (bi, i, ODD_OFF["qd"][0] // 512)),
                  pl.BlockSpec((1, t, 512), lambda bi, i: (bi, i, ODD_OFF["gd"][0] // 512)),
                  pl.BlockSpec((1, t, 128), lambda bi, i: (bi, i, ODD_OFF["gate"][0] // 128)),
                  pl.BlockSpec((t, 128), lambda bi, i: (i, 0)),
                  pl.BlockSpec((t, 128), lambda bi, i: (i, 0)),
                  full((1, 512)), dup(nseg), tr(nseg), dup(s), tr(s), dup(s), tr(s), full(ovlt.shape)],
        out_specs=pl.BlockSpec((1, t, 512), lambda bi, i: (bi, i, 0)),
        out_shape=jax.ShapeDtypeStruct((b, s, 512), BF16),
        compiler_params=_cparams(("arbitrary", "arbitrary")),
        name="nsa",
    )(bounds, z, z, z, cos, sin_s, jnp.tile(qn, 8).reshape(1, 512), kc, vct, ks, vst, kw, vwt, jnp.asarray(ovlt, BF16))


def _permute_cols(w, order, src):
    cols = []
    for name, width in order:
        o, sw = src[name]
        blk = w[:, o:o + sw]
        if sw < width:
            blk = jnp.pad(blk, ((0, 0), (0, width - sw)))
        cols.append(blk)
    return jnp.concatenate(cols, axis=1).astype(BF16)


def _block_diag_pairs(w):
    z = jnp.zeros((4, LANES, LANES), w.dtype)
    z = z.at[:, 0:64, 0:64].set(w[0::2])
    z = z.at[:, 64:128, 64:128].set(w[1::2])
    return z.astype(BF16)


def _rope_tables(pos):
    half = HEAD_DIM // 2
    inv = ROPE_THETA ** (-jnp.arange(half, dtype=F32) / half)
    ang = pos.astype(F32)[:, None] * inv[None, :]
    cos, sin = jnp.cos(ang), jnp.sin(ang)
    cos_t = jnp.tile(cos, (1, 4))
    sin_t = jnp.tile(jnp.concatenate([-sin, sin], axis=1), (1, 2))
    return cos_t, sin_t


def _expand_compress_w1(w1):
    hdim = w1.shape[1]
    w = w1.reshape(2, CMP_STRIDE, 1, HEAD_DIM, 1, hdim)
    same_head = jnp.eye(2, dtype=w1.dtype).reshape(1, 1, 2, 1, 2, 1)
    return (w * same_head).reshape(2, CMP_STRIDE * 2 * HEAD_DIM, 2 * hdim).astype(BF16)


def _expand_compress_w2(w2):
    hdim, hd = w2.shape
    z = jnp.zeros((2 * hdim, 2 * hd), w2.dtype)
    z = z.at[0:hdim, 0:hd].set(w2)
    z = z.at[hdim:, hd:].set(w2)
    return z.astype(BF16)


def _expand_pe(pe):
    p = pe.reshape(2, CMP_STRIDE, 1, HEAD_DIM)
    return jnp.broadcast_to(p, (2, CMP_STRIDE, 2, HEAD_DIM)).reshape(2, CMP_STRIDE * 2 * HEAD_DIM)


def _even_layer(h, mem, g, mem_g, w_mem_kv, m_qn, m_kn, w_in, w_out, a_qn, a_kn, a_sinks,
                conv_w, conv_b, w_r, b_r, w_i, b_i, lam, cos, sin_s):
    b, s, d = h.shape
    h2 = h.reshape(b * s, d)
    z, ob = _inproj_rglru(h2, g, _permute_cols(w_in, EVEN_ORDER, EVEN_SRC), s, conv_w, conv_b,
                          _block_diag_pairs(w_r), b_r, _block_diag_pairs(w_i), b_i, lam)
    z = z.reshape(b, s, EVEN_COLS)
    oa = _swa(z, cos, sin_s, a_qn, a_kn, a_sinks)
    out = _outproj_mem(h2, oa.reshape(b * s, 512), ob, z.reshape(b * s, EVEN_COLS), EVEN_OFF["qm"][0], EVEN_OFF["gm"][0],
                       s, mem, mem_g, w_mem_kv.astype(BF16), m_qn, m_kn, w_out.astype(BF16))
    return out.reshape(b, s, d)


def _odd_layer(h, mem, g, mem_g, w_mem_kv, m_qn, m_kn, w_in, w_out, c_lb, c_og,
               d_qn, d_kn_cmp, d_kn_slc, d_kn_win, pe_k, pe_v, w1k, w2k, w1v, w2v, cos, sin_s, cosc, sinc):
    b, s, d = h.shape
    h2 = h.reshape(b * s, d)
    z, kcd, vcd, oc = _inproj_hgrn(h2, g, _permute_cols(w_in, ODD_ORDER, ODD_SRC), s, c_lb, c_og)
    z = z.reshape(b, s, ODD_SPLIT[0])
    nseg = s // CMP_STRIDE
    seg = lambda a: a.reshape(b, nseg, CMP_STRIDE * 128)
    kc, vc, ks, vs, kw, vw = _nsa_prep(
        z, seg(kcd), seg(vcd), cos, sin_s, cosc, sinc, d_kn_cmp, d_kn_slc, d_kn_win,
        _expand_pe(pe_k), _expand_pe(pe_v), _expand_compress_w1(w1k), _expand_compress_w2(w2k),
        _expand_compress_w1(w1v), _expand_compress_w2(w2v).T)
    bounds = jnp.stack([_score_bound(d_qn, kn) for kn in (d_kn_cmp, d_kn_slc, d_kn_win)]).astype(F32)
    od = _nsa(z, cos, sin_s, d_qn, bounds, kc, vc, ks, vs, kw, vw)
    out = _outproj_mem(h2, oc, od.reshape(b * s, 512), z.reshape(b * s, ODD_SPLIT[0]), ODD_OFF["qm"][0], ODD_OFF["gm"][0],
                       s, mem, mem_g, w_mem_kv.astype(BF16), m_qn, m_kn, w_out.astype(BF16))
    return out.reshape(b, s, d)


def kernel(x, mem, norm_g, mem_norm_g, mem_w_kv, mem_qn, mem_kn, ev_w_in, ev_w_out, a_qn, a_kn, a_sinks,
           b_conv_w, b_conv_b, b_w_r, b_b_r, b_w_i, b_b_i, b_lambda, od_w_in, od_w_out, c_lb, c_onorm,
           d_qn, d_kn_cmp, d_kn_slc, d_kn_win, d_pe_k, d_pe_v, d_w1k, d_w2k, d_w1v, d_w2v):
    depth = norm_g.shape[0]
    assert depth == 2 and c_lb.shape[0] == 2, "the HGRN2 lower-bound formula in the kernel is written for depth 2"
    s = x.shape[1]
    assert s % 256 == 0 and s >= D_WINDOW
    pos = jnp.arange(s)
    cos, sin_s = _rope_tables(pos)
    nseg = s // CMP_STRIDE
    cmp_end = jnp.minimum(jnp.arange(nseg) * CMP_STRIDE + CMP_LEN - 1, s - 1)
    cosc, sinc = _rope_tables(cmp_end)
    h = _even_layer(x, mem, norm_g[0], mem_norm_g[0], mem_w_kv[0], mem_qn[0], mem_kn[0], ev_w_in[0], ev_w_out[0],
                    a_qn[0], a_kn[0], a_sinks[0], b_conv_w[0], b_conv_b[0], b_w_r[0], b_b_r[0], b_w_i[0], b_b_i[0],
                    b_lambda[0], cos, sin_s)
    h = _odd_layer(h, mem, norm_g[1], mem_norm_g[1], mem_w_kv[1], mem_qn[1], mem_kn[1], od_w_in[0], od_w_out[0],
                   c_lb, c_onorm[0], d_qn[0], d_kn_cmp[0], d_kn_slc[0], d_kn_win[0], d_pe_k[0], d_pe_v[0],
                   d_w1k[0], d_w2k[0], d_w1v[0], d_w2v[0], cos, sin_s, cosc, sinc)
    return h
```
